```python
import jax, jax.numpy as jnp
from jax import lax
import numpy as np

D_MODEL = 2048
BATCH = 8
SEQ = 8192
DEPTH = 1

HEAD_DIM = 64
N_HEADS_A = D_MODEL // (2 * HEAD_DIM)
N_KV_A = N_HEADS_A // 8
N_HEADS_B = D_MODEL // (2 * HEAD_DIM)
WINDOW_A = 128
DILATED_BRANCHES = ((128, 1), (512, 4), (2048, 16))
D_FF = 4 * D_MODEL
BLOCK = 128
EPS = 1e-5
NEG_INF = -1e30

Q_A = N_HEADS_A * HEAD_DIM
KV_A = N_KV_A * HEAD_DIM
Q_B = N_HEADS_B * HEAD_DIM
D_IN = Q_A + 2 * KV_A + 3 * Q_B
D_MIX = Q_A + Q_B

kernel_name = "hybrid_swa_sink_dilated_alibi_block"


def alibi_slopes(n):
    return jnp.asarray(2.0 ** (-8.0 * (np.arange(n) + 1) / n), dtype=jnp.float32)


def rmsnorm(x, g):
    x32 = x.astype(jnp.float32)
    y = x32 * lax.rsqrt(jnp.mean(x32 * x32, axis=-1, keepdims=True) + EPS)
    return y.astype(x.dtype) * g


def _with_prev_block(t, nb):
    b, L, G, Dh = t.shape
    tb = t.reshape(b, nb, BLOCK, G, Dh)
    prev = jnp.concatenate([jnp.zeros_like(tb[:, :1]), tb[:, :-1]], axis=1)
    return jnp.concatenate([prev, tb], axis=2)


def banded_attention(q, k, v, max_steps, step_dist, slopes, sinks):
    b, L, H, Dh = q.shape
    G = k.shape[2]
    R = H // G
    nb = L // BLOCK
    qb = q.reshape(b, nb, BLOCK, G, R, Dh)
    kb = _with_prev_block(k, nb)
    vb = _with_prev_block(v, nb)
    s = jnp.einsum('bnqgrd,bnkgd->bngrqk', qb, kb).astype(jnp.float32) * (Dh ** -0.5)
    qi = jnp.arange(BLOCK)[:, None]
    kj = jnp.arange(2 * BLOCK)[None, :]
    steps = qi + BLOCK - kj
    kpos = jnp.arange(nb)[:, None, None] * BLOCK + kj[None] - BLOCK
    valid = (steps >= 0) & (steps <= max_steps) & (kpos >= 0)
    alibi = slopes.reshape(G, R, 1, 1) * (step_dist * steps).astype(jnp.float32)
    s = jnp.where(valid[None, :, None, None], s - alibi[None, None], NEG_INF)
    m = jnp.max(s, axis=-1)
    if sinks is not None:
        sink = sinks.astype(jnp.float32).reshape(G, R, 1)
        m = jnp.maximum(m, sink)
    p = jnp.exp(s - m[..., None])
    denom = jnp.sum(p, axis=-1)
    if sinks is not None:
        denom = denom + jnp.exp(sink - m)
    o = jnp.einsum('bngrqk,bnkgd->bnqgrd', p, vb.astype(jnp.float32))
    o = o / jnp.moveaxis(denom, -1, 2)[..., None]
    lse = jnp.moveaxis(m + jnp.log(denom), -1, 2)
    return o.reshape(b, L, H, Dh), lse.reshape(b, L, H)


def _strided(t, dil, Lp):
    b, S, H, Dh = t.shape
    L = S // dil
    t = t.reshape(b, L, dil, H, Dh).transpose(0, 2, 1, 3, 4).reshape(b * dil, L, H, Dh)
    return jnp.pad(t, ((0, 0), (0, Lp - L), (0, 0), (0, 0)))


def dilated_mixture(q, k, v, slopes):
    b, S, H, Dh = q.shape
    outs, lses = [], []
    for window, dil in DILATED_BRANCHES:
        L = S // dil
        Lp = -(-L // BLOCK) * BLOCK
        o, lse = banded_attention(_strided(q, dil, Lp), _strided(k, dil, Lp),
                                  _strided(v, dil, Lp), window // dil, dil, slopes, None)
        outs.append(o[:, :L].reshape(b, dil, L, H, Dh).transpose(0, 2, 1, 3, 4).reshape(b, S, H, Dh))
        lses.append(lse[:, :L].reshape(b, dil, L, H).transpose(0, 2, 1, 3).reshape(b, S, H))
    w = jax.nn.softmax(jnp.stack(lses), axis=0)
    return jnp.einsum('nbsh,nbshd->bshd', w, jnp.stack(outs)).astype(q.dtype)


def _fwd_setup_inputs(seed: int = 0) -> dict:
    key = jax.random.key(seed)
    ks = jax.random.split(key, 12)
    f32 = jnp.float32
    x = jax.random.normal(ks[0], (BATCH, SEQ, D_MODEL), f32)
    g_attn = 1.0 + 0.02 * jax.random.normal(ks[1], (DEPTH, D_MODEL), f32)
    w_in = jax.random.normal(ks[2], (DEPTH, D_MODEL, D_IN), f32) * D_MODEL ** -0.5
    b_in = 0.02 * jax.random.normal(ks[3], (DEPTH, D_IN), f32)
    sinks_a = jax.random.normal(ks[4], (DEPTH, N_HEADS_A), f32)
    g_out_a = 1.0 + 0.02 * jax.random.normal(ks[5], (DEPTH, Q_A), f32)
    g_out_b = 1.0 + 0.02 * jax.random.normal(ks[6], (DEPTH, Q_B), f32)
    w_out = jax.random.normal(ks[7], (DEPTH, D_MIX, D_MODEL), f32) * D_MIX ** -0.5
    g_mlp = 1.0 + 0.02 * jax.random.normal(ks[8], (DEPTH, D_MODEL), f32)
    w_1 = jax.random.normal(ks[9], (DEPTH, D_MODEL, D_FF), f32) * D_MODEL ** -0.5
    w_2 = jax.random.normal(ks[10], (DEPTH, D_FF, D_MODEL), f32) * D_FF ** -0.5
    g_final = 1.0 + 0.02 * jax.random.normal(ks[11], (D_MODEL,), f32)
    return {"x": x, "g_attn": g_attn, "w_in": w_in, "b_in": b_in, "sinks_a": sinks_a,
            "g_out_a": g_out_a, "g_out_b": g_out_b, "w_out": w_out, "g_mlp": g_mlp,
            "w_1": w_1, "w_2": w_2, "g_final": g_final}


def _fwd_reference(x, g_attn, w_in, b_in, sinks_a, g_out_a, g_out_b, w_out, g_mlp, w_1, w_2, g_final):
    b, S, _ = x.shape
    slopes_a = alibi_slopes(N_HEADS_A)
    slopes_b = alibi_slopes(N_HEADS_B)
    for l in range(DEPTH):
        h = rmsnorm(x, g_attn[l])
        proj = jnp.einsum('bsd,de->bse', h, w_in[l]) + b_in[l]
        o1 = Q_A
        o2 = o1 + KV_A
        o3 = o2 + KV_A
        o4 = o3 + Q_B
        o5 = o4 + Q_B
        qa = proj[..., :o1].reshape(b, S, N_HEADS_A, HEAD_DIM)
        ka = proj[..., o1:o2].reshape(b, S, N_KV_A, HEAD_DIM)
        va = proj[..., o2:o3].reshape(b, S, N_KV_A, HEAD_DIM)
        qb = proj[..., o3:o4].reshape(b, S, N_HEADS_B, HEAD_DIM)
        kb = proj[..., o4:o5].reshape(b, S, N_HEADS_B, HEAD_DIM)
        vb = proj[..., o5:].reshape(b, S, N_HEADS_B, HEAD_DIM)
        oa, _ = banded_attention(qa, ka, va, WINDOW_A - 1, 1, slopes_a, sinks_a[l])
        ya = rmsnorm(oa.astype(x.dtype).reshape(b, S, Q_A), g_out_a[l])
        ob = dilated_mixture(qb, kb, vb, slopes_b)
        yb = rmsnorm(ob.reshape(b, S, Q_B), g_out_b[l])
        mix = jnp.concatenate([ya, yb], axis=-1)
        x = x + jnp.einsum('bse,ed->bsd', mix, w_out[l])
        h = rmsnorm(x, g_mlp[l])
        u = jax.nn.relu(jnp.einsum('bsd,df->bsf', h, w_1[l]))
        x = x + jnp.einsum('bsf,fd->bsd', u * u, w_2[l])
    return rmsnorm(x, g_final)


import jax as _jax
import jax.numpy as _jnp

TWIN_FORMAT = 'train_step'
FWD_PARAMS = ['x', 'g_attn', 'w_in', 'b_in', 'sinks_a', 'g_out_a', 'g_out_b', 'w_out', 'g_mlp', 'w_1', 'w_2', 'g_final']
TWIN_WEIGHTS = ['g_attn', 'w_in', 'b_in', 'sinks_a', 'g_out_a', 'g_out_b', 'w_out', 'g_mlp', 'w_1', 'w_2', 'g_final']
TWIN_DIFF_INPUT = 'x'
TWIN_INPUTS = ['x', 'g_attn', 'w_in', 'b_in', 'sinks_a', 'g_out_a', 'g_out_b', 'w_out', 'g_mlp', 'w_1', 'w_2', 'g_final', 'loss_target', 'm_g_attn', 'm_w_in', 'm_b_in', 'm_sinks_a', 'm_g_out_a', 'm_g_out_b', 'm_w_out', 'm_g_mlp', 'm_w_1', 'm_w_2', 'm_g_final', 'v_g_attn', 'v_w_in', 'v_b_in', 'v_sinks_a', 'v_g_out_a', 'v_g_out_b', 'v_w_out', 'v_g_mlp', 'v_w_1', 'v_w_2', 'v_g_final']
TWIN_OUTPUTS = ['loss', 'grad_x', 'grad_g_attn', 'grad_w_in', 'grad_b_in', 'grad_sinks_a', 'grad_g_out_a', 'grad_g_out_b', 'grad_w_out', 'grad_g_mlp', 'grad_w_1', 'grad_w_2', 'grad_g_final', 'delta_g_attn', 'delta_w_in', 'delta_b_in', 'delta_sinks_a', 'delta_g_out_a', 'delta_g_out_b', 'delta_w_out', 'delta_g_mlp', 'delta_w_1', 'delta_w_2', 'delta_g_final', 'new_m_g_attn', 'new_m_w_in', 'new_m_b_in', 'new_m_sinks_a', 'new_m_g_out_a', 'new_m_g_out_b', 'new_m_w_out', 'new_m_g_mlp', 'new_m_w_1', 'new_m_w_2', 'new_m_g_final', 'new_v_g_attn', 'new_v_w_in', 'new_v_b_in', 'new_v_sinks_a', 'new_v_g_out_a', 'new_v_g_out_b', 'new_v_w_out', 'new_v_g_mlp', 'new_v_w_1', 'new_v_w_2', 'new_v_g_final']
TWIN_LEAF_KINDS = {'loss': 'loss', 'grad_x': 'grad_x', 'grad_g_attn': 'grad_w', 'grad_w_in': 'grad_w', 'grad_b_in': 'grad_w', 'grad_sinks_a': 'grad_w', 'grad_g_out_a': 'grad_w', 'grad_g_out_b': 'grad_w', 'grad_w_out': 'grad_w', 'grad_g_mlp': 'grad_w', 'grad_w_1': 'grad_w', 'grad_w_2': 'grad_w', 'grad_g_final': 'grad_w', 'delta_g_attn': 'delta_w', 'delta_w_in': 'delta_w', 'delta_b_in': 'delta_w', 'delta_sinks_a': 'delta_w', 'delta_g_out_a': 'delta_w', 'delta_g_out_b': 'delta_w', 'delta_w_out': 'delta_w', 'delta_g_mlp': 'delta_w', 'delta_w_1': 'delta_w', 'delta_w_2': 'delta_w', 'delta_g_final': 'delta_w', 'new_m_g_attn': 'new_m', 'new_m_w_in': 'new_m', 'new_m_b_in': 'new_m', 'new_m_sinks_a': 'new_m', 'new_m_g_out_a': 'new_m', 'new_m_g_out_b': 'new_m', 'new_m_w_out': 'new_m', 'new_m_g_mlp': 'new_m', 'new_m_w_1': 'new_m', 'new_m_w_2': 'new_m', 'new_m_g_final': 'new_m', 'new_v_g_attn': 'new_v', 'new_v_w_in': 'new_v', 'new_v_b_in': 'new_v', 'new_v_sinks_a': 'new_v', 'new_v_g_out_a': 'new_v', 'new_v_g_out_b': 'new_v', 'new_v_w_out': 'new_v', 'new_v_g_mlp': 'new_v', 'new_v_w_1': 'new_v', 'new_v_w_2': 'new_v', 'new_v_g_final': 'new_v'}


def _forward(args):
    return _fwd_reference(*[args[k] for k in FWD_PARAMS])


def _output_shape():
    def fwd():
        inp = _fwd_setup_inputs(0)
        return _fwd_reference(*[inp[k] for k in FWD_PARAMS])
    out = _jax.eval_shape(fwd)
    return out.shape, out.dtype

N_MICROBATCH = 1
ADAM_LR = 0.001
ADAM_B1 = 0.9
ADAM_B2 = 0.999
ADAM_EPS = 1e-08
ADAM_WD = 0.01
ADAM_STEP = 10
PER_EXAMPLE_BATCH_AXIS = {'x': 0, 'loss_target': 0}
SHARED_INPUTS = []
_WEIGHT_DTYPES = {'g_attn': _jnp.float32, 'w_in': _jnp.float32, 'b_in': _jnp.float32, 'sinks_a': _jnp.float32, 'g_out_a': _jnp.float32, 'g_out_b': _jnp.float32, 'w_out': _jnp.float32, 'g_mlp': _jnp.float32, 'w_1': _jnp.float32, 'w_2': _jnp.float32, 'g_final': _jnp.float32}
MOMENT_SCALE = {'g_attn': 1.403577e-01, 'w_in': 9.203740e-02, 'b_in': 3.370140e-01, 'sinks_a': 8.291832e-02, 'g_out_a': 9.870841e-02, 'g_out_b': 9.761683e-02, 'w_out': 9.528336e-02, 'g_mlp': 9.885860e-02, 'w_1': 4.742964e-02, 'w_2': 9.579915e-02, 'g_final': 3.214682e+01}


def _to_microbatches(a, axis):
    t = _jnp.moveaxis(a, axis, 0)
    t = t.reshape((N_MICROBATCH, t.shape[0] // N_MICROBATCH) + t.shape[1:])
    return _jnp.moveaxis(t, 1, axis + 1)


def setup_inputs(seed: int = 0) -> dict:
    inp = _fwd_setup_inputs(seed)
    key = _jax.random.fold_in(_jax.random.key(seed), 7919)
    shape, _ = _output_shape()
    out = dict(inp)
    out["loss_target"] = _jax.random.normal(_jax.random.fold_in(key, 0), shape, _jnp.float32)
    for i, name in enumerate(TWIN_WEIGHTS):
        w = inp[name].astype(_jnp.float32)
        if MOMENT_SCALE is None:
            s = _jnp.sqrt(_jnp.mean(_jnp.square(w)) + 1e-30)
        else:
            s = MOMENT_SCALE[name]
        km, kv = _jax.random.split(_jax.random.fold_in(key, i + 1))
        out[name] = w
        out["m_" + name] = s * _jax.random.normal(km, w.shape, _jnp.float32)
        out["v_" + name] = (s * s) * _jax.random.uniform(kv, w.shape, _jnp.float32, 0.5, 1.5)
    if N_MICROBATCH > 1:
        for name, axis in PER_EXAMPLE_BATCH_AXIS.items():
            out[name] = _to_microbatches(out[name], axis)
    return {'x': out['x'], 'g_attn': out['g_attn'], 'w_in': out['w_in'], 'b_in': out['b_in'], 'sinks_a': out['sinks_a'], 'g_out_a': out['g_out_a'], 'g_out_b': out['g_out_b'], 'w_out': out['w_out'], 'g_mlp': out['g_mlp'], 'w_1': out['w_1'], 'w_2': out['w_2'], 'g_final': out['g_final'], 'loss_target': out['loss_target'], 'm_g_attn': out['m_g_attn'], 'm_w_in': out['m_w_in'], 'm_b_in': out['m_b_in'], 'm_sinks_a': out['m_sinks_a'], 'm_g_out_a': out['m_g_out_a'], 'm_g_out_b': out['m_g_out_b'], 'm_w_out': out['m_w_out'], 'm_g_mlp': out['m_g_mlp'], 'm_w_1': out['m_w_1'], 'm_w_2': out['m_w_2'], 'm_g_final': out['m_g_final'], 'v_g_attn': out['v_g_attn'], 'v_w_in': out['v_w_in'], 'v_b_in': out['v_b_in'], 'v_sinks_a': out['v_sinks_a'], 'v_g_out_a': out['v_g_out_a'], 'v_g_out_b': out['v_g_out_b'], 'v_w_out': out['v_w_out'], 'v_g_mlp': out['v_g_mlp'], 'v_w_1': out['v_w_1'], 'v_w_2': out['v_w_2'], 'v_g_final': out['v_g_final']}


def _loss(weights, diff, rest, loss_target):
    with _jax.named_scope("forward"):
        args = {**rest, TWIN_DIFF_INPUT: diff, **{k: w.astype(_WEIGHT_DTYPES[k]) for k, w in weights.items()}}
        y = _forward(args)
    with _jax.named_scope("loss_head"):
        err = _jnp.square(y.astype(_jnp.float32) - loss_target)
        return 0.5 * _jnp.sum(_jnp.mean(err, axis=-1)) if err.ndim else 0.5 * err


def _adamw(w, g, m, v):
    m = ADAM_B1 * m + (1.0 - ADAM_B1) * g
    v = ADAM_B2 * v + (1.0 - ADAM_B2) * _jnp.square(g)
    m_hat = m / (1.0 - ADAM_B1 ** ADAM_STEP)
    v_hat = v / (1.0 - ADAM_B2 ** ADAM_STEP)
    delta = -ADAM_LR * (m_hat / (_jnp.sqrt(v_hat) + ADAM_EPS) + ADAM_WD * w)
    return delta, m, v


def reference(x, g_attn, w_in, b_in, sinks_a, g_out_a, g_out_b, w_out, g_mlp, w_1, w_2, g_final, loss_target, m_g_attn, m_w_in, m_b_in, m_sinks_a, m_g_out_a, m_g_out_b, m_w_out, m_g_mlp, m_w_1, m_w_2, m_g_final, v_g_attn, v_w_in, v_b_in, v_sinks_a, v_g_out_a, v_g_out_b, v_w_out, v_g_mlp, v_w_1, v_w_2, v_g_final):
    given = dict(x=x, g_attn=g_attn, w_in=w_in, b_in=b_in, sinks_a=sinks_a, g_out_a=g_out_a, g_out_b=g_out_b, w_out=w_out, g_mlp=g_mlp, w_1=w_1, w_2=w_2, g_final=g_final, loss_target=loss_target, m_g_attn=m_g_attn, m_w_in=m_w_in, m_b_in=m_b_in, m_sinks_a=m_sinks_a, m_g_out_a=m_g_out_a, m_g_out_b=m_g_out_b, m_w_out=m_w_out, m_g_mlp=m_g_mlp, m_w_1=m_w_1, m_w_2=m_w_2, m_g_final=m_g_final, v_g_attn=v_g_attn, v_w_in=v_w_in, v_b_in=v_b_in, v_sinks_a=v_sinks_a, v_g_out_a=v_g_out_a, v_g_out_b=v_g_out_b, v_w_out=v_w_out, v_g_mlp=v_g_mlp, v_w_1=v_w_1, v_w_2=v_w_2, v_g_final=v_g_final)
    weights = {n: given[n] for n in TWIN_WEIGHTS}
    shared = {n: given[n] for n in SHARED_INPUTS}
    per_example = {n: given[n] for n in ['x']}
    grad_fn = _jax.value_and_grad(_loss, argnums=(0, 1))

    def one_microbatch(ex, loss_target):
        ex = dict(ex)
        diff = ex.pop(TWIN_DIFF_INPUT)
        return grad_fn(weights, diff, {**shared, **ex}, loss_target)

    if N_MICROBATCH == 1:
        loss, (grad_w, grad_x) = one_microbatch(per_example, given["loss_target"])
    else:
        def body(carry, xs):
            loss_sum, grad_sum = carry
            l_k, (gw_k, gx_k) = one_microbatch(xs[0], xs[1])
            with _jax.named_scope("update"):
                return (loss_sum + l_k, _jax.tree.map(_jnp.add, grad_sum, gw_k)), gx_k

        init = (_jnp.zeros((), _jnp.float32), _jax.tree.map(_jnp.zeros_like, weights))
        (loss, grad_w), grad_x = _jax.lax.scan(body, init, (per_example, given["loss_target"]))
    with _jax.named_scope("update"):
        delta_w, new_m, new_v = {}, {}, {}
        for n in TWIN_WEIGHTS:
            delta_w[n], new_m[n], new_v[n] = _adamw(weights[n], grad_w[n], given["m_" + n], given["v_" + n])
    return (loss, grad_x, *[grad_w[n] for n in TWIN_WEIGHTS], *[delta_w[n] for n in TWIN_WEIGHTS],
            *[new_m[n] for n in TWIN_WEIGHTS], *[new_v[n] for n in TWIN_WEIGHTS])
```

```python
import functools

import numpy as np
import jax
import jax.numpy as jnp
from jax import lax
from jax.experimental import pallas as pl
from jax.experimental.pallas import tpu as pltpu

F32 = jnp.float32
BF16 = jnp.bfloat16

HEAD_DIM = 64
N_HEADS = 16
KV_HEADS_A = 2
BLOCK = 128
WINDOW_A = 128
DILATED_BRANCHES = ((128, 1), (512, 4), (2048, 16))
EPS = 1e-5
NEG_INF = -1e30
N_DEV = 8

ADAM_LR = 0.001
ADAM_B1 = 0.9
ADAM_B2 = 0.999
ADAM_EPS = 1e-08
ADAM_WD = 0.01
ADAM_STEP = 10

VMEM_LIMIT_BYTES = 56 * 1024 * 1024
MESH = pl.DeviceIdType.MESH
ANY = pl.BlockSpec(memory_space=pl.ANY)

NN = (((1,), (0,)), ((), ()))
NT = (((1,), (1,)), ((), ()))
TN = (((0,), (0,)), ((), ()))


def _dot(a, b, dims):
    return lax.dot_general(a, b, dims, preferred_element_type=F32)


def _params(*sem):
    return pltpu.CompilerParams(dimension_semantics=sem, vmem_limit_bytes=VMEM_LIMIT_BYTES)


def _matmul(a, b, dims, out_dtypes, epilogue, *, tm, tn, tk, name, tile_ins=(), row_ins=()):
    if dims == "tn":
        K, M = a.shape
    else:
        M, K = a.shape
    N = b.shape[0] if dims == "nt" else b.shape[1]
    tm, tn, tk = min(tm, M), min(tn, N), min(tk, K)
    assert M % tm == 0 and N % tn == 0 and K % tk == 0, (name, M, N, K, tm, tn, tk)
    nk = K // tk
    n_tile, n_row, n_out = len(tile_ins), len(row_ins), len(out_dtypes)
    dn = {"nn": NN, "nt": NT, "tn": TN}[dims]

    def kern(*refs):
        a_ref, b_ref = refs[:2]
        tile_refs = refs[2:2 + n_tile]
        row_refs = refs[2 + n_tile:2 + n_tile + n_row]
        out_refs = refs[2 + n_tile + n_row:2 + n_tile + n_row + n_out]

        def finish(acc):
            outs = epilogue(acc, *[r[...] for r in tile_refs], *[r[...] for r in row_refs])
            for o_ref, o in zip(out_refs, outs):
                o_ref[...] = o.astype(o_ref.dtype)

        if nk == 1:
            finish(_dot(a_ref[...], b_ref[...], dn))
        else:
            acc_ref = refs[-1]
            k = pl.program_id(2)

            @pl.when(k == 0)
            def _():
                acc_ref[...] = jnp.zeros_like(acc_ref)

            acc_ref[...] += _dot(a_ref[...], b_ref[...], dn)

            @pl.when(k == nk - 1)
            def _():
                finish(acc_ref[...])

    if dims == "tn":
        a_spec = pl.BlockSpec((tk, tm), lambda i, j, k: (k, i))
    else:
        a_spec = pl.BlockSpec((tm, tk), lambda i, j, k: (i, k))
    if dims == "nt":
        b_spec = pl.BlockSpec((tn, tk), lambda i, j, k: (j, k))
    else:
        b_spec = pl.BlockSpec((tk, tn), lambda i, j, k: (k, j))
    tile_spec = pl.BlockSpec((tm, tn), lambda i, j, k: (i, j))
    row_spec = pl.BlockSpec((1, tn), lambda i, j, k: (0, j))
    outs = pl.pallas_call(
        kern,
        name=name,
        grid=(M // tm, N // tn, nk),
        in_specs=[a_spec, b_spec] + [tile_spec] * n_tile + [row_spec] * n_row,
        out_specs=[tile_spec] * n_out,
        out_shape=[jax.ShapeDtypeStruct((M, N), dt) for dt in out_dtypes],
        scratch_shapes=[pltpu.VMEM((tm, tn), F32)] if nk > 1 else [],
        compiler_params=_params("parallel", "parallel", "arbitrary"),
    )(a, b, *tile_ins, *row_ins)
    return outs


ROWS = 256
MIX_ROWS = 128


def _rstd(xv):
    return lax.rsqrt(jnp.mean(xv * xv, axis=-1, keepdims=True) + EPS)


def _norm_fwd(x, g, name):
    T, D = x.shape

    def kern(x_ref, g_ref, h_ref):
        xv = x_ref[...]
        h_ref[...] = ((xv * _rstd(xv)) * g_ref[...]).astype(h_ref.dtype)

    return pl.pallas_call(
        kern, name=name, grid=(T // ROWS,),
        in_specs=[pl.BlockSpec((ROWS, D), lambda i: (i, 0)), pl.BlockSpec((1, D), lambda i: (0, 0))],
        out_specs=pl.BlockSpec((ROWS, D), lambda i: (i, 0)),
        out_shape=jax.ShapeDtypeStruct((T, D), BF16),
        compiler_params=_params("parallel"),
    )(x, g)


def _norm_bwd(dh, x, g, res, name):
    T, D = x.shape

    def kern(dh_ref, x_ref, g_ref, res_ref, dx_ref, dxb_ref, dg_ref):
        @pl.when(pl.program_id(0) == 0)
        def _():
            dg_ref[...] = jnp.zeros_like(dg_ref)

        xv = x_ref[...]
        r = _rstd(xv)
        xn = xv * r
        dhv = dh_ref[...]
        dg_ref[...] += jnp.sum(dhv * xn, axis=0, keepdims=True)
        t = dhv * g_ref[...]
        dx = res_ref[...] + r * (t - xn * jnp.mean(t * xn, axis=-1, keepdims=True))
        dx_ref[...] = dx
        dxb_ref[...] = dx.astype(BF16)

    row = pl.BlockSpec((ROWS, D), lambda i: (i, 0))
    vec = pl.BlockSpec((1, D), lambda i: (0, 0))
    return pl.pallas_call(
        kern, name=name, grid=(T // ROWS,),
        in_specs=[row, row, vec, row],
        out_specs=[row, row, vec],
        out_shape=[jax.ShapeDtypeStruct((T, D), F32), jax.ShapeDtypeStruct((T, D), BF16),
                   jax.ShapeDtypeStruct((1, D), F32)],
        compiler_params=_params("arbitrary"),
    )(dh, x, g, res)


def _loss_head(x3, tgt, g):
    T, D = x3.shape

    def kern(x_ref, t_ref, g_ref, dx_ref, dxb_ref, dg_ref, loss_ref):
        @pl.when(pl.program_id(0) == 0)
        def _():
            dg_ref[...] = jnp.zeros_like(dg_ref)
            loss_ref[...] = jnp.zeros_like(loss_ref)

        xv = x_ref[...]
        gv = g_ref[...]
        r = _rstd(xv)
        xn = xv * r
        err = xn * gv - t_ref[...]
        per_tok = jnp.mean(err * err, axis=-1, keepdims=True)
        loss_ref[...] += 0.5 * jnp.sum(per_tok, axis=0, keepdims=True)
        dy = err * (1.0 / D)
        dg_ref[...] += jnp.sum(dy * xn, axis=0, keepdims=True)
        t = dy * gv
        dx = r * (t - xn * jnp.mean(t * xn, axis=-1, keepdims=True))
        dx_ref[...] = dx
        dxb_ref[...] = dx.astype(BF16)

    row = pl.BlockSpec((ROWS, D), lambda i: (i, 0))
    vec = pl.BlockSpec((1, D), lambda i: (0, 0))
    return pl.pallas_call(
        kern, name="loss_head", grid=(T // ROWS,),
        in_specs=[row, row, vec],
        out_specs=[row, row, vec, pl.BlockSpec((1, 128), lambda i: (0, 0))],
        out_shape=[jax.ShapeDtypeStruct((T, D), F32), jax.ShapeDtypeStruct((T, D), BF16),
                   jax.ShapeDtypeStruct((1, D), F32), jax.ShapeDtypeStruct((1, 128), F32)],
        compiler_params=_params("arbitrary"),
    )(x3, tgt, g)


def _head_seg_matrix(width):
    seg = np.arange(width) // HEAD_DIM
    return jnp.asarray(seg[:, None] == seg[None, :], dtype=BF16)


def _head_sum(v, seg):
    hi = v.astype(BF16)
    lo = (v - hi.astype(F32)).astype(BF16)
    return _dot(hi, seg, NN) + _dot(lo, seg, NN)


def _branch_weights(l1, l2, l3):
    lm = jnp.maximum(jnp.maximum(l1, l2), l3)
    e1, e2, e3 = jnp.exp(l1 - lm), jnp.exp(l2 - lm), jnp.exp(l3 - lm)
    inv = 1.0 / (e1 + e2 + e3)
    return e1 * inv, e2 * inv, e3 * inv


def _mix_fwd(oa, obs, lbs, ga, gb):
    T, W = oa.shape

    def kern(oa_ref, o1, o2, o3, l1, l2, l3, ga_ref, gb_ref, mix_ref):
        w1, w2, w3 = _branch_weights(l1[...], l2[...], l3[...])
        ob = w1 * o1[...] + w2 * o2[...] + w3 * o3[...]
        oav = oa_ref[...]
        mix_ref[:, :W] = ((oav * _rstd(oav)) * ga_ref[...]).astype(BF16)
        mix_ref[:, W:] = ((ob * _rstd(ob)) * gb_ref[...]).astype(BF16)

    row = pl.BlockSpec((MIX_ROWS, W), lambda i: (i, 0))
    vec = pl.BlockSpec((1, W), lambda i: (0, 0))
    return pl.pallas_call(
        kern, name="mix_fwd", grid=(T // MIX_ROWS,),
        in_specs=[row] * 7 + [vec, vec],
        out_specs=pl.BlockSpec((MIX_ROWS, 2 * W), lambda i: (i, 0)),
        out_shape=jax.ShapeDtypeStruct((T, 2 * W), BF16),
        compiler_params=_params("parallel"),
    )(oa, *obs, *lbs, ga, gb)


def _mix_bwd(dmix, oa, obs, lbs, ga, gb):
    T, W = oa.shape
    seg = _head_seg_matrix(W)

    def kern(dm_ref, oa_ref, o1, o2, o3, l1, l2, l3, ga_ref, gb_ref, seg_ref,
             doa_ref, da_ref, do1, do2, do3, d1, d2, d3, dga_ref, dgb_ref):
        @pl.when(pl.program_id(0) == 0)
        def _():
            dga_ref[...] = jnp.zeros_like(dga_ref)
            dgb_ref[...] = jnp.zeros_like(dgb_ref)

        segv = seg_ref[...]
        oav = oa_ref[...]
        r = _rstd(oav)
        on = oav * r
        dy = dm_ref[:, :W]
        dga_ref[...] += jnp.sum(dy * on, axis=0, keepdims=True)
        t = dy * ga_ref[...]
        doa = r * (t - on * jnp.mean(t * on, axis=-1, keepdims=True))
        doa_ref[...] = doa.astype(BF16)
        da_ref[...] = _head_sum(doa * oav, segv)
        w1, w2, w3 = _branch_weights(l1[...], l2[...], l3[...])
        ob = w1 * o1[...] + w2 * o2[...] + w3 * o3[...]
        r = _rstd(ob)
        on = ob * r
        dy = dm_ref[:, W:]
        dgb_ref[...] += jnp.sum(dy * on, axis=0, keepdims=True)
        t = dy * gb_ref[...]
        dob = r * (t - on * jnp.mean(t * on, axis=-1, keepdims=True))
        c = _head_sum(dob * ob, segv)
        do1[...] = (w1 * dob).astype(BF16)
        do2[...] = (w2 * dob).astype(BF16)
        do3[...] = (w3 * dob).astype(BF16)
        d1[...] = w1 * c
        d2[...] = w2 * c
        d3[...] = w3 * c

    row = pl.BlockSpec((MIX_ROWS, W), lambda i: (i, 0))
    vec = pl.BlockSpec((1, W), lambda i: (0, 0))
    bf = jax.ShapeDtypeStruct((T, W), BF16)
    ff = jax.ShapeDtypeStruct((T, W), F32)
    vv = jax.ShapeDtypeStruct((1, W), F32)
    return pl.pallas_call(
        kern, name="mix_bwd", grid=(T // MIX_ROWS,),
        in_specs=[pl.BlockSpec((MIX_ROWS, 2 * W), lambda i: (i, 0))] + [row] * 7 + [vec, vec,
                  pl.BlockSpec((W, W), lambda i: (0, 0))],
        out_specs=[row] * 8 + [vec, vec],
        out_shape=[bf, ff, bf, bf, bf, ff, ff, ff, vv, vv],
        compiler_params=_params("arbitrary"),
    )(dmix, oa, *obs, *lbs, ga, gb, seg)


def _alibi_slopes(n):
    return np.asarray(2.0 ** (-8.0 * (np.arange(n) + 1) / n)).astype(np.float32)


def _band_bias(max_steps, step_dist):
    qi = np.arange(BLOCK)[:, None]
    kj = np.arange(BLOCK)[None, :]
    slopes = _alibi_slopes(N_HEADS)
    out = []
    for steps in (qi - kj, qi + BLOCK - kj):
        valid = (steps >= 0) & (steps <= max_steps)
        alibi = slopes[:, None, None] * (step_dist * steps).astype(np.float32)[None]
        out.append(np.where(valid[None], -alibi, np.float32(NEG_INF)).astype(np.float32))
    return jnp.asarray(np.stack(out))


class _AttnLayout:
    def __init__(self, dil, kv_heads, q_stride, q_off, k_stride, k_off, v_off):
        self.dil = dil
        self.kv_heads = kv_heads
        self.kw = kv_heads * HEAD_DIM
        self.rep = N_HEADS // kv_heads
        self.q_col = lambda r: r * q_stride + q_off
        self.k_col = lambda r: r * k_stride + k_off
        self.v_col = lambda r: r * k_stride + v_off


QW = N_HEADS * HEAD_DIM


def _attn_fwd(proj, bias, sinks, lay, name):
    L = proj.shape[0]
    nb = L // BLOCK
    kw, rep = lay.kw, lay.rep
    use_sinks = sinks is not None
    scale = HEAD_DIM ** -0.5

    def kern(*refs):
        if use_sinks:
            q_ref, kc_ref, kp_ref, vc_ref, vp_ref, b_ref, s_ref, o_ref, l_ref = refs
        else:
            q_ref, kc_ref, kp_ref, vc_ref, vp_ref, b_ref, o_ref, l_ref = refs
        first = pl.program_id(1) == 0
        for h in range(N_HEADS):
            hs = slice(h * HEAD_DIM, (h + 1) * HEAD_DIM)
            gs = slice((h // rep) * HEAD_DIM, (h // rep + 1) * HEAD_DIM)
            q = q_ref[:, hs]
            s_c = _dot(q, kc_ref[:, gs], NT) * scale + b_ref[0, h]
            s_p = _dot(q, kp_ref[:, gs], NT) * scale + b_ref[1, h]
            s_p = jnp.where(first, NEG_INF, s_p)
            m = jnp.maximum(jnp.max(s_c, axis=-1, keepdims=True), jnp.max(s_p, axis=-1, keepdims=True))
            if use_sinks:
                sink = s_ref[:, h:h + 1]
                m = jnp.maximum(m, sink)
            p_c = jnp.exp(s_c - m)
            p_p = jnp.exp(s_p - m)
            denom = jnp.sum(p_c, axis=-1, keepdims=True) + jnp.sum(p_p, axis=-1, keepdims=True)
            if use_sinks:
                denom = denom + jnp.exp(sink - m)
            o = _dot(p_c.astype(BF16), vc_ref[:, gs], NN) + _dot(p_p.astype(BF16), vp_ref[:, gs], NN)
            o_ref[:, hs] = o / denom
            l_ref[:, hs] = jnp.broadcast_to(m + jnp.log(denom), (BLOCK, HEAD_DIM))

    prev = lambda i: jnp.maximum(i - 1, 0)
    in_specs = [
        pl.BlockSpec((BLOCK, QW), lambda r, i: (i, lay.q_col(r))),
        pl.BlockSpec((BLOCK, kw), lambda r, i: (i, lay.k_col(r))),
        pl.BlockSpec((BLOCK, kw), lambda r, i: (prev(i), lay.k_col(r))),
        pl.BlockSpec((BLOCK, kw), lambda r, i: (i, lay.v_col(r))),
        pl.BlockSpec((BLOCK, kw), lambda r, i: (prev(i), lay.v_col(r))),
        pl.BlockSpec((2, N_HEADS, BLOCK, BLOCK), lambda r, i: (0, 0, 0, 0)),
    ]
    args = [proj, proj, proj, proj, proj, bias]
    if use_sinks:
        in_specs.append(pl.BlockSpec((1, N_HEADS), lambda r, i: (0, 0)))
        args.append(sinks)
    out_spec = pl.BlockSpec((BLOCK, QW), lambda r, i: (i, r))
    out = jax.ShapeDtypeStruct((L, lay.dil * QW), F32)
    return pl.pallas_call(
        kern, name=name, grid=(lay.dil, nb),
        in_specs=in_specs, out_specs=[out_spec, out_spec], out_shape=[out, out],
        compiler_params=_params("parallel", "parallel"),
    )(*args)


def _attn_bwd(proj, do, lse, dd, bias, sinks, lay, name):
    L = proj.shape[0]
    nb = L // BLOCK
    kw, rep, kvh = lay.kw, lay.rep, lay.kv_heads
    use_sinks = sinks is not None
    scale = HEAD_DIM ** -0.5

    def kern(*refs):
        if use_sinks:
            (q_ref, kc_ref, kp_ref, vc_ref, vp_ref, do_ref, l_ref, d_ref, b_ref, s_ref,
             dq_ref, dk_ref, dv_ref, ds_ref, ck_ref, cv_ref) = refs
        else:
            (q_ref, kc_ref, kp_ref, vc_ref, vp_ref, do_ref, l_ref, d_ref, b_ref,
             dq_ref, dk_ref, dv_ref, ck_ref, cv_ref) = refs
        r = pl.program_id(0)
        i = pl.program_id(1)
        first = i == 0

        @pl.when(first)
        def _():
            ck_ref[...] = jnp.zeros_like(ck_ref)
            cv_ref[...] = jnp.zeros_like(cv_ref)

        if use_sinks:
            @pl.when(first & (r == 0))
            def _():
                ds_ref[...] = jnp.zeros_like(ds_ref)

        @pl.when(i < nb)
        def _():
            for g in range(kvh):
                gs = slice(g * HEAD_DIM, (g + 1) * HEAD_DIM)
                kc, kp, vc, vp = kc_ref[:, gs], kp_ref[:, gs], vc_ref[:, gs], vp_ref[:, gs]
                dkc = jnp.zeros((BLOCK, HEAD_DIM), F32)
                dkp = jnp.zeros((BLOCK, HEAD_DIM), F32)
                dvc = jnp.zeros((BLOCK, HEAD_DIM), F32)
                dvp = jnp.zeros((BLOCK, HEAD_DIM), F32)
                for h in range(g * rep, (g + 1) * rep):
                    hs = slice(h * HEAD_DIM, (h + 1) * HEAD_DIM)
                    q = q_ref[:, hs]
                    dov = do_ref[:, hs]
                    lcol = l_ref[:, h * HEAD_DIM:h * HEAD_DIM + 1]
                    dcol = d_ref[:, h * HEAD_DIM:h * HEAD_DIM + 1]
                    s_c = _dot(q, kc, NT) * scale + b_ref[0, h]
                    s_p = _dot(q, kp, NT) * scale + b_ref[1, h]
                    s_p = jnp.where(first, NEG_INF, s_p)
                    p_c = jnp.exp(s_c - lcol)
                    p_p = jnp.exp(s_p - lcol)
                    ds_c = (p_c * (_dot(dov, vc, NT) - dcol) * scale).astype(BF16)
                    ds_p = (p_p * (_dot(dov, vp, NT) - dcol) * scale).astype(BF16)
                    dq_ref[:, hs] = _dot(ds_c, kc, NN) + _dot(ds_p, kp, NN)
                    dkc += _dot(ds_c, q, TN)
                    dkp += _dot(ds_p, q, TN)
                    dvc += _dot(p_c.astype(BF16), dov, TN)
                    dvp += _dot(p_p.astype(BF16), dov, TN)
                    if use_sinks:
                        p_sink = jnp.exp(s_ref[:, h:h + 1] - lcol)
                        ds_ref[:, h:h + 1] += -jnp.sum(p_sink * dcol, axis=0, keepdims=True)
                dk_ref[:, gs] = ck_ref[:, gs] + dkp
                dv_ref[:, gs] = cv_ref[:, gs] + dvp
                ck_ref[:, gs] = dkc
                cv_ref[:, gs] = dvc

        @pl.when(i == nb)
        def _():
            dk_ref[...] = ck_ref[...]
            dv_ref[...] = cv_ref[...]

    cur = lambda i: jnp.minimum(i, nb - 1)
    prev = lambda i: jnp.maximum(jnp.minimum(i, nb - 1) - 1, 0)
    done = lambda i: jnp.maximum(i - 1, 0)
    qspec = lambda col: pl.BlockSpec((BLOCK, QW), lambda r, i: (cur(i), col(r)))
    in_specs = [
        qspec(lay.q_col),
        pl.BlockSpec((BLOCK, kw), lambda r, i: (cur(i), lay.k_col(r))),
        pl.BlockSpec((BLOCK, kw), lambda r, i: (prev(i), lay.k_col(r))),
        pl.BlockSpec((BLOCK, kw), lambda r, i: (cur(i), lay.v_col(r))),
        pl.BlockSpec((BLOCK, kw), lambda r, i: (prev(i), lay.v_col(r))),
        qspec(lambda r: r), qspec(lambda r: r), qspec(lambda r: r),
        pl.BlockSpec((2, N_HEADS, BLOCK, BLOCK), lambda r, i: (0, 0, 0, 0)),
    ]
    args = [proj, proj, proj, proj, proj, do, lse, dd, bias]
    out_specs = [
        qspec(lambda r: r),
        pl.BlockSpec((BLOCK, kw), lambda r, i: (done(i), r)),
        pl.BlockSpec((BLOCK, kw), lambda r, i: (done(i), r)),
    ]
    dkv_shape = jax.ShapeDtypeStruct((L, lay.dil * kw), F32)
    out_shape = [jax.ShapeDtypeStruct((L, lay.dil * QW), F32), dkv_shape, dkv_shape]
    if use_sinks:
        in_specs.append(pl.BlockSpec((1, N_HEADS), lambda r, i: (0, 0)))
        args.append(sinks)
        out_specs.append(pl.BlockSpec((1, N_HEADS), lambda r, i: (0, 0)))
        out_shape.append(jax.ShapeDtypeStruct((1, N_HEADS), F32))
    return pl.pallas_call(
        kern, name=name, grid=(lay.dil, nb + 1),
        in_specs=in_specs, out_specs=out_specs, out_shape=out_shape,
        scratch_shapes=[pltpu.VMEM((BLOCK, kw), F32), pltpu.VMEM((BLOCK, kw), F32)],
        compiler_params=_params("arbitrary", "arbitrary"),
    )(*args)


def _assemble(groups, name):
    T = groups[0][0].shape[0]
    widths = [g[0].shape[1] for g in groups]
    total = sum(widths)
    flat = [a for g in groups for a in g]

    def kern(*refs):
        ins = refs[:len(flat)]
        out_ref, cs_ref = refs[len(flat):]

        @pl.when(pl.program_id(0) == 0)
        def _():
            cs_ref[...] = jnp.zeros_like(cs_ref)

        pos = off = 0
        for g, w in zip(groups, widths):
            acc = ins[pos][...]
            for j in range(1, len(g)):
                acc = acc + ins[pos + j][...]
            pos += len(g)
            out_ref[:, off:off + w] = acc.astype(BF16)
            cs_ref[:, off:off + w] += jnp.sum(acc, axis=0, keepdims=True)
            off += w

    return pl.pallas_call(
        kern, name=name, grid=(T // ROWS,),
        in_specs=[pl.BlockSpec((ROWS, a.shape[1]), lambda i: (i, 0)) for a in flat],
        out_specs=[pl.BlockSpec((ROWS, total), lambda i: (i, 0)), pl.BlockSpec((1, total), lambda i: (0, 0))],
        out_shape=[jax.ShapeDtypeStruct((T, total), BF16), jax.ShapeDtypeStruct((1, total), F32)],
        compiler_params=_params("arbitrary"),
    )(*flat)


def _adamw(w, g, m, v, name):
    R, C = w.shape
    rows = min(R, ROWS)
    assert R % rows == 0

    def kern(w_ref, g_ref, m_ref, v_ref, d_ref, nm_ref, nv_ref):
        gv = g_ref[...]
        mn = ADAM_B1 * m_ref[...] + (1.0 - ADAM_B1) * gv
        vn = ADAM_B2 * v_ref[...] + (1.0 - ADAM_B2) * jnp.square(gv)
        m_hat = mn / (1.0 - ADAM_B1 ** ADAM_STEP)
        v_hat = vn / (1.0 - ADAM_B2 ** ADAM_STEP)
        d_ref[...] = -ADAM_LR * (m_hat / (jnp.sqrt(v_hat) + ADAM_EPS) + ADAM_WD * w_ref[...])
        nm_ref[...] = mn
        nv_ref[...] = vn

    blk = pl.BlockSpec((rows, C), lambda i: (i, 0))
    shp = jax.ShapeDtypeStruct((R, C), F32)
    return pl.pallas_call(
        kern, name=name, grid=(R // rows,),
        in_specs=[blk] * 4, out_specs=[blk] * 3, out_shape=[shp] * 3,
        compiler_params=_params("parallel"),
    )(w, g, m, v)


SUM_ROWS = 32


def _sum_slots(slots):
    n, R, C = slots.shape
    assert R % SUM_ROWS == 0

    def kern(s_ref, o_ref):
        acc = s_ref[0].astype(F32)
        for k in range(1, n):
            acc = acc + s_ref[k].astype(F32)
        o_ref[...] = acc

    return pl.pallas_call(
        kern, name="sum_partial_grads", grid=(R // SUM_ROWS,),
        in_specs=[pl.BlockSpec((n, SUM_ROWS, C), lambda i: (0, i, 0))],
        out_specs=pl.BlockSpec((SUM_ROWS, C), lambda i: (i, 0)),
        out_shape=jax.ShapeDtypeStruct((R, C), F32),
        compiler_params=_params("parallel"),
    )(slots)


def _place():
    return lax.axis_index("x"), lax.axis_index("y"), lax.axis_index("c")


def _index(p):
    return 4 * p[0] + 2 * p[1] + p[2]


FLIPS = [(fx, fy, fc) for fx in (0, 1) for fy in (0, 1) for fc in (0, 1)][1:]


def _peer(me, flip):
    return tuple(1 - a if f else a for a, f in zip(me, flip))


def _gather_rows(shards, name):
    nw = len(shards)

    def body(*refs):
        ins, outs = refs[:nw], refs[nw:2 * nw]
        send_sems, recv_sems, local_sems = refs[2 * nw:]
        x, y, c = me = _place()
        sibling = (x, y, 1 - c)
        chips = [(1 - x, y), (x, 1 - y), (1 - x, 1 - y)]

        def rows(w, p):
            n = ins[w].shape[0]
            return outs[w].at[pl.ds(_index(p) * n, n), :]

        def copy(w, k, block, to, src=None):
            return pltpu.make_async_remote_copy(
                src_ref=rows(w, block) if src is None else src, dst_ref=rows(w, block),
                send_sem=send_sems.at[7 * w + k], recv_sem=recv_sems.at[7 * w + k],
                device_id=to, device_id_type=MESH)

        mine = [pltpu.make_async_copy(ins[w], rows(w, me), local_sems.at[w]) for w in range(nw)]
        for cp in mine:
            cp.start()
        first = []
        for w in range(nw):
            first.append(copy(w, 0, me, sibling, src=ins[w]))
            first += [copy(w, 1 + j, me, (*chip, c), src=ins[w]) for j, chip in enumerate(chips)]
        for cp in first:
            cp.start()
        passed = []
        for w in range(nw):
            for j, chip in enumerate(chips):
                copy(w, 1 + j, (*chip, c), me).wait_recv()
                passed.append(copy(w, 4 + j, (*chip, c), sibling))
                passed[-1].start()
        for w in range(nw):
            copy(w, 0, sibling, me).wait_recv()
            for j, chip in enumerate(chips):
                copy(w, 4 + j, (*chip, 1 - c), me).wait_recv()
        for cp in first + passed:
            cp.wait_send()
        for cp in mine:
            cp.wait()

    return pl.pallas_call(
        body, name=name,
        in_specs=[ANY] * nw, out_specs=[ANY] * nw,
        out_shape=[jax.ShapeDtypeStruct((N_DEV * s.shape[0], s.shape[1]), s.dtype) for s in shards],
        scratch_shapes=[pltpu.SemaphoreType.DMA((7 * nw,)), pltpu.SemaphoreType.DMA((7 * nw,)),
                        pltpu.SemaphoreType.DMA((nw,))],
    )(*shards)


def _scatter_rows(parts, name):
    nw = len(parts)
    n_rows = [p.shape[0] // N_DEV for p in parts]
    offs = [sum(n_rows[:w]) for w in range(nw)]
    total, C = sum(n_rows), parts[0].shape[1]

    def body(*refs):
        ins, out = refs[:nw], refs[nw]
        send_sems, recv_sems, local_sems = refs[nw + 1:]
        me = _place()

        def src(w, owner):
            return ins[w].at[pl.ds(_index(owner) * n_rows[w], n_rows[w]), :]

        def dst(w, sender):
            return out.at[_index(sender), pl.ds(offs[w], n_rows[w]), :]

        def copy(k, w, owner, sender, to):
            return pltpu.make_async_remote_copy(
                src_ref=src(w, owner), dst_ref=dst(w, sender),
                send_sem=send_sems.at[nw * k + w], recv_sem=recv_sems.at[nw * k + w],
                device_id=to, device_id_type=MESH)

        mine = [pltpu.make_async_copy(src(w, me), dst(w, me), local_sems.at[w]) for w in range(nw)]
        for cp in mine:
            cp.start()
        sends = []
        for k, flip in enumerate(FLIPS):
            peer = _peer(me, flip)
            for w in range(nw):
                sends.append(copy(k, w, peer, me, peer))
                sends[-1].start()
        for k, flip in enumerate(FLIPS):
            peer = _peer(me, flip)
            for w in range(nw):
                copy(k, w, me, peer, me).wait_recv()
        for cp in sends:
            cp.wait_send()
        for cp in mine:
            cp.wait()

    return pl.pallas_call(
        body, name=name,
        in_specs=[ANY] * nw, out_specs=ANY,
        out_shape=jax.ShapeDtypeStruct((N_DEV, total, C), parts[0].dtype),
        scratch_shapes=[pltpu.SemaphoreType.DMA((7 * nw,)), pltpu.SemaphoreType.DMA((7 * nw,)),
                        pltpu.SemaphoreType.DMA((nw,))],
    )(*parts)


def _sum_over_devices(v):
    shape = v.shape

    def body(v_ref, sum_ref, all_ref, send_sems, recv_sems):
        me = _place()
        all_ref[_index(me)] = v_ref[...]
        sends = []
        for k, flip in enumerate(FLIPS):
            peer = _peer(me, flip)
            sends.append(pltpu.make_async_remote_copy(
                src_ref=v_ref, dst_ref=all_ref.at[_index(me)],
                send_sem=send_sems.at[k], recv_sem=recv_sems.at[k], device_id=peer, device_id_type=MESH))
            sends[-1].start()
        for k, flip in enumerate(FLIPS):
            peer = _peer(me, flip)
            pltpu.make_async_remote_copy(
                src_ref=v_ref, dst_ref=all_ref.at[_index(peer)],
                send_sem=send_sems.at[k], recv_sem=recv_sems.at[k], device_id=peer, device_id_type=MESH).wait_recv()
        for cp in sends:
            cp.wait_send()
        acc = all_ref[0]
        for s in range(1, N_DEV):
            acc = acc + all_ref[s]
        sum_ref[...] = acc

    vmem = pl.BlockSpec(memory_space=pltpu.VMEM)
    return pl.pallas_call(
        body, name="sum_small_grads",
        in_specs=[vmem], out_specs=[vmem, vmem],
        out_shape=[jax.ShapeDtypeStruct(shape, F32), jax.ShapeDtypeStruct((N_DEV,) + shape, F32)],
        scratch_shapes=[pltpu.SemaphoreType.DMA((7,)), pltpu.SemaphoreType.DMA((7,))],
    )(v)[0]


SMALL_ROWS = 8


def _pack_small(vectors):
    padded = []
    for vec in vectors:
        vec = vec.reshape(-1)
        padded.append(jnp.pad(vec, (0, -vec.shape[0] % 128)))
    flat = jnp.concatenate(padded)
    flat = jnp.pad(flat, (0, -flat.shape[0] % (SMALL_ROWS * 128)))
    return flat.reshape(SMALL_ROWS, -1)


def _unpack_small(packed, shapes):
    flat = packed.reshape(-1)
    out, off = [], 0
    for shp in shapes:
        n = int(np.prod(shp))
        out.append(flat[off:off + n].reshape(shp))
        off += n + (-n % 128)
    return out


def kernel(x, g_attn, w_in, b_in, sinks_a, g_out_a, g_out_b, w_out, g_mlp, w_1, w_2, g_final, loss_target, m_g_attn, m_w_in, m_b_in, m_sinks_a, m_g_out_a, m_g_out_b, m_w_out, m_g_mlp, m_w_1, m_w_2, m_g_final, v_g_attn, v_w_in, v_b_in, v_sinks_a, v_g_out_a, v_g_out_b, v_w_out, v_g_mlp, v_w_1, v_w_2, v_g_final):
    xs, tgt = x[0], loss_target[0]
    T, D = xs.shape
    n_a = QW + 2 * KV_HEADS_A * HEAD_DIM
    g_fin = g_final.reshape(1, D)

    shards = [w_in[0].T.astype(BF16), w_out[0].astype(BF16), w_1[0].T.astype(BF16), w_2[0].astype(BF16)]
    w_in_t, w_o, w_1_t, w_2_f = _gather_rows(shards, "gather_weights")
    w_in_ta, w_in_tb = w_in_t[:n_a], w_in_t[n_a:]

    ident = lambda acc: (acc,)
    add = lambda acc, other: (acc + other,)
    tiles = dict(tm=512, tn=1024)

    h1 = _norm_fwd(xs, g_attn, "norm_attn")
    proj_a, = _matmul(h1, w_in_ta, "nt", [BF16], add, tm=512, tn=n_a, tk=D, row_ins=[b_in[:, :n_a]], name="proj_a")
    proj_b, = _matmul(h1, w_in_tb, "nt", [BF16], add, tk=D, row_ins=[b_in[:, n_a:]], name="proj_b", **tiles)

    lay_a = _AttnLayout(1, KV_HEADS_A, 0, 0, 0, QW // (KV_HEADS_A * HEAD_DIM), QW // (KV_HEADS_A * HEAD_DIM) + 1)
    bias_a = _band_bias(WINDOW_A - 1, 1)
    o_a, l_a = _attn_fwd(proj_a, bias_a, sinks_a, lay_a, "attn_a_fwd")

    branches = []
    for window, dil in DILATED_BRANCHES:
        lay = _AttnLayout(dil, N_HEADS, 3, 0, 3, 1, 2)
        bias = _band_bias(window // dil, dil)
        view = proj_b.reshape(T // dil, dil * 3 * QW)
        o, l = _attn_fwd(view, bias, None, lay, f"attn_b{dil}_fwd")
        branches.append((lay, bias, view, o.reshape(T, QW), l.reshape(T, QW)))
    o_b = [br[3] for br in branches]
    l_b = [br[4] for br in branches]

    mix = _mix_fwd(o_a, o_b, l_b, g_out_a, g_out_b)
    x2, = _matmul(mix, w_o, "nn", [F32], add, tk=D, tile_ins=[xs], name="out_proj", **tiles)
    h2 = _norm_fwd(x2, g_mlp, "norm_mlp")

    def relu_sq(acc):
        u = jnp.maximum(acc, 0.0)
        return u, u * u

    u, u_sq = _matmul(h2, w_1_t, "nt", [BF16, BF16], relu_sq, tk=D, name="mlp_up", **tiles)
    x3, = _matmul(u_sq, w_2_f, "nn", [F32], add, tk=2048, tile_ins=[x2], name="mlp_down", **tiles)

    dx3, dx3_b, dg_final, loss_dev = _loss_head(x3, tgt, g_fin)

    d_pre, = _matmul(dx3_b, w_2_f, "nt", [BF16], lambda acc, uu: (acc * (2.0 * uu.astype(F32)),),
                     tk=D, tile_ins=[u], name="mlp_down_bwd", **tiles)
    wtiles = dict(tm=1024, tn=1024, tk=1024)
    dw_2, = _matmul(u_sq, dx3_b, "tn", [BF16], ident, name="mlp_down_wgrad", **wtiles)
    dh2, = _matmul(d_pre, w_1_t, "nn", [F32], ident, tk=2048, name="mlp_up_bwd", **tiles)
    dw_1_t, = _matmul(d_pre, h2, "tn", [BF16], ident, name="mlp_up_wgrad", **wtiles)
    dx2, dx2_b, dg_mlp = _norm_bwd(dh2, x2, g_mlp, dx3, "norm_mlp_bwd")

    dmix, = _matmul(dx2_b, w_o, "nt", [F32], ident, tk=D, name="out_proj_bwd", **tiles)
    dw_o, = _matmul(mix, dx2_b, "tn", [BF16], ident, name="out_proj_wgrad", **wtiles)
    do_a, dd_a, do1, do2, do3, dd1, dd2, dd3, dg_out_a, dg_out_b = _mix_bwd(dmix, o_a, o_b, l_b, g_out_a, g_out_b)

    dq_a, dk_a, dv_a, dsinks = _attn_bwd(proj_a, do_a, l_a, dd_a, bias_a, sinks_a, lay_a, "attn_a_bwd")
    dqs, dks, dvs = [], [], []
    for (lay, bias, view, _, l), do_n, dd_n in zip(branches, (do1, do2, do3), (dd1, dd2, dd3)):
        shape = (T // lay.dil, lay.dil * QW)
        dq, dk, dv = _attn_bwd(view, do_n.reshape(shape), l.reshape(shape), dd_n.reshape(shape), bias, None, lay,
                               f"attn_b{lay.dil}_bwd")
        dqs.append(dq.reshape(T, QW))
        dks.append(dk.reshape(T, QW))
        dvs.append(dv.reshape(T, QW))
    dproj_a, db_a = _assemble([[dq_a], [dk_a], [dv_a]], "dproj_a")
    dproj_b, db_b = _assemble([dqs, dks, dvs], "dproj_b")

    dh1_a, = _matmul(dproj_a, w_in_ta, "nn", [F32], ident, tk=n_a, name="in_proj_a_bwd", **tiles)
    dh1, = _matmul(dproj_b, w_in_tb, "nn", [F32], add, tk=3 * QW, tile_ins=[dh1_a], name="in_proj_b_bwd", **tiles)
    dw_in_ta, = _matmul(dproj_a, h1, "tn", [BF16], ident, tm=n_a, tn=1024, tk=512, name="in_proj_a_wgrad")
    dw_in_tb, = _matmul(dproj_b, h1, "tn", [BF16], ident, name="in_proj_b_wgrad", **wtiles)
    dw_in_t = jnp.concatenate([dw_in_ta, dw_in_tb], axis=0)
    dx, _, dg_attn = _norm_bwd(dh1, xs, g_attn, dx2, "norm_attn_bwd")

    slots = _scatter_rows([dw_in_t, dw_o, dw_1_t, dw_2], "scatter_grads")
    g_rows = _sum_slots(slots)
    n_in, n_o, n_1 = w_in.shape[2], w_out.shape[1], w_1.shape[2]
    g_w_in = g_rows[:n_in].T
    g_w_out = g_rows[n_in:n_in + n_o]
    g_w_1 = g_rows[n_in + n_o:n_in + n_o + n_1].T
    g_w_2 = g_rows[n_in + n_o + n_1:]

    small_w = [g_attn, b_in, sinks_a, g_out_a, g_out_b, g_mlp, g_final]
    small_m = [m_g_attn, m_b_in, m_sinks_a, m_g_out_a, m_g_out_b, m_g_mlp, m_g_final]
    small_v = [v_g_attn, v_b_in, v_sinks_a, v_g_out_a, v_g_out_b, v_g_mlp, v_g_final]
    small_g = [dg_attn, jnp.concatenate([db_a, db_b], axis=1), dsinks, dg_out_a, dg_out_b, dg_mlp, dg_final]
    summed = _sum_over_devices(_pack_small(small_g + [loss_dev[:, :1]]))
    shapes = [w.shape for w in small_w]
    *g_small, loss = _unpack_small(summed, shapes + [()])

    big = [
        _adamw(w_in[0], g_w_in, m_w_in[0], v_w_in[0], "adamw_w_in"),
        _adamw(w_out[0], g_w_out, m_w_out[0], v_w_out[0], "adamw_w_out"),
        _adamw(w_1[0], g_w_1, m_w_1[0], v_w_1[0], "adamw_w_1"),
        _adamw(w_2[0], g_w_2, m_w_2[0], v_w_2[0], "adamw_w_2"),
    ]
    g_packed = _pack_small(g_small)
    small = _adamw(_pack_small(small_w), g_packed, _pack_small(small_m), _pack_small(small_v), "adamw_small")
    small = [_unpack_small(s, shapes) for s in small]

    def ordered(small_list, big_list):
        s = list(small_list)
        return [s[0], big_list[0][None], s[1], s[2], s[3], s[4], big_list[1][None], s[5],
                big_list[2][None], big_list[3][None], s[6]]

    grads = ordered(g_small, [g_w_in, g_w_out, g_w_1, g_w_2])
    deltas = ordered(small[0], [b[0] for b in big])
    new_m = ordered(small[1], [b[1] for b in big])
    new_v = ordered(small[2], [b[2] for b in big])
    return (loss, dx[None], *grads, *deltas, *new_m, *new_v)
```

```python
import functools

import numpy as np
import jax
import jax.numpy as jnp
from jax import lax
from jax.experimental import pallas as pl
from jax.experimental.pallas import tpu as pltpu

F32 = jnp.float32
BF16 = jnp.bfloat16

HEAD_DIM = 64
N_HEADS = 16
KV_HEADS_A = 2
BLOCK = 128
WINDOW_A = 128
DILATED_BRANCHES = ((128, 1), (512, 4), (2048, 16))
EPS = 1e-5
NEG_INF = -1e30
N_DEV = 8

ADAM_LR = 0.001
ADAM_B1 = 0.9
ADAM_B2 = 0.999
ADAM_EPS = 1e-08
ADAM_WD = 0.01
ADAM_STEP = 10

VMEM_LIMIT_BYTES = 56 * 1024 * 1024
MESH = pl.DeviceIdType.MESH
ANY = pl.BlockSpec(memory_space=pl.ANY)

NN = (((1,), (0,)), ((), ()))
NT = (((1,), (1,)), ((), ()))
TN = (((0,), (0,)), ((), ()))


def _dot(a, b, dims):
    return lax.dot_general(a, b, dims, preferred_element_type=F32)


def _params(*sem):
    return pltpu.CompilerParams(dimension_semantics=sem, vmem_limit_bytes=VMEM_LIMIT_BYTES)


class _Exchange:
    def __init__(self, ins, out_shapes, n_remote, n_local, copies, aliases=None):
        self.ins, self.out_shapes = list(ins), list(out_shapes)
        self.n_remote, self.n_local = n_remote, n_local
        self.copies = copies
        self.aliases = aliases or {}

    def start(self, refs):
        local, sends, _ = self.copies(*refs)
        for cp in local + sends:
            cp.start()

    def finish(self, refs):
        local, sends, recvs = self.copies(*refs)
        for cp in recvs:
            cp.wait_recv()
        for cp in sends:
            cp.wait_send()
        for cp in local:
            cp.wait()


class _Ride:
    def __init__(self, ex, n_in, n_out, n_scratch):
        self.ex = ex
        self.n = (n_in, n_out, n_scratch)
        self.args = ex.ins if ex else []
        self.in_specs = [ANY] * len(self.args)
        self.out_shapes = ex.out_shapes if ex else []
        self.out_specs = [ANY] * len(self.out_shapes)
        self.scratch = [pltpu.SemaphoreType.DMA((ex.n_remote,)), pltpu.SemaphoreType.DMA((ex.n_remote,)),
                        pltpu.SemaphoreType.DMA((max(ex.n_local, 1),))] if ex else []
        self.aliases = {n_in + i: n_out + o for i, o in ex.aliases.items()} if ex else {}

    def split(self, refs):
        n_in, n_out, n_scratch = self.n
        a = n_in
        b = a + len(self.args)
        c = b + n_out
        d = c + len(self.out_shapes)
        e = d + n_scratch
        return refs[:a], refs[b:c], refs[d:e], (refs[a:b], refs[c:d], *refs[e:])

    def around(self, first, last, exrefs, compute):
        if self.ex is None:
            compute()
            return

        @pl.when(first)
        def _():
            self.ex.start(exrefs)

        compute()

        @pl.when(last)
        def _():
            self.ex.finish(exrefs)


def _run_exchange(ex, name):
    ride = _Ride(ex, 0, 0, 0)

    def body(*refs):
        exrefs = ride.split(refs)[3]
        ex.start(exrefs)
        ex.finish(exrefs)

    return pl.pallas_call(
        body, name=name, in_specs=ride.in_specs, out_specs=ride.out_specs, out_shape=ride.out_shapes,
        scratch_shapes=ride.scratch, input_output_aliases=ride.aliases,
    )(*ride.args)


def _matmul(a, b, dims, out_dtypes, epilogue, *, tm, tn, tk, name, tile_ins=(), row_ins=(), exchange=None):
    if dims == "tn":
        K, M = a.shape
    else:
        M, K = a.shape
    N = b.shape[0] if dims == "nt" else b.shape[1]
    tm, tn, tk = min(tm, M), min(tn, N), min(tk, K)
    assert M % tm == 0 and N % tn == 0 and K % tk == 0, (name, M, N, K, tm, tn, tk)
    grid = (M // tm, N // tn, K // tk)
    nk = grid[2]
    n_tile, n_row, n_out = len(tile_ins), len(row_ins), len(out_dtypes)
    dn = {"nn": NN, "nt": NT, "tn": TN}[dims]
    ride = _Ride(exchange, 2 + n_tile + n_row, n_out, 1 if nk > 1 else 0)

    def kern(*refs):
        ins, out_refs, scratch, exrefs = ride.split(refs)
        a_ref, b_ref = ins[:2]
        tile_refs = ins[2:2 + n_tile]
        row_refs = ins[2 + n_tile:]
        ids = [pl.program_id(d) for d in range(3)]

        def finish(acc):
            outs = epilogue(acc, *[r[...] for r in tile_refs], *[r[...] for r in row_refs])
            for o_ref, o in zip(out_refs, outs):
                o_ref[...] = o.astype(o_ref.dtype)

        def compute():
            if nk == 1:
                finish(_dot(a_ref[...], b_ref[...], dn))
                return
            acc_ref = scratch[0]

            @pl.when(ids[2] == 0)
            def _():
                acc_ref[...] = jnp.zeros_like(acc_ref)

            acc_ref[...] += _dot(a_ref[...], b_ref[...], dn)

            @pl.when(ids[2] == nk - 1)
            def _():
                finish(acc_ref[...])

        first = (ids[0] == 0) & (ids[1] == 0) & (ids[2] == 0)
        last = (ids[0] == grid[0] - 1) & (ids[1] == grid[1] - 1) & (ids[2] == grid[2] - 1)
        ride.around(first, last, exrefs, compute)

    if dims == "tn":
        a_spec = pl.BlockSpec((tk, tm), lambda i, j, k: (k, i))
    else:
        a_spec = pl.BlockSpec((tm, tk), lambda i, j, k: (i, k))
    if dims == "nt":
        b_spec = pl.BlockSpec((tn, tk), lambda i, j, k: (j, k))
    else:
        b_spec = pl.BlockSpec((tk, tn), lambda i, j, k: (k, j))
    tile_spec = pl.BlockSpec((tm, tn), lambda i, j, k: (i, j))
    row_spec = pl.BlockSpec((1, tn), lambda i, j, k: (0, j))
    sem = ("arbitrary",) * 3 if exchange else ("parallel", "parallel", "arbitrary")
    return pl.pallas_call(
        kern,
        name=name,
        grid=grid,
        in_specs=[a_spec, b_spec] + [tile_spec] * n_tile + [row_spec] * n_row + ride.in_specs,
        out_specs=[tile_spec] * n_out + ride.out_specs,
        out_shape=[jax.ShapeDtypeStruct((M, N), dt) for dt in out_dtypes] + ride.out_shapes,
        scratch_shapes=([pltpu.VMEM((tm, tn), F32)] if nk > 1 else []) + ride.scratch,
        input_output_aliases=ride.aliases,
        compiler_params=_params(*sem),
    )(a, b, *tile_ins, *row_ins, *ride.args)


ROWS = 256
MIX_ROWS = 128


def _rstd(xv):
    return lax.rsqrt(jnp.mean(xv * xv, axis=-1, keepdims=True) + EPS)


def _norm_fwd(x, g, name):
    T, D = x.shape

    def kern(x_ref, g_ref, h_ref):
        xv = x_ref[...]
        h_ref[...] = ((xv * _rstd(xv)) * g_ref[...]).astype(h_ref.dtype)

    return pl.pallas_call(
        kern, name=name, grid=(T // ROWS,),
        in_specs=[pl.BlockSpec((ROWS, D), lambda i: (i, 0)), pl.BlockSpec((1, D), lambda i: (0, 0))],
        out_specs=pl.BlockSpec((ROWS, D), lambda i: (i, 0)),
        out_shape=jax.ShapeDtypeStruct((T, D), BF16),
        compiler_params=_params("parallel"),
    )(x, g)


def _norm_bwd(dh, x, g, res, name):
    T, D = x.shape

    def kern(dh_ref, x_ref, g_ref, res_ref, dx_ref, dxb_ref, dg_ref):
        @pl.when(pl.program_id(0) == 0)
        def _():
            dg_ref[...] = jnp.zeros_like(dg_ref)

        xv = x_ref[...]
        r = _rstd(xv)
        xn = xv * r
        dhv = dh_ref[...]
        dg_ref[...] += jnp.sum(dhv * xn, axis=0, keepdims=True)
        t = dhv * g_ref[...]
        dx = res_ref[...] + r * (t - xn * jnp.mean(t * xn, axis=-1, keepdims=True))
        dx_ref[...] = dx
        dxb_ref[...] = dx.astype(BF16)

    row = pl.BlockSpec((ROWS, D), lambda i: (i, 0))
    vec = pl.BlockSpec((1, D), lambda i: (0, 0))
    return pl.pallas_call(
        kern, name=name, grid=(T // ROWS,),
        in_specs=[row, row, vec, row],
        out_specs=[row, row, vec],
        out_shape=[jax.ShapeDtypeStruct((T, D), F32), jax.ShapeDtypeStruct((T, D), BF16),
                   jax.ShapeDtypeStruct((1, D), F32)],
        compiler_params=_params("arbitrary"),
    )(dh, x, g, res)


def _loss_head(x3, tgt, g):
    T, D = x3.shape

    def kern(x_ref, t_ref, g_ref, dx_ref, dxb_ref, dg_ref, loss_ref):
        @pl.when(pl.program_id(0) == 0)
        def _():
            dg_ref[...] = jnp.zeros_like(dg_ref)
            loss_ref[...] = jnp.zeros_like(loss_ref)

        xv = x_ref[...]
        gv = g_ref[...]
        r = _rstd(xv)
        xn = xv * r
        err = xn * gv - t_ref[...]
        per_tok = jnp.mean(err * err, axis=-1, keepdims=True)
        loss_ref[...] += 0.5 * jnp.sum(per_tok, axis=0, keepdims=True)
        dy = err * (1.0 / D)
        dg_ref[...] += jnp.sum(dy * xn, axis=0, keepdims=True)
        t = dy * gv
        dx = r * (t - xn * jnp.mean(t * xn, axis=-1, keepdims=True))
        dx_ref[...] = dx
        dxb_ref[...] = dx.astype(BF16)

    row = pl.BlockSpec((ROWS, D), lambda i: (i, 0))
    vec = pl.BlockSpec((1, D), lambda i: (0, 0))
    return pl.pallas_call(
        kern, name="loss_head", grid=(T // ROWS,),
        in_specs=[row, row, vec],
        out_specs=[row, row, vec, pl.BlockSpec((1, 128), lambda i: (0, 0))],
        out_shape=[jax.ShapeDtypeStruct((T, D), F32), jax.ShapeDtypeStruct((T, D), BF16),
                   jax.ShapeDtypeStruct((1, D), F32), jax.ShapeDtypeStruct((1, 128), F32)],
        compiler_params=_params("arbitrary"),
    )(x3, tgt, g)


def _head_seg_matrix(width):
    seg = np.arange(width) // HEAD_DIM
    return jnp.asarray(seg[:, None] == seg[None, :], dtype=BF16)


def _head_sum(v, seg):
    hi = v.astype(BF16)
    lo = (v - hi.astype(F32)).astype(BF16)
    return _dot(hi, seg, NN) + _dot(lo, seg, NN)


def _branch_weights(l1, l2, l3):
    lm = jnp.maximum(jnp.maximum(l1, l2), l3)
    e1, e2, e3 = jnp.exp(l1 - lm), jnp.exp(l2 - lm), jnp.exp(l3 - lm)
    inv = 1.0 / (e1 + e2 + e3)
    return e1 * inv, e2 * inv, e3 * inv


def _mix_fwd(oa, obs, lbs, ga, gb):
    T, W = oa.shape

    def kern(oa_ref, o1, o2, o3, l1, l2, l3, ga_ref, gb_ref, mix_ref):
        w1, w2, w3 = _branch_weights(l1[...], l2[...], l3[...])
        ob = w1 * o1[...] + w2 * o2[...] + w3 * o3[...]
        oav = oa_ref[...]
        mix_ref[:, :W] = ((oav * _rstd(oav)) * ga_ref[...]).astype(BF16)
        mix_ref[:, W:] = ((ob * _rstd(ob)) * gb_ref[...]).astype(BF16)

    row = pl.BlockSpec((MIX_ROWS, W), lambda i: (i, 0))
    vec = pl.BlockSpec((1, W), lambda i: (0, 0))
    return pl.pallas_call(
        kern, name="mix_fwd", grid=(T // MIX_ROWS,),
        in_specs=[row] * 7 + [vec, vec],
        out_specs=pl.BlockSpec((MIX_ROWS, 2 * W), lambda i: (i, 0)),
        out_shape=jax.ShapeDtypeStruct((T, 2 * W), BF16),
        compiler_params=_params("parallel"),
    )(oa, *obs, *lbs, ga, gb)


def _mix_bwd(dmix, oa, obs, lbs, ga, gb):
    T, W = oa.shape
    seg = _head_seg_matrix(W)

    def kern(dm_ref, oa_ref, o1, o2, o3, l1, l2, l3, ga_ref, gb_ref, seg_ref,
             doa_ref, da_ref, do1, do2, do3, d1, d2, d3, dga_ref, dgb_ref):
        @pl.when(pl.program_id(0) == 0)
        def _():
            dga_ref[...] = jnp.zeros_like(dga_ref)
            dgb_ref[...] = jnp.zeros_like(dgb_ref)

        segv = seg_ref[...]
        oav = oa_ref[...]
        r = _rstd(oav)
        on = oav * r
        dy = dm_ref[:, :W]
        dga_ref[...] += jnp.sum(dy * on, axis=0, keepdims=True)
        t = dy * ga_ref[...]
        doa = r * (t - on * jnp.mean(t * on, axis=-1, keepdims=True))
        doa_ref[...] = doa.astype(BF16)
        da_ref[...] = _head_sum(doa * oav, segv)
        w1, w2, w3 = _branch_weights(l1[...], l2[...], l3[...])
        ob = w1 * o1[...] + w2 * o2[...] + w3 * o3[...]
        r = _rstd(ob)
        on = ob * r
        dy = dm_ref[:, W:]
        dgb_ref[...] += jnp.sum(dy * on, axis=0, keepdims=True)
        t = dy * gb_ref[...]
        dob = r * (t - on * jnp.mean(t * on, axis=-1, keepdims=True))
        c = _head_sum(dob * ob, segv)
        do1[...] = (w1 * dob).astype(BF16)
        do2[...] = (w2 * dob).astype(BF16)
        do3[...] = (w3 * dob).astype(BF16)
        d1[...] = w1 * c
        d2[...] = w2 * c
        d3[...] = w3 * c

    row = pl.BlockSpec((MIX_ROWS, W), lambda i: (i, 0))
    vec = pl.BlockSpec((1, W), lambda i: (0, 0))
    bf = jax.ShapeDtypeStruct((T, W), BF16)
    ff = jax.ShapeDtypeStruct((T, W), F32)
    vv = jax.ShapeDtypeStruct((1, W), F32)
    return pl.pallas_call(
        kern, name="mix_bwd", grid=(T // MIX_ROWS,),
        in_specs=[pl.BlockSpec((MIX_ROWS, 2 * W), lambda i: (i, 0))] + [row] * 7 + [vec, vec,
                  pl.BlockSpec((W, W), lambda i: (0, 0))],
        out_specs=[row] * 8 + [vec, vec],
        out_shape=[bf, ff, bf, bf, bf, ff, ff, ff, vv, vv],
        compiler_params=_params("arbitrary"),
    )(dmix, oa, *obs, *lbs, ga, gb, seg)


def _alibi_slopes(n):
    return np.asarray(2.0 ** (-8.0 * (np.arange(n) + 1) / n)).astype(np.float32)


def _band_bias(max_steps, step_dist):
    qi = np.arange(BLOCK)[:, None]
    kj = np.arange(BLOCK)[None, :]
    slopes = _alibi_slopes(N_HEADS)
    out = []
    for steps in (qi - kj, qi + BLOCK - kj):
        valid = (steps >= 0) & (steps <= max_steps)
        alibi = slopes[:, None, None] * (step_dist * steps).astype(np.float32)[None]
        out.append(np.where(valid[None], -alibi, np.float32(NEG_INF)).astype(np.float32))
    return jnp.asarray(np.stack(out))


class _AttnLayout:
    def __init__(self, dil, kv_heads, q_stride, q_off, k_stride, k_off, v_off):
        self.dil = dil
        self.kv_heads = kv_heads
        self.kw = kv_heads * HEAD_DIM
        self.rep = N_HEADS // kv_heads
        self.q_col = lambda r: r * q_stride + q_off
        self.k_col = lambda r: r * k_stride + k_off
        self.v_col = lambda r: r * k_stride + v_off


QW = N_HEADS * HEAD_DIM


def _attn_fwd(proj, bias, sinks, lay, name, exchange=None):
    L = proj.shape[0]
    nb = L // BLOCK
    kw, rep = lay.kw, lay.rep
    use_sinks = sinks is not None
    scale = HEAD_DIM ** -0.5

    ride = _Ride(exchange, 7 if use_sinks else 6, 2, 0)

    def kern(*refs):
        ins, (o_ref, l_ref), _, exrefs = ride.split(refs)
        q_ref, kc_ref, kp_ref, vc_ref, vp_ref, b_ref = ins[:6]
        s_ref = ins[6] if use_sinks else None
        r, i = pl.program_id(0), pl.program_id(1)
        first = i == 0
        ride.around(first & (r == 0), (i == nb - 1) & (r == lay.dil - 1), exrefs,
                    lambda: compute(q_ref, kc_ref, kp_ref, vc_ref, vp_ref, b_ref, s_ref, o_ref, l_ref, first))

    def compute(q_ref, kc_ref, kp_ref, vc_ref, vp_ref, b_ref, s_ref, o_ref, l_ref, first):
        for h in range(N_HEADS):
            hs = slice(h * HEAD_DIM, (h + 1) * HEAD_DIM)
            gs = slice((h // rep) * HEAD_DIM, (h // rep + 1) * HEAD_DIM)
            q = q_ref[:, hs]
            s_c = _dot(q, kc_ref[:, gs], NT) * scale + b_ref[0, h]
            s_p = _dot(q, kp_ref[:, gs], NT) * scale + b_ref[1, h]
            s_p = jnp.where(first, NEG_INF, s_p)
            m = jnp.maximum(jnp.max(s_c, axis=-1, keepdims=True), jnp.max(s_p, axis=-1, keepdims=True))
            if use_sinks:
                sink = s_ref[:, h:h + 1]
                m = jnp.maximum(m, sink)
            p_c = jnp.exp(s_c - m)
            p_p = jnp.exp(s_p - m)
            denom = jnp.sum(p_c, axis=-1, keepdims=True) + jnp.sum(p_p, axis=-1, keepdims=True)
            if use_sinks:
                denom = denom + jnp.exp(sink - m)
            o = _dot(p_c.astype(BF16), vc_ref[:, gs], NN) + _dot(p_p.astype(BF16), vp_ref[:, gs], NN)
            o_ref[:, hs] = o / denom
            l_ref[:, hs] = jnp.broadcast_to(m + jnp.log(denom), (BLOCK, HEAD_DIM))

    prev = lambda i: jnp.maximum(i - 1, 0)
    in_specs = [
        pl.BlockSpec((BLOCK, QW), lambda r, i: (i, lay.q_col(r))),
        pl.BlockSpec((BLOCK, kw), lambda r, i: (i, lay.k_col(r))),
        pl.BlockSpec((BLOCK, kw), lambda r, i: (prev(i), lay.k_col(r))),
        pl.BlockSpec((BLOCK, kw), lambda r, i: (i, lay.v_col(r))),
        pl.BlockSpec((BLOCK, kw), lambda r, i: (prev(i), lay.v_col(r))),
        pl.BlockSpec((2, N_HEADS, BLOCK, BLOCK), lambda r, i: (0, 0, 0, 0)),
    ]
    args = [proj, proj, proj, proj, proj, bias]
    if use_sinks:
        in_specs.append(pl.BlockSpec((1, N_HEADS), lambda r, i: (0, 0)))
        args.append(sinks)
    out_spec = pl.BlockSpec((BLOCK, QW), lambda r, i: (i, r))
    out = jax.ShapeDtypeStruct((L, lay.dil * QW), F32)
    return pl.pallas_call(
        kern, name=name, grid=(lay.dil, nb),
        in_specs=in_specs + ride.in_specs, out_specs=[out_spec, out_spec] + ride.out_specs,
        out_shape=[out, out] + ride.out_shapes, scratch_shapes=ride.scratch, input_output_aliases=ride.aliases,
        compiler_params=_params("arbitrary", "arbitrary"),
    )(*args, *ride.args)


def _attn_bwd(proj, do, lse, dd, bias, sinks, lay, name, exchange=None):
    L = proj.shape[0]
    nb = L // BLOCK
    kw, rep, kvh = lay.kw, lay.rep, lay.kv_heads
    use_sinks = sinks is not None
    scale = HEAD_DIM ** -0.5

    ride = _Ride(exchange, 10 if use_sinks else 9, 4 if use_sinks else 3, 2)

    def kern(*refs):
        ins, outs, (ck_ref, cv_ref), exrefs = ride.split(refs)
        q_ref, kc_ref, kp_ref, vc_ref, vp_ref, do_ref, l_ref, d_ref, b_ref = ins[:9]
        s_ref = ins[9] if use_sinks else None
        dq_ref, dk_ref, dv_ref = outs[:3]
        ds_ref = outs[3] if use_sinks else None
        r = pl.program_id(0)
        i = pl.program_id(1)
        ride.around((i == 0) & (r == 0), (i == nb) & (r == lay.dil - 1), exrefs,
                    lambda: compute(q_ref, kc_ref, kp_ref, vc_ref, vp_ref, do_ref, l_ref, d_ref, b_ref, s_ref,
                                    dq_ref, dk_ref, dv_ref, ds_ref, ck_ref, cv_ref, r, i))

    def compute(q_ref, kc_ref, kp_ref, vc_ref, vp_ref, do_ref, l_ref, d_ref, b_ref, s_ref,
                dq_ref, dk_ref, dv_ref, ds_ref, ck_ref, cv_ref, r, i):
        first = i == 0

        @pl.when(first)
        def _():
            ck_ref[...] = jnp.zeros_like(ck_ref)
            cv_ref[...] = jnp.zeros_like(cv_ref)

        if use_sinks:
            @pl.when(first & (r == 0))
            def _():
                ds_ref[...] = jnp.zeros_like(ds_ref)

        @pl.when(i < nb)
        def _():
            for g in range(kvh):
                gs = slice(g * HEAD_DIM, (g + 1) * HEAD_DIM)
                kc, kp, vc, vp = kc_ref[:, gs], kp_ref[:, gs], vc_ref[:, gs], vp_ref[:, gs]
                dkc = jnp.zeros((BLOCK, HEAD_DIM), F32)
                dkp = jnp.zeros((BLOCK, HEAD_DIM), F32)
                dvc = jnp.zeros((BLOCK, HEAD_DIM), F32)
                dvp = jnp.zeros((BLOCK, HEAD_DIM), F32)
                for h in range(g * rep, (g + 1) * rep):
                    hs = slice(h * HEAD_DIM, (h + 1) * HEAD_DIM)
                    q = q_ref[:, hs]
                    dov = do_ref[:, hs]
                    lcol = l_ref[:, h * HEAD_DIM:h * HEAD_DIM + 1]
                    dcol = d_ref[:, h * HEAD_DIM:h * HEAD_DIM + 1]
                    s_c = _dot(q, kc, NT) * scale + b_ref[0, h]
                    s_p = _dot(q, kp, NT) * scale + b_ref[1, h]
                    s_p = jnp.where(first, NEG_INF, s_p)
                    p_c = jnp.exp(s_c - lcol)
                    p_p = jnp.exp(s_p - lcol)
                    ds_c = (p_c * (_dot(dov, vc, NT) - dcol) * scale).astype(BF16)
                    ds_p = (p_p * (_dot(dov, vp, NT) - dcol) * scale).astype(BF16)
                    dq_ref[:, hs] = _dot(ds_c, kc, NN) + _dot(ds_p, kp, NN)
                    dkc += _dot(ds_c, q, TN)
                    dkp += _dot(ds_p, q, TN)
                    dvc += _dot(p_c.astype(BF16), dov, TN)
                    dvp += _dot(p_p.astype(BF16), dov, TN)
                    if use_sinks:
                        p_sink = jnp.exp(s_ref[:, h:h + 1] - lcol)
                        ds_ref[:, h:h + 1] += -jnp.sum(p_sink * dcol, axis=0, keepdims=True)
                dk_ref[:, gs] = ck_ref[:, gs] + dkp
                dv_ref[:, gs] = cv_ref[:, gs] + dvp
                ck_ref[:, gs] = dkc
                cv_ref[:, gs] = dvc

        @pl.when(i == nb)
        def _():
            dk_ref[...] = ck_ref[...]
            dv_ref[...] = cv_ref[...]

    cur = lambda i: jnp.minimum(i, nb - 1)
    prev = lambda i: jnp.maximum(jnp.minimum(i, nb - 1) - 1, 0)
    done = lambda i: jnp.maximum(i - 1, 0)
    qspec = lambda col: pl.BlockSpec((BLOCK, QW), lambda r, i: (cur(i), col(r)))
    in_specs = [
        qspec(lay.q_col),
        pl.BlockSpec((BLOCK, kw), lambda r, i: (cur(i), lay.k_col(r))),
        pl.BlockSpec((BLOCK, kw), lambda r, i: (prev(i), lay.k_col(r))),
        pl.BlockSpec((BLOCK, kw), lambda r, i: (cur(i), lay.v_col(r))),
        pl.BlockSpec((BLOCK, kw), lambda r, i: (prev(i), lay.v_col(r))),
        qspec(lambda r: r), qspec(lambda r: r), qspec(lambda r: r),
        pl.BlockSpec((2, N_HEADS, BLOCK, BLOCK), lambda r, i: (0, 0, 0, 0)),
    ]
    args = [proj, proj, proj, proj, proj, do, lse, dd, bias]
    out_specs = [
        qspec(lambda r: r),
        pl.BlockSpec((BLOCK, kw), lambda r, i: (done(i), r)),
        pl.BlockSpec((BLOCK, kw), lambda r, i: (done(i), r)),
    ]
    dkv_shape = jax.ShapeDtypeStruct((L, lay.dil * kw), F32)
    out_shape = [jax.ShapeDtypeStruct((L, lay.dil * QW), F32), dkv_shape, dkv_shape]
    if use_sinks:
        in_specs.append(pl.BlockSpec((1, N_HEADS), lambda r, i: (0, 0)))
        args.append(sinks)
        out_specs.append(pl.BlockSpec((1, N_HEADS), lambda r, i: (0, 0)))
        out_shape.append(jax.ShapeDtypeStruct((1, N_HEADS), F32))
    return pl.pallas_call(
        kern, name=name, grid=(lay.dil, nb + 1),
        in_specs=in_specs + ride.in_specs, out_specs=out_specs + ride.out_specs,
        out_shape=out_shape + ride.out_shapes,
        scratch_shapes=[pltpu.VMEM((BLOCK, kw), F32), pltpu.VMEM((BLOCK, kw), F32)] + ride.scratch,
        input_output_aliases=ride.aliases,
        compiler_params=_params("arbitrary", "arbitrary"),
    )(*args, *ride.args)


def _assemble(groups, name):
    T = groups[0][0].shape[0]
    widths = [g[0].shape[1] for g in groups]
    total = sum(widths)
    flat = [a for g in groups for a in g]

    def kern(*refs):
        ins = refs[:len(flat)]
        out_ref, cs_ref = refs[len(flat):]

        @pl.when(pl.program_id(0) == 0)
        def _():
            cs_ref[...] = jnp.zeros_like(cs_ref)

        pos = off = 0
        for g, w in zip(groups, widths):
            acc = ins[pos][...]
            for j in range(1, len(g)):
                acc = acc + ins[pos + j][...]
            pos += len(g)
            out_ref[:, off:off + w] = acc.astype(BF16)
            cs_ref[:, off:off + w] += jnp.sum(acc, axis=0, keepdims=True)
            off += w

    return pl.pallas_call(
        kern, name=name, grid=(T // ROWS,),
        in_specs=[pl.BlockSpec((ROWS, a.shape[1]), lambda i: (i, 0)) for a in flat],
        out_specs=[pl.BlockSpec((ROWS, total), lambda i: (i, 0)), pl.BlockSpec((1, total), lambda i: (0, 0))],
        out_shape=[jax.ShapeDtypeStruct((T, total), BF16), jax.ShapeDtypeStruct((1, total), F32)],
        compiler_params=_params("arbitrary"),
    )(*flat)


def _adamw(w, g, m, v, name):
    R, C = w.shape
    rows = min(R, ROWS)
    assert R % rows == 0

    def kern(w_ref, g_ref, m_ref, v_ref, d_ref, nm_ref, nv_ref):
        gv = g_ref[...]
        mn = ADAM_B1 * m_ref[...] + (1.0 - ADAM_B1) * gv
        vn = ADAM_B2 * v_ref[...] + (1.0 - ADAM_B2) * jnp.square(gv)
        m_hat = mn / (1.0 - ADAM_B1 ** ADAM_STEP)
        v_hat = vn / (1.0 - ADAM_B2 ** ADAM_STEP)
        d_ref[...] = -ADAM_LR * (m_hat / (jnp.sqrt(v_hat) + ADAM_EPS) + ADAM_WD * w_ref[...])
        nm_ref[...] = mn
        nv_ref[...] = vn

    blk = pl.BlockSpec((rows, C), lambda i: (i, 0))
    shp = jax.ShapeDtypeStruct((R, C), F32)
    return pl.pallas_call(
        kern, name=name, grid=(R // rows,),
        in_specs=[blk] * 4, out_specs=[blk] * 3, out_shape=[shp] * 3,
        compiler_params=_params("parallel"),
    )(w, g, m, v)


SUM_ROWS = 32


def _sum_slots(slots, name):
    n, R, C = slots.shape
    assert R % SUM_ROWS == 0

    def kern(s_ref, o_ref):
        acc = s_ref[0].astype(F32)
        for k in range(1, n):
            acc = acc + s_ref[k].astype(F32)
        o_ref[...] = acc

    return pl.pallas_call(
        kern, name=name, grid=(R // SUM_ROWS,),
        in_specs=[pl.BlockSpec((n, SUM_ROWS, C), lambda i: (0, i, 0))],
        out_specs=pl.BlockSpec((SUM_ROWS, C), lambda i: (i, 0)),
        out_shape=jax.ShapeDtypeStruct((R, C), F32),
        compiler_params=_params("parallel"),
    )(slots)


def _place():
    return lax.axis_index("x"), lax.axis_index("y"), lax.axis_index("c")


def _index(p):
    return 4 * p[0] + 2 * p[1] + p[2]


FLIPS = [(fx, fy, fc) for fx in (0, 1) for fy in (0, 1) for fc in (0, 1)][1:]


def _peer(me, flip):
    return tuple(1 - a if f else a for a, f in zip(me, flip))


def _gather_rows(shards, name):
    nw = len(shards)

    def body(*refs):
        ins, outs = refs[:nw], refs[nw:2 * nw]
        send_sems, recv_sems, local_sems = refs[2 * nw:]
        x, y, c = me = _place()
        sibling = (x, y, 1 - c)
        chips = [(1 - x, y), (x, 1 - y), (1 - x, 1 - y)]

        def rows(w, p):
            n = ins[w].shape[0]
            return outs[w].at[pl.ds(_index(p) * n, n), :]

        def copy(w, k, block, to, src=None):
            return pltpu.make_async_remote_copy(
                src_ref=rows(w, block) if src is None else src, dst_ref=rows(w, block),
                send_sem=send_sems.at[7 * w + k], recv_sem=recv_sems.at[7 * w + k],
                device_id=to, device_id_type=MESH)

        mine = [pltpu.make_async_copy(ins[w], rows(w, me), local_sems.at[w]) for w in range(nw)]
        for cp in mine:
            cp.start()
        first = []
        for w in range(nw):
            first.append(copy(w, 0, me, sibling, src=ins[w]))
            first += [copy(w, 1 + j, me, (*chip, c), src=ins[w]) for j, chip in enumerate(chips)]
        for cp in first:
            cp.start()
        passed = []
        for w in range(nw):
            for j, chip in enumerate(chips):
                copy(w, 1 + j, (*chip, c), me).wait_recv()
                passed.append(copy(w, 4 + j, (*chip, c), sibling))
                passed[-1].start()
        for w in range(nw):
            copy(w, 0, sibling, me).wait_recv()
            for j, chip in enumerate(chips):
                copy(w, 4 + j, (*chip, 1 - c), me).wait_recv()
        for cp in first + passed:
            cp.wait_send()
        for cp in mine:
            cp.wait()

    return pl.pallas_call(
        body, name=name,
        in_specs=[ANY] * nw, out_specs=[ANY] * nw,
        out_shape=[jax.ShapeDtypeStruct((N_DEV * s.shape[0], s.shape[1]), s.dtype) for s in shards],
        scratch_shapes=[pltpu.SemaphoreType.DMA((7 * nw,)), pltpu.SemaphoreType.DMA((7 * nw,)),
                        pltpu.SemaphoreType.DMA((nw,))],
    )(*shards)


def _gather_to_same_core(shards):
    nw = len(shards)

    def copies(ins, outs, send_sems, recv_sems, local_sems):
        x, y, c = me = _place()
        targets = [(x, y, 1 - c), (1 - x, y, c), (x, 1 - y, c), (1 - x, 1 - y, c)]

        def rows(w, p):
            n = ins[w].shape[0]
            return outs[w].at[pl.ds(_index(p) * n, n), :]

        def copy(w, k, block, to):
            return pltpu.make_async_remote_copy(
                src_ref=ins[w], dst_ref=rows(w, block), send_sem=send_sems.at[4 * w + k],
                recv_sem=recv_sems.at[4 * w + k], device_id=to, device_id_type=MESH)

        local = [pltpu.make_async_copy(ins[w], rows(w, me), local_sems.at[w]) for w in range(nw)]
        sends = [copy(w, k, me, to) for w in range(nw) for k, to in enumerate(targets)]
        recvs = [copy(w, k, frm, me) for w in range(nw) for k, frm in enumerate(targets)]
        return local, sends, recvs

    shapes = [jax.ShapeDtypeStruct((N_DEV * s.shape[0], s.shape[1]), s.dtype) for s in shards]
    return _Exchange(shards, shapes, 4 * nw, nw, copies)


def _gather_pass_on(partials):
    nw = len(partials)

    def copies(ins, outs, send_sems, recv_sems, local_sems):
        x, y, c = _place()
        sibling = (x, y, 1 - c)
        chips = [(1 - x, y), (x, 1 - y), (1 - x, 1 - y)]

        def rows(ref, w, p):
            n = ins[w].shape[0] // N_DEV
            return ref[w].at[pl.ds(_index(p) * n, n), :]

        def copy(w, j, core):
            block = (*chips[j], core)
            return pltpu.make_async_remote_copy(
                src_ref=rows(ins, w, block), dst_ref=rows(outs, w, block), send_sem=send_sems.at[3 * w + j],
                recv_sem=recv_sems.at[3 * w + j], device_id=sibling, device_id_type=MESH)

        sends = [copy(w, j, c) for w in range(nw) for j in range(3)]
        recvs = [copy(w, j, 1 - c) for w in range(nw) for j in range(3)]
        return [], sends, recvs

    shapes = [jax.ShapeDtypeStruct(p.shape, p.dtype) for p in partials]
    return _Exchange(partials, shapes, 3 * nw, 0, copies, aliases={w: w for w in range(nw)})


def _scatter_rows(parts):
    nw = len(parts)

    def copies(ins, outs, send_sems, recv_sems, local_sems):
        me = _place()

        def src(w, owner):
            n = ins[w].shape[0] // N_DEV
            return ins[w].at[pl.ds(_index(owner) * n, n), :]

        def copy(k, w, owner, sender, to):
            return pltpu.make_async_remote_copy(
                src_ref=src(w, owner), dst_ref=outs[w].at[_index(sender)],
                send_sem=send_sems.at[nw * k + w], recv_sem=recv_sems.at[nw * k + w],
                device_id=to, device_id_type=MESH)

        local = [pltpu.make_async_copy(src(w, me), outs[w].at[_index(me)], local_sems.at[w]) for w in range(nw)]
        peers = [_peer(me, flip) for flip in FLIPS]
        sends = [copy(k, w, peer, me, peer) for k, peer in enumerate(peers) for w in range(nw)]
        recvs = [copy(k, w, me, peer, me) for k, peer in enumerate(peers) for w in range(nw)]
        return local, sends, recvs

    shapes = [jax.ShapeDtypeStruct((N_DEV, p.shape[0] // N_DEV, p.shape[1]), p.dtype) for p in parts]
    return _Exchange(parts, shapes, 7 * nw, nw, copies)


def _sum_over_devices(v):
    shape = v.shape

    def body(v_ref, sum_ref, all_ref, send_sems, recv_sems):
        me = _place()
        all_ref[_index(me)] = v_ref[...]
        sends = []
        for k, flip in enumerate(FLIPS):
            peer = _peer(me, flip)
            sends.append(pltpu.make_async_remote_copy(
                src_ref=v_ref, dst_ref=all_ref.at[_index(me)],
                send_sem=send_sems.at[k], recv_sem=recv_sems.at[k], device_id=peer, device_id_type=MESH))
            sends[-1].start()
        for k, flip in enumerate(FLIPS):
            peer = _peer(me, flip)
            pltpu.make_async_remote_copy(
                src_ref=v_ref, dst_ref=all_ref.at[_index(peer)],
                send_sem=send_sems.at[k], recv_sem=recv_sems.at[k], device_id=peer, device_id_type=MESH).wait_recv()
        for cp in sends:
            cp.wait_send()
        acc = all_ref[0]
        for s in range(1, N_DEV):
            acc = acc + all_ref[s]
        sum_ref[...] = acc

    vmem = pl.BlockSpec(memory_space=pltpu.VMEM)
    return pl.pallas_call(
        body, name="sum_small_grads",
        in_specs=[vmem], out_specs=[vmem, vmem],
        out_shape=[jax.ShapeDtypeStruct(shape, F32), jax.ShapeDtypeStruct((N_DEV,) + shape, F32)],
        scratch_shapes=[pltpu.SemaphoreType.DMA((7,)), pltpu.SemaphoreType.DMA((7,))],
    )(v)[0]


SMALL_ROWS = 8


def _pack_small(vectors):
    padded = []
    for vec in vectors:
        vec = vec.reshape(-1)
        padded.append(jnp.pad(vec, (0, -vec.shape[0] % 128)))
    flat = jnp.concatenate(padded)
    flat = jnp.pad(flat, (0, -flat.shape[0] % (SMALL_ROWS * 128)))
    return flat.reshape(SMALL_ROWS, -1)


def _unpack_small(packed, shapes):
    flat = packed.reshape(-1)
    out, off = [], 0
    for shp in shapes:
        n = int(np.prod(shp))
        out.append(flat[off:off + n].reshape(shp))
        off += n + (-n % 128)
    return out


def kernel(x, g_attn, w_in, b_in, sinks_a, g_out_a, g_out_b, w_out, g_mlp, w_1, w_2, g_final, loss_target, m_g_attn, m_w_in, m_b_in, m_sinks_a, m_g_out_a, m_g_out_b, m_w_out, m_g_mlp, m_w_1, m_w_2, m_g_final, v_g_attn, v_w_in, v_b_in, v_sinks_a, v_g_out_a, v_g_out_b, v_w_out, v_g_mlp, v_w_1, v_w_2, v_g_final):
    xs, tgt = x[0], loss_target[0]
    T, D = xs.shape
    n_a = QW + 2 * KV_HEADS_A * HEAD_DIM
    g_fin = g_final.reshape(1, D)

    shards = [w_in[0].T.astype(BF16), w_out[0].astype(BF16), w_1[0].T.astype(BF16), w_2[0].astype(BF16)]
    w_in_t, = _gather_rows(shards[:1], "gather_w_in")
    w_in_ta, w_in_tb = w_in_t[:n_a], w_in_t[n_a:]

    ident = lambda acc: (acc,)
    add = lambda acc, other: (acc + other,)
    tiles = dict(tm=512, tn=1024)

    h1 = _norm_fwd(xs, g_attn, "norm_attn")
    proj_a, = _matmul(h1, w_in_ta, "nt", [BF16], add, tm=512, tn=n_a, tk=D, row_ins=[b_in[:, :n_a]], name="proj_a")
    proj_b, = _matmul(h1, w_in_tb, "nt", [BF16], add, tk=D, row_ins=[b_in[:, n_a:]], name="proj_b", **tiles)

    lay_a = _AttnLayout(1, KV_HEADS_A, 0, 0, 0, QW // (KV_HEADS_A * HEAD_DIM), QW // (KV_HEADS_A * HEAD_DIM) + 1)
    bias_a = _band_bias(WINDOW_A - 1, 1)
    o_a, l_a, w_o, w_1_t = _attn_fwd(proj_a, bias_a, sinks_a, lay_a, "attn_a_fwd",
                                     exchange=_gather_to_same_core(shards[1:3]))
    branches = []
    for n, (window, dil) in enumerate(DILATED_BRANCHES):
        lay = _AttnLayout(dil, N_HEADS, 3, 0, 3, 1, 2)
        bias = _band_bias(window // dil, dil)
        view = proj_b.reshape(T // dil, dil * 3 * QW)
        ride = None
        if n == 0:
            ride = _gather_to_same_core(shards[3:])
        elif n == 1:
            ride = _gather_pass_on([w_o, w_1_t, w_2_f])
        o, l, *got = _attn_fwd(view, bias, None, lay, f"attn_b{dil}_fwd", exchange=ride)
        if n == 0:
            w_2_f, = got
        elif n == 1:
            w_o, w_1_t, w_2_f = got
        branches.append((lay, bias, view, o.reshape(T, QW), l.reshape(T, QW)))
    o_b = [br[3] for br in branches]
    l_b = [br[4] for br in branches]

    mix = _mix_fwd(o_a, o_b, l_b, g_out_a, g_out_b)
    x2, = _matmul(mix, w_o, "nn", [F32], add, tk=D, tile_ins=[xs], name="out_proj", **tiles)
    h2 = _norm_fwd(x2, g_mlp, "norm_mlp")

    def relu_sq(acc):
        u = jnp.maximum(acc, 0.0)
        return u, u * u

    u, u_sq = _matmul(h2, w_1_t, "nt", [BF16, BF16], relu_sq, tk=D, name="mlp_up", **tiles)
    x3, = _matmul(u_sq, w_2_f, "nn", [F32], add, tk=2048, tile_ins=[x2], name="mlp_down", **tiles)

    dx3, dx3_b, dg_final, loss_dev = _loss_head(x3, tgt, g_fin)

    d_pre, = _matmul(dx3_b, w_2_f, "nt", [BF16], lambda acc, uu: (acc * (2.0 * uu.astype(F32)),),
                     tk=D, tile_ins=[u], name="mlp_down_bwd", **tiles)
    wtiles = dict(tm=1024, tn=1024, tk=1024)
    dw_2, = _matmul(u_sq, dx3_b, "tn", [BF16], ident, name="mlp_down_wgrad", **wtiles)
    dh2, slots_2 = _matmul(d_pre, w_1_t, "nn", [F32], ident, tk=2048, name="mlp_up_bwd", exchange=_scatter_rows([dw_2]),
                           **tiles)
    dw_1_t, = _matmul(d_pre, h2, "tn", [BF16], ident, name="mlp_up_wgrad", **wtiles)
    dx2, dx2_b, dg_mlp = _norm_bwd(dh2, x2, g_mlp, dx3, "norm_mlp_bwd")

    dmix, = _matmul(dx2_b, w_o, "nt", [F32], ident, tk=D, name="out_proj_bwd", **tiles)
    dw_o, = _matmul(mix, dx2_b, "tn", [BF16], ident, name="out_proj_wgrad", **wtiles)
    do_a, dd_a, do1, do2, do3, dd1, dd2, dd3, dg_out_a, dg_out_b = _mix_bwd(dmix, o_a, o_b, l_b, g_out_a, g_out_b)

    dq_a, dk_a, dv_a, dsinks, slots_1 = _attn_bwd(proj_a, do_a, l_a, dd_a, bias_a, sinks_a, lay_a, "attn_a_bwd",
                                                  exchange=_scatter_rows([dw_1_t]))
    dqs, dks, dvs = [], [], []
    for (lay, bias, view, _, l), do_n, dd_n in zip(branches, (do1, do2, do3), (dd1, dd2, dd3)):
        shape = (T // lay.dil, lay.dil * QW)
        ride = _scatter_rows([dw_o]) if lay.dil == 1 else None
        dq, dk, dv, *got = _attn_bwd(view, do_n.reshape(shape), l.reshape(shape), dd_n.reshape(shape), bias, None, lay,
                                     f"attn_b{lay.dil}_bwd", exchange=ride)
        if ride:
            slots_o, = got
        dqs.append(dq.reshape(T, QW))
        dks.append(dk.reshape(T, QW))
        dvs.append(dv.reshape(T, QW))
    dproj_a, db_a = _assemble([[dq_a], [dk_a], [dv_a]], "dproj_a")
    dproj_b, db_b = _assemble([dqs, dks, dvs], "dproj_b")

    dh1_a, = _matmul(dproj_a, w_in_ta, "nn", [F32], ident, tk=n_a, name="in_proj_a_bwd", **tiles)
    dh1, = _matmul(dproj_b, w_in_tb, "nn", [F32], add, tk=3 * QW, tile_ins=[dh1_a], name="in_proj_b_bwd", **tiles)
    dw_in_ta, = _matmul(dproj_a, h1, "tn", [BF16], ident, tm=n_a, tn=1024, tk=512, name="in_proj_a_wgrad")
    dw_in_tb, = _matmul(dproj_b, h1, "tn", [BF16], ident, name="in_proj_b_wgrad", **wtiles)
    dw_in_t = jnp.concatenate([dw_in_ta, dw_in_tb], axis=0)
    dx, _, dg_attn = _norm_bwd(dh1, xs, g_attn, dx2, "norm_attn_bwd")

    slots_in, = _run_exchange(_scatter_rows([dw_in_t]), "scatter_w_in_grads")
    g_w_in = _sum_slots(slots_in, "sum_w_in_grads").T
    g_w_out = _sum_slots(slots_o, "sum_w_out_grads")
    g_w_1 = _sum_slots(slots_1, "sum_w_1_grads").T
    g_w_2 = _sum_slots(slots_2, "sum_w_2_grads")

    small_w = [g_attn, b_in, sinks_a, g_out_a, g_out_b, g_mlp, g_final]
    small_m = [m_g_attn, m_b_in, m_sinks_a, m_g_out_a, m_g_out_b, m_g_mlp, m_g_final]
    small_v = [v_g_attn, v_b_in, v_sinks_a, v_g_out_a, v_g_out_b, v_g_mlp, v_g_final]
    small_g = [dg_attn, jnp.concatenate([db_a, db_b], axis=1), dsinks, dg_out_a, dg_out_b, dg_mlp, dg_final]
    summed = _sum_over_devices(_pack_small(small_g + [loss_dev[:, :1]]))
    shapes = [w.shape for w in small_w]
    *g_small, loss = _unpack_small(summed, shapes + [()])

    big = [
        _adamw(w_in[0], g_w_in, m_w_in[0], v_w_in[0], "adamw_w_in"),
        _adamw(w_out[0], g_w_out, m_w_out[0], v_w_out[0], "adamw_w_out"),
        _adamw(w_1[0], g_w_1, m_w_1[0], v_w_1[0], "adamw_w_1"),
        _adamw(w_2[0], g_w_2, m_w_2[0], v_w_2[0], "adamw_w_2"),
    ]
    g_packed = _pack_small(g_small)
    small = _adamw(_pack_small(small_w), g_packed, _pack_small(small_m), _pack_small(small_v), "adamw_small")
    small = [_unpack_small(s, shapes) for s in small]

    def ordered(small_list, big_list):
        s = list(small_list)
        return [s[0], big_list[0][None], s[1], s[2], s[3], s[4], big_list[1][None], s[5],
                big_list[2][None], big_list[3][None], s[6]]

    grads = ordered(g_small, [g_w_in, g_w_out, g_w_1, g_w_2])
    deltas = ordered(small[0], [b[0] for b in big])
    new_m = ordered(small[1], [b[1] for b in big])
    new_v = ordered(small[2], [b[2] for b in big])
    return (loss, dx[None], *grads, *deltas, *new_m, *new_v)
```

```python
import numpy as np
import jax
import jax.numpy as jnp
from jax import lax
from jax.experimental import pallas as pl
from jax.experimental.pallas import tpu as pltpu

F32 = jnp.float32
BF16 = jnp.bfloat16

HEAD_DIM = 64
N_HEADS = 16
KV_HEADS_A = 2
BLOCK = 128
WINDOW_A = 128
DILATED_BRANCHES = ((128, 1), (512, 4), (2048, 16))
EPS = 1e-5
NEG_INF = -1e30
N_DEV = 8

ADAM_LR = 0.001
ADAM_B1 = 0.9
ADAM_B2 = 0.999
ADAM_EPS = 1e-08
ADAM_WD = 0.01
ADAM_STEP = 10

VMEM_LIMIT_BYTES = 56 * 1024 * 1024
MESH = pl.DeviceIdType.MESH
ANY = pl.BlockSpec(memory_space=pl.ANY)

NN = (((1,), (0,)), ((), ()))
NT = (((1,), (1,)), ((), ()))
TN = (((0,), (0,)), ((), ()))


def _dot(a, b, dims):
    return lax.dot_general(a, b, dims, preferred_element_type=F32)


def _params(*sem):
    return pltpu.CompilerParams(dimension_semantics=sem, vmem_limit_bytes=VMEM_LIMIT_BYTES)


class _Exchange:
    def __init__(self, ins, out_shapes, n_remote, n_local, copies, aliases=None):
        self.ins, self.out_shapes = list(ins), list(out_shapes)
        self.n_remote, self.n_local = n_remote, n_local
        self.copies = copies
        self.aliases = aliases or {}

    def start(self, refs):
        local, sends, _ = self.copies(*refs)
        for cp in local + sends:
            cp.start()

    def finish(self, refs):
        local, sends, recvs = self.copies(*refs)
        for cp in recvs:
            cp.wait_recv()
        for cp in sends:
            cp.wait_send()
        for cp in local:
            cp.wait()


class _Ride:
    def __init__(self, ex, n_in, n_out, n_scratch):
        self.ex = ex
        self.n = (n_in, n_out, n_scratch)
        self.args = ex.ins if ex else []
        self.in_specs = [ANY] * len(self.args)
        self.out_shapes = ex.out_shapes if ex else []
        self.out_specs = [ANY] * len(self.out_shapes)
        self.scratch = [pltpu.SemaphoreType.DMA((ex.n_remote,)), pltpu.SemaphoreType.DMA((ex.n_remote,)),
                        pltpu.SemaphoreType.DMA((max(ex.n_local, 1),))] if ex else []
        self.aliases = {n_in + i: n_out + o for i, o in ex.aliases.items()} if ex else {}

    def split(self, refs):
        n_in, n_out, n_scratch = self.n
        a = n_in
        b = a + len(self.args)
        c = b + n_out
        d = c + len(self.out_shapes)
        e = d + n_scratch
        return refs[:a], refs[b:c], refs[d:e], (refs[a:b], refs[c:d], *refs[e:])

    def around(self, first, last, exrefs, compute):
        if self.ex is None:
            compute()
            return

        @pl.when(first)
        def _():
            self.ex.start(exrefs)

        compute()

        @pl.when(last)
        def _():
            self.ex.finish(exrefs)


def _run_exchange(ex, name):
    ride = _Ride(ex, 0, 0, 0)

    def body(*refs):
        exrefs = ride.split(refs)[3]
        ex.start(exrefs)
        ex.finish(exrefs)

    return pl.pallas_call(
        body, name=name, in_specs=ride.in_specs, out_specs=ride.out_specs, out_shape=ride.out_shapes,
        scratch_shapes=ride.scratch, input_output_aliases=ride.aliases,
    )(*ride.args)


def _matmul(a, b, dims, out_dtypes, epilogue, *, tm, tn, tk, name, tile_ins=(), row_ins=(), exchange=None):
    if dims == "tn":
        K, M = a.shape
    else:
        M, K = a.shape
    N = b.shape[0] if dims == "nt" else b.shape[1]
    tm, tn, tk = min(tm, M), min(tn, N), min(tk, K)
    assert M % tm == 0 and N % tn == 0 and K % tk == 0, (name, M, N, K, tm, tn, tk)
    grid = (M // tm, N // tn, K // tk)
    nk = grid[2]
    n_tile, n_row, n_out = len(tile_ins), len(row_ins), len(out_dtypes)
    dn = {"nn": NN, "nt": NT, "tn": TN}[dims]
    ride = _Ride(exchange, 2 + n_tile + n_row, n_out, 1 if nk > 1 else 0)

    def kern(*refs):
        ins, out_refs, scratch, exrefs = ride.split(refs)
        a_ref, b_ref = ins[:2]
        tile_refs = ins[2:2 + n_tile]
        row_refs = ins[2 + n_tile:]
        ids = [pl.program_id(d) for d in range(3)]

        def finish(acc):
            outs = epilogue(acc, *[r[...] for r in tile_refs], *[r[...] for r in row_refs])
            for o_ref, o in zip(out_refs, outs):
                o_ref[...] = o.astype(o_ref.dtype)

        def compute():
            if nk == 1:
                finish(_dot(a_ref[...], b_ref[...], dn))
                return
            acc_ref = scratch[0]

            @pl.when(ids[2] == 0)
            def _():
                acc_ref[...] = jnp.zeros_like(acc_ref)

            acc_ref[...] += _dot(a_ref[...], b_ref[...], dn)

            @pl.when(ids[2] == nk - 1)
            def _():
                finish(acc_ref[...])

        first = (ids[0] == 0) & (ids[1] == 0) & (ids[2] == 0)
        last = (ids[0] == grid[0] - 1) & (ids[1] == grid[1] - 1) & (ids[2] == grid[2] - 1)
        ride.around(first, last, exrefs, compute)

    if dims == "tn":
        a_spec = pl.BlockSpec((tk, tm), lambda i, j, k: (k, i))
    else:
        a_spec = pl.BlockSpec((tm, tk), lambda i, j, k: (i, k))
    if dims == "nt":
        b_spec = pl.BlockSpec((tn, tk), lambda i, j, k: (j, k))
    else:
        b_spec = pl.BlockSpec((tk, tn), lambda i, j, k: (k, j))
    tile_spec = pl.BlockSpec((tm, tn), lambda i, j, k: (i, j))
    row_spec = pl.BlockSpec((1, tn), lambda i, j, k: (0, j))
    sem = ("arbitrary",) * 3 if exchange else ("parallel", "parallel", "arbitrary")
    return pl.pallas_call(
        kern,
        name=name,
        grid=grid,
        in_specs=[a_spec, b_spec] + [tile_spec] * n_tile + [row_spec] * n_row + ride.in_specs,
        out_specs=[tile_spec] * n_out + ride.out_specs,
        out_shape=[jax.ShapeDtypeStruct((M, N), dt) for dt in out_dtypes] + ride.out_shapes,
        scratch_shapes=([pltpu.VMEM((tm, tn), F32)] if nk > 1 else []) + ride.scratch,
        input_output_aliases=ride.aliases,
        compiler_params=_params(*sem),
    )(a, b, *tile_ins, *row_ins, *ride.args)


ROWS = 256
MIX_ROWS = 128


def _rstd(xv):
    return lax.rsqrt(jnp.mean(xv * xv, axis=-1, keepdims=True) + EPS)


def _norm_fwd(x, g, name):
    T, D = x.shape

    def kern(x_ref, g_ref, h_ref):
        xv = x_ref[...]
        h_ref[...] = ((xv * _rstd(xv)) * g_ref[...]).astype(h_ref.dtype)

    return pl.pallas_call(
        kern, name=name, grid=(T // ROWS,),
        in_specs=[pl.BlockSpec((ROWS, D), lambda i: (i, 0)), pl.BlockSpec((1, D), lambda i: (0, 0))],
        out_specs=pl.BlockSpec((ROWS, D), lambda i: (i, 0)),
        out_shape=jax.ShapeDtypeStruct((T, D), BF16),
        compiler_params=_params("parallel"),
    )(x, g)


def _norm_bwd(dh, x, g, res, name):
    T, D = x.shape

    def kern(dh_ref, x_ref, g_ref, res_ref, dx_ref, dxb_ref, dg_ref):
        @pl.when(pl.program_id(0) == 0)
        def _():
            dg_ref[...] = jnp.zeros_like(dg_ref)

        xv = x_ref[...]
        r = _rstd(xv)
        xn = xv * r
        dhv = dh_ref[...]
        dg_ref[...] += jnp.sum(dhv * xn, axis=0, keepdims=True)
        t = dhv * g_ref[...]
        dx = res_ref[...] + r * (t - xn * jnp.mean(t * xn, axis=-1, keepdims=True))
        dx_ref[...] = dx
        dxb_ref[...] = dx.astype(BF16)

    row = pl.BlockSpec((ROWS, D), lambda i: (i, 0))
    vec = pl.BlockSpec((1, D), lambda i: (0, 0))
    return pl.pallas_call(
        kern, name=name, grid=(T // ROWS,),
        in_specs=[row, row, vec, row],
        out_specs=[row, row, vec],
        out_shape=[jax.ShapeDtypeStruct((T, D), F32), jax.ShapeDtypeStruct((T, D), BF16),
                   jax.ShapeDtypeStruct((1, D), F32)],
        compiler_params=_params("arbitrary"),
    )(dh, x, g, res)


def _loss_head(x3, tgt, g):
    T, D = x3.shape

    def kern(x_ref, t_ref, g_ref, dx_ref, dxb_ref, dg_ref, loss_ref):
        @pl.when(pl.program_id(0) == 0)
        def _():
            dg_ref[...] = jnp.zeros_like(dg_ref)
            loss_ref[...] = jnp.zeros_like(loss_ref)

        xv = x_ref[...]
        gv = g_ref[...]
        r = _rstd(xv)
        xn = xv * r
        err = xn * gv - t_ref[...]
        per_tok = jnp.mean(err * err, axis=-1, keepdims=True)
        loss_ref[...] += 0.5 * jnp.sum(per_tok, axis=0, keepdims=True)
        dy = err * (1.0 / D)
        dg_ref[...] += jnp.sum(dy * xn, axis=0, keepdims=True)
        t = dy * gv
        dx = r * (t - xn * jnp.mean(t * xn, axis=-1, keepdims=True))
        dx_ref[...] = dx
        dxb_ref[...] = dx.astype(BF16)

    row = pl.BlockSpec((ROWS, D), lambda i: (i, 0))
    vec = pl.BlockSpec((1, D), lambda i: (0, 0))
    return pl.pallas_call(
        kern, name="loss_head", grid=(T // ROWS,),
        in_specs=[row, row, vec],
        out_specs=[row, row, vec, pl.BlockSpec((1, 128), lambda i: (0, 0))],
        out_shape=[jax.ShapeDtypeStruct((T, D), F32), jax.ShapeDtypeStruct((T, D), BF16),
                   jax.ShapeDtypeStruct((1, D), F32), jax.ShapeDtypeStruct((1, 128), F32)],
        compiler_params=_params("arbitrary"),
    )(x3, tgt, g)


def _spread_matrix():
    head_of_lane = np.arange(N_HEADS * HEAD_DIM) // HEAD_DIM
    return jnp.asarray(np.arange(N_HEADS)[:, None] == head_of_lane[None, :], dtype=BF16)


def _pieces(v, n):
    out = []
    for _ in range(n):
        piece = v.astype(BF16)
        out.append(piece)
        v = v - piece.astype(F32)
    return out


def _spread(v, spread):
    return sum(_dot(p, spread, NN) for p in _pieces(v, 3))


def _head_sums(v, spread):
    return sum(_dot(p, spread, NT) for p in _pieces(v, 2))


def _branch_weights(l1, l2, l3):
    lm = jnp.maximum(jnp.maximum(l1, l2), l3)
    e1, e2, e3 = jnp.exp(l1 - lm), jnp.exp(l2 - lm), jnp.exp(l3 - lm)
    inv = 1.0 / (e1 + e2 + e3)
    return e1 * inv, e2 * inv, e3 * inv


def _mix_fwd(oa, obs, lbs, ga, gb):
    T, W = oa.shape

    def kern(oa_ref, o1, o2, o3, l1, l2, l3, ga_ref, gb_ref, sp_ref, mix_ref):
        sp = sp_ref[...]
        w1, w2, w3 = _branch_weights(l1[...], l2[...], l3[...])
        ob = _spread(w1, sp) * o1[...] + _spread(w2, sp) * o2[...] + _spread(w3, sp) * o3[...]
        oav = oa_ref[...]
        mix_ref[:, :W] = ((oav * _rstd(oav)) * ga_ref[...]).astype(BF16)
        mix_ref[:, W:] = ((ob * _rstd(ob)) * gb_ref[...]).astype(BF16)

    row = pl.BlockSpec((MIX_ROWS, W), lambda i: (i, 0))
    per_head = pl.BlockSpec((MIX_ROWS, N_HEADS), lambda i: (i, 0))
    vec = pl.BlockSpec((1, W), lambda i: (0, 0))
    return pl.pallas_call(
        kern, name="mix_fwd", grid=(T // MIX_ROWS,),
        in_specs=[row] * 4 + [per_head] * 3 + [vec, vec, pl.BlockSpec((N_HEADS, W), lambda i: (0, 0))],
        out_specs=pl.BlockSpec((MIX_ROWS, 2 * W), lambda i: (i, 0)),
        out_shape=jax.ShapeDtypeStruct((T, 2 * W), BF16),
        compiler_params=_params("parallel"),
    )(oa, *obs, *lbs, ga, gb, _spread_matrix())


def _mix_bwd(dmix, oa, obs, lbs, ga, gb):
    T, W = oa.shape

    def kern(dm_ref, oa_ref, o1, o2, o3, l1, l2, l3, ga_ref, gb_ref, sp_ref,
             doa_ref, da_ref, do1, do2, do3, d1, d2, d3, dga_ref, dgb_ref):
        @pl.when(pl.program_id(0) == 0)
        def _():
            dga_ref[...] = jnp.zeros_like(dga_ref)
            dgb_ref[...] = jnp.zeros_like(dgb_ref)

        sp = sp_ref[...]
        oav = oa_ref[...]
        r = _rstd(oav)
        on = oav * r
        dy = dm_ref[:, :W]
        dga_ref[...] += jnp.sum(dy * on, axis=0, keepdims=True)
        t = dy * ga_ref[...]
        doa = r * (t - on * jnp.mean(t * on, axis=-1, keepdims=True))
        doa_ref[...] = doa.astype(BF16)
        da_ref[...] = _head_sums(doa * oav, sp)
        w1, w2, w3 = _branch_weights(l1[...], l2[...], l3[...])
        s1, s2, s3 = _spread(w1, sp), _spread(w2, sp), _spread(w3, sp)
        ob = s1 * o1[...] + s2 * o2[...] + s3 * o3[...]
        r = _rstd(ob)
        on = ob * r
        dy = dm_ref[:, W:]
        dgb_ref[...] += jnp.sum(dy * on, axis=0, keepdims=True)
        t = dy * gb_ref[...]
        dob = r * (t - on * jnp.mean(t * on, axis=-1, keepdims=True))
        c = _head_sums(dob * ob, sp)
        do1[...] = (s1 * dob).astype(BF16)
        do2[...] = (s2 * dob).astype(BF16)
        do3[...] = (s3 * dob).astype(BF16)
        d1[...] = w1 * c
        d2[...] = w2 * c
        d3[...] = w3 * c

    row = pl.BlockSpec((MIX_ROWS, W), lambda i: (i, 0))
    per_head = pl.BlockSpec((MIX_ROWS, N_HEADS), lambda i: (i, 0))
    vec = pl.BlockSpec((1, W), lambda i: (0, 0))
    bf = jax.ShapeDtypeStruct((T, W), BF16)
    ph = jax.ShapeDtypeStruct((T, N_HEADS), F32)
    vv = jax.ShapeDtypeStruct((1, W), F32)
    return pl.pallas_call(
        kern, name="mix_bwd", grid=(T // MIX_ROWS,),
        in_specs=[pl.BlockSpec((MIX_ROWS, 2 * W), lambda i: (i, 0))] + [row] * 4 + [per_head] * 3 + [vec, vec,
                  pl.BlockSpec((N_HEADS, W), lambda i: (0, 0))],
        out_specs=[row, per_head, row, row, row, per_head, per_head, per_head, vec, vec],
        out_shape=[bf, ph, bf, bf, bf, ph, ph, ph, vv, vv],
        compiler_params=_params("arbitrary"),
    )(dmix, oa, *obs, *lbs, ga, gb, _spread_matrix())


def _alibi_slopes(n):
    return np.asarray(2.0 ** (-8.0 * (np.arange(n) + 1) / n)).astype(np.float32)


def _band_bias(max_steps, step_dist):
    qi = np.arange(BLOCK)[None, :]
    kj = np.arange(BLOCK)[:, None]
    slopes = _alibi_slopes(N_HEADS)
    out = []
    for steps in (qi - kj, qi + BLOCK - kj):
        valid = (steps >= 0) & (steps <= max_steps)
        alibi = slopes[:, None, None] * (step_dist * steps).astype(np.float32)[None]
        out.append(np.where(valid[None], -alibi, np.float32(NEG_INF)).astype(np.float32))
    return jnp.asarray(np.stack(out))


class _AttnLayout:
    def __init__(self, dil, kv_heads, q_stride, q_off, k_stride, k_off, v_off):
        self.dil = dil
        self.kv_heads = kv_heads
        self.kw = kv_heads * HEAD_DIM
        self.rep = N_HEADS // kv_heads
        self.q_col = lambda r: r * q_stride + q_off
        self.k_col = lambda r: r * k_stride + k_off
        self.v_col = lambda r: r * k_stride + v_off


QW = N_HEADS * HEAD_DIM
LANES = 128


def _head_cols(h):
    return slice(h * HEAD_DIM, (h + 1) * HEAD_DIM)


def _attn_fwd(proj, bias, sinks, lay, name, exchange=None):
    L = proj.shape[0]
    nb = L // BLOCK
    kw, rep = lay.kw, lay.rep
    use_sinks = sinks is not None
    scale = HEAD_DIM ** -0.5
    ride = _Ride(exchange, 7 if use_sinks else 6, 2, 0)

    def kern(*refs):
        ins, (o_ref, l_ref), _, exrefs = ride.split(refs)
        q_ref, kc_ref, kp_ref, vc_ref, vp_ref, b_ref = ins[:6]
        s_ref = ins[6] if use_sinks else None
        r, i = pl.program_id(0), pl.program_id(1)
        first = i == 0
        ride.around(first & (r == 0), (i == nb - 1) & (r == lay.dil - 1), exrefs,
                    lambda: compute(q_ref, kc_ref, kp_ref, vc_ref, vp_ref, b_ref, s_ref, o_ref, l_ref, first))

    def compute(q_ref, kc_ref, kp_ref, vc_ref, vp_ref, b_ref, s_ref, o_ref, l_ref, first):
        values_t = {}
        for pair in range(N_HEADS // 2):
            halves = []
            for h in (2 * pair, 2 * pair + 1):
                g = h // rep
                blk = slice(g // 2 * LANES, (g // 2 + 1) * LANES)
                if g // 2 not in values_t:
                    values_t[g // 2] = (vc_ref[:, blk].T, vp_ref[:, blk].T)
                vct, vpt = values_t[g // 2]
                rows = slice(g % 2 * HEAD_DIM, (g % 2 + 1) * HEAD_DIM)
                q = q_ref[:, _head_cols(h)]
                s_c = _dot(kc_ref[:, _head_cols(g)], q, NT) * scale + b_ref[0, h]
                s_p = _dot(kp_ref[:, _head_cols(g)], q, NT) * scale + b_ref[1, h]
                s_p = jnp.where(first, NEG_INF, s_p)
                m = jnp.maximum(jnp.max(s_c, axis=0, keepdims=True), jnp.max(s_p, axis=0, keepdims=True))
                if use_sinks:
                    sink = s_ref[:, h:h + 1]
                    m = jnp.maximum(m, sink)
                p_c = jnp.exp(s_c - m)
                p_p = jnp.exp(s_p - m)
                denom = jnp.sum(p_c, axis=0, keepdims=True) + jnp.sum(p_p, axis=0, keepdims=True)
                if use_sinks:
                    denom = denom + jnp.exp(sink - m)
                o_t = _dot(vct[rows], p_c.astype(BF16), NN) + _dot(vpt[rows], p_p.astype(BF16), NN)
                halves.append(o_t / denom)
                l_ref[h:h + 1, :] = m + jnp.log(denom)
            o_ref[:, pair * LANES:(pair + 1) * LANES] = jnp.concatenate(halves, axis=0).T

    prev = lambda i: jnp.maximum(i - 1, 0)
    in_specs = [
        pl.BlockSpec((BLOCK, QW), lambda r, i: (i, lay.q_col(r))),
        pl.BlockSpec((BLOCK, kw), lambda r, i: (i, lay.k_col(r))),
        pl.BlockSpec((BLOCK, kw), lambda r, i: (prev(i), lay.k_col(r))),
        pl.BlockSpec((BLOCK, kw), lambda r, i: (i, lay.v_col(r))),
        pl.BlockSpec((BLOCK, kw), lambda r, i: (prev(i), lay.v_col(r))),
        pl.BlockSpec((2, N_HEADS, BLOCK, BLOCK), lambda r, i: (0, 0, 0, 0)),
    ]
    args = [proj, proj, proj, proj, proj, bias]
    if use_sinks:
        in_specs.append(pl.BlockSpec((1, N_HEADS), lambda r, i: (0, 0)))
        args.append(sinks)
    out_specs = [pl.BlockSpec((BLOCK, QW), lambda r, i: (i, r)),
                 pl.BlockSpec((None, N_HEADS, BLOCK), lambda r, i: (r, 0, i))]
    out_shape = [jax.ShapeDtypeStruct((L, lay.dil * QW), F32), jax.ShapeDtypeStruct((lay.dil, N_HEADS, L), F32)]
    return pl.pallas_call(
        kern, name=name, grid=(lay.dil, nb),
        in_specs=in_specs + ride.in_specs, out_specs=out_specs + ride.out_specs,
        out_shape=out_shape + ride.out_shapes, scratch_shapes=ride.scratch, input_output_aliases=ride.aliases,
        compiler_params=_params("arbitrary", "arbitrary"),
    )(*args, *ride.args)


def _attn_bwd(proj, do, lse, dd, bias, sinks, lay, name, exchange=None):
    L = proj.shape[0]
    nb = L // BLOCK
    kw, rep, kvh = lay.kw, lay.rep, lay.kv_heads
    use_sinks = sinks is not None
    scale = HEAD_DIM ** -0.5
    ride = _Ride(exchange, 10 if use_sinks else 9, 4 if use_sinks else 3, 2)

    def kern(*refs):
        ins, outs, (ck_ref, cv_ref), exrefs = ride.split(refs)
        q_ref, kc_ref, kp_ref, vc_ref, vp_ref, do_ref, l_ref, d_ref, b_ref = ins[:9]
        s_ref = ins[9] if use_sinks else None
        dq_ref, dk_ref, dv_ref = outs[:3]
        ds_ref = outs[3] if use_sinks else None
        r = pl.program_id(0)
        i = pl.program_id(1)
        ride.around((i == 0) & (r == 0), (i == nb) & (r == lay.dil - 1), exrefs,
                    lambda: compute(q_ref, kc_ref, kp_ref, vc_ref, vp_ref, do_ref, l_ref, d_ref, b_ref, s_ref,
                                    dq_ref, dk_ref, dv_ref, ds_ref, ck_ref, cv_ref, r, i))

    def compute(q_ref, kc_ref, kp_ref, vc_ref, vp_ref, do_ref, l_ref, d_ref, b_ref, s_ref,
                dq_ref, dk_ref, dv_ref, ds_ref, ck_ref, cv_ref, r, i):
        first = i == 0

        @pl.when(first)
        def _():
            ck_ref[...] = jnp.zeros_like(ck_ref)
            cv_ref[...] = jnp.zeros_like(cv_ref)

        if use_sinks:
            @pl.when(first & (r == 0))
            def _():
                ds_ref[...] = jnp.zeros_like(ds_ref)

        @pl.when(i < nb)
        def _():
            dq_halves = []
            keys_t = {}
            for g in range(kvh):
                gs = _head_cols(g)
                blk = slice(g // 2 * LANES, (g // 2 + 1) * LANES)
                rows = slice(g % 2 * HEAD_DIM, (g % 2 + 1) * HEAD_DIM)
                kc, kp, vc, vp = kc_ref[:, gs], kp_ref[:, gs], vc_ref[:, gs], vp_ref[:, gs]
                if g // 2 not in keys_t:
                    keys_t[g // 2] = (kc_ref[:, blk].T, kp_ref[:, blk].T)
                kct, kpt = keys_t[g // 2][0][rows], keys_t[g // 2][1][rows]
                dkc = jnp.zeros((BLOCK, HEAD_DIM), F32)
                dkp = jnp.zeros((BLOCK, HEAD_DIM), F32)
                dvc = jnp.zeros((BLOCK, HEAD_DIM), F32)
                dvp = jnp.zeros((BLOCK, HEAD_DIM), F32)
                for h in range(g * rep, (g + 1) * rep):
                    q = q_ref[:, _head_cols(h)]
                    dov = do_ref[:, _head_cols(h)]
                    lrow = l_ref[h:h + 1, :]
                    drow = d_ref[h:h + 1, :]
                    s_c = _dot(kc, q, NT) * scale + b_ref[0, h]
                    s_p = _dot(kp, q, NT) * scale + b_ref[1, h]
                    s_p = jnp.where(first, NEG_INF, s_p)
                    p_c = jnp.exp(s_c - lrow)
                    p_p = jnp.exp(s_p - lrow)
                    ds_c = (p_c * (_dot(vc, dov, NT) - drow) * scale).astype(BF16)
                    ds_p = (p_p * (_dot(vp, dov, NT) - drow) * scale).astype(BF16)
                    dkc += _dot(ds_c, q, NN)
                    dkp += _dot(ds_p, q, NN)
                    dvc += _dot(p_c.astype(BF16), dov, NN)
                    dvp += _dot(p_p.astype(BF16), dov, NN)
                    dq_halves.append(_dot(kct, ds_c, NN) + _dot(kpt, ds_p, NN))
                    if h % 2:
                        dq_ref[:, (h // 2) * LANES:(h // 2 + 1) * LANES] = jnp.concatenate(dq_halves, axis=0).T
                        dq_halves = []
                    if use_sinks:
                        ds_ref[h:h + 1, :] += -(jnp.exp(s_ref[:, h:h + 1] - lrow) * drow)
                dk_ref[:, gs] = ck_ref[:, gs] + dkp
                dv_ref[:, gs] = cv_ref[:, gs] + dvp
                ck_ref[:, gs] = dkc
                cv_ref[:, gs] = dvc

        @pl.when(i == nb)
        def _():
            dk_ref[...] = ck_ref[...]
            dv_ref[...] = cv_ref[...]
            if use_sinks:
                @pl.when(r == lay.dil - 1)
                def _():
                    ds_ref[...] = jnp.broadcast_to(jnp.sum(ds_ref[...], axis=1, keepdims=True), ds_ref.shape)

    cur = lambda i: jnp.minimum(i, nb - 1)
    prev = lambda i: jnp.maximum(jnp.minimum(i, nb - 1) - 1, 0)
    done = lambda i: jnp.maximum(i - 1, 0)
    qspec = lambda col: pl.BlockSpec((BLOCK, QW), lambda r, i: (cur(i), col(r)))
    per_head = pl.BlockSpec((None, N_HEADS, BLOCK), lambda r, i: (r, 0, cur(i)))
    in_specs = [
        qspec(lay.q_col),
        pl.BlockSpec((BLOCK, kw), lambda r, i: (cur(i), lay.k_col(r))),
        pl.BlockSpec((BLOCK, kw), lambda r, i: (prev(i), lay.k_col(r))),
        pl.BlockSpec((BLOCK, kw), lambda r, i: (cur(i), lay.v_col(r))),
        pl.BlockSpec((BLOCK, kw), lambda r, i: (prev(i), lay.v_col(r))),
        qspec(lambda r: r), per_head, per_head,
        pl.BlockSpec((2, N_HEADS, BLOCK, BLOCK), lambda r, i: (0, 0, 0, 0)),
    ]
    args = [proj, proj, proj, proj, proj, do, lse, dd, bias]
    out_specs = [
        qspec(lambda r: r),
        pl.BlockSpec((BLOCK, kw), lambda r, i: (done(i), r)),
        pl.BlockSpec((BLOCK, kw), lambda r, i: (done(i), r)),
    ]
    dkv_shape = jax.ShapeDtypeStruct((L, lay.dil * kw), F32)
    out_shape = [jax.ShapeDtypeStruct((L, lay.dil * QW), F32), dkv_shape, dkv_shape]
    if use_sinks:
        in_specs.append(pl.BlockSpec((1, N_HEADS), lambda r, i: (0, 0)))
        args.append(sinks)
        out_specs.append(pl.BlockSpec((N_HEADS, LANES), lambda r, i: (0, 0)))
        out_shape.append(jax.ShapeDtypeStruct((N_HEADS, LANES), F32))
    return pl.pallas_call(
        kern, name=name, grid=(lay.dil, nb + 1),
        in_specs=in_specs + ride.in_specs, out_specs=out_specs + ride.out_specs,
        out_shape=out_shape + ride.out_shapes,
        scratch_shapes=[pltpu.VMEM((BLOCK, kw), F32), pltpu.VMEM((BLOCK, kw), F32)] + ride.scratch,
        input_output_aliases=ride.aliases,
        compiler_params=_params("arbitrary", "arbitrary"),
    )(*args, *ride.args)


def _assemble(groups, name):
    T = groups[0][0].shape[0]
    widths = [g[0].shape[1] for g in groups]
    total = sum(widths)
    flat = [a for g in groups for a in g]

    def kern(*refs):
        ins = refs[:len(flat)]
        out_ref, cs_ref = refs[len(flat):]

        @pl.when(pl.program_id(0) == 0)
        def _():
            cs_ref[...] = jnp.zeros_like(cs_ref)

        pos = off = 0
        for g, w in zip(groups, widths):
            acc = ins[pos][...]
            for j in range(1, len(g)):
                acc = acc + ins[pos + j][...]
            pos += len(g)
            out_ref[:, off:off + w] = acc.astype(BF16)
            cs_ref[:, off:off + w] += jnp.sum(acc, axis=0, keepdims=True)
            off += w

    return pl.pallas_call(
        kern, name=name, grid=(T // ROWS,),
        in_specs=[pl.BlockSpec((ROWS, a.shape[1]), lambda i: (i, 0)) for a in flat],
        out_specs=[pl.BlockSpec((ROWS, total), lambda i: (i, 0)), pl.BlockSpec((1, total), lambda i: (0, 0))],
        out_shape=[jax.ShapeDtypeStruct((T, total), BF16), jax.ShapeDtypeStruct((1, total), F32)],
        compiler_params=_params("arbitrary"),
    )(*flat)


def _adamw(w, g, m, v, name):
    R, C = w.shape
    rows = min(R, ROWS)
    assert R % rows == 0

    def kern(w_ref, g_ref, m_ref, v_ref, d_ref, nm_ref, nv_ref):
        gv = g_ref[...]
        mn = ADAM_B1 * m_ref[...] + (1.0 - ADAM_B1) * gv
        vn = ADAM_B2 * v_ref[...] + (1.0 - ADAM_B2) * jnp.square(gv)
        m_hat = mn / (1.0 - ADAM_B1 ** ADAM_STEP)
        v_hat = vn / (1.0 - ADAM_B2 ** ADAM_STEP)
        d_ref[...] = -ADAM_LR * (m_hat / (jnp.sqrt(v_hat) + ADAM_EPS) + ADAM_WD * w_ref[...])
        nm_ref[...] = mn
        nv_ref[...] = vn

    blk = pl.BlockSpec((rows, C), lambda i: (i, 0))
    shp = jax.ShapeDtypeStruct((R, C), F32)
    return pl.pallas_call(
        kern, name=name, grid=(R // rows,),
        in_specs=[blk] * 4, out_specs=[blk] * 3, out_shape=[shp] * 3,
        compiler_params=_params("parallel"),
    )(w, g, m, v)


SUM_ROWS = 32


def _sum_slots(slots, name):
    n, R, C = slots.shape
    assert R % SUM_ROWS == 0

    def kern(s_ref, o_ref):
        acc = s_ref[0].astype(F32)
        for k in range(1, n):
            acc = acc + s_ref[k].astype(F32)
        o_ref[...] = acc

    return pl.pallas_call(
        kern, name=name, grid=(R // SUM_ROWS,),
        in_specs=[pl.BlockSpec((n, SUM_ROWS, C), lambda i: (0, i, 0))],
        out_specs=pl.BlockSpec((SUM_ROWS, C), lambda i: (i, 0)),
        out_shape=jax.ShapeDtypeStruct((R, C), F32),
        compiler_params=_params("parallel"),
    )(slots)


def _place():
    return lax.axis_index("x"), lax.axis_index("y"), lax.axis_index("c")


def _index(p):
    return 4 * p[0] + 2 * p[1] + p[2]


FLIPS = [(fx, fy, fc) for fx in (0, 1) for fy in (0, 1) for fc in (0, 1)][1:]


def _peer(me, flip):
    return tuple(1 - a if f else a for a, f in zip(me, flip))


def _gather_rows(shards, name):
    nw = len(shards)

    def body(*refs):
        ins, outs = refs[:nw], refs[nw:2 * nw]
        send_sems, recv_sems, local_sems = refs[2 * nw:]
        x, y, c = me = _place()
        sibling = (x, y, 1 - c)
        chips = [(1 - x, y), (x, 1 - y), (1 - x, 1 - y)]

        def rows(w, p):
            n = ins[w].shape[0]
            return outs[w].at[pl.ds(_index(p) * n, n), :]

        def copy(w, k, block, to, src=None):
            return pltpu.make_async_remote_copy(
                src_ref=rows(w, block) if src is None else src, dst_ref=rows(w, block),
                send_sem=send_sems.at[7 * w + k], recv_sem=recv_sems.at[7 * w + k],
                device_id=to, device_id_type=MESH)

        mine = [pltpu.make_async_copy(ins[w], rows(w, me), local_sems.at[w]) for w in range(nw)]
        for cp in mine:
            cp.start()
        first = []
        for w in range(nw):
            first.append(copy(w, 0, me, sibling, src=ins[w]))
            first += [copy(w, 1 + j, me, (*chip, c), src=ins[w]) for j, chip in enumerate(chips)]
        for cp in first:
            cp.start()
        passed = []
        for w in range(nw):
            for j, chip in enumerate(chips):
                copy(w, 1 + j, (*chip, c), me).wait_recv()
                passed.append(copy(w, 4 + j, (*chip, c), sibling))
                passed[-1].start()
        for w in range(nw):
            copy(w, 0, sibling, me).wait_recv()
            for j, chip in enumerate(chips):
                copy(w, 4 + j, (*chip, 1 - c), me).wait_recv()
        for cp in first + passed:
            cp.wait_send()
        for cp in mine:
            cp.wait()

    return pl.pallas_call(
        body, name=name,
        in_specs=[ANY] * nw, out_specs=[ANY] * nw,
        out_shape=[jax.ShapeDtypeStruct((N_DEV * s.shape[0], s.shape[1]), s.dtype) for s in shards],
        scratch_shapes=[pltpu.SemaphoreType.DMA((7 * nw,)), pltpu.SemaphoreType.DMA((7 * nw,)),
                        pltpu.SemaphoreType.DMA((nw,))],
    )(*shards)


def _gather_to_same_core(shards):
    nw = len(shards)

    def copies(ins, outs, send_sems, recv_sems, local_sems):
        x, y, c = me = _place()
        targets = [(x, y, 1 - c), (1 - x, y, c), (x, 1 - y, c), (1 - x, 1 - y, c)]

        def rows(w, p):
            n = ins[w].shape[0]
            return outs[w].at[pl.ds(_index(p) * n, n), :]

        def copy(w, k, block, to):
            return pltpu.make_async_remote_copy(
                src_ref=ins[w], dst_ref=rows(w, block), send_sem=send_sems.at[4 * w + k],
                recv_sem=recv_sems.at[4 * w + k], device_id=to, device_id_type=MESH)

        local = [pltpu.make_async_copy(ins[w], rows(w, me), local_sems.at[w]) for w in range(nw)]
        sends = [copy(w, k, me, to) for w in range(nw) for k, to in enumerate(targets)]
        recvs = [copy(w, k, frm, me) for w in range(nw) for k, frm in enumerate(targets)]
        return local, sends, recvs

    shapes = [jax.ShapeDtypeStruct((N_DEV * s.shape[0], s.shape[1]), s.dtype) for s in shards]
    return _Exchange(shards, shapes, 4 * nw, nw, copies)


def _gather_pass_on(partials):
    nw = len(partials)

    def copies(ins, outs, send_sems, recv_sems, local_sems):
        x, y, c = _place()
        sibling = (x, y, 1 - c)
        chips = [(1 - x, y), (x, 1 - y), (1 - x, 1 - y)]

        def rows(ref, w, p):
            n = ins[w].shape[0] // N_DEV
            return ref[w].at[pl.ds(_index(p) * n, n), :]

        def copy(w, j, core):
            block = (*chips[j], core)
            return pltpu.make_async_remote_copy(
                src_ref=rows(ins, w, block), dst_ref=rows(outs, w, block), send_sem=send_sems.at[3 * w + j],
                recv_sem=recv_sems.at[3 * w + j], device_id=sibling, device_id_type=MESH)

        sends = [copy(w, j, c) for w in range(nw) for j in range(3)]
        recvs = [copy(w, j, 1 - c) for w in range(nw) for j in range(3)]
        return [], sends, recvs

    shapes = [jax.ShapeDtypeStruct(p.shape, p.dtype) for p in partials]
    return _Exchange(partials, shapes, 3 * nw, 0, copies, aliases={w: w for w in range(nw)})


def _scatter_rows(parts):
    nw = len(parts)

    def copies(ins, outs, send_sems, recv_sems, local_sems):
        me = _place()

        def src(w, owner):
            n = ins[w].shape[0] // N_DEV
            return ins[w].at[pl.ds(_index(owner) * n, n), :]

        def copy(k, w, owner, sender, to):
            return pltpu.make_async_remote_copy(
                src_ref=src(w, owner), dst_ref=outs[w].at[_index(sender)],
                send_sem=send_sems.at[nw * k + w], recv_sem=recv_sems.at[nw * k + w],
                device_id=to, device_id_type=MESH)

        local = [pltpu.make_async_copy(src(w, me), outs[w].at[_index(me)], local_sems.at[w]) for w in range(nw)]
        peers = [_peer(me, flip) for flip in FLIPS]
        sends = [copy(k, w, peer, me, peer) for k, peer in enumerate(peers) for w in range(nw)]
        recvs = [copy(k, w, me, peer, me) for k, peer in enumerate(peers) for w in range(nw)]
        return local, sends, recvs

    shapes = [jax.ShapeDtypeStruct((N_DEV, p.shape[0] // N_DEV, p.shape[1]), p.dtype) for p in parts]
    return _Exchange(parts, shapes, 7 * nw, nw, copies)


def _sum_over_devices(v):
    shape = v.shape

    def body(v_ref, sum_ref, all_ref, send_sems, recv_sems):
        me = _place()
        all_ref[_index(me)] = v_ref[...]
        sends = []
        for k, flip in enumerate(FLIPS):
            peer = _peer(me, flip)
            sends.append(pltpu.make_async_remote_copy(
                src_ref=v_ref, dst_ref=all_ref.at[_index(me)],
                send_sem=send_sems.at[k], recv_sem=recv_sems.at[k], device_id=peer, device_id_type=MESH))
            sends[-1].start()
        for k, flip in enumerate(FLIPS):
            peer = _peer(me, flip)
            pltpu.make_async_remote_copy(
                src_ref=v_ref, dst_ref=all_ref.at[_index(peer)],
                send_sem=send_sems.at[k], recv_sem=recv_sems.at[k], device_id=peer, device_id_type=MESH).wait_recv()
        for cp in sends:
            cp.wait_send()
        acc = all_ref[0]
        for s in range(1, N_DEV):
            acc = acc + all_ref[s]
        sum_ref[...] = acc

    vmem = pl.BlockSpec(memory_space=pltpu.VMEM)
    return pl.pallas_call(
        body, name="sum_small_grads",
        in_specs=[vmem], out_specs=[vmem, vmem],
        out_shape=[jax.ShapeDtypeStruct(shape, F32), jax.ShapeDtypeStruct((N_DEV,) + shape, F32)],
        scratch_shapes=[pltpu.SemaphoreType.DMA((7,)), pltpu.SemaphoreType.DMA((7,))],
    )(v)[0]


SMALL_ROWS = 8


def _pack_small(vectors):
    padded = []
    for vec in vectors:
        vec = vec.reshape(-1)
        padded.append(jnp.pad(vec, (0, -vec.shape[0] % 128)))
    flat = jnp.concatenate(padded)
    flat = jnp.pad(flat, (0, -flat.shape[0] % (SMALL_ROWS * 128)))
    return flat.reshape(SMALL_ROWS, -1)


def _unpack_small(packed, shapes):
    flat = packed.reshape(-1)
    out, off = [], 0
    for shp in shapes:
        n = int(np.prod(shp))
        out.append(flat[off:off + n].reshape(shp))
        off += n + (-n % 128)
    return out


def kernel(x, g_attn, w_in, b_in, sinks_a, g_out_a, g_out_b, w_out, g_mlp, w_1, w_2, g_final, loss_target, m_g_attn, m_w_in, m_b_in, m_sinks_a, m_g_out_a, m_g_out_b, m_w_out, m_g_mlp, m_w_1, m_w_2, m_g_final, v_g_attn, v_w_in, v_b_in, v_sinks_a, v_g_out_a, v_g_out_b, v_w_out, v_g_mlp, v_w_1, v_w_2, v_g_final):
    xs, tgt = x[0], loss_target[0]
    T, D = xs.shape
    n_a = QW + 2 * KV_HEADS_A * HEAD_DIM
    g_fin = g_final.reshape(1, D)

    shards = [w_in[0].T.astype(BF16), w_out[0].astype(BF16), w_1[0].T.astype(BF16), w_2[0].astype(BF16)]
    w_in_t, = _gather_rows(shards[:1], "gather_w_in")
    w_in_ta, w_in_tb = w_in_t[:n_a], w_in_t[n_a:]

    ident = lambda acc: (acc,)
    add = lambda acc, other: (acc + other,)
    tiles = dict(tm=512, tn=1024)

    h1 = _norm_fwd(xs, g_attn, "norm_attn")
    proj_a, = _matmul(h1, w_in_ta, "nt", [BF16], add, tm=512, tn=n_a, tk=D, row_ins=[b_in[:, :n_a]], name="proj_a")
    proj_b, = _matmul(h1, w_in_tb, "nt", [BF16], add, tk=D, row_ins=[b_in[:, n_a:]], name="proj_b", **tiles)

    lay_a = _AttnLayout(1, KV_HEADS_A, 0, 0, 0, QW // (KV_HEADS_A * HEAD_DIM), QW // (KV_HEADS_A * HEAD_DIM) + 1)
    bias_a = _band_bias(WINDOW_A - 1, 1)
    o_a, l_a, w_o, w_1_t = _attn_fwd(proj_a, bias_a, sinks_a, lay_a, "attn_a_fwd",
                                     exchange=_gather_to_same_core(shards[1:3]))
    branches = []
    for n, (window, dil) in enumerate(DILATED_BRANCHES):
        lay = _AttnLayout(dil, N_HEADS, 3, 0, 3, 1, 2)
        bias = _band_bias(window // dil, dil)
        view = proj_b.reshape(T // dil, dil * 3 * QW)
        ride = None
        if n == 0:
            ride = _gather_to_same_core(shards[3:])
        elif n == 1:
            ride = _gather_pass_on([w_o, w_1_t, w_2_f])
        o, lse, *got = _attn_fwd(view, bias, None, lay, f"attn_b{dil}_fwd", exchange=ride)
        if n == 0:
            w_2_f, = got
        elif n == 1:
            w_o, w_1_t, w_2_f = got
        branches.append((lay, bias, view, o.reshape(T, QW), lse))
    o_b = [br[3] for br in branches]
    l_b = [br[4].transpose(2, 0, 1).reshape(T, N_HEADS) for br in branches]

    mix = _mix_fwd(o_a, o_b, l_b, g_out_a, g_out_b)
    x2, = _matmul(mix, w_o, "nn", [F32], add, tk=D, tile_ins=[xs], name="out_proj", **tiles)
    h2 = _norm_fwd(x2, g_mlp, "norm_mlp")

    def relu_sq(acc):
        u = jnp.maximum(acc, 0.0)
        return u, u * u

    u, u_sq = _matmul(h2, w_1_t, "nt", [BF16, BF16], relu_sq, tk=D, name="mlp_up", **tiles)
    x3, = _matmul(u_sq, w_2_f, "nn", [F32], add, tk=2048, tile_ins=[x2], name="mlp_down", **tiles)

    dx3, dx3_b, dg_final, loss_dev = _loss_head(x3, tgt, g_fin)

    d_pre, = _matmul(dx3_b, w_2_f, "nt", [BF16], lambda acc, uu: (acc * (2.0 * uu.astype(F32)),),
                     tk=D, tile_ins=[u], name="mlp_down_bwd", **tiles)
    wtiles = dict(tm=1024, tn=1024, tk=1024)
    dw_2, = _matmul(u_sq, dx3_b, "tn", [BF16], ident, name="mlp_down_wgrad", **wtiles)
    dh2, slots_2 = _matmul(d_pre, w_1_t, "nn", [F32], ident, tk=2048, name="mlp_up_bwd", exchange=_scatter_rows([dw_2]),
                           **tiles)
    dw_1_t, = _matmul(d_pre, h2, "tn", [BF16], ident, name="mlp_up_wgrad", **wtiles)
    dx2, dx2_b, dg_mlp = _norm_bwd(dh2, x2, g_mlp, dx3, "norm_mlp_bwd")

    dmix, = _matmul(dx2_b, w_o, "nt", [F32], ident, tk=D, name="out_proj_bwd", **tiles)
    dw_o, = _matmul(mix, dx2_b, "tn", [BF16], ident, name="out_proj_wgrad", **wtiles)
    do_a, dd_a, do1, do2, do3, dd1, dd2, dd3, dg_out_a, dg_out_b = _mix_bwd(dmix, o_a, o_b, l_b, g_out_a, g_out_b)

    by_class = lambda d, dil: d.reshape(T // dil, dil, N_HEADS).transpose(1, 2, 0)
    dq_a, dk_a, dv_a, dsinks, slots_1 = _attn_bwd(proj_a, do_a, l_a, by_class(dd_a, 1), bias_a, sinks_a, lay_a,
                                                  "attn_a_bwd", exchange=_scatter_rows([dw_1_t]))
    dsinks = dsinks[:, 0].reshape(1, N_HEADS)
    dqs, dks, dvs = [], [], []
    for (lay, bias, view, _, lse), do_n, dd_n in zip(branches, (do1, do2, do3), (dd1, dd2, dd3)):
        shape = (T // lay.dil, lay.dil * QW)
        ride = _scatter_rows([dw_o]) if lay.dil == 1 else None
        dq, dk, dv, *got = _attn_bwd(view, do_n.reshape(shape), lse, by_class(dd_n, lay.dil), bias, None, lay,
                                     f"attn_b{lay.dil}_bwd", exchange=ride)
        if ride:
            slots_o, = got
        dqs.append(dq.reshape(T, QW))
        dks.append(dk.reshape(T, QW))
        dvs.append(dv.reshape(T, QW))
    dproj_a, db_a = _assemble([[dq_a], [dk_a], [dv_a]], "dproj_a")
    dproj_b, db_b = _assemble([dqs, dks, dvs], "dproj_b")

    dh1_a, = _matmul(dproj_a, w_in_ta, "nn", [F32], ident, tk=n_a, name="in_proj_a_bwd", **tiles)
    dh1, = _matmul(dproj_b, w_in_tb, "nn", [F32], add, tk=3 * QW, tile_ins=[dh1_a], name="in_proj_b_bwd", **tiles)
    dw_in_ta, = _matmul(dproj_a, h1, "tn", [BF16], ident, tm=n_a, tn=1024, tk=512, name="in_proj_a_wgrad")
    dw_in_tb, = _matmul(dproj_b, h1, "tn", [BF16], ident, name="in_proj_b_wgrad", **wtiles)
    dw_in_t = jnp.concatenate([dw_in_ta, dw_in_tb], axis=0)
    dx, _, dg_attn = _norm_bwd(dh1, xs, g_attn, dx2, "norm_attn_bwd")

    slots_in, = _run_exchange(_scatter_rows([dw_in_t]), "scatter_w_in_grads")
    g_w_in = _sum_slots(slots_in, "sum_w_in_grads").T
    g_w_out = _sum_slots(slots_o, "sum_w_out_grads")
    g_w_1 = _sum_slots(slots_1, "sum_w_1_grads").T
    g_w_2 = _sum_slots(slots_2, "sum_w_2_grads")

    small_w = [g_attn, b_in, sinks_a, g_out_a, g_out_b, g_mlp, g_final]
    small_m = [m_g_attn, m_b_in, m_sinks_a, m_g_out_a, m_g_out_b, m_g_mlp, m_g_final]
    small_v = [v_g_attn, v_b_in, v_sinks_a, v_g_out_a, v_g_out_b, v_g_mlp, v_g_final]
    small_g = [dg_attn, jnp.concatenate([db_a, db_b], axis=1), dsinks, dg_out_a, dg_out_b, dg_mlp, dg_final]
    summed = _sum_over_devices(_pack_small(small_g + [loss_dev[:, :1]]))
    shapes = [w.shape for w in small_w]
    *g_small, loss = _unpack_small(summed, shapes + [()])

    big = [
        _adamw(w_in[0], g_w_in, m_w_in[0], v_w_in[0], "adamw_w_in"),
        _adamw(w_out[0], g_w_out, m_w_out[0], v_w_out[0], "adamw_w_out"),
        _adamw(w_1[0], g_w_1, m_w_1[0], v_w_1[0], "adamw_w_1"),
        _adamw(w_2[0], g_w_2, m_w_2[0], v_w_2[0], "adamw_w_2"),
    ]
    g_packed = _pack_small(g_small)
    small = _adamw(_pack_small(small_w), g_packed, _pack_small(small_m), _pack_small(small_v), "adamw_small")
    small = [_unpack_small(s, shapes) for s in small]

    def ordered(small_list, big_list):
        s = list(small_list)
        return [s[0], big_list[0][None], s[1], s[2], s[3], s[4], big_list[1][None], s[5],
                big_list[2][None], big_list[3][None], s[6]]

    grads = ordered(g_small, [g_w_in, g_w_out, g_w_1, g_w_2])
    deltas = ordered(small[0], [b[0] for b in big])
    new_m = ordered(small[1], [b[1] for b in big])
    new_v = ordered(small[2], [b[2] for b in big])
    return (loss, dx[None], *grads, *deltas, *new_m, *new_v)
```

```python
import numpy as np
import jax
import jax.numpy as jnp
from jax import lax
from jax.experimental import pallas as pl
from jax.experimental.pallas import tpu as pltpu

F32 = jnp.float32
BF16 = jnp.bfloat16

HEAD_DIM = 64
N_HEADS = 16
KV_HEADS_A = 2
BLOCK = 128
WINDOW_A = 128
DILATED_BRANCHES = ((128, 1), (512, 4), (2048, 16))
EPS = 1e-5
NEG_INF = -1e30
N_DEV = 8

ADAM_LR = 0.001
ADAM_B1 = 0.9
ADAM_B2 = 0.999
ADAM_EPS = 1e-08
ADAM_WD = 0.01
ADAM_STEP = 10

VMEM_LIMIT_BYTES = 56 * 1024 * 1024
MESH = pl.DeviceIdType.MESH
ANY = pl.BlockSpec(memory_space=pl.ANY)

NN = (((1,), (0,)), ((), ()))
NT = (((1,), (1,)), ((), ()))
TN = (((0,), (0,)), ((), ()))


def _dot(a, b, dims):
    return lax.dot_general(a, b, dims, preferred_element_type=F32)


def _params(*sem):
    return pltpu.CompilerParams(dimension_semantics=sem, vmem_limit_bytes=VMEM_LIMIT_BYTES)


RELAY_AT = 0.6


class _Exchange:
    def __init__(self, ins, out_shapes, n_remote, n_local, copies, aliases=None, relay=None):
        self.ins, self.out_shapes = list(ins), list(out_shapes)
        self.n_remote, self.n_local = n_remote, n_local
        self.copies = copies
        self.relay = relay
        self.aliases = aliases or {}

    def start(self, refs):
        local, sends, _ = self.copies(*refs)
        for cp in local + sends:
            cp.start()

    def middle(self, refs):
        arrived, onward = self.relay(*refs)
        for got, cp in zip(arrived, onward):
            got.wait_recv()
            cp.start()

    def finish(self, refs):
        local, sends, recvs = self.copies(*refs)
        for cp in recvs:
            cp.wait_recv()
        for cp in sends:
            cp.wait_send()
        for cp in local:
            cp.wait()
        if self.relay:
            for cp in self.relay(*refs)[1]:
                cp.wait_send()


class _Ride:
    def __init__(self, ex, n_in, n_out, n_scratch):
        self.ex = ex
        self.n = (n_in, n_out, n_scratch)
        self.args = ex.ins if ex else []
        self.in_specs = [ANY] * len(self.args)
        self.out_shapes = ex.out_shapes if ex else []
        self.out_specs = [ANY] * len(self.out_shapes)
        self.scratch = [pltpu.SemaphoreType.DMA((ex.n_remote,)), pltpu.SemaphoreType.DMA((ex.n_remote,)),
                        pltpu.SemaphoreType.DMA((max(ex.n_local, 1),))] if ex else []
        self.aliases = {n_in + i: n_out + o for i, o in ex.aliases.items()} if ex else {}

    def split(self, refs):
        n_in, n_out, n_scratch = self.n
        a = n_in
        b = a + len(self.args)
        c = b + n_out
        d = c + len(self.out_shapes)
        e = d + n_scratch
        return refs[:a], refs[b:c], refs[d:e], (refs[a:b], refs[c:d], *refs[e:])

    def around(self, step, n_steps, exrefs, compute):
        if self.ex is None:
            compute()
            return

        @pl.when(step == 0)
        def _():
            self.ex.start(exrefs)

        compute()

        if self.ex.relay:
            @pl.when(step == int(RELAY_AT * (n_steps - 1)))
            def _():
                self.ex.middle(exrefs)

        @pl.when(step == n_steps - 1)
        def _():
            self.ex.finish(exrefs)


def _run_exchange(ex, name):
    ride = _Ride(ex, 0, 0, 0)

    def body(*refs):
        exrefs = ride.split(refs)[3]
        ex.start(exrefs)
        if ex.relay:
            ex.middle(exrefs)
        ex.finish(exrefs)

    return pl.pallas_call(
        body, name=name, in_specs=ride.in_specs, out_specs=ride.out_specs, out_shape=ride.out_shapes,
        scratch_shapes=ride.scratch, input_output_aliases=ride.aliases,
    )(*ride.args)


def _matmul(a, b, dims, out_dtypes, epilogue, *, tm, tn, tk, name, tile_ins=(), row_ins=(), exchange=None):
    if dims == "tn":
        K, M = a.shape
    else:
        M, K = a.shape
    N = b.shape[0] if dims == "nt" else b.shape[1]
    tm, tn, tk = min(tm, M), min(tn, N), min(tk, K)
    assert M % tm == 0 and N % tn == 0 and K % tk == 0, (name, M, N, K, tm, tn, tk)
    grid = (M // tm, N // tn, K // tk)
    nk = grid[2]
    n_tile, n_row, n_out = len(tile_ins), len(row_ins), len(out_dtypes)
    dn = {"nn": NN, "nt": NT, "tn": TN}[dims]
    ride = _Ride(exchange, 2 + n_tile + n_row, n_out, 1 if nk > 1 else 0)

    def kern(*refs):
        ins, out_refs, scratch, exrefs = ride.split(refs)
        a_ref, b_ref = ins[:2]
        tile_refs = ins[2:2 + n_tile]
        row_refs = ins[2 + n_tile:]
        ids = [pl.program_id(d) for d in range(3)]

        def finish(acc):
            outs = epilogue(acc, *[r[...] for r in tile_refs], *[r[...] for r in row_refs])
            for o_ref, o in zip(out_refs, outs):
                o_ref[...] = o.astype(o_ref.dtype)

        def compute():
            if nk == 1:
                finish(_dot(a_ref[...], b_ref[...], dn))
                return
            acc_ref = scratch[0]

            @pl.when(ids[2] == 0)
            def _():
                acc_ref[...] = jnp.zeros_like(acc_ref)

            acc_ref[...] += _dot(a_ref[...], b_ref[...], dn)

            @pl.when(ids[2] == nk - 1)
            def _():
                finish(acc_ref[...])

        ride.around((ids[0] * grid[1] + ids[1]) * grid[2] + ids[2], grid[0] * grid[1] * grid[2], exrefs, compute)

    if dims == "tn":
        a_spec = pl.BlockSpec((tk, tm), lambda i, j, k: (k, i))
    else:
        a_spec = pl.BlockSpec((tm, tk), lambda i, j, k: (i, k))
    if dims == "nt":
        b_spec = pl.BlockSpec((tn, tk), lambda i, j, k: (j, k))
    else:
        b_spec = pl.BlockSpec((tk, tn), lambda i, j, k: (k, j))
    tile_spec = pl.BlockSpec((tm, tn), lambda i, j, k: (i, j))
    row_spec = pl.BlockSpec((1, tn), lambda i, j, k: (0, j))
    sem = ("arbitrary",) * 3 if exchange else ("parallel", "parallel", "arbitrary")
    return pl.pallas_call(
        kern,
        name=name,
        grid=grid,
        in_specs=[a_spec, b_spec] + [tile_spec] * n_tile + [row_spec] * n_row + ride.in_specs,
        out_specs=[tile_spec] * n_out + ride.out_specs,
        out_shape=[jax.ShapeDtypeStruct((M, N), dt) for dt in out_dtypes] + ride.out_shapes,
        scratch_shapes=([pltpu.VMEM((tm, tn), F32)] if nk > 1 else []) + ride.scratch,
        input_output_aliases=ride.aliases,
        compiler_params=_params(*sem),
    )(a, b, *tile_ins, *row_ins, *ride.args)


ROWS = 256
MIX_ROWS = 128


def _rstd(xv):
    return lax.rsqrt(jnp.mean(xv * xv, axis=-1, keepdims=True) + EPS)


def _norm_fwd(x, g, name, exchange=None):
    T, D = x.shape
    ride = _Ride(exchange, 2, 1, 0)

    def kern(*refs):
        (x_ref, g_ref), (h_ref,), _, exrefs = ride.split(refs)

        def compute():
            xv = x_ref[...]
            h_ref[...] = ((xv * _rstd(xv)) * g_ref[...]).astype(h_ref.dtype)

        ride.around(pl.program_id(0), T // ROWS, exrefs, compute)

    row = pl.BlockSpec((ROWS, D), lambda i: (i, 0))
    return pl.pallas_call(
        kern, name=name, grid=(T // ROWS,),
        in_specs=[row, pl.BlockSpec((1, D), lambda i: (0, 0))] + ride.in_specs,
        out_specs=[row] + ride.out_specs,
        out_shape=[jax.ShapeDtypeStruct((T, D), BF16)] + ride.out_shapes,
        scratch_shapes=ride.scratch, input_output_aliases=ride.aliases,
        compiler_params=_params("arbitrary"),
    )(x, g, *ride.args)


def _norm_bwd(dh, x, g, res, name, exchange=None):
    T, D = x.shape
    ride = _Ride(exchange, 4, 3, 0)

    def kern(*refs):
        (dh_ref, x_ref, g_ref, res_ref), (dx_ref, dxb_ref, dg_ref), _, exrefs = ride.split(refs)

        def compute():
            @pl.when(pl.program_id(0) == 0)
            def _():
                dg_ref[...] = jnp.zeros_like(dg_ref)

            xv = x_ref[...]
            r = _rstd(xv)
            xn = xv * r
            dhv = dh_ref[...]
            dg_ref[...] += jnp.sum(dhv * xn, axis=0, keepdims=True)
            t = dhv * g_ref[...]
            dx = res_ref[...] + r * (t - xn * jnp.mean(t * xn, axis=-1, keepdims=True))
            dx_ref[...] = dx
            dxb_ref[...] = dx.astype(BF16)

        ride.around(pl.program_id(0), T // ROWS, exrefs, compute)

    row = pl.BlockSpec((ROWS, D), lambda i: (i, 0))
    vec = pl.BlockSpec((1, D), lambda i: (0, 0))
    return pl.pallas_call(
        kern, name=name, grid=(T // ROWS,),
        in_specs=[row, row, vec, row] + ride.in_specs,
        out_specs=[row, row, vec] + ride.out_specs,
        out_shape=[jax.ShapeDtypeStruct((T, D), F32), jax.ShapeDtypeStruct((T, D), BF16),
                   jax.ShapeDtypeStruct((1, D), F32)] + ride.out_shapes,
        scratch_shapes=ride.scratch, input_output_aliases=ride.aliases,
        compiler_params=_params("arbitrary"),
    )(dh, x, g, res, *ride.args)


def _loss_head(x3, tgt, g):
    T, D = x3.shape

    def kern(x_ref, t_ref, g_ref, dx_ref, dxb_ref, dg_ref, loss_ref):
        @pl.when(pl.program_id(0) == 0)
        def _():
            dg_ref[...] = jnp.zeros_like(dg_ref)
            loss_ref[...] = jnp.zeros_like(loss_ref)

        xv = x_ref[...]
        gv = g_ref[...]
        r = _rstd(xv)
        xn = xv * r
        err = xn * gv - t_ref[...]
        per_tok = jnp.mean(err * err, axis=-1, keepdims=True)
        loss_ref[...] += 0.5 * jnp.sum(per_tok, axis=0, keepdims=True)
        dy = err * (1.0 / D)
        dg_ref[...] += jnp.sum(dy * xn, axis=0, keepdims=True)
        t = dy * gv
        dx = r * (t - xn * jnp.mean(t * xn, axis=-1, keepdims=True))
        dx_ref[...] = dx
        dxb_ref[...] = dx.astype(BF16)

    row = pl.BlockSpec((ROWS, D), lambda i: (i, 0))
    vec = pl.BlockSpec((1, D), lambda i: (0, 0))
    return pl.pallas_call(
        kern, name="loss_head", grid=(T // ROWS,),
        in_specs=[row, row, vec],
        out_specs=[row, row, vec, pl.BlockSpec((1, 128), lambda i: (0, 0))],
        out_shape=[jax.ShapeDtypeStruct((T, D), F32), jax.ShapeDtypeStruct((T, D), BF16),
                   jax.ShapeDtypeStruct((1, D), F32), jax.ShapeDtypeStruct((1, 128), F32)],
        compiler_params=_params("arbitrary"),
    )(x3, tgt, g)


def _spread_matrix():
    head_of_lane = np.arange(N_HEADS * HEAD_DIM) // HEAD_DIM
    return jnp.asarray(np.arange(N_HEADS)[:, None] == head_of_lane[None, :], dtype=BF16)


def _pieces(v, n):
    out = []
    for _ in range(n):
        piece = v.astype(BF16)
        out.append(piece)
        v = v - piece.astype(F32)
    return out


def _spread(v, spread):
    return sum(_dot(p, spread, NN) for p in _pieces(v, 3))


def _head_sums(v, spread):
    return sum(_dot(p, spread, NT) for p in _pieces(v, 2))


def _branch_weights(l1, l2, l3):
    lm = jnp.maximum(jnp.maximum(l1, l2), l3)
    e1, e2, e3 = jnp.exp(l1 - lm), jnp.exp(l2 - lm), jnp.exp(l3 - lm)
    inv = 1.0 / (e1 + e2 + e3)
    return e1 * inv, e2 * inv, e3 * inv


def _mix_fwd(oa, obs, lbs, ga, gb):
    T, W = oa.shape

    def kern(oa_ref, o1, o2, o3, l1, l2, l3, ga_ref, gb_ref, sp_ref, mix_ref):
        sp = sp_ref[...]
        w1, w2, w3 = _branch_weights(l1[...], l2[...], l3[...])
        ob = _spread(w1, sp) * o1[...] + _spread(w2, sp) * o2[...] + _spread(w3, sp) * o3[...]
        oav = oa_ref[...]
        mix_ref[:, :W] = ((oav * _rstd(oav)) * ga_ref[...]).astype(BF16)
        mix_ref[:, W:] = ((ob * _rstd(ob)) * gb_ref[...]).astype(BF16)

    row = pl.BlockSpec((MIX_ROWS, W), lambda i: (i, 0))
    per_head = pl.BlockSpec((MIX_ROWS, N_HEADS), lambda i: (i, 0))
    vec = pl.BlockSpec((1, W), lambda i: (0, 0))
    return pl.pallas_call(
        kern, name="mix_fwd", grid=(T // MIX_ROWS,),
        in_specs=[row] * 4 + [per_head] * 3 + [vec, vec, pl.BlockSpec((N_HEADS, W), lambda i: (0, 0))],
        out_specs=pl.BlockSpec((MIX_ROWS, 2 * W), lambda i: (i, 0)),
        out_shape=jax.ShapeDtypeStruct((T, 2 * W), BF16),
        compiler_params=_params("parallel"),
    )(oa, *obs, *lbs, ga, gb, _spread_matrix())


def _mix_bwd(dmix, oa, obs, lbs, ga, gb):
    T, W = oa.shape

    def kern(dm_ref, oa_ref, o1, o2, o3, l1, l2, l3, ga_ref, gb_ref, sp_ref,
             doa_ref, da_ref, do1, do2, do3, d1, d2, d3, dga_ref, dgb_ref):
        @pl.when(pl.program_id(0) == 0)
        def _():
            dga_ref[...] = jnp.zeros_like(dga_ref)
            dgb_ref[...] = jnp.zeros_like(dgb_ref)

        sp = sp_ref[...]
        oav = oa_ref[...]
        r = _rstd(oav)
        on = oav * r
        dy = dm_ref[:, :W]
        dga_ref[...] += jnp.sum(dy * on, axis=0, keepdims=True)
        t = dy * ga_ref[...]
        doa = r * (t - on * jnp.mean(t * on, axis=-1, keepdims=True))
        doa_ref[...] = doa.astype(BF16)
        da_ref[...] = _head_sums(doa * oav, sp)
        w1, w2, w3 = _branch_weights(l1[...], l2[...], l3[...])
        s1, s2, s3 = _spread(w1, sp), _spread(w2, sp), _spread(w3, sp)
        ob = s1 * o1[...] + s2 * o2[...] + s3 * o3[...]
        r = _rstd(ob)
        on = ob * r
        dy = dm_ref[:, W:]
        dgb_ref[...] += jnp.sum(dy * on, axis=0, keepdims=True)
        t = dy * gb_ref[...]
        dob = r * (t - on * jnp.mean(t * on, axis=-1, keepdims=True))
        c = _head_sums(dob * ob, sp)
        do1[...] = (s1 * dob).astype(BF16)
        do2[...] = (s2 * dob).astype(BF16)
        do3[...] = (s3 * dob).astype(BF16)
        d1[...] = w1 * c
        d2[...] = w2 * c
        d3[...] = w3 * c

    row = pl.BlockSpec((MIX_ROWS, W), lambda i: (i, 0))
    per_head = pl.BlockSpec((MIX_ROWS, N_HEADS), lambda i: (i, 0))
    vec = pl.BlockSpec((1, W), lambda i: (0, 0))
    bf = jax.ShapeDtypeStruct((T, W), BF16)
    ph = jax.ShapeDtypeStruct((T, N_HEADS), F32)
    vv = jax.ShapeDtypeStruct((1, W), F32)
    return pl.pallas_call(
        kern, name="mix_bwd", grid=(T // MIX_ROWS,),
        in_specs=[pl.BlockSpec((MIX_ROWS, 2 * W), lambda i: (i, 0))] + [row] * 4 + [per_head] * 3 + [vec, vec,
                  pl.BlockSpec((N_HEADS, W), lambda i: (0, 0))],
        out_specs=[row, per_head, row, row, row, per_head, per_head, per_head, vec, vec],
        out_shape=[bf, ph, bf, bf, bf, ph, ph, ph, vv, vv],
        compiler_params=_params("arbitrary"),
    )(dmix, oa, *obs, *lbs, ga, gb, _spread_matrix())


def _alibi_slopes(n):
    return np.asarray(2.0 ** (-8.0 * (np.arange(n) + 1) / n)).astype(np.float32)


def _band_bias(max_steps, step_dist):
    qi = np.arange(BLOCK)[None, :]
    kj = np.arange(BLOCK)[:, None]
    slopes = _alibi_slopes(N_HEADS)
    out = []
    for steps in (qi - kj, qi + BLOCK - kj):
        valid = (steps >= 0) & (steps <= max_steps)
        alibi = slopes[:, None, None] * (step_dist * steps).astype(np.float32)[None]
        out.append(np.where(valid[None], -alibi, np.float32(NEG_INF)).astype(np.float32))
    return jnp.asarray(np.stack(out))


class _AttnLayout:
    def __init__(self, dil, kv_heads, q_stride, q_off, k_stride, k_off, v_off):
        self.dil = dil
        self.kv_heads = kv_heads
        self.kw = kv_heads * HEAD_DIM
        self.rep = N_HEADS // kv_heads
        self.q_col = lambda r: r * q_stride + q_off
        self.k_col = lambda r: r * k_stride + k_off
        self.v_col = lambda r: r * k_stride + v_off


QW = N_HEADS * HEAD_DIM
LANES = 128


def _head_cols(h):
    return slice(h * HEAD_DIM, (h + 1) * HEAD_DIM)


def _attn_fwd(proj, bias, sinks, lay, name, exchange=None):
    L = proj.shape[0]
    nb = L // BLOCK
    kw, rep = lay.kw, lay.rep
    use_sinks = sinks is not None
    scale = HEAD_DIM ** -0.5
    ride = _Ride(exchange, 7 if use_sinks else 6, 2, 2)

    def kern(*refs):
        ins, (o_ref, l_ref), (sc_ref, pr_ref), exrefs = ride.split(refs)
        q_ref, kc_ref, kp_ref, vc_ref, vp_ref, b_ref = ins[:6]
        s_ref = ins[6] if use_sinks else None
        r, i = pl.program_id(0), pl.program_id(1)
        first = i == 0
        ride.around(r * nb + i, lay.dil * nb, exrefs,
                    lambda: compute(q_ref, kc_ref, kp_ref, vc_ref, vp_ref, b_ref, s_ref, o_ref, l_ref, first,
                                    sc_ref, pr_ref))

    def compute(q_ref, kc_ref, kp_ref, vc_ref, vp_ref, b_ref, s_ref, o_ref, l_ref, first, sc_ref, pr_ref):
        for h in range(N_HEADS):
            g = h // rep
            q = q_ref[:, _head_cols(h)]
            sc_ref[h, 0] = _dot(kc_ref[:, _head_cols(g)], q, NT) * scale + b_ref[0, h]
            s_p = _dot(kp_ref[:, _head_cols(g)], q, NT) * scale + b_ref[1, h]
            sc_ref[h, 1] = jnp.where(first, NEG_INF, s_p)
        inv = []
        for h in range(N_HEADS):
            s_c, s_p = sc_ref[h, 0], sc_ref[h, 1]
            m = jnp.maximum(jnp.max(s_c, axis=0, keepdims=True), jnp.max(s_p, axis=0, keepdims=True))
            if use_sinks:
                sink = s_ref[:, h:h + 1]
                m = jnp.maximum(m, sink)
            p_c = jnp.exp(s_c - m)
            p_p = jnp.exp(s_p - m)
            denom = jnp.sum(p_c, axis=0, keepdims=True) + jnp.sum(p_p, axis=0, keepdims=True)
            if use_sinks:
                denom = denom + jnp.exp(sink - m)
            pr_ref[h, 0] = p_c.astype(BF16)
            pr_ref[h, 1] = p_p.astype(BF16)
            l_ref[h:h + 1, :] = m + jnp.log(denom)
            inv.append(1.0 / denom)
        values_t = {}
        for pair in range(N_HEADS // 2):
            halves = []
            for h in (2 * pair, 2 * pair + 1):
                g = h // rep
                blk = slice(g // 2 * LANES, (g // 2 + 1) * LANES)
                if g // 2 not in values_t:
                    values_t[g // 2] = (vc_ref[:, blk].T, vp_ref[:, blk].T)
                vct, vpt = values_t[g // 2]
                rows = slice(g % 2 * HEAD_DIM, (g % 2 + 1) * HEAD_DIM)
                o_t = _dot(vct[rows], pr_ref[h, 0], NN) + _dot(vpt[rows], pr_ref[h, 1], NN)
                halves.append(o_t * inv[h])
            o_ref[:, pair * LANES:(pair + 1) * LANES] = jnp.concatenate(halves, axis=0).T

    prev = lambda i: jnp.maximum(i - 1, 0)
    in_specs = [
        pl.BlockSpec((BLOCK, QW), lambda r, i: (i, lay.q_col(r))),
        pl.BlockSpec((BLOCK, kw), lambda r, i: (i, lay.k_col(r))),
        pl.BlockSpec((BLOCK, kw), lambda r, i: (prev(i), lay.k_col(r))),
        pl.BlockSpec((BLOCK, kw), lambda r, i: (i, lay.v_col(r))),
        pl.BlockSpec((BLOCK, kw), lambda r, i: (prev(i), lay.v_col(r))),
        pl.BlockSpec((2, N_HEADS, BLOCK, BLOCK), lambda r, i: (0, 0, 0, 0)),
    ]
    args = [proj, proj, proj, proj, proj, bias]
    if use_sinks:
        in_specs.append(pl.BlockSpec((1, N_HEADS), lambda r, i: (0, 0)))
        args.append(sinks)
    out_specs = [pl.BlockSpec((BLOCK, QW), lambda r, i: (i, r)),
                 pl.BlockSpec((None, N_HEADS, BLOCK), lambda r, i: (r, 0, i))]
    out_shape = [jax.ShapeDtypeStruct((L, lay.dil * QW), F32), jax.ShapeDtypeStruct((lay.dil, N_HEADS, L), F32)]
    return pl.pallas_call(
        kern, name=name, grid=(lay.dil, nb),
        in_specs=in_specs + ride.in_specs, out_specs=out_specs + ride.out_specs,
        out_shape=out_shape + ride.out_shapes,
        scratch_shapes=[pltpu.VMEM((N_HEADS, 2, BLOCK, BLOCK), F32), pltpu.VMEM((N_HEADS, 2, BLOCK, BLOCK), BF16)]
        + ride.scratch,
        input_output_aliases=ride.aliases,
        compiler_params=_params("arbitrary", "arbitrary"),
    )(*args, *ride.args)


def _attn_bwd(proj, do, lse, dd, bias, sinks, lay, name, exchange=None):
    L = proj.shape[0]
    nb = L // BLOCK
    kw, rep, kvh = lay.kw, lay.rep, lay.kv_heads
    use_sinks = sinks is not None
    scale = HEAD_DIM ** -0.5
    ride = _Ride(exchange, 10 if use_sinks else 9, 4 if use_sinks else 3, 6)

    def kern(*refs):
        ins, outs, (ck_ref, cv_ref, *staged), exrefs = ride.split(refs)
        q_ref, kc_ref, kp_ref, vc_ref, vp_ref, do_ref, l_ref, d_ref, b_ref = ins[:9]
        s_ref = ins[9] if use_sinks else None
        dq_ref, dk_ref, dv_ref = outs[:3]
        ds_ref = outs[3] if use_sinks else None
        r = pl.program_id(0)
        i = pl.program_id(1)
        ride.around(r * (nb + 1) + i, lay.dil * (nb + 1), exrefs,
                    lambda: compute(q_ref, kc_ref, kp_ref, vc_ref, vp_ref, do_ref, l_ref, d_ref, b_ref, s_ref,
                                    dq_ref, dk_ref, dv_ref, ds_ref, ck_ref, cv_ref, r, i, *staged))

    def compute(q_ref, kc_ref, kp_ref, vc_ref, vp_ref, do_ref, l_ref, d_ref, b_ref, s_ref,
                dq_ref, dk_ref, dv_ref, ds_ref, ck_ref, cv_ref, r, i, sc_ref, dp_ref, pr_ref, dsc_ref):
        first = i == 0

        @pl.when(first)
        def _():
            ck_ref[...] = jnp.zeros_like(ck_ref)
            cv_ref[...] = jnp.zeros_like(cv_ref)

        if use_sinks:
            @pl.when(first & (r == 0))
            def _():
                ds_ref[...] = jnp.zeros_like(ds_ref)

        @pl.when(i < nb)
        def _():
            for h in range(N_HEADS):
                gs = _head_cols(h // rep)
                q = q_ref[:, _head_cols(h)]
                dov = do_ref[:, _head_cols(h)]
                sc_ref[h, 0] = _dot(kc_ref[:, gs], q, NT) * scale + b_ref[0, h]
                s_p = _dot(kp_ref[:, gs], q, NT) * scale + b_ref[1, h]
                sc_ref[h, 1] = jnp.where(first, NEG_INF, s_p)
                dp_ref[h, 0] = _dot(vc_ref[:, gs], dov, NT)
                dp_ref[h, 1] = _dot(vp_ref[:, gs], dov, NT)
            for h in range(N_HEADS):
                lrow = l_ref[h:h + 1, :]
                drow = d_ref[h:h + 1, :]
                for c in range(2):
                    p = jnp.exp(sc_ref[h, c] - lrow)
                    pr_ref[h, c] = p.astype(BF16)
                    dsc_ref[h, c] = (p * (dp_ref[h, c] - drow) * scale).astype(BF16)
                if use_sinks:
                    ds_ref[h:h + 1, :] += -(jnp.exp(s_ref[:, h:h + 1] - lrow) * drow)
            dq_halves = []
            keys_t = {}
            for g in range(kvh):
                gs = _head_cols(g)
                blk = slice(g // 2 * LANES, (g // 2 + 1) * LANES)
                rows = slice(g % 2 * HEAD_DIM, (g % 2 + 1) * HEAD_DIM)
                if g // 2 not in keys_t:
                    keys_t[g // 2] = (kc_ref[:, blk].T, kp_ref[:, blk].T)
                kct, kpt = keys_t[g // 2][0][rows], keys_t[g // 2][1][rows]
                dkc = jnp.zeros((BLOCK, HEAD_DIM), F32)
                dkp = jnp.zeros((BLOCK, HEAD_DIM), F32)
                dvc = jnp.zeros((BLOCK, HEAD_DIM), F32)
                dvp = jnp.zeros((BLOCK, HEAD_DIM), F32)
                for h in range(g * rep, (g + 1) * rep):
                    q = q_ref[:, _head_cols(h)]
                    dov = do_ref[:, _head_cols(h)]
                    ds_c, ds_p = dsc_ref[h, 0], dsc_ref[h, 1]
                    dkc += _dot(ds_c, q, NN)
                    dkp += _dot(ds_p, q, NN)
                    dvc += _dot(pr_ref[h, 0], dov, NN)
                    dvp += _dot(pr_ref[h, 1], dov, NN)
                    dq_halves.append(_dot(kct, ds_c, NN) + _dot(kpt, ds_p, NN))
                    if h % 2:
                        dq_ref[:, (h // 2) * LANES:(h // 2 + 1) * LANES] = jnp.concatenate(dq_halves, axis=0).T
                        dq_halves = []
                dk_ref[:, gs] = ck_ref[:, gs] + dkp
                dv_ref[:, gs] = cv_ref[:, gs] + dvp
                ck_ref[:, gs] = dkc
                cv_ref[:, gs] = dvc

        @pl.when(i == nb)
        def _():
            dk_ref[...] = ck_ref[...]
            dv_ref[...] = cv_ref[...]
            if use_sinks:
                @pl.when(r == lay.dil - 1)
                def _():
                    ds_ref[...] = jnp.broadcast_to(jnp.sum(ds_ref[...], axis=1, keepdims=True), ds_ref.shape)

    cur = lambda i: jnp.minimum(i, nb - 1)
    prev = lambda i: jnp.maximum(jnp.minimum(i, nb - 1) - 1, 0)
    done = lambda i: jnp.maximum(i - 1, 0)
    qspec = lambda col: pl.BlockSpec((BLOCK, QW), lambda r, i: (cur(i), col(r)))
    per_head = pl.BlockSpec((None, N_HEADS, BLOCK), lambda r, i: (r, 0, cur(i)))
    in_specs = [
        qspec(lay.q_col),
        pl.BlockSpec((BLOCK, kw), lambda r, i: (cur(i), lay.k_col(r))),
        pl.BlockSpec((BLOCK, kw), lambda r, i: (prev(i), lay.k_col(r))),
        pl.BlockSpec((BLOCK, kw), lambda r, i: (cur(i), lay.v_col(r))),
        pl.BlockSpec((BLOCK, kw), lambda r, i: (prev(i), lay.v_col(r))),
        qspec(lambda r: r), per_head, per_head,
        pl.BlockSpec((2, N_HEADS, BLOCK, BLOCK), lambda r, i: (0, 0, 0, 0)),
    ]
    args = [proj, proj, proj, proj, proj, do, lse, dd, bias]
    out_specs = [
        qspec(lambda r: r),
        pl.BlockSpec((BLOCK, kw), lambda r, i: (done(i), r)),
        pl.BlockSpec((BLOCK, kw), lambda r, i: (done(i), r)),
    ]
    dkv_shape = jax.ShapeDtypeStruct((L, lay.dil * kw), F32)
    out_shape = [jax.ShapeDtypeStruct((L, lay.dil * QW), F32), dkv_shape, dkv_shape]
    if use_sinks:
        in_specs.append(pl.BlockSpec((1, N_HEADS), lambda r, i: (0, 0)))
        args.append(sinks)
        out_specs.append(pl.BlockSpec((N_HEADS, LANES), lambda r, i: (0, 0)))
        out_shape.append(jax.ShapeDtypeStruct((N_HEADS, LANES), F32))
    return pl.pallas_call(
        kern, name=name, grid=(lay.dil, nb + 1),
        in_specs=in_specs + ride.in_specs, out_specs=out_specs + ride.out_specs,
        out_shape=out_shape + ride.out_shapes,
        scratch_shapes=[pltpu.VMEM((BLOCK, kw), F32), pltpu.VMEM((BLOCK, kw), F32)]
        + [pltpu.VMEM((N_HEADS, 2, BLOCK, BLOCK), dt) for dt in (F32, F32, BF16, BF16)] + ride.scratch,
        input_output_aliases=ride.aliases,
        compiler_params=_params("arbitrary", "arbitrary"),
    )(*args, *ride.args)


def _assemble(groups, name):
    T = groups[0][0].shape[0]
    widths = [g[0].shape[1] for g in groups]
    total = sum(widths)
    flat = [a for g in groups for a in g]

    def kern(*refs):
        ins = refs[:len(flat)]
        out_ref, cs_ref = refs[len(flat):]

        @pl.when(pl.program_id(0) == 0)
        def _():
            cs_ref[...] = jnp.zeros_like(cs_ref)

        pos = off = 0
        for g, w in zip(groups, widths):
            acc = ins[pos][...]
            for j in range(1, len(g)):
                acc = acc + ins[pos + j][...]
            pos += len(g)
            out_ref[:, off:off + w] = acc.astype(BF16)
            cs_ref[:, off:off + w] += jnp.sum(acc, axis=0, keepdims=True)
            off += w

    return pl.pallas_call(
        kern, name=name, grid=(T // ROWS,),
        in_specs=[pl.BlockSpec((ROWS, a.shape[1]), lambda i: (i, 0)) for a in flat],
        out_specs=[pl.BlockSpec((ROWS, total), lambda i: (i, 0)), pl.BlockSpec((1, total), lambda i: (0, 0))],
        out_shape=[jax.ShapeDtypeStruct((T, total), BF16), jax.ShapeDtypeStruct((1, total), F32)],
        compiler_params=_params("arbitrary"),
    )(*flat)


def _adamw(w, g, m, v, name):
    R, C = w.shape
    rows = min(R, ROWS)
    assert R % rows == 0

    def kern(w_ref, g_ref, m_ref, v_ref, d_ref, nm_ref, nv_ref):
        gv = g_ref[...]
        mn = ADAM_B1 * m_ref[...] + (1.0 - ADAM_B1) * gv
        vn = ADAM_B2 * v_ref[...] + (1.0 - ADAM_B2) * jnp.square(gv)
        m_hat = mn / (1.0 - ADAM_B1 ** ADAM_STEP)
        v_hat = vn / (1.0 - ADAM_B2 ** ADAM_STEP)
        d_ref[...] = -ADAM_LR * (m_hat / (jnp.sqrt(v_hat) + ADAM_EPS) + ADAM_WD * w_ref[...])
        nm_ref[...] = mn
        nv_ref[...] = vn

    blk = pl.BlockSpec((rows, C), lambda i: (i, 0))
    shp = jax.ShapeDtypeStruct((R, C), F32)
    return pl.pallas_call(
        kern, name=name, grid=(R // rows,),
        in_specs=[blk] * 4, out_specs=[blk] * 3, out_shape=[shp] * 3,
        compiler_params=_params("parallel"),
    )(w, g, m, v)


SUM_ROWS = 32


def _sum_slots(slots, name):
    n, R, C = slots.shape
    assert R % SUM_ROWS == 0

    def kern(s_ref, o_ref):
        acc = s_ref[0].astype(F32)
        for k in range(1, n):
            acc = acc + s_ref[k].astype(F32)
        o_ref[...] = acc

    return pl.pallas_call(
        kern, name=name, grid=(R // SUM_ROWS,),
        in_specs=[pl.BlockSpec((n, SUM_ROWS, C), lambda i: (0, i, 0))],
        out_specs=pl.BlockSpec((SUM_ROWS, C), lambda i: (i, 0)),
        out_shape=jax.ShapeDtypeStruct((R, C), F32),
        compiler_params=_params("parallel"),
    )(slots)


def _place():
    return lax.axis_index("x"), lax.axis_index("y"), lax.axis_index("c")


def _index(p):
    return 4 * p[0] + 2 * p[1] + p[2]


FLIPS = [(fx, fy, fc) for fx in (0, 1) for fy in (0, 1) for fc in (0, 1)][1:]


def _peer(me, flip):
    return tuple(1 - a if f else a for a, f in zip(me, flip))


def _gather_rows(shards, part=(0, 1), into=None):
    nw = len(shards)

    def plan(ins, outs, send_sems, recv_sems):
        x, y, c = me = _place()
        sibling = (x, y, 1 - c)
        chips = [(1 - x, y), (x, 1 - y), (1 - x, 1 - y)]

        def span(w):
            cnt = ins[w].shape[0] // part[1]
            return part[0] * cnt, cnt

        def rows(w, p):
            lo, cnt = span(w)
            return outs[w].at[pl.ds(_index(p) * ins[w].shape[0] + lo, cnt), :]

        def own(w):
            lo, cnt = span(w)
            return ins[w].at[pl.ds(lo, cnt), :]

        def copy(w, k, block, to):
            return pltpu.make_async_remote_copy(
                src_ref=own(w) if block is me else rows(w, block), dst_ref=rows(w, block),
                send_sem=send_sems.at[7 * w + k], recv_sem=recv_sems.at[7 * w + k],
                device_id=to, device_id_type=MESH)

        return me, sibling, chips, c, rows, own, copy

    def copies(ins, outs, send_sems, recv_sems, local_sems):
        me, sibling, chips, c, rows, own, copy = plan(ins, outs, send_sems, recv_sems)
        local = [pltpu.make_async_copy(own(w), rows(w, me), local_sems.at[w]) for w in range(nw)]
        sends, recvs = [], []
        for w in range(nw):
            sends.append(copy(w, 0, me, sibling))
            sends += [copy(w, 1 + j, me, (*chip, c)) for j, chip in enumerate(chips)]
            recvs.append(copy(w, 0, sibling, me))
            recvs += [copy(w, 4 + j, (*chip, 1 - c), me) for j, chip in enumerate(chips)]
        return local, sends, recvs

    def relay(ins, outs, send_sems, recv_sems, local_sems):
        me, sibling, chips, c, rows, own, copy = plan(ins, outs, send_sems, recv_sems)
        arrived = [copy(w, 1 + j, (*chip, c), me) for w in range(nw) for j, chip in enumerate(chips)]
        onward = [copy(w, 4 + j, (*chip, c), sibling) for w in range(nw) for j, chip in enumerate(chips)]
        return arrived, onward

    shapes = [jax.ShapeDtypeStruct((N_DEV * s.shape[0], s.shape[1]), s.dtype) for s in shards]
    aliases = {nw + w: w for w in range(nw)} if into else None
    return _Exchange(shards + (into or []), shapes, 7 * nw, nw, copies, aliases=aliases, relay=relay)


def _scatter_rows(parts, part=(0, 1)):
    nw = len(parts)

    def copies(ins, outs, send_sems, recv_sems, local_sems):
        me = _place()

        def src(w, owner):
            n = ins[w].shape[0] // N_DEV
            cnt = n // part[1]
            return ins[w].at[pl.ds(_index(owner) * n + part[0] * cnt, cnt), :]

        def copy(k, w, owner, sender, to):
            return pltpu.make_async_remote_copy(
                src_ref=src(w, owner), dst_ref=outs[w].at[_index(sender)],
                send_sem=send_sems.at[nw * k + w], recv_sem=recv_sems.at[nw * k + w],
                device_id=to, device_id_type=MESH)

        local = [pltpu.make_async_copy(src(w, me), outs[w].at[_index(me)], local_sems.at[w]) for w in range(nw)]
        peers = [_peer(me, flip) for flip in FLIPS]
        sends = [copy(k, w, peer, me, peer) for k, peer in enumerate(peers) for w in range(nw)]
        recvs = [copy(k, w, me, peer, me) for k, peer in enumerate(peers) for w in range(nw)]
        return local, sends, recvs

    shapes = [jax.ShapeDtypeStruct((N_DEV, p.shape[0] // N_DEV // part[1], p.shape[1]), p.dtype) for p in parts]
    return _Exchange(parts, shapes, 7 * nw, nw, copies)


def _sum_over_devices(v):
    shape = v.shape

    def body(v_ref, sum_ref, all_ref, send_sems, recv_sems):
        me = _place()
        all_ref[_index(me)] = v_ref[...]
        sends = []
        for k, flip in enumerate(FLIPS):
            peer = _peer(me, flip)
            sends.append(pltpu.make_async_remote_copy(
                src_ref=v_ref, dst_ref=all_ref.at[_index(me)],
                send_sem=send_sems.at[k], recv_sem=recv_sems.at[k], device_id=peer, device_id_type=MESH))
            sends[-1].start()
        for k, flip in enumerate(FLIPS):
            peer = _peer(me, flip)
            pltpu.make_async_remote_copy(
                src_ref=v_ref, dst_ref=all_ref.at[_index(peer)],
                send_sem=send_sems.at[k], recv_sem=recv_sems.at[k], device_id=peer, device_id_type=MESH).wait_recv()
        for cp in sends:
            cp.wait_send()
        acc = all_ref[0]
        for s in range(1, N_DEV):
            acc = acc + all_ref[s]
        sum_ref[...] = acc

    vmem = pl.BlockSpec(memory_space=pltpu.VMEM)
    return pl.pallas_call(
        body, name="sum_small_grads",
        in_specs=[vmem], out_specs=[vmem, vmem],
        out_shape=[jax.ShapeDtypeStruct(shape, F32), jax.ShapeDtypeStruct((N_DEV,) + shape, F32)],
        scratch_shapes=[pltpu.SemaphoreType.DMA((7,)), pltpu.SemaphoreType.DMA((7,))],
    )(v)[0]


SMALL_ROWS = 8


def _pack_small(vectors):
    padded = []
    for vec in vectors:
        vec = vec.reshape(-1)
        padded.append(jnp.pad(vec, (0, -vec.shape[0] % 128)))
    flat = jnp.concatenate(padded)
    flat = jnp.pad(flat, (0, -flat.shape[0] % (SMALL_ROWS * 128)))
    return flat.reshape(SMALL_ROWS, -1)


def _unpack_small(packed, shapes):
    flat = packed.reshape(-1)
    out, off = [], 0
    for shp in shapes:
        n = int(np.prod(shp))
        out.append(flat[off:off + n].reshape(shp))
        off += n + (-n % 128)
    return out


def kernel(x, g_attn, w_in, b_in, sinks_a, g_out_a, g_out_b, w_out, g_mlp, w_1, w_2, g_final, loss_target, m_g_attn, m_w_in, m_b_in, m_sinks_a, m_g_out_a, m_g_out_b, m_w_out, m_g_mlp, m_w_1, m_w_2, m_g_final, v_g_attn, v_w_in, v_b_in, v_sinks_a, v_g_out_a, v_g_out_b, v_w_out, v_g_mlp, v_w_1, v_w_2, v_g_final):
    xs, tgt = x[0], loss_target[0]
    T, D = xs.shape
    n_a = QW + 2 * KV_HEADS_A * HEAD_DIM
    g_fin = g_final.reshape(1, D)

    shards = [w_in[0].T.astype(BF16), w_out[0].astype(BF16), w_1[0].T.astype(BF16), w_2[0].astype(BF16)]
    ident = lambda acc: (acc,)
    add = lambda acc, other: (acc + other,)
    tiles = dict(tm=512, tn=1024)

    h1, w_in_t = _norm_fwd(xs, g_attn, "norm_attn", exchange=_gather_rows(shards[:1]))
    w_in_ta, w_in_tb = w_in_t[:n_a], w_in_t[n_a:]
    proj_a, = _matmul(h1, w_in_ta, "nt", [BF16], add, tm=512, tn=n_a, tk=D, row_ins=[b_in[:, :n_a]], name="proj_a")
    proj_b, w_o = _matmul(h1, w_in_tb, "nt", [BF16], add, tk=D, row_ins=[b_in[:, n_a:]], name="proj_b",
                          exchange=_gather_rows(shards[1:2]), **tiles)

    lay_a = _AttnLayout(1, KV_HEADS_A, 0, 0, 0, QW // (KV_HEADS_A * HEAD_DIM), QW // (KV_HEADS_A * HEAD_DIM) + 1)
    bias_a = _band_bias(WINDOW_A - 1, 1)
    o_a, l_a, w_1_t = _attn_fwd(proj_a, bias_a, sinks_a, lay_a, "attn_a_fwd",
                                exchange=_gather_rows(shards[2:3], part=(0, 2)))
    branches = []
    for n, (window, dil) in enumerate(DILATED_BRANCHES):
        lay = _AttnLayout(dil, N_HEADS, 3, 0, 3, 1, 2)
        bias = _band_bias(window // dil, dil)
        view = proj_b.reshape(T // dil, dil * 3 * QW)
        ride = _gather_rows(shards[2:3], part=(1, 2), into=[w_1_t]) if n == 0 else None
        o, lse, *got = _attn_fwd(view, bias, None, lay, f"attn_b{dil}_fwd", exchange=ride)
        if ride:
            w_1_t, = got
        branches.append((lay, bias, view, o.reshape(T, QW), lse))
    o_b = [br[3] for br in branches]
    l_b = [br[4].transpose(2, 0, 1).reshape(T, N_HEADS) for br in branches]

    mix = _mix_fwd(o_a, o_b, l_b, g_out_a, g_out_b)
    x2, = _matmul(mix, w_o, "nn", [F32], add, tk=D, tile_ins=[xs], name="out_proj", **tiles)
    h2, = _norm_fwd(x2, g_mlp, "norm_mlp")

    def relu_sq(acc):
        u = jnp.maximum(acc, 0.0)
        return u, u * u

    u, u_sq, w_2_f = _matmul(h2, w_1_t, "nt", [BF16, BF16], relu_sq, tk=D, name="mlp_up",
                             exchange=_gather_rows(shards[3:]), **tiles)
    x3, = _matmul(u_sq, w_2_f, "nn", [F32], add, tk=2048, tile_ins=[x2], name="mlp_down", **tiles)

    dx3, dx3_b, dg_final, loss_dev = _loss_head(x3, tgt, g_fin)

    d_pre, = _matmul(dx3_b, w_2_f, "nt", [BF16], lambda acc, uu: (acc * (2.0 * uu.astype(F32)),),
                     tk=D, tile_ins=[u], name="mlp_down_bwd", **tiles)
    wtiles = dict(tm=1024, tn=1024, tk=1024)
    dw_2, = _matmul(u_sq, dx3_b, "tn", [BF16], ident, name="mlp_down_wgrad", **wtiles)
    dh2, slots_2 = _matmul(d_pre, w_1_t, "nn", [F32], ident, tk=2048, name="mlp_up_bwd", exchange=_scatter_rows([dw_2]),
                           **tiles)
    dw_1_t, = _matmul(d_pre, h2, "tn", [BF16], ident, name="mlp_up_wgrad", **wtiles)
    dx2, dx2_b, dg_mlp = _norm_bwd(dh2, x2, g_mlp, dx3, "norm_mlp_bwd")

    dmix, = _matmul(dx2_b, w_o, "nt", [F32], ident, tk=D, name="out_proj_bwd", **tiles)
    dw_o, = _matmul(mix, dx2_b, "tn", [BF16], ident, name="out_proj_wgrad", **wtiles)
    do_a, dd_a, do1, do2, do3, dd1, dd2, dd3, dg_out_a, dg_out_b = _mix_bwd(dmix, o_a, o_b, l_b, g_out_a, g_out_b)

    by_class = lambda d, dil: d.reshape(T // dil, dil, N_HEADS).transpose(1, 2, 0)
    dq_a, dk_a, dv_a, dsinks, slots_1a = _attn_bwd(proj_a, do_a, l_a, by_class(dd_a, 1), bias_a, sinks_a, lay_a,
                                                   "attn_a_bwd", exchange=_scatter_rows([dw_1_t], part=(0, 2)))
    dsinks = dsinks[:, 0].reshape(1, N_HEADS)
    dqs, dks, dvs = [], [], []
    rides = [_scatter_rows([dw_1_t], part=(1, 2)), _scatter_rows([dw_o]), None]
    for (lay, bias, view, _, lse), do_n, dd_n, ride in zip(branches, (do1, do2, do3), (dd1, dd2, dd3), rides):
        shape = (T // lay.dil, lay.dil * QW)
        dq, dk, dv, *got = _attn_bwd(view, do_n.reshape(shape), lse, by_class(dd_n, lay.dil), bias, None, lay,
                                     f"attn_b{lay.dil}_bwd", exchange=ride)
        if lay.dil == 1:
            slots_1b, = got
        elif ride:
            slots_o, = got
        dqs.append(dq.reshape(T, QW))
        dks.append(dk.reshape(T, QW))
        dvs.append(dv.reshape(T, QW))
    dproj_a, db_a = _assemble([[dq_a], [dk_a], [dv_a]], "dproj_a")
    dproj_b, db_b = _assemble([dqs, dks, dvs], "dproj_b")

    dh1_a, = _matmul(dproj_a, w_in_ta, "nn", [F32], ident, tk=n_a, name="in_proj_a_bwd", **tiles)
    dh1, = _matmul(dproj_b, w_in_tb, "nn", [F32], add, tk=3 * QW, tile_ins=[dh1_a], name="in_proj_b_bwd", **tiles)
    dw_in_ta, = _matmul(dproj_a, h1, "tn", [BF16], ident, tm=n_a, tn=1024, tk=512, name="in_proj_a_wgrad")
    dw_in_tb, = _matmul(dproj_b, h1, "tn", [BF16], ident, name="in_proj_b_wgrad", **wtiles)
    dw_in_t = jnp.concatenate([dw_in_ta, dw_in_tb], axis=0)
    dx, _, dg_attn, slots_in = _norm_bwd(dh1, xs, g_attn, dx2, "norm_attn_bwd", exchange=_scatter_rows([dw_in_t]))

    g_w_in = _sum_slots(slots_in, "sum_w_in_grads").T
    g_w_out = _sum_slots(slots_o, "sum_w_out_grads")
    g_w_1 = jnp.concatenate([_sum_slots(slots_1a, "sum_w_1a_grads"), _sum_slots(slots_1b, "sum_w_1b_grads")]).T
    g_w_2 = _sum_slots(slots_2, "sum_w_2_grads")

    small_w = [g_attn, b_in, sinks_a, g_out_a, g_out_b, g_mlp, g_final]
    small_m = [m_g_attn, m_b_in, m_sinks_a, m_g_out_a, m_g_out_b, m_g_mlp, m_g_final]
    small_v = [v_g_attn, v_b_in, v_sinks_a, v_g_out_a, v_g_out_b, v_g_mlp, v_g_final]
    small_g = [dg_attn, jnp.concatenate([db_a, db_b], axis=1), dsinks, dg_out_a, dg_out_b, dg_mlp, dg_final]
    summed = _sum_over_devices(_pack_small(small_g + [loss_dev[:, :1]]))
    shapes = [w.shape for w in small_w]
    *g_small, loss = _unpack_small(summed, shapes + [()])

    big = [
        _adamw(w_in[0], g_w_in, m_w_in[0], v_w_in[0], "adamw_w_in"),
        _adamw(w_out[0], g_w_out, m_w_out[0], v_w_out[0], "adamw_w_out"),
        _adamw(w_1[0], g_w_1, m_w_1[0], v_w_1[0], "adamw_w_1"),
        _adamw(w_2[0], g_w_2, m_w_2[0], v_w_2[0], "adamw_w_2"),
    ]
    g_packed = _pack_small(g_small)
    small = _adamw(_pack_small(small_w), g_packed, _pack_small(small_m), _pack_small(small_v), "adamw_small")
    small = [_unpack_small(s, shapes) for s in small]

    def ordered(small_list, big_list):
        s = list(small_list)
        return [s[0], big_list[0][None], s[1], s[2], s[3], s[4], big_list[1][None], s[5],
                big_list[2][None], big_list[3][None], s[6]]

    grads = ordered(g_small, [g_w_in, g_w_out, g_w_1, g_w_2])
    deltas = ordered(small[0], [b[0] for b in big])
    new_m = ordered(small[1], [b[1] for b in big])
    new_v = ordered(small[2], [b[2] for b in big])
    return (loss, dx[None], *grads, *deltas, *new_m, *new_v)
```

```python
import numpy as np
import jax
import jax.numpy as jnp
from jax import lax
from jax.experimental import pallas as pl
from jax.experimental.pallas import tpu as pltpu

F32 = jnp.float32
BF16 = jnp.bfloat16

HEAD_DIM = 64
N_HEADS = 16
KV_HEADS_A = 2
BLOCK = 128
WINDOW_A = 128
DILATED_BRANCHES = ((128, 1), (512, 4), (2048, 16))
EPS = 1e-5
NEG_INF = -1e30
N_DEV = 8

ADAM_LR = 0.001
ADAM_B1 = 0.9
ADAM_B2 = 0.999
ADAM_EPS = 1e-08
ADAM_WD = 0.01
ADAM_STEP = 10

VMEM_LIMIT_BYTES = 56 * 1024 * 1024
MESH = pl.DeviceIdType.MESH
ANY = pl.BlockSpec(memory_space=pl.ANY)

NN = (((1,), (0,)), ((), ()))
NT = (((1,), (1,)), ((), ()))
TN = (((0,), (0,)), ((), ()))


def _dot(a, b, dims):
    return lax.dot_general(a, b, dims, preferred_element_type=F32)


def _params(*sem):
    return pltpu.CompilerParams(dimension_semantics=sem, vmem_limit_bytes=VMEM_LIMIT_BYTES)


RELAY_AT = 0.6


class _Exchange:
    def __init__(self, ins, out_shapes, n_remote, n_local, copies, aliases=None, relay=None):
        self.ins, self.out_shapes = list(ins), list(out_shapes)
        self.n_remote, self.n_local = n_remote, n_local
        self.copies = copies
        self.relay = relay
        self.aliases = aliases or {}

    def start(self, refs):
        local, sends, _ = self.copies(*refs)
        for cp in local + sends:
            cp.start()

    def middle(self, refs):
        arrived, onward = self.relay(*refs)
        for got, cp in zip(arrived, onward):
            got.wait_recv()
            cp.start()

    def finish(self, refs):
        local, sends, recvs = self.copies(*refs)
        for cp in recvs:
            cp.wait_recv()
        for cp in sends:
            cp.wait_send()
        for cp in local:
            cp.wait()
        if self.relay:
            for cp in self.relay(*refs)[1]:
                cp.wait_send()


class _Ride:
    def __init__(self, ex, n_in, n_out, n_scratch):
        self.ex = ex
        self.n = (n_in, n_out, n_scratch)
        self.args = ex.ins if ex else []
        self.in_specs = [ANY] * len(self.args)
        self.out_shapes = ex.out_shapes if ex else []
        self.out_specs = [ANY] * len(self.out_shapes)
        self.scratch = [pltpu.SemaphoreType.DMA((ex.n_remote,)), pltpu.SemaphoreType.DMA((ex.n_remote,)),
                        pltpu.SemaphoreType.DMA((max(ex.n_local, 1),))] if ex else []
        self.aliases = {n_in + i: n_out + o for i, o in ex.aliases.items()} if ex else {}

    def split(self, refs):
        n_in, n_out, n_scratch = self.n
        a = n_in
        b = a + len(self.args)
        c = b + n_out
        d = c + len(self.out_shapes)
        e = d + n_scratch
        return refs[:a], refs[b:c], refs[d:e], (refs[a:b], refs[c:d], *refs[e:])

    def around(self, step, n_steps, exrefs, compute):
        if self.ex is None:
            compute()
            return

        @pl.when(step == 0)
        def _():
            self.ex.start(exrefs)

        compute()

        if self.ex.relay:
            @pl.when(step == int(RELAY_AT * (n_steps - 1)))
            def _():
                self.ex.middle(exrefs)

        @pl.when(step == n_steps - 1)
        def _():
            self.ex.finish(exrefs)


def _matmul(a, b, dims, out_dtypes, epilogue, *, tm, tn, tk, name, tile_ins=(), row_ins=(), exchange=None):
    if dims == "tn":
        K, M = a.shape
    else:
        M, K = a.shape
    N = b.shape[0] if dims == "nt" else b.shape[1]
    tm, tn, tk = min(tm, M), min(tn, N), min(tk, K)
    assert M % tm == 0 and N % tn == 0 and K % tk == 0, (name, M, N, K, tm, tn, tk)
    grid = (M // tm, N // tn, K // tk)
    nk = grid[2]
    n_tile, n_row, n_out = len(tile_ins), len(row_ins), len(out_dtypes)
    dn = {"nn": NN, "nt": NT, "tn": TN}[dims]
    ride = _Ride(exchange, 2 + n_tile + n_row, n_out, 1 if nk > 1 else 0)

    def kern(*refs):
        ins, out_refs, scratch, exrefs = ride.split(refs)
        a_ref, b_ref = ins[:2]
        tile_refs = ins[2:2 + n_tile]
        row_refs = ins[2 + n_tile:]
        ids = [pl.program_id(d) for d in range(3)]

        def finish(acc):
            outs = epilogue(acc, *[r[...] for r in tile_refs], *[r[...] for r in row_refs])
            for o_ref, o in zip(out_refs, outs):
                o_ref[...] = o.astype(o_ref.dtype)

        def compute():
            if nk == 1:
                finish(_dot(a_ref[...], b_ref[...], dn))
                return
            acc_ref = scratch[0]

            @pl.when(ids[2] == 0)
            def _():
                acc_ref[...] = jnp.zeros_like(acc_ref)

            acc_ref[...] += _dot(a_ref[...], b_ref[...], dn)

            @pl.when(ids[2] == nk - 1)
            def _():
                finish(acc_ref[...])

        ride.around((ids[0] * grid[1] + ids[1]) * grid[2] + ids[2], grid[0] * grid[1] * grid[2], exrefs, compute)

    if dims == "tn":
        a_spec = pl.BlockSpec((tk, tm), lambda i, j, k: (k, i))
    else:
        a_spec = pl.BlockSpec((tm, tk), lambda i, j, k: (i, k))
    if dims == "nt":
        b_spec = pl.BlockSpec((tn, tk), lambda i, j, k: (j, k))
    else:
        b_spec = pl.BlockSpec((tk, tn), lambda i, j, k: (k, j))
    tile_spec = pl.BlockSpec((tm, tn), lambda i, j, k: (i, j))
    row_spec = pl.BlockSpec((1, tn), lambda i, j, k: (0, j))
    sem = ("arbitrary",) * 3 if exchange else ("parallel", "parallel", "arbitrary")
    return pl.pallas_call(
        kern,
        name=name,
        grid=grid,
        in_specs=[a_spec, b_spec] + [tile_spec] * n_tile + [row_spec] * n_row + ride.in_specs,
        out_specs=[tile_spec] * n_out + ride.out_specs,
        out_shape=[jax.ShapeDtypeStruct((M, N), dt) for dt in out_dtypes] + ride.out_shapes,
        scratch_shapes=([pltpu.VMEM((tm, tn), F32)] if nk > 1 else []) + ride.scratch,
        input_output_aliases=ride.aliases,
        compiler_params=_params(*sem),
    )(a, b, *tile_ins, *row_ins, *ride.args)


PROJ_ROWS = 256


def _proj_views(a, w_t, bias, dils, name, exchange=None):
    T, K = a.shape
    N = w_t.shape[0]
    ride = _Ride(exchange, 3, len(dils), 1)

    def kern(*refs):
        (a_ref, w_ref, b_ref), outs, (scr,), exrefs = ride.split(refs)

        def compute():
            acc = _dot(a_ref[...], w_ref[...], NT) + b_ref[...]
            for out_ref, dil in zip(outs, dils):
                _to_class_order(acc, scr, out_ref, dil)

        ride.around(pl.program_id(0), T // PROJ_ROWS, exrefs, compute)

    return pl.pallas_call(
        kern, name=name, grid=(T // PROJ_ROWS,),
        in_specs=[pl.BlockSpec((PROJ_ROWS, K), lambda i: (i, 0)), pl.BlockSpec((N, K), lambda i: (0, 0)),
                  pl.BlockSpec((1, N), lambda i: (0, 0))] + ride.in_specs,
        out_specs=[_view_spec(PROJ_ROWS, N, d) for d in dils] + ride.out_specs,
        out_shape=[jax.ShapeDtypeStruct((T // d, d * N), BF16) for d in dils] + ride.out_shapes,
        scratch_shapes=[_regroup_scratch(PROJ_ROWS, N)] + ride.scratch,
        input_output_aliases=ride.aliases,
        compiler_params=_params("arbitrary"),
    )(a, w_t, bias, *ride.args)


ROWS = 256
MIX_ROWS = 128


def _rstd(xv):
    return lax.rsqrt(jnp.mean(xv * xv, axis=-1, keepdims=True) + EPS)


def _norm_fwd(x, g, name, exchange=None):
    T, D = x.shape
    ride = _Ride(exchange, 2, 1, 0)

    def kern(*refs):
        (x_ref, g_ref), (h_ref,), _, exrefs = ride.split(refs)

        def compute():
            xv = x_ref[...]
            h_ref[...] = ((xv * _rstd(xv)) * g_ref[...]).astype(h_ref.dtype)

        ride.around(pl.program_id(0), T // ROWS, exrefs, compute)

    row = pl.BlockSpec((ROWS, D), lambda i: (i, 0))
    return pl.pallas_call(
        kern, name=name, grid=(T // ROWS,),
        in_specs=[row, pl.BlockSpec((1, D), lambda i: (0, 0))] + ride.in_specs,
        out_specs=[row] + ride.out_specs,
        out_shape=[jax.ShapeDtypeStruct((T, D), BF16)] + ride.out_shapes,
        scratch_shapes=ride.scratch, input_output_aliases=ride.aliases,
        compiler_params=_params("arbitrary"),
    )(x, g, *ride.args)


def _norm_bwd(dh, x, g, res, name, exchange=None):
    T, D = x.shape
    ride = _Ride(exchange, 4, 3, 0)

    def kern(*refs):
        (dh_ref, x_ref, g_ref, res_ref), (dx_ref, dxb_ref, dg_ref), _, exrefs = ride.split(refs)

        def compute():
            @pl.when(pl.program_id(0) == 0)
            def _():
                dg_ref[...] = jnp.zeros_like(dg_ref)

            xv = x_ref[...]
            r = _rstd(xv)
            xn = xv * r
            dhv = dh_ref[...]
            dg_ref[...] += jnp.sum(dhv * xn, axis=0, keepdims=True)
            t = dhv * g_ref[...]
            dx = res_ref[...] + r * (t - xn * jnp.mean(t * xn, axis=-1, keepdims=True))
            dx_ref[...] = dx
            dxb_ref[...] = dx.astype(BF16)

        ride.around(pl.program_id(0), T // ROWS, exrefs, compute)

    row = pl.BlockSpec((ROWS, D), lambda i: (i, 0))
    vec = pl.BlockSpec((1, D), lambda i: (0, 0))
    return pl.pallas_call(
        kern, name=name, grid=(T // ROWS,),
        in_specs=[row, row, vec, row] + ride.in_specs,
        out_specs=[row, row, vec] + ride.out_specs,
        out_shape=[jax.ShapeDtypeStruct((T, D), F32), jax.ShapeDtypeStruct((T, D), BF16),
                   jax.ShapeDtypeStruct((1, D), F32)] + ride.out_shapes,
        scratch_shapes=ride.scratch, input_output_aliases=ride.aliases,
        compiler_params=_params("arbitrary"),
    )(dh, x, g, res, *ride.args)


def _loss_head(x3, tgt, g):
    T, D = x3.shape

    def kern(x_ref, t_ref, g_ref, dx_ref, dxb_ref, dg_ref, loss_ref):
        @pl.when(pl.program_id(0) == 0)
        def _():
            dg_ref[...] = jnp.zeros_like(dg_ref)
            loss_ref[...] = jnp.zeros_like(loss_ref)

        xv = x_ref[...]
        gv = g_ref[...]
        r = _rstd(xv)
        xn = xv * r
        err = xn * gv - t_ref[...]
        per_tok = jnp.mean(err * err, axis=-1, keepdims=True)
        loss_ref[...] += 0.5 * jnp.sum(per_tok, axis=0, keepdims=True)
        dy = err * (1.0 / D)
        dg_ref[...] += jnp.sum(dy * xn, axis=0, keepdims=True)
        t = dy * gv
        dx = r * (t - xn * jnp.mean(t * xn, axis=-1, keepdims=True))
        dx_ref[...] = dx
        dxb_ref[...] = dx.astype(BF16)

    row = pl.BlockSpec((ROWS, D), lambda i: (i, 0))
    vec = pl.BlockSpec((1, D), lambda i: (0, 0))
    return pl.pallas_call(
        kern, name="loss_head", grid=(T // ROWS,),
        in_specs=[row, row, vec],
        out_specs=[row, row, vec, pl.BlockSpec((1, 128), lambda i: (0, 0))],
        out_shape=[jax.ShapeDtypeStruct((T, D), F32), jax.ShapeDtypeStruct((T, D), BF16),
                   jax.ShapeDtypeStruct((1, D), F32), jax.ShapeDtypeStruct((1, 128), F32)],
        compiler_params=_params("arbitrary"),
    )(x3, tgt, g)


def _spread_matrix():
    head_of_lane = np.arange(N_HEADS * HEAD_DIM) // HEAD_DIM
    return jnp.asarray(np.arange(N_HEADS)[:, None] == head_of_lane[None, :], dtype=BF16)


def _pieces(v, n):
    out = []
    for _ in range(n):
        piece = v.astype(BF16)
        out.append(piece)
        v = v - piece.astype(F32)
    return out


def _spread(v, spread):
    return sum(_dot(p, spread, NN) for p in _pieces(v, 3))


def _head_sums(v, spread):
    return sum(_dot(p, spread, NT) for p in _pieces(v, 2))


def _branch_weights(l1, l2, l3):
    lm = jnp.maximum(jnp.maximum(l1, l2), l3)
    e1, e2, e3 = jnp.exp(l1 - lm), jnp.exp(l2 - lm), jnp.exp(l3 - lm)
    inv = 1.0 / (e1 + e2 + e3)
    return e1 * inv, e2 * inv, e3 * inv


def _regroup_scratch(rows, width):
    return pltpu.VMEM((width // LANES, rows, LANES), F32)


def _to_token_order(view_ref, scr, dil):
    if dil == 1:
        return view_ref[...]
    n_l, w = view_ref.shape[0], view_ref.shape[1] // dil
    for r in range(dil):
        for cb in range(w // LANES):
            scr[cb, pl.ds(r, n_l, stride=dil), :] = view_ref[:, r * w + cb * LANES:r * w + (cb + 1) * LANES]
    return jnp.concatenate([scr[cb] for cb in range(w // LANES)], axis=1)


def _to_class_order(val, scr, view_ref, dil):
    if dil == 1:
        view_ref[...] = val.astype(view_ref.dtype)
        return
    n, w = val.shape
    for cb in range(w // LANES):
        scr[cb] = val[:, cb * LANES:(cb + 1) * LANES]
    for r in range(dil):
        for cb in range(w // LANES):
            view_ref[:, r * w + cb * LANES:r * w + (cb + 1) * LANES] = (
                scr[cb, pl.ds(r, n // dil, stride=dil), :].astype(view_ref.dtype))


def _view_spec(rows, width, dil):
    return pl.BlockSpec((rows // dil, dil * width), lambda i: (i, 0))


def _mix_fwd(oa, obs, lbs, ga, gb, dils):
    T, W = oa.shape

    def kern(oa_ref, o1, o2, o3, l1, l2, l3, ga_ref, gb_ref, sp_ref, mix_ref, *scr):
        sp = sp_ref[...]
        w1, w2, w3 = _branch_weights(l1[...], l2[...], l3[...])
        on = [_to_token_order(o, s, d) for o, s, d in zip((o1, o2, o3), scr, dils)]
        ob = _spread(w1, sp) * on[0] + _spread(w2, sp) * on[1] + _spread(w3, sp) * on[2]
        oav = oa_ref[...]
        mix_ref[:, :W] = ((oav * _rstd(oav)) * ga_ref[...]).astype(BF16)
        mix_ref[:, W:] = ((ob * _rstd(ob)) * gb_ref[...]).astype(BF16)

    row = pl.BlockSpec((MIX_ROWS, W), lambda i: (i, 0))
    per_head = pl.BlockSpec((MIX_ROWS, N_HEADS), lambda i: (i, 0))
    vec = pl.BlockSpec((1, W), lambda i: (0, 0))
    return pl.pallas_call(
        kern, name="mix_fwd", grid=(T // MIX_ROWS,),
        in_specs=[row] + [_view_spec(MIX_ROWS, W, d) for d in dils] + [per_head] * 3
        + [vec, vec, pl.BlockSpec((N_HEADS, W), lambda i: (0, 0))],
        out_specs=pl.BlockSpec((MIX_ROWS, 2 * W), lambda i: (i, 0)),
        out_shape=jax.ShapeDtypeStruct((T, 2 * W), BF16),
        scratch_shapes=[_regroup_scratch(MIX_ROWS, W)] * 3,
        compiler_params=_params("parallel"),
    )(oa, *obs, *lbs, ga, gb, _spread_matrix())


def _mix_bwd(dmix, oa, obs, lbs, ga, gb, dils):
    T, W = oa.shape

    def kern(dm_ref, oa_ref, o1, o2, o3, l1, l2, l3, ga_ref, gb_ref, sp_ref,
             doa_ref, da_ref, do1, do2, do3, d1, d2, d3, dga_ref, dgb_ref, *scr):
        @pl.when(pl.program_id(0) == 0)
        def _():
            dga_ref[...] = jnp.zeros_like(dga_ref)
            dgb_ref[...] = jnp.zeros_like(dgb_ref)

        sp = sp_ref[...]
        oav = oa_ref[...]
        r = _rstd(oav)
        on = oav * r
        dy = dm_ref[:, :W]
        dga_ref[...] += jnp.sum(dy * on, axis=0, keepdims=True)
        t = dy * ga_ref[...]
        doa = r * (t - on * jnp.mean(t * on, axis=-1, keepdims=True))
        doa_ref[...] = doa.astype(BF16)
        da_ref[...] = _head_sums(doa * oav, sp)
        w1, w2, w3 = _branch_weights(l1[...], l2[...], l3[...])
        s1, s2, s3 = _spread(w1, sp), _spread(w2, sp), _spread(w3, sp)
        on = [_to_token_order(o, sc, d) for o, sc, d in zip((o1, o2, o3), scr, dils)]
        ob = s1 * on[0] + s2 * on[1] + s3 * on[2]
        r = _rstd(ob)
        on = ob * r
        dy = dm_ref[:, W:]
        dgb_ref[...] += jnp.sum(dy * on, axis=0, keepdims=True)
        t = dy * gb_ref[...]
        dob = r * (t - on * jnp.mean(t * on, axis=-1, keepdims=True))
        c = _head_sums(dob * ob, sp)
        for do_ref, sn, sc, d in zip((do1, do2, do3), (s1, s2, s3), scr, dils):
            _to_class_order(sn * dob, sc, do_ref, d)
        d1[...] = w1 * c
        d2[...] = w2 * c
        d3[...] = w3 * c

    row = pl.BlockSpec((MIX_ROWS, W), lambda i: (i, 0))
    per_head = pl.BlockSpec((MIX_ROWS, N_HEADS), lambda i: (i, 0))
    vec = pl.BlockSpec((1, W), lambda i: (0, 0))
    bf = jax.ShapeDtypeStruct((T, W), BF16)
    ph = jax.ShapeDtypeStruct((T, N_HEADS), F32)
    vv = jax.ShapeDtypeStruct((1, W), F32)
    views = [_view_spec(MIX_ROWS, W, d) for d in dils]
    return pl.pallas_call(
        kern, name="mix_bwd", grid=(T // MIX_ROWS,),
        in_specs=[pl.BlockSpec((MIX_ROWS, 2 * W), lambda i: (i, 0)), row] + views + [per_head] * 3 + [vec, vec,
                  pl.BlockSpec((N_HEADS, W), lambda i: (0, 0))],
        out_specs=[row, per_head] + views + [per_head, per_head, per_head, vec, vec],
        out_shape=[bf, ph] + [jax.ShapeDtypeStruct(o.shape, F32) for o in obs] + [ph, ph, ph, vv, vv],
        scratch_shapes=[_regroup_scratch(MIX_ROWS, W)] * 3,
        compiler_params=_params("arbitrary"),
    )(dmix, oa, *obs, *lbs, ga, gb, _spread_matrix())


def _alibi_slopes(n):
    return np.asarray(2.0 ** (-8.0 * (np.arange(n) + 1) / n)).astype(np.float32)


def _band_bias(max_steps, step_dist):
    qi = np.arange(BLOCK)[None, :]
    kj = np.arange(BLOCK)[:, None]
    slopes = _alibi_slopes(N_HEADS)
    out = []
    for steps in (qi - kj, qi + BLOCK - kj):
        valid = (steps >= 0) & (steps <= max_steps)
        alibi = slopes[:, None, None] * (step_dist * steps).astype(np.float32)[None]
        out.append(np.where(valid[None], -alibi, np.float32(NEG_INF)).astype(np.float32))
    return jnp.asarray(np.stack(out))


class _AttnLayout:
    def __init__(self, dil, kv_heads, q_stride, q_off, k_stride, k_off, v_off):
        self.dil = dil
        self.kv_heads = kv_heads
        self.kw = kv_heads * HEAD_DIM
        self.rep = N_HEADS // kv_heads
        self.q_col = lambda r: r * q_stride + q_off
        self.k_col = lambda r: r * k_stride + k_off
        self.v_col = lambda r: r * k_stride + v_off


QW = N_HEADS * HEAD_DIM
LANES = 128


def _head_cols(h):
    return slice(h * HEAD_DIM, (h + 1) * HEAD_DIM)


def _attn_fwd(proj, bias, sinks, lay, name, exchange=None):
    L = proj.shape[0]
    nb = L // BLOCK
    kw, rep = lay.kw, lay.rep
    use_sinks = sinks is not None
    scale = HEAD_DIM ** -0.5
    ride = _Ride(exchange, 7 if use_sinks else 6, 2, 2)

    def kern(*refs):
        ins, (o_ref, l_ref), (sc_ref, pr_ref), exrefs = ride.split(refs)
        q_ref, kc_ref, kp_ref, vc_ref, vp_ref, b_ref = ins[:6]
        s_ref = ins[6] if use_sinks else None
        r, i = pl.program_id(0), pl.program_id(1)
        first = i == 0
        ride.around(r * nb + i, lay.dil * nb, exrefs,
                    lambda: compute(q_ref, kc_ref, kp_ref, vc_ref, vp_ref, b_ref, s_ref, o_ref, l_ref, first,
                                    sc_ref, pr_ref))

    def compute(q_ref, kc_ref, kp_ref, vc_ref, vp_ref, b_ref, s_ref, o_ref, l_ref, first, sc_ref, pr_ref):
        for h in range(N_HEADS):
            g = h // rep
            q = q_ref[:, _head_cols(h)]
            sc_ref[h, 0] = _dot(kc_ref[:, _head_cols(g)], q, NT) * scale + b_ref[0, h]
            s_p = _dot(kp_ref[:, _head_cols(g)], q, NT) * scale + b_ref[1, h]
            sc_ref[h, 1] = jnp.where(first, NEG_INF, s_p)
        inv = []
        for h in range(N_HEADS):
            s_c, s_p = sc_ref[h, 0], sc_ref[h, 1]
            m = jnp.maximum(jnp.max(s_c, axis=0, keepdims=True), jnp.max(s_p, axis=0, keepdims=True))
            if use_sinks:
                sink = s_ref[:, h:h + 1]
                m = jnp.maximum(m, sink)
            p_c = jnp.exp(s_c - m)
            p_p = jnp.exp(s_p - m)
            denom = jnp.sum(p_c, axis=0, keepdims=True) + jnp.sum(p_p, axis=0, keepdims=True)
            if use_sinks:
                denom = denom + jnp.exp(sink - m)
            pr_ref[h, 0] = p_c.astype(BF16)
            pr_ref[h, 1] = p_p.astype(BF16)
            l_ref[h:h + 1, :] = m + jnp.log(denom)
            inv.append(1.0 / denom)
        values_t = {}
        for pair in range(N_HEADS // 2):
            halves = []
            for h in (2 * pair, 2 * pair + 1):
                g = h // rep
                blk = slice(g // 2 * LANES, (g // 2 + 1) * LANES)
                if g // 2 not in values_t:
                    values_t[g // 2] = (vc_ref[:, blk].T, vp_ref[:, blk].T)
                vct, vpt = values_t[g // 2]
                rows = slice(g % 2 * HEAD_DIM, (g % 2 + 1) * HEAD_DIM)
                o_t = _dot(vct[rows], pr_ref[h, 0], NN) + _dot(vpt[rows], pr_ref[h, 1], NN)
                halves.append(o_t * inv[h])
            o_ref[:, pair * LANES:(pair + 1) * LANES] = jnp.concatenate(halves, axis=0).T

    prev = lambda i: jnp.maximum(i - 1, 0)
    in_specs = [
        pl.BlockSpec((BLOCK, QW), lambda r, i: (i, lay.q_col(r))),
        pl.BlockSpec((BLOCK, kw), lambda r, i: (i, lay.k_col(r))),
        pl.BlockSpec((BLOCK, kw), lambda r, i: (prev(i), lay.k_col(r))),
        pl.BlockSpec((BLOCK, kw), lambda r, i: (i, lay.v_col(r))),
        pl.BlockSpec((BLOCK, kw), lambda r, i: (prev(i), lay.v_col(r))),
        pl.BlockSpec((2, N_HEADS, BLOCK, BLOCK), lambda r, i: (0, 0, 0, 0)),
    ]
    args = [proj, proj, proj, proj, proj, bias]
    if use_sinks:
        in_specs.append(pl.BlockSpec((1, N_HEADS), lambda r, i: (0, 0)))
        args.append(sinks)
    out_specs = [pl.BlockSpec((BLOCK, QW), lambda r, i: (i, r)),
                 pl.BlockSpec((None, N_HEADS, BLOCK), lambda r, i: (r, 0, i))]
    out_shape = [jax.ShapeDtypeStruct((L, lay.dil * QW), F32), jax.ShapeDtypeStruct((lay.dil, N_HEADS, L), F32)]
    return pl.pallas_call(
        kern, name=name, grid=(lay.dil, nb),
        in_specs=in_specs + ride.in_specs, out_specs=out_specs + ride.out_specs,
        out_shape=out_shape + ride.out_shapes,
        scratch_shapes=[pltpu.VMEM((N_HEADS, 2, BLOCK, BLOCK), F32), pltpu.VMEM((N_HEADS, 2, BLOCK, BLOCK), BF16)]
        + ride.scratch,
        input_output_aliases=ride.aliases,
        compiler_params=_params("arbitrary", "arbitrary"),
    )(*args, *ride.args)


def _attn_bwd(proj, do, lse, dd, bias, sinks, lay, name, exchange=None):
    L = proj.shape[0]
    nb = L // BLOCK
    kw, rep, kvh = lay.kw, lay.rep, lay.kv_heads
    use_sinks = sinks is not None
    scale = HEAD_DIM ** -0.5
    ride = _Ride(exchange, 10 if use_sinks else 9, 4 if use_sinks else 3, 6)

    def kern(*refs):
        ins, outs, (ck_ref, cv_ref, *staged), exrefs = ride.split(refs)
        q_ref, kc_ref, kp_ref, vc_ref, vp_ref, do_ref, l_ref, d_ref, b_ref = ins[:9]
        s_ref = ins[9] if use_sinks else None
        dq_ref, dk_ref, dv_ref = outs[:3]
        ds_ref = outs[3] if use_sinks else None
        r = pl.program_id(0)
        i = pl.program_id(1)
        ride.around(r * (nb + 1) + i, lay.dil * (nb + 1), exrefs,
                    lambda: compute(q_ref, kc_ref, kp_ref, vc_ref, vp_ref, do_ref, l_ref, d_ref, b_ref, s_ref,
                                    dq_ref, dk_ref, dv_ref, ds_ref, ck_ref, cv_ref, r, i, *staged))

    def compute(q_ref, kc_ref, kp_ref, vc_ref, vp_ref, do_ref, l_ref, d_ref, b_ref, s_ref,
                dq_ref, dk_ref, dv_ref, ds_ref, ck_ref, cv_ref, r, i, sc_ref, dp_ref, pr_ref, dsc_ref):
        first = i == 0

        @pl.when(first)
        def _():
            ck_ref[...] = jnp.zeros_like(ck_ref)
            cv_ref[...] = jnp.zeros_like(cv_ref)

        if use_sinks:
            @pl.when(first & (r == 0))
            def _():
                ds_ref[...] = jnp.zeros_like(ds_ref)

        @pl.when(i < nb)
        def _():
            for h in range(N_HEADS):
                gs = _head_cols(h // rep)
                q = q_ref[:, _head_cols(h)]
                dov = do_ref[:, _head_cols(h)].astype(BF16)
                sc_ref[h, 0] = _dot(kc_ref[:, gs], q, NT) * scale + b_ref[0, h]
                s_p = _dot(kp_ref[:, gs], q, NT) * scale + b_ref[1, h]
                sc_ref[h, 1] = jnp.where(first, NEG_INF, s_p)
                dp_ref[h, 0] = _dot(vc_ref[:, gs], dov, NT)
                dp_ref[h, 1] = _dot(vp_ref[:, gs], dov, NT)
            for h in range(N_HEADS):
                lrow = l_ref[h:h + 1, :]
                drow = d_ref[h:h + 1, :]
                for c in range(2):
                    p = jnp.exp(sc_ref[h, c] - lrow)
                    pr_ref[h, c] = p.astype(BF16)
                    dsc_ref[h, c] = (p * (dp_ref[h, c] - drow) * scale).astype(BF16)
                if use_sinks:
                    ds_ref[h:h + 1, :] += -(jnp.exp(s_ref[:, h:h + 1] - lrow) * drow)
            dq_halves = []
            keys_t = {}
            for g in range(kvh):
                gs = _head_cols(g)
                blk = slice(g // 2 * LANES, (g // 2 + 1) * LANES)
                rows = slice(g % 2 * HEAD_DIM, (g % 2 + 1) * HEAD_DIM)
                if g // 2 not in keys_t:
                    keys_t[g // 2] = (kc_ref[:, blk].T, kp_ref[:, blk].T)
                kct, kpt = keys_t[g // 2][0][rows], keys_t[g // 2][1][rows]
                dkc = jnp.zeros((BLOCK, HEAD_DIM), F32)
                dkp = jnp.zeros((BLOCK, HEAD_DIM), F32)
                dvc = jnp.zeros((BLOCK, HEAD_DIM), F32)
                dvp = jnp.zeros((BLOCK, HEAD_DIM), F32)
                for h in range(g * rep, (g + 1) * rep):
                    q = q_ref[:, _head_cols(h)]
                    dov = do_ref[:, _head_cols(h)].astype(BF16)
                    ds_c, ds_p = dsc_ref[h, 0], dsc_ref[h, 1]
                    dkc += _dot(ds_c, q, NN)
                    dkp += _dot(ds_p, q, NN)
                    dvc += _dot(pr_ref[h, 0], dov, NN)
                    dvp += _dot(pr_ref[h, 1], dov, NN)
                    dq_halves.append(_dot(kct, ds_c, NN) + _dot(kpt, ds_p, NN))
                    if h % 2:
                        dq_ref[:, (h // 2) * LANES:(h // 2 + 1) * LANES] = jnp.concatenate(dq_halves, axis=0).T
                        dq_halves = []
                dk_ref[:, gs] = ck_ref[:, gs] + dkp
                dv_ref[:, gs] = cv_ref[:, gs] + dvp
                ck_ref[:, gs] = dkc
                cv_ref[:, gs] = dvc

        @pl.when(i == nb)
        def _():
            dk_ref[...] = ck_ref[...]
            dv_ref[...] = cv_ref[...]
            if use_sinks:
                @pl.when(r == lay.dil - 1)
                def _():
                    ds_ref[...] = jnp.broadcast_to(jnp.sum(ds_ref[...], axis=1, keepdims=True), ds_ref.shape)

    cur = lambda i: jnp.minimum(i, nb - 1)
    prev = lambda i: jnp.maximum(jnp.minimum(i, nb - 1) - 1, 0)
    done = lambda i: jnp.maximum(i - 1, 0)
    qspec = lambda col: pl.BlockSpec((BLOCK, QW), lambda r, i: (cur(i), col(r)))
    per_head = pl.BlockSpec((None, N_HEADS, BLOCK), lambda r, i: (r, 0, cur(i)))
    in_specs = [
        qspec(lay.q_col),
        pl.BlockSpec((BLOCK, kw), lambda r, i: (cur(i), lay.k_col(r))),
        pl.BlockSpec((BLOCK, kw), lambda r, i: (prev(i), lay.k_col(r))),
        pl.BlockSpec((BLOCK, kw), lambda r, i: (cur(i), lay.v_col(r))),
        pl.BlockSpec((BLOCK, kw), lambda r, i: (prev(i), lay.v_col(r))),
        qspec(lambda r: r), per_head, per_head,
        pl.BlockSpec((2, N_HEADS, BLOCK, BLOCK), lambda r, i: (0, 0, 0, 0)),
    ]
    args = [proj, proj, proj, proj, proj, do, lse, dd, bias]
    out_specs = [
        qspec(lambda r: r),
        pl.BlockSpec((BLOCK, kw), lambda r, i: (done(i), r)),
        pl.BlockSpec((BLOCK, kw), lambda r, i: (done(i), r)),
    ]
    dkv_shape = jax.ShapeDtypeStruct((L, lay.dil * kw), F32)
    out_shape = [jax.ShapeDtypeStruct((L, lay.dil * QW), F32), dkv_shape, dkv_shape]
    if use_sinks:
        in_specs.append(pl.BlockSpec((1, N_HEADS), lambda r, i: (0, 0)))
        args.append(sinks)
        out_specs.append(pl.BlockSpec((N_HEADS, LANES), lambda r, i: (0, 0)))
        out_shape.append(jax.ShapeDtypeStruct((N_HEADS, LANES), F32))
    return pl.pallas_call(
        kern, name=name, grid=(lay.dil, nb + 1),
        in_specs=in_specs + ride.in_specs, out_specs=out_specs + ride.out_specs,
        out_shape=out_shape + ride.out_shapes,
        scratch_shapes=[pltpu.VMEM((BLOCK, kw), F32), pltpu.VMEM((BLOCK, kw), F32)]
        + [pltpu.VMEM((N_HEADS, 2, BLOCK, BLOCK), dt) for dt in (F32, F32, BF16, BF16)] + ride.scratch,
        input_output_aliases=ride.aliases,
        compiler_params=_params("arbitrary", "arbitrary"),
    )(*args, *ride.args)


def _assemble(groups, name, dils=(1,)):
    T = groups[0][0].shape[0] * dils[0]
    widths = [g[0].shape[1] // dils[0] for g in groups]
    total = sum(widths)
    flat = [a for g in groups for a in g]
    member_dils = [d for g in groups for d in dils[:len(g)]]

    def kern(*refs):
        ins = refs[:len(flat)]
        out_ref, cs_ref = refs[len(flat):len(flat) + 2]
        scr = refs[len(flat) + 2:]

        @pl.when(pl.program_id(0) == 0)
        def _():
            cs_ref[...] = jnp.zeros_like(cs_ref)

        pos = off = 0
        for g, w in zip(groups, widths):
            acc = _to_token_order(ins[pos], None, dils[0])
            for j in range(1, len(g)):
                acc = acc + _to_token_order(ins[pos + j], scr[j - 1], dils[j])
            pos += len(g)
            out_ref[:, off:off + w] = acc.astype(BF16)
            cs_ref[:, off:off + w] += jnp.sum(acc, axis=0, keepdims=True)
            off += w

    return pl.pallas_call(
        kern, name=name, grid=(T // ROWS,),
        in_specs=[_view_spec(ROWS, a.shape[1] // d, d) for a, d in zip(flat, member_dils)],
        out_specs=[pl.BlockSpec((ROWS, total), lambda i: (i, 0)), pl.BlockSpec((1, total), lambda i: (0, 0))],
        out_shape=[jax.ShapeDtypeStruct((T, total), BF16), jax.ShapeDtypeStruct((1, total), F32)],
        scratch_shapes=[_regroup_scratch(ROWS, max(widths))] * (len(dils) - 1),
        compiler_params=_params("arbitrary"),
    )(*flat)


def _adamw(w, g, m, v, name):
    R, C = w.shape
    rows = min(R, ROWS)
    assert R % rows == 0

    def kern(w_ref, g_ref, m_ref, v_ref, d_ref, nm_ref, nv_ref):
        gv = g_ref[...]
        mn = ADAM_B1 * m_ref[...] + (1.0 - ADAM_B1) * gv
        vn = ADAM_B2 * v_ref[...] + (1.0 - ADAM_B2) * jnp.square(gv)
        m_hat = mn / (1.0 - ADAM_B1 ** ADAM_STEP)
        v_hat = vn / (1.0 - ADAM_B2 ** ADAM_STEP)
        d_ref[...] = -ADAM_LR * (m_hat / (jnp.sqrt(v_hat) + ADAM_EPS) + ADAM_WD * w_ref[...])
        nm_ref[...] = mn
        nv_ref[...] = vn

    blk = pl.BlockSpec((rows, C), lambda i: (i, 0))
    shp = jax.ShapeDtypeStruct((R, C), F32)
    return pl.pallas_call(
        kern, name=name, grid=(R // rows,),
        in_specs=[blk] * 4, out_specs=[blk] * 3, out_shape=[shp] * 3,
        compiler_params=_params("parallel"),
    )(w, g, m, v)


def _sum_slots(slots, name):
    n, R, C = slots.shape
    SUM_ROWS = next(rows for rows in (128, 64, 32, 16) if R % rows == 0)

    def kern(s_ref, o_ref):
        acc = s_ref[0].astype(F32)
        for k in range(1, n):
            acc = acc + s_ref[k].astype(F32)
        o_ref[...] = acc

    return pl.pallas_call(
        kern, name=name, grid=(R // SUM_ROWS,),
        in_specs=[pl.BlockSpec((n, SUM_ROWS, C), lambda i: (0, i, 0))],
        out_specs=pl.BlockSpec((SUM_ROWS, C), lambda i: (i, 0)),
        out_shape=jax.ShapeDtypeStruct((R, C), F32),
        compiler_params=_params("parallel"),
    )(slots)


def _place():
    return lax.axis_index("x"), lax.axis_index("y"), lax.axis_index("c")


def _index(p):
    return 4 * p[0] + 2 * p[1] + p[2]


FLIPS = [(fx, fy, fc) for fx in (0, 1) for fy in (0, 1) for fc in (0, 1)][1:]


def _peer(me, flip):
    return tuple(1 - a if f else a for a, f in zip(me, flip))


def _gather_rows(shards, part=(0, 1), into=None):
    nw = len(shards)

    def plan(ins, outs, send_sems, recv_sems):
        x, y, c = me = _place()
        sibling = (x, y, 1 - c)
        chips = [(1 - x, y), (x, 1 - y), (1 - x, 1 - y)]

        def span(w):
            cnt = ins[w].shape[0] // part[1]
            return part[0] * cnt, cnt

        def rows(w, p):
            lo, cnt = span(w)
            return outs[w].at[pl.ds(_index(p) * ins[w].shape[0] + lo, cnt), :]

        def own(w):
            lo, cnt = span(w)
            return ins[w].at[pl.ds(lo, cnt), :]

        def copy(w, k, block, to):
            return pltpu.make_async_remote_copy(
                src_ref=own(w) if block is me else rows(w, block), dst_ref=rows(w, block),
                send_sem=send_sems.at[7 * w + k], recv_sem=recv_sems.at[7 * w + k],
                device_id=to, device_id_type=MESH)

        return me, sibling, chips, c, rows, own, copy

    def copies(ins, outs, send_sems, recv_sems, local_sems):
        me, sibling, chips, c, rows, own, copy = plan(ins, outs, send_sems, recv_sems)
        local = [pltpu.make_async_copy(own(w), rows(w, me), local_sems.at[w]) for w in range(nw)]
        sends, recvs = [], []
        for w in range(nw):
            sends.append(copy(w, 0, me, sibling))
            sends += [copy(w, 1 + j, me, (*chip, c)) for j, chip in enumerate(chips)]
            recvs.append(copy(w, 0, sibling, me))
            recvs += [copy(w, 4 + j, (*chip, 1 - c), me) for j, chip in enumerate(chips)]
        return local, sends, recvs

    def relay(ins, outs, send_sems, recv_sems, local_sems):
        me, sibling, chips, c, rows, own, copy = plan(ins, outs, send_sems, recv_sems)
        arrived = [copy(w, 1 + j, (*chip, c), me) for w in range(nw) for j, chip in enumerate(chips)]
        onward = [copy(w, 4 + j, (*chip, c), sibling) for w in range(nw) for j, chip in enumerate(chips)]
        return arrived, onward

    shapes = [jax.ShapeDtypeStruct((N_DEV * s.shape[0], s.shape[1]), s.dtype) for s in shards]
    aliases = {nw + w: w for w in range(nw)} if into else None
    return _Exchange(shards + (into or []), shapes, 7 * nw, nw, copies, aliases=aliases, relay=relay)


def _scatter_rows(parts, part=(0, 1)):
    nw = len(parts)

    def copies(ins, outs, send_sems, recv_sems, local_sems):
        me = _place()

        def src(w, owner):
            n = ins[w].shape[0] // N_DEV
            cnt = n // part[1]
            return ins[w].at[pl.ds(_index(owner) * n + part[0] * cnt, cnt), :]

        def copy(k, w, owner, sender, to):
            return pltpu.make_async_remote_copy(
                src_ref=src(w, owner), dst_ref=outs[w].at[_index(sender)],
                send_sem=send_sems.at[nw * k + w], recv_sem=recv_sems.at[nw * k + w],
                device_id=to, device_id_type=MESH)

        local = [pltpu.make_async_copy(src(w, me), outs[w].at[_index(me)], local_sems.at[w]) for w in range(nw)]
        peers = [_peer(me, flip) for flip in FLIPS]
        sends = [copy(k, w, peer, me, peer) for k, peer in enumerate(peers) for w in range(nw)]
        recvs = [copy(k, w, me, peer, me) for k, peer in enumerate(peers) for w in range(nw)]
        return local, sends, recvs

    shapes = [jax.ShapeDtypeStruct((N_DEV, p.shape[0] // N_DEV // part[1], p.shape[1]), p.dtype) for p in parts]
    return _Exchange(parts, shapes, 7 * nw, nw, copies)


def _sum_over_devices(v):
    shape = v.shape

    def body(v_ref, sum_ref, all_ref, send_sems, recv_sems):
        me = _place()
        all_ref[_index(me)] = v_ref[...]
        sends = []
        for k, flip in enumerate(FLIPS):
            peer = _peer(me, flip)
            sends.append(pltpu.make_async_remote_copy(
                src_ref=v_ref, dst_ref=all_ref.at[_index(me)],
                send_sem=send_sems.at[k], recv_sem=recv_sems.at[k], device_id=peer, device_id_type=MESH))
            sends[-1].start()
        for k, flip in enumerate(FLIPS):
            peer = _peer(me, flip)
            pltpu.make_async_remote_copy(
                src_ref=v_ref, dst_ref=all_ref.at[_index(peer)],
                send_sem=send_sems.at[k], recv_sem=recv_sems.at[k], device_id=peer, device_id_type=MESH).wait_recv()
        for cp in sends:
            cp.wait_send()
        acc = all_ref[0]
        for s in range(1, N_DEV):
            acc = acc + all_ref[s]
        sum_ref[...] = acc

    vmem = pl.BlockSpec(memory_space=pltpu.VMEM)
    return pl.pallas_call(
        body, name="sum_small_grads",
        in_specs=[vmem], out_specs=[vmem, vmem],
        out_shape=[jax.ShapeDtypeStruct(shape, F32), jax.ShapeDtypeStruct((N_DEV,) + shape, F32)],
        scratch_shapes=[pltpu.SemaphoreType.DMA((7,)), pltpu.SemaphoreType.DMA((7,))],
    )(v)[0]


SMALL_ROWS = 8


def _pack_small(vectors):
    padded = []
    for vec in vectors:
        vec = vec.reshape(-1)
        padded.append(jnp.pad(vec, (0, -vec.shape[0] % 128)))
    flat = jnp.concatenate(padded)
    flat = jnp.pad(flat, (0, -flat.shape[0] % (SMALL_ROWS * 128)))
    return flat.reshape(SMALL_ROWS, -1)


def _unpack_small(packed, shapes):
    flat = packed.reshape(-1)
    out, off = [], 0
    for shp in shapes:
        n = int(np.prod(shp))
        out.append(flat[off:off + n].reshape(shp))
        off += n + (-n % 128)
    return out


def kernel(x, g_attn, w_in, b_in, sinks_a, g_out_a, g_out_b, w_out, g_mlp, w_1, w_2, g_final, loss_target, m_g_attn, m_w_in, m_b_in, m_sinks_a, m_g_out_a, m_g_out_b, m_w_out, m_g_mlp, m_w_1, m_w_2, m_g_final, v_g_attn, v_w_in, v_b_in, v_sinks_a, v_g_out_a, v_g_out_b, v_w_out, v_g_mlp, v_w_1, v_w_2, v_g_final):
    xs, tgt = x[0], loss_target[0]
    T, D = xs.shape
    n_a = QW + 2 * KV_HEADS_A * HEAD_DIM
    g_fin = g_final.reshape(1, D)

    shards = [w_in[0].T.astype(BF16), w_out[0].astype(BF16), w_1[0].T.astype(BF16), w_2[0].astype(BF16)]
    ident = lambda acc: (acc,)
    add = lambda acc, other: (acc + other,)
    tiles = dict(tm=512, tn=1024)

    h1, w_in_t = _norm_fwd(xs, g_attn, "norm_attn", exchange=_gather_rows(shards[:1]))
    w_in_ta, w_in_tb = w_in_t[:n_a], w_in_t[n_a:]
    proj_a, = _matmul(h1, w_in_ta, "nt", [BF16], add, tm=512, tn=n_a, tk=D, row_ins=[b_in[:, :n_a]], name="proj_a")
    dils = [dil for _, dil in DILATED_BRANCHES]
    *proj_b, w_o = _proj_views(h1, w_in_tb, b_in[:, n_a:], dils, "proj_b", exchange=_gather_rows(shards[1:2]))

    lay_a = _AttnLayout(1, KV_HEADS_A, 0, 0, 0, QW // (KV_HEADS_A * HEAD_DIM), QW // (KV_HEADS_A * HEAD_DIM) + 1)
    bias_a = _band_bias(WINDOW_A - 1, 1)
    o_a, l_a, w_1_t = _attn_fwd(proj_a, bias_a, sinks_a, lay_a, "attn_a_fwd",
                                exchange=_gather_rows(shards[2:3], part=(0, 4)))
    branches = []
    for n, (window, dil) in enumerate(DILATED_BRANCHES):
        lay = _AttnLayout(dil, N_HEADS, 3, 0, 3, 1, 2)
        bias = _band_bias(window // dil, dil)
        ride = _gather_rows(shards[2:3], part=(n + 1, 4), into=[w_1_t])
        o, lse, w_1_t = _attn_fwd(proj_b[n], bias, None, lay, f"attn_b{dil}_fwd", exchange=ride)
        branches.append((lay, bias, proj_b[n], o, lse))
    o_b = [br[3] for br in branches]
    l_b = [br[4].transpose(2, 0, 1).reshape(T, N_HEADS) for br in branches]

    mix = _mix_fwd(o_a, o_b, l_b, g_out_a, g_out_b, dils)
    x2, = _matmul(mix, w_o, "nn", [F32], add, tk=D, tile_ins=[xs], name="out_proj", **tiles)
    h2, = _norm_fwd(x2, g_mlp, "norm_mlp")

    def relu_sq(acc):
        u = jnp.maximum(acc, 0.0)
        return u, u * u

    u, u_sq, w_2_f = _matmul(h2, w_1_t, "nt", [BF16, BF16], relu_sq, tk=D, name="mlp_up",
                             exchange=_gather_rows(shards[3:]), **tiles)
    x3, = _matmul(u_sq, w_2_f, "nn", [F32], add, tk=2048, tile_ins=[x2], name="mlp_down", **tiles)

    dx3, dx3_b, dg_final, loss_dev = _loss_head(x3, tgt, g_fin)

    d_pre, = _matmul(dx3_b, w_2_f, "nt", [BF16], lambda acc, uu: (acc * (2.0 * uu.astype(F32)),),
                     tk=D, tile_ins=[u], name="mlp_down_bwd", **tiles)
    wtiles = dict(tm=1024, tn=1024, tk=1024)
    dw_2, = _matmul(u_sq, dx3_b, "tn", [BF16], ident, name="mlp_down_wgrad", **wtiles)
    dh2, slots_2 = _matmul(d_pre, w_1_t, "nn", [F32], ident, tk=2048, name="mlp_up_bwd", exchange=_scatter_rows([dw_2]),
                           **tiles)
    dw_1_t, = _matmul(d_pre, h2, "tn", [BF16], ident, name="mlp_up_wgrad", **wtiles)
    dx2, dx2_b, dg_mlp = _norm_bwd(dh2, x2, g_mlp, dx3, "norm_mlp_bwd")

    dmix, = _matmul(dx2_b, w_o, "nt", [F32], ident, tk=D, name="out_proj_bwd", **tiles)
    dw_o, = _matmul(mix, dx2_b, "tn", [BF16], ident, name="out_proj_wgrad", **wtiles)
    do_a, dd_a, do1, do2, do3, dd1, dd2, dd3, dg_out_a, dg_out_b = _mix_bwd(dmix, o_a, o_b, l_b, g_out_a, g_out_b,
                                                                                  dils)

    by_class = lambda d, dil: d.reshape(T // dil, dil, N_HEADS).transpose(1, 2, 0)
    dq_a, dk_a, dv_a, dsinks, slots_1a = _attn_bwd(proj_a, do_a, l_a, by_class(dd_a, 1), bias_a, sinks_a, lay_a,
                                                   "attn_a_bwd", exchange=_scatter_rows([dw_1_t], part=(0, 2)))
    dsinks = dsinks[:, 0].reshape(1, N_HEADS)
    dqs, dks, dvs = [], [], []
    rides = [_scatter_rows([dw_1_t], part=(1, 2)), _scatter_rows([dw_o]), None]
    for (lay, bias, view, _, lse), do_n, dd_n, ride in zip(branches, (do1, do2, do3), (dd1, dd2, dd3), rides):
        dq, dk, dv, *got = _attn_bwd(view, do_n, lse, by_class(dd_n, lay.dil), bias, None, lay,
                                     f"attn_b{lay.dil}_bwd", exchange=ride)
        if lay.dil == 1:
            slots_1b, = got
        elif ride:
            slots_o, = got
        dqs.append(dq)
        dks.append(dk)
        dvs.append(dv)
    dproj_a, db_a = _assemble([[dq_a], [dk_a], [dv_a]], "dproj_a")
    dproj_b, db_b = _assemble([dqs, dks, dvs], "dproj_b", dils)

    dw_in_ta, = _matmul(dproj_a, h1, "tn", [BF16], ident, tm=n_a, tn=1024, tk=512, name="in_proj_a_wgrad")
    dw_in_tb, = _matmul(dproj_b, h1, "tn", [BF16], ident, name="in_proj_b_wgrad", **wtiles)
    dw_in_t = jnp.concatenate([dw_in_ta, dw_in_tb], axis=0)
    dh1_a, slots_in_a = _matmul(dproj_a, w_in_ta, "nn", [F32], ident, tk=n_a, name="in_proj_a_bwd",
                                exchange=_scatter_rows([dw_in_t], part=(0, 2)), **tiles)
    dh1, slots_in_b = _matmul(dproj_b, w_in_tb, "nn", [F32], add, tk=3 * QW, tile_ins=[dh1_a], name="in_proj_b_bwd",
                              exchange=_scatter_rows([dw_in_t], part=(1, 2)), **tiles)
    dx, _, dg_attn = _norm_bwd(dh1, xs, g_attn, dx2, "norm_attn_bwd")

    g_w_in = jnp.concatenate([_sum_slots(slots_in_a, "sum_w_in_a_grads"), _sum_slots(slots_in_b, "sum_w_in_b_grads")]).T
    g_w_out = _sum_slots(slots_o, "sum_w_out_grads")
    g_w_1 = jnp.concatenate([_sum_slots(slots_1a, "sum_w_1a_grads"), _sum_slots(slots_1b, "sum_w_1b_grads")]).T
    g_w_2 = _sum_slots(slots_2, "sum_w_2_grads")

    small_w = [g_attn, b_in, sinks_a, g_out_a, g_out_b, g_mlp, g_final]
    small_m = [m_g_attn, m_b_in, m_sinks_a, m_g_out_a, m_g_out_b, m_g_mlp, m_g_final]
    small_v = [v_g_attn, v_b_in, v_sinks_a, v_g_out_a, v_g_out_b, v_g_mlp, v_g_final]
    small_g = [dg_attn, jnp.concatenate([db_a, db_b], axis=1), dsinks, dg_out_a, dg_out_b, dg_mlp, dg_final]
    summed = _sum_over_devices(_pack_small(small_g + [loss_dev[:, :1]]))
    shapes = [w.shape for w in small_w]
    *g_small, loss = _unpack_small(summed, shapes + [()])

    big = [
        _adamw(w_in[0], g_w_in, m_w_in[0], v_w_in[0], "adamw_w_in"),
        _adamw(w_out[0], g_w_out, m_w_out[0], v_w_out[0], "adamw_w_out"),
        _adamw(w_1[0], g_w_1, m_w_1[0], v_w_1[0], "adamw_w_1"),
        _adamw(w_2[0], g_w_2, m_w_2[0], v_w_2[0], "adamw_w_2"),
    ]
    g_packed = _pack_small(g_small)
    small = _adamw(_pack_small(small_w), g_packed, _pack_small(small_m), _pack_small(small_v), "adamw_small")
    small = [_unpack_small(s, shapes) for s in small]

    def ordered(small_list, big_list):
        s = list(small_list)
        return [s[0], big_list[0][None], s[1], s[2], s[3], s[4], big_list[1][None], s[5],
                big_list[2][None], big_list[3][None], s[6]]

    grads = ordered(g_small, [g_w_in, g_w_out, g_w_1, g_w_2])
    deltas = ordered(small[0], [b[0] for b in big])
    new_m = ordered(small[1], [b[1] for b in big])
    new_v = ordered(small[2], [b[2] for b in big])
    return (loss, dx[None], *grads, *deltas, *new_m, *new_v)
```

```python
import numpy as np
import jax
import jax.numpy as jnp
from jax import lax
from jax.experimental import pallas as pl
from jax.experimental.pallas import tpu as pltpu

F32 = jnp.float32
BF16 = jnp.bfloat16

HEAD_DIM = 64
N_HEADS = 16
KV_HEADS_A = 2
BLOCK = 128
WINDOW_A = 128
DILATED_BRANCHES = ((128, 1), (512, 4), (2048, 16))
EPS = 1e-5
NEG_INF = -1e30
N_DEV = 8

ADAM_LR = 0.001
ADAM_B1 = 0.9
ADAM_B2 = 0.999
ADAM_EPS = 1e-08
ADAM_WD = 0.01
ADAM_STEP = 10

VMEM_LIMIT_BYTES = 56 * 1024 * 1024
MESH = pl.DeviceIdType.MESH
ANY = pl.BlockSpec(memory_space=pl.ANY)

NN = (((1,), (0,)), ((), ()))
NT = (((1,), (1,)), ((), ()))
TN = (((0,), (0,)), ((), ()))


def _dot(a, b, dims):
    return lax.dot_general(a, b, dims, preferred_element_type=F32)


def _params(*sem):
    return pltpu.CompilerParams(dimension_semantics=sem, vmem_limit_bytes=VMEM_LIMIT_BYTES)


RELAY_AT = 0.6


class _Exchange:
    def __init__(self, ins, out_shapes, n_remote, n_local, copies, aliases=None, relay=None):
        self.ins, self.out_shapes = list(ins), list(out_shapes)
        self.n_remote, self.n_local = n_remote, n_local
        self.copies = copies
        self.relay = relay
        self.aliases = aliases or {}

    def start(self, refs):
        local, sends, _ = self.copies(*refs)
        for cp in local + sends:
            cp.start()

    def middle(self, refs):
        arrived, onward = self.relay(*refs)
        for got, cp in zip(arrived, onward):
            got.wait_recv()
            cp.start()

    def finish(self, refs):
        local, sends, recvs = self.copies(*refs)
        for cp in recvs:
            cp.wait_recv()
        for cp in sends:
            cp.wait_send()
        for cp in local:
            cp.wait()
        if self.relay:
            for cp in self.relay(*refs)[1]:
                cp.wait_send()


class _Ride:
    def __init__(self, ex, n_in, n_out, n_scratch):
        self.ex = ex
        self.n = (n_in, n_out, n_scratch)
        self.args = ex.ins if ex else []
        self.in_specs = [ANY] * len(self.args)
        self.out_shapes = ex.out_shapes if ex else []
        self.out_specs = [ANY] * len(self.out_shapes)
        self.scratch = [pltpu.SemaphoreType.DMA((ex.n_remote,)), pltpu.SemaphoreType.DMA((ex.n_remote,)),
                        pltpu.SemaphoreType.DMA((max(ex.n_local, 1),))] if ex else []
        self.aliases = {n_in + i: n_out + o for i, o in ex.aliases.items()} if ex else {}

    def split(self, refs):
        n_in, n_out, n_scratch = self.n
        a = n_in
        b = a + len(self.args)
        c = b + n_out
        d = c + len(self.out_shapes)
        e = d + n_scratch
        return refs[:a], refs[b:c], refs[d:e], (refs[a:b], refs[c:d], *refs[e:])

    def around(self, step, n_steps, exrefs, compute):
        if self.ex is None:
            compute()
            return

        @pl.when(step == 0)
        def _():
            self.ex.start(exrefs)

        compute()

        if self.ex.relay:
            @pl.when(step == int(RELAY_AT * (n_steps - 1)))
            def _():
                self.ex.middle(exrefs)

        @pl.when(step == n_steps - 1)
        def _():
            self.ex.finish(exrefs)


def _matmul(a, b, dims, out_dtypes, epilogue, *, tm, tn, tk, name, tile_ins=(), row_ins=(), exchange=None):
    if dims == "tn":
        K, M = a.shape
    else:
        M, K = a.shape
    N = b.shape[0] if dims == "nt" else b.shape[1]
    tm, tn, tk = min(tm, M), min(tn, N), min(tk, K)
    assert M % tm == 0 and N % tn == 0 and K % tk == 0, (name, M, N, K, tm, tn, tk)
    grid = (M // tm, N // tn, K // tk)
    nk = grid[2]
    n_tile, n_row, n_out = len(tile_ins), len(row_ins), len(out_dtypes)
    dn = {"nn": NN, "nt": NT, "tn": TN}[dims]
    ride = _Ride(exchange, 2 + n_tile + n_row, n_out, 1 if nk > 1 else 0)

    def kern(*refs):
        ins, out_refs, scratch, exrefs = ride.split(refs)
        a_ref, b_ref = ins[:2]
        tile_refs = ins[2:2 + n_tile]
        row_refs = ins[2 + n_tile:]
        ids = [pl.program_id(d) for d in range(3)]

        def finish(acc):
            outs = epilogue(acc, *[r[...] for r in tile_refs], *[r[...] for r in row_refs])
            for o_ref, o in zip(out_refs, outs):
                o_ref[...] = o.astype(o_ref.dtype)

        def compute():
            if nk == 1:
                finish(_dot(a_ref[...], b_ref[...], dn))
                return
            acc_ref = scratch[0]

            @pl.when(ids[2] == 0)
            def _():
                acc_ref[...] = jnp.zeros_like(acc_ref)

            acc_ref[...] += _dot(a_ref[...], b_ref[...], dn)

            @pl.when(ids[2] == nk - 1)
            def _():
                finish(acc_ref[...])

        ride.around((ids[0] * grid[1] + ids[1]) * grid[2] + ids[2], grid[0] * grid[1] * grid[2], exrefs, compute)

    if dims == "tn":
        a_spec = pl.BlockSpec((tk, tm), lambda i, j, k: (k, i))
    else:
        a_spec = pl.BlockSpec((tm, tk), lambda i, j, k: (i, k))
    if dims == "nt":
        b_spec = pl.BlockSpec((tn, tk), lambda i, j, k: (j, k))
    else:
        b_spec = pl.BlockSpec((tk, tn), lambda i, j, k: (k, j))
    tile_spec = pl.BlockSpec((tm, tn), lambda i, j, k: (i, j))
    row_spec = pl.BlockSpec((1, tn), lambda i, j, k: (0, j))
    sem = ("arbitrary",) * 3 if exchange else ("parallel", "parallel", "arbitrary")
    return pl.pallas_call(
        kern,
        name=name,
        grid=grid,
        in_specs=[a_spec, b_spec] + [tile_spec] * n_tile + [row_spec] * n_row + ride.in_specs,
        out_specs=[tile_spec] * n_out + ride.out_specs,
        out_shape=[jax.ShapeDtypeStruct((M, N), dt) for dt in out_dtypes] + ride.out_shapes,
        scratch_shapes=([pltpu.VMEM((tm, tn), F32)] if nk > 1 else []) + ride.scratch,
        input_output_aliases=ride.aliases,
        compiler_params=_params(*sem),
    )(a, b, *tile_ins, *row_ins, *ride.args)


PROJ_ROWS = 256


def _proj_views(a, w_t, bias, dils, name, exchange=None):
    T, K = a.shape
    N = w_t.shape[0]
    ride = _Ride(exchange, 3, len(dils), 1)

    def kern(*refs):
        (a_ref, w_ref, b_ref), outs, (scr,), exrefs = ride.split(refs)

        def compute():
            acc = _dot(a_ref[...], w_ref[...], NT) + b_ref[...]
            for out_ref, dil in zip(outs, dils):
                _to_class_order(acc, scr, out_ref, dil)

        ride.around(pl.program_id(0), T // PROJ_ROWS, exrefs, compute)

    return pl.pallas_call(
        kern, name=name, grid=(T // PROJ_ROWS,),
        in_specs=[pl.BlockSpec((PROJ_ROWS, K), lambda i: (i, 0)), pl.BlockSpec((N, K), lambda i: (0, 0)),
                  pl.BlockSpec((1, N), lambda i: (0, 0))] + ride.in_specs,
        out_specs=[_view_spec(PROJ_ROWS, N, d) for d in dils] + ride.out_specs,
        out_shape=[jax.ShapeDtypeStruct((T // d, d * N), BF16) for d in dils] + ride.out_shapes,
        scratch_shapes=[_regroup_scratch(PROJ_ROWS, N)] + ride.scratch,
        input_output_aliases=ride.aliases,
        compiler_params=_params("arbitrary"),
    )(a, w_t, bias, *ride.args)


ROWS = 256
MIX_ROWS = 128


def _rstd(xv):
    return lax.rsqrt(jnp.mean(xv * xv, axis=-1, keepdims=True) + EPS)


def _norm_fwd(x, g, name, exchange=None):
    T, D = x.shape
    ride = _Ride(exchange, 2, 1, 0)

    def kern(*refs):
        (x_ref, g_ref), (h_ref,), _, exrefs = ride.split(refs)

        def compute():
            xv = x_ref[...]
            h_ref[...] = ((xv * _rstd(xv)) * g_ref[...]).astype(h_ref.dtype)

        ride.around(pl.program_id(0), T // ROWS, exrefs, compute)

    row = pl.BlockSpec((ROWS, D), lambda i: (i, 0))
    return pl.pallas_call(
        kern, name=name, grid=(T // ROWS,),
        in_specs=[row, pl.BlockSpec((1, D), lambda i: (0, 0))] + ride.in_specs,
        out_specs=[row] + ride.out_specs,
        out_shape=[jax.ShapeDtypeStruct((T, D), BF16)] + ride.out_shapes,
        scratch_shapes=ride.scratch, input_output_aliases=ride.aliases,
        compiler_params=_params("arbitrary"),
    )(x, g, *ride.args)


def _norm_bwd(dh, x, g, res, name, exchange=None):
    T, D = x.shape
    ride = _Ride(exchange, 4, 3, 0)

    def kern(*refs):
        (dh_ref, x_ref, g_ref, res_ref), (dx_ref, dxb_ref, dg_ref), _, exrefs = ride.split(refs)

        def compute():
            @pl.when(pl.program_id(0) == 0)
            def _():
                dg_ref[...] = jnp.zeros_like(dg_ref)

            xv = x_ref[...]
            r = _rstd(xv)
            xn = xv * r
            dhv = dh_ref[...]
            dg_ref[...] += jnp.sum(dhv * xn, axis=0, keepdims=True)
            t = dhv * g_ref[...]
            dx = res_ref[...] + r * (t - xn * jnp.mean(t * xn, axis=-1, keepdims=True))
            dx_ref[...] = dx
            dxb_ref[...] = dx.astype(BF16)

        ride.around(pl.program_id(0), T // ROWS, exrefs, compute)

    row = pl.BlockSpec((ROWS, D), lambda i: (i, 0))
    vec = pl.BlockSpec((1, D), lambda i: (0, 0))
    return pl.pallas_call(
        kern, name=name, grid=(T // ROWS,),
        in_specs=[row, row, vec, row] + ride.in_specs,
        out_specs=[row, row, vec] + ride.out_specs,
        out_shape=[jax.ShapeDtypeStruct((T, D), F32), jax.ShapeDtypeStruct((T, D), BF16),
                   jax.ShapeDtypeStruct((1, D), F32)] + ride.out_shapes,
        scratch_shapes=ride.scratch, input_output_aliases=ride.aliases,
        compiler_params=_params("arbitrary"),
    )(dh, x, g, res, *ride.args)


def _loss_head(x3, tgt, g):
    T, D = x3.shape

    def kern(x_ref, t_ref, g_ref, dx_ref, dxb_ref, dg_ref, loss_ref):
        @pl.when(pl.program_id(0) == 0)
        def _():
            dg_ref[...] = jnp.zeros_like(dg_ref)
            loss_ref[...] = jnp.zeros_like(loss_ref)

        xv = x_ref[...]
        gv = g_ref[...]
        r = _rstd(xv)
        xn = xv * r
        err = xn * gv - t_ref[...]
        per_tok = jnp.mean(err * err, axis=-1, keepdims=True)
        loss_ref[...] += 0.5 * jnp.sum(per_tok, axis=0, keepdims=True)
        dy = err * (1.0 / D)
        dg_ref[...] += jnp.sum(dy * xn, axis=0, keepdims=True)
        t = dy * gv
        dx = r * (t - xn * jnp.mean(t * xn, axis=-1, keepdims=True))
        dx_ref[...] = dx
        dxb_ref[...] = dx.astype(BF16)

    row = pl.BlockSpec((ROWS, D), lambda i: (i, 0))
    vec = pl.BlockSpec((1, D), lambda i: (0, 0))
    return pl.pallas_call(
        kern, name="loss_head", grid=(T // ROWS,),
        in_specs=[row, row, vec],
        out_specs=[row, row, vec, pl.BlockSpec((1, 128), lambda i: (0, 0))],
        out_shape=[jax.ShapeDtypeStruct((T, D), F32), jax.ShapeDtypeStruct((T, D), BF16),
                   jax.ShapeDtypeStruct((1, D), F32), jax.ShapeDtypeStruct((1, 128), F32)],
        compiler_params=_params("arbitrary"),
    )(x3, tgt, g)


def _spread_matrix():
    head_of_lane = np.arange(N_HEADS * HEAD_DIM) // HEAD_DIM
    return jnp.asarray(np.arange(N_HEADS)[:, None] == head_of_lane[None, :], dtype=BF16)


def _pieces(v, n):
    out = []
    for _ in range(n):
        piece = v.astype(BF16)
        out.append(piece)
        v = v - piece.astype(F32)
    return out


def _spread(v, spread):
    return sum(_dot(p, spread, NN) for p in _pieces(v, 3))


def _head_sums(v, spread):
    return sum(_dot(p, spread, NT) for p in _pieces(v, 2))


def _branch_weights(l1, l2, l3):
    lm = jnp.maximum(jnp.maximum(l1, l2), l3)
    e1, e2, e3 = jnp.exp(l1 - lm), jnp.exp(l2 - lm), jnp.exp(l3 - lm)
    inv = 1.0 / (e1 + e2 + e3)
    return e1 * inv, e2 * inv, e3 * inv


def _regroup_scratch(rows, width):
    return pltpu.VMEM((width // LANES, rows, LANES), F32)


def _to_token_order(view_ref, scr, dil):
    if dil == 1:
        return view_ref[...]
    n_l, w = view_ref.shape[0], view_ref.shape[1] // dil
    for r in range(dil):
        for cb in range(w // LANES):
            scr[cb, pl.ds(r, n_l, stride=dil), :] = view_ref[:, r * w + cb * LANES:r * w + (cb + 1) * LANES]
    return jnp.concatenate([scr[cb] for cb in range(w // LANES)], axis=1)


def _to_class_order(val, scr, view_ref, dil):
    if dil == 1:
        view_ref[...] = val.astype(view_ref.dtype)
        return
    n, w = val.shape
    for cb in range(w // LANES):
        scr[cb] = val[:, cb * LANES:(cb + 1) * LANES]
    for r in range(dil):
        for cb in range(w // LANES):
            view_ref[:, r * w + cb * LANES:r * w + (cb + 1) * LANES] = (
                scr[cb, pl.ds(r, n // dil, stride=dil), :].astype(view_ref.dtype))


def _view_spec(rows, width, dil):
    return pl.BlockSpec((rows // dil, dil * width), lambda i: (i, 0))


def _mix_fwd(oa, obs, lbs, ga, gb, dils):
    T, W = oa.shape

    def kern(oa_ref, o1, o2, o3, l1, l2, l3, ga_ref, gb_ref, sp_ref, mix_ref, *scr):
        sp = sp_ref[...]
        w1, w2, w3 = _branch_weights(l1[...], l2[...], l3[...])
        on = [_to_token_order(o, s, d) for o, s, d in zip((o1, o2, o3), scr, dils)]
        ob = _spread(w1, sp) * on[0] + _spread(w2, sp) * on[1] + _spread(w3, sp) * on[2]
        oav = oa_ref[...]
        mix_ref[:, :W] = ((oav * _rstd(oav)) * ga_ref[...]).astype(BF16)
        mix_ref[:, W:] = ((ob * _rstd(ob)) * gb_ref[...]).astype(BF16)

    row = pl.BlockSpec((MIX_ROWS, W), lambda i: (i, 0))
    per_head = pl.BlockSpec((MIX_ROWS, N_HEADS), lambda i: (i, 0))
    vec = pl.BlockSpec((1, W), lambda i: (0, 0))
    return pl.pallas_call(
        kern, name="mix_fwd", grid=(T // MIX_ROWS,),
        in_specs=[row] + [_view_spec(MIX_ROWS, W, d) for d in dils] + [per_head] * 3
        + [vec, vec, pl.BlockSpec((N_HEADS, W), lambda i: (0, 0))],
        out_specs=pl.BlockSpec((MIX_ROWS, 2 * W), lambda i: (i, 0)),
        out_shape=jax.ShapeDtypeStruct((T, 2 * W), BF16),
        scratch_shapes=[_regroup_scratch(MIX_ROWS, W)] * 3,
        compiler_params=_params("parallel"),
    )(oa, *obs, *lbs, ga, gb, _spread_matrix())


def _mix_bwd(dmix, oa, obs, lbs, ga, gb, dils):
    T, W = oa.shape

    def kern(dm_ref, oa_ref, o1, o2, o3, l1, l2, l3, ga_ref, gb_ref, sp_ref,
             doa_ref, da_ref, do1, do2, do3, d1, d2, d3, dga_ref, dgb_ref, *scr):
        @pl.when(pl.program_id(0) == 0)
        def _():
            dga_ref[...] = jnp.zeros_like(dga_ref)
            dgb_ref[...] = jnp.zeros_like(dgb_ref)

        sp = sp_ref[...]
        oav = oa_ref[...]
        r = _rstd(oav)
        on = oav * r
        dy = dm_ref[:, :W]
        dga_ref[...] += jnp.sum(dy * on, axis=0, keepdims=True)
        t = dy * ga_ref[...]
        doa = r * (t - on * jnp.mean(t * on, axis=-1, keepdims=True))
        doa_ref[...] = doa.astype(BF16)
        da_ref[...] = _head_sums(doa * oav, sp)
        w1, w2, w3 = _branch_weights(l1[...], l2[...], l3[...])
        s1, s2, s3 = _spread(w1, sp), _spread(w2, sp), _spread(w3, sp)
        on = [_to_token_order(o, sc, d) for o, sc, d in zip((o1, o2, o3), scr, dils)]
        ob = s1 * on[0] + s2 * on[1] + s3 * on[2]
        r = _rstd(ob)
        on = ob * r
        dy = dm_ref[:, W:]
        dgb_ref[...] += jnp.sum(dy * on, axis=0, keepdims=True)
        t = dy * gb_ref[...]
        dob = r * (t - on * jnp.mean(t * on, axis=-1, keepdims=True))
        c = _head_sums(dob * ob, sp)
        for do_ref, sn, sc, d in zip((do1, do2, do3), (s1, s2, s3), scr, dils):
            _to_class_order(sn * dob, sc, do_ref, d)
        d1[...] = w1 * c
        d2[...] = w2 * c
        d3[...] = w3 * c

    row = pl.BlockSpec((MIX_ROWS, W), lambda i: (i, 0))
    per_head = pl.BlockSpec((MIX_ROWS, N_HEADS), lambda i: (i, 0))
    vec = pl.BlockSpec((1, W), lambda i: (0, 0))
    bf = jax.ShapeDtypeStruct((T, W), BF16)
    ph = jax.ShapeDtypeStruct((T, N_HEADS), F32)
    vv = jax.ShapeDtypeStruct((1, W), F32)
    views = [_view_spec(MIX_ROWS, W, d) for d in dils]
    return pl.pallas_call(
        kern, name="mix_bwd", grid=(T // MIX_ROWS,),
        in_specs=[pl.BlockSpec((MIX_ROWS, 2 * W), lambda i: (i, 0)), row] + views + [per_head] * 3 + [vec, vec,
                  pl.BlockSpec((N_HEADS, W), lambda i: (0, 0))],
        out_specs=[row, per_head] + views + [per_head, per_head, per_head, vec, vec],
        out_shape=[bf, ph] + [jax.ShapeDtypeStruct(o.shape, F32) for o in obs] + [ph, ph, ph, vv, vv],
        scratch_shapes=[_regroup_scratch(MIX_ROWS, W)] * 3,
        compiler_params=_params("arbitrary"),
    )(dmix, oa, *obs, *lbs, ga, gb, _spread_matrix())


def _alibi_slopes(n):
    return np.asarray(2.0 ** (-8.0 * (np.arange(n) + 1) / n)).astype(np.float32)


def _band_bias(max_steps, step_dist):
    qi = np.arange(BLOCK)[None, :]
    kj = np.arange(BLOCK)[:, None]
    slopes = _alibi_slopes(N_HEADS)
    halves = []
    for steps in (qi + BLOCK - kj, qi - kj):
        valid = (steps >= 0) & (steps <= max_steps)
        alibi = slopes[:, None, None] * (step_dist * steps).astype(np.float32)[None]
        halves.append(np.where(valid[None], -alibi, np.float32(NEG_INF)).astype(np.float32))
    per_head = np.concatenate(halves, axis=1)
    return jnp.asarray(np.concatenate([per_head[0::2], per_head[1::2]], axis=2))


class _AttnLayout:
    def __init__(self, dil, kv_heads, q_stride, q_off, k_stride, k_off, v_off):
        self.dil = dil
        self.kv_heads = kv_heads
        self.kw = kv_heads * HEAD_DIM
        self.rep = N_HEADS // kv_heads
        self.q_col = lambda r: r * q_stride + q_off
        self.k_col = lambda r: r * k_stride + k_off
        self.v_col = lambda r: r * k_stride + v_off


QW = N_HEADS * HEAD_DIM
LANES = 128


PAIRS = N_HEADS // 2


def _pair_cols(pair):
    return slice(pair * LANES, (pair + 1) * LANES)


def _first_head_lanes(shape):
    return lax.broadcasted_iota(jnp.int32, shape, 1) < HEAD_DIM


def _split_heads(pair):
    first = _first_head_lanes(pair.shape)
    zero = jnp.zeros_like(pair)
    return jnp.concatenate([jnp.where(first, pair, zero), jnp.where(first, zero, pair)], axis=0)


def _kv_pair(ref, pair, rep):
    if rep == 1:
        return ref[:, _pair_cols(pair)]
    blk = ref[...].astype(F32)
    other = pltpu.roll(blk, HEAD_DIM, 1)
    first = _first_head_lanes(blk.shape)
    both = jnp.where(first, blk, other) if (2 * pair // rep) % 2 == 0 else jnp.where(first, other, blk)
    return both.astype(ref.dtype)


def _paired_kv(prev_ref, cur_ref, rep, transposed=False):
    memo = {}

    def get(pair):
        key = pair if rep == 1 else 2 * pair // rep
        if key not in memo:
            blocks = [_kv_pair(ref, pair, rep) for ref in (prev_ref, cur_ref)]
            memo[key] = jnp.concatenate([b.T for b in blocks], axis=1) if transposed else jnp.concatenate(blocks, axis=0)
        return memo[key]

    return get


def _attn_fwd(proj, bias, sinks, lay, name, exchange=None):
    L = proj.shape[0]
    nb = L // BLOCK
    kw, rep = lay.kw, lay.rep
    use_sinks = sinks is not None
    scale = HEAD_DIM ** -0.5
    ride = _Ride(exchange, 7 if use_sinks else 6, 2, 2)

    def kern(*refs):
        ins, (o_ref, l_ref), (sc_ref, pr_ref), exrefs = ride.split(refs)
        q_ref, kc_ref, kp_ref, vc_ref, vp_ref, b_ref = ins[:6]
        s_ref = ins[6] if use_sinks else None
        r, i = pl.program_id(0), pl.program_id(1)
        first = i == 0
        ride.around(r * nb + i, lay.dil * nb, exrefs,
                    lambda: compute(q_ref, kc_ref, kp_ref, vc_ref, vp_ref, b_ref, s_ref, o_ref, l_ref, first,
                                    sc_ref, pr_ref))

    def compute(q_ref, kc_ref, kp_ref, vc_ref, vp_ref, b_ref, s_ref, o_ref, l_ref, first, sc_ref, pr_ref):
        keys, values_t = _paired_kv(kp_ref, kc_ref, rep), _paired_kv(vp_ref, vc_ref, rep, transposed=True)
        for pair in range(PAIRS):
            qs = _split_heads(q_ref[:, _pair_cols(pair)])
            s = _dot(keys(pair), qs, NT) * scale + b_ref[pair]
            sc_ref[pair, :BLOCK] = jnp.where(first, NEG_INF, s[:BLOCK])
            sc_ref[pair, BLOCK:] = s[BLOCK:]
        inv = []
        for h in range(N_HEADS):
            cols = slice(h % 2 * BLOCK, (h % 2 + 1) * BLOCK)
            s = sc_ref[h // 2, :, cols]
            m = jnp.max(s, axis=0, keepdims=True)
            if use_sinks:
                sink = s_ref[:, h:h + 1]
                m = jnp.maximum(m, sink)
            p = jnp.exp(s - m)
            denom = jnp.sum(p, axis=0, keepdims=True)
            if use_sinks:
                denom = denom + jnp.exp(sink - m)
            pr_ref[h // 2, :, cols] = p.astype(BF16)
            l_ref[h:h + 1, :] = m + jnp.log(denom)
            inv.append(1.0 / denom)
        for pair in range(PAIRS):
            both = _dot(values_t(pair), pr_ref[pair], NN)
            o_t = jnp.concatenate([both[:HEAD_DIM, :BLOCK] * inv[2 * pair], both[HEAD_DIM:, BLOCK:] * inv[2 * pair + 1]],
                                  axis=0)
            o_ref[:, _pair_cols(pair)] = o_t.T

    prev = lambda i: jnp.maximum(i - 1, 0)
    in_specs = [
        pl.BlockSpec((BLOCK, QW), lambda r, i: (i, lay.q_col(r))),
        pl.BlockSpec((BLOCK, kw), lambda r, i: (i, lay.k_col(r))),
        pl.BlockSpec((BLOCK, kw), lambda r, i: (prev(i), lay.k_col(r))),
        pl.BlockSpec((BLOCK, kw), lambda r, i: (i, lay.v_col(r))),
        pl.BlockSpec((BLOCK, kw), lambda r, i: (prev(i), lay.v_col(r))),
        pl.BlockSpec((PAIRS, 2 * BLOCK, 2 * BLOCK), lambda r, i: (0, 0, 0)),
    ]
    args = [proj, proj, proj, proj, proj, bias]
    if use_sinks:
        in_specs.append(pl.BlockSpec((1, N_HEADS), lambda r, i: (0, 0)))
        args.append(sinks)
    out_specs = [pl.BlockSpec((BLOCK, QW), lambda r, i: (i, r)),
                 pl.BlockSpec((None, N_HEADS, BLOCK), lambda r, i: (r, 0, i))]
    out_shape = [jax.ShapeDtypeStruct((L, lay.dil * QW), F32), jax.ShapeDtypeStruct((lay.dil, N_HEADS, L), F32)]
    return pl.pallas_call(
        kern, name=name, grid=(lay.dil, nb),
        in_specs=in_specs + ride.in_specs, out_specs=out_specs + ride.out_specs,
        out_shape=out_shape + ride.out_shapes,
        scratch_shapes=[pltpu.VMEM((PAIRS, 2 * BLOCK, 2 * BLOCK), dt) for dt in (F32, BF16)] + ride.scratch,
        input_output_aliases=ride.aliases,
        compiler_params=_params("arbitrary", "arbitrary"),
    )(*args, *ride.args)


def _attn_bwd(proj, do, lse, dd, bias, sinks, lay, name, exchange=None):
    L = proj.shape[0]
    nb = L // BLOCK
    kw, rep = lay.kw, lay.rep
    assert rep == 1 or lay.kv_heads == 2, "grouped queries: the two kv heads fill one 128-lane block"
    use_sinks = sinks is not None
    scale = HEAD_DIM ** -0.5
    ride = _Ride(exchange, 10 if use_sinks else 9, 4 if use_sinks else 3, 6)

    def kern(*refs):
        ins, outs, (ck_ref, cv_ref, *staged), exrefs = ride.split(refs)
        q_ref, kc_ref, kp_ref, vc_ref, vp_ref, do_ref, l_ref, d_ref, b_ref = ins[:9]
        s_ref = ins[9] if use_sinks else None
        dq_ref, dk_ref, dv_ref = outs[:3]
        ds_ref = outs[3] if use_sinks else None
        r = pl.program_id(0)
        i = pl.program_id(1)
        ride.around(r * (nb + 1) + i, lay.dil * (nb + 1), exrefs,
                    lambda: compute(q_ref, kc_ref, kp_ref, vc_ref, vp_ref, do_ref, l_ref, d_ref, b_ref, s_ref,
                                    dq_ref, dk_ref, dv_ref, ds_ref, ck_ref, cv_ref, r, i, *staged))

    def compute(q_ref, kc_ref, kp_ref, vc_ref, vp_ref, do_ref, l_ref, d_ref, b_ref, s_ref,
                dq_ref, dk_ref, dv_ref, ds_ref, ck_ref, cv_ref, r, i, sc_ref, dp_ref, pr_ref, dsc_ref):
        first = i == 0

        @pl.when(first)
        def _():
            ck_ref[...] = jnp.zeros_like(ck_ref)
            cv_ref[...] = jnp.zeros_like(cv_ref)

        if use_sinks:
            @pl.when(first & (r == 0))
            def _():
                ds_ref[...] = jnp.zeros_like(ds_ref)

        @pl.when(i < nb)
        def _():
            keys, values = _paired_kv(kp_ref, kc_ref, rep), _paired_kv(vp_ref, vc_ref, rep)
            keys_t = _paired_kv(kp_ref, kc_ref, rep, transposed=True)
            for pair in range(PAIRS):
                qs = _split_heads(q_ref[:, _pair_cols(pair)])
                dos = _split_heads(do_ref[:, _pair_cols(pair)].astype(BF16))
                s = _dot(keys(pair), qs, NT) * scale + b_ref[pair]
                sc_ref[pair, :BLOCK] = jnp.where(first, NEG_INF, s[:BLOCK])
                sc_ref[pair, BLOCK:] = s[BLOCK:]
                dp_ref[pair] = _dot(values(pair), dos, NT)
            for h in range(N_HEADS):
                cols = slice(h % 2 * BLOCK, (h % 2 + 1) * BLOCK)
                lrow = l_ref[h:h + 1, :]
                drow = d_ref[h:h + 1, :]
                p = jnp.exp(sc_ref[h // 2, :, cols] - lrow)
                pr_ref[h // 2, :, cols] = p.astype(BF16)
                dsc_ref[h // 2, :, cols] = (p * (dp_ref[h // 2, :, cols] - drow) * scale).astype(BF16)
                if use_sinks:
                    ds_ref[h:h + 1, :] += -(jnp.exp(s_ref[:, h:h + 1] - lrow) * drow)
            grouped = {}
            for pair in range(PAIRS):
                cols = _pair_cols(pair)
                qs = _split_heads(q_ref[:, cols])
                dos = _split_heads(do_ref[:, cols].astype(BF16))
                ds = dsc_ref[pair]
                both = _dot(keys_t(pair), ds, NN)
                dq_ref[:, cols] = jnp.concatenate([both[:HEAD_DIM, :BLOCK], both[HEAD_DIM:, BLOCK:]], axis=0).T
                dk = _dot(ds, qs, NN)
                dv = _dot(pr_ref[pair], dos, NN)
                if rep == 1:
                    dk_ref[:, cols] = ck_ref[:, cols] + dk[:BLOCK]
                    dv_ref[:, cols] = cv_ref[:, cols] + dv[:BLOCK]
                    ck_ref[:, cols] = dk[BLOCK:]
                    cv_ref[:, cols] = dv[BLOCK:]
                else:
                    g = 2 * pair // rep
                    grouped[g] = (dk, dv) if g not in grouped else (grouped[g][0] + dk, grouped[g][1] + dv)
            if rep > 1:
                fold = lambda t: t + pltpu.roll(t, HEAD_DIM, 1)
                first_half = _first_head_lanes((2 * BLOCK, LANES))
                dk = jnp.where(first_half, fold(grouped[0][0]), fold(grouped[1][0]))
                dv = jnp.where(first_half, fold(grouped[0][1]), fold(grouped[1][1]))
                dk_ref[...] = ck_ref[...] + dk[:BLOCK]
                dv_ref[...] = cv_ref[...] + dv[:BLOCK]
                ck_ref[...] = dk[BLOCK:]
                cv_ref[...] = dv[BLOCK:]

        @pl.when(i == nb)
        def _():
            dk_ref[...] = ck_ref[...]
            dv_ref[...] = cv_ref[...]
            if use_sinks:
                @pl.when(r == lay.dil - 1)
                def _():
                    ds_ref[...] = jnp.broadcast_to(jnp.sum(ds_ref[...], axis=1, keepdims=True), ds_ref.shape)

    cur = lambda i: jnp.minimum(i, nb - 1)
    prev = lambda i: jnp.maximum(jnp.minimum(i, nb - 1) - 1, 0)
    done = lambda i: jnp.maximum(i - 1, 0)
    qspec = lambda col: pl.BlockSpec((BLOCK, QW), lambda r, i: (cur(i), col(r)))
    per_head = pl.BlockSpec((None, N_HEADS, BLOCK), lambda r, i: (r, 0, cur(i)))
    in_specs = [
        qspec(lay.q_col),
        pl.BlockSpec((BLOCK, kw), lambda r, i: (cur(i), lay.k_col(r))),
        pl.BlockSpec((BLOCK, kw), lambda r, i: (prev(i), lay.k_col(r))),
        pl.BlockSpec((BLOCK, kw), lambda r, i: (cur(i), lay.v_col(r))),
        pl.BlockSpec((BLOCK, kw), lambda r, i: (prev(i), lay.v_col(r))),
        qspec(lambda r: r), per_head, per_head,
        pl.BlockSpec((PAIRS, 2 * BLOCK, 2 * BLOCK), lambda r, i: (0, 0, 0)),
    ]
    args = [proj, proj, proj, proj, proj, do, lse, dd, bias]
    out_specs = [
        qspec(lambda r: r),
        pl.BlockSpec((BLOCK, kw), lambda r, i: (done(i), r)),
        pl.BlockSpec((BLOCK, kw), lambda r, i: (done(i), r)),
    ]
    dkv_shape = jax.ShapeDtypeStruct((L, lay.dil * kw), F32)
    out_shape = [jax.ShapeDtypeStruct((L, lay.dil * QW), F32), dkv_shape, dkv_shape]
    if use_sinks:
        in_specs.append(pl.BlockSpec((1, N_HEADS), lambda r, i: (0, 0)))
        args.append(sinks)
        out_specs.append(pl.BlockSpec((N_HEADS, LANES), lambda r, i: (0, 0)))
        out_shape.append(jax.ShapeDtypeStruct((N_HEADS, LANES), F32))
    return pl.pallas_call(
        kern, name=name, grid=(lay.dil, nb + 1),
        in_specs=in_specs + ride.in_specs, out_specs=out_specs + ride.out_specs,
        out_shape=out_shape + ride.out_shapes,
        scratch_shapes=[pltpu.VMEM((BLOCK, kw), F32), pltpu.VMEM((BLOCK, kw), F32)]
        + [pltpu.VMEM((PAIRS, 2 * BLOCK, 2 * BLOCK), dt) for dt in (F32, F32, BF16, BF16)] + ride.scratch,
        input_output_aliases=ride.aliases,
        compiler_params=_params("arbitrary", "arbitrary"),
    )(*args, *ride.args)


def _assemble(groups, name, dils=(1,)):
    T = groups[0][0].shape[0] * dils[0]
    widths = [g[0].shape[1] // dils[0] for g in groups]
    total = sum(widths)
    flat = [a for g in groups for a in g]
    member_dils = [d for g in groups for d in dils[:len(g)]]

    def kern(*refs):
        ins = refs[:len(flat)]
        out_ref, cs_ref = refs[len(flat):len(flat) + 2]
        scr = refs[len(flat) + 2:]

        @pl.when(pl.program_id(0) == 0)
        def _():
            cs_ref[...] = jnp.zeros_like(cs_ref)

        pos = off = 0
        for g, w in zip(groups, widths):
            acc = _to_token_order(ins[pos], None, dils[0])
            for j in range(1, len(g)):
                acc = acc + _to_token_order(ins[pos + j], scr[j - 1], dils[j])
            pos += len(g)
            out_ref[:, off:off + w] = acc.astype(BF16)
            cs_ref[:, off:off + w] += jnp.sum(acc, axis=0, keepdims=True)
            off += w

    return pl.pallas_call(
        kern, name=name, grid=(T // ROWS,),
        in_specs=[_view_spec(ROWS, a.shape[1] // d, d) for a, d in zip(flat, member_dils)],
        out_specs=[pl.BlockSpec((ROWS, total), lambda i: (i, 0)), pl.BlockSpec((1, total), lambda i: (0, 0))],
        out_shape=[jax.ShapeDtypeStruct((T, total), BF16), jax.ShapeDtypeStruct((1, total), F32)],
        scratch_shapes=[_regroup_scratch(ROWS, max(widths))] * (len(dils) - 1),
        compiler_params=_params("arbitrary"),
    )(*flat)


def _adamw(w, g, m, v, name):
    R, C = w.shape
    rows = min(R, ROWS)
    assert R % rows == 0

    def kern(w_ref, g_ref, m_ref, v_ref, d_ref, nm_ref, nv_ref):
        gv = g_ref[...]
        mn = ADAM_B1 * m_ref[...] + (1.0 - ADAM_B1) * gv
        vn = ADAM_B2 * v_ref[...] + (1.0 - ADAM_B2) * jnp.square(gv)
        m_hat = mn / (1.0 - ADAM_B1 ** ADAM_STEP)
        v_hat = vn / (1.0 - ADAM_B2 ** ADAM_STEP)
        d_ref[...] = -ADAM_LR * (m_hat / (jnp.sqrt(v_hat) + ADAM_EPS) + ADAM_WD * w_ref[...])
        nm_ref[...] = mn
        nv_ref[...] = vn

    blk = pl.BlockSpec((rows, C), lambda i: (i, 0))
    shp = jax.ShapeDtypeStruct((R, C), F32)
    return pl.pallas_call(
        kern, name=name, grid=(R // rows,),
        in_specs=[blk] * 4, out_specs=[blk] * 3, out_shape=[shp] * 3,
        compiler_params=_params("parallel"),
    )(w, g, m, v)


def _sum_slots(slots, name):
    n, R, C = slots.shape
    SUM_ROWS = next(rows for rows in (128, 64, 32, 16) if R % rows == 0)

    def kern(s_ref, o_ref):
        acc = s_ref[0].astype(F32)
        for k in range(1, n):
            acc = acc + s_ref[k].astype(F32)
        o_ref[...] = acc

    return pl.pallas_call(
        kern, name=name, grid=(R // SUM_ROWS,),
        in_specs=[pl.BlockSpec((n, SUM_ROWS, C), lambda i: (0, i, 0))],
        out_specs=pl.BlockSpec((SUM_ROWS, C), lambda i: (i, 0)),
        out_shape=jax.ShapeDtypeStruct((R, C), F32),
        compiler_params=_params("parallel"),
    )(slots)


def _place():
    return lax.axis_index("x"), lax.axis_index("y"), lax.axis_index("c")


def _index(p):
    return 4 * p[0] + 2 * p[1] + p[2]


FLIPS = [(fx, fy, fc) for fx in (0, 1) for fy in (0, 1) for fc in (0, 1)][1:]


def _peer(me, flip):
    return tuple(1 - a if f else a for a, f in zip(me, flip))


def _gather_rows(shards, part=(0, 1), into=None):
    nw = len(shards)

    def plan(ins, outs, send_sems, recv_sems):
        x, y, c = me = _place()
        sibling = (x, y, 1 - c)
        chips = [(1 - x, y), (x, 1 - y), (1 - x, 1 - y)]

        def span(w):
            cnt = ins[w].shape[0] // part[1]
            return part[0] * cnt, cnt

        def rows(w, p):
            lo, cnt = span(w)
            return outs[w].at[pl.ds(_index(p) * ins[w].shape[0] + lo, cnt), :]

        def own(w):
            lo, cnt = span(w)
            return ins[w].at[pl.ds(lo, cnt), :]

        def copy(w, k, block, to):
            return pltpu.make_async_remote_copy(
                src_ref=own(w) if block is me else rows(w, block), dst_ref=rows(w, block),
                send_sem=send_sems.at[7 * w + k], recv_sem=recv_sems.at[7 * w + k],
                device_id=to, device_id_type=MESH)

        return me, sibling, chips, c, rows, own, copy

    def copies(ins, outs, send_sems, recv_sems, local_sems):
        me, sibling, chips, c, rows, own, copy = plan(ins, outs, send_sems, recv_sems)
        local = [pltpu.make_async_copy(own(w), rows(w, me), local_sems.at[w]) for w in range(nw)]
        sends, recvs = [], []
        for w in range(nw):
            sends.append(copy(w, 0, me, sibling))
            sends += [copy(w, 1 + j, me, (*chip, c)) for j, chip in enumerate(chips)]
            recvs.append(copy(w, 0, sibling, me))
            recvs += [copy(w, 4 + j, (*chip, 1 - c), me) for j, chip in enumerate(chips)]
        return local, sends, recvs

    def relay(ins, outs, send_sems, recv_sems, local_sems):
        me, sibling, chips, c, rows, own, copy = plan(ins, outs, send_sems, recv_sems)
        arrived = [copy(w, 1 + j, (*chip, c), me) for w in range(nw) for j, chip in enumerate(chips)]
        onward = [copy(w, 4 + j, (*chip, c), sibling) for w in range(nw) for j, chip in enumerate(chips)]
        return arrived, onward

    shapes = [jax.ShapeDtypeStruct((N_DEV * s.shape[0], s.shape[1]), s.dtype) for s in shards]
    aliases = {nw + w: w for w in range(nw)} if into else None
    return _Exchange(shards + (into or []), shapes, 7 * nw, nw, copies, aliases=aliases, relay=relay)


def _scatter_rows(parts, part=(0, 1)):
    nw = len(parts)

    def copies(ins, outs, send_sems, recv_sems, local_sems):
        me = _place()

        def src(w, owner):
            n = ins[w].shape[0] // N_DEV
            cnt = n // part[1]
            return ins[w].at[pl.ds(_index(owner) * n + part[0] * cnt, cnt), :]

        def copy(k, w, owner, sender, to):
            return pltpu.make_async_remote_copy(
                src_ref=src(w, owner), dst_ref=outs[w].at[_index(sender)],
                send_sem=send_sems.at[nw * k + w], recv_sem=recv_sems.at[nw * k + w],
                device_id=to, device_id_type=MESH)

        local = [pltpu.make_async_copy(src(w, me), outs[w].at[_index(me)], local_sems.at[w]) for w in range(nw)]
        peers = [_peer(me, flip) for flip in FLIPS]
        sends = [copy(k, w, peer, me, peer) for k, peer in enumerate(peers) for w in range(nw)]
        recvs = [copy(k, w, me, peer, me) for k, peer in enumerate(peers) for w in range(nw)]
        return local, sends, recvs

    shapes = [jax.ShapeDtypeStruct((N_DEV, p.shape[0] // N_DEV // part[1], p.shape[1]), p.dtype) for p in parts]
    return _Exchange(parts, shapes, 7 * nw, nw, copies)


def _sum_over_devices(v):
    shape = v.shape

    def body(v_ref, sum_ref, all_ref, send_sems, recv_sems):
        me = _place()
        all_ref[_index(me)] = v_ref[...]
        sends = []
        for k, flip in enumerate(FLIPS):
            peer = _peer(me, flip)
            sends.append(pltpu.make_async_remote_copy(
                src_ref=v_ref, dst_ref=all_ref.at[_index(me)],
                send_sem=send_sems.at[k], recv_sem=recv_sems.at[k], device_id=peer, device_id_type=MESH))
            sends[-1].start()
        for k, flip in enumerate(FLIPS):
            peer = _peer(me, flip)
            pltpu.make_async_remote_copy(
                src_ref=v_ref, dst_ref=all_ref.at[_index(peer)],
                send_sem=send_sems.at[k], recv_sem=recv_sems.at[k], device_id=peer, device_id_type=MESH).wait_recv()
        for cp in sends:
            cp.wait_send()
        acc = all_ref[0]
        for s in range(1, N_DEV):
            acc = acc + all_ref[s]
        sum_ref[...] = acc

    vmem = pl.BlockSpec(memory_space=pltpu.VMEM)
    return pl.pallas_call(
        body, name="sum_small_grads",
        in_specs=[vmem], out_specs=[vmem, vmem],
        out_shape=[jax.ShapeDtypeStruct(shape, F32), jax.ShapeDtypeStruct((N_DEV,) + shape, F32)],
        scratch_shapes=[pltpu.SemaphoreType.DMA((7,)), pltpu.SemaphoreType.DMA((7,))],
    )(v)[0]


SMALL_ROWS = 8


def _pack_small(vectors):
    padded = []
    for vec in vectors:
        vec = vec.reshape(-1)
        padded.append(jnp.pad(vec, (0, -vec.shape[0] % 128)))
    flat = jnp.concatenate(padded)
    flat = jnp.pad(flat, (0, -flat.shape[0] % (SMALL_ROWS * 128)))
    return flat.reshape(SMALL_ROWS, -1)


def _unpack_small(packed, shapes):
    flat = packed.reshape(-1)
    out, off = [], 0
    for shp in shapes:
        n = int(np.prod(shp))
        out.append(flat[off:off + n].reshape(shp))
        off += n + (-n % 128)
    return out


def kernel(x, g_attn, w_in, b_in, sinks_a, g_out_a, g_out_b, w_out, g_mlp, w_1, w_2, g_final, loss_target, m_g_attn, m_w_in, m_b_in, m_sinks_a, m_g_out_a, m_g_out_b, m_w_out, m_g_mlp, m_w_1, m_w_2, m_g_final, v_g_attn, v_w_in, v_b_in, v_sinks_a, v_g_out_a, v_g_out_b, v_w_out, v_g_mlp, v_w_1, v_w_2, v_g_final):
    xs, tgt = x[0], loss_target[0]
    T, D = xs.shape
    n_a = QW + 2 * KV_HEADS_A * HEAD_DIM
    g_fin = g_final.reshape(1, D)

    shards = [w_in[0].T.astype(BF16), w_out[0].astype(BF16), w_1[0].T.astype(BF16), w_2[0].astype(BF16)]
    ident = lambda acc: (acc,)
    add = lambda acc, other: (acc + other,)
    tiles = dict(tm=512, tn=1024)

    h1, w_in_t = _norm_fwd(xs, g_attn, "norm_attn", exchange=_gather_rows(shards[:1]))
    w_in_ta, w_in_tb = w_in_t[:n_a], w_in_t[n_a:]
    proj_a, = _matmul(h1, w_in_ta, "nt", [BF16], add, tm=512, tn=n_a, tk=D, row_ins=[b_in[:, :n_a]], name="proj_a")
    dils = [dil for _, dil in DILATED_BRANCHES]
    *proj_b, w_o = _proj_views(h1, w_in_tb, b_in[:, n_a:], dils, "proj_b", exchange=_gather_rows(shards[1:2]))

    lay_a = _AttnLayout(1, KV_HEADS_A, 0, 0, 0, QW // (KV_HEADS_A * HEAD_DIM), QW // (KV_HEADS_A * HEAD_DIM) + 1)
    bias_a = _band_bias(WINDOW_A - 1, 1)
    o_a, l_a, w_1_t = _attn_fwd(proj_a, bias_a, sinks_a, lay_a, "attn_a_fwd",
                                exchange=_gather_rows(shards[2:3], part=(0, 4)))
    branches = []
    for n, (window, dil) in enumerate(DILATED_BRANCHES):
        lay = _AttnLayout(dil, N_HEADS, 3, 0, 3, 1, 2)
        bias = _band_bias(window // dil, dil)
        ride = _gather_rows(shards[2:3], part=(n + 1, 4), into=[w_1_t])
        o, lse, w_1_t = _attn_fwd(proj_b[n], bias, None, lay, f"attn_b{dil}_fwd", exchange=ride)
        branches.append((lay, bias, proj_b[n], o, lse))
    o_b = [br[3] for br in branches]
    l_b = [br[4].transpose(2, 0, 1).reshape(T, N_HEADS) for br in branches]

    mix = _mix_fwd(o_a, o_b, l_b, g_out_a, g_out_b, dils)
    x2, = _matmul(mix, w_o, "nn", [F32], add, tk=D, tile_ins=[xs], name="out_proj", **tiles)
    h2, = _norm_fwd(x2, g_mlp, "norm_mlp")

    def relu_sq(acc):
        u = jnp.maximum(acc, 0.0)
        return u, u * u

    u, u_sq, w_2_f = _matmul(h2, w_1_t, "nt", [BF16, BF16], relu_sq, tk=D, name="mlp_up",
                             exchange=_gather_rows(shards[3:]), **tiles)
    x3, = _matmul(u_sq, w_2_f, "nn", [F32], add, tk=2048, tile_ins=[x2], name="mlp_down", **tiles)

    dx3, dx3_b, dg_final, loss_dev = _loss_head(x3, tgt, g_fin)

    d_pre, = _matmul(dx3_b, w_2_f, "nt", [BF16], lambda acc, uu: (acc * (2.0 * uu.astype(F32)),),
                     tk=D, tile_ins=[u], name="mlp_down_bwd", **tiles)
    wtiles = dict(tm=1024, tn=1024, tk=1024)
    dw_2, = _matmul(u_sq, dx3_b, "tn", [BF16], ident, name="mlp_down_wgrad", **wtiles)
    dh2, slots_2 = _matmul(d_pre, w_1_t, "nn", [F32], ident, tk=2048, name="mlp_up_bwd", exchange=_scatter_rows([dw_2]),
                           **tiles)
    dw_1_t, = _matmul(d_pre, h2, "tn", [BF16], ident, name="mlp_up_wgrad", **wtiles)
    dx2, dx2_b, dg_mlp = _norm_bwd(dh2, x2, g_mlp, dx3, "norm_mlp_bwd")

    dmix, = _matmul(dx2_b, w_o, "nt", [F32], ident, tk=D, name="out_proj_bwd", **tiles)
    dw_o, = _matmul(mix, dx2_b, "tn", [BF16], ident, name="out_proj_wgrad", **wtiles)
    do_a, dd_a, do1, do2, do3, dd1, dd2, dd3, dg_out_a, dg_out_b = _mix_bwd(dmix, o_a, o_b, l_b, g_out_a, g_out_b,
                                                                                  dils)

    by_class = lambda d, dil: d.reshape(T // dil, dil, N_HEADS).transpose(1, 2, 0)
    dq_a, dk_a, dv_a, dsinks, slots_1a = _attn_bwd(proj_a, do_a, l_a, by_class(dd_a, 1), bias_a, sinks_a, lay_a,
                                                   "attn_a_bwd", exchange=_scatter_rows([dw_1_t], part=(0, 2)))
    dsinks = dsinks[:, 0].reshape(1, N_HEADS)
    dqs, dks, dvs = [], [], []
    rides = [_scatter_rows([dw_1_t], part=(1, 2)), _scatter_rows([dw_o]), None]
    for (lay, bias, view, _, lse), do_n, dd_n, ride in zip(branches, (do1, do2, do3), (dd1, dd2, dd3), rides):
        dq, dk, dv, *got = _attn_bwd(view, do_n, lse, by_class(dd_n, lay.dil), bias, None, lay,
                                     f"attn_b{lay.dil}_bwd", exchange=ride)
        if lay.dil == 1:
            slots_1b, = got
        elif ride:
            slots_o, = got
        dqs.append(dq)
        dks.append(dk)
        dvs.append(dv)
    dproj_a, db_a = _assemble([[dq_a], [dk_a], [dv_a]], "dproj_a")
    dproj_b, db_b = _assemble([dqs, dks, dvs], "dproj_b", dils)

    dw_in_ta, = _matmul(dproj_a, h1, "tn", [BF16], ident, tm=n_a, tn=1024, tk=512, name="in_proj_a_wgrad")
    dw_in_tb, = _matmul(dproj_b, h1, "tn", [BF16], ident, name="in_proj_b_wgrad", **wtiles)
    dw_in_t = jnp.concatenate([dw_in_ta, dw_in_tb], axis=0)
    dh1_a, slots_in_a = _matmul(dproj_a, w_in_ta, "nn", [F32], ident, tk=n_a, name="in_proj_a_bwd",
                                exchange=_scatter_rows([dw_in_t], part=(0, 2)), **tiles)
    dh1, slots_in_b = _matmul(dproj_b, w_in_tb, "nn", [F32], add, tk=3 * QW, tile_ins=[dh1_a], name="in_proj_b_bwd",
                              exchange=_scatter_rows([dw_in_t], part=(1, 2)), **tiles)
    dx, _, dg_attn = _norm_bwd(dh1, xs, g_attn, dx2, "norm_attn_bwd")

    g_w_in = jnp.concatenate([_sum_slots(slots_in_a, "sum_w_in_a_grads"), _sum_slots(slots_in_b, "sum_w_in_b_grads")]).T
    g_w_out = _sum_slots(slots_o, "sum_w_out_grads")
    g_w_1 = jnp.concatenate([_sum_slots(slots_1a, "sum_w_1a_grads"), _sum_slots(slots_1b, "sum_w_1b_grads")]).T
    g_w_2 = _sum_slots(slots_2, "sum_w_2_grads")

    small_w = [g_attn, b_in, sinks_a, g_out_a, g_out_b, g_mlp, g_final]
    small_m = [m_g_attn, m_b_in, m_sinks_a, m_g_out_a, m_g_out_b, m_g_mlp, m_g_final]
    small_v = [v_g_attn, v_b_in, v_sinks_a, v_g_out_a, v_g_out_b, v_g_mlp, v_g_final]
    small_g = [dg_attn, jnp.concatenate([db_a, db_b], axis=1), dsinks, dg_out_a, dg_out_b, dg_mlp, dg_final]
    summed = _sum_over_devices(_pack_small(small_g + [loss_dev[:, :1]]))
    shapes = [w.shape for w in small_w]
    *g_small, loss = _unpack_small(summed, shapes + [()])

    big = [
        _adamw(w_in[0], g_w_in, m_w_in[0], v_w_in[0], "adamw_w_in"),
        _adamw(w_out[0], g_w_out, m_w_out[0], v_w_out[0], "adamw_w_out"),
        _adamw(w_1[0], g_w_1, m_w_1[0], v_w_1[0], "adamw_w_1"),
        _adamw(w_2[0], g_w_2, m_w_2[0], v_w_2[0], "adamw_w_2"),
    ]
    g_packed = _pack_small(g_small)
    small = _adamw(_pack_small(small_w), g_packed, _pack_small(small_m), _pack_small(small_v), "adamw_small")
    small = [_unpack_small(s, shapes) for s in small]

    def ordered(small_list, big_list):
        s = list(small_list)
        return [s[0], big_list[0][None], s[1], s[2], s[3], s[4], big_list[1][None], s[5],
                big_list[2][None], big_list[3][None], s[6]]

    grads = ordered(g_small, [g_w_in, g_w_out, g_w_1, g_w_2])
    deltas = ordered(small[0], [b[0] for b in big])
    new_m = ordered(small[1], [b[1] for b in big])
    new_v = ordered(small[2], [b[2] for b in big])
    return (loss, dx[None], *grads, *deltas, *new_m, *new_v)
```

```python
import numpy as np
import jax
import jax.numpy as jnp
from jax import lax
from jax.experimental import pallas as pl
from jax.experimental.pallas import tpu as pltpu

F32 = jnp.float32
BF16 = jnp.bfloat16

HEAD_DIM = 64
N_HEADS = 16
KV_HEADS_A = 2
BLOCK = 128
WINDOW_A = 128
DILATED_BRANCHES = ((128, 1), (512, 4), (2048, 16))
EPS = 1e-5
NEG_INF = -1e30
N_DEV = 8

ADAM_LR = 0.001
ADAM_B1 = 0.9
ADAM_B2 = 0.999
ADAM_EPS = 1e-08
ADAM_WD = 0.01
ADAM_STEP = 10

VMEM_LIMIT_BYTES = 56 * 1024 * 1024
MESH = pl.DeviceIdType.MESH
ANY = pl.BlockSpec(memory_space=pl.ANY)

NN = (((1,), (0,)), ((), ()))
NT = (((1,), (1,)), ((), ()))
TN = (((0,), (0,)), ((), ()))


def _dot(a, b, dims):
    return lax.dot_general(a, b, dims, preferred_element_type=F32)


def _params(*sem):
    return pltpu.CompilerParams(dimension_semantics=sem, vmem_limit_bytes=VMEM_LIMIT_BYTES)


RELAY_AT = 0.6


class _Exchange:
    def __init__(self, ins, out_shapes, n_remote, n_local, copies, aliases=None, relay=None):
        self.ins, self.out_shapes = list(ins), list(out_shapes)
        self.n_remote, self.n_local = n_remote, n_local
        self.copies = copies
        self.relay = relay
        self.aliases = aliases or {}

    def start(self, refs):
        local, sends, _ = self.copies(*refs)
        for cp in local + sends:
            cp.start()

    def middle(self, refs):
        arrived, onward = self.relay(*refs)
        for got, cp in zip(arrived, onward):
            got.wait_recv()
            cp.start()

    def finish(self, refs):
        local, sends, recvs = self.copies(*refs)
        for cp in recvs:
            cp.wait_recv()
        for cp in sends:
            cp.wait_send()
        for cp in local:
            cp.wait()
        if self.relay:
            for cp in self.relay(*refs)[1]:
                cp.wait_send()


class _Ride:
    def __init__(self, ex, n_in, n_out, n_scratch):
        self.ex = ex
        self.n = (n_in, n_out, n_scratch)
        self.args = ex.ins if ex else []
        self.in_specs = [ANY] * len(self.args)
        self.out_shapes = ex.out_shapes if ex else []
        self.out_specs = [ANY] * len(self.out_shapes)
        self.scratch = [pltpu.SemaphoreType.DMA((ex.n_remote,)), pltpu.SemaphoreType.DMA((ex.n_remote,)),
                        pltpu.SemaphoreType.DMA((max(ex.n_local, 1),))] if ex else []
        self.aliases = {n_in + i: n_out + o for i, o in ex.aliases.items()} if ex else {}

    def split(self, refs):
        n_in, n_out, n_scratch = self.n
        a = n_in
        b = a + len(self.args)
        c = b + n_out
        d = c + len(self.out_shapes)
        e = d + n_scratch
        return refs[:a], refs[b:c], refs[d:e], (refs[a:b], refs[c:d], *refs[e:])

    def around(self, step, n_steps, exrefs, compute):
        if self.ex is None:
            compute()
            return

        @pl.when(step == 0)
        def _():
            self.ex.start(exrefs)

        compute()

        if self.ex.relay:
            @pl.when(step == int(RELAY_AT * (n_steps - 1)))
            def _():
                self.ex.middle(exrefs)

        @pl.when(step == n_steps - 1)
        def _():
            self.ex.finish(exrefs)


def _matmul(a, b, dims, out_dtypes, epilogue, *, tm, tn, tk, name, tile_ins=(), row_ins=(), exchange=None):
    if dims == "tn":
        K, M = a.shape
    else:
        M, K = a.shape
    N = b.shape[0] if dims == "nt" else b.shape[1]
    tm, tn, tk = min(tm, M), min(tn, N), min(tk, K)
    assert M % tm == 0 and N % tn == 0 and K % tk == 0, (name, M, N, K, tm, tn, tk)
    grid = (M // tm, N // tn, K // tk)
    nk = grid[2]
    n_tile, n_row, n_out = len(tile_ins), len(row_ins), len(out_dtypes)
    dn = {"nn": NN, "nt": NT, "tn": TN}[dims]
    ride = _Ride(exchange, 2 + n_tile + n_row, n_out, 1 if nk > 1 else 0)

    def kern(*refs):
        ins, out_refs, scratch, exrefs = ride.split(refs)
        a_ref, b_ref = ins[:2]
        tile_refs = ins[2:2 + n_tile]
        row_refs = ins[2 + n_tile:]
        ids = [pl.program_id(d) for d in range(3)]

        def finish(acc):
            outs = epilogue(acc, *[r[...] for r in tile_refs], *[r[...] for r in row_refs])
            for o_ref, o in zip(out_refs, outs):
                o_ref[...] = o.astype(o_ref.dtype)

        def compute():
            if nk == 1:
                finish(_dot(a_ref[...], b_ref[...], dn))
                return
            acc_ref = scratch[0]

            @pl.when(ids[2] == 0)
            def _():
                acc_ref[...] = jnp.zeros_like(acc_ref)

            acc_ref[...] += _dot(a_ref[...], b_ref[...], dn)

            @pl.when(ids[2] == nk - 1)
            def _():
                finish(acc_ref[...])

        ride.around((ids[0] * grid[1] + ids[1]) * grid[2] + ids[2], grid[0] * grid[1] * grid[2], exrefs, compute)

    if dims == "tn":
        a_spec = pl.BlockSpec((tk, tm), lambda i, j, k: (k, i))
    else:
        a_spec = pl.BlockSpec((tm, tk), lambda i, j, k: (i, k))
    if dims == "nt":
        b_spec = pl.BlockSpec((tn, tk), lambda i, j, k: (j, k))
    else:
        b_spec = pl.BlockSpec((tk, tn), lambda i, j, k: (k, j))
    tile_spec = pl.BlockSpec((tm, tn), lambda i, j, k: (i, j))
    row_spec = pl.BlockSpec((1, tn), lambda i, j, k: (0, j))
    sem = ("arbitrary",) * 3 if exchange else ("parallel", "parallel", "arbitrary")
    return pl.pallas_call(
        kern,
        name=name,
        grid=grid,
        in_specs=[a_spec, b_spec] + [tile_spec] * n_tile + [row_spec] * n_row + ride.in_specs,
        out_specs=[tile_spec] * n_out + ride.out_specs,
        out_shape=[jax.ShapeDtypeStruct((M, N), dt) for dt in out_dtypes] + ride.out_shapes,
        scratch_shapes=([pltpu.VMEM((tm, tn), F32)] if nk > 1 else []) + ride.scratch,
        input_output_aliases=ride.aliases,
        compiler_params=_params(*sem),
    )(a, b, *tile_ins, *row_ins, *ride.args)


PROJ_ROWS = 256


def _proj_views(a, w_t, bias, dils, name, exchange=None):
    T, K = a.shape
    N = w_t.shape[0]
    ride = _Ride(exchange, 3, len(dils), 1)

    def kern(*refs):
        (a_ref, w_ref, b_ref), outs, (scr,), exrefs = ride.split(refs)

        def compute():
            acc = _dot(a_ref[...], w_ref[...], NT) + b_ref[...]
            for out_ref, dil in zip(outs, dils):
                _to_class_order(acc, scr, out_ref, dil)

        ride.around(pl.program_id(0), T // PROJ_ROWS, exrefs, compute)

    return pl.pallas_call(
        kern, name=name, grid=(T // PROJ_ROWS,),
        in_specs=[pl.BlockSpec((PROJ_ROWS, K), lambda i: (i, 0)), pl.BlockSpec((N, K), lambda i: (0, 0)),
                  pl.BlockSpec((1, N), lambda i: (0, 0))] + ride.in_specs,
        out_specs=[_view_spec(PROJ_ROWS, N, d) for d in dils] + ride.out_specs,
        out_shape=[jax.ShapeDtypeStruct((T // d, d * N), BF16) for d in dils] + ride.out_shapes,
        scratch_shapes=[_regroup_scratch(PROJ_ROWS, N)] + ride.scratch,
        input_output_aliases=ride.aliases,
        compiler_params=_params("arbitrary"),
    )(a, w_t, bias, *ride.args)


ROWS = 256
MIX_ROWS = 128


def _rstd(xv):
    return lax.rsqrt(jnp.mean(xv * xv, axis=-1, keepdims=True) + EPS)


def _norm_fwd(x, g, name, exchange=None):
    T, D = x.shape
    ride = _Ride(exchange, 2, 1, 0)

    def kern(*refs):
        (x_ref, g_ref), (h_ref,), _, exrefs = ride.split(refs)

        def compute():
            xv = x_ref[...]
            h_ref[...] = ((xv * _rstd(xv)) * g_ref[...]).astype(h_ref.dtype)

        ride.around(pl.program_id(0), T // ROWS, exrefs, compute)

    row = pl.BlockSpec((ROWS, D), lambda i: (i, 0))
    return pl.pallas_call(
        kern, name=name, grid=(T // ROWS,),
        in_specs=[row, pl.BlockSpec((1, D), lambda i: (0, 0))] + ride.in_specs,
        out_specs=[row] + ride.out_specs,
        out_shape=[jax.ShapeDtypeStruct((T, D), BF16)] + ride.out_shapes,
        scratch_shapes=ride.scratch, input_output_aliases=ride.aliases,
        compiler_params=_params("arbitrary"),
    )(x, g, *ride.args)


def _norm_bwd(dh, x, g, res, name, exchange=None):
    T, D = x.shape
    ride = _Ride(exchange, 4, 3, 0)

    def kern(*refs):
        (dh_ref, x_ref, g_ref, res_ref), (dx_ref, dxb_ref, dg_ref), _, exrefs = ride.split(refs)

        def compute():
            @pl.when(pl.program_id(0) == 0)
            def _():
                dg_ref[...] = jnp.zeros_like(dg_ref)

            xv = x_ref[...]
            r = _rstd(xv)
            xn = xv * r
            dhv = dh_ref[...]
            dg_ref[...] += jnp.sum(dhv * xn, axis=0, keepdims=True)
            t = dhv * g_ref[...]
            dx = res_ref[...] + r * (t - xn * jnp.mean(t * xn, axis=-1, keepdims=True))
            dx_ref[...] = dx
            dxb_ref[...] = dx.astype(BF16)

        ride.around(pl.program_id(0), T // ROWS, exrefs, compute)

    row = pl.BlockSpec((ROWS, D), lambda i: (i, 0))
    vec = pl.BlockSpec((1, D), lambda i: (0, 0))
    return pl.pallas_call(
        kern, name=name, grid=(T // ROWS,),
        in_specs=[row, row, vec, row] + ride.in_specs,
        out_specs=[row, row, vec] + ride.out_specs,
        out_shape=[jax.ShapeDtypeStruct((T, D), F32), jax.ShapeDtypeStruct((T, D), BF16),
                   jax.ShapeDtypeStruct((1, D), F32)] + ride.out_shapes,
        scratch_shapes=ride.scratch, input_output_aliases=ride.aliases,
        compiler_params=_params("arbitrary"),
    )(dh, x, g, res, *ride.args)


def _loss_head(x3, tgt, g):
    T, D = x3.shape

    def kern(x_ref, t_ref, g_ref, dx_ref, dxb_ref, dg_ref, loss_ref):
        @pl.when(pl.program_id(0) == 0)
        def _():
            dg_ref[...] = jnp.zeros_like(dg_ref)
            loss_ref[...] = jnp.zeros_like(loss_ref)

        xv = x_ref[...]
        gv = g_ref[...]
        r = _rstd(xv)
        xn = xv * r
        err = xn * gv - t_ref[...]
        per_tok = jnp.mean(err * err, axis=-1, keepdims=True)
        loss_ref[...] += 0.5 * jnp.sum(per_tok, axis=0, keepdims=True)
        dy = err * (1.0 / D)
        dg_ref[...] += jnp.sum(dy * xn, axis=0, keepdims=True)
        t = dy * gv
        dx = r * (t - xn * jnp.mean(t * xn, axis=-1, keepdims=True))
        dx_ref[...] = dx
        dxb_ref[...] = dx.astype(BF16)

    row = pl.BlockSpec((ROWS, D), lambda i: (i, 0))
    vec = pl.BlockSpec((1, D), lambda i: (0, 0))
    return pl.pallas_call(
        kern, name="loss_head", grid=(T // ROWS,),
        in_specs=[row, row, vec],
        out_specs=[row, row, vec, pl.BlockSpec((1, 128), lambda i: (0, 0))],
        out_shape=[jax.ShapeDtypeStruct((T, D), F32), jax.ShapeDtypeStruct((T, D), BF16),
                   jax.ShapeDtypeStruct((1, D), F32), jax.ShapeDtypeStruct((1, 128), F32)],
        compiler_params=_params("arbitrary"),
    )(x3, tgt, g)


def _spread_matrix():
    head_of_lane = np.arange(N_HEADS * HEAD_DIM) // HEAD_DIM
    return jnp.asarray(np.arange(N_HEADS)[:, None] == head_of_lane[None, :], dtype=BF16)


def _pieces(v, n):
    out = []
    for _ in range(n):
        piece = v.astype(BF16)
        out.append(piece)
        v = v - piece.astype(F32)
    return out


def _spread(v, spread):
    return sum(_dot(p, spread, NN) for p in _pieces(v, 3))


def _head_sums(v, spread):
    return sum(_dot(p, spread, NT) for p in _pieces(v, 2))


def _branch_weights(l1, l2, l3):
    lm = jnp.maximum(jnp.maximum(l1, l2), l3)
    e1, e2, e3 = jnp.exp(l1 - lm), jnp.exp(l2 - lm), jnp.exp(l3 - lm)
    inv = 1.0 / (e1 + e2 + e3)
    return e1 * inv, e2 * inv, e3 * inv


def _regroup_scratch(rows, width):
    return pltpu.VMEM((width // LANES, rows, LANES), F32)


def _to_token_order(view_ref, scr, dil):
    if dil == 1:
        return view_ref[...]
    n_l, w = view_ref.shape[0], view_ref.shape[1] // dil
    for r in range(dil):
        for cb in range(w // LANES):
            scr[cb, pl.ds(r, n_l, stride=dil), :] = view_ref[:, r * w + cb * LANES:r * w + (cb + 1) * LANES]
    return jnp.concatenate([scr[cb] for cb in range(w // LANES)], axis=1)


def _to_class_order(val, scr, view_ref, dil):
    if dil == 1:
        view_ref[...] = val.astype(view_ref.dtype)
        return
    n, w = val.shape
    for cb in range(w // LANES):
        scr[cb] = val[:, cb * LANES:(cb + 1) * LANES]
    for r in range(dil):
        for cb in range(w // LANES):
            view_ref[:, r * w + cb * LANES:r * w + (cb + 1) * LANES] = (
                scr[cb, pl.ds(r, n // dil, stride=dil), :].astype(view_ref.dtype))


def _view_spec(rows, width, dil):
    return pl.BlockSpec((rows // dil, dil * width), lambda i: (i, 0))


def _mix_fwd(oa, obs, lbs, ga, gb, dils):
    T, W = oa.shape

    def kern(oa_ref, o1, o2, o3, l1, l2, l3, ga_ref, gb_ref, sp_ref, mix_ref, *scr):
        sp = sp_ref[...]
        w1, w2, w3 = _branch_weights(l1[...], l2[...], l3[...])
        on = [_to_token_order(o, s, d) for o, s, d in zip((o1, o2, o3), scr, dils)]
        ob = _spread(w1, sp) * on[0] + _spread(w2, sp) * on[1] + _spread(w3, sp) * on[2]
        oav = oa_ref[...]
        mix_ref[:, :W] = ((oav * _rstd(oav)) * ga_ref[...]).astype(BF16)
        mix_ref[:, W:] = ((ob * _rstd(ob)) * gb_ref[...]).astype(BF16)

    row = pl.BlockSpec((MIX_ROWS, W), lambda i: (i, 0))
    per_head = pl.BlockSpec((MIX_ROWS, N_HEADS), lambda i: (i, 0))
    vec = pl.BlockSpec((1, W), lambda i: (0, 0))
    return pl.pallas_call(
        kern, name="mix_fwd", grid=(T // MIX_ROWS,),
        in_specs=[row] + [_view_spec(MIX_ROWS, W, d) for d in dils] + [per_head] * 3
        + [vec, vec, pl.BlockSpec((N_HEADS, W), lambda i: (0, 0))],
        out_specs=pl.BlockSpec((MIX_ROWS, 2 * W), lambda i: (i, 0)),
        out_shape=jax.ShapeDtypeStruct((T, 2 * W), BF16),
        scratch_shapes=[_regroup_scratch(MIX_ROWS, W)] * 3,
        compiler_params=_params("parallel"),
    )(oa, *obs, *lbs, ga, gb, _spread_matrix())


def _mix_bwd(dmix, oa, obs, lbs, ga, gb, dils, exchange=None):
    T, W = oa.shape
    ride = _Ride(exchange, 11, 10, 3)

    def kern(*refs):
        ins, outs, scr, exrefs = ride.split(refs)
        ride.around(pl.program_id(0), T // MIX_ROWS, exrefs, lambda: compute(*ins, *outs, *scr))

    def compute(dm_ref, oa_ref, o1, o2, o3, l1, l2, l3, ga_ref, gb_ref, sp_ref,
                doa_ref, da_ref, do1, do2, do3, d1, d2, d3, dga_ref, dgb_ref, *scr):
        @pl.when(pl.program_id(0) == 0)
        def _():
            dga_ref[...] = jnp.zeros_like(dga_ref)
            dgb_ref[...] = jnp.zeros_like(dgb_ref)

        sp = sp_ref[...]
        oav = oa_ref[...]
        r = _rstd(oav)
        on = oav * r
        dy = dm_ref[:, :W]
        dga_ref[...] += jnp.sum(dy * on, axis=0, keepdims=True)
        t = dy * ga_ref[...]
        doa = r * (t - on * jnp.mean(t * on, axis=-1, keepdims=True))
        doa_ref[...] = doa.astype(BF16)
        da_ref[...] = _head_sums(doa * oav, sp)
        w1, w2, w3 = _branch_weights(l1[...], l2[...], l3[...])
        s1, s2, s3 = _spread(w1, sp), _spread(w2, sp), _spread(w3, sp)
        on = [_to_token_order(o, sc, d) for o, sc, d in zip((o1, o2, o3), scr, dils)]
        ob = s1 * on[0] + s2 * on[1] + s3 * on[2]
        r = _rstd(ob)
        on = ob * r
        dy = dm_ref[:, W:]
        dgb_ref[...] += jnp.sum(dy * on, axis=0, keepdims=True)
        t = dy * gb_ref[...]
        dob = r * (t - on * jnp.mean(t * on, axis=-1, keepdims=True))
        c = _head_sums(dob * ob, sp)
        for do_ref, sn, sc, d in zip((do1, do2, do3), (s1, s2, s3), scr, dils):
            _to_class_order(sn * dob, sc, do_ref, d)
        d1[...] = w1 * c
        d2[...] = w2 * c
        d3[...] = w3 * c

    row = pl.BlockSpec((MIX_ROWS, W), lambda i: (i, 0))
    per_head = pl.BlockSpec((MIX_ROWS, N_HEADS), lambda i: (i, 0))
    vec = pl.BlockSpec((1, W), lambda i: (0, 0))
    bf = jax.ShapeDtypeStruct((T, W), BF16)
    ph = jax.ShapeDtypeStruct((T, N_HEADS), F32)
    vv = jax.ShapeDtypeStruct((1, W), F32)
    views = [_view_spec(MIX_ROWS, W, d) for d in dils]
    return pl.pallas_call(
        kern, name="mix_bwd", grid=(T // MIX_ROWS,),
        in_specs=[pl.BlockSpec((MIX_ROWS, 2 * W), lambda i: (i, 0)), row] + views + [per_head] * 3 + [vec, vec,
                  pl.BlockSpec((N_HEADS, W), lambda i: (0, 0))] + ride.in_specs,
        out_specs=[row, per_head] + views + [per_head, per_head, per_head, vec, vec] + ride.out_specs,
        out_shape=[bf, ph] + [jax.ShapeDtypeStruct(o.shape, F32) for o in obs] + [ph, ph, ph, vv, vv]
        + ride.out_shapes,
        scratch_shapes=[_regroup_scratch(MIX_ROWS, W)] * 3 + ride.scratch,
        input_output_aliases=ride.aliases,
        compiler_params=_params("arbitrary"),
    )(dmix, oa, *obs, *lbs, ga, gb, _spread_matrix(), *ride.args)


def _alibi_slopes(n):
    return np.asarray(2.0 ** (-8.0 * (np.arange(n) + 1) / n)).astype(np.float32)


def _band_bias(max_steps, step_dist):
    qi = np.arange(BLOCK)[None, :]
    kj = np.arange(BLOCK)[:, None]
    slopes = _alibi_slopes(N_HEADS)
    halves = []
    for steps in (qi + BLOCK - kj, qi - kj):
        valid = (steps >= 0) & (steps <= max_steps)
        alibi = slopes[:, None, None] * (step_dist * steps).astype(np.float32)[None]
        halves.append(np.where(valid[None], -alibi, np.float32(NEG_INF)).astype(np.float32))
    per_head = np.concatenate(halves, axis=1)
    return jnp.asarray(np.concatenate([per_head[0::2], per_head[1::2]], axis=2))


class _AttnLayout:
    def __init__(self, dil, kv_heads, q_stride, q_off, k_stride, k_off, v_off):
        self.dil = dil
        self.kv_heads = kv_heads
        self.kw = kv_heads * HEAD_DIM
        self.rep = N_HEADS // kv_heads
        self.q_col = lambda r: r * q_stride + q_off
        self.k_col = lambda r: r * k_stride + k_off
        self.v_col = lambda r: r * k_stride + v_off


QW = N_HEADS * HEAD_DIM
LANES = 128


PAIRS = N_HEADS // 2


def _pair_cols(pair):
    return slice(pair * LANES, (pair + 1) * LANES)


def _first_head_lanes(shape):
    return lax.broadcasted_iota(jnp.int32, shape, 1) < HEAD_DIM


def _split_heads(pair):
    first = _first_head_lanes(pair.shape)
    zero = jnp.zeros_like(pair)
    return jnp.concatenate([jnp.where(first, pair, zero), jnp.where(first, zero, pair)], axis=0)


def _kv_pair(ref, pair, rep):
    if rep == 1:
        return ref[:, _pair_cols(pair)]
    blk = ref[...].astype(F32)
    other = pltpu.roll(blk, HEAD_DIM, 1)
    first = _first_head_lanes(blk.shape)
    both = jnp.where(first, blk, other) if (2 * pair // rep) % 2 == 0 else jnp.where(first, other, blk)
    return both.astype(ref.dtype)


def _paired_kv(prev_ref, cur_ref, rep, transposed=False):
    memo = {}

    def get(pair):
        key = pair if rep == 1 else 2 * pair // rep
        if key not in memo:
            blocks = [_kv_pair(ref, pair, rep) for ref in (prev_ref, cur_ref)]
            memo[key] = jnp.concatenate([b.T for b in blocks], axis=1) if transposed else jnp.concatenate(blocks, axis=0)
        return memo[key]

    return get


def _attn_fwd(proj, bias, sinks, lay, name, exchange=None):
    L = proj.shape[0]
    nb = L // BLOCK
    kw, rep = lay.kw, lay.rep
    use_sinks = sinks is not None
    scale = HEAD_DIM ** -0.5
    ride = _Ride(exchange, 7 if use_sinks else 6, 2, 2)

    def kern(*refs):
        ins, (o_ref, l_ref), (sc_ref, pr_ref), exrefs = ride.split(refs)
        q_ref, kc_ref, kp_ref, vc_ref, vp_ref, b_ref = ins[:6]
        s_ref = ins[6] if use_sinks else None
        r, i = pl.program_id(0), pl.program_id(1)
        first = i == 0
        ride.around(r * nb + i, lay.dil * nb, exrefs,
                    lambda: compute(q_ref, kc_ref, kp_ref, vc_ref, vp_ref, b_ref, s_ref, o_ref, l_ref, first,
                                    sc_ref, pr_ref))

    def compute(q_ref, kc_ref, kp_ref, vc_ref, vp_ref, b_ref, s_ref, o_ref, l_ref, first, sc_ref, pr_ref):
        keys, values_t = _paired_kv(kp_ref, kc_ref, rep), _paired_kv(vp_ref, vc_ref, rep, transposed=True)
        for pair in range(PAIRS):
            qs = _split_heads(q_ref[:, _pair_cols(pair)])
            s = _dot(keys(pair), qs, NT) * scale + b_ref[pair]
            sc_ref[pair, :BLOCK] = jnp.where(first, NEG_INF, s[:BLOCK])
            sc_ref[pair, BLOCK:] = s[BLOCK:]
        inv = []
        for h in range(N_HEADS):
            cols = slice(h % 2 * BLOCK, (h % 2 + 1) * BLOCK)
            s = sc_ref[h // 2, :, cols]
            m = jnp.max(s, axis=0, keepdims=True)
            if use_sinks:
                sink = s_ref[:, h:h + 1]
                m = jnp.maximum(m, sink)
            p = jnp.exp(s - m)
            denom = jnp.sum(p, axis=0, keepdims=True)
            if use_sinks:
                denom = denom + jnp.exp(sink - m)
            pr_ref[h // 2, :, cols] = p.astype(BF16)
            l_ref[h:h + 1, :] = m + jnp.log(denom)
            inv.append(1.0 / denom)
        for pair in range(PAIRS):
            both = _dot(values_t(pair), pr_ref[pair], NN)
            o_t = jnp.concatenate([both[:HEAD_DIM, :BLOCK] * inv[2 * pair], both[HEAD_DIM:, BLOCK:] * inv[2 * pair + 1]],
                                  axis=0)
            o_ref[:, _pair_cols(pair)] = o_t.T

    prev = lambda i: jnp.maximum(i - 1, 0)
    in_specs = [
        pl.BlockSpec((BLOCK, QW), lambda r, i: (i, lay.q_col(r))),
        pl.BlockSpec((BLOCK, kw), lambda r, i: (i, lay.k_col(r))),
        pl.BlockSpec((BLOCK, kw), lambda r, i: (prev(i), lay.k_col(r))),
        pl.BlockSpec((BLOCK, kw), lambda r, i: (i, lay.v_col(r))),
        pl.BlockSpec((BLOCK, kw), lambda r, i: (prev(i), lay.v_col(r))),
        pl.BlockSpec((PAIRS, 2 * BLOCK, 2 * BLOCK), lambda r, i: (0, 0, 0)),
    ]
    args = [proj, proj, proj, proj, proj, bias]
    if use_sinks:
        in_specs.append(pl.BlockSpec((1, N_HEADS), lambda r, i: (0, 0)))
        args.append(sinks)
    out_specs = [pl.BlockSpec((BLOCK, QW), lambda r, i: (i, r)),
                 pl.BlockSpec((None, N_HEADS, BLOCK), lambda r, i: (r, 0, i))]
    out_shape = [jax.ShapeDtypeStruct((L, lay.dil * QW), F32), jax.ShapeDtypeStruct((lay.dil, N_HEADS, L), F32)]
    return pl.pallas_call(
        kern, name=name, grid=(lay.dil, nb),
        in_specs=in_specs + ride.in_specs, out_specs=out_specs + ride.out_specs,
        out_shape=out_shape + ride.out_shapes,
        scratch_shapes=[pltpu.VMEM((PAIRS, 2 * BLOCK, 2 * BLOCK), dt) for dt in (F32, BF16)] + ride.scratch,
        input_output_aliases=ride.aliases,
        compiler_params=_params("arbitrary", "arbitrary"),
    )(*args, *ride.args)


def _attn_bwd(proj, do, lse, dd, bias, sinks, lay, name, exchange=None):
    L = proj.shape[0]
    nb = L // BLOCK
    kw, rep = lay.kw, lay.rep
    assert rep == 1 or lay.kv_heads == 2, "grouped queries: the two kv heads fill one 128-lane block"
    use_sinks = sinks is not None
    scale = HEAD_DIM ** -0.5
    ride = _Ride(exchange, 10 if use_sinks else 9, 4 if use_sinks else 3, 6)

    def kern(*refs):
        ins, outs, (ck_ref, cv_ref, *staged), exrefs = ride.split(refs)
        q_ref, kc_ref, kp_ref, vc_ref, vp_ref, do_ref, l_ref, d_ref, b_ref = ins[:9]
        s_ref = ins[9] if use_sinks else None
        dq_ref, dk_ref, dv_ref = outs[:3]
        ds_ref = outs[3] if use_sinks else None
        r = pl.program_id(0)
        i = pl.program_id(1)
        ride.around(r * (nb + 1) + i, lay.dil * (nb + 1), exrefs,
                    lambda: compute(q_ref, kc_ref, kp_ref, vc_ref, vp_ref, do_ref, l_ref, d_ref, b_ref, s_ref,
                                    dq_ref, dk_ref, dv_ref, ds_ref, ck_ref, cv_ref, r, i, *staged))

    def compute(q_ref, kc_ref, kp_ref, vc_ref, vp_ref, do_ref, l_ref, d_ref, b_ref, s_ref,
                dq_ref, dk_ref, dv_ref, ds_ref, ck_ref, cv_ref, r, i, sc_ref, dp_ref, pr_ref, dsc_ref):
        first = i == 0

        @pl.when(first)
        def _():
            ck_ref[...] = jnp.zeros_like(ck_ref)
            cv_ref[...] = jnp.zeros_like(cv_ref)

        if use_sinks:
            @pl.when(first & (r == 0))
            def _():
                ds_ref[...] = jnp.zeros_like(ds_ref)

        @pl.when(i < nb)
        def _():
            keys, values = _paired_kv(kp_ref, kc_ref, rep), _paired_kv(vp_ref, vc_ref, rep)
            keys_t = _paired_kv(kp_ref, kc_ref, rep, transposed=True)
            for pair in range(PAIRS):
                qs = _split_heads(q_ref[:, _pair_cols(pair)])
                dos = _split_heads(do_ref[:, _pair_cols(pair)].astype(BF16))
                s = _dot(keys(pair), qs, NT) * scale + b_ref[pair]
                sc_ref[pair, :BLOCK] = jnp.where(first, NEG_INF, s[:BLOCK])
                sc_ref[pair, BLOCK:] = s[BLOCK:]
                dp_ref[pair] = _dot(values(pair), dos, NT)
            for h in range(N_HEADS):
                cols = slice(h % 2 * BLOCK, (h % 2 + 1) * BLOCK)
                lrow = l_ref[h:h + 1, :]
                drow = d_ref[h:h + 1, :]
                p = jnp.exp(sc_ref[h // 2, :, cols] - lrow)
                pr_ref[h // 2, :, cols] = p.astype(BF16)
                dsc_ref[h // 2, :, cols] = (p * (dp_ref[h // 2, :, cols] - drow) * scale).astype(BF16)
                if use_sinks:
                    ds_ref[h:h + 1, :] += -(jnp.exp(s_ref[:, h:h + 1] - lrow) * drow)
            grouped = {}
            for pair in range(PAIRS):
                cols = _pair_cols(pair)
                qs = _split_heads(q_ref[:, cols])
                dos = _split_heads(do_ref[:, cols].astype(BF16))
                ds = dsc_ref[pair]
                both = _dot(keys_t(pair), ds, NN)
                dq_ref[:, cols] = jnp.concatenate([both[:HEAD_DIM, :BLOCK], both[HEAD_DIM:, BLOCK:]], axis=0).T
                dk = _dot(ds, qs, NN)
                dv = _dot(pr_ref[pair], dos, NN)
                if rep == 1:
                    dk_ref[:, cols] = ck_ref[:, cols] + dk[:BLOCK]
                    dv_ref[:, cols] = cv_ref[:, cols] + dv[:BLOCK]
                    ck_ref[:, cols] = dk[BLOCK:]
                    cv_ref[:, cols] = dv[BLOCK:]
                else:
                    g = 2 * pair // rep
                    grouped[g] = (dk, dv) if g not in grouped else (grouped[g][0] + dk, grouped[g][1] + dv)
            if rep > 1:
                fold = lambda t: t + pltpu.roll(t, HEAD_DIM, 1)
                first_half = _first_head_lanes((2 * BLOCK, LANES))
                dk = jnp.where(first_half, fold(grouped[0][0]), fold(grouped[1][0]))
                dv = jnp.where(first_half, fold(grouped[0][1]), fold(grouped[1][1]))
                dk_ref[...] = ck_ref[...] + dk[:BLOCK]
                dv_ref[...] = cv_ref[...] + dv[:BLOCK]
                ck_ref[...] = dk[BLOCK:]
                cv_ref[...] = dv[BLOCK:]

        @pl.when(i == nb)
        def _():
            dk_ref[...] = ck_ref[...]
            dv_ref[...] = cv_ref[...]
            if use_sinks:
                @pl.when(r == lay.dil - 1)
                def _():
                    ds_ref[...] = jnp.broadcast_to(jnp.sum(ds_ref[...], axis=1, keepdims=True), ds_ref.shape)

    cur = lambda i: jnp.minimum(i, nb - 1)
    prev = lambda i: jnp.maximum(jnp.minimum(i, nb - 1) - 1, 0)
    done = lambda i: jnp.maximum(i - 1, 0)
    qspec = lambda col: pl.BlockSpec((BLOCK, QW), lambda r, i: (cur(i), col(r)))
    per_head = pl.BlockSpec((None, N_HEADS, BLOCK), lambda r, i: (r, 0, cur(i)))
    in_specs = [
        qspec(lay.q_col),
        pl.BlockSpec((BLOCK, kw), lambda r, i: (cur(i), lay.k_col(r))),
        pl.BlockSpec((BLOCK, kw), lambda r, i: (prev(i), lay.k_col(r))),
        pl.BlockSpec((BLOCK, kw), lambda r, i: (cur(i), lay.v_col(r))),
        pl.BlockSpec((BLOCK, kw), lambda r, i: (prev(i), lay.v_col(r))),
        qspec(lambda r: r), per_head, per_head,
        pl.BlockSpec((PAIRS, 2 * BLOCK, 2 * BLOCK), lambda r, i: (0, 0, 0)),
    ]
    args = [proj, proj, proj, proj, proj, do, lse, dd, bias]
    out_specs = [
        qspec(lambda r: r),
        pl.BlockSpec((BLOCK, kw), lambda r, i: (done(i), r)),
        pl.BlockSpec((BLOCK, kw), lambda r, i: (done(i), r)),
    ]
    dkv_shape = jax.ShapeDtypeStruct((L, lay.dil * kw), F32)
    out_shape = [jax.ShapeDtypeStruct((L, lay.dil * QW), F32), dkv_shape, dkv_shape]
    if use_sinks:
        in_specs.append(pl.BlockSpec((1, N_HEADS), lambda r, i: (0, 0)))
        args.append(sinks)
        out_specs.append(pl.BlockSpec((N_HEADS, LANES), lambda r, i: (0, 0)))
        out_shape.append(jax.ShapeDtypeStruct((N_HEADS, LANES), F32))
    return pl.pallas_call(
        kern, name=name, grid=(lay.dil, nb + 1),
        in_specs=in_specs + ride.in_specs, out_specs=out_specs + ride.out_specs,
        out_shape=out_shape + ride.out_shapes,
        scratch_shapes=[pltpu.VMEM((BLOCK, kw), F32), pltpu.VMEM((BLOCK, kw), F32)]
        + [pltpu.VMEM((PAIRS, 2 * BLOCK, 2 * BLOCK), dt) for dt in (F32, F32, BF16, BF16)] + ride.scratch,
        input_output_aliases=ride.aliases,
        compiler_params=_params("arbitrary", "arbitrary"),
    )(*args, *ride.args)


def _assemble(groups, name, dils=(1,)):
    T = groups[0][0].shape[0] * dils[0]
    widths = [g[0].shape[1] // dils[0] for g in groups]
    total = sum(widths)
    flat = [a for g in groups for a in g]
    member_dils = [d for g in groups for d in dils[:len(g)]]

    def kern(*refs):
        ins = refs[:len(flat)]
        out_ref, cs_ref = refs[len(flat):len(flat) + 2]
        scr = refs[len(flat) + 2:]

        @pl.when(pl.program_id(0) == 0)
        def _():
            cs_ref[...] = jnp.zeros_like(cs_ref)

        pos = off = 0
        for g, w in zip(groups, widths):
            acc = _to_token_order(ins[pos], None, dils[0])
            for j in range(1, len(g)):
                acc = acc + _to_token_order(ins[pos + j], scr[j - 1], dils[j])
            pos += len(g)
            out_ref[:, off:off + w] = acc.astype(BF16)
            cs_ref[:, off:off + w] += jnp.sum(acc, axis=0, keepdims=True)
            off += w

    return pl.pallas_call(
        kern, name=name, grid=(T // ROWS,),
        in_specs=[_view_spec(ROWS, a.shape[1] // d, d) for a, d in zip(flat, member_dils)],
        out_specs=[pl.BlockSpec((ROWS, total), lambda i: (i, 0)), pl.BlockSpec((1, total), lambda i: (0, 0))],
        out_shape=[jax.ShapeDtypeStruct((T, total), BF16), jax.ShapeDtypeStruct((1, total), F32)],
        scratch_shapes=[_regroup_scratch(ROWS, max(widths))] * (len(dils) - 1),
        compiler_params=_params("arbitrary"),
    )(*flat)


def _adamw(w, g, m, v, name):
    R, C = w.shape
    rows = min(R, ROWS)
    assert R % rows == 0

    def kern(w_ref, g_ref, m_ref, v_ref, d_ref, nm_ref, nv_ref):
        gv = g_ref[...]
        mn = ADAM_B1 * m_ref[...] + (1.0 - ADAM_B1) * gv
        vn = ADAM_B2 * v_ref[...] + (1.0 - ADAM_B2) * jnp.square(gv)
        m_hat = mn / (1.0 - ADAM_B1 ** ADAM_STEP)
        v_hat = vn / (1.0 - ADAM_B2 ** ADAM_STEP)
        d_ref[...] = -ADAM_LR * (m_hat / (jnp.sqrt(v_hat) + ADAM_EPS) + ADAM_WD * w_ref[...])
        nm_ref[...] = mn
        nv_ref[...] = vn

    blk = pl.BlockSpec((rows, C), lambda i: (i, 0))
    shp = jax.ShapeDtypeStruct((R, C), F32)
    return pl.pallas_call(
        kern, name=name, grid=(R // rows,),
        in_specs=[blk] * 4, out_specs=[blk] * 3, out_shape=[shp] * 3,
        compiler_params=_params("parallel"),
    )(w, g, m, v)


def _sum_slots(slots, name):
    n, R, C = slots.shape
    SUM_ROWS = next(rows for rows in (128, 64, 32, 16) if R % rows == 0)

    def kern(s_ref, o_ref):
        acc = s_ref[0].astype(F32)
        for k in range(1, n):
            acc = acc + s_ref[k].astype(F32)
        o_ref[...] = acc

    return pl.pallas_call(
        kern, name=name, grid=(R // SUM_ROWS,),
        in_specs=[pl.BlockSpec((n, SUM_ROWS, C), lambda i: (0, i, 0))],
        out_specs=pl.BlockSpec((SUM_ROWS, C), lambda i: (i, 0)),
        out_shape=jax.ShapeDtypeStruct((R, C), F32),
        compiler_params=_params("parallel"),
    )(slots)


def _place():
    return lax.axis_index("x"), lax.axis_index("y"), lax.axis_index("c")


def _index(p):
    return 4 * p[0] + 2 * p[1] + p[2]


FLIPS = [(fx, fy, fc) for fx in (0, 1) for fy in (0, 1) for fc in (0, 1)][1:]


def _peer(me, flip):
    return tuple(1 - a if f else a for a, f in zip(me, flip))


def _gather_rows(shards, part=(0, 1), into=None):
    nw = len(shards)

    def plan(ins, outs, send_sems, recv_sems):
        x, y, c = me = _place()
        sibling = (x, y, 1 - c)
        chips = [(1 - x, y), (x, 1 - y), (1 - x, 1 - y)]

        def span(w):
            cnt = ins[w].shape[0] // part[1]
            return part[0] * cnt, cnt

        def rows(w, p):
            lo, cnt = span(w)
            return outs[w].at[pl.ds(_index(p) * ins[w].shape[0] + lo, cnt), :]

        def own(w):
            lo, cnt = span(w)
            return ins[w].at[pl.ds(lo, cnt), :]

        def copy(w, k, block, to):
            return pltpu.make_async_remote_copy(
                src_ref=own(w) if block is me else rows(w, block), dst_ref=rows(w, block),
                send_sem=send_sems.at[7 * w + k], recv_sem=recv_sems.at[7 * w + k],
                device_id=to, device_id_type=MESH)

        return me, sibling, chips, c, rows, own, copy

    def copies(ins, outs, send_sems, recv_sems, local_sems):
        me, sibling, chips, c, rows, own, copy = plan(ins, outs, send_sems, recv_sems)
        local = [pltpu.make_async_copy(own(w), rows(w, me), local_sems.at[w]) for w in range(nw)]
        sends, recvs = [], []
        for w in range(nw):
            sends.append(copy(w, 0, me, sibling))
            sends += [copy(w, 1 + j, me, (*chip, c)) for j, chip in enumerate(chips)]
            recvs.append(copy(w, 0, sibling, me))
            recvs += [copy(w, 4 + j, (*chip, 1 - c), me) for j, chip in enumerate(chips)]
        return local, sends, recvs

    def relay(ins, outs, send_sems, recv_sems, local_sems):
        me, sibling, chips, c, rows, own, copy = plan(ins, outs, send_sems, recv_sems)
        arrived = [copy(w, 1 + j, (*chip, c), me) for w in range(nw) for j, chip in enumerate(chips)]
        onward = [copy(w, 4 + j, (*chip, c), sibling) for w in range(nw) for j, chip in enumerate(chips)]
        return arrived, onward

    shapes = [jax.ShapeDtypeStruct((N_DEV * s.shape[0], s.shape[1]), s.dtype) for s in shards]
    aliases = {nw + w: w for w in range(nw)} if into else None
    return _Exchange(shards + (into or []), shapes, 7 * nw, nw, copies, aliases=aliases, relay=relay)


def _scatter_rows(parts, part=(0, 1)):
    nw = len(parts)

    def copies(ins, outs, send_sems, recv_sems, local_sems):
        me = _place()

        def src(w, owner):
            n = ins[w].shape[0] // N_DEV
            cnt = n // part[1]
            return ins[w].at[pl.ds(_index(owner) * n + part[0] * cnt, cnt), :]

        def copy(k, w, owner, sender, to):
            return pltpu.make_async_remote_copy(
                src_ref=src(w, owner), dst_ref=outs[w].at[_index(sender)],
                send_sem=send_sems.at[nw * k + w], recv_sem=recv_sems.at[nw * k + w],
                device_id=to, device_id_type=MESH)

        local = [pltpu.make_async_copy(src(w, me), outs[w].at[_index(me)], local_sems.at[w]) for w in range(nw)]
        peers = [_peer(me, flip) for flip in FLIPS]
        sends = [copy(k, w, peer, me, peer) for k, peer in enumerate(peers) for w in range(nw)]
        recvs = [copy(k, w, me, peer, me) for k, peer in enumerate(peers) for w in range(nw)]
        return local, sends, recvs

    shapes = [jax.ShapeDtypeStruct((N_DEV, p.shape[0] // N_DEV // part[1], p.shape[1]), p.dtype) for p in parts]
    return _Exchange(parts, shapes, 7 * nw, nw, copies)


def _sum_over_devices(v):
    shape = v.shape

    def body(v_ref, sum_ref, all_ref, send_sems, recv_sems):
        me = _place()
        all_ref[_index(me)] = v_ref[...]
        sends = []
        for k, flip in enumerate(FLIPS):
            peer = _peer(me, flip)
            sends.append(pltpu.make_async_remote_copy(
                src_ref=v_ref, dst_ref=all_ref.at[_index(me)],
                send_sem=send_sems.at[k], recv_sem=recv_sems.at[k], device_id=peer, device_id_type=MESH))
            sends[-1].start()
        for k, flip in enumerate(FLIPS):
            peer = _peer(me, flip)
            pltpu.make_async_remote_copy(
                src_ref=v_ref, dst_ref=all_ref.at[_index(peer)],
                send_sem=send_sems.at[k], recv_sem=recv_sems.at[k], device_id=peer, device_id_type=MESH).wait_recv()
        for cp in sends:
            cp.wait_send()
        acc = all_ref[0]
        for s in range(1, N_DEV):
            acc = acc + all_ref[s]
        sum_ref[...] = acc

    vmem = pl.BlockSpec(memory_space=pltpu.VMEM)
    return pl.pallas_call(
        body, name="sum_small_grads",
        in_specs=[vmem], out_specs=[vmem, vmem],
        out_shape=[jax.ShapeDtypeStruct(shape, F32), jax.ShapeDtypeStruct((N_DEV,) + shape, F32)],
        scratch_shapes=[pltpu.SemaphoreType.DMA((7,)), pltpu.SemaphoreType.DMA((7,))],
    )(v)[0]


SMALL_ROWS = 8


def _pack_small(vectors):
    padded = []
    for vec in vectors:
        vec = vec.reshape(-1)
        padded.append(jnp.pad(vec, (0, -vec.shape[0] % 128)))
    flat = jnp.concatenate(padded)
    flat = jnp.pad(flat, (0, -flat.shape[0] % (SMALL_ROWS * 128)))
    return flat.reshape(SMALL_ROWS, -1)


def _unpack_small(packed, shapes):
    flat = packed.reshape(-1)
    out, off = [], 0
    for shp in shapes:
        n = int(np.prod(shp))
        out.append(flat[off:off + n].reshape(shp))
        off += n + (-n % 128)
    return out


def kernel(x, g_attn, w_in, b_in, sinks_a, g_out_a, g_out_b, w_out, g_mlp, w_1, w_2, g_final, loss_target, m_g_attn, m_w_in, m_b_in, m_sinks_a, m_g_out_a, m_g_out_b, m_w_out, m_g_mlp, m_w_1, m_w_2, m_g_final, v_g_attn, v_w_in, v_b_in, v_sinks_a, v_g_out_a, v_g_out_b, v_w_out, v_g_mlp, v_w_1, v_w_2, v_g_final):
    xs, tgt = x[0], loss_target[0]
    T, D = xs.shape
    n_a = QW + 2 * KV_HEADS_A * HEAD_DIM
    g_fin = g_final.reshape(1, D)

    shards = [w_in[0].T.astype(BF16), w_out[0].astype(BF16), w_1[0].T.astype(BF16), w_2[0].astype(BF16)]
    ident = lambda acc: (acc,)
    add = lambda acc, other: (acc + other,)
    tiles = dict(tm=512, tn=1024)

    h1, w_in_t = _norm_fwd(xs, g_attn, "norm_attn", exchange=_gather_rows(shards[:1]))
    w_in_ta, w_in_tb = w_in_t[:n_a], w_in_t[n_a:]
    proj_a, = _matmul(h1, w_in_ta, "nt", [BF16], add, tm=512, tn=n_a, tk=D, row_ins=[b_in[:, :n_a]], name="proj_a")
    dils = [dil for _, dil in DILATED_BRANCHES]
    *proj_b, w_o = _proj_views(h1, w_in_tb, b_in[:, n_a:], dils, "proj_b", exchange=_gather_rows(shards[1:2]))

    lay_a = _AttnLayout(1, KV_HEADS_A, 0, 0, 0, QW // (KV_HEADS_A * HEAD_DIM), QW // (KV_HEADS_A * HEAD_DIM) + 1)
    bias_a = _band_bias(WINDOW_A - 1, 1)
    o_a, l_a, w_1_t = _attn_fwd(proj_a, bias_a, sinks_a, lay_a, "attn_a_fwd",
                                exchange=_gather_rows(shards[2:3], part=(0, 4)))
    branches = []
    for n, (window, dil) in enumerate(DILATED_BRANCHES):
        lay = _AttnLayout(dil, N_HEADS, 3, 0, 3, 1, 2)
        bias = _band_bias(window // dil, dil)
        ride = _gather_rows(shards[2:3], part=(n + 1, 4), into=[w_1_t])
        o, lse, w_1_t = _attn_fwd(proj_b[n], bias, None, lay, f"attn_b{dil}_fwd", exchange=ride)
        branches.append((lay, bias, proj_b[n], o, lse))
    o_b = [br[3] for br in branches]
    l_b = [br[4].transpose(2, 0, 1).reshape(T, N_HEADS) for br in branches]

    mix = _mix_fwd(o_a, o_b, l_b, g_out_a, g_out_b, dils)
    x2, = _matmul(mix, w_o, "nn", [F32], add, tk=D, tile_ins=[xs], name="out_proj", **tiles)
    h2, = _norm_fwd(x2, g_mlp, "norm_mlp")

    def relu_sq(acc):
        u = jnp.maximum(acc, 0.0)
        return u, u * u

    u, u_sq, w_2_f = _matmul(h2, w_1_t, "nt", [BF16, BF16], relu_sq, tk=D, name="mlp_up",
                             exchange=_gather_rows(shards[3:]), **tiles)
    x3, = _matmul(u_sq, w_2_f, "nn", [F32], add, tk=2048, tile_ins=[x2], name="mlp_down", **tiles)

    dx3, dx3_b, dg_final, loss_dev = _loss_head(x3, tgt, g_fin)

    d_pre, = _matmul(dx3_b, w_2_f, "nt", [BF16], lambda acc, uu: (acc * (2.0 * uu.astype(F32)),),
                     tk=D, tile_ins=[u], name="mlp_down_bwd", **tiles)
    wtiles = dict(tm=1024, tn=1024, tk=1024)
    dw_2, = _matmul(u_sq, dx3_b, "tn", [BF16], ident, name="mlp_down_wgrad", **wtiles)
    dh2, slots_2 = _matmul(d_pre, w_1_t, "nn", [F32], ident, tk=2048, name="mlp_up_bwd", exchange=_scatter_rows([dw_2]),
                           **tiles)
    dw_1_t, = _matmul(d_pre, h2, "tn", [BF16], ident, name="mlp_up_wgrad", **wtiles)
    dx2, dx2_b, dg_mlp = _norm_bwd(dh2, x2, g_mlp, dx3, "norm_mlp_bwd")

    dmix, = _matmul(dx2_b, w_o, "nt", [F32], ident, tk=D, name="out_proj_bwd", **tiles)
    dw_o, = _matmul(mix, dx2_b, "tn", [BF16], ident, name="out_proj_wgrad", **wtiles)
    do_a, dd_a, do1, do2, do3, dd1, dd2, dd3, dg_out_a, dg_out_b, slots_o = _mix_bwd(
        dmix, o_a, o_b, l_b, g_out_a, g_out_b, dils, exchange=_scatter_rows([dw_o]))

    by_class = lambda d, dil: d.reshape(T // dil, dil, N_HEADS).transpose(1, 2, 0)
    slots_1 = [None] * 4
    dq_a, dk_a, dv_a, dsinks, slots_1[0] = _attn_bwd(proj_a, do_a, l_a, by_class(dd_a, 1), bias_a, sinks_a, lay_a,
                                                     "attn_a_bwd", exchange=_scatter_rows([dw_1_t], part=(0, 4)))
    dsinks = dsinks[:, 0].reshape(1, N_HEADS)
    dqs, dks, dvs = [], [], []
    for n, ((lay, bias, view, _, lse), do_n, dd_n) in enumerate(zip(branches, (do1, do2, do3), (dd1, dd2, dd3))):
        dq, dk, dv, slots_1[n + 1] = _attn_bwd(view, do_n, lse, by_class(dd_n, lay.dil), bias, None, lay,
                                               f"attn_b{lay.dil}_bwd",
                                               exchange=_scatter_rows([dw_1_t], part=(n + 1, 4)))
        dqs.append(dq)
        dks.append(dk)
        dvs.append(dv)
    dproj_a, db_a = _assemble([[dq_a], [dk_a], [dv_a]], "dproj_a")
    dproj_b, db_b = _assemble([dqs, dks, dvs], "dproj_b", dils)

    dw_in_ta, = _matmul(dproj_a, h1, "tn", [BF16], ident, tm=n_a, tn=1024, tk=512, name="in_proj_a_wgrad")
    dw_in_tb, = _matmul(dproj_b, h1, "tn", [BF16], ident, name="in_proj_b_wgrad", **wtiles)
    dw_in_t = jnp.concatenate([dw_in_ta, dw_in_tb], axis=0)
    dh1_a, slots_in_a = _matmul(dproj_a, w_in_ta, "nn", [F32], ident, tk=n_a, name="in_proj_a_bwd",
                                exchange=_scatter_rows([dw_in_t], part=(0, 2)), **tiles)
    dh1, slots_in_b = _matmul(dproj_b, w_in_tb, "nn", [F32], add, tk=3 * QW, tile_ins=[dh1_a], name="in_proj_b_bwd",
                              exchange=_scatter_rows([dw_in_t], part=(1, 2)), **tiles)
    dx, _, dg_attn = _norm_bwd(dh1, xs, g_attn, dx2, "norm_attn_bwd")

    g_w_in = jnp.concatenate([_sum_slots(slots_in_a, "sum_w_in_a_grads"), _sum_slots(slots_in_b, "sum_w_in_b_grads")]).T
    g_w_out = _sum_slots(slots_o, "sum_w_out_grads")
    g_w_1 = jnp.concatenate([_sum_slots(s, f"sum_w_1_grads_{n}") for n, s in enumerate(slots_1)]).T
    g_w_2 = _sum_slots(slots_2, "sum_w_2_grads")

    small_w = [g_attn, b_in, sinks_a, g_out_a, g_out_b, g_mlp, g_final]
    small_m = [m_g_attn, m_b_in, m_sinks_a, m_g_out_a, m_g_out_b, m_g_mlp, m_g_final]
    small_v = [v_g_attn, v_b_in, v_sinks_a, v_g_out_a, v_g_out_b, v_g_mlp, v_g_final]
    small_g = [dg_attn, jnp.concatenate([db_a, db_b], axis=1), dsinks, dg_out_a, dg_out_b, dg_mlp, dg_final]
    summed = _sum_over_devices(_pack_small(small_g + [loss_dev[:, :1]]))
    shapes = [w.shape for w in small_w]
    *g_small, loss = _unpack_small(summed, shapes + [()])

    big = [
        _adamw(w_in[0], g_w_in, m_w_in[0], v_w_in[0], "adamw_w_in"),
        _adamw(w_out[0], g_w_out, m_w_out[0], v_w_out[0], "adamw_w_out"),
        _adamw(w_1[0], g_w_1, m_w_1[0], v_w_1[0], "adamw_w_1"),
        _adamw(w_2[0], g_w_2, m_w_2[0], v_w_2[0], "adamw_w_2"),
    ]
    g_packed = _pack_small(g_small)
    small = _adamw(_pack_small(small_w), g_packed, _pack_small(small_m), _pack_small(small_v), "adamw_small")
    small = [_unpack_small(s, shapes) for s in small]

    def ordered(small_list, big_list):
        s = list(small_list)
        return [s[0], big_list[0][None], s[1], s[2], s[3], s[4], big_list[1][None], s[5],
                big_list[2][None], big_list[3][None], s[6]]

    grads = ordered(g_small, [g_w_in, g_w_out, g_w_1, g_w_2])
    deltas = ordered(small[0], [b[0] for b in big])
    new_m = ordered(small[1], [b[1] for b in big])
    new_v = ordered(small[2], [b[2] for b in big])
    return (loss, dx[None], *grads, *deltas, *new_m, *new_v)
```

```python
import numpy as np
import jax
import jax.numpy as jnp
from jax import lax
from jax.experimental import pallas as pl
from jax.experimental.pallas import tpu as pltpu

F32 = jnp.float32
BF16 = jnp.bfloat16

HEAD_DIM = 64
N_HEADS = 16
KV_HEADS_A = 2
BLOCK = 128
WINDOW_A = 128
DILATED_BRANCHES = ((128, 1), (512, 4), (2048, 16))
EPS = 1e-5
NEG_INF = -1e30
N_DEV = 8

ADAM_LR = 0.001
ADAM_B1 = 0.9
ADAM_B2 = 0.999
ADAM_EPS = 1e-08
ADAM_WD = 0.01
ADAM_STEP = 10

VMEM_LIMIT_BYTES = 56 * 1024 * 1024
MESH = pl.DeviceIdType.MESH
ANY = pl.BlockSpec(memory_space=pl.ANY)

NN = (((1,), (0,)), ((), ()))
NT = (((1,), (1,)), ((), ()))
TN = (((0,), (0,)), ((), ()))


def _dot(a, b, dims):
    return lax.dot_general(a, b, dims, preferred_element_type=F32)


def _params(*sem):
    return pltpu.CompilerParams(dimension_semantics=sem, vmem_limit_bytes=VMEM_LIMIT_BYTES)


RELAY_AT = 0.6


class _Exchange:
    def __init__(self, ins, out_shapes, n_remote, n_local, copies, aliases=None, relay=None):
        self.ins, self.out_shapes = list(ins), list(out_shapes)
        self.n_remote, self.n_local = n_remote, n_local
        self.copies = copies
        self.relay = relay
        self.aliases = aliases or {}

    def start(self, refs):
        local, sends, _ = self.copies(*refs)
        for cp in local + sends:
            cp.start()

    def middle(self, refs):
        arrived, onward = self.relay(*refs)
        for got, cp in zip(arrived, onward):
            got.wait_recv()
            cp.start()

    def finish(self, refs):
        local, sends, recvs = self.copies(*refs)
        for cp in recvs:
            cp.wait_recv()
        for cp in sends:
            cp.wait_send()
        for cp in local:
            cp.wait()
        if self.relay:
            for cp in self.relay(*refs)[1]:
                cp.wait_send()


class _Ride:
    def __init__(self, ex, n_in, n_out, n_scratch):
        self.ex = ex
        self.n = (n_in, n_out, n_scratch)
        self.args = ex.ins if ex else []
        self.in_specs = [ANY] * len(self.args)
        self.out_shapes = ex.out_shapes if ex else []
        self.out_specs = [ANY] * len(self.out_shapes)
        self.scratch = [pltpu.SemaphoreType.DMA((ex.n_remote,)), pltpu.SemaphoreType.DMA((ex.n_remote,)),
                        pltpu.SemaphoreType.DMA((max(ex.n_local, 1),))] if ex else []
        self.aliases = {n_in + i: n_out + o for i, o in ex.aliases.items()} if ex else {}

    def split(self, refs):
        n_in, n_out, n_scratch = self.n
        a = n_in
        b = a + len(self.args)
        c = b + n_out
        d = c + len(self.out_shapes)
        e = d + n_scratch
        return refs[:a], refs[b:c], refs[d:e], (refs[a:b], refs[c:d], *refs[e:])

    def around(self, step, n_steps, exrefs, compute):
        if self.ex is None:
            compute()
            return

        @pl.when(step == 0)
        def _():
            self.ex.start(exrefs)

        compute()

        if self.ex.relay:
            @pl.when(step == int(RELAY_AT * (n_steps - 1)))
            def _():
                self.ex.middle(exrefs)

        @pl.when(step == n_steps - 1)
        def _():
            self.ex.finish(exrefs)


def _matmul(a, b, dims, out_dtypes, epilogue, *, tm, tn, tk, name, tile_ins=(), row_ins=(), exchange=None):
    if dims == "tn":
        K, M = a.shape
    else:
        M, K = a.shape
    N = b.shape[0] if dims == "nt" else b.shape[1]
    tm, tn, tk = min(tm, M), min(tn, N), min(tk, K)
    assert M % tm == 0 and N % tn == 0 and K % tk == 0, (name, M, N, K, tm, tn, tk)
    grid = (M // tm, N // tn, K // tk)
    nk = grid[2]
    n_tile, n_row, n_out = len(tile_ins), len(row_ins), len(out_dtypes)
    dn = {"nn": NN, "nt": NT, "tn": TN}[dims]
    ride = _Ride(exchange, 2 + n_tile + n_row, n_out, 1 if nk > 1 else 0)

    def kern(*refs):
        ins, out_refs, scratch, exrefs = ride.split(refs)
        a_ref, b_ref = ins[:2]
        tile_refs = ins[2:2 + n_tile]
        row_refs = ins[2 + n_tile:]
        ids = [pl.program_id(d) for d in range(3)]

        def finish(acc):
            outs = epilogue(acc, *[r[...] for r in tile_refs], *[r[...] for r in row_refs])
            for o_ref, o in zip(out_refs, outs):
                o_ref[...] = o.astype(o_ref.dtype)

        def compute():
            if nk == 1:
                finish(_dot(a_ref[...], b_ref[...], dn))
                return
            acc_ref = scratch[0]

            @pl.when(ids[2] == 0)
            def _():
                acc_ref[...] = jnp.zeros_like(acc_ref)

            acc_ref[...] += _dot(a_ref[...], b_ref[...], dn)

            @pl.when(ids[2] == nk - 1)
            def _():
                finish(acc_ref[...])

        ride.around((ids[0] * grid[1] + ids[1]) * grid[2] + ids[2], grid[0] * grid[1] * grid[2], exrefs, compute)

    if dims == "tn":
        a_spec = pl.BlockSpec((tk, tm), lambda i, j, k: (k, i))
    else:
        a_spec = pl.BlockSpec((tm, tk), lambda i, j, k: (i, k))
    if dims == "nt":
        b_spec = pl.BlockSpec((tn, tk), lambda i, j, k: (j, k))
    else:
        b_spec = pl.BlockSpec((tk, tn), lambda i, j, k: (k, j))
    tile_spec = pl.BlockSpec((tm, tn), lambda i, j, k: (i, j))
    row_spec = pl.BlockSpec((1, tn), lambda i, j, k: (0, j))
    sem = ("arbitrary",) * 3 if exchange else ("parallel", "parallel", "arbitrary")
    return pl.pallas_call(
        kern,
        name=name,
        grid=grid,
        in_specs=[a_spec, b_spec] + [tile_spec] * n_tile + [row_spec] * n_row + ride.in_specs,
        out_specs=[tile_spec] * n_out + ride.out_specs,
        out_shape=[jax.ShapeDtypeStruct((M, N), dt) for dt in out_dtypes] + ride.out_shapes,
        scratch_shapes=([pltpu.VMEM((tm, tn), F32)] if nk > 1 else []) + ride.scratch,
        input_output_aliases=ride.aliases,
        compiler_params=_params(*sem),
    )(a, b, *tile_ins, *row_ins, *ride.args)


PROJ_ROWS = 256


def _proj_views(a, w_t, bias, dils, name, exchange=None):
    T, K = a.shape
    N = w_t.shape[0]
    ride = _Ride(exchange, 3, len(dils), 1)

    def kern(*refs):
        (a_ref, w_ref, b_ref), outs, (scr,), exrefs = ride.split(refs)

        def compute():
            acc = _dot(a_ref[...], w_ref[...], NT) + b_ref[...]
            for out_ref, dil in zip(outs, dils):
                _to_class_order(acc, scr, out_ref, dil)

        ride.around(pl.program_id(0), T // PROJ_ROWS, exrefs, compute)

    return pl.pallas_call(
        kern, name=name, grid=(T // PROJ_ROWS,),
        in_specs=[pl.BlockSpec((PROJ_ROWS, K), lambda i: (i, 0)), pl.BlockSpec((N, K), lambda i: (0, 0)),
                  pl.BlockSpec((1, N), lambda i: (0, 0))] + ride.in_specs,
        out_specs=[_view_spec(PROJ_ROWS, N, d) for d in dils] + ride.out_specs,
        out_shape=[jax.ShapeDtypeStruct((T // d, d * N), BF16) for d in dils] + ride.out_shapes,
        scratch_shapes=[_regroup_scratch(PROJ_ROWS, N)] + ride.scratch,
        input_output_aliases=ride.aliases,
        compiler_params=_params("arbitrary"),
    )(a, w_t, bias, *ride.args)


ROWS = 256
MIX_ROWS = 128


def _rstd(xv):
    return lax.rsqrt(jnp.mean(xv * xv, axis=-1, keepdims=True) + EPS)


def _norm_fwd(x, g, name, exchange=None):
    T, D = x.shape
    ride = _Ride(exchange, 2, 1, 0)

    def kern(*refs):
        (x_ref, g_ref), (h_ref,), _, exrefs = ride.split(refs)

        def compute():
            xv = x_ref[...]
            h_ref[...] = ((xv * _rstd(xv)) * g_ref[...]).astype(h_ref.dtype)

        ride.around(pl.program_id(0), T // ROWS, exrefs, compute)

    row = pl.BlockSpec((ROWS, D), lambda i: (i, 0))
    return pl.pallas_call(
        kern, name=name, grid=(T // ROWS,),
        in_specs=[row, pl.BlockSpec((1, D), lambda i: (0, 0))] + ride.in_specs,
        out_specs=[row] + ride.out_specs,
        out_shape=[jax.ShapeDtypeStruct((T, D), BF16)] + ride.out_shapes,
        scratch_shapes=ride.scratch, input_output_aliases=ride.aliases,
        compiler_params=_params("arbitrary"),
    )(x, g, *ride.args)


def _norm_bwd(dh, x, g, res, name, exchange=None):
    T, D = x.shape
    ride = _Ride(exchange, 4, 3, 0)

    def kern(*refs):
        (dh_ref, x_ref, g_ref, res_ref), (dx_ref, dxb_ref, dg_ref), _, exrefs = ride.split(refs)

        def compute():
            @pl.when(pl.program_id(0) == 0)
            def _():
                dg_ref[...] = jnp.zeros_like(dg_ref)

            xv = x_ref[...]
            r = _rstd(xv)
            xn = xv * r
            dhv = dh_ref[...]
            dg_ref[...] += jnp.sum(dhv * xn, axis=0, keepdims=True)
            t = dhv * g_ref[...]
            dx = res_ref[...] + r * (t - xn * jnp.mean(t * xn, axis=-1, keepdims=True))
            dx_ref[...] = dx
            dxb_ref[...] = dx.astype(BF16)

        ride.around(pl.program_id(0), T // ROWS, exrefs, compute)

    row = pl.BlockSpec((ROWS, D), lambda i: (i, 0))
    vec = pl.BlockSpec((1, D), lambda i: (0, 0))
    return pl.pallas_call(
        kern, name=name, grid=(T // ROWS,),
        in_specs=[row, row, vec, row] + ride.in_specs,
        out_specs=[row, row, vec] + ride.out_specs,
        out_shape=[jax.ShapeDtypeStruct((T, D), F32), jax.ShapeDtypeStruct((T, D), BF16),
                   jax.ShapeDtypeStruct((1, D), F32)] + ride.out_shapes,
        scratch_shapes=ride.scratch, input_output_aliases=ride.aliases,
        compiler_params=_params("arbitrary"),
    )(dh, x, g, res, *ride.args)


def _loss_head(x3, tgt, g):
    T, D = x3.shape

    def kern(x_ref, t_ref, g_ref, dx_ref, dxb_ref, dg_ref, loss_ref):
        @pl.when(pl.program_id(0) == 0)
        def _():
            dg_ref[...] = jnp.zeros_like(dg_ref)
            loss_ref[...] = jnp.zeros_like(loss_ref)

        xv = x_ref[...]
        gv = g_ref[...]
        r = _rstd(xv)
        xn = xv * r
        err = xn * gv - t_ref[...]
        per_tok = jnp.mean(err * err, axis=-1, keepdims=True)
        loss_ref[...] += 0.5 * jnp.sum(per_tok, axis=0, keepdims=True)
        dy = err * (1.0 / D)
        dg_ref[...] += jnp.sum(dy * xn, axis=0, keepdims=True)
        t = dy * gv
        dx = r * (t - xn * jnp.mean(t * xn, axis=-1, keepdims=True))
        dx_ref[...] = dx
        dxb_ref[...] = dx.astype(BF16)

    row = pl.BlockSpec((ROWS, D), lambda i: (i, 0))
    vec = pl.BlockSpec((1, D), lambda i: (0, 0))
    return pl.pallas_call(
        kern, name="loss_head", grid=(T // ROWS,),
        in_specs=[row, row, vec],
        out_specs=[row, row, vec, pl.BlockSpec((1, 128), lambda i: (0, 0))],
        out_shape=[jax.ShapeDtypeStruct((T, D), F32), jax.ShapeDtypeStruct((T, D), BF16),
                   jax.ShapeDtypeStruct((1, D), F32), jax.ShapeDtypeStruct((1, 128), F32)],
        compiler_params=_params("arbitrary"),
    )(x3, tgt, g)


def _spread_matrix():
    head_of_lane = np.arange(N_HEADS * HEAD_DIM) // HEAD_DIM
    return jnp.asarray(np.arange(N_HEADS)[:, None] == head_of_lane[None, :], dtype=BF16)


def _pieces(v, n):
    out = []
    for _ in range(n):
        piece = v.astype(BF16)
        out.append(piece)
        v = v - piece.astype(F32)
    return out


def _spread(v, spread):
    return sum(_dot(p, spread, NN) for p in _pieces(v, 3))


def _head_sums(v, spread):
    return sum(_dot(p, spread, NT) for p in _pieces(v, 2))


def _branch_weights(l1, l2, l3):
    lm = jnp.maximum(jnp.maximum(l1, l2), l3)
    e1, e2, e3 = jnp.exp(l1 - lm), jnp.exp(l2 - lm), jnp.exp(l3 - lm)
    inv = 1.0 / (e1 + e2 + e3)
    return e1 * inv, e2 * inv, e3 * inv


def _regroup_scratch(rows, width):
    return pltpu.VMEM((width // LANES, rows, LANES), F32)


def _to_token_order(view_ref, scr, dil):
    if dil == 1:
        return view_ref[...]
    n_l, w = view_ref.shape[0], view_ref.shape[1] // dil
    for r in range(dil):
        for cb in range(w // LANES):
            scr[cb, pl.ds(r, n_l, stride=dil), :] = view_ref[:, r * w + cb * LANES:r * w + (cb + 1) * LANES]
    return jnp.concatenate([scr[cb] for cb in range(w // LANES)], axis=1)


def _to_class_order(val, scr, view_ref, dil):
    if dil == 1:
        view_ref[...] = val.astype(view_ref.dtype)
        return
    n, w = val.shape
    for cb in range(w // LANES):
        scr[cb] = val[:, cb * LANES:(cb + 1) * LANES]
    for r in range(dil):
        for cb in range(w // LANES):
            view_ref[:, r * w + cb * LANES:r * w + (cb + 1) * LANES] = (
                scr[cb, pl.ds(r, n // dil, stride=dil), :].astype(view_ref.dtype))


def _view_spec(rows, width, dil):
    return pl.BlockSpec((rows // dil, dil * width), lambda i: (i, 0))


def _mix_fwd(oa, obs, lbs, ga, gb, dils):
    T, W = oa.shape

    def kern(oa_ref, o1, o2, o3, l1, l2, l3, ga_ref, gb_ref, sp_ref, mix_ref, *scr):
        sp = sp_ref[...]
        w1, w2, w3 = _branch_weights(l1[...], l2[...], l3[...])
        on = [_to_token_order(o, s, d) for o, s, d in zip((o1, o2, o3), scr, dils)]
        ob = _spread(w1, sp) * on[0] + _spread(w2, sp) * on[1] + _spread(w3, sp) * on[2]
        oav = oa_ref[...]
        mix_ref[:, :W] = ((oav * _rstd(oav)) * ga_ref[...]).astype(BF16)
        mix_ref[:, W:] = ((ob * _rstd(ob)) * gb_ref[...]).astype(BF16)

    row = pl.BlockSpec((MIX_ROWS, W), lambda i: (i, 0))
    per_head = pl.BlockSpec((MIX_ROWS, N_HEADS), lambda i: (i, 0))
    vec = pl.BlockSpec((1, W), lambda i: (0, 0))
    return pl.pallas_call(
        kern, name="mix_fwd", grid=(T // MIX_ROWS,),
        in_specs=[row] + [_view_spec(MIX_ROWS, W, d) for d in dils] + [per_head] * 3
        + [vec, vec, pl.BlockSpec((N_HEADS, W), lambda i: (0, 0))],
        out_specs=pl.BlockSpec((MIX_ROWS, 2 * W), lambda i: (i, 0)),
        out_shape=jax.ShapeDtypeStruct((T, 2 * W), BF16),
        scratch_shapes=[_regroup_scratch(MIX_ROWS, W)] * 3,
        compiler_params=_params("parallel"),
    )(oa, *obs, *lbs, ga, gb, _spread_matrix())


def _mix_bwd(dmix, oa, obs, lbs, ga, gb, dils, exchange=None):
    T, W = oa.shape
    ride = _Ride(exchange, 11, 10, 3)

    def kern(*refs):
        ins, outs, scr, exrefs = ride.split(refs)
        ride.around(pl.program_id(0), T // MIX_ROWS, exrefs, lambda: compute(*ins, *outs, *scr))

    def compute(dm_ref, oa_ref, o1, o2, o3, l1, l2, l3, ga_ref, gb_ref, sp_ref,
                doa_ref, da_ref, do1, do2, do3, d1, d2, d3, dga_ref, dgb_ref, *scr):
        @pl.when(pl.program_id(0) == 0)
        def _():
            dga_ref[...] = jnp.zeros_like(dga_ref)
            dgb_ref[...] = jnp.zeros_like(dgb_ref)

        sp = sp_ref[...]
        oav = oa_ref[...]
        r = _rstd(oav)
        on = oav * r
        dy = dm_ref[:, :W]
        dga_ref[...] += jnp.sum(dy * on, axis=0, keepdims=True)
        t = dy * ga_ref[...]
        doa = r * (t - on * jnp.mean(t * on, axis=-1, keepdims=True))
        doa_ref[...] = doa.astype(BF16)
        da_ref[...] = _head_sums(doa * oav, sp)
        w1, w2, w3 = _branch_weights(l1[...], l2[...], l3[...])
        s1, s2, s3 = _spread(w1, sp), _spread(w2, sp), _spread(w3, sp)
        on = [_to_token_order(o, sc, d) for o, sc, d in zip((o1, o2, o3), scr, dils)]
        ob = s1 * on[0] + s2 * on[1] + s3 * on[2]
        r = _rstd(ob)
        on = ob * r
        dy = dm_ref[:, W:]
        dgb_ref[...] += jnp.sum(dy * on, axis=0, keepdims=True)
        t = dy * gb_ref[...]
        dob = r * (t - on * jnp.mean(t * on, axis=-1, keepdims=True))
        c = _head_sums(dob * ob, sp)
        for do_ref, sn, sc, d in zip((do1, do2, do3), (s1, s2, s3), scr, dils):
            _to_class_order(sn * dob, sc, do_ref, d)
        d1[...] = w1 * c
        d2[...] = w2 * c
        d3[...] = w3 * c

    row = pl.BlockSpec((MIX_ROWS, W), lambda i: (i, 0))
    per_head = pl.BlockSpec((MIX_ROWS, N_HEADS), lambda i: (i, 0))
    vec = pl.BlockSpec((1, W), lambda i: (0, 0))
    bf = jax.ShapeDtypeStruct((T, W), BF16)
    ph = jax.ShapeDtypeStruct((T, N_HEADS), F32)
    vv = jax.ShapeDtypeStruct((1, W), F32)
    views = [_view_spec(MIX_ROWS, W, d) for d in dils]
    return pl.pallas_call(
        kern, name="mix_bwd", grid=(T // MIX_ROWS,),
        in_specs=[pl.BlockSpec((MIX_ROWS, 2 * W), lambda i: (i, 0)), row] + views + [per_head] * 3 + [vec, vec,
                  pl.BlockSpec((N_HEADS, W), lambda i: (0, 0))] + ride.in_specs,
        out_specs=[row, per_head] + views + [per_head, per_head, per_head, vec, vec] + ride.out_specs,
        out_shape=[bf, ph] + [jax.ShapeDtypeStruct(o.shape, F32) for o in obs] + [ph, ph, ph, vv, vv]
        + ride.out_shapes,
        scratch_shapes=[_regroup_scratch(MIX_ROWS, W)] * 3 + ride.scratch,
        input_output_aliases=ride.aliases,
        compiler_params=_params("arbitrary"),
    )(dmix, oa, *obs, *lbs, ga, gb, _spread_matrix(), *ride.args)


def _alibi_slopes(n):
    return np.asarray(2.0 ** (-8.0 * (np.arange(n) + 1) / n)).astype(np.float32)


def _band_bias(max_steps, step_dist):
    qi = np.arange(BLOCK)[None, :]
    kj = np.arange(BLOCK)[:, None]
    slopes = _alibi_slopes(N_HEADS)
    halves = []
    for steps in (qi + BLOCK - kj, qi - kj):
        valid = (steps >= 0) & (steps <= max_steps)
        alibi = slopes[:, None, None] * (step_dist * steps).astype(np.float32)[None]
        halves.append(np.where(valid[None], -alibi, np.float32(NEG_INF)).astype(np.float32))
    per_head = np.concatenate(halves, axis=1)
    return jnp.asarray(np.concatenate([per_head[0::2], per_head[1::2]], axis=2))


class _AttnLayout:
    def __init__(self, dil, kv_heads, q_stride, q_off, k_stride, k_off, v_off):
        self.dil = dil
        self.kv_heads = kv_heads
        self.kw = kv_heads * HEAD_DIM
        self.rep = N_HEADS // kv_heads
        self.q_col = lambda r: r * q_stride + q_off
        self.k_col = lambda r: r * k_stride + k_off
        self.v_col = lambda r: r * k_stride + v_off


QW = N_HEADS * HEAD_DIM
LANES = 128


PAIRS = N_HEADS // 2


def _pair_cols(pair):
    return slice(pair * LANES, (pair + 1) * LANES)


def _first_head_lanes(shape):
    return lax.broadcasted_iota(jnp.int32, shape, 1) < HEAD_DIM


def _split_heads(pair):
    first = _first_head_lanes(pair.shape)
    zero = jnp.zeros_like(pair)
    return jnp.concatenate([jnp.where(first, pair, zero), jnp.where(first, zero, pair)], axis=0)


def _kv_pair(ref, pair, rep):
    if rep == 1:
        return ref[:, _pair_cols(pair)]
    blk = ref[...].astype(F32)
    other = pltpu.roll(blk, HEAD_DIM, 1)
    first = _first_head_lanes(blk.shape)
    both = jnp.where(first, blk, other) if (2 * pair // rep) % 2 == 0 else jnp.where(first, other, blk)
    return both.astype(ref.dtype)


def _paired_kv(prev_ref, cur_ref, rep, transposed=False):
    memo = {}

    def get(pair):
        key = pair if rep == 1 else 2 * pair // rep
        if key not in memo:
            blocks = [_kv_pair(ref, pair, rep) for ref in (prev_ref, cur_ref)]
            memo[key] = jnp.concatenate([b.T for b in blocks], axis=1) if transposed else jnp.concatenate(blocks, axis=0)
        return memo[key]

    return get


def _attn_fwd(proj, bias, sinks, lay, name, exchange=None):
    L = proj.shape[0]
    nb = L // BLOCK
    kw, rep = lay.kw, lay.rep
    use_sinks = sinks is not None
    scale = HEAD_DIM ** -0.5
    ride = _Ride(exchange, 7 if use_sinks else 6, 2, 2)

    def kern(*refs):
        ins, (o_ref, l_ref), (sc_ref, pr_ref), exrefs = ride.split(refs)
        q_ref, kc_ref, kp_ref, vc_ref, vp_ref, b_ref = ins[:6]
        s_ref = ins[6] if use_sinks else None
        r, i = pl.program_id(0), pl.program_id(1)
        first = i == 0
        ride.around(r * nb + i, lay.dil * nb, exrefs,
                    lambda: compute(q_ref, kc_ref, kp_ref, vc_ref, vp_ref, b_ref, s_ref, o_ref, l_ref, first,
                                    sc_ref, pr_ref))

    def compute(q_ref, kc_ref, kp_ref, vc_ref, vp_ref, b_ref, s_ref, o_ref, l_ref, first, sc_ref, pr_ref):
        keys, values_t = _paired_kv(kp_ref, kc_ref, rep), _paired_kv(vp_ref, vc_ref, rep, transposed=True)
        for pair in range(PAIRS):
            qs = _split_heads(q_ref[:, _pair_cols(pair)])
            s = _dot(keys(pair), qs, NT) * scale + b_ref[pair]
            sc_ref[pair, :BLOCK] = jnp.where(first, NEG_INF, s[:BLOCK])
            sc_ref[pair, BLOCK:] = s[BLOCK:]
        inv = []
        for h in range(N_HEADS):
            cols = slice(h % 2 * BLOCK, (h % 2 + 1) * BLOCK)
            s = sc_ref[h // 2, :, cols]
            m = jnp.max(s, axis=0, keepdims=True)
            if use_sinks:
                sink = s_ref[:, h:h + 1]
                m = jnp.maximum(m, sink)
            p = jnp.exp(s - m)
            denom = jnp.sum(p, axis=0, keepdims=True)
            if use_sinks:
                denom = denom + jnp.exp(sink - m)
            pr_ref[h // 2, :, cols] = p.astype(BF16)
            l_ref[h:h + 1, :] = m + jnp.log(denom)
            inv.append(1.0 / denom)
        for pair in range(PAIRS):
            both = _dot(values_t(pair), pr_ref[pair], NN)
            o_t = jnp.concatenate([both[:HEAD_DIM, :BLOCK] * inv[2 * pair], both[HEAD_DIM:, BLOCK:] * inv[2 * pair + 1]],
                                  axis=0)
            o_ref[:, _pair_cols(pair)] = o_t.T

    prev = lambda i: jnp.maximum(i - 1, 0)
    in_specs = [
        pl.BlockSpec((BLOCK, QW), lambda r, i: (i, lay.q_col(r))),
        pl.BlockSpec((BLOCK, kw), lambda r, i: (i, lay.k_col(r))),
        pl.BlockSpec((BLOCK, kw), lambda r, i: (prev(i), lay.k_col(r))),
        pl.BlockSpec((BLOCK, kw), lambda r, i: (i, lay.v_col(r))),
        pl.BlockSpec((BLOCK, kw), lambda r, i: (prev(i), lay.v_col(r))),
        pl.BlockSpec((PAIRS, 2 * BLOCK, 2 * BLOCK), lambda r, i: (0, 0, 0)),
    ]
    args = [proj, proj, proj, proj, proj, bias]
    if use_sinks:
        in_specs.append(pl.BlockSpec((1, N_HEADS), lambda r, i: (0, 0)))
        args.append(sinks)
    out_specs = [pl.BlockSpec((BLOCK, QW), lambda r, i: (i, r)),
                 pl.BlockSpec((None, N_HEADS, BLOCK), lambda r, i: (r, 0, i))]
    out_shape = [jax.ShapeDtypeStruct((L, lay.dil * QW), F32), jax.ShapeDtypeStruct((lay.dil, N_HEADS, L), F32)]
    return pl.pallas_call(
        kern, name=name, grid=(lay.dil, nb),
        in_specs=in_specs + ride.in_specs, out_specs=out_specs + ride.out_specs,
        out_shape=out_shape + ride.out_shapes,
        scratch_shapes=[pltpu.VMEM((PAIRS, 2 * BLOCK, 2 * BLOCK), dt) for dt in (F32, BF16)] + ride.scratch,
        input_output_aliases=ride.aliases,
        compiler_params=_params("arbitrary", "arbitrary"),
    )(*args, *ride.args)


def _attn_bwd(proj, do, lse, dd, bias, sinks, lay, name, exchange=None):
    L = proj.shape[0]
    nb = L // BLOCK
    kw, rep = lay.kw, lay.rep
    assert rep == 1 or lay.kv_heads == 2, "grouped queries: the two kv heads fill one 128-lane block"
    use_sinks = sinks is not None
    scale = HEAD_DIM ** -0.5
    ride = _Ride(exchange, 10 if use_sinks else 9, 4 if use_sinks else 3, 6)

    def kern(*refs):
        ins, outs, (ck_ref, cv_ref, *staged), exrefs = ride.split(refs)
        q_ref, kc_ref, kp_ref, vc_ref, vp_ref, do_ref, l_ref, d_ref, b_ref = ins[:9]
        s_ref = ins[9] if use_sinks else None
        dq_ref, dk_ref, dv_ref = outs[:3]
        ds_ref = outs[3] if use_sinks else None
        r = pl.program_id(0)
        i = pl.program_id(1)
        ride.around(r * (nb + 1) + i, lay.dil * (nb + 1), exrefs,
                    lambda: compute(q_ref, kc_ref, kp_ref, vc_ref, vp_ref, do_ref, l_ref, d_ref, b_ref, s_ref,
                                    dq_ref, dk_ref, dv_ref, ds_ref, ck_ref, cv_ref, r, i, *staged))

    def compute(q_ref, kc_ref, kp_ref, vc_ref, vp_ref, do_ref, l_ref, d_ref, b_ref, s_ref,
                dq_ref, dk_ref, dv_ref, ds_ref, ck_ref, cv_ref, r, i, sc_ref, dp_ref, pr_ref, dsc_ref):
        first = i == 0

        @pl.when(first)
        def _():
            ck_ref[...] = jnp.zeros_like(ck_ref)
            cv_ref[...] = jnp.zeros_like(cv_ref)

        if use_sinks:
            @pl.when(first & (r == 0))
            def _():
                ds_ref[...] = jnp.zeros_like(ds_ref)

        @pl.when(i < nb)
        def _():
            keys, values = _paired_kv(kp_ref, kc_ref, rep), _paired_kv(vp_ref, vc_ref, rep)
            keys_t = _paired_kv(kp_ref, kc_ref, rep, transposed=True)
            for pair in range(PAIRS):
                qs = _split_heads(q_ref[:, _pair_cols(pair)])
                dos = _split_heads(do_ref[:, _pair_cols(pair)].astype(BF16))
                s = _dot(keys(pair), qs, NT) * scale + b_ref[pair]
                sc_ref[pair, :BLOCK] = jnp.where(first, NEG_INF, s[:BLOCK])
                sc_ref[pair, BLOCK:] = s[BLOCK:]
                dp_ref[pair] = _dot(values(pair), dos, NT)
            for h in range(N_HEADS):
                cols = slice(h % 2 * BLOCK, (h % 2 + 1) * BLOCK)
                lrow = l_ref[h:h + 1, :]
                drow = d_ref[h:h + 1, :]
                p = jnp.exp(sc_ref[h // 2, :, cols] - lrow)
                pr_ref[h // 2, :, cols] = p.astype(BF16)
                dsc_ref[h // 2, :, cols] = (p * (dp_ref[h // 2, :, cols] - drow) * scale).astype(BF16)
                if use_sinks:
                    ds_ref[h:h + 1, :] += -(jnp.exp(s_ref[:, h:h + 1] - lrow) * drow)
            grouped = {}
            for pair in range(PAIRS):
                cols = _pair_cols(pair)
                qs = _split_heads(q_ref[:, cols])
                dos = _split_heads(do_ref[:, cols].astype(BF16))
                ds = dsc_ref[pair]
                both = _dot(keys_t(pair), ds, NN)
                dq_ref[:, cols] = jnp.concatenate([both[:HEAD_DIM, :BLOCK], both[HEAD_DIM:, BLOCK:]], axis=0).T
                dk = _dot(ds, qs, NN)
                dv = _dot(pr_ref[pair], dos, NN)
                if rep == 1:
                    dk_ref[:, cols] = ck_ref[:, cols] + dk[:BLOCK]
                    dv_ref[:, cols] = cv_ref[:, cols] + dv[:BLOCK]
                    ck_ref[:, cols] = dk[BLOCK:]
                    cv_ref[:, cols] = dv[BLOCK:]
                else:
                    g = 2 * pair // rep
                    grouped[g] = (dk, dv) if g not in grouped else (grouped[g][0] + dk, grouped[g][1] + dv)
            if rep > 1:
                fold = lambda t: t + pltpu.roll(t, HEAD_DIM, 1)
                first_half = _first_head_lanes((2 * BLOCK, LANES))
                dk = jnp.where(first_half, fold(grouped[0][0]), fold(grouped[1][0]))
                dv = jnp.where(first_half, fold(grouped[0][1]), fold(grouped[1][1]))
                dk_ref[...] = ck_ref[...] + dk[:BLOCK]
                dv_ref[...] = cv_ref[...] + dv[:BLOCK]
                ck_ref[...] = dk[BLOCK:]
                cv_ref[...] = dv[BLOCK:]

        @pl.when(i == nb)
        def _():
            dk_ref[...] = ck_ref[...]
            dv_ref[...] = cv_ref[...]
            if use_sinks:
                @pl.when(r == lay.dil - 1)
                def _():
                    ds_ref[...] = jnp.broadcast_to(jnp.sum(ds_ref[...], axis=1, keepdims=True), ds_ref.shape)

    cur = lambda i: jnp.minimum(i, nb - 1)
    prev = lambda i: jnp.maximum(jnp.minimum(i, nb - 1) - 1, 0)
    done = lambda i: jnp.maximum(i - 1, 0)
    qspec = lambda col: pl.BlockSpec((BLOCK, QW), lambda r, i: (cur(i), col(r)))
    per_head = pl.BlockSpec((None, N_HEADS, BLOCK), lambda r, i: (r, 0, cur(i)))
    in_specs = [
        qspec(lay.q_col),
        pl.BlockSpec((BLOCK, kw), lambda r, i: (cur(i), lay.k_col(r))),
        pl.BlockSpec((BLOCK, kw), lambda r, i: (prev(i), lay.k_col(r))),
        pl.BlockSpec((BLOCK, kw), lambda r, i: (cur(i), lay.v_col(r))),
        pl.BlockSpec((BLOCK, kw), lambda r, i: (prev(i), lay.v_col(r))),
        qspec(lambda r: r), per_head, per_head,
        pl.BlockSpec((PAIRS, 2 * BLOCK, 2 * BLOCK), lambda r, i: (0, 0, 0)),
    ]
    args = [proj, proj, proj, proj, proj, do, lse, dd, bias]
    out_specs = [
        qspec(lambda r: r),
        pl.BlockSpec((BLOCK, kw), lambda r, i: (done(i), r)),
        pl.BlockSpec((BLOCK, kw), lambda r, i: (done(i), r)),
    ]
    dkv_shape = jax.ShapeDtypeStruct((L, lay.dil * kw), F32)
    out_shape = [jax.ShapeDtypeStruct((L, lay.dil * QW), F32), dkv_shape, dkv_shape]
    if use_sinks:
        in_specs.append(pl.BlockSpec((1, N_HEADS), lambda r, i: (0, 0)))
        args.append(sinks)
        out_specs.append(pl.BlockSpec((N_HEADS, LANES), lambda r, i: (0, 0)))
        out_shape.append(jax.ShapeDtypeStruct((N_HEADS, LANES), F32))
    return pl.pallas_call(
        kern, name=name, grid=(lay.dil, nb + 1),
        in_specs=in_specs + ride.in_specs, out_specs=out_specs + ride.out_specs,
        out_shape=out_shape + ride.out_shapes,
        scratch_shapes=[pltpu.VMEM((BLOCK, kw), F32), pltpu.VMEM((BLOCK, kw), F32)]
        + [pltpu.VMEM((PAIRS, 2 * BLOCK, 2 * BLOCK), dt) for dt in (F32, F32, BF16, BF16)] + ride.scratch,
        input_output_aliases=ride.aliases,
        compiler_params=_params("arbitrary", "arbitrary"),
    )(*args, *ride.args)


def _assemble(groups, name, dils=(1,)):
    T = groups[0][0].shape[0] * dils[0]
    widths = [g[0].shape[1] // dils[0] for g in groups]
    total = sum(widths)
    flat = [a for g in groups for a in g]
    member_dils = [d for g in groups for d in dils[:len(g)]]

    def kern(*refs):
        ins = refs[:len(flat)]
        out_ref, cs_ref = refs[len(flat):len(flat) + 2]
        scr = refs[len(flat) + 2:]

        @pl.when(pl.program_id(0) == 0)
        def _():
            cs_ref[...] = jnp.zeros_like(cs_ref)

        pos = off = 0
        for g, w in zip(groups, widths):
            acc = _to_token_order(ins[pos], None, dils[0])
            for j in range(1, len(g)):
                acc = acc + _to_token_order(ins[pos + j], scr[j - 1], dils[j])
            pos += len(g)
            out_ref[:, off:off + w] = acc.astype(BF16)
            cs_ref[:, off:off + w] += jnp.sum(acc, axis=0, keepdims=True)
            off += w

    return pl.pallas_call(
        kern, name=name, grid=(T // ROWS,),
        in_specs=[_view_spec(ROWS, a.shape[1] // d, d) for a, d in zip(flat, member_dils)],
        out_specs=[pl.BlockSpec((ROWS, total), lambda i: (i, 0)), pl.BlockSpec((1, total), lambda i: (0, 0))],
        out_shape=[jax.ShapeDtypeStruct((T, total), BF16), jax.ShapeDtypeStruct((1, total), F32)],
        scratch_shapes=[_regroup_scratch(ROWS, max(widths))] * (len(dils) - 1),
        compiler_params=_params("arbitrary"),
    )(*flat)


def _adamw(w, g, m, v, name):
    R, C = w.shape
    rows = min(R, ROWS)
    assert R % rows == 0

    def kern(w_ref, g_ref, m_ref, v_ref, d_ref, nm_ref, nv_ref):
        gv = g_ref[...]
        mn = ADAM_B1 * m_ref[...] + (1.0 - ADAM_B1) * gv
        vn = ADAM_B2 * v_ref[...] + (1.0 - ADAM_B2) * jnp.square(gv)
        m_hat = mn / (1.0 - ADAM_B1 ** ADAM_STEP)
        v_hat = vn / (1.0 - ADAM_B2 ** ADAM_STEP)
        d_ref[...] = -ADAM_LR * (m_hat / (jnp.sqrt(v_hat) + ADAM_EPS) + ADAM_WD * w_ref[...])
        nm_ref[...] = mn
        nv_ref[...] = vn

    blk = pl.BlockSpec((rows, C), lambda i: (i, 0))
    shp = jax.ShapeDtypeStruct((R, C), F32)
    return pl.pallas_call(
        kern, name=name, grid=(R // rows,),
        in_specs=[blk] * 4, out_specs=[blk] * 3, out_shape=[shp] * 3,
        compiler_params=_params("parallel"),
    )(w, g, m, v)


def _sum_slots(slots, name):
    n, R, C = slots.shape
    SUM_ROWS = next(rows for rows in (128, 64, 32, 16) if R % rows == 0)

    def kern(s_ref, o_ref):
        acc = s_ref[0].astype(F32)
        for k in range(1, n):
            acc = acc + s_ref[k].astype(F32)
        o_ref[...] = acc

    return pl.pallas_call(
        kern, name=name, grid=(R // SUM_ROWS,),
        in_specs=[pl.BlockSpec((n, SUM_ROWS, C), lambda i: (0, i, 0))],
        out_specs=pl.BlockSpec((SUM_ROWS, C), lambda i: (i, 0)),
        out_shape=jax.ShapeDtypeStruct((R, C), F32),
        compiler_params=_params("parallel"),
    )(slots)


def _place():
    return lax.axis_index("x"), lax.axis_index("y"), lax.axis_index("c")


def _index(p):
    return 4 * p[0] + 2 * p[1] + p[2]


FLIPS = [(fx, fy, fc) for fx in (0, 1) for fy in (0, 1) for fc in (0, 1)][1:]


def _peer(me, flip):
    return tuple(1 - a if f else a for a, f in zip(me, flip))


def _gather_rows(shards, part=(0, 1), into=None):
    nw = len(shards)

    def plan(ins, outs, send_sems, recv_sems):
        x, y, c = me = _place()
        sibling = (x, y, 1 - c)
        chips = [(1 - x, y), (x, 1 - y), (1 - x, 1 - y)]

        def span(w):
            cnt = ins[w].shape[0] // part[1]
            return part[0] * cnt, cnt

        def rows(w, p):
            lo, cnt = span(w)
            return outs[w].at[pl.ds(_index(p) * ins[w].shape[0] + lo, cnt), :]

        def own(w):
            lo, cnt = span(w)
            return ins[w].at[pl.ds(lo, cnt), :]

        def copy(w, k, block, to):
            return pltpu.make_async_remote_copy(
                src_ref=own(w) if block is me else rows(w, block), dst_ref=rows(w, block),
                send_sem=send_sems.at[7 * w + k], recv_sem=recv_sems.at[7 * w + k],
                device_id=to, device_id_type=MESH)

        return me, sibling, chips, c, rows, own, copy

    def copies(ins, outs, send_sems, recv_sems, local_sems):
        me, sibling, chips, c, rows, own, copy = plan(ins, outs, send_sems, recv_sems)
        local = [pltpu.make_async_copy(own(w), rows(w, me), local_sems.at[w]) for w in range(nw)]
        sends, recvs = [], []
        for w in range(nw):
            sends.append(copy(w, 0, me, sibling))
            sends += [copy(w, 1 + j, me, (*chip, c)) for j, chip in enumerate(chips)]
            recvs.append(copy(w, 0, sibling, me))
            recvs += [copy(w, 4 + j, (*chip, 1 - c), me) for j, chip in enumerate(chips)]
        return local, sends, recvs

    def relay(ins, outs, send_sems, recv_sems, local_sems):
        me, sibling, chips, c, rows, own, copy = plan(ins, outs, send_sems, recv_sems)
        arrived = [copy(w, 1 + j, (*chip, c), me) for w in range(nw) for j, chip in enumerate(chips)]
        onward = [copy(w, 4 + j, (*chip, c), sibling) for w in range(nw) for j, chip in enumerate(chips)]
        return arrived, onward

    shapes = [jax.ShapeDtypeStruct((N_DEV * s.shape[0], s.shape[1]), s.dtype) for s in shards]
    aliases = {nw + w: w for w in range(nw)} if into else None
    return _Exchange(shards + (into or []), shapes, 7 * nw, nw, copies, aliases=aliases, relay=relay)


def _scatter_rows(parts, part=(0, 1)):
    nw = len(parts)

    def copies(ins, outs, send_sems, recv_sems, local_sems):
        me = _place()

        def src(w, owner):
            n = ins[w].shape[0] // N_DEV
            cnt = n // part[1]
            return ins[w].at[pl.ds(_index(owner) * n + part[0] * cnt, cnt), :]

        def copy(k, w, owner, sender, to):
            return pltpu.make_async_remote_copy(
                src_ref=src(w, owner), dst_ref=outs[w].at[_index(sender)],
                send_sem=send_sems.at[nw * k + w], recv_sem=recv_sems.at[nw * k + w],
                device_id=to, device_id_type=MESH)

        local = [pltpu.make_async_copy(src(w, me), outs[w].at[_index(me)], local_sems.at[w]) for w in range(nw)]
        peers = [_peer(me, flip) for flip in FLIPS]
        sends = [copy(k, w, peer, me, peer) for k, peer in enumerate(peers) for w in range(nw)]
        recvs = [copy(k, w, me, peer, me) for k, peer in enumerate(peers) for w in range(nw)]
        return local, sends, recvs

    shapes = [jax.ShapeDtypeStruct((N_DEV, p.shape[0] // N_DEV // part[1], p.shape[1]), p.dtype) for p in parts]
    return _Exchange(parts, shapes, 7 * nw, nw, copies)


def _sum_over_devices(v):
    shape = v.shape

    def body(v_ref, sum_ref, all_ref, send_sems, recv_sems):
        me = _place()
        all_ref[_index(me)] = v_ref[...]
        sends = []
        for k, flip in enumerate(FLIPS):
            peer = _peer(me, flip)
            sends.append(pltpu.make_async_remote_copy(
                src_ref=v_ref, dst_ref=all_ref.at[_index(me)],
                send_sem=send_sems.at[k], recv_sem=recv_sems.at[k], device_id=peer, device_id_type=MESH))
            sends[-1].start()
        for k, flip in enumerate(FLIPS):
            peer = _peer(me, flip)
            pltpu.make_async_remote_copy(
                src_ref=v_ref, dst_ref=all_ref.at[_index(peer)],
                send_sem=send_sems.at[k], recv_sem=recv_sems.at[k], device_id=peer, device_id_type=MESH).wait_recv()
        for cp in sends:
            cp.wait_send()
        acc = all_ref[0]
        for s in range(1, N_DEV):
            acc = acc + all_ref[s]
        sum_ref[...] = acc

    vmem = pl.BlockSpec(memory_space=pltpu.VMEM)
    return pl.pallas_call(
        body, name="sum_small_grads",
        in_specs=[vmem], out_specs=[vmem, vmem],
        out_shape=[jax.ShapeDtypeStruct(shape, F32), jax.ShapeDtypeStruct((N_DEV,) + shape, F32)],
        scratch_shapes=[pltpu.SemaphoreType.DMA((7,)), pltpu.SemaphoreType.DMA((7,))],
    )(v)[0]


SMALL_ROWS = 8


def _pack_small(vectors):
    padded = []
    for vec in vectors:
        vec = vec.reshape(-1)
        padded.append(jnp.pad(vec, (0, -vec.shape[0] % 128)))
    flat = jnp.concatenate(padded)
    flat = jnp.pad(flat, (0, -flat.shape[0] % (SMALL_ROWS * 128)))
    return flat.reshape(SMALL_ROWS, -1)


def _unpack_small(packed, shapes):
    flat = packed.reshape(-1)
    out, off = [], 0
    for shp in shapes:
        n = int(np.prod(shp))
        out.append(flat[off:off + n].reshape(shp))
        off += n + (-n % 128)
    return out


def kernel(x, g_attn, w_in, b_in, sinks_a, g_out_a, g_out_b, w_out, g_mlp, w_1, w_2, g_final, loss_target, m_g_attn, m_w_in, m_b_in, m_sinks_a, m_g_out_a, m_g_out_b, m_w_out, m_g_mlp, m_w_1, m_w_2, m_g_final, v_g_attn, v_w_in, v_b_in, v_sinks_a, v_g_out_a, v_g_out_b, v_w_out, v_g_mlp, v_w_1, v_w_2, v_g_final):
    xs, tgt = x[0], loss_target[0]
    T, D = xs.shape
    n_a = QW + 2 * KV_HEADS_A * HEAD_DIM
    g_fin = g_final.reshape(1, D)

    shards = [w_in[0].T.astype(BF16), w_out[0].astype(BF16), w_1[0].T.astype(BF16), w_2[0].astype(BF16)]
    ident = lambda acc: (acc,)
    add = lambda acc, other: (acc + other,)
    tiles = dict(tm=512, tn=1024)

    h1, w_in_t = _norm_fwd(xs, g_attn, "norm_attn", exchange=_gather_rows(shards[:1]))
    w_in_ta, w_in_tb = w_in_t[:n_a], w_in_t[n_a:]
    proj_a, = _matmul(h1, w_in_ta, "nt", [BF16], add, tm=512, tn=n_a, tk=D, row_ins=[b_in[:, :n_a]], name="proj_a")
    dils = [dil for _, dil in DILATED_BRANCHES]
    *proj_b, w_o = _proj_views(h1, w_in_tb, b_in[:, n_a:], dils, "proj_b", exchange=_gather_rows(shards[1:2]))

    lay_a = _AttnLayout(1, KV_HEADS_A, 0, 0, 0, QW // (KV_HEADS_A * HEAD_DIM), QW // (KV_HEADS_A * HEAD_DIM) + 1)
    bias_a = _band_bias(WINDOW_A - 1, 1)
    o_a, l_a, w_1_t = _attn_fwd(proj_a, bias_a, sinks_a, lay_a, "attn_a_fwd",
                                exchange=_gather_rows(shards[2:3], part=(0, 4)))
    branches = []
    for n, (window, dil) in enumerate(DILATED_BRANCHES):
        lay = _AttnLayout(dil, N_HEADS, 3, 0, 3, 1, 2)
        bias = _band_bias(window // dil, dil)
        ride = _gather_rows(shards[2:3], part=(n + 1, 4), into=[w_1_t])
        o, lse, w_1_t = _attn_fwd(proj_b[n], bias, None, lay, f"attn_b{dil}_fwd", exchange=ride)
        branches.append((lay, bias, proj_b[n], o, lse))
    o_b = [br[3] for br in branches]
    l_b = [br[4].transpose(2, 0, 1).reshape(T, N_HEADS) for br in branches]

    mix = _mix_fwd(o_a, o_b, l_b, g_out_a, g_out_b, dils)
    x2, = _matmul(mix, w_o, "nn", [F32], add, tk=D, tile_ins=[xs], name="out_proj", **tiles)
    h2, = _norm_fwd(x2, g_mlp, "norm_mlp")

    def relu_sq(acc):
        u = jnp.maximum(acc, 0.0)
        return u, u * u

    wide = dict(tm=512, tn=2048)
    u, u_sq, w_2_f = _matmul(h2, w_1_t, "nt", [BF16, BF16], relu_sq, tk=D, name="mlp_up",
                             exchange=_gather_rows(shards[3:]), **wide)
    x3, = _matmul(u_sq, w_2_f, "nn", [F32], add, tk=4096, tile_ins=[x2], name="mlp_down", **tiles)

    dx3, dx3_b, dg_final, loss_dev = _loss_head(x3, tgt, g_fin)

    d_pre, = _matmul(dx3_b, w_2_f, "nt", [BF16], lambda acc, uu: (acc * (2.0 * uu.astype(F32)),),
                     tk=D, tile_ins=[u], name="mlp_down_bwd", **wide)
    wtiles = dict(tm=1024, tn=1024, tk=4096)
    dw_2, = _matmul(u_sq, dx3_b, "tn", [BF16], ident, name="mlp_down_wgrad", **wtiles)
    dh2, slots_2 = _matmul(d_pre, w_1_t, "nn", [F32], ident, tk=4096, name="mlp_up_bwd", exchange=_scatter_rows([dw_2]),
                           **tiles)
    dw_1_t, = _matmul(d_pre, h2, "tn", [BF16], ident, name="mlp_up_wgrad", **wtiles)
    dx2, dx2_b, dg_mlp = _norm_bwd(dh2, x2, g_mlp, dx3, "norm_mlp_bwd")

    dmix, = _matmul(dx2_b, w_o, "nt", [F32], ident, tk=D, name="out_proj_bwd", **tiles)
    dw_o, = _matmul(mix, dx2_b, "tn", [BF16], ident, name="out_proj_wgrad", **wtiles)
    do_a, dd_a, do1, do2, do3, dd1, dd2, dd3, dg_out_a, dg_out_b, slots_o = _mix_bwd(
        dmix, o_a, o_b, l_b, g_out_a, g_out_b, dils, exchange=_scatter_rows([dw_o]))

    by_class = lambda d, dil: d.reshape(T // dil, dil, N_HEADS).transpose(1, 2, 0)
    slots_1 = [None] * 4
    dq_a, dk_a, dv_a, dsinks, slots_1[0] = _attn_bwd(proj_a, do_a, l_a, by_class(dd_a, 1), bias_a, sinks_a, lay_a,
                                                     "attn_a_bwd", exchange=_scatter_rows([dw_1_t], part=(0, 4)))
    dsinks = dsinks[:, 0].reshape(1, N_HEADS)
    dqs, dks, dvs = [], [], []
    for n, ((lay, bias, view, _, lse), do_n, dd_n) in enumerate(zip(branches, (do1, do2, do3), (dd1, dd2, dd3))):
        dq, dk, dv, slots_1[n + 1] = _attn_bwd(view, do_n, lse, by_class(dd_n, lay.dil), bias, None, lay,
                                               f"attn_b{lay.dil}_bwd",
                                               exchange=_scatter_rows([dw_1_t], part=(n + 1, 4)))
        dqs.append(dq)
        dks.append(dk)
        dvs.append(dv)
    dproj, db_in = _assemble([[dq_a], [dk_a], [dv_a], dqs, dks, dvs], "dproj", dils)

    n_in = w_in_t.shape[0]
    dw_in_t, = _matmul(dproj, h1, "tn", [BF16], ident, tm=n_in // 2, tn=1024, tk=512, name="in_proj_wgrad")
    dh1, slots_in = _matmul(dproj, w_in_t, "nn", [F32], ident, tk=n_in, name="in_proj_bwd",
                            exchange=_scatter_rows([dw_in_t]), **tiles)
    dx, _, dg_attn = _norm_bwd(dh1, xs, g_attn, dx2, "norm_attn_bwd")

    g_w_in = _sum_slots(slots_in, "sum_w_in_grads").T
    g_w_out = _sum_slots(slots_o, "sum_w_out_grads")
    g_w_1 = jnp.concatenate([_sum_slots(s, f"sum_w_1_grads_{n}") for n, s in enumerate(slots_1)]).T
    g_w_2 = _sum_slots(slots_2, "sum_w_2_grads")

    small_w = [g_attn, b_in, sinks_a, g_out_a, g_out_b, g_mlp, g_final]
    small_m = [m_g_attn, m_b_in, m_sinks_a, m_g_out_a, m_g_out_b, m_g_mlp, m_g_final]
    small_v = [v_g_attn, v_b_in, v_sinks_a, v_g_out_a, v_g_out_b, v_g_mlp, v_g_final]
    small_g = [dg_attn, db_in, dsinks, dg_out_a, dg_out_b, dg_mlp, dg_final]
    summed = _sum_over_devices(_pack_small(small_g + [loss_dev[:, :1]]))
    shapes = [w.shape for w in small_w]
    *g_small, loss = _unpack_small(summed, shapes + [()])

    big = [
        _adamw(w_in[0], g_w_in, m_w_in[0], v_w_in[0], "adamw_w_in"),
        _adamw(w_out[0], g_w_out, m_w_out[0], v_w_out[0], "adamw_w_out"),
        _adamw(w_1[0], g_w_1, m_w_1[0], v_w_1[0], "adamw_w_1"),
        _adamw(w_2[0], g_w_2, m_w_2[0], v_w_2[0], "adamw_w_2"),
    ]
    g_packed = _pack_small(g_small)
    small = _adamw(_pack_small(small_w), g_packed, _pack_small(small_m), _pack_small(small_v), "adamw_small")
    small = [_unpack_small(s, shapes) for s in small]

    def ordered(small_list, big_list):
        s = list(small_list)
        return [s[0], big_list[0][None], s[1], s[2], s[3], s[4], big_list[1][None], s[5],
                big_list[2][None], big_list[3][None], s[6]]

    grads = ordered(g_small, [g_w_in, g_w_out, g_w_1, g_w_2])
    deltas = ordered(small[0], [b[0] for b in big])
    new_m = ordered(small[1], [b[1] for b in big])
    new_v = ordered(small[2], [b[2] for b in big])
    return (loss, dx[None], *grads, *deltas, *new_m, *new_v)
```

```python
import numpy as np
import jax
import jax.numpy as jnp
from jax import lax
from jax.experimental import pallas as pl
from jax.experimental.pallas import tpu as pltpu

F32 = jnp.float32
BF16 = jnp.bfloat16

HEAD_DIM = 64
N_HEADS = 16
KV_HEADS_A = 2
BLOCK = 128
WINDOW_A = 128
DILATED_BRANCHES = ((128, 1), (512, 4), (2048, 16))
EPS = 1e-5
NEG_INF = -1e30
N_DEV = 8

ADAM_LR = 0.001
ADAM_B1 = 0.9
ADAM_B2 = 0.999
ADAM_EPS = 1e-08
ADAM_WD = 0.01
ADAM_STEP = 10

VMEM_LIMIT_BYTES = 56 * 1024 * 1024
MESH = pl.DeviceIdType.MESH
ANY = pl.BlockSpec(memory_space=pl.ANY)

NN = (((1,), (0,)), ((), ()))
NT = (((1,), (1,)), ((), ()))
TN = (((0,), (0,)), ((), ()))


def _dot(a, b, dims):
    return lax.dot_general(a, b, dims, preferred_element_type=F32)


def _params(*sem):
    return pltpu.CompilerParams(dimension_semantics=sem, vmem_limit_bytes=VMEM_LIMIT_BYTES)


RELAY_AT = 0.6


class _Exchange:
    def __init__(self, ins, out_shapes, n_remote, n_local, copies, aliases=None, relay=None):
        self.ins, self.out_shapes = list(ins), list(out_shapes)
        self.n_remote, self.n_local = n_remote, n_local
        self.copies = copies
        self.relay = relay
        self.aliases = aliases or {}

    def start(self, refs):
        local, sends, _ = self.copies(*refs)
        for cp in local + sends:
            cp.start()

    def middle(self, refs):
        arrived, onward = self.relay(*refs)
        for got, cp in zip(arrived, onward):
            got.wait_recv()
            cp.start()

    def finish(self, refs):
        local, sends, recvs = self.copies(*refs)
        for cp in recvs:
            cp.wait_recv()
        for cp in sends:
            cp.wait_send()
        for cp in local:
            cp.wait()
        if self.relay:
            for cp in self.relay(*refs)[1]:
                cp.wait_send()


class _Ride:
    def __init__(self, ex, n_in, n_out, n_scratch):
        self.ex = ex
        self.n = (n_in, n_out, n_scratch)
        self.args = ex.ins if ex else []
        self.in_specs = [ANY] * len(self.args)
        self.out_shapes = ex.out_shapes if ex else []
        self.out_specs = [ANY] * len(self.out_shapes)
        self.scratch = [pltpu.SemaphoreType.DMA((ex.n_remote,)), pltpu.SemaphoreType.DMA((ex.n_remote,)),
                        pltpu.SemaphoreType.DMA((max(ex.n_local, 1),))] if ex else []
        self.aliases = {n_in + i: n_out + o for i, o in ex.aliases.items()} if ex else {}

    def split(self, refs):
        n_in, n_out, n_scratch = self.n
        a = n_in
        b = a + len(self.args)
        c = b + n_out
        d = c + len(self.out_shapes)
        e = d + n_scratch
        return refs[:a], refs[b:c], refs[d:e], (refs[a:b], refs[c:d], *refs[e:])

    def around(self, step, n_steps, exrefs, compute):
        if self.ex is None:
            compute()
            return

        @pl.when(step == 0)
        def _():
            self.ex.start(exrefs)

        compute()

        if self.ex.relay:
            @pl.when(step == int(RELAY_AT * (n_steps - 1)))
            def _():
                self.ex.middle(exrefs)

        @pl.when(step == n_steps - 1)
        def _():
            self.ex.finish(exrefs)


def _matmul(a, b, dims, out_dtypes, epilogue, *, tm, tn, tk, name, tile_ins=(), row_ins=(), exchange=None):
    if dims == "tn":
        K, M = a.shape
    else:
        M, K = a.shape
    N = b.shape[0] if dims == "nt" else b.shape[1]
    tm, tn, tk = min(tm, M), min(tn, N), min(tk, K)
    assert M % tm == 0 and N % tn == 0 and K % tk == 0, (name, M, N, K, tm, tn, tk)
    grid = (M // tm, N // tn, K // tk)
    nk = grid[2]
    n_tile, n_row, n_out = len(tile_ins), len(row_ins), len(out_dtypes)
    dn = {"nn": NN, "nt": NT, "tn": TN}[dims]
    ride = _Ride(exchange, 2 + n_tile + n_row, n_out, 1 if nk > 1 else 0)

    def kern(*refs):
        ins, out_refs, scratch, exrefs = ride.split(refs)
        a_ref, b_ref = ins[:2]
        tile_refs = ins[2:2 + n_tile]
        row_refs = ins[2 + n_tile:]
        ids = [pl.program_id(d) for d in range(3)]

        def finish(acc):
            outs = epilogue(acc, *[r[...] for r in tile_refs], *[r[...] for r in row_refs])
            for o_ref, o in zip(out_refs, outs):
                o_ref[...] = o.astype(o_ref.dtype)

        def compute():
            if nk == 1:
                finish(_dot(a_ref[...], b_ref[...], dn))
                return
            acc_ref = scratch[0]

            @pl.when(ids[2] == 0)
            def _():
                acc_ref[...] = jnp.zeros_like(acc_ref)

            acc_ref[...] += _dot(a_ref[...], b_ref[...], dn)

            @pl.when(ids[2] == nk - 1)
            def _():
                finish(acc_ref[...])

        ride.around((ids[0] * grid[1] + ids[1]) * grid[2] + ids[2], grid[0] * grid[1] * grid[2], exrefs, compute)

    if dims == "tn":
        a_spec = pl.BlockSpec((tk, tm), lambda i, j, k: (k, i))
    else:
        a_spec = pl.BlockSpec((tm, tk), lambda i, j, k: (i, k))
    if dims == "nt":
        b_spec = pl.BlockSpec((tn, tk), lambda i, j, k: (j, k))
    else:
        b_spec = pl.BlockSpec((tk, tn), lambda i, j, k: (k, j))
    tile_spec = pl.BlockSpec((tm, tn), lambda i, j, k: (i, j))
    row_spec = pl.BlockSpec((1, tn), lambda i, j, k: (0, j))
    sem = ("arbitrary",) * 3 if exchange else ("parallel", "parallel", "arbitrary")
    return pl.pallas_call(
        kern,
        name=name,
        grid=grid,
        in_specs=[a_spec, b_spec] + [tile_spec] * n_tile + [row_spec] * n_row + ride.in_specs,
        out_specs=[tile_spec] * n_out + ride.out_specs,
        out_shape=[jax.ShapeDtypeStruct((M, N), dt) for dt in out_dtypes] + ride.out_shapes,
        scratch_shapes=([pltpu.VMEM((tm, tn), F32)] if nk > 1 else []) + ride.scratch,
        input_output_aliases=ride.aliases,
        compiler_params=_params(*sem),
    )(a, b, *tile_ins, *row_ins, *ride.args)


PROJ_ROWS = 256


def _proj_views(a, w_t, bias, dils, name, exchange=None):
    T, K = a.shape
    N = w_t.shape[0]
    ride = _Ride(exchange, 3, len(dils), 1)

    def kern(*refs):
        (a_ref, w_ref, b_ref), outs, (scr,), exrefs = ride.split(refs)

        def compute():
            acc = _dot(a_ref[...], w_ref[...], NT) + b_ref[...]
            for out_ref, dil in zip(outs, dils):
                _to_class_order(acc, scr, out_ref, dil)

        ride.around(pl.program_id(0), T // PROJ_ROWS, exrefs, compute)

    return pl.pallas_call(
        kern, name=name, grid=(T // PROJ_ROWS,),
        in_specs=[pl.BlockSpec((PROJ_ROWS, K), lambda i: (i, 0)), pl.BlockSpec((N, K), lambda i: (0, 0)),
                  pl.BlockSpec((1, N), lambda i: (0, 0))] + ride.in_specs,
        out_specs=[_view_spec(PROJ_ROWS, N, d) for d in dils] + ride.out_specs,
        out_shape=[jax.ShapeDtypeStruct((T // d, d * N), BF16) for d in dils] + ride.out_shapes,
        scratch_shapes=[_regroup_scratch(PROJ_ROWS, N)] + ride.scratch,
        input_output_aliases=ride.aliases,
        compiler_params=_params("arbitrary"),
    )(a, w_t, bias, *ride.args)


ROWS = 256
MIX_ROWS = 128


def _rstd(xv):
    return lax.rsqrt(jnp.mean(xv * xv, axis=-1, keepdims=True) + EPS)


def _norm_fwd(x, g, name, exchange=None):
    T, D = x.shape
    ride = _Ride(exchange, 2, 1, 0)

    def kern(*refs):
        (x_ref, g_ref), (h_ref,), _, exrefs = ride.split(refs)

        def compute():
            xv = x_ref[...]
            h_ref[...] = ((xv * _rstd(xv)) * g_ref[...]).astype(h_ref.dtype)

        ride.around(pl.program_id(0), T // ROWS, exrefs, compute)

    row = pl.BlockSpec((ROWS, D), lambda i: (i, 0))
    return pl.pallas_call(
        kern, name=name, grid=(T // ROWS,),
        in_specs=[row, pl.BlockSpec((1, D), lambda i: (0, 0))] + ride.in_specs,
        out_specs=[row] + ride.out_specs,
        out_shape=[jax.ShapeDtypeStruct((T, D), BF16)] + ride.out_shapes,
        scratch_shapes=ride.scratch, input_output_aliases=ride.aliases,
        compiler_params=_params("arbitrary"),
    )(x, g, *ride.args)


def _norm_bwd(dh, x, g, res, name, exchange=None):
    T, D = x.shape
    ride = _Ride(exchange, 4, 3, 0)

    def kern(*refs):
        (dh_ref, x_ref, g_ref, res_ref), (dx_ref, dxb_ref, dg_ref), _, exrefs = ride.split(refs)

        def compute():
            @pl.when(pl.program_id(0) == 0)
            def _():
                dg_ref[...] = jnp.zeros_like(dg_ref)

            xv = x_ref[...]
            r = _rstd(xv)
            xn = xv * r
            dhv = dh_ref[...]
            dg_ref[...] += jnp.sum(dhv * xn, axis=0, keepdims=True)
            t = dhv * g_ref[...]
            dx = res_ref[...] + r * (t - xn * jnp.mean(t * xn, axis=-1, keepdims=True))
            dx_ref[...] = dx
            dxb_ref[...] = dx.astype(BF16)

        ride.around(pl.program_id(0), T // ROWS, exrefs, compute)

    row = pl.BlockSpec((ROWS, D), lambda i: (i, 0))
    vec = pl.BlockSpec((1, D), lambda i: (0, 0))
    return pl.pallas_call(
        kern, name=name, grid=(T // ROWS,),
        in_specs=[row, row, vec, row] + ride.in_specs,
        out_specs=[row, row, vec] + ride.out_specs,
        out_shape=[jax.ShapeDtypeStruct((T, D), F32), jax.ShapeDtypeStruct((T, D), BF16),
                   jax.ShapeDtypeStruct((1, D), F32)] + ride.out_shapes,
        scratch_shapes=ride.scratch, input_output_aliases=ride.aliases,
        compiler_params=_params("arbitrary"),
    )(dh, x, g, res, *ride.args)


def _loss_head(x3, tgt, g):
    T, D = x3.shape

    def kern(x_ref, t_ref, g_ref, dx_ref, dxb_ref, dg_ref, loss_ref):
        @pl.when(pl.program_id(0) == 0)
        def _():
            dg_ref[...] = jnp.zeros_like(dg_ref)
            loss_ref[...] = jnp.zeros_like(loss_ref)

        xv = x_ref[...]
        gv = g_ref[...]
        r = _rstd(xv)
        xn = xv * r
        err = xn * gv - t_ref[...]
        per_tok = jnp.mean(err * err, axis=-1, keepdims=True)
        loss_ref[...] += 0.5 * jnp.sum(per_tok, axis=0, keepdims=True)
        dy = err * (1.0 / D)
        dg_ref[...] += jnp.sum(dy * xn, axis=0, keepdims=True)
        t = dy * gv
        dx = r * (t - xn * jnp.mean(t * xn, axis=-1, keepdims=True))
        dx_ref[...] = dx
        dxb_ref[...] = dx.astype(BF16)

    row = pl.BlockSpec((ROWS, D), lambda i: (i, 0))
    vec = pl.BlockSpec((1, D), lambda i: (0, 0))
    return pl.pallas_call(
        kern, name="loss_head", grid=(T // ROWS,),
        in_specs=[row, row, vec],
        out_specs=[row, row, vec, pl.BlockSpec((1, 128), lambda i: (0, 0))],
        out_shape=[jax.ShapeDtypeStruct((T, D), F32), jax.ShapeDtypeStruct((T, D), BF16),
                   jax.ShapeDtypeStruct((1, D), F32), jax.ShapeDtypeStruct((1, 128), F32)],
        compiler_params=_params("arbitrary"),
    )(x3, tgt, g)


def _spread_matrix():
    head_of_lane = np.arange(N_HEADS * HEAD_DIM) // HEAD_DIM
    return jnp.asarray(np.arange(N_HEADS)[:, None] == head_of_lane[None, :], dtype=BF16)


def _pieces(v, n):
    out = []
    for _ in range(n):
        piece = v.astype(BF16)
        out.append(piece)
        v = v - piece.astype(F32)
    return out


def _spread(v, spread):
    return sum(_dot(p, spread, NN) for p in _pieces(v, 2))


def _spread_weights(w1, w2, spread):
    s1, s2 = _spread(w1, spread), _spread(w2, spread)
    return s1, s2, 1.0 - s1 - s2


def _head_sums(v, spread):
    return sum(_dot(p, spread, NT) for p in _pieces(v, 2))


def _branch_weights(l1, l2, l3):
    lm = jnp.maximum(jnp.maximum(l1, l2), l3)
    e1, e2, e3 = jnp.exp(l1 - lm), jnp.exp(l2 - lm), jnp.exp(l3 - lm)
    inv = 1.0 / (e1 + e2 + e3)
    return e1 * inv, e2 * inv, e3 * inv


def _regroup_scratch(rows, width):
    return pltpu.VMEM((width // LANES, rows, LANES), F32)


def _to_token_order(view_ref, scr, dil):
    if dil == 1:
        return view_ref[...]
    n_l, w = view_ref.shape[0], view_ref.shape[1] // dil
    for r in range(dil):
        for cb in range(w // LANES):
            scr[cb, pl.ds(r, n_l, stride=dil), :] = view_ref[:, r * w + cb * LANES:r * w + (cb + 1) * LANES]
    return jnp.concatenate([scr[cb] for cb in range(w // LANES)], axis=1)


def _to_class_order(val, scr, view_ref, dil):
    if dil == 1:
        view_ref[...] = val.astype(view_ref.dtype)
        return
    n, w = val.shape
    for cb in range(w // LANES):
        scr[cb] = val[:, cb * LANES:(cb + 1) * LANES]
    for r in range(dil):
        for cb in range(w // LANES):
            view_ref[:, r * w + cb * LANES:r * w + (cb + 1) * LANES] = (
                scr[cb, pl.ds(r, n // dil, stride=dil), :].astype(view_ref.dtype))


def _view_spec(rows, width, dil):
    return pl.BlockSpec((rows // dil, dil * width), lambda i: (i, 0))


def _mix_fwd(oa, obs, lbs, ga, gb, dils):
    T, W = oa.shape

    def kern(oa_ref, o1, o2, o3, l1, l2, l3, ga_ref, gb_ref, sp_ref, mix_ref, *scr):
        sp = sp_ref[...]
        w1, w2, w3 = _branch_weights(l1[...], l2[...], l3[...])
        on = [_to_token_order(o, s, d) for o, s, d in zip((o1, o2, o3), scr, dils)]
        s1, s2, s3 = _spread_weights(w1, w2, sp)
        ob = s1 * on[0] + s2 * on[1] + s3 * on[2]
        oav = oa_ref[...]
        mix_ref[:, :W] = ((oav * _rstd(oav)) * ga_ref[...]).astype(BF16)
        mix_ref[:, W:] = ((ob * _rstd(ob)) * gb_ref[...]).astype(BF16)

    row = pl.BlockSpec((MIX_ROWS, W), lambda i: (i, 0))
    per_head = pl.BlockSpec((MIX_ROWS, N_HEADS), lambda i: (i, 0))
    vec = pl.BlockSpec((1, W), lambda i: (0, 0))
    return pl.pallas_call(
        kern, name="mix_fwd", grid=(T // MIX_ROWS,),
        in_specs=[row] + [_view_spec(MIX_ROWS, W, d) for d in dils] + [per_head] * 3
        + [vec, vec, pl.BlockSpec((N_HEADS, W), lambda i: (0, 0))],
        out_specs=pl.BlockSpec((MIX_ROWS, 2 * W), lambda i: (i, 0)),
        out_shape=jax.ShapeDtypeStruct((T, 2 * W), BF16),
        scratch_shapes=[_regroup_scratch(MIX_ROWS, W)] * 3,
        compiler_params=_params("parallel"),
    )(oa, *obs, *lbs, ga, gb, _spread_matrix())


def _mix_bwd(dmix, oa, obs, lbs, ga, gb, dils, exchange=None):
    T, W = oa.shape
    ride = _Ride(exchange, 11, 10, 3)

    def kern(*refs):
        ins, outs, scr, exrefs = ride.split(refs)
        ride.around(pl.program_id(0), T // MIX_ROWS, exrefs, lambda: compute(*ins, *outs, *scr))

    def compute(dm_ref, oa_ref, o1, o2, o3, l1, l2, l3, ga_ref, gb_ref, sp_ref,
                doa_ref, da_ref, do1, do2, do3, d1, d2, d3, dga_ref, dgb_ref, *scr):
        @pl.when(pl.program_id(0) == 0)
        def _():
            dga_ref[...] = jnp.zeros_like(dga_ref)
            dgb_ref[...] = jnp.zeros_like(dgb_ref)

        sp = sp_ref[...]
        oav = oa_ref[...]
        r = _rstd(oav)
        on = oav * r
        dy = dm_ref[:, :W]
        dga_ref[...] += jnp.sum(dy * on, axis=0, keepdims=True)
        t = dy * ga_ref[...]
        doa = r * (t - on * jnp.mean(t * on, axis=-1, keepdims=True))
        doa_ref[...] = doa.astype(BF16)
        da_ref[...] = _head_sums(doa * oav, sp)
        w1, w2, w3 = _branch_weights(l1[...], l2[...], l3[...])
        s1, s2, s3 = _spread_weights(w1, w2, sp)
        on = [_to_token_order(o, sc, d) for o, sc, d in zip((o1, o2, o3), scr, dils)]
        ob = s1 * on[0] + s2 * on[1] + s3 * on[2]
        r = _rstd(ob)
        on = ob * r
        dy = dm_ref[:, W:]
        dgb_ref[...] += jnp.sum(dy * on, axis=0, keepdims=True)
        t = dy * gb_ref[...]
        dob = r * (t - on * jnp.mean(t * on, axis=-1, keepdims=True))
        c = _head_sums(dob * ob, sp)
        for do_ref, sn, sc, d in zip((do1, do2, do3), (s1, s2, s3), scr, dils):
            _to_class_order(sn * dob, sc, do_ref, d)
        d1[...] = w1 * c
        d2[...] = w2 * c
        d3[...] = w3 * c

    row = pl.BlockSpec((MIX_ROWS, W), lambda i: (i, 0))
    per_head = pl.BlockSpec((MIX_ROWS, N_HEADS), lambda i: (i, 0))
    vec = pl.BlockSpec((1, W), lambda i: (0, 0))
    bf = jax.ShapeDtypeStruct((T, W), BF16)
    ph = jax.ShapeDtypeStruct((T, N_HEADS), F32)
    vv = jax.ShapeDtypeStruct((1, W), F32)
    views = [_view_spec(MIX_ROWS, W, d) for d in dils]
    return pl.pallas_call(
        kern, name="mix_bwd", grid=(T // MIX_ROWS,),
        in_specs=[pl.BlockSpec((MIX_ROWS, 2 * W), lambda i: (i, 0)), row] + views + [per_head] * 3 + [vec, vec,
                  pl.BlockSpec((N_HEADS, W), lambda i: (0, 0))] + ride.in_specs,
        out_specs=[row, per_head] + views + [per_head, per_head, per_head, vec, vec] + ride.out_specs,
        out_shape=[bf, ph] + [jax.ShapeDtypeStruct(o.shape, F32) for o in obs] + [ph, ph, ph, vv, vv]
        + ride.out_shapes,
        scratch_shapes=[_regroup_scratch(MIX_ROWS, W)] * 3 + ride.scratch,
        input_output_aliases=ride.aliases,
        compiler_params=_params("arbitrary"),
    )(dmix, oa, *obs, *lbs, ga, gb, _spread_matrix(), *ride.args)


def _alibi_slopes(n):
    return np.asarray(2.0 ** (-8.0 * (np.arange(n) + 1) / n)).astype(np.float32)


def _band_bias(max_steps, step_dist):
    qi = np.arange(BLOCK)[None, :]
    kj = np.arange(BLOCK)[:, None]
    slopes = _alibi_slopes(N_HEADS)
    halves = []
    for steps in (qi + BLOCK - kj, qi - kj):
        valid = (steps >= 0) & (steps <= max_steps)
        alibi = slopes[:, None, None] * (step_dist * steps).astype(np.float32)[None]
        halves.append(np.where(valid[None], -alibi, np.float32(NEG_INF)).astype(np.float32))
    per_head = np.concatenate(halves, axis=1)
    return jnp.asarray(np.concatenate([per_head[0::2], per_head[1::2]], axis=2))


class _AttnLayout:
    def __init__(self, dil, kv_heads, q_stride, q_off, k_stride, k_off, v_off):
        self.dil = dil
        self.kv_heads = kv_heads
        self.kw = kv_heads * HEAD_DIM
        self.rep = N_HEADS // kv_heads
        self.q_col = lambda r: r * q_stride + q_off
        self.k_col = lambda r: r * k_stride + k_off
        self.v_col = lambda r: r * k_stride + v_off


QW = N_HEADS * HEAD_DIM
LANES = 128


PAIRS = N_HEADS // 2


def _pair_cols(pair):
    return slice(pair * LANES, (pair + 1) * LANES)


def _first_head_lanes(shape):
    return lax.broadcasted_iota(jnp.int32, shape, 1) < HEAD_DIM


def _split_heads(pair):
    first = _first_head_lanes(pair.shape)
    zero = jnp.zeros_like(pair)
    return jnp.concatenate([jnp.where(first, pair, zero), jnp.where(first, zero, pair)], axis=0)


def _kv_pair(ref, pair, rep):
    if rep == 1:
        return ref[:, _pair_cols(pair)]
    blk = ref[...].astype(F32)
    other = pltpu.roll(blk, HEAD_DIM, 1)
    first = _first_head_lanes(blk.shape)
    both = jnp.where(first, blk, other) if (2 * pair // rep) % 2 == 0 else jnp.where(first, other, blk)
    return both.astype(ref.dtype)


def _paired_kv(prev_ref, cur_ref, rep, transposed=False):
    memo = {}

    def get(pair):
        key = pair if rep == 1 else 2 * pair // rep
        if key not in memo:
            blocks = [_kv_pair(ref, pair, rep) for ref in (prev_ref, cur_ref)]
            memo[key] = jnp.concatenate([b.T for b in blocks], axis=1) if transposed else jnp.concatenate(blocks, axis=0)
        return memo[key]

    return get


def _attn_fwd(proj, bias, sinks, lay, name, exchange=None):
    L = proj.shape[0]
    nb = L // BLOCK
    kw, rep = lay.kw, lay.rep
    use_sinks = sinks is not None
    scale = HEAD_DIM ** -0.5
    ride = _Ride(exchange, 7 if use_sinks else 6, 2, 2)

    def kern(*refs):
        ins, (o_ref, l_ref), (sc_ref, pr_ref), exrefs = ride.split(refs)
        q_ref, kc_ref, kp_ref, vc_ref, vp_ref, b_ref = ins[:6]
        s_ref = ins[6] if use_sinks else None
        r, i = pl.program_id(0), pl.program_id(1)
        first = i == 0
        ride.around(r * nb + i, lay.dil * nb, exrefs,
                    lambda: compute(q_ref, kc_ref, kp_ref, vc_ref, vp_ref, b_ref, s_ref, o_ref, l_ref, first,
                                    sc_ref, pr_ref))

    def compute(q_ref, kc_ref, kp_ref, vc_ref, vp_ref, b_ref, s_ref, o_ref, l_ref, first, sc_ref, pr_ref):
        keys, values_t = _paired_kv(kp_ref, kc_ref, rep), _paired_kv(vp_ref, vc_ref, rep, transposed=True)
        for pair in range(PAIRS):
            qs = _split_heads(q_ref[:, _pair_cols(pair)])
            s = _dot(keys(pair), qs, NT) * scale + b_ref[pair]
            sc_ref[pair, :BLOCK] = jnp.where(first, NEG_INF, s[:BLOCK])
            sc_ref[pair, BLOCK:] = s[BLOCK:]
        inv = []
        for h in range(N_HEADS):
            cols = slice(h % 2 * BLOCK, (h % 2 + 1) * BLOCK)
            s = sc_ref[h // 2, :, cols]
            m = jnp.max(s, axis=0, keepdims=True)
            if use_sinks:
                sink = s_ref[:, h:h + 1]
                m = jnp.maximum(m, sink)
            p = jnp.exp(s - m)
            denom = jnp.sum(p, axis=0, keepdims=True)
            if use_sinks:
                denom = denom + jnp.exp(sink - m)
            pr_ref[h // 2, :, cols] = p.astype(BF16)
            l_ref[h:h + 1, :] = m + jnp.log(denom)
            inv.append(1.0 / denom)
        for pair in range(PAIRS):
            both = _dot(values_t(pair), pr_ref[pair], NN)
            o_t = jnp.concatenate([both[:HEAD_DIM, :BLOCK] * inv[2 * pair], both[HEAD_DIM:, BLOCK:] * inv[2 * pair + 1]],
                                  axis=0)
            o_ref[:, _pair_cols(pair)] = o_t.T

    prev = lambda i: jnp.maximum(i - 1, 0)
    in_specs = [
        pl.BlockSpec((BLOCK, QW), lambda r, i: (i, lay.q_col(r))),
        pl.BlockSpec((BLOCK, kw), lambda r, i: (i, lay.k_col(r))),
        pl.BlockSpec((BLOCK, kw), lambda r, i: (prev(i), lay.k_col(r))),
        pl.BlockSpec((BLOCK, kw), lambda r, i: (i, lay.v_col(r))),
        pl.BlockSpec((BLOCK, kw), lambda r, i: (prev(i), lay.v_col(r))),
        pl.BlockSpec((PAIRS, 2 * BLOCK, 2 * BLOCK), lambda r, i: (0, 0, 0)),
    ]
    args = [proj, proj, proj, proj, proj, bias]
    if use_sinks:
        in_specs.append(pl.BlockSpec((1, N_HEADS), lambda r, i: (0, 0)))
        args.append(sinks)
    out_specs = [pl.BlockSpec((BLOCK, QW), lambda r, i: (i, r)),
                 pl.BlockSpec((None, N_HEADS, BLOCK), lambda r, i: (r, 0, i))]
    out_shape = [jax.ShapeDtypeStruct((L, lay.dil * QW), F32), jax.ShapeDtypeStruct((lay.dil, N_HEADS, L), F32)]
    return pl.pallas_call(
        kern, name=name, grid=(lay.dil, nb),
        in_specs=in_specs + ride.in_specs, out_specs=out_specs + ride.out_specs,
        out_shape=out_shape + ride.out_shapes,
        scratch_shapes=[pltpu.VMEM((PAIRS, 2 * BLOCK, 2 * BLOCK), dt) for dt in (F32, BF16)] + ride.scratch,
        input_output_aliases=ride.aliases,
        compiler_params=_params("arbitrary", "arbitrary"),
    )(*args, *ride.args)


def _attn_bwd(proj, do, lse, dd, bias, sinks, lay, name, exchange=None):
    L = proj.shape[0]
    nb = L // BLOCK
    kw, rep = lay.kw, lay.rep
    assert rep == 1 or lay.kv_heads == 2, "grouped queries: the two kv heads fill one 128-lane block"
    use_sinks = sinks is not None
    scale = HEAD_DIM ** -0.5
    ride = _Ride(exchange, 10 if use_sinks else 9, 4 if use_sinks else 3, 6)

    def kern(*refs):
        ins, outs, (ck_ref, cv_ref, *staged), exrefs = ride.split(refs)
        q_ref, kc_ref, kp_ref, vc_ref, vp_ref, do_ref, l_ref, d_ref, b_ref = ins[:9]
        s_ref = ins[9] if use_sinks else None
        dq_ref, dk_ref, dv_ref = outs[:3]
        ds_ref = outs[3] if use_sinks else None
        r = pl.program_id(0)
        i = pl.program_id(1)
        ride.around(r * (nb + 1) + i, lay.dil * (nb + 1), exrefs,
                    lambda: compute(q_ref, kc_ref, kp_ref, vc_ref, vp_ref, do_ref, l_ref, d_ref, b_ref, s_ref,
                                    dq_ref, dk_ref, dv_ref, ds_ref, ck_ref, cv_ref, r, i, *staged))

    def compute(q_ref, kc_ref, kp_ref, vc_ref, vp_ref, do_ref, l_ref, d_ref, b_ref, s_ref,
                dq_ref, dk_ref, dv_ref, ds_ref, ck_ref, cv_ref, r, i, sc_ref, dp_ref, pr_ref, dsc_ref):
        first = i == 0

        @pl.when(first)
        def _():
            ck_ref[...] = jnp.zeros_like(ck_ref)
            cv_ref[...] = jnp.zeros_like(cv_ref)

        if use_sinks:
            @pl.when(first & (r == 0))
            def _():
                ds_ref[...] = jnp.zeros_like(ds_ref)

        @pl.when(i < nb)
        def _():
            keys, values = _paired_kv(kp_ref, kc_ref, rep), _paired_kv(vp_ref, vc_ref, rep)
            keys_t = _paired_kv(kp_ref, kc_ref, rep, transposed=True)
            for pair in range(PAIRS):
                qs = _split_heads(q_ref[:, _pair_cols(pair)])
                dos = _split_heads(do_ref[:, _pair_cols(pair)].astype(BF16))
                s = _dot(keys(pair), qs, NT) * scale + b_ref[pair]
                sc_ref[pair, :BLOCK] = jnp.where(first, NEG_INF, s[:BLOCK])
                sc_ref[pair, BLOCK:] = s[BLOCK:]
                dp_ref[pair] = _dot(values(pair), dos, NT)
            for h in range(N_HEADS):
                cols = slice(h % 2 * BLOCK, (h % 2 + 1) * BLOCK)
                lrow = l_ref[h:h + 1, :]
                drow = d_ref[h:h + 1, :]
                p = jnp.exp(sc_ref[h // 2, :, cols] - lrow)
                pr_ref[h // 2, :, cols] = p.astype(BF16)
                dsc_ref[h // 2, :, cols] = (p * (dp_ref[h // 2, :, cols] - drow) * scale).astype(BF16)
                if use_sinks:
                    ds_ref[h:h + 1, :] += -(jnp.exp(s_ref[:, h:h + 1] - lrow) * drow)
            grouped = {}
            for pair in range(PAIRS):
                cols = _pair_cols(pair)
                qs = _split_heads(q_ref[:, cols])
                dos = _split_heads(do_ref[:, cols].astype(BF16))
                ds = dsc_ref[pair]
                both = _dot(keys_t(pair), ds, NN)
                dq_ref[:, cols] = jnp.concatenate([both[:HEAD_DIM, :BLOCK], both[HEAD_DIM:, BLOCK:]], axis=0).T
                dk = _dot(ds, qs, NN)
                dv = _dot(pr_ref[pair], dos, NN)
                if rep == 1:
                    dk_ref[:, cols] = ck_ref[:, cols] + dk[:BLOCK]
                    dv_ref[:, cols] = cv_ref[:, cols] + dv[:BLOCK]
                    ck_ref[:, cols] = dk[BLOCK:]
                    cv_ref[:, cols] = dv[BLOCK:]
                else:
                    g = 2 * pair // rep
                    grouped[g] = (dk, dv) if g not in grouped else (grouped[g][0] + dk, grouped[g][1] + dv)
            if rep > 1:
                fold = lambda t: t + pltpu.roll(t, HEAD_DIM, 1)
                first_half = _first_head_lanes((2 * BLOCK, LANES))
                dk = jnp.where(first_half, fold(grouped[0][0]), fold(grouped[1][0]))
                dv = jnp.where(first_half, fold(grouped[0][1]), fold(grouped[1][1]))
                dk_ref[...] = ck_ref[...] + dk[:BLOCK]
                dv_ref[...] = cv_ref[...] + dv[:BLOCK]
                ck_ref[...] = dk[BLOCK:]
                cv_ref[...] = dv[BLOCK:]

        @pl.when(i == nb)
        def _():
            dk_ref[...] = ck_ref[...]
            dv_ref[...] = cv_ref[...]
            if use_sinks:
                @pl.when(r == lay.dil - 1)
                def _():
                    ds_ref[...] = jnp.broadcast_to(jnp.sum(ds_ref[...], axis=1, keepdims=True), ds_ref.shape)

    cur = lambda i: jnp.minimum(i, nb - 1)
    prev = lambda i: jnp.maximum(jnp.minimum(i, nb - 1) - 1, 0)
    done = lambda i: jnp.maximum(i - 1, 0)
    qspec = lambda col: pl.BlockSpec((BLOCK, QW), lambda r, i: (cur(i), col(r)))
    per_head = pl.BlockSpec((None, N_HEADS, BLOCK), lambda r, i: (r, 0, cur(i)))
    in_specs = [
        qspec(lay.q_col),
        pl.BlockSpec((BLOCK, kw), lambda r, i: (cur(i), lay.k_col(r))),
        pl.BlockSpec((BLOCK, kw), lambda r, i: (prev(i), lay.k_col(r))),
        pl.BlockSpec((BLOCK, kw), lambda r, i: (cur(i), lay.v_col(r))),
        pl.BlockSpec((BLOCK, kw), lambda r, i: (prev(i), lay.v_col(r))),
        qspec(lambda r: r), per_head, per_head,
        pl.BlockSpec((PAIRS, 2 * BLOCK, 2 * BLOCK), lambda r, i: (0, 0, 0)),
    ]
    args = [proj, proj, proj, proj, proj, do, lse, dd, bias]
    out_specs = [
        qspec(lambda r: r),
        pl.BlockSpec((BLOCK, kw), lambda r, i: (done(i), r)),
        pl.BlockSpec((BLOCK, kw), lambda r, i: (done(i), r)),
    ]
    dkv_shape = jax.ShapeDtypeStruct((L, lay.dil * kw), F32)
    out_shape = [jax.ShapeDtypeStruct((L, lay.dil * QW), F32), dkv_shape, dkv_shape]
    if use_sinks:
        in_specs.append(pl.BlockSpec((1, N_HEADS), lambda r, i: (0, 0)))
        args.append(sinks)
        out_specs.append(pl.BlockSpec((N_HEADS, LANES), lambda r, i: (0, 0)))
        out_shape.append(jax.ShapeDtypeStruct((N_HEADS, LANES), F32))
    return pl.pallas_call(
        kern, name=name, grid=(lay.dil, nb + 1),
        in_specs=in_specs + ride.in_specs, out_specs=out_specs + ride.out_specs,
        out_shape=out_shape + ride.out_shapes,
        scratch_shapes=[pltpu.VMEM((BLOCK, kw), F32), pltpu.VMEM((BLOCK, kw), F32)]
        + [pltpu.VMEM((PAIRS, 2 * BLOCK, 2 * BLOCK), dt) for dt in (F32, F32, BF16, BF16)] + ride.scratch,
        input_output_aliases=ride.aliases,
        compiler_params=_params("arbitrary", "arbitrary"),
    )(*args, *ride.args)


def _assemble(groups, name, dils=(1,)):
    T = groups[0][0].shape[0] * dils[0]
    widths = [g[0].shape[1] // dils[0] for g in groups]
    total = sum(widths)
    flat = [a for g in groups for a in g]
    member_dils = [d for g in groups for d in dils[:len(g)]]

    def kern(*refs):
        ins = refs[:len(flat)]
        out_ref, cs_ref = refs[len(flat):len(flat) + 2]
        scr = refs[len(flat) + 2:]

        @pl.when(pl.program_id(0) == 0)
        def _():
            cs_ref[...] = jnp.zeros_like(cs_ref)

        pos = off = 0
        for g, w in zip(groups, widths):
            acc = _to_token_order(ins[pos], None, dils[0])
            for j in range(1, len(g)):
                acc = acc + _to_token_order(ins[pos + j], scr[j - 1], dils[j])
            pos += len(g)
            out_ref[:, off:off + w] = acc.astype(BF16)
            cs_ref[:, off:off + w] += jnp.sum(acc, axis=0, keepdims=True)
            off += w

    return pl.pallas_call(
        kern, name=name, grid=(T // ROWS,),
        in_specs=[_view_spec(ROWS, a.shape[1] // d, d) for a, d in zip(flat, member_dils)],
        out_specs=[pl.BlockSpec((ROWS, total), lambda i: (i, 0)), pl.BlockSpec((1, total), lambda i: (0, 0))],
        out_shape=[jax.ShapeDtypeStruct((T, total), BF16), jax.ShapeDtypeStruct((1, total), F32)],
        scratch_shapes=[_regroup_scratch(ROWS, max(widths))] * (len(dils) - 1),
        compiler_params=_params("arbitrary"),
    )(*flat)


def _adamw(w, g, m, v, name):
    R, C = w.shape
    rows = min(R, ROWS)
    assert R % rows == 0

    def kern(w_ref, g_ref, m_ref, v_ref, d_ref, nm_ref, nv_ref):
        gv = g_ref[...]
        mn = ADAM_B1 * m_ref[...] + (1.0 - ADAM_B1) * gv
        vn = ADAM_B2 * v_ref[...] + (1.0 - ADAM_B2) * jnp.square(gv)
        m_hat = mn / (1.0 - ADAM_B1 ** ADAM_STEP)
        v_hat = vn / (1.0 - ADAM_B2 ** ADAM_STEP)
        d_ref[...] = -ADAM_LR * (m_hat / (jnp.sqrt(v_hat) + ADAM_EPS) + ADAM_WD * w_ref[...])
        nm_ref[...] = mn
        nv_ref[...] = vn

    blk = pl.BlockSpec((rows, C), lambda i: (i, 0))
    shp = jax.ShapeDtypeStruct((R, C), F32)
    return pl.pallas_call(
        kern, name=name, grid=(R // rows,),
        in_specs=[blk] * 4, out_specs=[blk] * 3, out_shape=[shp] * 3,
        compiler_params=_params("parallel"),
    )(w, g, m, v)


def _sum_slots(slots, name):
    n, R, C = slots.shape
    SUM_ROWS = next(rows for rows in (128, 64, 32, 16) if R % rows == 0)

    def kern(s_ref, o_ref):
        acc = s_ref[0].astype(F32)
        for k in range(1, n):
            acc = acc + s_ref[k].astype(F32)
        o_ref[...] = acc

    return pl.pallas_call(
        kern, name=name, grid=(R // SUM_ROWS,),
        in_specs=[pl.BlockSpec((n, SUM_ROWS, C), lambda i: (0, i, 0))],
        out_specs=pl.BlockSpec((SUM_ROWS, C), lambda i: (i, 0)),
        out_shape=jax.ShapeDtypeStruct((R, C), F32),
        compiler_params=_params("parallel"),
    )(slots)


def _place():
    return lax.axis_index("x"), lax.axis_index("y"), lax.axis_index("c")


def _index(p):
    return 4 * p[0] + 2 * p[1] + p[2]


FLIPS = [(fx, fy, fc) for fx in (0, 1) for fy in (0, 1) for fc in (0, 1)][1:]


def _peer(me, flip):
    return tuple(1 - a if f else a for a, f in zip(me, flip))


def _gather_rows(shards, part=(0, 1), into=None):
    nw = len(shards)

    def plan(ins, outs, send_sems, recv_sems):
        x, y, c = me = _place()
        sibling = (x, y, 1 - c)
        chips = [(1 - x, y), (x, 1 - y), (1 - x, 1 - y)]

        def span(w):
            cnt = ins[w].shape[0] // part[1]
            return part[0] * cnt, cnt

        def rows(w, p):
            lo, cnt = span(w)
            return outs[w].at[pl.ds(_index(p) * ins[w].shape[0] + lo, cnt), :]

        def own(w):
            lo, cnt = span(w)
            return ins[w].at[pl.ds(lo, cnt), :]

        def copy(w, k, block, to):
            return pltpu.make_async_remote_copy(
                src_ref=own(w) if block is me else rows(w, block), dst_ref=rows(w, block),
                send_sem=send_sems.at[7 * w + k], recv_sem=recv_sems.at[7 * w + k],
                device_id=to, device_id_type=MESH)

        return me, sibling, chips, c, rows, own, copy

    def copies(ins, outs, send_sems, recv_sems, local_sems):
        me, sibling, chips, c, rows, own, copy = plan(ins, outs, send_sems, recv_sems)
        local = [pltpu.make_async_copy(own(w), rows(w, me), local_sems.at[w]) for w in range(nw)]
        sends, recvs = [], []
        for w in range(nw):
            sends.append(copy(w, 0, me, sibling))
            sends += [copy(w, 1 + j, me, (*chip, c)) for j, chip in enumerate(chips)]
            recvs.append(copy(w, 0, sibling, me))
            recvs += [copy(w, 4 + j, (*chip, 1 - c), me) for j, chip in enumerate(chips)]
        return local, sends, recvs

    def relay(ins, outs, send_sems, recv_sems, local_sems):
        me, sibling, chips, c, rows, own, copy = plan(ins, outs, send_sems, recv_sems)
        arrived = [copy(w, 1 + j, (*chip, c), me) for w in range(nw) for j, chip in enumerate(chips)]
        onward = [copy(w, 4 + j, (*chip, c), sibling) for w in range(nw) for j, chip in enumerate(chips)]
        return arrived, onward

    shapes = [jax.ShapeDtypeStruct((N_DEV * s.shape[0], s.shape[1]), s.dtype) for s in shards]
    aliases = {nw + w: w for w in range(nw)} if into else None
    return _Exchange(shards + (into or []), shapes, 7 * nw, nw, copies, aliases=aliases, relay=relay)


def _scatter_rows(parts, part=(0, 1)):
    nw = len(parts)

    def copies(ins, outs, send_sems, recv_sems, local_sems):
        me = _place()

        def src(w, owner):
            n = ins[w].shape[0] // N_DEV
            cnt = n // part[1]
            return ins[w].at[pl.ds(_index(owner) * n + part[0] * cnt, cnt), :]

        def copy(k, w, owner, sender, to):
            return pltpu.make_async_remote_copy(
                src_ref=src(w, owner), dst_ref=outs[w].at[_index(sender)],
                send_sem=send_sems.at[nw * k + w], recv_sem=recv_sems.at[nw * k + w],
                device_id=to, device_id_type=MESH)

        local = [pltpu.make_async_copy(src(w, me), outs[w].at[_index(me)], local_sems.at[w]) for w in range(nw)]
        peers = [_peer(me, flip) for flip in FLIPS]
        sends = [copy(k, w, peer, me, peer) for k, peer in enumerate(peers) for w in range(nw)]
        recvs = [copy(k, w, me, peer, me) for k, peer in enumerate(peers) for w in range(nw)]
        return local, sends, recvs

    shapes = [jax.ShapeDtypeStruct((N_DEV, p.shape[0] // N_DEV // part[1], p.shape[1]), p.dtype) for p in parts]
    return _Exchange(parts, shapes, 7 * nw, nw, copies)


def _sum_over_devices(v):
    shape = v.shape

    def body(v_ref, sum_ref, all_ref, send_sems, recv_sems):
        me = _place()
        all_ref[_index(me)] = v_ref[...]
        sends = []
        for k, flip in enumerate(FLIPS):
            peer = _peer(me, flip)
            sends.append(pltpu.make_async_remote_copy(
                src_ref=v_ref, dst_ref=all_ref.at[_index(me)],
                send_sem=send_sems.at[k], recv_sem=recv_sems.at[k], device_id=peer, device_id_type=MESH))
            sends[-1].start()
        for k, flip in enumerate(FLIPS):
            peer = _peer(me, flip)
            pltpu.make_async_remote_copy(
                src_ref=v_ref, dst_ref=all_ref.at[_index(peer)],
                send_sem=send_sems.at[k], recv_sem=recv_sems.at[k], device_id=peer, device_id_type=MESH).wait_recv()
        for cp in sends:
            cp.wait_send()
        acc = all_ref[0]
        for s in range(1, N_DEV):
            acc = acc + all_ref[s]
        sum_ref[...] = acc

    vmem = pl.BlockSpec(memory_space=pltpu.VMEM)
    return pl.pallas_call(
        body, name="sum_small_grads",
        in_specs=[vmem], out_specs=[vmem, vmem],
        out_shape=[jax.ShapeDtypeStruct(shape, F32), jax.ShapeDtypeStruct((N_DEV,) + shape, F32)],
        scratch_shapes=[pltpu.SemaphoreType.DMA((7,)), pltpu.SemaphoreType.DMA((7,))],
    )(v)[0]


SMALL_ROWS = 8


def _pack_small(vectors):
    padded = []
    for vec in vectors:
        vec = vec.reshape(-1)
        padded.append(jnp.pad(vec, (0, -vec.shape[0] % 128)))
    flat = jnp.concatenate(padded)
    flat = jnp.pad(flat, (0, -flat.shape[0] % (SMALL_ROWS * 128)))
    return flat.reshape(SMALL_ROWS, -1)


def _unpack_small(packed, shapes):
    flat = packed.reshape(-1)
    out, off = [], 0
    for shp in shapes:
        n = int(np.prod(shp))
        out.append(flat[off:off + n].reshape(shp))
        off += n + (-n % 128)
    return out


def kernel(x, g_attn, w_in, b_in, sinks_a, g_out_a, g_out_b, w_out, g_mlp, w_1, w_2, g_final, loss_target, m_g_attn, m_w_in, m_b_in, m_sinks_a, m_g_out_a, m_g_out_b, m_w_out, m_g_mlp, m_w_1, m_w_2, m_g_final, v_g_attn, v_w_in, v_b_in, v_sinks_a, v_g_out_a, v_g_out_b, v_w_out, v_g_mlp, v_w_1, v_w_2, v_g_final):
    xs, tgt = x[0], loss_target[0]
    T, D = xs.shape
    n_a = QW + 2 * KV_HEADS_A * HEAD_DIM
    g_fin = g_final.reshape(1, D)

    shards = [w_in[0].T.astype(BF16), w_out[0].astype(BF16), w_1[0].T.astype(BF16), w_2[0].astype(BF16)]
    ident = lambda acc: (acc,)
    add = lambda acc, other: (acc + other,)
    tiles = dict(tm=512, tn=1024)

    h1, w_in_t = _norm_fwd(xs, g_attn, "norm_attn", exchange=_gather_rows(shards[:1]))
    w_in_ta, w_in_tb = w_in_t[:n_a], w_in_t[n_a:]
    proj_a, = _matmul(h1, w_in_ta, "nt", [BF16], add, tm=512, tn=n_a, tk=D, row_ins=[b_in[:, :n_a]], name="proj_a")
    dils = [dil for _, dil in DILATED_BRANCHES]
    *proj_b, w_o = _proj_views(h1, w_in_tb, b_in[:, n_a:], dils, "proj_b", exchange=_gather_rows(shards[1:2]))

    lay_a = _AttnLayout(1, KV_HEADS_A, 0, 0, 0, QW // (KV_HEADS_A * HEAD_DIM), QW // (KV_HEADS_A * HEAD_DIM) + 1)
    bias_a = _band_bias(WINDOW_A - 1, 1)
    o_a, l_a, w_1_t = _attn_fwd(proj_a, bias_a, sinks_a, lay_a, "attn_a_fwd",
                                exchange=_gather_rows(shards[2:3], part=(0, 4)))
    branches = []
    for n, (window, dil) in enumerate(DILATED_BRANCHES):
        lay = _AttnLayout(dil, N_HEADS, 3, 0, 3, 1, 2)
        bias = _band_bias(window // dil, dil)
        ride = _gather_rows(shards[2:3], part=(n + 1, 4), into=[w_1_t])
        o, lse, w_1_t = _attn_fwd(proj_b[n], bias, None, lay, f"attn_b{dil}_fwd", exchange=ride)
        branches.append((lay, bias, proj_b[n], o, lse))
    o_b = [br[3] for br in branches]
    l_b = [br[4].transpose(2, 0, 1).reshape(T, N_HEADS) for br in branches]

    mix = _mix_fwd(o_a, o_b, l_b, g_out_a, g_out_b, dils)
    wide = dict(tm=512, tn=2048)

    def residual_and_norm(acc, res, g):
        x_new = acc + res
        return x_new, (x_new * _rstd(x_new)) * g

    assert wide["tn"] == D
    x2, h2 = _matmul(mix, w_o, "nn", [F32, BF16], residual_and_norm, tk=D, tile_ins=[xs], row_ins=[g_mlp],
                     name="out_proj", **wide)

    def relu_sq(acc):
        u = jnp.maximum(acc, 0.0)
        return u, u * u

    u, u_sq, w_2_f = _matmul(h2, w_1_t, "nt", [BF16, BF16], relu_sq, tk=D, name="mlp_up",
                             exchange=_gather_rows(shards[3:]), **wide)
    x3, = _matmul(u_sq, w_2_f, "nn", [F32], add, tk=4096, tile_ins=[x2], name="mlp_down", **tiles)

    dx3, dx3_b, dg_final, loss_dev = _loss_head(x3, tgt, g_fin)

    d_pre, = _matmul(dx3_b, w_2_f, "nt", [BF16], lambda acc, uu: (acc * (2.0 * uu.astype(F32)),),
                     tk=D, tile_ins=[u], name="mlp_down_bwd", **wide)
    wtiles = dict(tm=1024, tn=1024, tk=4096)
    dw_2, = _matmul(u_sq, dx3_b, "tn", [BF16], ident, name="mlp_down_wgrad", **wtiles)
    dh2, slots_2a = _matmul(d_pre, w_1_t, "nn", [F32], ident, tk=4096, name="mlp_up_bwd",
                            exchange=_scatter_rows([dw_2], part=(0, 2)), **tiles)
    dw_1_t, slots_2b = _matmul(d_pre, h2, "tn", [BF16], ident, name="mlp_up_wgrad",
                               exchange=_scatter_rows([dw_2], part=(1, 2)), **wtiles)
    dx2, dx2_b, dg_mlp = _norm_bwd(dh2, x2, g_mlp, dx3, "norm_mlp_bwd")

    dmix, = _matmul(dx2_b, w_o, "nt", [F32], ident, tk=D, name="out_proj_bwd", **wide)
    dw_o, = _matmul(mix, dx2_b, "tn", [BF16], ident, name="out_proj_wgrad", **wtiles)
    do_a, dd_a, do1, do2, do3, dd1, dd2, dd3, dg_out_a, dg_out_b, slots_o = _mix_bwd(
        dmix, o_a, o_b, l_b, g_out_a, g_out_b, dils, exchange=_scatter_rows([dw_o]))

    by_class = lambda d, dil: d.reshape(T // dil, dil, N_HEADS).transpose(1, 2, 0)
    slots_1 = [None] * 4
    dq_a, dk_a, dv_a, dsinks, slots_1[0] = _attn_bwd(proj_a, do_a, l_a, by_class(dd_a, 1), bias_a, sinks_a, lay_a,
                                                     "attn_a_bwd", exchange=_scatter_rows([dw_1_t], part=(0, 4)))
    dsinks = dsinks[:, 0].reshape(1, N_HEADS)
    dqs, dks, dvs = [], [], []
    for n, ((lay, bias, view, _, lse), do_n, dd_n) in enumerate(zip(branches, (do1, do2, do3), (dd1, dd2, dd3))):
        dq, dk, dv, slots_1[n + 1] = _attn_bwd(view, do_n, lse, by_class(dd_n, lay.dil), bias, None, lay,
                                               f"attn_b{lay.dil}_bwd",
                                               exchange=_scatter_rows([dw_1_t], part=(n + 1, 4)))
        dqs.append(dq)
        dks.append(dk)
        dvs.append(dv)
    dproj, db_in = _assemble([[dq_a], [dk_a], [dv_a], dqs, dks, dvs], "dproj", dils)

    n_in = w_in_t.shape[0]
    dw_in_t, = _matmul(dproj, h1, "tn", [BF16], ident, tm=n_in // 2, tn=1024, tk=1024, name="in_proj_wgrad")
    dh1, slots_in = _matmul(dproj, w_in_t, "nn", [F32], ident, tk=n_in, name="in_proj_bwd",
                            exchange=_scatter_rows([dw_in_t]), **tiles)
    dx, _, dg_attn = _norm_bwd(dh1, xs, g_attn, dx2, "norm_attn_bwd")

    g_w_in = _sum_slots(slots_in, "sum_w_in_grads").T
    g_w_out = _sum_slots(slots_o, "sum_w_out_grads")
    g_w_1 = jnp.concatenate([_sum_slots(s, f"sum_w_1_grads_{n}") for n, s in enumerate(slots_1)]).T
    g_w_2 = jnp.concatenate([_sum_slots(slots_2a, "sum_w_2_grads_0"), _sum_slots(slots_2b, "sum_w_2_grads_1")])

    small_w = [g_attn, b_in, sinks_a, g_out_a, g_out_b, g_mlp, g_final]
    small_m = [m_g_attn, m_b_in, m_sinks_a, m_g_out_a, m_g_out_b, m_g_mlp, m_g_final]
    small_v = [v_g_attn, v_b_in, v_sinks_a, v_g_out_a, v_g_out_b, v_g_mlp, v_g_final]
    small_g = [dg_attn, db_in, dsinks, dg_out_a, dg_out_b, dg_mlp, dg_final]
    summed = _sum_over_devices(_pack_small(small_g + [loss_dev[:, :1]]))
    shapes = [w.shape for w in small_w]
    *g_small, loss = _unpack_small(summed, shapes + [()])

    big = [
        _adamw(w_in[0], g_w_in, m_w_in[0], v_w_in[0], "adamw_w_in"),
        _adamw(w_out[0], g_w_out, m_w_out[0], v_w_out[0], "adamw_w_out"),
        _adamw(w_1[0], g_w_1, m_w_1[0], v_w_1[0], "adamw_w_1"),
        _adamw(w_2[0], g_w_2, m_w_2[0], v_w_2[0], "adamw_w_2"),
    ]
    g_packed = _pack_small(g_small)
    small = _adamw(_pack_small(small_w), g_packed, _pack_small(small_m), _pack_small(small_v), "adamw_small")
    small = [_unpack_small(s, shapes) for s in small]

    def ordered(small_list, big_list):
        s = list(small_list)
        return [s[0], big_list[0][None], s[1], s[2], s[3], s[4], big_list[1][None], s[5],
                big_list[2][None], big_list[3][None], s[6]]

    grads = ordered(g_small, [g_w_in, g_w_out, g_w_1, g_w_2])
    deltas = ordered(small[0], [b[0] for b in big])
    new_m = ordered(small[1], [b[1] for b in big])
    new_v = ordered(small[2], [b[2] for b in big])
    return (loss, dx[None], *grads, *deltas, *new_m, *new_v)
```

```python
import numpy as np
import jax
import jax.numpy as jnp
from jax import lax
from jax.experimental import pallas as pl
from jax.experimental.pallas import tpu as pltpu

F32 = jnp.float32
BF16 = jnp.bfloat16

HEAD_DIM = 64
N_HEADS = 16
KV_HEADS_A = 2
BLOCK = 128
WINDOW_A = 128
DILATED_BRANCHES = ((128, 1), (512, 4), (2048, 16))
EPS = 1e-5
NEG_INF = -1e30
N_DEV = 8

ADAM_LR = 0.001
ADAM_B1 = 0.9
ADAM_B2 = 0.999
ADAM_EPS = 1e-08
ADAM_WD = 0.01
ADAM_STEP = 10

VMEM_LIMIT_BYTES = 56 * 1024 * 1024
MESH = pl.DeviceIdType.MESH
ANY = pl.BlockSpec(memory_space=pl.ANY)

NN = (((1,), (0,)), ((), ()))
NT = (((1,), (1,)), ((), ()))
TN = (((0,), (0,)), ((), ()))


def _dot(a, b, dims):
    return lax.dot_general(a, b, dims, preferred_element_type=F32)


def _params(*sem):
    return pltpu.CompilerParams(dimension_semantics=sem, vmem_limit_bytes=VMEM_LIMIT_BYTES)


RELAY_AT = 0.6


class _Exchange:
    def __init__(self, ins, out_shapes, n_remote, n_local, copies, aliases=None, relay=None):
        self.ins, self.out_shapes = list(ins), list(out_shapes)
        self.n_remote, self.n_local = n_remote, n_local
        self.copies = copies
        self.relay = relay
        self.aliases = aliases or {}

    def start(self, refs):
        local, sends, _ = self.copies(*refs)
        for cp in local + sends:
            cp.start()

    def middle(self, refs):
        arrived, onward = self.relay(*refs)
        for got, cp in zip(arrived, onward):
            got.wait_recv()
            cp.start()

    def finish(self, refs):
        local, sends, recvs = self.copies(*refs)
        for cp in recvs:
            cp.wait_recv()
        for cp in sends:
            cp.wait_send()
        for cp in local:
            cp.wait()
        if self.relay:
            for cp in self.relay(*refs)[1]:
                cp.wait_send()


class _Ride:
    def __init__(self, ex, n_in, n_out, n_scratch):
        self.ex = ex
        self.n = (n_in, n_out, n_scratch)
        self.args = ex.ins if ex else []
        self.in_specs = [ANY] * len(self.args)
        self.out_shapes = ex.out_shapes if ex else []
        self.out_specs = [ANY] * len(self.out_shapes)
        self.scratch = [pltpu.SemaphoreType.DMA((ex.n_remote,)), pltpu.SemaphoreType.DMA((ex.n_remote,)),
                        pltpu.SemaphoreType.DMA((max(ex.n_local, 1),))] if ex else []
        self.aliases = {n_in + i: n_out + o for i, o in ex.aliases.items()} if ex else {}

    def split(self, refs):
        n_in, n_out, n_scratch = self.n
        a = n_in
        b = a + len(self.args)
        c = b + n_out
        d = c + len(self.out_shapes)
        e = d + n_scratch
        return refs[:a], refs[b:c], refs[d:e], (refs[a:b], refs[c:d], *refs[e:])

    def around(self, step, n_steps, exrefs, compute):
        if self.ex is None:
            compute()
            return

        @pl.when(step == 0)
        def _():
            self.ex.start(exrefs)

        compute()

        if self.ex.relay:
            @pl.when(step == int(RELAY_AT * (n_steps - 1)))
            def _():
                self.ex.middle(exrefs)

        @pl.when(step == n_steps - 1)
        def _():
            self.ex.finish(exrefs)


def _matmul(a, b, dims, out_dtypes, epilogue, *, tm, tn, tk, name, tile_ins=(), row_ins=(), exchange=None):
    if dims == "tn":
        K, M = a.shape
    else:
        M, K = a.shape
    N = b.shape[0] if dims == "nt" else b.shape[1]
    tm, tn, tk = min(tm, M), min(tn, N), min(tk, K)
    assert M % tm == 0 and N % tn == 0 and K % tk == 0, (name, M, N, K, tm, tn, tk)
    grid = (M // tm, N // tn, K // tk)
    nk = grid[2]
    n_tile, n_row, n_out = len(tile_ins), len(row_ins), len(out_dtypes)
    dn = {"nn": NN, "nt": NT, "tn": TN}[dims]
    ride = _Ride(exchange, 2 + n_tile + n_row, n_out, 1 if nk > 1 else 0)

    def kern(*refs):
        ins, out_refs, scratch, exrefs = ride.split(refs)
        a_ref, b_ref = ins[:2]
        tile_refs = ins[2:2 + n_tile]
        row_refs = ins[2 + n_tile:]
        ids = [pl.program_id(d) for d in range(3)]

        def finish(acc):
            outs = epilogue(acc, *[r[...] for r in tile_refs], *[r[...] for r in row_refs])
            for o_ref, o in zip(out_refs, outs):
                o_ref[...] = o.astype(o_ref.dtype)

        def compute():
            if nk == 1:
                finish(_dot(a_ref[...], b_ref[...], dn))
                return
            acc_ref = scratch[0]

            @pl.when(ids[2] == 0)
            def _():
                acc_ref[...] = jnp.zeros_like(acc_ref)

            acc_ref[...] += _dot(a_ref[...], b_ref[...], dn)

            @pl.when(ids[2] == nk - 1)
            def _():
                finish(acc_ref[...])

        ride.around((ids[0] * grid[1] + ids[1]) * grid[2] + ids[2], grid[0] * grid[1] * grid[2], exrefs, compute)

    if dims == "tn":
        a_spec = pl.BlockSpec((tk, tm), lambda i, j, k: (k, i))
    else:
        a_spec = pl.BlockSpec((tm, tk), lambda i, j, k: (i, k))
    if dims == "nt":
        b_spec = pl.BlockSpec((tn, tk), lambda i, j, k: (j, k))
    else:
        b_spec = pl.BlockSpec((tk, tn), lambda i, j, k: (k, j))
    tile_spec = pl.BlockSpec((tm, tn), lambda i, j, k: (i, j))
    row_spec = pl.BlockSpec((1, tn), lambda i, j, k: (0, j))
    sem = ("arbitrary",) * 3 if exchange else ("parallel", "parallel", "arbitrary")
    return pl.pallas_call(
        kern,
        name=name,
        grid=grid,
        in_specs=[a_spec, b_spec] + [tile_spec] * n_tile + [row_spec] * n_row + ride.in_specs,
        out_specs=[tile_spec] * n_out + ride.out_specs,
        out_shape=[jax.ShapeDtypeStruct((M, N), dt) for dt in out_dtypes] + ride.out_shapes,
        scratch_shapes=([pltpu.VMEM((tm, tn), F32)] if nk > 1 else []) + ride.scratch,
        input_output_aliases=ride.aliases,
        compiler_params=_params(*sem),
    )(a, b, *tile_ins, *row_ins, *ride.args)


PROJ_ROWS = 256


def _proj_views(a, w_t, bias, dils, name, exchange=None):
    T, K = a.shape
    N = w_t.shape[0]
    ride = _Ride(exchange, 3, len(dils), 1)

    def kern(*refs):
        (a_ref, w_ref, b_ref), outs, (scr,), exrefs = ride.split(refs)

        def compute():
            acc = _dot(a_ref[...], w_ref[...], NT) + b_ref[...]
            for out_ref, dil in zip(outs, dils):
                _to_class_order(acc, scr, out_ref, dil)

        ride.around(pl.program_id(0), T // PROJ_ROWS, exrefs, compute)

    return pl.pallas_call(
        kern, name=name, grid=(T // PROJ_ROWS,),
        in_specs=[pl.BlockSpec((PROJ_ROWS, K), lambda i: (i, 0)), pl.BlockSpec((N, K), lambda i: (0, 0)),
                  pl.BlockSpec((1, N), lambda i: (0, 0))] + ride.in_specs,
        out_specs=[_view_spec(PROJ_ROWS, N, d) for d in dils] + ride.out_specs,
        out_shape=[jax.ShapeDtypeStruct((T // d, d * N), BF16) for d in dils] + ride.out_shapes,
        scratch_shapes=[_regroup_scratch(PROJ_ROWS, N)] + ride.scratch,
        input_output_aliases=ride.aliases,
        compiler_params=_params("arbitrary"),
    )(a, w_t, bias, *ride.args)


ROWS = 256
MIX_ROWS = 128


def _rstd(xv):
    return lax.rsqrt(jnp.mean(xv * xv, axis=-1, keepdims=True) + EPS)


def _norm_fwd(x, g, name, exchange=None):
    T, D = x.shape
    ride = _Ride(exchange, 2, 1, 0)

    def kern(*refs):
        (x_ref, g_ref), (h_ref,), _, exrefs = ride.split(refs)

        def compute():
            xv = x_ref[...]
            h_ref[...] = ((xv * _rstd(xv)) * g_ref[...]).astype(h_ref.dtype)

        ride.around(pl.program_id(0), T // ROWS, exrefs, compute)

    row = pl.BlockSpec((ROWS, D), lambda i: (i, 0))
    return pl.pallas_call(
        kern, name=name, grid=(T // ROWS,),
        in_specs=[row, pl.BlockSpec((1, D), lambda i: (0, 0))] + ride.in_specs,
        out_specs=[row] + ride.out_specs,
        out_shape=[jax.ShapeDtypeStruct((T, D), BF16)] + ride.out_shapes,
        scratch_shapes=ride.scratch, input_output_aliases=ride.aliases,
        compiler_params=_params("arbitrary"),
    )(x, g, *ride.args)


def _norm_bwd(dh, x, g, res, name, exchange=None):
    T, D = x.shape
    ride = _Ride(exchange, 4, 3, 0)

    def kern(*refs):
        (dh_ref, x_ref, g_ref, res_ref), (dx_ref, dxb_ref, dg_ref), _, exrefs = ride.split(refs)

        def compute():
            @pl.when(pl.program_id(0) == 0)
            def _():
                dg_ref[...] = jnp.zeros_like(dg_ref)

            xv = x_ref[...]
            r = _rstd(xv)
            xn = xv * r
            dhv = dh_ref[...].astype(F32)
            dg_ref[...] += jnp.sum(dhv * xn, axis=0, keepdims=True)
            t = dhv * g_ref[...]
            dx = res_ref[...] + r * (t - xn * jnp.mean(t * xn, axis=-1, keepdims=True))
            dx_ref[...] = dx
            dxb_ref[...] = dx.astype(BF16)

        ride.around(pl.program_id(0), T // ROWS, exrefs, compute)

    row = pl.BlockSpec((ROWS, D), lambda i: (i, 0))
    vec = pl.BlockSpec((1, D), lambda i: (0, 0))
    return pl.pallas_call(
        kern, name=name, grid=(T // ROWS,),
        in_specs=[row, row, vec, row] + ride.in_specs,
        out_specs=[row, row, vec] + ride.out_specs,
        out_shape=[jax.ShapeDtypeStruct((T, D), F32), jax.ShapeDtypeStruct((T, D), BF16),
                   jax.ShapeDtypeStruct((1, D), F32)] + ride.out_shapes,
        scratch_shapes=ride.scratch, input_output_aliases=ride.aliases,
        compiler_params=_params("arbitrary"),
    )(dh, x, g, res, *ride.args)


def _loss_head(x3, tgt, g):
    T, D = x3.shape

    def kern(x_ref, t_ref, g_ref, dx_ref, dxb_ref, dg_ref, loss_ref):
        @pl.when(pl.program_id(0) == 0)
        def _():
            dg_ref[...] = jnp.zeros_like(dg_ref)
            loss_ref[...] = jnp.zeros_like(loss_ref)

        xv = x_ref[...]
        gv = g_ref[...]
        r = _rstd(xv)
        xn = xv * r
        err = xn * gv - t_ref[...]
        per_tok = jnp.mean(err * err, axis=-1, keepdims=True)
        loss_ref[...] += 0.5 * jnp.sum(per_tok, axis=0, keepdims=True)
        dy = err * (1.0 / D)
        dg_ref[...] += jnp.sum(dy * xn, axis=0, keepdims=True)
        t = dy * gv
        dx = r * (t - xn * jnp.mean(t * xn, axis=-1, keepdims=True))
        dx_ref[...] = dx
        dxb_ref[...] = dx.astype(BF16)

    row = pl.BlockSpec((ROWS, D), lambda i: (i, 0))
    vec = pl.BlockSpec((1, D), lambda i: (0, 0))
    return pl.pallas_call(
        kern, name="loss_head", grid=(T // ROWS,),
        in_specs=[row, row, vec],
        out_specs=[row, row, vec, pl.BlockSpec((1, 128), lambda i: (0, 0))],
        out_shape=[jax.ShapeDtypeStruct((T, D), F32), jax.ShapeDtypeStruct((T, D), BF16),
                   jax.ShapeDtypeStruct((1, D), F32), jax.ShapeDtypeStruct((1, 128), F32)],
        compiler_params=_params("arbitrary"),
    )(x3, tgt, g)


def _spread_matrix():
    head_of_lane = np.arange(N_HEADS * HEAD_DIM) // HEAD_DIM
    return jnp.asarray(np.arange(N_HEADS)[:, None] == head_of_lane[None, :], dtype=BF16)


def _pieces(v, n):
    out = []
    for _ in range(n):
        piece = v.astype(BF16)
        out.append(piece)
        v = v - piece.astype(F32)
    return out


def _spread(v, spread):
    return sum(_dot(p, spread, NN) for p in _pieces(v, 2))


def _spread_weights(w1, w2, spread):
    s1, s2 = _spread(w1, spread), _spread(w2, spread)
    return s1, s2, 1.0 - s1 - s2


def _head_sums(v, spread):
    return sum(_dot(p, spread, NT) for p in _pieces(v, 2))


def _branch_weights(l1, l2, l3):
    lm = jnp.maximum(jnp.maximum(l1, l2), l3)
    e1, e2, e3 = jnp.exp(l1 - lm), jnp.exp(l2 - lm), jnp.exp(l3 - lm)
    inv = 1.0 / (e1 + e2 + e3)
    return e1 * inv, e2 * inv, e3 * inv


def _regroup_scratch(rows, width):
    return pltpu.VMEM((width // LANES, rows, LANES), F32)


def _to_token_order(view_ref, scr, dil):
    if dil == 1:
        return view_ref[...].astype(F32)
    n_l, w = view_ref.shape[0], view_ref.shape[1] // dil
    for r in range(dil):
        for cb in range(w // LANES):
            scr[cb, pl.ds(r, n_l, stride=dil), :] = view_ref[:, r * w + cb * LANES:r * w + (cb + 1) * LANES].astype(F32)
    return jnp.concatenate([scr[cb] for cb in range(w // LANES)], axis=1)


def _to_class_order(val, scr, view_ref, dil):
    if dil == 1:
        view_ref[...] = val.astype(view_ref.dtype)
        return
    n, w = val.shape
    for cb in range(w // LANES):
        scr[cb] = val[:, cb * LANES:(cb + 1) * LANES]
    for r in range(dil):
        for cb in range(w // LANES):
            view_ref[:, r * w + cb * LANES:r * w + (cb + 1) * LANES] = (
                scr[cb, pl.ds(r, n // dil, stride=dil), :].astype(view_ref.dtype))


def _view_spec(rows, width, dil):
    return pl.BlockSpec((rows // dil, dil * width), lambda i: (i, 0))


def _mix_fwd(oa, obs, lbs, ga, gb, dils):
    T, W = oa.shape

    def kern(oa_ref, o1, o2, o3, l1, l2, l3, ga_ref, gb_ref, sp_ref, mix_ref, *scr):
        sp = sp_ref[...]
        w1, w2, w3 = _branch_weights(l1[...], l2[...], l3[...])
        on = [_to_token_order(o, s, d) for o, s, d in zip((o1, o2, o3), scr, dils)]
        s1, s2, s3 = _spread_weights(w1, w2, sp)
        ob = s1 * on[0] + s2 * on[1] + s3 * on[2]
        oav = oa_ref[...]
        mix_ref[:, :W] = ((oav * _rstd(oav)) * ga_ref[...]).astype(BF16)
        mix_ref[:, W:] = ((ob * _rstd(ob)) * gb_ref[...]).astype(BF16)

    row = pl.BlockSpec((MIX_ROWS, W), lambda i: (i, 0))
    per_head = pl.BlockSpec((MIX_ROWS, N_HEADS), lambda i: (i, 0))
    vec = pl.BlockSpec((1, W), lambda i: (0, 0))
    return pl.pallas_call(
        kern, name="mix_fwd", grid=(T // MIX_ROWS,),
        in_specs=[row] + [_view_spec(MIX_ROWS, W, d) for d in dils] + [per_head] * 3
        + [vec, vec, pl.BlockSpec((N_HEADS, W), lambda i: (0, 0))],
        out_specs=pl.BlockSpec((MIX_ROWS, 2 * W), lambda i: (i, 0)),
        out_shape=jax.ShapeDtypeStruct((T, 2 * W), BF16),
        scratch_shapes=[_regroup_scratch(MIX_ROWS, W)] * 3,
        compiler_params=_params("parallel"),
    )(oa, *obs, *lbs, ga, gb, _spread_matrix())


def _mix_bwd(dmix, oa, obs, lbs, ga, gb, dils, exchange=None):
    T, W = oa.shape
    ride = _Ride(exchange, 11, 10, 3)

    def kern(*refs):
        ins, outs, scr, exrefs = ride.split(refs)
        ride.around(pl.program_id(0), T // MIX_ROWS, exrefs, lambda: compute(*ins, *outs, *scr))

    def compute(dm_ref, oa_ref, o1, o2, o3, l1, l2, l3, ga_ref, gb_ref, sp_ref,
                doa_ref, da_ref, do1, do2, do3, d1, d2, d3, dga_ref, dgb_ref, *scr):
        @pl.when(pl.program_id(0) == 0)
        def _():
            dga_ref[...] = jnp.zeros_like(dga_ref)
            dgb_ref[...] = jnp.zeros_like(dgb_ref)

        sp = sp_ref[...]
        oav = oa_ref[...]
        r = _rstd(oav)
        on = oav * r
        dy = dm_ref[:, :W]
        dga_ref[...] += jnp.sum(dy * on, axis=0, keepdims=True)
        t = dy * ga_ref[...]
        doa = r * (t - on * jnp.mean(t * on, axis=-1, keepdims=True))
        doa_ref[...] = doa.astype(BF16)
        da_ref[...] = _head_sums(doa * oav, sp)
        w1, w2, w3 = _branch_weights(l1[...], l2[...], l3[...])
        s1, s2, s3 = _spread_weights(w1, w2, sp)
        on = [_to_token_order(o, sc, d) for o, sc, d in zip((o1, o2, o3), scr, dils)]
        ob = s1 * on[0] + s2 * on[1] + s3 * on[2]
        r = _rstd(ob)
        on = ob * r
        dy = dm_ref[:, W:]
        dgb_ref[...] += jnp.sum(dy * on, axis=0, keepdims=True)
        t = dy * gb_ref[...]
        dob = r * (t - on * jnp.mean(t * on, axis=-1, keepdims=True))
        c = _head_sums(dob * ob, sp)
        for do_ref, sn, sc, d in zip((do1, do2, do3), (s1, s2, s3), scr, dils):
            _to_class_order(sn * dob, sc, do_ref, d)
        d1[...] = w1 * c
        d2[...] = w2 * c
        d3[...] = w3 * c

    row = pl.BlockSpec((MIX_ROWS, W), lambda i: (i, 0))
    per_head = pl.BlockSpec((MIX_ROWS, N_HEADS), lambda i: (i, 0))
    vec = pl.BlockSpec((1, W), lambda i: (0, 0))
    bf = jax.ShapeDtypeStruct((T, W), BF16)
    ph = jax.ShapeDtypeStruct((T, N_HEADS), F32)
    vv = jax.ShapeDtypeStruct((1, W), F32)
    views = [_view_spec(MIX_ROWS, W, d) for d in dils]
    return pl.pallas_call(
        kern, name="mix_bwd", grid=(T // MIX_ROWS,),
        in_specs=[pl.BlockSpec((MIX_ROWS, 2 * W), lambda i: (i, 0)), row] + views + [per_head] * 3 + [vec, vec,
                  pl.BlockSpec((N_HEADS, W), lambda i: (0, 0))] + ride.in_specs,
        out_specs=[row, per_head] + views + [per_head, per_head, per_head, vec, vec] + ride.out_specs,
        out_shape=[bf, ph] + [jax.ShapeDtypeStruct(o.shape, F32) for o in obs] + [ph, ph, ph, vv, vv]
        + ride.out_shapes,
        scratch_shapes=[_regroup_scratch(MIX_ROWS, W)] * 3 + ride.scratch,
        input_output_aliases=ride.aliases,
        compiler_params=_params("arbitrary"),
    )(dmix, oa, *obs, *lbs, ga, gb, _spread_matrix(), *ride.args)


def _alibi_slopes(n):
    return np.asarray(2.0 ** (-8.0 * (np.arange(n) + 1) / n)).astype(np.float32)


def _band_bias(max_steps, step_dist):
    qi = np.arange(BLOCK)[None, :]
    kj = np.arange(BLOCK)[:, None]
    slopes = _alibi_slopes(N_HEADS)
    halves = []
    for steps in (qi + BLOCK - kj, qi - kj):
        valid = (steps >= 0) & (steps <= max_steps)
        alibi = slopes[:, None, None] * (step_dist * steps).astype(np.float32)[None]
        halves.append(np.where(valid[None], -alibi, np.float32(NEG_INF)).astype(np.float32))
    per_head = np.concatenate(halves, axis=1)
    return jnp.asarray(np.concatenate([per_head[0::2], per_head[1::2]], axis=2))


class _AttnLayout:
    def __init__(self, dil, kv_heads, q_stride, q_off, k_stride, k_off, v_off):
        self.dil = dil
        self.kv_heads = kv_heads
        self.kw = kv_heads * HEAD_DIM
        self.rep = N_HEADS // kv_heads
        self.q_col = lambda r: r * q_stride + q_off
        self.k_col = lambda r: r * k_stride + k_off
        self.v_col = lambda r: r * k_stride + v_off


QW = N_HEADS * HEAD_DIM
LANES = 128


PAIRS = N_HEADS // 2


def _pair_cols(pair):
    return slice(pair * LANES, (pair + 1) * LANES)


def _first_head_lanes(shape):
    return lax.broadcasted_iota(jnp.int32, shape, 1) < HEAD_DIM


def _split_heads(pair):
    first = _first_head_lanes(pair.shape)
    zero = jnp.zeros_like(pair)
    return jnp.concatenate([jnp.where(first, pair, zero), jnp.where(first, zero, pair)], axis=0)


def _kv_pair(ref, pair, rep):
    if rep == 1:
        return ref[:, _pair_cols(pair)]
    blk = ref[...].astype(F32)
    other = pltpu.roll(blk, HEAD_DIM, 1)
    first = _first_head_lanes(blk.shape)
    both = jnp.where(first, blk, other) if (2 * pair // rep) % 2 == 0 else jnp.where(first, other, blk)
    return both.astype(ref.dtype)


def _paired_kv(prev_ref, cur_ref, rep, transposed=False):
    memo = {}

    def get(pair):
        key = pair if rep == 1 else 2 * pair // rep
        if key not in memo:
            blocks = [_kv_pair(ref, pair, rep) for ref in (prev_ref, cur_ref)]
            memo[key] = jnp.concatenate([b.T for b in blocks], axis=1) if transposed else jnp.concatenate(blocks, axis=0)
        return memo[key]

    return get


def _attn_fwd(proj, bias, sinks, lay, name, exchange=None):
    L = proj.shape[0]
    nb = L // BLOCK
    kw, rep = lay.kw, lay.rep
    use_sinks = sinks is not None
    scale = HEAD_DIM ** -0.5
    ride = _Ride(exchange, 7 if use_sinks else 6, 2, 2)

    def kern(*refs):
        ins, (o_ref, l_ref), (sc_ref, pr_ref), exrefs = ride.split(refs)
        q_ref, kc_ref, kp_ref, vc_ref, vp_ref, b_ref = ins[:6]
        s_ref = ins[6] if use_sinks else None
        r, i = pl.program_id(0), pl.program_id(1)
        first = i == 0
        ride.around(r * nb + i, lay.dil * nb, exrefs,
                    lambda: compute(q_ref, kc_ref, kp_ref, vc_ref, vp_ref, b_ref, s_ref, o_ref, l_ref, first,
                                    sc_ref, pr_ref))

    def compute(q_ref, kc_ref, kp_ref, vc_ref, vp_ref, b_ref, s_ref, o_ref, l_ref, first, sc_ref, pr_ref):
        keys, values_t = _paired_kv(kp_ref, kc_ref, rep), _paired_kv(vp_ref, vc_ref, rep, transposed=True)
        for pair in range(PAIRS):
            qs = _split_heads(q_ref[:, _pair_cols(pair)])
            s = _dot(keys(pair), qs, NT) * scale + b_ref[pair]
            sc_ref[pair, :BLOCK] = jnp.where(first, NEG_INF, s[:BLOCK])
            sc_ref[pair, BLOCK:] = s[BLOCK:]
        inv = []
        for h in range(N_HEADS):
            cols = slice(h % 2 * BLOCK, (h % 2 + 1) * BLOCK)
            s = sc_ref[h // 2, :, cols]
            m = jnp.max(s, axis=0, keepdims=True)
            if use_sinks:
                sink = s_ref[:, h:h + 1]
                m = jnp.maximum(m, sink)
            p = jnp.exp(s - m)
            denom = jnp.sum(p, axis=0, keepdims=True)
            if use_sinks:
                denom = denom + jnp.exp(sink - m)
            pr_ref[h // 2, :, cols] = p.astype(BF16)
            l_ref[h:h + 1, :] = m + jnp.log(denom)
            inv.append(1.0 / denom)
        for pair in range(PAIRS):
            both = _dot(values_t(pair), pr_ref[pair], NN)
            o_t = jnp.concatenate([both[:HEAD_DIM, :BLOCK] * inv[2 * pair], both[HEAD_DIM:, BLOCK:] * inv[2 * pair + 1]],
                                  axis=0)
            o_ref[:, _pair_cols(pair)] = o_t.T

    prev = lambda i: jnp.maximum(i - 1, 0)
    in_specs = [
        pl.BlockSpec((BLOCK, QW), lambda r, i: (i, lay.q_col(r))),
        pl.BlockSpec((BLOCK, kw), lambda r, i: (i, lay.k_col(r))),
        pl.BlockSpec((BLOCK, kw), lambda r, i: (prev(i), lay.k_col(r))),
        pl.BlockSpec((BLOCK, kw), lambda r, i: (i, lay.v_col(r))),
        pl.BlockSpec((BLOCK, kw), lambda r, i: (prev(i), lay.v_col(r))),
        pl.BlockSpec((PAIRS, 2 * BLOCK, 2 * BLOCK), lambda r, i: (0, 0, 0)),
    ]
    args = [proj, proj, proj, proj, proj, bias]
    if use_sinks:
        in_specs.append(pl.BlockSpec((1, N_HEADS), lambda r, i: (0, 0)))
        args.append(sinks)
    out_specs = [pl.BlockSpec((BLOCK, QW), lambda r, i: (i, r)),
                 pl.BlockSpec((None, N_HEADS, BLOCK), lambda r, i: (r, 0, i))]
    out_shape = [jax.ShapeDtypeStruct((L, lay.dil * QW), F32), jax.ShapeDtypeStruct((lay.dil, N_HEADS, L), F32)]
    return pl.pallas_call(
        kern, name=name, grid=(lay.dil, nb),
        in_specs=in_specs + ride.in_specs, out_specs=out_specs + ride.out_specs,
        out_shape=out_shape + ride.out_shapes,
        scratch_shapes=[pltpu.VMEM((PAIRS, 2 * BLOCK, 2 * BLOCK), dt) for dt in (F32, BF16)] + ride.scratch,
        input_output_aliases=ride.aliases,
        compiler_params=_params("arbitrary", "arbitrary"),
    )(*args, *ride.args)


def _attn_bwd(proj, do, lse, dd, bias, sinks, lay, name, exchange=None):
    L = proj.shape[0]
    nb = L // BLOCK
    kw, rep = lay.kw, lay.rep
    assert rep == 1 or lay.kv_heads == 2, "grouped queries: the two kv heads fill one 128-lane block"
    use_sinks = sinks is not None
    scale = HEAD_DIM ** -0.5
    ride = _Ride(exchange, 10 if use_sinks else 9, 4 if use_sinks else 3, 6)

    def kern(*refs):
        ins, outs, (ck_ref, cv_ref, *staged), exrefs = ride.split(refs)
        q_ref, kc_ref, kp_ref, vc_ref, vp_ref, do_ref, l_ref, d_ref, b_ref = ins[:9]
        s_ref = ins[9] if use_sinks else None
        dq_ref, dk_ref, dv_ref = outs[:3]
        ds_ref = outs[3] if use_sinks else None
        r = pl.program_id(0)
        i = pl.program_id(1)
        ride.around(r * (nb + 1) + i, lay.dil * (nb + 1), exrefs,
                    lambda: compute(q_ref, kc_ref, kp_ref, vc_ref, vp_ref, do_ref, l_ref, d_ref, b_ref, s_ref,
                                    dq_ref, dk_ref, dv_ref, ds_ref, ck_ref, cv_ref, r, i, *staged))

    def compute(q_ref, kc_ref, kp_ref, vc_ref, vp_ref, do_ref, l_ref, d_ref, b_ref, s_ref,
                dq_ref, dk_ref, dv_ref, ds_ref, ck_ref, cv_ref, r, i, sc_ref, dp_ref, pr_ref, dsc_ref):
        first = i == 0

        @pl.when(first)
        def _():
            ck_ref[...] = jnp.zeros_like(ck_ref)
            cv_ref[...] = jnp.zeros_like(cv_ref)

        if use_sinks:
            @pl.when(first & (r == 0))
            def _():
                ds_ref[...] = jnp.zeros_like(ds_ref)

        @pl.when(i < nb)
        def _():
            keys, values = _paired_kv(kp_ref, kc_ref, rep), _paired_kv(vp_ref, vc_ref, rep)
            keys_t = _paired_kv(kp_ref, kc_ref, rep, transposed=True)
            for pair in range(PAIRS):
                qs = _split_heads(q_ref[:, _pair_cols(pair)])
                dos = _split_heads(do_ref[:, _pair_cols(pair)].astype(BF16))
                s = _dot(keys(pair), qs, NT) * scale + b_ref[pair]
                sc_ref[pair, :BLOCK] = jnp.where(first, NEG_INF, s[:BLOCK])
                sc_ref[pair, BLOCK:] = s[BLOCK:]
                dp_ref[pair] = _dot(values(pair), dos, NT)
            for h in range(N_HEADS):
                cols = slice(h % 2 * BLOCK, (h % 2 + 1) * BLOCK)
                lrow = l_ref[h:h + 1, :]
                drow = d_ref[h:h + 1, :]
                p = jnp.exp(sc_ref[h // 2, :, cols] - lrow)
                pr_ref[h // 2, :, cols] = p.astype(BF16)
                dsc_ref[h // 2, :, cols] = (p * (dp_ref[h // 2, :, cols] - drow) * scale).astype(BF16)
                if use_sinks:
                    ds_ref[h:h + 1, :] += -(jnp.exp(s_ref[:, h:h + 1] - lrow) * drow)
            grouped = {}
            for pair in range(PAIRS):
                cols = _pair_cols(pair)
                qs = _split_heads(q_ref[:, cols])
                dos = _split_heads(do_ref[:, cols].astype(BF16))
                ds = dsc_ref[pair]
                both = _dot(keys_t(pair), ds, NN)
                dq_t = jnp.concatenate([both[:HEAD_DIM, :BLOCK], both[HEAD_DIM:, BLOCK:]], axis=0)
                dq_ref[:, cols] = dq_t.T.astype(dq_ref.dtype)
                dk = _dot(ds, qs, NN)
                dv = _dot(pr_ref[pair], dos, NN)
                if rep == 1:
                    dk_ref[:, cols] = (ck_ref[:, cols] + dk[:BLOCK]).astype(dk_ref.dtype)
                    dv_ref[:, cols] = (cv_ref[:, cols] + dv[:BLOCK]).astype(dv_ref.dtype)
                    ck_ref[:, cols] = dk[BLOCK:]
                    cv_ref[:, cols] = dv[BLOCK:]
                else:
                    g = 2 * pair // rep
                    grouped[g] = (dk, dv) if g not in grouped else (grouped[g][0] + dk, grouped[g][1] + dv)
            if rep > 1:
                fold = lambda t: t + pltpu.roll(t, HEAD_DIM, 1)
                first_half = _first_head_lanes((2 * BLOCK, LANES))
                dk = jnp.where(first_half, fold(grouped[0][0]), fold(grouped[1][0]))
                dv = jnp.where(first_half, fold(grouped[0][1]), fold(grouped[1][1]))
                dk_ref[...] = (ck_ref[...] + dk[:BLOCK]).astype(dk_ref.dtype)
                dv_ref[...] = (cv_ref[...] + dv[:BLOCK]).astype(dv_ref.dtype)
                ck_ref[...] = dk[BLOCK:]
                cv_ref[...] = dv[BLOCK:]

        @pl.when(i == nb)
        def _():
            dk_ref[...] = ck_ref[...].astype(dk_ref.dtype)
            dv_ref[...] = cv_ref[...].astype(dv_ref.dtype)
            if use_sinks:
                @pl.when(r == lay.dil - 1)
                def _():
                    ds_ref[...] = jnp.broadcast_to(jnp.sum(ds_ref[...], axis=1, keepdims=True), ds_ref.shape)

    cur = lambda i: jnp.minimum(i, nb - 1)
    prev = lambda i: jnp.maximum(jnp.minimum(i, nb - 1) - 1, 0)
    done = lambda i: jnp.maximum(i - 1, 0)
    qspec = lambda col: pl.BlockSpec((BLOCK, QW), lambda r, i: (cur(i), col(r)))
    per_head = pl.BlockSpec((None, N_HEADS, BLOCK), lambda r, i: (r, 0, cur(i)))
    in_specs = [
        qspec(lay.q_col),
        pl.BlockSpec((BLOCK, kw), lambda r, i: (cur(i), lay.k_col(r))),
        pl.BlockSpec((BLOCK, kw), lambda r, i: (prev(i), lay.k_col(r))),
        pl.BlockSpec((BLOCK, kw), lambda r, i: (cur(i), lay.v_col(r))),
        pl.BlockSpec((BLOCK, kw), lambda r, i: (prev(i), lay.v_col(r))),
        qspec(lambda r: r), per_head, per_head,
        pl.BlockSpec((PAIRS, 2 * BLOCK, 2 * BLOCK), lambda r, i: (0, 0, 0)),
    ]
    args = [proj, proj, proj, proj, proj, do, lse, dd, bias]
    out_specs = [
        qspec(lambda r: r),
        pl.BlockSpec((BLOCK, kw), lambda r, i: (done(i), r)),
        pl.BlockSpec((BLOCK, kw), lambda r, i: (done(i), r)),
    ]
    dkv_shape = jax.ShapeDtypeStruct((L, lay.dil * kw), BF16)
    out_shape = [jax.ShapeDtypeStruct((L, lay.dil * QW), BF16), dkv_shape, dkv_shape]
    if use_sinks:
        in_specs.append(pl.BlockSpec((1, N_HEADS), lambda r, i: (0, 0)))
        args.append(sinks)
        out_specs.append(pl.BlockSpec((N_HEADS, LANES), lambda r, i: (0, 0)))
        out_shape.append(jax.ShapeDtypeStruct((N_HEADS, LANES), F32))
    return pl.pallas_call(
        kern, name=name, grid=(lay.dil, nb + 1),
        in_specs=in_specs + ride.in_specs, out_specs=out_specs + ride.out_specs,
        out_shape=out_shape + ride.out_shapes,
        scratch_shapes=[pltpu.VMEM((BLOCK, kw), F32), pltpu.VMEM((BLOCK, kw), F32)]
        + [pltpu.VMEM((PAIRS, 2 * BLOCK, 2 * BLOCK), dt) for dt in (F32, F32, BF16, BF16)] + ride.scratch,
        input_output_aliases=ride.aliases,
        compiler_params=_params("arbitrary", "arbitrary"),
    )(*args, *ride.args)


def _assemble(groups, name, dils=(1,)):
    T = groups[0][0].shape[0] * dils[0]
    widths = [g[0].shape[1] // dils[0] for g in groups]
    total = sum(widths)
    flat = [a for g in groups for a in g]
    member_dils = [d for g in groups for d in dils[:len(g)]]

    def kern(*refs):
        ins = refs[:len(flat)]
        out_ref, cs_ref = refs[len(flat):len(flat) + 2]
        scr = refs[len(flat) + 2:]

        @pl.when(pl.program_id(0) == 0)
        def _():
            cs_ref[...] = jnp.zeros_like(cs_ref)

        pos = off = 0
        for g, w in zip(groups, widths):
            acc = _to_token_order(ins[pos], None, dils[0])
            for j in range(1, len(g)):
                acc = acc + _to_token_order(ins[pos + j], scr[j - 1], dils[j])
            pos += len(g)
            out_ref[:, off:off + w] = acc.astype(BF16)
            cs_ref[:, off:off + w] += jnp.sum(acc, axis=0, keepdims=True)
            off += w

    return pl.pallas_call(
        kern, name=name, grid=(T // ROWS,),
        in_specs=[_view_spec(ROWS, a.shape[1] // d, d) for a, d in zip(flat, member_dils)],
        out_specs=[pl.BlockSpec((ROWS, total), lambda i: (i, 0)), pl.BlockSpec((1, total), lambda i: (0, 0))],
        out_shape=[jax.ShapeDtypeStruct((T, total), BF16), jax.ShapeDtypeStruct((1, total), F32)],
        scratch_shapes=[_regroup_scratch(ROWS, max(widths))] * (len(dils) - 1),
        compiler_params=_params("arbitrary"),
    )(*flat)


def _adamw(w, g, m, v, name):
    R, C = w.shape
    rows = min(R, ROWS)
    assert R % rows == 0

    def kern(w_ref, g_ref, m_ref, v_ref, d_ref, nm_ref, nv_ref):
        gv = g_ref[...]
        mn = ADAM_B1 * m_ref[...] + (1.0 - ADAM_B1) * gv
        vn = ADAM_B2 * v_ref[...] + (1.0 - ADAM_B2) * jnp.square(gv)
        m_hat = mn / (1.0 - ADAM_B1 ** ADAM_STEP)
        v_hat = vn / (1.0 - ADAM_B2 ** ADAM_STEP)
        d_ref[...] = -ADAM_LR * (m_hat / (jnp.sqrt(v_hat) + ADAM_EPS) + ADAM_WD * w_ref[...])
        nm_ref[...] = mn
        nv_ref[...] = vn

    blk = pl.BlockSpec((rows, C), lambda i: (i, 0))
    shp = jax.ShapeDtypeStruct((R, C), F32)
    return pl.pallas_call(
        kern, name=name, grid=(R // rows,),
        in_specs=[blk] * 4, out_specs=[blk] * 3, out_shape=[shp] * 3,
        compiler_params=_params("parallel"),
    )(w, g, m, v)


def _sum_slots(slots, name):
    n, R, C = slots.shape
    SUM_ROWS = next(rows for rows in (128, 64, 32, 16) if R % rows == 0)

    def kern(s_ref, o_ref):
        acc = s_ref[0].astype(F32)
        for k in range(1, n):
            acc = acc + s_ref[k].astype(F32)
        o_ref[...] = acc

    return pl.pallas_call(
        kern, name=name, grid=(R // SUM_ROWS,),
        in_specs=[pl.BlockSpec((n, SUM_ROWS, C), lambda i: (0, i, 0))],
        out_specs=pl.BlockSpec((SUM_ROWS, C), lambda i: (i, 0)),
        out_shape=jax.ShapeDtypeStruct((R, C), F32),
        compiler_params=_params("parallel"),
    )(slots)


def _place():
    return lax.axis_index("x"), lax.axis_index("y"), lax.axis_index("c")


def _index(p):
    return 4 * p[0] + 2 * p[1] + p[2]


FLIPS = [(fx, fy, fc) for fx in (0, 1) for fy in (0, 1) for fc in (0, 1)][1:]


def _peer(me, flip):
    return tuple(1 - a if f else a for a, f in zip(me, flip))


def _gather_rows(shards, part=(0, 1), into=None):
    nw = len(shards)

    def plan(ins, outs, send_sems, recv_sems):
        x, y, c = me = _place()
        sibling = (x, y, 1 - c)
        chips = [(1 - x, y), (x, 1 - y), (1 - x, 1 - y)]

        def span(w):
            cnt = ins[w].shape[0] // part[1]
            return part[0] * cnt, cnt

        def rows(w, p):
            lo, cnt = span(w)
            return outs[w].at[pl.ds(_index(p) * ins[w].shape[0] + lo, cnt), :]

        def own(w):
            lo, cnt = span(w)
            return ins[w].at[pl.ds(lo, cnt), :]

        def copy(w, k, block, to):
            return pltpu.make_async_remote_copy(
                src_ref=own(w) if block is me else rows(w, block), dst_ref=rows(w, block),
                send_sem=send_sems.at[7 * w + k], recv_sem=recv_sems.at[7 * w + k],
                device_id=to, device_id_type=MESH)

        return me, sibling, chips, c, rows, own, copy

    def copies(ins, outs, send_sems, recv_sems, local_sems):
        me, sibling, chips, c, rows, own, copy = plan(ins, outs, send_sems, recv_sems)
        local = [pltpu.make_async_copy(own(w), rows(w, me), local_sems.at[w]) for w in range(nw)]
        sends, recvs = [], []
        for w in range(nw):
            sends.append(copy(w, 0, me, sibling))
            sends += [copy(w, 1 + j, me, (*chip, c)) for j, chip in enumerate(chips)]
            recvs.append(copy(w, 0, sibling, me))
            recvs += [copy(w, 4 + j, (*chip, 1 - c), me) for j, chip in enumerate(chips)]
        return local, sends, recvs

    def relay(ins, outs, send_sems, recv_sems, local_sems):
        me, sibling, chips, c, rows, own, copy = plan(ins, outs, send_sems, recv_sems)
        arrived = [copy(w, 1 + j, (*chip, c), me) for w in range(nw) for j, chip in enumerate(chips)]
        onward = [copy(w, 4 + j, (*chip, c), sibling) for w in range(nw) for j, chip in enumerate(chips)]
        return arrived, onward

    shapes = [jax.ShapeDtypeStruct((N_DEV * s.shape[0], s.shape[1]), s.dtype) for s in shards]
    aliases = {nw + w: w for w in range(nw)} if into else None
    return _Exchange(shards + (into or []), shapes, 7 * nw, nw, copies, aliases=aliases, relay=relay)


def _scatter_rows(parts, part=(0, 1)):
    nw = len(parts)

    def copies(ins, outs, send_sems, recv_sems, local_sems):
        me = _place()

        def src(w, owner):
            n = ins[w].shape[0] // N_DEV
            cnt = n // part[1]
            return ins[w].at[pl.ds(_index(owner) * n + part[0] * cnt, cnt), :]

        def copy(k, w, owner, sender, to):
            return pltpu.make_async_remote_copy(
                src_ref=src(w, owner), dst_ref=outs[w].at[_index(sender)],
                send_sem=send_sems.at[nw * k + w], recv_sem=recv_sems.at[nw * k + w],
                device_id=to, device_id_type=MESH)

        local = [pltpu.make_async_copy(src(w, me), outs[w].at[_index(me)], local_sems.at[w]) for w in range(nw)]
        peers = [_peer(me, flip) for flip in FLIPS]
        sends = [copy(k, w, peer, me, peer) for k, peer in enumerate(peers) for w in range(nw)]
        recvs = [copy(k, w, me, peer, me) for k, peer in enumerate(peers) for w in range(nw)]
        return local, sends, recvs

    shapes = [jax.ShapeDtypeStruct((N_DEV, p.shape[0] // N_DEV // part[1], p.shape[1]), p.dtype) for p in parts]
    return _Exchange(parts, shapes, 7 * nw, nw, copies)


def _sum_over_devices(v):
    shape = v.shape

    def body(v_ref, sum_ref, all_ref, send_sems, recv_sems):
        me = _place()
        all_ref[_index(me)] = v_ref[...]
        sends = []
        for k, flip in enumerate(FLIPS):
            peer = _peer(me, flip)
            sends.append(pltpu.make_async_remote_copy(
                src_ref=v_ref, dst_ref=all_ref.at[_index(me)],
                send_sem=send_sems.at[k], recv_sem=recv_sems.at[k], device_id=peer, device_id_type=MESH))
            sends[-1].start()
        for k, flip in enumerate(FLIPS):
            peer = _peer(me, flip)
            pltpu.make_async_remote_copy(
                src_ref=v_ref, dst_ref=all_ref.at[_index(peer)],
                send_sem=send_sems.at[k], recv_sem=recv_sems.at[k], device_id=peer, device_id_type=MESH).wait_recv()
        for cp in sends:
            cp.wait_send()
        acc = all_ref[0]
        for s in range(1, N_DEV):
            acc = acc + all_ref[s]
        sum_ref[...] = acc

    vmem = pl.BlockSpec(memory_space=pltpu.VMEM)
    return pl.pallas_call(
        body, name="sum_small_grads",
        in_specs=[vmem], out_specs=[vmem, vmem],
        out_shape=[jax.ShapeDtypeStruct(shape, F32), jax.ShapeDtypeStruct((N_DEV,) + shape, F32)],
        scratch_shapes=[pltpu.SemaphoreType.DMA((7,)), pltpu.SemaphoreType.DMA((7,))],
    )(v)[0]


SMALL_ROWS = 8


def _pack_small(vectors):
    padded = []
    for vec in vectors:
        vec = vec.reshape(-1)
        padded.append(jnp.pad(vec, (0, -vec.shape[0] % 128)))
    flat = jnp.concatenate(padded)
    flat = jnp.pad(flat, (0, -flat.shape[0] % (SMALL_ROWS * 128)))
    return flat.reshape(SMALL_ROWS, -1)


def _unpack_small(packed, shapes):
    flat = packed.reshape(-1)
    out, off = [], 0
    for shp in shapes:
        n = int(np.prod(shp))
        out.append(flat[off:off + n].reshape(shp))
        off += n + (-n % 128)
    return out


def kernel(x, g_attn, w_in, b_in, sinks_a, g_out_a, g_out_b, w_out, g_mlp, w_1, w_2, g_final, loss_target, m_g_attn, m_w_in, m_b_in, m_sinks_a, m_g_out_a, m_g_out_b, m_w_out, m_g_mlp, m_w_1, m_w_2, m_g_final, v_g_attn, v_w_in, v_b_in, v_sinks_a, v_g_out_a, v_g_out_b, v_w_out, v_g_mlp, v_w_1, v_w_2, v_g_final):
    xs, tgt = x[0], loss_target[0]
    T, D = xs.shape
    n_a = QW + 2 * KV_HEADS_A * HEAD_DIM
    g_fin = g_final.reshape(1, D)

    shards = [w_in[0].T.astype(BF16), w_out[0].astype(BF16), w_1[0].T.astype(BF16), w_2[0].astype(BF16)]
    ident = lambda acc: (acc,)
    add = lambda acc, other: (acc + other,)
    tiles = dict(tm=512, tn=1024)

    h1, w_in_t = _norm_fwd(xs, g_attn, "norm_attn", exchange=_gather_rows(shards[:1]))
    w_in_ta, w_in_tb = w_in_t[:n_a], w_in_t[n_a:]
    proj_a, = _matmul(h1, w_in_ta, "nt", [BF16], add, tm=512, tn=n_a, tk=D, row_ins=[b_in[:, :n_a]], name="proj_a")
    dils = [dil for _, dil in DILATED_BRANCHES]
    *proj_b, w_o = _proj_views(h1, w_in_tb, b_in[:, n_a:], dils, "proj_b", exchange=_gather_rows(shards[1:2]))

    lay_a = _AttnLayout(1, KV_HEADS_A, 0, 0, 0, QW // (KV_HEADS_A * HEAD_DIM), QW // (KV_HEADS_A * HEAD_DIM) + 1)
    bias_a = _band_bias(WINDOW_A - 1, 1)
    o_a, l_a, w_1_t = _attn_fwd(proj_a, bias_a, sinks_a, lay_a, "attn_a_fwd",
                                exchange=_gather_rows(shards[2:3], part=(0, 4)))
    branches = []
    for n, (window, dil) in enumerate(DILATED_BRANCHES):
        lay = _AttnLayout(dil, N_HEADS, 3, 0, 3, 1, 2)
        bias = _band_bias(window // dil, dil)
        ride = _gather_rows(shards[2:3], part=(n + 1, 4), into=[w_1_t])
        o, lse, w_1_t = _attn_fwd(proj_b[n], bias, None, lay, f"attn_b{dil}_fwd", exchange=ride)
        branches.append((lay, bias, proj_b[n], o, lse))
    o_b = [br[3] for br in branches]
    l_b = [br[4].transpose(2, 0, 1).reshape(T, N_HEADS) for br in branches]

    mix = _mix_fwd(o_a, o_b, l_b, g_out_a, g_out_b, dils)
    wide = dict(tm=512, tn=2048)

    def residual_and_norm(acc, res, g):
        x_new = acc + res
        return x_new, (x_new * _rstd(x_new)) * g

    assert wide["tn"] == D
    x2, h2 = _matmul(mix, w_o, "nn", [F32, BF16], residual_and_norm, tk=D, tile_ins=[xs], row_ins=[g_mlp],
                     name="out_proj", **wide)

    def relu_sq(acc):
        u = jnp.maximum(acc, 0.0)
        return u, u * u

    u, u_sq, w_2_f = _matmul(h2, w_1_t, "nt", [BF16, BF16], relu_sq, tk=D, name="mlp_up",
                             exchange=_gather_rows(shards[3:]), **wide)
    x3, = _matmul(u_sq, w_2_f, "nn", [F32], add, tk=4096, tile_ins=[x2], name="mlp_down", **tiles)

    dx3, dx3_b, dg_final, loss_dev = _loss_head(x3, tgt, g_fin)

    d_pre, = _matmul(dx3_b, w_2_f, "nt", [BF16], lambda acc, uu: (acc * (2.0 * uu.astype(F32)),),
                     tk=D, tile_ins=[u], name="mlp_down_bwd", **wide)
    wtiles = dict(tm=1024, tn=1024, tk=4096)
    dw_2, = _matmul(u_sq, dx3_b, "tn", [BF16], ident, name="mlp_down_wgrad", **wtiles)
    dh2, slots_2a = _matmul(d_pre, w_1_t, "nn", [BF16], ident, tk=4096, name="mlp_up_bwd",
                            exchange=_scatter_rows([dw_2], part=(0, 2)), **tiles)
    dw_1_t, slots_2b = _matmul(d_pre, h2, "tn", [BF16], ident, name="mlp_up_wgrad",
                               exchange=_scatter_rows([dw_2], part=(1, 2)), **wtiles)
    dx2, dx2_b, dg_mlp = _norm_bwd(dh2, x2, g_mlp, dx3, "norm_mlp_bwd")

    dmix, = _matmul(dx2_b, w_o, "nt", [F32], ident, tk=D, name="out_proj_bwd", **wide)
    dw_o, = _matmul(mix, dx2_b, "tn", [BF16], ident, name="out_proj_wgrad", **wtiles)
    do_a, dd_a, do1, do2, do3, dd1, dd2, dd3, dg_out_a, dg_out_b, slots_o = _mix_bwd(
        dmix, o_a, o_b, l_b, g_out_a, g_out_b, dils, exchange=_scatter_rows([dw_o]))

    by_class = lambda d, dil: d.reshape(T // dil, dil, N_HEADS).transpose(1, 2, 0)
    slots_1 = [None] * 4
    dq_a, dk_a, dv_a, dsinks, slots_1[0] = _attn_bwd(proj_a, do_a, l_a, by_class(dd_a, 1), bias_a, sinks_a, lay_a,
                                                     "attn_a_bwd", exchange=_scatter_rows([dw_1_t], part=(0, 4)))
    dsinks = dsinks[:, 0].reshape(1, N_HEADS)
    dqs, dks, dvs = [], [], []
    for n, ((lay, bias, view, _, lse), do_n, dd_n) in enumerate(zip(branches, (do1, do2, do3), (dd1, dd2, dd3))):
        dq, dk, dv, slots_1[n + 1] = _attn_bwd(view, do_n, lse, by_class(dd_n, lay.dil), bias, None, lay,
                                               f"attn_b{lay.dil}_bwd",
                                               exchange=_scatter_rows([dw_1_t], part=(n + 1, 4)))
        dqs.append(dq)
        dks.append(dk)
        dvs.append(dv)
    dproj, db_in = _assemble([[dq_a], [dk_a], [dv_a], dqs, dks, dvs], "dproj", dils)

    n_in = w_in_t.shape[0]
    dw_in_t, = _matmul(dproj, h1, "tn", [BF16], ident, tm=n_in // 2, tn=1024, tk=1024, name="in_proj_wgrad")
    dh1, slots_in = _matmul(dproj, w_in_t, "nn", [BF16], ident, tk=n_in, name="in_proj_bwd",
                            exchange=_scatter_rows([dw_in_t]), **tiles)
    dx, _, dg_attn = _norm_bwd(dh1, xs, g_attn, dx2, "norm_attn_bwd")

    g_w_in = _sum_slots(slots_in, "sum_w_in_grads").T
    g_w_out = _sum_slots(slots_o, "sum_w_out_grads")
    g_w_1 = jnp.concatenate([_sum_slots(s, f"sum_w_1_grads_{n}") for n, s in enumerate(slots_1)]).T
    g_w_2 = jnp.concatenate([_sum_slots(slots_2a, "sum_w_2_grads_0"), _sum_slots(slots_2b, "sum_w_2_grads_1")])

    small_w = [g_attn, b_in, sinks_a, g_out_a, g_out_b, g_mlp, g_final]
    small_m = [m_g_attn, m_b_in, m_sinks_a, m_g_out_a, m_g_out_b, m_g_mlp, m_g_final]
    small_v = [v_g_attn, v_b_in, v_sinks_a, v_g_out_a, v_g_out_b, v_g_mlp, v_g_final]
    small_g = [dg_attn, db_in, dsinks, dg_out_a, dg_out_b, dg_mlp, dg_final]
    summed = _sum_over_devices(_pack_small(small_g + [loss_dev[:, :1]]))
    shapes = [w.shape for w in small_w]
    *g_small, loss = _unpack_small(summed, shapes + [()])

    big = [
        _adamw(w_in[0], g_w_in, m_w_in[0], v_w_in[0], "adamw_w_in"),
        _adamw(w_out[0], g_w_out, m_w_out[0], v_w_out[0], "adamw_w_out"),
        _adamw(w_1[0], g_w_1, m_w_1[0], v_w_1[0], "adamw_w_1"),
        _adamw(w_2[0], g_w_2, m_w_2[0], v_w_2[0], "adamw_w_2"),
    ]
    g_packed = _pack_small(g_small)
    small = _adamw(_pack_small(small_w), g_packed, _pack_small(small_m), _pack_small(small_v), "adamw_small")
    small = [_unpack_small(s, shapes) for s in small]

    def ordered(small_list, big_list):
        s = list(small_list)
        return [s[0], big_list[0][None], s[1], s[2], s[3], s[4], big_list[1][None], s[5],
                big_list[2][None], big_list[3][None], s[6]]

    grads = ordered(g_small, [g_w_in, g_w_out, g_w_1, g_w_2])
    deltas = ordered(small[0], [b[0] for b in big])
    new_m = ordered(small[1], [b[1] for b in big])
    new_v = ordered(small[2], [b[2] for b in big])
    return (loss, dx[None], *grads, *deltas, *new_m, *new_v)
```

```python
import numpy as np
import jax
import jax.numpy as jnp
from jax import lax
from jax.experimental import pallas as pl
from jax.experimental.pallas import tpu as pltpu

F32 = jnp.float32
BF16 = jnp.bfloat16

HEAD_DIM = 64
N_HEADS = 16
KV_HEADS_A = 2
BLOCK = 128
WINDOW_A = 128
DILATED_BRANCHES = ((128, 1), (512, 4), (2048, 16))
EPS = 1e-5
NEG_INF = -1e30
N_DEV = 8

ADAM_LR = 0.001
ADAM_B1 = 0.9
ADAM_B2 = 0.999
ADAM_EPS = 1e-08
ADAM_WD = 0.01
ADAM_STEP = 10

VMEM_LIMIT_BYTES = 56 * 1024 * 1024
MESH = pl.DeviceIdType.MESH
ANY = pl.BlockSpec(memory_space=pl.ANY)

NN = (((1,), (0,)), ((), ()))
NT = (((1,), (1,)), ((), ()))
TN = (((0,), (0,)), ((), ()))


def _dot(a, b, dims):
    return lax.dot_general(a, b, dims, preferred_element_type=F32)


def _params(*sem):
    return pltpu.CompilerParams(dimension_semantics=sem, vmem_limit_bytes=VMEM_LIMIT_BYTES)


RELAY_AT = 0.6


class _Exchange:
    def __init__(self, ins, out_shapes, n_remote, n_local, copies, aliases=None, relay=None):
        self.ins, self.out_shapes = list(ins), list(out_shapes)
        self.n_remote, self.n_local = n_remote, n_local
        self.copies = copies
        self.relay = relay
        self.aliases = aliases or {}

    def start(self, refs):
        local, sends, _ = self.copies(*refs)
        for cp in local + sends:
            cp.start()

    def middle(self, refs):
        arrived, onward = self.relay(*refs)
        for got, cp in zip(arrived, onward):
            got.wait_recv()
            cp.start()

    def finish(self, refs):
        local, sends, recvs = self.copies(*refs)
        for cp in recvs:
            cp.wait_recv()
        for cp in sends:
            cp.wait_send()
        for cp in local:
            cp.wait()
        if self.relay:
            for cp in self.relay(*refs)[1]:
                cp.wait_send()


class _Ride:
    def __init__(self, ex, n_in, n_out, n_scratch):
        self.ex = ex
        self.n = (n_in, n_out, n_scratch)
        self.args = ex.ins if ex else []
        self.in_specs = [ANY] * len(self.args)
        self.out_shapes = ex.out_shapes if ex else []
        self.out_specs = [ANY] * len(self.out_shapes)
        self.scratch = [pltpu.SemaphoreType.DMA((ex.n_remote,)), pltpu.SemaphoreType.DMA((ex.n_remote,)),
                        pltpu.SemaphoreType.DMA((max(ex.n_local, 1),))] if ex else []
        self.aliases = {n_in + i: n_out + o for i, o in ex.aliases.items()} if ex else {}

    def split(self, refs):
        n_in, n_out, n_scratch = self.n
        a = n_in
        b = a + len(self.args)
        c = b + n_out
        d = c + len(self.out_shapes)
        e = d + n_scratch
        return refs[:a], refs[b:c], refs[d:e], (refs[a:b], refs[c:d], *refs[e:])

    def around(self, step, n_steps, exrefs, compute):
        if self.ex is None:
            compute()
            return

        @pl.when(step == 0)
        def _():
            self.ex.start(exrefs)

        compute()

        if self.ex.relay:
            @pl.when(step == int(RELAY_AT * (n_steps - 1)))
            def _():
                self.ex.middle(exrefs)

        @pl.when(step == n_steps - 1)
        def _():
            self.ex.finish(exrefs)


def _matmul(a, b, dims, out_dtypes, epilogue, *, tm, tn, tk, name, tile_ins=(), row_ins=(), exchange=None):
    if dims == "tn":
        K, M = a.shape
    else:
        M, K = a.shape
    N = b.shape[0] if dims == "nt" else b.shape[1]
    tm, tn, tk = min(tm, M), min(tn, N), min(tk, K)
    assert M % tm == 0 and N % tn == 0 and K % tk == 0, (name, M, N, K, tm, tn, tk)
    grid = (M // tm, N // tn, K // tk)
    nk = grid[2]
    n_tile, n_row, n_out = len(tile_ins), len(row_ins), len(out_dtypes)
    dn = {"nn": NN, "nt": NT, "tn": TN}[dims]
    ride = _Ride(exchange, 2 + n_tile + n_row, n_out, 1 if nk > 1 else 0)

    def kern(*refs):
        ins, out_refs, scratch, exrefs = ride.split(refs)
        a_ref, b_ref = ins[:2]
        tile_refs = ins[2:2 + n_tile]
        row_refs = ins[2 + n_tile:]
        ids = [pl.program_id(d) for d in range(3)]

        def finish(acc):
            outs = epilogue(acc, *[r[...] for r in tile_refs], *[r[...] for r in row_refs])
            for o_ref, o in zip(out_refs, outs):
                o_ref[...] = o.astype(o_ref.dtype)

        def compute():
            if nk == 1:
                finish(_dot(a_ref[...], b_ref[...], dn))
                return
            acc_ref = scratch[0]

            @pl.when(ids[2] == 0)
            def _():
                acc_ref[...] = jnp.zeros_like(acc_ref)

            acc_ref[...] += _dot(a_ref[...], b_ref[...], dn)

            @pl.when(ids[2] == nk - 1)
            def _():
                finish(acc_ref[...])

        ride.around((ids[0] * grid[1] + ids[1]) * grid[2] + ids[2], grid[0] * grid[1] * grid[2], exrefs, compute)

    if dims == "tn":
        a_spec = pl.BlockSpec((tk, tm), lambda i, j, k: (k, i))
    else:
        a_spec = pl.BlockSpec((tm, tk), lambda i, j, k: (i, k))
    if dims == "nt":
        b_spec = pl.BlockSpec((tn, tk), lambda i, j, k: (j, k))
    else:
        b_spec = pl.BlockSpec((tk, tn), lambda i, j, k: (k, j))
    tile_spec = pl.BlockSpec((tm, tn), lambda i, j, k: (i, j))
    row_spec = pl.BlockSpec((1, tn), lambda i, j, k: (0, j))
    sem = ("arbitrary",) * 3 if exchange else ("parallel", "parallel", "arbitrary")
    return pl.pallas_call(
        kern,
        name=name,
        grid=grid,
        in_specs=[a_spec, b_spec] + [tile_spec] * n_tile + [row_spec] * n_row + ride.in_specs,
        out_specs=[tile_spec] * n_out + ride.out_specs,
        out_shape=[jax.ShapeDtypeStruct((M, N), dt) for dt in out_dtypes] + ride.out_shapes,
        scratch_shapes=([pltpu.VMEM((tm, tn), F32)] if nk > 1 else []) + ride.scratch,
        input_output_aliases=ride.aliases,
        compiler_params=_params(*sem),
    )(a, b, *tile_ins, *row_ins, *ride.args)


PROJ_ROWS = 256


def _proj_views(a, w_t, bias, dils, name, exchange=None):
    T, K = a.shape
    N = w_t.shape[0]
    ride = _Ride(exchange, 3, len(dils), 1)

    def kern(*refs):
        (a_ref, w_ref, b_ref), outs, (scr,), exrefs = ride.split(refs)

        def compute():
            acc = _dot(a_ref[...], w_ref[...], NT) + b_ref[...]
            for out_ref, dil in zip(outs, dils):
                _to_class_order(acc, scr, out_ref, dil)

        ride.around(pl.program_id(0), T // PROJ_ROWS, exrefs, compute)

    return pl.pallas_call(
        kern, name=name, grid=(T // PROJ_ROWS,),
        in_specs=[pl.BlockSpec((PROJ_ROWS, K), lambda i: (i, 0)), pl.BlockSpec((N, K), lambda i: (0, 0)),
                  pl.BlockSpec((1, N), lambda i: (0, 0))] + ride.in_specs,
        out_specs=[_view_spec(PROJ_ROWS, N, d) for d in dils] + ride.out_specs,
        out_shape=[jax.ShapeDtypeStruct((T // d, d * N), BF16) for d in dils] + ride.out_shapes,
        scratch_shapes=[_regroup_scratch(PROJ_ROWS, N)] + ride.scratch,
        input_output_aliases=ride.aliases,
        compiler_params=_params("arbitrary"),
    )(a, w_t, bias, *ride.args)


ROWS = 256
MIX_ROWS = 128


def _rstd(xv):
    return lax.rsqrt(jnp.mean(xv * xv, axis=-1, keepdims=True) + EPS)


def _norm_fwd(x, g, name, exchange=None):
    T, D = x.shape
    ride = _Ride(exchange, 2, 1, 0)

    def kern(*refs):
        (x_ref, g_ref), (h_ref,), _, exrefs = ride.split(refs)

        def compute():
            xv = x_ref[...]
            h_ref[...] = ((xv * _rstd(xv)) * g_ref[...]).astype(h_ref.dtype)

        ride.around(pl.program_id(0), T // ROWS, exrefs, compute)

    row = pl.BlockSpec((ROWS, D), lambda i: (i, 0))
    return pl.pallas_call(
        kern, name=name, grid=(T // ROWS,),
        in_specs=[row, pl.BlockSpec((1, D), lambda i: (0, 0))] + ride.in_specs,
        out_specs=[row] + ride.out_specs,
        out_shape=[jax.ShapeDtypeStruct((T, D), BF16)] + ride.out_shapes,
        scratch_shapes=ride.scratch, input_output_aliases=ride.aliases,
        compiler_params=_params("arbitrary"),
    )(x, g, *ride.args)


def _norm_bwd(dh, x, g, res, name, then_w_t=None):
    T, D = x.shape
    rows = PROJ_ROWS if then_w_t is not None else ROWS

    def kern(dh_ref, x_ref, g_ref, res_ref, *rest):
        if then_w_t is None:
            dx_ref, dg_ref = rest
        else:
            w_ref, dx_ref, dg_ref, dxb_ref, y_ref = rest

        @pl.when(pl.program_id(0) == 0)
        def _():
            dg_ref[...] = jnp.zeros_like(dg_ref)

        xv = x_ref[...]
        r = _rstd(xv)
        xn = xv * r
        dhv = dh_ref[...].astype(F32)
        dg_ref[...] += jnp.sum(dhv * xn, axis=0, keepdims=True)
        t = dhv * g_ref[...]
        dx = res_ref[...] + r * (t - xn * jnp.mean(t * xn, axis=-1, keepdims=True))
        dx_ref[...] = dx
        if then_w_t is not None:
            dxb = dx.astype(BF16)
            dxb_ref[...] = dxb
            y_ref[...] = _dot(dxb, w_ref[...], NT)

    row = pl.BlockSpec((rows, D), lambda i: (i, 0))
    vec = pl.BlockSpec((1, D), lambda i: (0, 0))
    in_specs, args = [row, row, vec, row], [dh, x, g, res]
    out_specs = [row, vec]
    out_shape = [jax.ShapeDtypeStruct((T, D), F32), jax.ShapeDtypeStruct((1, D), F32)]
    if then_w_t is not None:
        N = then_w_t.shape[0]
        in_specs.append(pl.BlockSpec((N, D), lambda i: (0, 0)))
        args.append(then_w_t)
        out_specs += [row, pl.BlockSpec((rows, N), lambda i: (i, 0))]
        out_shape += [jax.ShapeDtypeStruct((T, D), BF16), jax.ShapeDtypeStruct((T, N), F32)]
    return pl.pallas_call(
        kern, name=name, grid=(T // rows,), in_specs=in_specs, out_specs=out_specs, out_shape=out_shape,
        compiler_params=_params("arbitrary"),
    )(*args)


def _loss_head(x3, tgt, g):
    T, D = x3.shape

    def kern(x_ref, t_ref, g_ref, dx_ref, dxb_ref, dg_ref, loss_ref):
        @pl.when(pl.program_id(0) == 0)
        def _():
            dg_ref[...] = jnp.zeros_like(dg_ref)
            loss_ref[...] = jnp.zeros_like(loss_ref)

        xv = x_ref[...]
        gv = g_ref[...]
        r = _rstd(xv)
        xn = xv * r
        err = xn * gv - t_ref[...]
        per_tok = jnp.mean(err * err, axis=-1, keepdims=True)
        loss_ref[...] += 0.5 * jnp.sum(per_tok, axis=0, keepdims=True)
        dy = err * (1.0 / D)
        dg_ref[...] += jnp.sum(dy * xn, axis=0, keepdims=True)
        t = dy * gv
        dx = r * (t - xn * jnp.mean(t * xn, axis=-1, keepdims=True))
        dx_ref[...] = dx
        dxb_ref[...] = dx.astype(BF16)

    row = pl.BlockSpec((ROWS, D), lambda i: (i, 0))
    vec = pl.BlockSpec((1, D), lambda i: (0, 0))
    return pl.pallas_call(
        kern, name="loss_head", grid=(T // ROWS,),
        in_specs=[row, row, vec],
        out_specs=[row, row, vec, pl.BlockSpec((1, 128), lambda i: (0, 0))],
        out_shape=[jax.ShapeDtypeStruct((T, D), F32), jax.ShapeDtypeStruct((T, D), BF16),
                   jax.ShapeDtypeStruct((1, D), F32), jax.ShapeDtypeStruct((1, 128), F32)],
        compiler_params=_params("arbitrary"),
    )(x3, tgt, g)


def _spread_matrix():
    head_of_lane = np.arange(N_HEADS * HEAD_DIM) // HEAD_DIM
    return jnp.asarray(np.arange(N_HEADS)[:, None] == head_of_lane[None, :], dtype=BF16)


def _pieces(v, n):
    out = []
    for _ in range(n):
        piece = v.astype(BF16)
        out.append(piece)
        v = v - piece.astype(F32)
    return out


def _spread(v, spread):
    return sum(_dot(p, spread, NN) for p in _pieces(v, 2))


def _spread_weights(w1, w2, spread):
    s1, s2 = _spread(w1, spread), _spread(w2, spread)
    return s1, s2, 1.0 - s1 - s2


def _head_sums(v, spread):
    return sum(_dot(p, spread, NT) for p in _pieces(v, 2))


def _branch_weights(l1, l2, l3):
    lm = jnp.maximum(jnp.maximum(l1, l2), l3)
    e1, e2, e3 = jnp.exp(l1 - lm), jnp.exp(l2 - lm), jnp.exp(l3 - lm)
    inv = 1.0 / (e1 + e2 + e3)
    return e1 * inv, e2 * inv, e3 * inv


def _regroup_scratch(rows, width):
    return pltpu.VMEM((width // LANES, rows, LANES), F32)


def _to_token_order(view_ref, scr, dil):
    if dil == 1:
        return view_ref[...].astype(F32)
    n_l, w = view_ref.shape[0], view_ref.shape[1] // dil
    for r in range(dil):
        for cb in range(w // LANES):
            scr[cb, pl.ds(r, n_l, stride=dil), :] = view_ref[:, r * w + cb * LANES:r * w + (cb + 1) * LANES].astype(F32)
    return jnp.concatenate([scr[cb] for cb in range(w // LANES)], axis=1)


def _to_class_order(val, scr, view_ref, dil):
    if dil == 1:
        view_ref[...] = val.astype(view_ref.dtype)
        return
    n, w = val.shape
    for cb in range(w // LANES):
        scr[cb] = val[:, cb * LANES:(cb + 1) * LANES]
    for r in range(dil):
        for cb in range(w // LANES):
            view_ref[:, r * w + cb * LANES:r * w + (cb + 1) * LANES] = (
                scr[cb, pl.ds(r, n // dil, stride=dil), :].astype(view_ref.dtype))


def _view_spec(rows, width, dil):
    return pl.BlockSpec((rows // dil, dil * width), lambda i: (i, 0))


def _mix_fwd(oa, obs, lbs, ga, gb, dils):
    T, W = oa.shape

    def kern(oa_ref, o1, o2, o3, l1, l2, l3, ga_ref, gb_ref, sp_ref, mix_ref, *scr):
        sp = sp_ref[...]
        w1, w2, w3 = _branch_weights(l1[...], l2[...], l3[...])
        on = [_to_token_order(o, s, d) for o, s, d in zip((o1, o2, o3), scr, dils)]
        s1, s2, s3 = _spread_weights(w1, w2, sp)
        ob = s1 * on[0] + s2 * on[1] + s3 * on[2]
        oav = oa_ref[...]
        mix_ref[:, :W] = ((oav * _rstd(oav)) * ga_ref[...]).astype(BF16)
        mix_ref[:, W:] = ((ob * _rstd(ob)) * gb_ref[...]).astype(BF16)

    row = pl.BlockSpec((MIX_ROWS, W), lambda i: (i, 0))
    per_head = pl.BlockSpec((MIX_ROWS, N_HEADS), lambda i: (i, 0))
    vec = pl.BlockSpec((1, W), lambda i: (0, 0))
    return pl.pallas_call(
        kern, name="mix_fwd", grid=(T // MIX_ROWS,),
        in_specs=[row] + [_view_spec(MIX_ROWS, W, d) for d in dils] + [per_head] * 3
        + [vec, vec, pl.BlockSpec((N_HEADS, W), lambda i: (0, 0))],
        out_specs=pl.BlockSpec((MIX_ROWS, 2 * W), lambda i: (i, 0)),
        out_shape=jax.ShapeDtypeStruct((T, 2 * W), BF16),
        scratch_shapes=[_regroup_scratch(MIX_ROWS, W)] * 3,
        compiler_params=_params("parallel"),
    )(oa, *obs, *lbs, ga, gb, _spread_matrix())


def _mix_bwd(dmix, oa, obs, lbs, ga, gb, dils, exchange=None):
    T, W = oa.shape
    ride = _Ride(exchange, 11, 10, 3)

    def kern(*refs):
        ins, outs, scr, exrefs = ride.split(refs)
        ride.around(pl.program_id(0), T // MIX_ROWS, exrefs, lambda: compute(*ins, *outs, *scr))

    def compute(dm_ref, oa_ref, o1, o2, o3, l1, l2, l3, ga_ref, gb_ref, sp_ref,
                doa_ref, da_ref, do1, do2, do3, d1, d2, d3, dga_ref, dgb_ref, *scr):
        @pl.when(pl.program_id(0) == 0)
        def _():
            dga_ref[...] = jnp.zeros_like(dga_ref)
            dgb_ref[...] = jnp.zeros_like(dgb_ref)

        sp = sp_ref[...]
        oav = oa_ref[...]
        r = _rstd(oav)
        on = oav * r
        dy = dm_ref[:, :W]
        dga_ref[...] += jnp.sum(dy * on, axis=0, keepdims=True)
        t = dy * ga_ref[...]
        doa = r * (t - on * jnp.mean(t * on, axis=-1, keepdims=True))
        doa_ref[...] = doa.astype(BF16)
        da_ref[...] = _head_sums(doa * oav, sp)
        w1, w2, w3 = _branch_weights(l1[...], l2[...], l3[...])
        s1, s2, s3 = _spread_weights(w1, w2, sp)
        on = [_to_token_order(o, sc, d) for o, sc, d in zip((o1, o2, o3), scr, dils)]
        ob = s1 * on[0] + s2 * on[1] + s3 * on[2]
        r = _rstd(ob)
        on = ob * r
        dy = dm_ref[:, W:]
        dgb_ref[...] += jnp.sum(dy * on, axis=0, keepdims=True)
        t = dy * gb_ref[...]
        dob = r * (t - on * jnp.mean(t * on, axis=-1, keepdims=True))
        c = _head_sums(dob * ob, sp)
        for do_ref, sn, sc, d in zip((do1, do2, do3), (s1, s2, s3), scr, dils):
            _to_class_order(sn * dob, sc, do_ref, d)
        d1[...] = w1 * c
        d2[...] = w2 * c
        d3[...] = w3 * c

    row = pl.BlockSpec((MIX_ROWS, W), lambda i: (i, 0))
    per_head = pl.BlockSpec((MIX_ROWS, N_HEADS), lambda i: (i, 0))
    vec = pl.BlockSpec((1, W), lambda i: (0, 0))
    bf = jax.ShapeDtypeStruct((T, W), BF16)
    ph = jax.ShapeDtypeStruct((T, N_HEADS), F32)
    vv = jax.ShapeDtypeStruct((1, W), F32)
    views = [_view_spec(MIX_ROWS, W, d) for d in dils]
    return pl.pallas_call(
        kern, name="mix_bwd", grid=(T // MIX_ROWS,),
        in_specs=[pl.BlockSpec((MIX_ROWS, 2 * W), lambda i: (i, 0)), row] + views + [per_head] * 3 + [vec, vec,
                  pl.BlockSpec((N_HEADS, W), lambda i: (0, 0))] + ride.in_specs,
        out_specs=[row, per_head] + views + [per_head, per_head, per_head, vec, vec] + ride.out_specs,
        out_shape=[bf, ph] + [jax.ShapeDtypeStruct(o.shape, F32) for o in obs] + [ph, ph, ph, vv, vv]
        + ride.out_shapes,
        scratch_shapes=[_regroup_scratch(MIX_ROWS, W)] * 3 + ride.scratch,
        input_output_aliases=ride.aliases,
        compiler_params=_params("arbitrary"),
    )(dmix, oa, *obs, *lbs, ga, gb, _spread_matrix(), *ride.args)


def _alibi_slopes(n):
    return np.asarray(2.0 ** (-8.0 * (np.arange(n) + 1) / n)).astype(np.float32)


def _band_bias(max_steps, step_dist):
    qi = np.arange(BLOCK)[None, :]
    kj = np.arange(BLOCK)[:, None]
    slopes = _alibi_slopes(N_HEADS)
    halves = []
    for steps in (qi + BLOCK - kj, qi - kj):
        valid = (steps >= 0) & (steps <= max_steps)
        alibi = slopes[:, None, None] * (step_dist * steps).astype(np.float32)[None]
        halves.append(np.where(valid[None], -alibi, np.float32(NEG_INF)).astype(np.float32))
    per_head = np.concatenate(halves, axis=1)
    return jnp.asarray(np.concatenate([per_head[0::2], per_head[1::2]], axis=2))


class _AttnLayout:
    def __init__(self, dil, kv_heads, q_stride, q_off, k_stride, k_off, v_off):
        self.dil = dil
        self.kv_heads = kv_heads
        self.kw = kv_heads * HEAD_DIM
        self.rep = N_HEADS // kv_heads
        self.q_col = lambda r: r * q_stride + q_off
        self.k_col = lambda r: r * k_stride + k_off
        self.v_col = lambda r: r * k_stride + v_off


QW = N_HEADS * HEAD_DIM
LANES = 128


PAIRS = N_HEADS // 2


def _pair_cols(pair):
    return slice(pair * LANES, (pair + 1) * LANES)


def _first_head_lanes(shape):
    return lax.broadcasted_iota(jnp.int32, shape, 1) < HEAD_DIM


def _split_heads(pair):
    first = _first_head_lanes(pair.shape)
    zero = jnp.zeros_like(pair)
    return jnp.concatenate([jnp.where(first, pair, zero), jnp.where(first, zero, pair)], axis=0)


def _kv_pair(ref, pair, rep):
    if rep == 1:
        return ref[:, _pair_cols(pair)]
    blk = ref[...].astype(F32)
    other = pltpu.roll(blk, HEAD_DIM, 1)
    first = _first_head_lanes(blk.shape)
    both = jnp.where(first, blk, other) if (2 * pair // rep) % 2 == 0 else jnp.where(first, other, blk)
    return both.astype(ref.dtype)


def _paired_kv(prev_ref, cur_ref, rep, transposed=False):
    memo = {}

    def get(pair):
        key = pair if rep == 1 else 2 * pair // rep
        if key not in memo:
            blocks = [_kv_pair(ref, pair, rep) for ref in (prev_ref, cur_ref)]
            memo[key] = jnp.concatenate([b.T for b in blocks], axis=1) if transposed else jnp.concatenate(blocks, axis=0)
        return memo[key]

    return get


def _attn_fwd(proj, bias, sinks, lay, name, exchange=None):
    L = proj.shape[0]
    nb = L // BLOCK
    kw, rep = lay.kw, lay.rep
    use_sinks = sinks is not None
    scale = HEAD_DIM ** -0.5
    ride = _Ride(exchange, 7 if use_sinks else 6, 2, 2)

    def kern(*refs):
        ins, (o_ref, l_ref), (sc_ref, pr_ref), exrefs = ride.split(refs)
        q_ref, kc_ref, kp_ref, vc_ref, vp_ref, b_ref = ins[:6]
        s_ref = ins[6] if use_sinks else None
        r, i = pl.program_id(0), pl.program_id(1)
        first = i == 0
        ride.around(r * nb + i, lay.dil * nb, exrefs,
                    lambda: compute(q_ref, kc_ref, kp_ref, vc_ref, vp_ref, b_ref, s_ref, o_ref, l_ref, first,
                                    sc_ref, pr_ref))

    def compute(q_ref, kc_ref, kp_ref, vc_ref, vp_ref, b_ref, s_ref, o_ref, l_ref, first, sc_ref, pr_ref):
        keys, values_t = _paired_kv(kp_ref, kc_ref, rep), _paired_kv(vp_ref, vc_ref, rep, transposed=True)
        for pair in range(PAIRS):
            qs = _split_heads(q_ref[:, _pair_cols(pair)])
            s = _dot(keys(pair), qs, NT) * scale + b_ref[pair]
            sc_ref[pair, :BLOCK] = jnp.where(first, NEG_INF, s[:BLOCK])
            sc_ref[pair, BLOCK:] = s[BLOCK:]
        inv = []
        for h in range(N_HEADS):
            cols = slice(h % 2 * BLOCK, (h % 2 + 1) * BLOCK)
            s = sc_ref[h // 2, :, cols]
            m = jnp.max(s, axis=0, keepdims=True)
            if use_sinks:
                sink = s_ref[:, h:h + 1]
                m = jnp.maximum(m, sink)
            p = jnp.exp(s - m)
            denom = jnp.sum(p, axis=0, keepdims=True)
            if use_sinks:
                denom = denom + jnp.exp(sink - m)
            pr_ref[h // 2, :, cols] = p.astype(BF16)
            l_ref[h:h + 1, :] = m + jnp.log(denom)
            inv.append(1.0 / denom)
        for pair in range(PAIRS):
            both = _dot(values_t(pair), pr_ref[pair], NN)
            o_t = jnp.concatenate([both[:HEAD_DIM, :BLOCK] * inv[2 * pair], both[HEAD_DIM:, BLOCK:] * inv[2 * pair + 1]],
                                  axis=0)
            o_ref[:, _pair_cols(pair)] = o_t.T

    prev = lambda i: jnp.maximum(i - 1, 0)
    in_specs = [
        pl.BlockSpec((BLOCK, QW), lambda r, i: (i, lay.q_col(r))),
        pl.BlockSpec((BLOCK, kw), lambda r, i: (i, lay.k_col(r))),
        pl.BlockSpec((BLOCK, kw), lambda r, i: (prev(i), lay.k_col(r))),
        pl.BlockSpec((BLOCK, kw), lambda r, i: (i, lay.v_col(r))),
        pl.BlockSpec((BLOCK, kw), lambda r, i: (prev(i), lay.v_col(r))),
        pl.BlockSpec((PAIRS, 2 * BLOCK, 2 * BLOCK), lambda r, i: (0, 0, 0)),
    ]
    args = [proj, proj, proj, proj, proj, bias]
    if use_sinks:
        in_specs.append(pl.BlockSpec((1, N_HEADS), lambda r, i: (0, 0)))
        args.append(sinks)
    out_specs = [pl.BlockSpec((BLOCK, QW), lambda r, i: (i, r)),
                 pl.BlockSpec((None, N_HEADS, BLOCK), lambda r, i: (r, 0, i))]
    out_shape = [jax.ShapeDtypeStruct((L, lay.dil * QW), F32), jax.ShapeDtypeStruct((lay.dil, N_HEADS, L), F32)]
    return pl.pallas_call(
        kern, name=name, grid=(lay.dil, nb),
        in_specs=in_specs + ride.in_specs, out_specs=out_specs + ride.out_specs,
        out_shape=out_shape + ride.out_shapes,
        scratch_shapes=[pltpu.VMEM((PAIRS, 2 * BLOCK, 2 * BLOCK), dt) for dt in (F32, BF16)] + ride.scratch,
        input_output_aliases=ride.aliases,
        compiler_params=_params("arbitrary", "arbitrary"),
    )(*args, *ride.args)


def _attn_bwd(proj, do, lse, dd, bias, sinks, lay, name, exchange=None):
    L = proj.shape[0]
    nb = L // BLOCK
    kw, rep = lay.kw, lay.rep
    assert rep == 1 or lay.kv_heads == 2, "grouped queries: the two kv heads fill one 128-lane block"
    use_sinks = sinks is not None
    scale = HEAD_DIM ** -0.5
    ride = _Ride(exchange, 10 if use_sinks else 9, 4 if use_sinks else 3, 6)

    def kern(*refs):
        ins, outs, (ck_ref, cv_ref, *staged), exrefs = ride.split(refs)
        q_ref, kc_ref, kp_ref, vc_ref, vp_ref, do_ref, l_ref, d_ref, b_ref = ins[:9]
        s_ref = ins[9] if use_sinks else None
        dq_ref, dk_ref, dv_ref = outs[:3]
        ds_ref = outs[3] if use_sinks else None
        r = pl.program_id(0)
        i = pl.program_id(1)
        ride.around(r * (nb + 1) + i, lay.dil * (nb + 1), exrefs,
                    lambda: compute(q_ref, kc_ref, kp_ref, vc_ref, vp_ref, do_ref, l_ref, d_ref, b_ref, s_ref,
                                    dq_ref, dk_ref, dv_ref, ds_ref, ck_ref, cv_ref, r, i, *staged))

    def compute(q_ref, kc_ref, kp_ref, vc_ref, vp_ref, do_ref, l_ref, d_ref, b_ref, s_ref,
                dq_ref, dk_ref, dv_ref, ds_ref, ck_ref, cv_ref, r, i, sc_ref, dp_ref, pr_ref, dsc_ref):
        first = i == 0

        @pl.when(first)
        def _():
            ck_ref[...] = jnp.zeros_like(ck_ref)
            cv_ref[...] = jnp.zeros_like(cv_ref)

        if use_sinks:
            @pl.when(first & (r == 0))
            def _():
                ds_ref[...] = jnp.zeros_like(ds_ref)

        @pl.when(i < nb)
        def _():
            keys, values = _paired_kv(kp_ref, kc_ref, rep), _paired_kv(vp_ref, vc_ref, rep)
            keys_t = _paired_kv(kp_ref, kc_ref, rep, transposed=True)
            for pair in range(PAIRS):
                qs = _split_heads(q_ref[:, _pair_cols(pair)])
                dos = _split_heads(do_ref[:, _pair_cols(pair)].astype(BF16))
                s = _dot(keys(pair), qs, NT) * scale + b_ref[pair]
                sc_ref[pair, :BLOCK] = jnp.where(first, NEG_INF, s[:BLOCK])
                sc_ref[pair, BLOCK:] = s[BLOCK:]
                dp_ref[pair] = _dot(values(pair), dos, NT)
            for h in range(N_HEADS):
                cols = slice(h % 2 * BLOCK, (h % 2 + 1) * BLOCK)
                lrow = l_ref[h:h + 1, :]
                drow = d_ref[h:h + 1, :]
                p = jnp.exp(sc_ref[h // 2, :, cols] - lrow)
                pr_ref[h // 2, :, cols] = p.astype(BF16)
                dsc_ref[h // 2, :, cols] = (p * (dp_ref[h // 2, :, cols] - drow) * scale).astype(BF16)
                if use_sinks:
                    ds_ref[h:h + 1, :] += -(jnp.exp(s_ref[:, h:h + 1] - lrow) * drow)
            grouped = {}
            for pair in range(PAIRS):
                cols = _pair_cols(pair)
                qs = _split_heads(q_ref[:, cols])
                dos = _split_heads(do_ref[:, cols].astype(BF16))
                ds = dsc_ref[pair]
                both = _dot(keys_t(pair), ds, NN)
                dq_t = jnp.concatenate([both[:HEAD_DIM, :BLOCK], both[HEAD_DIM:, BLOCK:]], axis=0)
                dq_ref[:, cols] = dq_t.T.astype(dq_ref.dtype)
                dk = _dot(ds, qs, NN)
                dv = _dot(pr_ref[pair], dos, NN)
                if rep == 1:
                    dk_ref[:, cols] = (ck_ref[:, cols] + dk[:BLOCK]).astype(dk_ref.dtype)
                    dv_ref[:, cols] = (cv_ref[:, cols] + dv[:BLOCK]).astype(dv_ref.dtype)
                    ck_ref[:, cols] = dk[BLOCK:]
                    cv_ref[:, cols] = dv[BLOCK:]
                else:
                    g = 2 * pair // rep
                    grouped[g] = (dk, dv) if g not in grouped else (grouped[g][0] + dk, grouped[g][1] + dv)
            if rep > 1:
                fold = lambda t: t + pltpu.roll(t, HEAD_DIM, 1)
                first_half = _first_head_lanes((2 * BLOCK, LANES))
                dk = jnp.where(first_half, fold(grouped[0][0]), fold(grouped[1][0]))
                dv = jnp.where(first_half, fold(grouped[0][1]), fold(grouped[1][1]))
                dk_ref[...] = (ck_ref[...] + dk[:BLOCK]).astype(dk_ref.dtype)
                dv_ref[...] = (cv_ref[...] + dv[:BLOCK]).astype(dv_ref.dtype)
                ck_ref[...] = dk[BLOCK:]
                cv_ref[...] = dv[BLOCK:]

        @pl.when(i == nb)
        def _():
            dk_ref[...] = ck_ref[...].astype(dk_ref.dtype)
            dv_ref[...] = cv_ref[...].astype(dv_ref.dtype)
            if use_sinks:
                @pl.when(r == lay.dil - 1)
                def _():
                    ds_ref[...] = jnp.broadcast_to(jnp.sum(ds_ref[...], axis=1, keepdims=True), ds_ref.shape)

    cur = lambda i: jnp.minimum(i, nb - 1)
    prev = lambda i: jnp.maximum(jnp.minimum(i, nb - 1) - 1, 0)
    done = lambda i: jnp.maximum(i - 1, 0)
    qspec = lambda col: pl.BlockSpec((BLOCK, QW), lambda r, i: (cur(i), col(r)))
    per_head = pl.BlockSpec((None, N_HEADS, BLOCK), lambda r, i: (r, 0, cur(i)))
    in_specs = [
        qspec(lay.q_col),
        pl.BlockSpec((BLOCK, kw), lambda r, i: (cur(i), lay.k_col(r))),
        pl.BlockSpec((BLOCK, kw), lambda r, i: (prev(i), lay.k_col(r))),
        pl.BlockSpec((BLOCK, kw), lambda r, i: (cur(i), lay.v_col(r))),
        pl.BlockSpec((BLOCK, kw), lambda r, i: (prev(i), lay.v_col(r))),
        qspec(lambda r: r), per_head, per_head,
        pl.BlockSpec((PAIRS, 2 * BLOCK, 2 * BLOCK), lambda r, i: (0, 0, 0)),
    ]
    args = [proj, proj, proj, proj, proj, do, lse, dd, bias]
    out_specs = [
        qspec(lambda r: r),
        pl.BlockSpec((BLOCK, kw), lambda r, i: (done(i), r)),
        pl.BlockSpec((BLOCK, kw), lambda r, i: (done(i), r)),
    ]
    dkv_shape = jax.ShapeDtypeStruct((L, lay.dil * kw), BF16)
    out_shape = [jax.ShapeDtypeStruct((L, lay.dil * QW), BF16), dkv_shape, dkv_shape]
    if use_sinks:
        in_specs.append(pl.BlockSpec((1, N_HEADS), lambda r, i: (0, 0)))
        args.append(sinks)
        out_specs.append(pl.BlockSpec((N_HEADS, LANES), lambda r, i: (0, 0)))
        out_shape.append(jax.ShapeDtypeStruct((N_HEADS, LANES), F32))
    return pl.pallas_call(
        kern, name=name, grid=(lay.dil, nb + 1),
        in_specs=in_specs + ride.in_specs, out_specs=out_specs + ride.out_specs,
        out_shape=out_shape + ride.out_shapes,
        scratch_shapes=[pltpu.VMEM((BLOCK, kw), F32), pltpu.VMEM((BLOCK, kw), F32)]
        + [pltpu.VMEM((PAIRS, 2 * BLOCK, 2 * BLOCK), dt) for dt in (F32, F32, BF16, BF16)] + ride.scratch,
        input_output_aliases=ride.aliases,
        compiler_params=_params("arbitrary", "arbitrary"),
    )(*args, *ride.args)


def _assemble(groups, name, dils=(1,)):
    T = groups[0][0].shape[0] * dils[0]
    widths = [g[0].shape[1] // dils[0] for g in groups]
    total = sum(widths)
    flat = [a for g in groups for a in g]
    member_dils = [d for g in groups for d in dils[:len(g)]]

    def kern(*refs):
        ins = refs[:len(flat)]
        out_ref, cs_ref = refs[len(flat):len(flat) + 2]
        scr = refs[len(flat) + 2:]

        @pl.when(pl.program_id(0) == 0)
        def _():
            cs_ref[...] = jnp.zeros_like(cs_ref)

        pos = off = 0
        for g, w in zip(groups, widths):
            acc = _to_token_order(ins[pos], None, dils[0])
            for j in range(1, len(g)):
                acc = acc + _to_token_order(ins[pos + j], scr[j - 1], dils[j])
            pos += len(g)
            out_ref[:, off:off + w] = acc.astype(BF16)
            cs_ref[:, off:off + w] += jnp.sum(acc, axis=0, keepdims=True)
            off += w

    return pl.pallas_call(
        kern, name=name, grid=(T // ROWS,),
        in_specs=[_view_spec(ROWS, a.shape[1] // d, d) for a, d in zip(flat, member_dils)],
        out_specs=[pl.BlockSpec((ROWS, total), lambda i: (i, 0)), pl.BlockSpec((1, total), lambda i: (0, 0))],
        out_shape=[jax.ShapeDtypeStruct((T, total), BF16), jax.ShapeDtypeStruct((1, total), F32)],
        scratch_shapes=[_regroup_scratch(ROWS, max(widths))] * (len(dils) - 1),
        compiler_params=_params("arbitrary"),
    )(*flat)


def _adamw(w, g, m, v, name):
    R, C = w.shape
    rows = min(R, ROWS)
    assert R % rows == 0

    def kern(w_ref, g_ref, m_ref, v_ref, d_ref, nm_ref, nv_ref):
        gv = g_ref[...]
        mn = ADAM_B1 * m_ref[...] + (1.0 - ADAM_B1) * gv
        vn = ADAM_B2 * v_ref[...] + (1.0 - ADAM_B2) * jnp.square(gv)
        m_hat = mn / (1.0 - ADAM_B1 ** ADAM_STEP)
        v_hat = vn / (1.0 - ADAM_B2 ** ADAM_STEP)
        d_ref[...] = -ADAM_LR * (m_hat / (jnp.sqrt(v_hat) + ADAM_EPS) + ADAM_WD * w_ref[...])
        nm_ref[...] = mn
        nv_ref[...] = vn

    blk = pl.BlockSpec((rows, C), lambda i: (i, 0))
    shp = jax.ShapeDtypeStruct((R, C), F32)
    return pl.pallas_call(
        kern, name=name, grid=(R // rows,),
        in_specs=[blk] * 4, out_specs=[blk] * 3, out_shape=[shp] * 3,
        compiler_params=_params("parallel"),
    )(w, g, m, v)


def _sum_slots(slots, name):
    n, R, C = slots.shape
    SUM_ROWS = next(rows for rows in (128, 64, 32, 16) if R % rows == 0)

    def kern(s_ref, o_ref):
        acc = s_ref[0].astype(F32)
        for k in range(1, n):
            acc = acc + s_ref[k].astype(F32)
        o_ref[...] = acc

    return pl.pallas_call(
        kern, name=name, grid=(R // SUM_ROWS,),
        in_specs=[pl.BlockSpec((n, SUM_ROWS, C), lambda i: (0, i, 0))],
        out_specs=pl.BlockSpec((SUM_ROWS, C), lambda i: (i, 0)),
        out_shape=jax.ShapeDtypeStruct((R, C), F32),
        compiler_params=_params("parallel"),
    )(slots)


def _place():
    return lax.axis_index("x"), lax.axis_index("y"), lax.axis_index("c")


def _index(p):
    return 4 * p[0] + 2 * p[1] + p[2]


FLIPS = [(fx, fy, fc) for fx in (0, 1) for fy in (0, 1) for fc in (0, 1)][1:]


def _peer(me, flip):
    return tuple(1 - a if f else a for a, f in zip(me, flip))


def _gather_rows(shards, part=(0, 1), into=None):
    nw = len(shards)

    def plan(ins, outs, send_sems, recv_sems):
        x, y, c = me = _place()
        sibling = (x, y, 1 - c)
        chips = [(1 - x, y), (x, 1 - y), (1 - x, 1 - y)]

        def span(w):
            cnt = ins[w].shape[0] // part[1]
            return part[0] * cnt, cnt

        def rows(w, p):
            lo, cnt = span(w)
            return outs[w].at[pl.ds(_index(p) * ins[w].shape[0] + lo, cnt), :]

        def own(w):
            lo, cnt = span(w)
            return ins[w].at[pl.ds(lo, cnt), :]

        def copy(w, k, block, to):
            return pltpu.make_async_remote_copy(
                src_ref=own(w) if block is me else rows(w, block), dst_ref=rows(w, block),
                send_sem=send_sems.at[7 * w + k], recv_sem=recv_sems.at[7 * w + k],
                device_id=to, device_id_type=MESH)

        return me, sibling, chips, c, rows, own, copy

    def copies(ins, outs, send_sems, recv_sems, local_sems):
        me, sibling, chips, c, rows, own, copy = plan(ins, outs, send_sems, recv_sems)
        local = [pltpu.make_async_copy(own(w), rows(w, me), local_sems.at[w]) for w in range(nw)]
        sends, recvs = [], []
        for w in range(nw):
            sends.append(copy(w, 0, me, sibling))
            sends += [copy(w, 1 + j, me, (*chip, c)) for j, chip in enumerate(chips)]
            recvs.append(copy(w, 0, sibling, me))
            recvs += [copy(w, 4 + j, (*chip, 1 - c), me) for j, chip in enumerate(chips)]
        return local, sends, recvs

    def relay(ins, outs, send_sems, recv_sems, local_sems):
        me, sibling, chips, c, rows, own, copy = plan(ins, outs, send_sems, recv_sems)
        arrived = [copy(w, 1 + j, (*chip, c), me) for w in range(nw) for j, chip in enumerate(chips)]
        onward = [copy(w, 4 + j, (*chip, c), sibling) for w in range(nw) for j, chip in enumerate(chips)]
        return arrived, onward

    shapes = [jax.ShapeDtypeStruct((N_DEV * s.shape[0], s.shape[1]), s.dtype) for s in shards]
    aliases = {nw + w: w for w in range(nw)} if into else None
    return _Exchange(shards + (into or []), shapes, 7 * nw, nw, copies, aliases=aliases, relay=relay)


def _scatter_rows(parts, part=(0, 1)):
    nw = len(parts)

    def copies(ins, outs, send_sems, recv_sems, local_sems):
        me = _place()

        def src(w, owner):
            n = ins[w].shape[0] // N_DEV
            cnt = n // part[1]
            return ins[w].at[pl.ds(_index(owner) * n + part[0] * cnt, cnt), :]

        def copy(k, w, owner, sender, to):
            return pltpu.make_async_remote_copy(
                src_ref=src(w, owner), dst_ref=outs[w].at[_index(sender)],
                send_sem=send_sems.at[nw * k + w], recv_sem=recv_sems.at[nw * k + w],
                device_id=to, device_id_type=MESH)

        local = [pltpu.make_async_copy(src(w, me), outs[w].at[_index(me)], local_sems.at[w]) for w in range(nw)]
        peers = [_peer(me, flip) for flip in FLIPS]
        sends = [copy(k, w, peer, me, peer) for k, peer in enumerate(peers) for w in range(nw)]
        recvs = [copy(k, w, me, peer, me) for k, peer in enumerate(peers) for w in range(nw)]
        return local, sends, recvs

    shapes = [jax.ShapeDtypeStruct((N_DEV, p.shape[0] // N_DEV // part[1], p.shape[1]), p.dtype) for p in parts]
    return _Exchange(parts, shapes, 7 * nw, nw, copies)


def _sum_over_devices(v):
    shape = v.shape

    def body(v_ref, sum_ref, all_ref, send_sems, recv_sems):
        me = _place()
        all_ref[_index(me)] = v_ref[...]
        sends = []
        for k, flip in enumerate(FLIPS):
            peer = _peer(me, flip)
            sends.append(pltpu.make_async_remote_copy(
                src_ref=v_ref, dst_ref=all_ref.at[_index(me)],
                send_sem=send_sems.at[k], recv_sem=recv_sems.at[k], device_id=peer, device_id_type=MESH))
            sends[-1].start()
        for k, flip in enumerate(FLIPS):
            peer = _peer(me, flip)
            pltpu.make_async_remote_copy(
                src_ref=v_ref, dst_ref=all_ref.at[_index(peer)],
                send_sem=send_sems.at[k], recv_sem=recv_sems.at[k], device_id=peer, device_id_type=MESH).wait_recv()
        for cp in sends:
            cp.wait_send()
        acc = all_ref[0]
        for s in range(1, N_DEV):
            acc = acc + all_ref[s]
        sum_ref[...] = acc

    vmem = pl.BlockSpec(memory_space=pltpu.VMEM)
    return pl.pallas_call(
        body, name="sum_small_grads",
        in_specs=[vmem], out_specs=[vmem, vmem],
        out_shape=[jax.ShapeDtypeStruct(shape, F32), jax.ShapeDtypeStruct((N_DEV,) + shape, F32)],
        scratch_shapes=[pltpu.SemaphoreType.DMA((7,)), pltpu.SemaphoreType.DMA((7,))],
    )(v)[0]


SMALL_ROWS = 8


def _pack_small(vectors):
    padded = []
    for vec in vectors:
        vec = vec.reshape(-1)
        padded.append(jnp.pad(vec, (0, -vec.shape[0] % 128)))
    flat = jnp.concatenate(padded)
    flat = jnp.pad(flat, (0, -flat.shape[0] % (SMALL_ROWS * 128)))
    return flat.reshape(SMALL_ROWS, -1)


def _unpack_small(packed, shapes):
    flat = packed.reshape(-1)
    out, off = [], 0
    for shp in shapes:
        n = int(np.prod(shp))
        out.append(flat[off:off + n].reshape(shp))
        off += n + (-n % 128)
    return out


def kernel(x, g_attn, w_in, b_in, sinks_a, g_out_a, g_out_b, w_out, g_mlp, w_1, w_2, g_final, loss_target, m_g_attn, m_w_in, m_b_in, m_sinks_a, m_g_out_a, m_g_out_b, m_w_out, m_g_mlp, m_w_1, m_w_2, m_g_final, v_g_attn, v_w_in, v_b_in, v_sinks_a, v_g_out_a, v_g_out_b, v_w_out, v_g_mlp, v_w_1, v_w_2, v_g_final):
    xs, tgt = x[0], loss_target[0]
    T, D = xs.shape
    n_a = QW + 2 * KV_HEADS_A * HEAD_DIM
    g_fin = g_final.reshape(1, D)

    shards = [w_in[0].T.astype(BF16), w_out[0].astype(BF16), w_1[0].T.astype(BF16), w_2[0].astype(BF16)]
    ident = lambda acc: (acc,)
    add = lambda acc, other: (acc + other,)
    tiles = dict(tm=512, tn=1024)

    h1, w_in_t = _norm_fwd(xs, g_attn, "norm_attn", exchange=_gather_rows(shards[:1]))
    w_in_ta, w_in_tb = w_in_t[:n_a], w_in_t[n_a:]
    proj_a, = _matmul(h1, w_in_ta, "nt", [BF16], add, tm=512, tn=n_a, tk=D, row_ins=[b_in[:, :n_a]], name="proj_a")
    dils = [dil for _, dil in DILATED_BRANCHES]
    *proj_b, w_o = _proj_views(h1, w_in_tb, b_in[:, n_a:], dils, "proj_b", exchange=_gather_rows(shards[1:2]))

    lay_a = _AttnLayout(1, KV_HEADS_A, 0, 0, 0, QW // (KV_HEADS_A * HEAD_DIM), QW // (KV_HEADS_A * HEAD_DIM) + 1)
    bias_a = _band_bias(WINDOW_A - 1, 1)
    o_a, l_a, w_1_t = _attn_fwd(proj_a, bias_a, sinks_a, lay_a, "attn_a_fwd",
                                exchange=_gather_rows(shards[2:3], part=(0, 4)))
    branches = []
    for n, (window, dil) in enumerate(DILATED_BRANCHES):
        lay = _AttnLayout(dil, N_HEADS, 3, 0, 3, 1, 2)
        bias = _band_bias(window // dil, dil)
        ride = _gather_rows(shards[2:3], part=(n + 1, 4), into=[w_1_t])
        o, lse, w_1_t = _attn_fwd(proj_b[n], bias, None, lay, f"attn_b{dil}_fwd", exchange=ride)
        branches.append((lay, bias, proj_b[n], o, lse))
    o_b = [br[3] for br in branches]
    l_b = [br[4].transpose(2, 0, 1).reshape(T, N_HEADS) for br in branches]

    mix = _mix_fwd(o_a, o_b, l_b, g_out_a, g_out_b, dils)
    wide = dict(tm=512, tn=2048)

    def residual_and_norm(acc, res, g):
        x_new = acc + res
        return x_new, (x_new * _rstd(x_new)) * g

    assert wide["tn"] == D
    x2, h2 = _matmul(mix, w_o, "nn", [F32, BF16], residual_and_norm, tk=D, tile_ins=[xs], row_ins=[g_mlp],
                     name="out_proj", **wide)

    def relu_sq(acc):
        u = jnp.maximum(acc, 0.0)
        return u, u * u

    u, u_sq, w_2_f = _matmul(h2, w_1_t, "nt", [BF16, BF16], relu_sq, tk=D, name="mlp_up",
                             exchange=_gather_rows(shards[3:]), **wide)
    x3, = _matmul(u_sq, w_2_f, "nn", [F32], add, tk=4096, tile_ins=[x2], name="mlp_down", **tiles)

    dx3, dx3_b, dg_final, loss_dev = _loss_head(x3, tgt, g_fin)

    d_pre, = _matmul(dx3_b, w_2_f, "nt", [BF16], lambda acc, uu: (acc * (2.0 * uu.astype(F32)),),
                     tk=D, tile_ins=[u], name="mlp_down_bwd", **wide)
    wtiles = dict(tm=1024, tn=1024, tk=4096)
    dw_2, = _matmul(u_sq, dx3_b, "tn", [BF16], ident, name="mlp_down_wgrad", **wtiles)
    dh2, slots_2a = _matmul(d_pre, w_1_t, "nn", [BF16], ident, tk=4096, name="mlp_up_bwd",
                            exchange=_scatter_rows([dw_2], part=(0, 2)), **tiles)
    dw_1_t, slots_2b = _matmul(d_pre, h2, "tn", [BF16], ident, name="mlp_up_wgrad",
                               exchange=_scatter_rows([dw_2], part=(1, 2)), **wtiles)
    dx2, dg_mlp, dx2_b, dmix = _norm_bwd(dh2, x2, g_mlp, dx3, "norm_mlp_bwd", then_w_t=w_o)
    dw_o, = _matmul(mix, dx2_b, "tn", [BF16], ident, name="out_proj_wgrad", **wtiles)
    do_a, dd_a, do1, do2, do3, dd1, dd2, dd3, dg_out_a, dg_out_b, slots_o = _mix_bwd(
        dmix, o_a, o_b, l_b, g_out_a, g_out_b, dils, exchange=_scatter_rows([dw_o]))

    by_class = lambda d, dil: d.reshape(T // dil, dil, N_HEADS).transpose(1, 2, 0)
    slots_1 = [None] * 4
    dq_a, dk_a, dv_a, dsinks, slots_1[0] = _attn_bwd(proj_a, do_a, l_a, by_class(dd_a, 1), bias_a, sinks_a, lay_a,
                                                     "attn_a_bwd", exchange=_scatter_rows([dw_1_t], part=(0, 4)))
    dsinks = dsinks[:, 0].reshape(1, N_HEADS)
    dqs, dks, dvs = [], [], []
    for n, ((lay, bias, view, _, lse), do_n, dd_n) in enumerate(zip(branches, (do1, do2, do3), (dd1, dd2, dd3))):
        dq, dk, dv, slots_1[n + 1] = _attn_bwd(view, do_n, lse, by_class(dd_n, lay.dil), bias, None, lay,
                                               f"attn_b{lay.dil}_bwd",
                                               exchange=_scatter_rows([dw_1_t], part=(n + 1, 4)))
        dqs.append(dq)
        dks.append(dk)
        dvs.append(dv)
    dproj, db_in = _assemble([[dq_a], [dk_a], [dv_a], dqs, dks, dvs], "dproj", dils)

    n_in = w_in_t.shape[0]
    dw_in_t, = _matmul(dproj, h1, "tn", [BF16], ident, tm=n_in // 2, tn=1024, tk=1024, name="in_proj_wgrad")
    dh1, slots_in = _matmul(dproj, w_in_t, "nn", [BF16], ident, tk=n_in, name="in_proj_bwd",
                            exchange=_scatter_rows([dw_in_t]), **tiles)
    dx, dg_attn = _norm_bwd(dh1, xs, g_attn, dx2, "norm_attn_bwd")

    g_w_in = _sum_slots(slots_in, "sum_w_in_grads").T
    g_w_out = _sum_slots(slots_o, "sum_w_out_grads")
    g_w_1 = jnp.concatenate([_sum_slots(s, f"sum_w_1_grads_{n}") for n, s in enumerate(slots_1)]).T
    g_w_2 = jnp.concatenate([_sum_slots(slots_2a, "sum_w_2_grads_0"), _sum_slots(slots_2b, "sum_w_2_grads_1")])

    small_w = [g_attn, b_in, sinks_a, g_out_a, g_out_b, g_mlp, g_final]
    small_m = [m_g_attn, m_b_in, m_sinks_a, m_g_out_a, m_g_out_b, m_g_mlp, m_g_final]
    small_v = [v_g_attn, v_b_in, v_sinks_a, v_g_out_a, v_g_out_b, v_g_mlp, v_g_final]
    small_g = [dg_attn, db_in, dsinks, dg_out_a, dg_out_b, dg_mlp, dg_final]
    summed = _sum_over_devices(_pack_small(small_g + [loss_dev[:, :1]]))
    shapes = [w.shape for w in small_w]
    *g_small, loss = _unpack_small(summed, shapes + [()])

    big = [
        _adamw(w_in[0], g_w_in, m_w_in[0], v_w_in[0], "adamw_w_in"),
        _adamw(w_out[0], g_w_out, m_w_out[0], v_w_out[0], "adamw_w_out"),
        _adamw(w_1[0], g_w_1, m_w_1[0], v_w_1[0], "adamw_w_1"),
        _adamw(w_2[0], g_w_2, m_w_2[0], v_w_2[0], "adamw_w_2"),
    ]
    g_packed = _pack_small(g_small)
    small = _adamw(_pack_small(small_w), g_packed, _pack_small(small_m), _pack_small(small_v), "adamw_small")
    small = [_unpack_small(s, shapes) for s in small]

    def ordered(small_list, big_list):
        s = list(small_list)
        return [s[0], big_list[0][None], s[1], s[2], s[3], s[4], big_list[1][None], s[5],
                big_list[2][None], big_list[3][None], s[6]]

    grads = ordered(g_small, [g_w_in, g_w_out, g_w_1, g_w_2])
    deltas = ordered(small[0], [b[0] for b in big])
    new_m = ordered(small[1], [b[1] for b in big])
    new_v = ordered(small[2], [b[2] for b in big])
    return (loss, dx[None], *grads, *deltas, *new_m, *new_v)
```

```python
import numpy as np
import jax
import jax.numpy as jnp
from jax import lax
from jax.experimental import pallas as pl
from jax.experimental.pallas import tpu as pltpu

F32 = jnp.float32
BF16 = jnp.bfloat16

HEAD_DIM = 64
N_HEADS = 16
KV_HEADS_A = 2
BLOCK = 128
WINDOW_A = 128
DILATED_BRANCHES = ((128, 1), (512, 4), (2048, 16))
EPS = 1e-5
NEG_INF = -1e30
N_DEV = 8

ADAM_LR = 0.001
ADAM_B1 = 0.9
ADAM_B2 = 0.999
ADAM_EPS = 1e-08
ADAM_WD = 0.01
ADAM_STEP = 10

VMEM_LIMIT_BYTES = 56 * 1024 * 1024
MESH = pl.DeviceIdType.MESH
ANY = pl.BlockSpec(memory_space=pl.ANY)

NN = (((1,), (0,)), ((), ()))
NT = (((1,), (1,)), ((), ()))
TN = (((0,), (0,)), ((), ()))


def _dot(a, b, dims):
    return lax.dot_general(a, b, dims, preferred_element_type=F32)


def _params(*sem):
    return pltpu.CompilerParams(dimension_semantics=sem, vmem_limit_bytes=VMEM_LIMIT_BYTES)


RELAY_AT = 0.6


class _Exchange:
    def __init__(self, ins, out_shapes, n_remote, n_local, copies, aliases=None, relay=None):
        self.ins, self.out_shapes = list(ins), list(out_shapes)
        self.n_remote, self.n_local = n_remote, n_local
        self.copies = copies
        self.relay = relay
        self.aliases = aliases or {}

    def start(self, refs):
        local, sends, _ = self.copies(*refs)
        for cp in local + sends:
            cp.start()

    def middle(self, refs):
        arrived, onward = self.relay(*refs)
        for got, cp in zip(arrived, onward):
            got.wait_recv()
            cp.start()

    def finish(self, refs):
        local, sends, recvs = self.copies(*refs)
        for cp in recvs:
            cp.wait_recv()
        for cp in sends:
            cp.wait_send()
        for cp in local:
            cp.wait()
        if self.relay:
            for cp in self.relay(*refs)[1]:
                cp.wait_send()


class _Ride:
    def __init__(self, ex, n_in, n_out, n_scratch):
        self.ex = ex
        self.n = (n_in, n_out, n_scratch)
        self.args = ex.ins if ex else []
        self.in_specs = [ANY] * len(self.args)
        self.out_shapes = ex.out_shapes if ex else []
        self.out_specs = [ANY] * len(self.out_shapes)
        self.scratch = [pltpu.SemaphoreType.DMA((ex.n_remote,)), pltpu.SemaphoreType.DMA((ex.n_remote,)),
                        pltpu.SemaphoreType.DMA((max(ex.n_local, 1),))] if ex else []
        self.aliases = {n_in + i: n_out + o for i, o in ex.aliases.items()} if ex else {}

    def split(self, refs):
        n_in, n_out, n_scratch = self.n
        a = n_in
        b = a + len(self.args)
        c = b + n_out
        d = c + len(self.out_shapes)
        e = d + n_scratch
        return refs[:a], refs[b:c], refs[d:e], (refs[a:b], refs[c:d], *refs[e:])

    def around(self, step, n_steps, exrefs, compute):
        if self.ex is None:
            compute()
            return

        @pl.when(step == 0)
        def _():
            self.ex.start(exrefs)

        compute()

        if self.ex.relay:
            @pl.when(step == int(RELAY_AT * (n_steps - 1)))
            def _():
                self.ex.middle(exrefs)

        @pl.when(step == n_steps - 1)
        def _():
            self.ex.finish(exrefs)


def _matmul(a, b, dims, out_dtypes, epilogue, *, tm, tn, tk, name, tile_ins=(), row_ins=(), exchange=None):
    if dims == "tn":
        K, M = a.shape
    else:
        M, K = a.shape
    N = b.shape[0] if dims == "nt" else b.shape[1]
    tm, tn, tk = min(tm, M), min(tn, N), min(tk, K)
    assert M % tm == 0 and N % tn == 0 and K % tk == 0, (name, M, N, K, tm, tn, tk)
    grid = (M // tm, N // tn, K // tk)
    nk = grid[2]
    n_tile, n_row, n_out = len(tile_ins), len(row_ins), len(out_dtypes)
    dn = {"nn": NN, "nt": NT, "tn": TN}[dims]
    ride = _Ride(exchange, 2 + n_tile + n_row, n_out, 1 if nk > 1 else 0)

    def kern(*refs):
        ins, out_refs, scratch, exrefs = ride.split(refs)
        a_ref, b_ref = ins[:2]
        tile_refs = ins[2:2 + n_tile]
        row_refs = ins[2 + n_tile:]
        ids = [pl.program_id(d) for d in range(3)]

        def finish(acc):
            outs = epilogue(acc, *[r[...] for r in tile_refs], *[r[...] for r in row_refs])
            for o_ref, o in zip(out_refs, outs):
                o_ref[...] = o.astype(o_ref.dtype)

        def compute():
            if nk == 1:
                finish(_dot(a_ref[...], b_ref[...], dn))
                return
            acc_ref = scratch[0]

            @pl.when(ids[2] == 0)
            def _():
                acc_ref[...] = jnp.zeros_like(acc_ref)

            acc_ref[...] += _dot(a_ref[...], b_ref[...], dn)

            @pl.when(ids[2] == nk - 1)
            def _():
                finish(acc_ref[...])

        ride.around((ids[0] * grid[1] + ids[1]) * grid[2] + ids[2], grid[0] * grid[1] * grid[2], exrefs, compute)

    if dims == "tn":
        a_spec = pl.BlockSpec((tk, tm), lambda i, j, k: (k, i))
    else:
        a_spec = pl.BlockSpec((tm, tk), lambda i, j, k: (i, k))
    if dims == "nt":
        b_spec = pl.BlockSpec((tn, tk), lambda i, j, k: (j, k))
    else:
        b_spec = pl.BlockSpec((tk, tn), lambda i, j, k: (k, j))
    tile_spec = pl.BlockSpec((tm, tn), lambda i, j, k: (i, j))
    row_spec = pl.BlockSpec((1, tn), lambda i, j, k: (0, j))
    sem = ("arbitrary",) * 3 if exchange else ("parallel", "parallel", "arbitrary")
    return pl.pallas_call(
        kern,
        name=name,
        grid=grid,
        in_specs=[a_spec, b_spec] + [tile_spec] * n_tile + [row_spec] * n_row + ride.in_specs,
        out_specs=[tile_spec] * n_out + ride.out_specs,
        out_shape=[jax.ShapeDtypeStruct((M, N), dt) for dt in out_dtypes] + ride.out_shapes,
        scratch_shapes=([pltpu.VMEM((tm, tn), F32)] if nk > 1 else []) + ride.scratch,
        input_output_aliases=ride.aliases,
        compiler_params=_params(*sem),
    )(a, b, *tile_ins, *row_ins, *ride.args)


PROJ_ROWS = 256


def _proj_views(a, w_t, bias, cols, dils, name, exchange=None):
    T, K = a.shape
    first, N = cols
    ride = _Ride(exchange, 3, len(dils), 1)

    def kern(*refs):
        (a_ref, w_ref, b_ref), outs, (scr,), exrefs = ride.split(refs)

        def compute():
            acc = _dot(a_ref[...], w_ref[...], NT) + b_ref[...]
            for out_ref, dil in zip(outs, dils):
                _to_class_order(acc, scr, out_ref, dil)

        ride.around(pl.program_id(0), T // PROJ_ROWS, exrefs, compute)

    return pl.pallas_call(
        kern, name=name, grid=(T // PROJ_ROWS,),
        in_specs=[pl.BlockSpec((PROJ_ROWS, K), lambda i: (i, 0)),
                  pl.BlockSpec((pl.Element(N), pl.Element(K)), lambda i: (first, 0)),
                  pl.BlockSpec((pl.Element(1), pl.Element(N)), lambda i: (0, first))] + ride.in_specs,
        out_specs=[_view_spec(PROJ_ROWS, N, d) for d in dils] + ride.out_specs,
        out_shape=[jax.ShapeDtypeStruct((T // d, d * N), BF16) for d in dils] + ride.out_shapes,
        scratch_shapes=[_regroup_scratch(PROJ_ROWS, N)] + ride.scratch,
        input_output_aliases=ride.aliases,
        compiler_params=_params("arbitrary"),
    )(a, w_t, bias, *ride.args)


ROWS = 256
MIX_ROWS = 128


def _rstd(xv):
    return lax.rsqrt(jnp.mean(xv * xv, axis=-1, keepdims=True) + EPS)


def _norm_fwd(x, g, name, exchange=None):
    T, D = x.shape
    ride = _Ride(exchange, 2, 1, 0)

    def kern(*refs):
        (x_ref, g_ref), (h_ref,), _, exrefs = ride.split(refs)

        def compute():
            xv = x_ref[...]
            h_ref[...] = ((xv * _rstd(xv)) * g_ref[...]).astype(h_ref.dtype)

        ride.around(pl.program_id(0), T // ROWS, exrefs, compute)

    row = pl.BlockSpec((ROWS, D), lambda i: (i, 0))
    return pl.pallas_call(
        kern, name=name, grid=(T // ROWS,),
        in_specs=[row, pl.BlockSpec((1, D), lambda i: (0, 0))] + ride.in_specs,
        out_specs=[row] + ride.out_specs,
        out_shape=[jax.ShapeDtypeStruct((T, D), BF16)] + ride.out_shapes,
        scratch_shapes=ride.scratch, input_output_aliases=ride.aliases,
        compiler_params=_params("arbitrary"),
    )(x, g, *ride.args)


def _norm_bwd(dh, x, g, res, name, then_w_t=None):
    T, D = x.shape
    rows = PROJ_ROWS if then_w_t is not None else ROWS

    def kern(dh_ref, x_ref, g_ref, res_ref, *rest):
        if then_w_t is None:
            dx_ref, dg_ref = rest
        else:
            w_ref, dx_ref, dg_ref, dxb_ref, y_ref = rest

        @pl.when(pl.program_id(0) == 0)
        def _():
            dg_ref[...] = jnp.zeros_like(dg_ref)

        xv = x_ref[...]
        r = _rstd(xv)
        xn = xv * r
        dhv = dh_ref[...].astype(F32)
        dg_ref[...] += jnp.sum(dhv * xn, axis=0, keepdims=True)
        t = dhv * g_ref[...]
        dx = res_ref[...] + r * (t - xn * jnp.mean(t * xn, axis=-1, keepdims=True))
        dx_ref[...] = dx
        if then_w_t is not None:
            dxb = dx.astype(BF16)
            dxb_ref[...] = dxb
            y_ref[...] = _dot(dxb, w_ref[...], NT)

    row = pl.BlockSpec((rows, D), lambda i: (i, 0))
    vec = pl.BlockSpec((1, D), lambda i: (0, 0))
    in_specs, args = [row, row, vec, row], [dh, x, g, res]
    out_specs = [row, vec]
    out_shape = [jax.ShapeDtypeStruct((T, D), F32), jax.ShapeDtypeStruct((1, D), F32)]
    if then_w_t is not None:
        N = then_w_t.shape[0]
        in_specs.append(pl.BlockSpec((N, D), lambda i: (0, 0)))
        args.append(then_w_t)
        out_specs += [row, pl.BlockSpec((rows, N), lambda i: (i, 0))]
        out_shape += [jax.ShapeDtypeStruct((T, D), BF16), jax.ShapeDtypeStruct((T, N), F32)]
    return pl.pallas_call(
        kern, name=name, grid=(T // rows,), in_specs=in_specs, out_specs=out_specs, out_shape=out_shape,
        compiler_params=_params("arbitrary"),
    )(*args)


def _loss_head(x3, tgt, g):
    T, D = x3.shape

    def kern(x_ref, t_ref, g_ref, dx_ref, dxb_ref, dg_ref, loss_ref):
        @pl.when(pl.program_id(0) == 0)
        def _():
            dg_ref[...] = jnp.zeros_like(dg_ref)
            loss_ref[...] = jnp.zeros_like(loss_ref)

        xv = x_ref[...]
        gv = g_ref[...]
        r = _rstd(xv)
        xn = xv * r
        err = xn * gv - t_ref[...]
        per_tok = jnp.mean(err * err, axis=-1, keepdims=True)
        loss_ref[...] += 0.5 * jnp.sum(per_tok, axis=0, keepdims=True)
        dy = err * (1.0 / D)
        dg_ref[...] += jnp.sum(dy * xn, axis=0, keepdims=True)
        t = dy * gv
        dx = r * (t - xn * jnp.mean(t * xn, axis=-1, keepdims=True))
        dx_ref[...] = dx
        dxb_ref[...] = dx.astype(BF16)

    row = pl.BlockSpec((ROWS, D), lambda i: (i, 0))
    vec = pl.BlockSpec((1, D), lambda i: (0, 0))
    return pl.pallas_call(
        kern, name="loss_head", grid=(T // ROWS,),
        in_specs=[row, row, vec],
        out_specs=[row, row, vec, pl.BlockSpec((1, 128), lambda i: (0, 0))],
        out_shape=[jax.ShapeDtypeStruct((T, D), F32), jax.ShapeDtypeStruct((T, D), BF16),
                   jax.ShapeDtypeStruct((1, D), F32), jax.ShapeDtypeStruct((1, 128), F32)],
        compiler_params=_params("arbitrary"),
    )(x3, tgt, g)


def _spread_matrix():
    head_of_lane = np.arange(N_HEADS * HEAD_DIM) // HEAD_DIM
    return jnp.asarray(np.arange(N_HEADS)[:, None] == head_of_lane[None, :], dtype=BF16)


def _pieces(v, n):
    out = []
    for _ in range(n):
        piece = v.astype(BF16)
        out.append(piece)
        v = v - piece.astype(F32)
    return out


def _spread(v, spread):
    return sum(_dot(p, spread, NN) for p in _pieces(v, 2))


def _spread_weights(w1, w2, spread):
    s1, s2 = _spread(w1, spread), _spread(w2, spread)
    return s1, s2, 1.0 - s1 - s2


def _head_sums(v, spread):
    return sum(_dot(p, spread, NT) for p in _pieces(v, 2))


def _branch_weights(l1, l2, l3):
    lm = jnp.maximum(jnp.maximum(l1, l2), l3)
    e1, e2, e3 = jnp.exp(l1 - lm), jnp.exp(l2 - lm), jnp.exp(l3 - lm)
    inv = 1.0 / (e1 + e2 + e3)
    return e1 * inv, e2 * inv, e3 * inv


def _regroup_scratch(rows, width):
    return pltpu.VMEM((width // LANES, rows, LANES), F32)


def _to_token_order(view_ref, scr, dil):
    if dil == 1:
        return view_ref[...].astype(F32)
    n_l, w = view_ref.shape[0], view_ref.shape[1] // dil
    for r in range(dil):
        for cb in range(w // LANES):
            scr[cb, pl.ds(r, n_l, stride=dil), :] = view_ref[:, r * w + cb * LANES:r * w + (cb + 1) * LANES].astype(F32)
    return jnp.concatenate([scr[cb] for cb in range(w // LANES)], axis=1)


def _to_class_order(val, scr, view_ref, dil):
    if dil == 1:
        view_ref[...] = val.astype(view_ref.dtype)
        return
    n, w = val.shape
    for cb in range(w // LANES):
        scr[cb] = val[:, cb * LANES:(cb + 1) * LANES]
    for r in range(dil):
        for cb in range(w // LANES):
            view_ref[:, r * w + cb * LANES:r * w + (cb + 1) * LANES] = (
                scr[cb, pl.ds(r, n // dil, stride=dil), :].astype(view_ref.dtype))


def _view_spec(rows, width, dil):
    return pl.BlockSpec((rows // dil, dil * width), lambda i: (i, 0))


def _mix_fwd(oa, obs, lbs, ga, gb, dils):
    T, W = oa.shape

    def kern(oa_ref, o1, o2, o3, l1, l2, l3, ga_ref, gb_ref, sp_ref, mix_ref, *scr):
        sp = sp_ref[...]
        w1, w2, w3 = _branch_weights(l1[...], l2[...], l3[...])
        on = [_to_token_order(o, s, d) for o, s, d in zip((o1, o2, o3), scr, dils)]
        s1, s2, s3 = _spread_weights(w1, w2, sp)
        ob = s1 * on[0] + s2 * on[1] + s3 * on[2]
        oav = oa_ref[...]
        mix_ref[:, :W] = ((oav * _rstd(oav)) * ga_ref[...]).astype(BF16)
        mix_ref[:, W:] = ((ob * _rstd(ob)) * gb_ref[...]).astype(BF16)

    row = pl.BlockSpec((MIX_ROWS, W), lambda i: (i, 0))
    per_head = pl.BlockSpec((MIX_ROWS, N_HEADS), lambda i: (i, 0))
    vec = pl.BlockSpec((1, W), lambda i: (0, 0))
    return pl.pallas_call(
        kern, name="mix_fwd", grid=(T // MIX_ROWS,),
        in_specs=[row] + [_view_spec(MIX_ROWS, W, d) for d in dils] + [per_head] * 3
        + [vec, vec, pl.BlockSpec((N_HEADS, W), lambda i: (0, 0))],
        out_specs=pl.BlockSpec((MIX_ROWS, 2 * W), lambda i: (i, 0)),
        out_shape=jax.ShapeDtypeStruct((T, 2 * W), BF16),
        scratch_shapes=[_regroup_scratch(MIX_ROWS, W)] * 3,
        compiler_params=_params("parallel"),
    )(oa, *obs, *lbs, ga, gb, _spread_matrix())


def _mix_bwd(dmix, oa, obs, lbs, ga, gb, dils, exchange=None):
    T, W = oa.shape
    ride = _Ride(exchange, 11, 10, 3)

    def kern(*refs):
        ins, outs, scr, exrefs = ride.split(refs)
        ride.around(pl.program_id(0), T // MIX_ROWS, exrefs, lambda: compute(*ins, *outs, *scr))

    def compute(dm_ref, oa_ref, o1, o2, o3, l1, l2, l3, ga_ref, gb_ref, sp_ref,
                doa_ref, da_ref, do1, do2, do3, d1, d2, d3, dga_ref, dgb_ref, *scr):
        @pl.when(pl.program_id(0) == 0)
        def _():
            dga_ref[...] = jnp.zeros_like(dga_ref)
            dgb_ref[...] = jnp.zeros_like(dgb_ref)

        sp = sp_ref[...]
        oav = oa_ref[...]
        r = _rstd(oav)
        on = oav * r
        dy = dm_ref[:, :W]
        dga_ref[...] += jnp.sum(dy * on, axis=0, keepdims=True)
        t = dy * ga_ref[...]
        doa = r * (t - on * jnp.mean(t * on, axis=-1, keepdims=True))
        doa_ref[...] = doa.astype(BF16)
        da_ref[...] = _head_sums(doa * oav, sp)
        w1, w2, w3 = _branch_weights(l1[...], l2[...], l3[...])
        s1, s2, s3 = _spread_weights(w1, w2, sp)
        on = [_to_token_order(o, sc, d) for o, sc, d in zip((o1, o2, o3), scr, dils)]
        ob = s1 * on[0] + s2 * on[1] + s3 * on[2]
        r = _rstd(ob)
        on = ob * r
        dy = dm_ref[:, W:]
        dgb_ref[...] += jnp.sum(dy * on, axis=0, keepdims=True)
        t = dy * gb_ref[...]
        dob = r * (t - on * jnp.mean(t * on, axis=-1, keepdims=True))
        c = _head_sums(dob * ob, sp)
        for do_ref, sn, sc, d in zip((do1, do2, do3), (s1, s2, s3), scr, dils):
            _to_class_order(sn * dob, sc, do_ref, d)
        d1[...] = w1 * c
        d2[...] = w2 * c
        d3[...] = w3 * c

    row = pl.BlockSpec((MIX_ROWS, W), lambda i: (i, 0))
    per_head = pl.BlockSpec((MIX_ROWS, N_HEADS), lambda i: (i, 0))
    vec = pl.BlockSpec((1, W), lambda i: (0, 0))
    bf = jax.ShapeDtypeStruct((T, W), BF16)
    ph = jax.ShapeDtypeStruct((T, N_HEADS), F32)
    vv = jax.ShapeDtypeStruct((1, W), F32)
    views = [_view_spec(MIX_ROWS, W, d) for d in dils]
    return pl.pallas_call(
        kern, name="mix_bwd", grid=(T // MIX_ROWS,),
        in_specs=[pl.BlockSpec((MIX_ROWS, 2 * W), lambda i: (i, 0)), row] + views + [per_head] * 3 + [vec, vec,
                  pl.BlockSpec((N_HEADS, W), lambda i: (0, 0))] + ride.in_specs,
        out_specs=[row, per_head] + views + [per_head, per_head, per_head, vec, vec] + ride.out_specs,
        out_shape=[bf, ph] + [jax.ShapeDtypeStruct(o.shape, F32) for o in obs] + [ph, ph, ph, vv, vv]
        + ride.out_shapes,
        scratch_shapes=[_regroup_scratch(MIX_ROWS, W)] * 3 + ride.scratch,
        input_output_aliases=ride.aliases,
        compiler_params=_params("arbitrary"),
    )(dmix, oa, *obs, *lbs, ga, gb, _spread_matrix(), *ride.args)


def _alibi_slopes(n):
    return np.asarray(2.0 ** (-8.0 * (np.arange(n) + 1) / n)).astype(np.float32)


def _band_bias(max_steps, step_dist):
    qi = np.arange(BLOCK)[None, :]
    kj = np.arange(BLOCK)[:, None]
    slopes = _alibi_slopes(N_HEADS)
    halves = []
    for steps in (qi + BLOCK - kj, qi - kj):
        valid = (steps >= 0) & (steps <= max_steps)
        alibi = slopes[:, None, None] * (step_dist * steps).astype(np.float32)[None]
        halves.append(np.where(valid[None], -alibi, np.float32(NEG_INF)).astype(np.float32))
    per_head = np.concatenate(halves, axis=1)
    return jnp.asarray(np.concatenate([per_head[0::2], per_head[1::2]], axis=2))


class _AttnLayout:
    def __init__(self, dil, kv_heads, q_stride, q_off, k_stride, k_off, v_off):
        self.dil = dil
        self.kv_heads = kv_heads
        self.kw = kv_heads * HEAD_DIM
        self.rep = N_HEADS // kv_heads
        self.q_col = lambda r: r * q_stride + q_off
        self.k_col = lambda r: r * k_stride + k_off
        self.v_col = lambda r: r * k_stride + v_off


QW = N_HEADS * HEAD_DIM
LANES = 128


PAIRS = N_HEADS // 2


def _pair_cols(pair):
    return slice(pair * LANES, (pair + 1) * LANES)


def _first_head_lanes(shape):
    return lax.broadcasted_iota(jnp.int32, shape, 1) < HEAD_DIM


def _split_heads(pair):
    first = _first_head_lanes(pair.shape)
    zero = jnp.zeros_like(pair)
    return jnp.concatenate([jnp.where(first, pair, zero), jnp.where(first, zero, pair)], axis=0)


def _kv_pair(ref, pair, rep):
    if rep == 1:
        return ref[:, _pair_cols(pair)]
    blk = ref[...].astype(F32)
    other = pltpu.roll(blk, HEAD_DIM, 1)
    first = _first_head_lanes(blk.shape)
    both = jnp.where(first, blk, other) if (2 * pair // rep) % 2 == 0 else jnp.where(first, other, blk)
    return both.astype(ref.dtype)


def _paired_kv(prev_ref, cur_ref, rep, transposed=False):
    memo = {}

    def get(pair):
        key = pair if rep == 1 else 2 * pair // rep
        if key not in memo:
            blocks = [_kv_pair(ref, pair, rep) for ref in (prev_ref, cur_ref)]
            memo[key] = jnp.concatenate([b.T for b in blocks], axis=1) if transposed else jnp.concatenate(blocks, axis=0)
        return memo[key]

    return get


def _attn_fwd(proj, bias, sinks, lay, name, exchange=None):
    L = proj.shape[0]
    nb = L // BLOCK
    kw, rep = lay.kw, lay.rep
    use_sinks = sinks is not None
    scale = HEAD_DIM ** -0.5
    ride = _Ride(exchange, 7 if use_sinks else 6, 2, 2)

    def kern(*refs):
        ins, (o_ref, l_ref), (sc_ref, pr_ref), exrefs = ride.split(refs)
        q_ref, kc_ref, kp_ref, vc_ref, vp_ref, b_ref = ins[:6]
        s_ref = ins[6] if use_sinks else None
        r, i = pl.program_id(0), pl.program_id(1)
        first = i == 0
        ride.around(r * nb + i, lay.dil * nb, exrefs,
                    lambda: compute(q_ref, kc_ref, kp_ref, vc_ref, vp_ref, b_ref, s_ref, o_ref, l_ref, first,
                                    sc_ref, pr_ref))

    def compute(q_ref, kc_ref, kp_ref, vc_ref, vp_ref, b_ref, s_ref, o_ref, l_ref, first, sc_ref, pr_ref):
        keys, values_t = _paired_kv(kp_ref, kc_ref, rep), _paired_kv(vp_ref, vc_ref, rep, transposed=True)
        for pair in range(PAIRS):
            qs = _split_heads(q_ref[:, _pair_cols(pair)])
            s = _dot(keys(pair), qs, NT) * scale + b_ref[pair]
            sc_ref[pair, :BLOCK] = jnp.where(first, NEG_INF, s[:BLOCK])
            sc_ref[pair, BLOCK:] = s[BLOCK:]
        inv = []
        for h in range(N_HEADS):
            cols = slice(h % 2 * BLOCK, (h % 2 + 1) * BLOCK)
            s = sc_ref[h // 2, :, cols]
            m = jnp.max(s, axis=0, keepdims=True)
            if use_sinks:
                sink = s_ref[:, h:h + 1]
                m = jnp.maximum(m, sink)
            p = jnp.exp(s - m)
            denom = jnp.sum(p, axis=0, keepdims=True)
            if use_sinks:
                denom = denom + jnp.exp(sink - m)
            pr_ref[h // 2, :, cols] = p.astype(BF16)
            l_ref[h:h + 1, :] = m + jnp.log(denom)
            inv.append(1.0 / denom)
        for pair in range(PAIRS):
            both = _dot(values_t(pair), pr_ref[pair], NN)
            o_t = jnp.concatenate([both[:HEAD_DIM, :BLOCK] * inv[2 * pair], both[HEAD_DIM:, BLOCK:] * inv[2 * pair + 1]],
                                  axis=0)
            o_ref[:, _pair_cols(pair)] = o_t.T

    prev = lambda i: jnp.maximum(i - 1, 0)
    in_specs = [
        pl.BlockSpec((BLOCK, QW), lambda r, i: (i, lay.q_col(r))),
        pl.BlockSpec((BLOCK, kw), lambda r, i: (i, lay.k_col(r))),
        pl.BlockSpec((BLOCK, kw), lambda r, i: (prev(i), lay.k_col(r))),
        pl.BlockSpec((BLOCK, kw), lambda r, i: (i, lay.v_col(r))),
        pl.BlockSpec((BLOCK, kw), lambda r, i: (prev(i), lay.v_col(r))),
        pl.BlockSpec((PAIRS, 2 * BLOCK, 2 * BLOCK), lambda r, i: (0, 0, 0)),
    ]
    args = [proj, proj, proj, proj, proj, bias]
    if use_sinks:
        in_specs.append(pl.BlockSpec((1, N_HEADS), lambda r, i: (0, 0)))
        args.append(sinks)
    out_specs = [pl.BlockSpec((BLOCK, QW), lambda r, i: (i, r)),
                 pl.BlockSpec((None, N_HEADS, BLOCK), lambda r, i: (r, 0, i))]
    out_shape = [jax.ShapeDtypeStruct((L, lay.dil * QW), F32), jax.ShapeDtypeStruct((lay.dil, N_HEADS, L), F32)]
    return pl.pallas_call(
        kern, name=name, grid=(lay.dil, nb),
        in_specs=in_specs + ride.in_specs, out_specs=out_specs + ride.out_specs,
        out_shape=out_shape + ride.out_shapes,
        scratch_shapes=[pltpu.VMEM((PAIRS, 2 * BLOCK, 2 * BLOCK), dt) for dt in (F32, BF16)] + ride.scratch,
        input_output_aliases=ride.aliases,
        compiler_params=_params("arbitrary", "arbitrary"),
    )(*args, *ride.args)


def _attn_bwd(proj, do, lse, dd, bias, sinks, lay, name, exchange=None):
    L = proj.shape[0]
    nb = L // BLOCK
    kw, rep = lay.kw, lay.rep
    assert rep == 1 or lay.kv_heads == 2, "grouped queries: the two kv heads fill one 128-lane block"
    use_sinks = sinks is not None
    scale = HEAD_DIM ** -0.5
    ride = _Ride(exchange, 10 if use_sinks else 9, 4 if use_sinks else 3, 6)

    def kern(*refs):
        ins, outs, (ck_ref, cv_ref, *staged), exrefs = ride.split(refs)
        q_ref, kc_ref, kp_ref, vc_ref, vp_ref, do_ref, l_ref, d_ref, b_ref = ins[:9]
        s_ref = ins[9] if use_sinks else None
        dq_ref, dk_ref, dv_ref = outs[:3]
        ds_ref = outs[3] if use_sinks else None
        r = pl.program_id(0)
        i = pl.program_id(1)
        ride.around(r * (nb + 1) + i, lay.dil * (nb + 1), exrefs,
                    lambda: compute(q_ref, kc_ref, kp_ref, vc_ref, vp_ref, do_ref, l_ref, d_ref, b_ref, s_ref,
                                    dq_ref, dk_ref, dv_ref, ds_ref, ck_ref, cv_ref, r, i, *staged))

    def compute(q_ref, kc_ref, kp_ref, vc_ref, vp_ref, do_ref, l_ref, d_ref, b_ref, s_ref,
                dq_ref, dk_ref, dv_ref, ds_ref, ck_ref, cv_ref, r, i, sc_ref, dp_ref, pr_ref, dsc_ref):
        first = i == 0

        @pl.when(first)
        def _():
            ck_ref[...] = jnp.zeros_like(ck_ref)
            cv_ref[...] = jnp.zeros_like(cv_ref)

        if use_sinks:
            @pl.when(first & (r == 0))
            def _():
                ds_ref[...] = jnp.zeros_like(ds_ref)

        @pl.when(i < nb)
        def _():
            keys, values = _paired_kv(kp_ref, kc_ref, rep), _paired_kv(vp_ref, vc_ref, rep)
            keys_t = _paired_kv(kp_ref, kc_ref, rep, transposed=True)
            for pair in range(PAIRS):
                qs = _split_heads(q_ref[:, _pair_cols(pair)])
                dos = _split_heads(do_ref[:, _pair_cols(pair)].astype(BF16))
                s = _dot(keys(pair), qs, NT) * scale + b_ref[pair]
                sc_ref[pair, :BLOCK] = jnp.where(first, NEG_INF, s[:BLOCK])
                sc_ref[pair, BLOCK:] = s[BLOCK:]
                dp_ref[pair] = _dot(values(pair), dos, NT)
            for h in range(N_HEADS):
                cols = slice(h % 2 * BLOCK, (h % 2 + 1) * BLOCK)
                lrow = l_ref[h:h + 1, :]
                drow = d_ref[h:h + 1, :]
                p = jnp.exp(sc_ref[h // 2, :, cols] - lrow)
                pr_ref[h // 2, :, cols] = p.astype(BF16)
                dsc_ref[h // 2, :, cols] = (p * (dp_ref[h // 2, :, cols] - drow) * scale).astype(BF16)
                if use_sinks:
                    ds_ref[h:h + 1, :] += -(jnp.exp(s_ref[:, h:h + 1] - lrow) * drow)
            grouped = {}
            for pair in range(PAIRS):
                cols = _pair_cols(pair)
                qs = _split_heads(q_ref[:, cols])
                dos = _split_heads(do_ref[:, cols].astype(BF16))
                ds = dsc_ref[pair]
                both = _dot(keys_t(pair), ds, NN)
                dq_t = jnp.concatenate([both[:HEAD_DIM, :BLOCK], both[HEAD_DIM:, BLOCK:]], axis=0)
                dq_ref[:, cols] = dq_t.T.astype(dq_ref.dtype)
                dk = _dot(ds, qs, NN)
                dv = _dot(pr_ref[pair], dos, NN)
                if rep == 1:
                    dk_ref[:, cols] = (ck_ref[:, cols] + dk[:BLOCK]).astype(dk_ref.dtype)
                    dv_ref[:, cols] = (cv_ref[:, cols] + dv[:BLOCK]).astype(dv_ref.dtype)
                    ck_ref[:, cols] = dk[BLOCK:]
                    cv_ref[:, cols] = dv[BLOCK:]
                else:
                    g = 2 * pair // rep
                    grouped[g] = (dk, dv) if g not in grouped else (grouped[g][0] + dk, grouped[g][1] + dv)
            if rep > 1:
                fold = lambda t: t + pltpu.roll(t, HEAD_DIM, 1)
                first_half = _first_head_lanes((2 * BLOCK, LANES))
                dk = jnp.where(first_half, fold(grouped[0][0]), fold(grouped[1][0]))
                dv = jnp.where(first_half, fold(grouped[0][1]), fold(grouped[1][1]))
                dk_ref[...] = (ck_ref[...] + dk[:BLOCK]).astype(dk_ref.dtype)
                dv_ref[...] = (cv_ref[...] + dv[:BLOCK]).astype(dv_ref.dtype)
                ck_ref[...] = dk[BLOCK:]
                cv_ref[...] = dv[BLOCK:]

        @pl.when(i == nb)
        def _():
            dk_ref[...] = ck_ref[...].astype(dk_ref.dtype)
            dv_ref[...] = cv_ref[...].astype(dv_ref.dtype)
            if use_sinks:
                @pl.when(r == lay.dil - 1)
                def _():
                    ds_ref[...] = jnp.broadcast_to(jnp.sum(ds_ref[...], axis=1, keepdims=True), ds_ref.shape)

    cur = lambda i: jnp.minimum(i, nb - 1)
    prev = lambda i: jnp.maximum(jnp.minimum(i, nb - 1) - 1, 0)
    done = lambda i: jnp.maximum(i - 1, 0)
    qspec = lambda col: pl.BlockSpec((BLOCK, QW), lambda r, i: (cur(i), col(r)))
    per_head = pl.BlockSpec((None, N_HEADS, BLOCK), lambda r, i: (r, 0, cur(i)))
    in_specs = [
        qspec(lay.q_col),
        pl.BlockSpec((BLOCK, kw), lambda r, i: (cur(i), lay.k_col(r))),
        pl.BlockSpec((BLOCK, kw), lambda r, i: (prev(i), lay.k_col(r))),
        pl.BlockSpec((BLOCK, kw), lambda r, i: (cur(i), lay.v_col(r))),
        pl.BlockSpec((BLOCK, kw), lambda r, i: (prev(i), lay.v_col(r))),
        qspec(lambda r: r), per_head, per_head,
        pl.BlockSpec((PAIRS, 2 * BLOCK, 2 * BLOCK), lambda r, i: (0, 0, 0)),
    ]
    args = [proj, proj, proj, proj, proj, do, lse, dd, bias]
    out_specs = [
        qspec(lambda r: r),
        pl.BlockSpec((BLOCK, kw), lambda r, i: (done(i), r)),
        pl.BlockSpec((BLOCK, kw), lambda r, i: (done(i), r)),
    ]
    dkv_shape = jax.ShapeDtypeStruct((L, lay.dil * kw), BF16)
    out_shape = [jax.ShapeDtypeStruct((L, lay.dil * QW), BF16), dkv_shape, dkv_shape]
    if use_sinks:
        in_specs.append(pl.BlockSpec((1, N_HEADS), lambda r, i: (0, 0)))
        args.append(sinks)
        out_specs.append(pl.BlockSpec((N_HEADS, LANES), lambda r, i: (0, 0)))
        out_shape.append(jax.ShapeDtypeStruct((N_HEADS, LANES), F32))
    return pl.pallas_call(
        kern, name=name, grid=(lay.dil, nb + 1),
        in_specs=in_specs + ride.in_specs, out_specs=out_specs + ride.out_specs,
        out_shape=out_shape + ride.out_shapes,
        scratch_shapes=[pltpu.VMEM((BLOCK, kw), F32), pltpu.VMEM((BLOCK, kw), F32)]
        + [pltpu.VMEM((PAIRS, 2 * BLOCK, 2 * BLOCK), dt) for dt in (F32, F32, BF16, BF16)] + ride.scratch,
        input_output_aliases=ride.aliases,
        compiler_params=_params("arbitrary", "arbitrary"),
    )(*args, *ride.args)


def _assemble(groups, name, dils=(1,)):
    T = groups[0][0].shape[0] * dils[0]
    widths = [g[0].shape[1] // dils[0] for g in groups]
    total = sum(widths)
    flat = [a for g in groups for a in g]
    member_dils = [d for g in groups for d in dils[:len(g)]]

    def kern(*refs):
        ins = refs[:len(flat)]
        out_ref, cs_ref = refs[len(flat):len(flat) + 2]
        scr = refs[len(flat) + 2:]

        @pl.when(pl.program_id(0) == 0)
        def _():
            cs_ref[...] = jnp.zeros_like(cs_ref)

        pos = off = 0
        for g, w in zip(groups, widths):
            acc = _to_token_order(ins[pos], None, dils[0])
            for j in range(1, len(g)):
                acc = acc + _to_token_order(ins[pos + j], scr[j - 1], dils[j])
            pos += len(g)
            out_ref[:, off:off + w] = acc.astype(BF16)
            cs_ref[:, off:off + w] += jnp.sum(acc, axis=0, keepdims=True)
            off += w

    return pl.pallas_call(
        kern, name=name, grid=(T // ROWS,),
        in_specs=[_view_spec(ROWS, a.shape[1] // d, d) for a, d in zip(flat, member_dils)],
        out_specs=[pl.BlockSpec((ROWS, total), lambda i: (i, 0)), pl.BlockSpec((1, total), lambda i: (0, 0))],
        out_shape=[jax.ShapeDtypeStruct((T, total), BF16), jax.ShapeDtypeStruct((1, total), F32)],
        scratch_shapes=[_regroup_scratch(ROWS, max(widths))] * (len(dils) - 1),
        compiler_params=_params("arbitrary"),
    )(*flat)


def _adamw(w, g, m, v, name):
    _, R, C = w.shape
    rows = min(R, ROWS)
    assert R % rows == 0

    def kern(w_ref, g_ref, m_ref, v_ref, d_ref, nm_ref, nv_ref):
        gv = g_ref[...]
        mn = ADAM_B1 * m_ref[...] + (1.0 - ADAM_B1) * gv
        vn = ADAM_B2 * v_ref[...] + (1.0 - ADAM_B2) * jnp.square(gv)
        m_hat = mn / (1.0 - ADAM_B1 ** ADAM_STEP)
        v_hat = vn / (1.0 - ADAM_B2 ** ADAM_STEP)
        d_ref[...] = -ADAM_LR * (m_hat / (jnp.sqrt(v_hat) + ADAM_EPS) + ADAM_WD * w_ref[...])
        nm_ref[...] = mn
        nv_ref[...] = vn

    blk = pl.BlockSpec((None, rows, C), lambda i: (0, i, 0))
    shp = jax.ShapeDtypeStruct((1, R, C), F32)
    return pl.pallas_call(
        kern, name=name, grid=(R // rows,),
        in_specs=[blk, pl.BlockSpec((rows, C), lambda i: (i, 0)), blk, blk], out_specs=[blk] * 3, out_shape=[shp] * 3,
        compiler_params=_params("parallel"),
    )(w, g, m, v)


def _sum_slots(slots, name):
    n, R, C = slots.shape
    SUM_ROWS = next(rows for rows in (128, 64, 32, 16) if R % rows == 0)

    def kern(s_ref, o_ref):
        acc = s_ref[0].astype(F32)
        for k in range(1, n):
            acc = acc + s_ref[k].astype(F32)
        o_ref[...] = acc

    return pl.pallas_call(
        kern, name=name, grid=(R // SUM_ROWS,),
        in_specs=[pl.BlockSpec((n, SUM_ROWS, C), lambda i: (0, i, 0))],
        out_specs=pl.BlockSpec((SUM_ROWS, C), lambda i: (i, 0)),
        out_shape=jax.ShapeDtypeStruct((R, C), F32),
        compiler_params=_params("parallel"),
    )(slots)


def _place():
    return lax.axis_index("x"), lax.axis_index("y"), lax.axis_index("c")


def _index(p):
    return 4 * p[0] + 2 * p[1] + p[2]


FLIPS = [(fx, fy, fc) for fx in (0, 1) for fy in (0, 1) for fc in (0, 1)][1:]


def _peer(me, flip):
    return tuple(1 - a if f else a for a, f in zip(me, flip))


def _gather_rows(shards, part=(0, 1), into=None):
    nw = len(shards)

    def plan(ins, outs, send_sems, recv_sems):
        x, y, c = me = _place()
        sibling = (x, y, 1 - c)
        chips = [(1 - x, y), (x, 1 - y), (1 - x, 1 - y)]

        def span(w):
            cnt = ins[w].shape[0] // part[1]
            return part[0] * cnt, cnt

        def rows(w, p):
            lo, cnt = span(w)
            return outs[w].at[pl.ds(_index(p) * ins[w].shape[0] + lo, cnt), :]

        def own(w):
            lo, cnt = span(w)
            return ins[w].at[pl.ds(lo, cnt), :]

        def copy(w, k, block, to):
            return pltpu.make_async_remote_copy(
                src_ref=own(w) if block is me else rows(w, block), dst_ref=rows(w, block),
                send_sem=send_sems.at[7 * w + k], recv_sem=recv_sems.at[7 * w + k],
                device_id=to, device_id_type=MESH)

        return me, sibling, chips, c, rows, own, copy

    def copies(ins, outs, send_sems, recv_sems, local_sems):
        me, sibling, chips, c, rows, own, copy = plan(ins, outs, send_sems, recv_sems)
        local = [pltpu.make_async_copy(own(w), rows(w, me), local_sems.at[w]) for w in range(nw)]
        sends, recvs = [], []
        for w in range(nw):
            sends.append(copy(w, 0, me, sibling))
            sends += [copy(w, 1 + j, me, (*chip, c)) for j, chip in enumerate(chips)]
            recvs.append(copy(w, 0, sibling, me))
            recvs += [copy(w, 4 + j, (*chip, 1 - c), me) for j, chip in enumerate(chips)]
        return local, sends, recvs

    def relay(ins, outs, send_sems, recv_sems, local_sems):
        me, sibling, chips, c, rows, own, copy = plan(ins, outs, send_sems, recv_sems)
        arrived = [copy(w, 1 + j, (*chip, c), me) for w in range(nw) for j, chip in enumerate(chips)]
        onward = [copy(w, 4 + j, (*chip, c), sibling) for w in range(nw) for j, chip in enumerate(chips)]
        return arrived, onward

    shapes = [jax.ShapeDtypeStruct((N_DEV * s.shape[0], s.shape[1]), s.dtype) for s in shards]
    aliases = {nw + w: w for w in range(nw)} if into else None
    return _Exchange(shards + (into or []), shapes, 7 * nw, nw, copies, aliases=aliases, relay=relay)


def _scatter_rows(parts, part=(0, 1)):
    nw = len(parts)

    def copies(ins, outs, send_sems, recv_sems, local_sems):
        me = _place()

        def src(w, owner):
            n = ins[w].shape[0] // N_DEV
            cnt = n // part[1]
            return ins[w].at[pl.ds(_index(owner) * n + part[0] * cnt, cnt), :]

        def copy(k, w, owner, sender, to):
            return pltpu.make_async_remote_copy(
                src_ref=src(w, owner), dst_ref=outs[w].at[_index(sender)],
                send_sem=send_sems.at[nw * k + w], recv_sem=recv_sems.at[nw * k + w],
                device_id=to, device_id_type=MESH)

        local = [pltpu.make_async_copy(src(w, me), outs[w].at[_index(me)], local_sems.at[w]) for w in range(nw)]
        peers = [_peer(me, flip) for flip in FLIPS]
        sends = [copy(k, w, peer, me, peer) for k, peer in enumerate(peers) for w in range(nw)]
        recvs = [copy(k, w, me, peer, me) for k, peer in enumerate(peers) for w in range(nw)]
        return local, sends, recvs

    shapes = [jax.ShapeDtypeStruct((N_DEV, p.shape[0] // N_DEV // part[1], p.shape[1]), p.dtype) for p in parts]
    return _Exchange(parts, shapes, 7 * nw, nw, copies)


def _sum_over_devices(v):
    shape = v.shape

    def body(v_ref, sum_ref, all_ref, send_sems, recv_sems):
        me = _place()
        all_ref[_index(me)] = v_ref[...]
        sends = []
        for k, flip in enumerate(FLIPS):
            peer = _peer(me, flip)
            sends.append(pltpu.make_async_remote_copy(
                src_ref=v_ref, dst_ref=all_ref.at[_index(me)],
                send_sem=send_sems.at[k], recv_sem=recv_sems.at[k], device_id=peer, device_id_type=MESH))
            sends[-1].start()
        for k, flip in enumerate(FLIPS):
            peer = _peer(me, flip)
            pltpu.make_async_remote_copy(
                src_ref=v_ref, dst_ref=all_ref.at[_index(peer)],
                send_sem=send_sems.at[k], recv_sem=recv_sems.at[k], device_id=peer, device_id_type=MESH).wait_recv()
        for cp in sends:
            cp.wait_send()
        acc = all_ref[0]
        for s in range(1, N_DEV):
            acc = acc + all_ref[s]
        sum_ref[...] = acc

    vmem = pl.BlockSpec(memory_space=pltpu.VMEM)
    return pl.pallas_call(
        body, name="sum_small_grads",
        in_specs=[vmem], out_specs=[vmem, vmem],
        out_shape=[jax.ShapeDtypeStruct(shape, F32), jax.ShapeDtypeStruct((N_DEV,) + shape, F32)],
        scratch_shapes=[pltpu.SemaphoreType.DMA((7,)), pltpu.SemaphoreType.DMA((7,))],
    )(v)[0]


SMALL_ROWS = 8


def _pack_small(vectors):
    padded = []
    for vec in vectors:
        vec = vec.reshape(-1)
        padded.append(jnp.pad(vec, (0, -vec.shape[0] % 128)))
    flat = jnp.concatenate(padded)
    flat = jnp.pad(flat, (0, -flat.shape[0] % (SMALL_ROWS * 128)))
    return flat.reshape(SMALL_ROWS, -1)


def _unpack_small(packed, shapes):
    flat = packed.reshape(-1)
    out, off = [], 0
    for shp in shapes:
        n = int(np.prod(shp))
        out.append(flat[off:off + n].reshape(shp))
        off += n + (-n % 128)
    return out


def kernel(x, g_attn, w_in, b_in, sinks_a, g_out_a, g_out_b, w_out, g_mlp, w_1, w_2, g_final, loss_target, m_g_attn, m_w_in, m_b_in, m_sinks_a, m_g_out_a, m_g_out_b, m_w_out, m_g_mlp, m_w_1, m_w_2, m_g_final, v_g_attn, v_w_in, v_b_in, v_sinks_a, v_g_out_a, v_g_out_b, v_w_out, v_g_mlp, v_w_1, v_w_2, v_g_final):
    xs, tgt = x[0], loss_target[0]
    T, D = xs.shape
    n_a = QW + 2 * KV_HEADS_A * HEAD_DIM
    g_fin = g_final.reshape(1, D)

    shards = [w_in[0].T.astype(BF16), w_out[0].astype(BF16), w_1[0].T.astype(BF16), w_2[0].astype(BF16)]
    ident = lambda acc: (acc,)
    add = lambda acc, other: (acc + other,)
    tiles = dict(tm=512, tn=1024)

    h1, w_in_t = _norm_fwd(xs, g_attn, "norm_attn", exchange=_gather_rows(shards[:1]))
    n_in = w_in_t.shape[0]
    proj_a, = _proj_views(h1, w_in_t, b_in, (0, n_a), [1], "proj_a")
    dils = [dil for _, dil in DILATED_BRANCHES]
    *proj_b, w_o = _proj_views(h1, w_in_t, b_in, (n_a, n_in - n_a), dils, "proj_b", exchange=_gather_rows(shards[1:2]))

    lay_a = _AttnLayout(1, KV_HEADS_A, 0, 0, 0, QW // (KV_HEADS_A * HEAD_DIM), QW // (KV_HEADS_A * HEAD_DIM) + 1)
    bias_a = _band_bias(WINDOW_A - 1, 1)
    o_a, l_a, w_1_t = _attn_fwd(proj_a, bias_a, sinks_a, lay_a, "attn_a_fwd",
                                exchange=_gather_rows(shards[2:3], part=(0, 4)))
    branches = []
    for n, (window, dil) in enumerate(DILATED_BRANCHES):
        lay = _AttnLayout(dil, N_HEADS, 3, 0, 3, 1, 2)
        bias = _band_bias(window // dil, dil)
        ride = _gather_rows(shards[2:3], part=(n + 1, 4), into=[w_1_t])
        o, lse, w_1_t = _attn_fwd(proj_b[n], bias, None, lay, f"attn_b{dil}_fwd", exchange=ride)
        branches.append((lay, bias, proj_b[n], o, lse))
    o_b = [br[3] for br in branches]
    l_b = [br[4].transpose(2, 0, 1).reshape(T, N_HEADS) for br in branches]

    mix = _mix_fwd(o_a, o_b, l_b, g_out_a, g_out_b, dils)
    wide = dict(tm=512, tn=2048)

    def residual_and_norm(acc, res, g):
        x_new = acc + res
        return x_new, (x_new * _rstd(x_new)) * g

    assert wide["tn"] == D
    x2, h2 = _matmul(mix, w_o, "nn", [F32, BF16], residual_and_norm, tk=D, tile_ins=[xs], row_ins=[g_mlp],
                     name="out_proj", **wide)

    def relu_sq(acc):
        u = jnp.maximum(acc, 0.0)
        return u, u * u

    u, u_sq, w_2_f = _matmul(h2, w_1_t, "nt", [BF16, BF16], relu_sq, tk=D, name="mlp_up",
                             exchange=_gather_rows(shards[3:]), **wide)
    x3, = _matmul(u_sq, w_2_f, "nn", [F32], add, tk=4096, tile_ins=[x2], name="mlp_down", **tiles)

    dx3, dx3_b, dg_final, loss_dev = _loss_head(x3, tgt, g_fin)

    d_pre, = _matmul(dx3_b, w_2_f, "nt", [BF16], lambda acc, uu: (acc * (2.0 * uu.astype(F32)),),
                     tk=D, tile_ins=[u], name="mlp_down_bwd", **wide)
    wtiles = dict(tm=1024, tn=1024, tk=4096)
    dw_2, = _matmul(u_sq, dx3_b, "tn", [BF16], ident, name="mlp_down_wgrad", **wtiles)
    dh2, slots_2a = _matmul(d_pre, w_1_t, "nn", [BF16], ident, tk=4096, name="mlp_up_bwd",
                            exchange=_scatter_rows([dw_2], part=(0, 2)), **tiles)
    dw_1_t, slots_2b = _matmul(d_pre, h2, "tn", [BF16], ident, name="mlp_up_wgrad",
                               exchange=_scatter_rows([dw_2], part=(1, 2)), **wtiles)
    dx2, dg_mlp, dx2_b, dmix = _norm_bwd(dh2, x2, g_mlp, dx3, "norm_mlp_bwd", then_w_t=w_o)
    dw_o, = _matmul(mix, dx2_b, "tn", [BF16], ident, name="out_proj_wgrad", **wtiles)
    do_a, dd_a, do1, do2, do3, dd1, dd2, dd3, dg_out_a, dg_out_b, slots_o = _mix_bwd(
        dmix, o_a, o_b, l_b, g_out_a, g_out_b, dils, exchange=_scatter_rows([dw_o]))

    by_class = lambda d, dil: d.reshape(T // dil, dil, N_HEADS).transpose(1, 2, 0)
    slots_1 = [None] * 4
    dq_a, dk_a, dv_a, dsinks, slots_1[0] = _attn_bwd(proj_a, do_a, l_a, by_class(dd_a, 1), bias_a, sinks_a, lay_a,
                                                     "attn_a_bwd", exchange=_scatter_rows([dw_1_t], part=(0, 4)))
    dsinks = dsinks[:, 0].reshape(1, N_HEADS)
    dqs, dks, dvs = [], [], []
    for n, ((lay, bias, view, _, lse), do_n, dd_n) in enumerate(zip(branches, (do1, do2, do3), (dd1, dd2, dd3))):
        dq, dk, dv, slots_1[n + 1] = _attn_bwd(view, do_n, lse, by_class(dd_n, lay.dil), bias, None, lay,
                                               f"attn_b{lay.dil}_bwd",
                                               exchange=_scatter_rows([dw_1_t], part=(n + 1, 4)))
        dqs.append(dq)
        dks.append(dk)
        dvs.append(dv)
    dproj, db_in = _assemble([[dq_a], [dk_a], [dv_a], dqs, dks, dvs], "dproj", dils)

    dw_in_t, = _matmul(dproj, h1, "tn", [BF16], ident, tm=n_in // 2, tn=1024, tk=1024, name="in_proj_wgrad")
    dh1, slots_in = _matmul(dproj, w_in_t, "nn", [BF16], ident, tk=n_in, name="in_proj_bwd",
                            exchange=_scatter_rows([dw_in_t]), **tiles)
    dx, dg_attn = _norm_bwd(dh1, xs, g_attn, dx2, "norm_attn_bwd")

    g_w_in = _sum_slots(slots_in, "sum_w_in_grads").T
    g_w_out = _sum_slots(slots_o, "sum_w_out_grads")
    g_w_1 = jnp.concatenate([_sum_slots(s, f"sum_w_1_grads_{n}") for n, s in enumerate(slots_1)]).T
    g_w_2 = jnp.concatenate([_sum_slots(slots_2a, "sum_w_2_grads_0"), _sum_slots(slots_2b, "sum_w_2_grads_1")])

    small_w = [g_attn, b_in, sinks_a, g_out_a, g_out_b, g_mlp, g_final]
    small_m = [m_g_attn, m_b_in, m_sinks_a, m_g_out_a, m_g_out_b, m_g_mlp, m_g_final]
    small_v = [v_g_attn, v_b_in, v_sinks_a, v_g_out_a, v_g_out_b, v_g_mlp, v_g_final]
    small_g = [dg_attn, db_in, dsinks, dg_out_a, dg_out_b, dg_mlp, dg_final]
    summed = _sum_over_devices(_pack_small(small_g + [loss_dev[:, :1]]))
    shapes = [w.shape for w in small_w]
    *g_small, loss = _unpack_small(summed, shapes + [()])

    big = [
        _adamw(w_in, g_w_in, m_w_in, v_w_in, "adamw_w_in"),
        _adamw(w_out, g_w_out, m_w_out, v_w_out, "adamw_w_out"),
        _adamw(w_1, g_w_1, m_w_1, v_w_1, "adamw_w_1"),
        _adamw(w_2, g_w_2, m_w_2, v_w_2, "adamw_w_2"),
    ]
    g_packed = _pack_small(g_small)
    small = _adamw(_pack_small(small_w)[None], g_packed, _pack_small(small_m)[None], _pack_small(small_v)[None],
                   "adamw_small")
    small = [_unpack_small(s, shapes) for s in small]

    def ordered(small_list, big_list):
        s = list(small_list)
        return [s[0], big_list[0], s[1], s[2], s[3], s[4], big_list[1], s[5], big_list[2], big_list[3], s[6]]

    grads = ordered(g_small, [g[None] for g in (g_w_in, g_w_out, g_w_1, g_w_2)])
    deltas = ordered(small[0], [b[0] for b in big])
    new_m = ordered(small[1], [b[1] for b in big])
    new_v = ordered(small[2], [b[2] for b in big])
    return (loss, dx[None], *grads, *deltas, *new_m, *new_v)
```

```python
import numpy as np
import jax
import jax.numpy as jnp
from jax import lax
from jax.experimental import pallas as pl
from jax.experimental.pallas import tpu as pltpu

F32 = jnp.float32
BF16 = jnp.bfloat16

HEAD_DIM = 64
N_HEADS = 16
KV_HEADS_A = 2
BLOCK = 128
WINDOW_A = 128
DILATED_BRANCHES = ((128, 1), (512, 4), (2048, 16))
EPS = 1e-5
NEG_INF = -1e30
N_DEV = 8

ADAM_LR = 0.001
ADAM_B1 = 0.9
ADAM_B2 = 0.999
ADAM_EPS = 1e-08
ADAM_WD = 0.01
ADAM_STEP = 10

VMEM_LIMIT_BYTES = 56 * 1024 * 1024
MESH = pl.DeviceIdType.MESH
ANY = pl.BlockSpec(memory_space=pl.ANY)

NN = (((1,), (0,)), ((), ()))
NT = (((1,), (1,)), ((), ()))
TN = (((0,), (0,)), ((), ()))


def _dot(a, b, dims):
    return lax.dot_general(a, b, dims, preferred_element_type=F32)


def _params(*sem):
    return pltpu.CompilerParams(dimension_semantics=sem, vmem_limit_bytes=VMEM_LIMIT_BYTES)


RELAY_AT = 0.6


class _Exchange:
    def __init__(self, ins, out_shapes, n_remote, n_local, copies, aliases=None, relay=None, relay_at=RELAY_AT):
        self.ins, self.out_shapes = list(ins), list(out_shapes)
        self.n_remote, self.n_local = n_remote, n_local
        self.copies = copies
        self.relay = relay
        self.relay_at = relay_at
        self.aliases = aliases or {}

    def start(self, refs):
        local, sends, _ = self.copies(*refs)
        for cp in local + sends:
            cp.start()

    def middle(self, refs):
        arrived, onward = self.relay(*refs)
        for got, cp in zip(arrived, onward):
            got.wait_recv()
            cp.start()

    def finish(self, refs):
        local, sends, recvs = self.copies(*refs)
        for cp in recvs:
            cp.wait_recv()
        for cp in sends:
            cp.wait_send()
        for cp in local:
            cp.wait()
        if self.relay:
            for cp in self.relay(*refs)[1]:
                cp.wait_send()


class _Ride:
    def __init__(self, ex, n_in, n_out, n_scratch):
        self.ex = ex
        self.n = (n_in, n_out, n_scratch)
        self.args = ex.ins if ex else []
        self.in_specs = [ANY] * len(self.args)
        self.out_shapes = ex.out_shapes if ex else []
        self.out_specs = [ANY] * len(self.out_shapes)
        self.scratch = [pltpu.SemaphoreType.DMA((ex.n_remote,)), pltpu.SemaphoreType.DMA((ex.n_remote,)),
                        pltpu.SemaphoreType.DMA((max(ex.n_local, 1),))] if ex else []
        self.aliases = {n_in + i: n_out + o for i, o in ex.aliases.items()} if ex else {}

    def split(self, refs):
        n_in, n_out, n_scratch = self.n
        a = n_in
        b = a + len(self.args)
        c = b + n_out
        d = c + len(self.out_shapes)
        e = d + n_scratch
        return refs[:a], refs[b:c], refs[d:e], (refs[a:b], refs[c:d], *refs[e:])

    def around(self, step, n_steps, exrefs, compute):
        if self.ex is None:
            compute()
            return

        @pl.when(step == 0)
        def _():
            self.ex.start(exrefs)

        compute()

        if self.ex.relay:
            @pl.when(step == int(self.ex.relay_at * (n_steps - 1)))
            def _():
                self.ex.middle(exrefs)

        @pl.when(step == n_steps - 1)
        def _():
            self.ex.finish(exrefs)


def _matmul(a, b, dims, out_dtypes, epilogue, *, tm, tn, tk, name, tile_ins=(), row_ins=(), exchange=None):
    if dims == "tn":
        K, M = a.shape
    else:
        M, K = a.shape
    N = b.shape[0] if dims == "nt" else b.shape[1]
    tm, tn, tk = min(tm, M), min(tn, N), min(tk, K)
    assert M % tm == 0 and N % tn == 0 and K % tk == 0, (name, M, N, K, tm, tn, tk)
    grid = (M // tm, N // tn, K // tk)
    nk = grid[2]
    n_tile, n_row, n_out = len(tile_ins), len(row_ins), len(out_dtypes)
    dn = {"nn": NN, "nt": NT, "tn": TN}[dims]
    ride = _Ride(exchange, 2 + n_tile + n_row, n_out, 1 if nk > 1 else 0)

    def kern(*refs):
        ins, out_refs, scratch, exrefs = ride.split(refs)
        a_ref, b_ref = ins[:2]
        tile_refs = ins[2:2 + n_tile]
        row_refs = ins[2 + n_tile:]
        ids = [pl.program_id(d) for d in range(3)]

        def finish(acc):
            outs = epilogue(acc, *[r[...] for r in tile_refs], *[r[...] for r in row_refs])
            for o_ref, o in zip(out_refs, outs):
                o_ref[...] = o.astype(o_ref.dtype)

        def compute():
            if nk == 1:
                finish(_dot(a_ref[...], b_ref[...], dn))
                return
            acc_ref = scratch[0]

            @pl.when(ids[2] == 0)
            def _():
                acc_ref[...] = jnp.zeros_like(acc_ref)

            acc_ref[...] += _dot(a_ref[...], b_ref[...], dn)

            @pl.when(ids[2] == nk - 1)
            def _():
                finish(acc_ref[...])

        ride.around((ids[0] * grid[1] + ids[1]) * grid[2] + ids[2], grid[0] * grid[1] * grid[2], exrefs, compute)

    if dims == "tn":
        a_spec = pl.BlockSpec((tk, tm), lambda i, j, k: (k, i))
    else:
        a_spec = pl.BlockSpec((tm, tk), lambda i, j, k: (i, k))
    if dims == "nt":
        b_spec = pl.BlockSpec((tn, tk), lambda i, j, k: (j, k))
    else:
        b_spec = pl.BlockSpec((tk, tn), lambda i, j, k: (k, j))
    tile_spec = pl.BlockSpec((tm, tn), lambda i, j, k: (i, j))
    row_spec = pl.BlockSpec((1, tn), lambda i, j, k: (0, j))
    sem = ("arbitrary",) * 3 if exchange else ("parallel", "parallel", "arbitrary")
    return pl.pallas_call(
        kern,
        name=name,
        grid=grid,
        in_specs=[a_spec, b_spec] + [tile_spec] * n_tile + [row_spec] * n_row + ride.in_specs,
        out_specs=[tile_spec] * n_out + ride.out_specs,
        out_shape=[jax.ShapeDtypeStruct((M, N), dt) for dt in out_dtypes] + ride.out_shapes,
        scratch_shapes=([pltpu.VMEM((tm, tn), F32)] if nk > 1 else []) + ride.scratch,
        input_output_aliases=ride.aliases,
        compiler_params=_params(*sem),
    )(a, b, *tile_ins, *row_ins, *ride.args)


PROJ_ROWS = 256


def _proj_views(a, w_t, bias, cols, dils, name, exchange=None):
    T, K = a.shape
    first, N = cols
    ride = _Ride(exchange, 3, len(dils), 1)

    def kern(*refs):
        (a_ref, w_ref, b_ref), outs, (scr,), exrefs = ride.split(refs)

        def compute():
            acc = _dot(a_ref[...], w_ref[...], NT) + b_ref[...]
            for out_ref, dil in zip(outs, dils):
                _to_class_order(acc, scr, out_ref, dil)

        ride.around(pl.program_id(0), T // PROJ_ROWS, exrefs, compute)

    return pl.pallas_call(
        kern, name=name, grid=(T // PROJ_ROWS,),
        in_specs=[pl.BlockSpec((PROJ_ROWS, K), lambda i: (i, 0)),
                  pl.BlockSpec((pl.Element(N), pl.Element(K)), lambda i: (first, 0)),
                  pl.BlockSpec((pl.Element(1), pl.Element(N)), lambda i: (0, first))] + ride.in_specs,
        out_specs=[_view_spec(PROJ_ROWS, N, d) for d in dils] + ride.out_specs,
        out_shape=[jax.ShapeDtypeStruct((T // d, d * N), BF16) for d in dils] + ride.out_shapes,
        scratch_shapes=[_regroup_scratch(PROJ_ROWS, N)] + ride.scratch,
        input_output_aliases=ride.aliases,
        compiler_params=_params("arbitrary"),
    )(a, w_t, bias, *ride.args)


ROWS = 256
MIX_ROWS = 128


def _rstd(xv):
    return lax.rsqrt(jnp.mean(xv * xv, axis=-1, keepdims=True) + EPS)


def _norm_fwd(x, g, name, exchange=None):
    T, D = x.shape
    ride = _Ride(exchange, 2, 1, 0)

    def kern(*refs):
        (x_ref, g_ref), (h_ref,), _, exrefs = ride.split(refs)

        def compute():
            xv = x_ref[...]
            h_ref[...] = ((xv * _rstd(xv)) * g_ref[...]).astype(h_ref.dtype)

        ride.around(pl.program_id(0), T // ROWS, exrefs, compute)

    row = pl.BlockSpec((ROWS, D), lambda i: (i, 0))
    return pl.pallas_call(
        kern, name=name, grid=(T // ROWS,),
        in_specs=[row, pl.BlockSpec((1, D), lambda i: (0, 0))] + ride.in_specs,
        out_specs=[row] + ride.out_specs,
        out_shape=[jax.ShapeDtypeStruct((T, D), BF16)] + ride.out_shapes,
        scratch_shapes=ride.scratch, input_output_aliases=ride.aliases,
        compiler_params=_params("arbitrary"),
    )(x, g, *ride.args)


def _norm_bwd(dh, x, g, res, name, then_w_t=None):
    T, D = x.shape
    rows = PROJ_ROWS if then_w_t is not None else ROWS

    def kern(dh_ref, x_ref, g_ref, res_ref, *rest):
        if then_w_t is None:
            dx_ref, dg_ref = rest
        else:
            w_ref, dx_ref, dg_ref, dxb_ref, y_ref = rest

        @pl.when(pl.program_id(0) == 0)
        def _():
            dg_ref[...] = jnp.zeros_like(dg_ref)

        xv = x_ref[...]
        r = _rstd(xv)
        xn = xv * r
        dhv = dh_ref[...].astype(F32)
        dg_ref[...] += jnp.sum(dhv * xn, axis=0, keepdims=True)
        t = dhv * g_ref[...]
        dx = res_ref[...] + r * (t - xn * jnp.mean(t * xn, axis=-1, keepdims=True))
        dx_ref[...] = dx
        if then_w_t is not None:
            dxb = dx.astype(BF16)
            dxb_ref[...] = dxb
            y_ref[...] = _dot(dxb, w_ref[...], NT)

    row = pl.BlockSpec((rows, D), lambda i: (i, 0))
    vec = pl.BlockSpec((1, D), lambda i: (0, 0))
    in_specs, args = [row, row, vec, row], [dh, x, g, res]
    out_specs = [row, vec]
    out_shape = [jax.ShapeDtypeStruct((T, D), F32), jax.ShapeDtypeStruct((1, D), F32)]
    if then_w_t is not None:
        N = then_w_t.shape[0]
        in_specs.append(pl.BlockSpec((N, D), lambda i: (0, 0)))
        args.append(then_w_t)
        out_specs += [row, pl.BlockSpec((rows, N), lambda i: (i, 0))]
        out_shape += [jax.ShapeDtypeStruct((T, D), BF16), jax.ShapeDtypeStruct((T, N), F32)]
    return pl.pallas_call(
        kern, name=name, grid=(T // rows,), in_specs=in_specs, out_specs=out_specs, out_shape=out_shape,
        compiler_params=_params("arbitrary"),
    )(*args)


def _loss_head(x3, tgt, g):
    T, D = x3.shape

    def kern(x_ref, t_ref, g_ref, dx_ref, dxb_ref, dg_ref, loss_ref):
        @pl.when(pl.program_id(0) == 0)
        def _():
            dg_ref[...] = jnp.zeros_like(dg_ref)
            loss_ref[...] = jnp.zeros_like(loss_ref)

        xv = x_ref[...]
        gv = g_ref[...]
        r = _rstd(xv)
        xn = xv * r
        err = xn * gv - t_ref[...]
        per_tok = jnp.mean(err * err, axis=-1, keepdims=True)
        loss_ref[...] += 0.5 * jnp.sum(per_tok, axis=0, keepdims=True)
        dy = err * (1.0 / D)
        dg_ref[...] += jnp.sum(dy * xn, axis=0, keepdims=True)
        t = dy * gv
        dx = r * (t - xn * jnp.mean(t * xn, axis=-1, keepdims=True))
        dx_ref[...] = dx
        dxb_ref[...] = dx.astype(BF16)

    row = pl.BlockSpec((ROWS, D), lambda i: (i, 0))
    vec = pl.BlockSpec((1, D), lambda i: (0, 0))
    return pl.pallas_call(
        kern, name="loss_head", grid=(T // ROWS,),
        in_specs=[row, row, vec],
        out_specs=[row, row, vec, pl.BlockSpec((1, 128), lambda i: (0, 0))],
        out_shape=[jax.ShapeDtypeStruct((T, D), F32), jax.ShapeDtypeStruct((T, D), BF16),
                   jax.ShapeDtypeStruct((1, D), F32), jax.ShapeDtypeStruct((1, 128), F32)],
        compiler_params=_params("arbitrary"),
    )(x3, tgt, g)


def _spread_matrix():
    head_of_lane = np.arange(N_HEADS * HEAD_DIM) // HEAD_DIM
    return jnp.asarray(np.arange(N_HEADS)[:, None] == head_of_lane[None, :], dtype=BF16)


def _pieces(v, n):
    out = []
    for _ in range(n):
        piece = v.astype(BF16)
        out.append(piece)
        v = v - piece.astype(F32)
    return out


def _spread(v, spread):
    return sum(_dot(p, spread, NN) for p in _pieces(v, 2))


def _spread_weights(w1, w2, spread):
    s1, s2 = _spread(w1, spread), _spread(w2, spread)
    return s1, s2, 1.0 - s1 - s2


def _head_sums(v, spread):
    return sum(_dot(p, spread, NT) for p in _pieces(v, 2))


def _branch_weights(l1, l2, l3):
    lm = jnp.maximum(jnp.maximum(l1, l2), l3)
    e1, e2, e3 = jnp.exp(l1 - lm), jnp.exp(l2 - lm), jnp.exp(l3 - lm)
    inv = 1.0 / (e1 + e2 + e3)
    return e1 * inv, e2 * inv, e3 * inv


def _regroup_scratch(rows, width):
    return pltpu.VMEM((width // LANES, rows, LANES), F32)


def _to_token_order(view_ref, scr, dil):
    if dil == 1:
        return view_ref[...].astype(F32)
    n_l, w = view_ref.shape[0], view_ref.shape[1] // dil
    for r in range(dil):
        for cb in range(w // LANES):
            scr[cb, pl.ds(r, n_l, stride=dil), :] = view_ref[:, r * w + cb * LANES:r * w + (cb + 1) * LANES].astype(F32)
    return jnp.concatenate([scr[cb] for cb in range(w // LANES)], axis=1)


def _to_class_order(val, scr, view_ref, dil):
    if dil == 1:
        view_ref[...] = val.astype(view_ref.dtype)
        return
    n, w = val.shape
    for cb in range(w // LANES):
        scr[cb] = val[:, cb * LANES:(cb + 1) * LANES]
    for r in range(dil):
        for cb in range(w // LANES):
            view_ref[:, r * w + cb * LANES:r * w + (cb + 1) * LANES] = (
                scr[cb, pl.ds(r, n // dil, stride=dil), :].astype(view_ref.dtype))


def _view_spec(rows, width, dil):
    return pl.BlockSpec((rows // dil, dil * width), lambda i: (i, 0))


def _mix_fwd(oa, obs, lbs, ga, gb, dils):
    T, W = oa.shape

    def kern(oa_ref, o1, o2, o3, l1, l2, l3, ga_ref, gb_ref, sp_ref, mix_ref, *scr):
        sp = sp_ref[...]
        w1, w2, w3 = _branch_weights(l1[...], l2[...], l3[...])
        on = [_to_token_order(o, s, d) for o, s, d in zip((o1, o2, o3), scr, dils)]
        s1, s2, s3 = _spread_weights(w1, w2, sp)
        ob = s1 * on[0] + s2 * on[1] + s3 * on[2]
        oav = oa_ref[...]
        mix_ref[:, :W] = ((oav * _rstd(oav)) * ga_ref[...]).astype(BF16)
        mix_ref[:, W:] = ((ob * _rstd(ob)) * gb_ref[...]).astype(BF16)

    row = pl.BlockSpec((MIX_ROWS, W), lambda i: (i, 0))
    per_head = pl.BlockSpec((MIX_ROWS, N_HEADS), lambda i: (i, 0))
    vec = pl.BlockSpec((1, W), lambda i: (0, 0))
    return pl.pallas_call(
        kern, name="mix_fwd", grid=(T // MIX_ROWS,),
        in_specs=[row] + [_view_spec(MIX_ROWS, W, d) for d in dils] + [per_head] * 3
        + [vec, vec, pl.BlockSpec((N_HEADS, W), lambda i: (0, 0))],
        out_specs=pl.BlockSpec((MIX_ROWS, 2 * W), lambda i: (i, 0)),
        out_shape=jax.ShapeDtypeStruct((T, 2 * W), BF16),
        scratch_shapes=[_regroup_scratch(MIX_ROWS, W)] * 3,
        compiler_params=_params("parallel"),
    )(oa, *obs, *lbs, ga, gb, _spread_matrix())


def _mix_bwd(dmix, oa, obs, lbs, ga, gb, dils, exchange=None):
    T, W = oa.shape
    ride = _Ride(exchange, 11, 10, 3)

    def kern(*refs):
        ins, outs, scr, exrefs = ride.split(refs)
        ride.around(pl.program_id(0), T // MIX_ROWS, exrefs, lambda: compute(*ins, *outs, *scr))

    def compute(dm_ref, oa_ref, o1, o2, o3, l1, l2, l3, ga_ref, gb_ref, sp_ref,
                doa_ref, da_ref, do1, do2, do3, d1, d2, d3, dga_ref, dgb_ref, *scr):
        @pl.when(pl.program_id(0) == 0)
        def _():
            dga_ref[...] = jnp.zeros_like(dga_ref)
            dgb_ref[...] = jnp.zeros_like(dgb_ref)

        sp = sp_ref[...]
        oav = oa_ref[...]
        r = _rstd(oav)
        on = oav * r
        dy = dm_ref[:, :W]
        dga_ref[...] += jnp.sum(dy * on, axis=0, keepdims=True)
        t = dy * ga_ref[...]
        doa = r * (t - on * jnp.mean(t * on, axis=-1, keepdims=True))
        doa_ref[...] = doa.astype(BF16)
        da_ref[...] = _head_sums(doa * oav, sp)
        w1, w2, w3 = _branch_weights(l1[...], l2[...], l3[...])
        s1, s2, s3 = _spread_weights(w1, w2, sp)
        on = [_to_token_order(o, sc, d) for o, sc, d in zip((o1, o2, o3), scr, dils)]
        ob = s1 * on[0] + s2 * on[1] + s3 * on[2]
        r = _rstd(ob)
        on = ob * r
        dy = dm_ref[:, W:]
        dgb_ref[...] += jnp.sum(dy * on, axis=0, keepdims=True)
        t = dy * gb_ref[...]
        dob = r * (t - on * jnp.mean(t * on, axis=-1, keepdims=True))
        c = _head_sums(dob * ob, sp)
        for do_ref, sn, sc, d in zip((do1, do2, do3), (s1, s2, s3), scr, dils):
            _to_class_order(sn * dob, sc, do_ref, d)
        d1[...] = w1 * c
        d2[...] = w2 * c
        d3[...] = w3 * c

    row = pl.BlockSpec((MIX_ROWS, W), lambda i: (i, 0))
    per_head = pl.BlockSpec((MIX_ROWS, N_HEADS), lambda i: (i, 0))
    vec = pl.BlockSpec((1, W), lambda i: (0, 0))
    bf = jax.ShapeDtypeStruct((T, W), BF16)
    ph = jax.ShapeDtypeStruct((T, N_HEADS), F32)
    vv = jax.ShapeDtypeStruct((1, W), F32)
    views = [_view_spec(MIX_ROWS, W, d) for d in dils]
    return pl.pallas_call(
        kern, name="mix_bwd", grid=(T // MIX_ROWS,),
        in_specs=[pl.BlockSpec((MIX_ROWS, 2 * W), lambda i: (i, 0)), row] + views + [per_head] * 3 + [vec, vec,
                  pl.BlockSpec((N_HEADS, W), lambda i: (0, 0))] + ride.in_specs,
        out_specs=[row, per_head] + views + [per_head, per_head, per_head, vec, vec] + ride.out_specs,
        out_shape=[bf, ph] + [jax.ShapeDtypeStruct(o.shape, F32) for o in obs] + [ph, ph, ph, vv, vv]
        + ride.out_shapes,
        scratch_shapes=[_regroup_scratch(MIX_ROWS, W)] * 3 + ride.scratch,
        input_output_aliases=ride.aliases,
        compiler_params=_params("arbitrary"),
    )(dmix, oa, *obs, *lbs, ga, gb, _spread_matrix(), *ride.args)


def _alibi_slopes(n):
    return np.asarray(2.0 ** (-8.0 * (np.arange(n) + 1) / n)).astype(np.float32)


def _band_bias(max_steps, step_dist):
    qi = np.arange(BLOCK)[None, :]
    kj = np.arange(BLOCK)[:, None]
    slopes = _alibi_slopes(N_HEADS)
    halves = []
    for steps in (qi + BLOCK - kj, qi - kj):
        valid = (steps >= 0) & (steps <= max_steps)
        alibi = slopes[:, None, None] * (step_dist * steps).astype(np.float32)[None]
        halves.append(np.where(valid[None], -alibi, np.float32(NEG_INF)).astype(np.float32))
    per_head = np.concatenate(halves, axis=1)
    return jnp.asarray(np.concatenate([per_head[0::2], per_head[1::2]], axis=2))


class _AttnLayout:
    def __init__(self, dil, kv_heads, q_stride, q_off, k_stride, k_off, v_off):
        self.dil = dil
        self.kv_heads = kv_heads
        self.kw = kv_heads * HEAD_DIM
        self.rep = N_HEADS // kv_heads
        self.q_col = lambda r: r * q_stride + q_off
        self.k_col = lambda r: r * k_stride + k_off
        self.v_col = lambda r: r * k_stride + v_off


QW = N_HEADS * HEAD_DIM
LANES = 128


PAIRS = N_HEADS // 2


def _pair_cols(pair):
    return slice(pair * LANES, (pair + 1) * LANES)


def _first_head_lanes(shape):
    return lax.broadcasted_iota(jnp.int32, shape, 1) < HEAD_DIM


def _split_heads(pair):
    first = _first_head_lanes(pair.shape)
    zero = jnp.zeros_like(pair)
    return jnp.concatenate([jnp.where(first, pair, zero), jnp.where(first, zero, pair)], axis=0)


def _kv_pair(ref, pair, rep):
    if rep == 1:
        return ref[:, _pair_cols(pair)]
    blk = ref[...].astype(F32)
    other = pltpu.roll(blk, HEAD_DIM, 1)
    first = _first_head_lanes(blk.shape)
    both = jnp.where(first, blk, other) if (2 * pair // rep) % 2 == 0 else jnp.where(first, other, blk)
    return both.astype(ref.dtype)


def _paired_kv(prev_ref, cur_ref, rep, transposed=False):
    memo = {}

    def get(pair):
        key = pair if rep == 1 else 2 * pair // rep
        if key not in memo:
            blocks = [_kv_pair(ref, pair, rep) for ref in (prev_ref, cur_ref)]
            memo[key] = jnp.concatenate([b.T for b in blocks], axis=1) if transposed else jnp.concatenate(blocks, axis=0)
        return memo[key]

    return get


def _attn_fwd(proj, bias, sinks, lay, name, exchange=None):
    L = proj.shape[0]
    nb = L // BLOCK
    kw, rep = lay.kw, lay.rep
    use_sinks = sinks is not None
    scale = HEAD_DIM ** -0.5
    ride = _Ride(exchange, 7 if use_sinks else 6, 2, 2)

    def kern(*refs):
        ins, (o_ref, l_ref), (sc_ref, pr_ref), exrefs = ride.split(refs)
        q_ref, kc_ref, kp_ref, vc_ref, vp_ref, b_ref = ins[:6]
        s_ref = ins[6] if use_sinks else None
        r, i = pl.program_id(0), pl.program_id(1)
        first = i == 0
        ride.around(r * nb + i, lay.dil * nb, exrefs,
                    lambda: compute(q_ref, kc_ref, kp_ref, vc_ref, vp_ref, b_ref, s_ref, o_ref, l_ref, first,
                                    sc_ref, pr_ref))

    def compute(q_ref, kc_ref, kp_ref, vc_ref, vp_ref, b_ref, s_ref, o_ref, l_ref, first, sc_ref, pr_ref):
        keys, values_t = _paired_kv(kp_ref, kc_ref, rep), _paired_kv(vp_ref, vc_ref, rep, transposed=True)
        for pair in range(PAIRS):
            qs = _split_heads(q_ref[:, _pair_cols(pair)])
            s_prev = _dot(keys(pair)[:BLOCK], qs, NT) * scale + b_ref[pair, :BLOCK]
            sc_ref[pair, :BLOCK] = jnp.where(first, NEG_INF, s_prev)
            sc_ref[pair, BLOCK:] = _dot(keys(pair)[BLOCK:], qs, NT) * scale + b_ref[pair, BLOCK:]
        inv = []
        for h in range(N_HEADS):
            cols = slice(h % 2 * BLOCK, (h % 2 + 1) * BLOCK)
            s = sc_ref[h // 2, :, cols]
            m = jnp.max(s, axis=0, keepdims=True)
            if use_sinks:
                sink = s_ref[:, h:h + 1]
                m = jnp.maximum(m, sink)
            p = jnp.exp(s - m)
            denom = jnp.sum(p, axis=0, keepdims=True)
            if use_sinks:
                denom = denom + jnp.exp(sink - m)
            pr_ref[h // 2, :, cols] = p.astype(BF16)
            l_ref[h:h + 1, :] = m + jnp.log(denom)
            inv.append(1.0 / denom)
        for pair in range(PAIRS):
            both = _dot(values_t(pair), pr_ref[pair], NN)
            o_t = jnp.concatenate([both[:HEAD_DIM, :BLOCK] * inv[2 * pair], both[HEAD_DIM:, BLOCK:] * inv[2 * pair + 1]],
                                  axis=0)
            o_ref[:, _pair_cols(pair)] = o_t.T

    prev = lambda i: jnp.maximum(i - 1, 0)
    in_specs = [
        pl.BlockSpec((BLOCK, QW), lambda r, i: (i, lay.q_col(r))),
        pl.BlockSpec((BLOCK, kw), lambda r, i: (i, lay.k_col(r))),
        pl.BlockSpec((BLOCK, kw), lambda r, i: (prev(i), lay.k_col(r))),
        pl.BlockSpec((BLOCK, kw), lambda r, i: (i, lay.v_col(r))),
        pl.BlockSpec((BLOCK, kw), lambda r, i: (prev(i), lay.v_col(r))),
        pl.BlockSpec((PAIRS, 2 * BLOCK, 2 * BLOCK), lambda r, i: (0, 0, 0)),
    ]
    args = [proj, proj, proj, proj, proj, bias]
    if use_sinks:
        in_specs.append(pl.BlockSpec((1, N_HEADS), lambda r, i: (0, 0)))
        args.append(sinks)
    out_specs = [pl.BlockSpec((BLOCK, QW), lambda r, i: (i, r)),
                 pl.BlockSpec((None, N_HEADS, BLOCK), lambda r, i: (r, 0, i))]
    out_shape = [jax.ShapeDtypeStruct((L, lay.dil * QW), F32), jax.ShapeDtypeStruct((lay.dil, N_HEADS, L), F32)]
    return pl.pallas_call(
        kern, name=name, grid=(lay.dil, nb),
        in_specs=in_specs + ride.in_specs, out_specs=out_specs + ride.out_specs,
        out_shape=out_shape + ride.out_shapes,
        scratch_shapes=[pltpu.VMEM((PAIRS, 2 * BLOCK, 2 * BLOCK), dt) for dt in (F32, BF16)] + ride.scratch,
        input_output_aliases=ride.aliases,
        compiler_params=_params("arbitrary", "arbitrary"),
    )(*args, *ride.args)


def _attn_bwd(proj, do, lse, dd, bias, sinks, lay, name, exchange=None):
    L = proj.shape[0]
    nb = L // BLOCK
    kw, rep = lay.kw, lay.rep
    assert rep == 1 or lay.kv_heads == 2, "grouped queries: the two kv heads fill one 128-lane block"
    use_sinks = sinks is not None
    scale = HEAD_DIM ** -0.5
    ride = _Ride(exchange, 10 if use_sinks else 9, 4 if use_sinks else 3, 6)

    def kern(*refs):
        ins, outs, (ck_ref, cv_ref, *staged), exrefs = ride.split(refs)
        q_ref, kc_ref, kp_ref, vc_ref, vp_ref, do_ref, l_ref, d_ref, b_ref = ins[:9]
        s_ref = ins[9] if use_sinks else None
        dq_ref, dk_ref, dv_ref = outs[:3]
        ds_ref = outs[3] if use_sinks else None
        r = pl.program_id(0)
        i = pl.program_id(1)
        ride.around(r * (nb + 1) + i, lay.dil * (nb + 1), exrefs,
                    lambda: compute(q_ref, kc_ref, kp_ref, vc_ref, vp_ref, do_ref, l_ref, d_ref, b_ref, s_ref,
                                    dq_ref, dk_ref, dv_ref, ds_ref, ck_ref, cv_ref, r, i, *staged))

    def compute(q_ref, kc_ref, kp_ref, vc_ref, vp_ref, do_ref, l_ref, d_ref, b_ref, s_ref,
                dq_ref, dk_ref, dv_ref, ds_ref, ck_ref, cv_ref, r, i, sc_ref, dp_ref, pr_ref, dsc_ref):
        first = i == 0

        @pl.when(first)
        def _():
            ck_ref[...] = jnp.zeros_like(ck_ref)
            cv_ref[...] = jnp.zeros_like(cv_ref)

        if use_sinks:
            @pl.when(first & (r == 0))
            def _():
                ds_ref[...] = jnp.zeros_like(ds_ref)

        @pl.when(i < nb)
        def _():
            keys, values = _paired_kv(kp_ref, kc_ref, rep), _paired_kv(vp_ref, vc_ref, rep)
            keys_t = _paired_kv(kp_ref, kc_ref, rep, transposed=True)
            for pair in range(PAIRS):
                qs = _split_heads(q_ref[:, _pair_cols(pair)])
                dos = _split_heads(do_ref[:, _pair_cols(pair)].astype(BF16))
                s = _dot(keys(pair), qs, NT) * scale + b_ref[pair]
                sc_ref[pair, :BLOCK] = jnp.where(first, NEG_INF, s[:BLOCK])
                sc_ref[pair, BLOCK:] = s[BLOCK:]
                dp_ref[pair] = _dot(values(pair), dos, NT)
            for h in range(N_HEADS):
                cols = slice(h % 2 * BLOCK, (h % 2 + 1) * BLOCK)
                lrow = l_ref[h:h + 1, :]
                drow = d_ref[h:h + 1, :]
                p = jnp.exp(sc_ref[h // 2, :, cols] - lrow)
                pr_ref[h // 2, :, cols] = p.astype(BF16)
                dsc_ref[h // 2, :, cols] = (p * (dp_ref[h // 2, :, cols] - drow) * scale).astype(BF16)
                if use_sinks:
                    ds_ref[h:h + 1, :] += -(jnp.exp(s_ref[:, h:h + 1] - lrow) * drow)
            grouped = {}
            for pair in range(PAIRS):
                cols = _pair_cols(pair)
                qs = _split_heads(q_ref[:, cols])
                dos = _split_heads(do_ref[:, cols].astype(BF16))
                ds = dsc_ref[pair]
                both = _dot(keys_t(pair), ds, NN)
                dq_t = jnp.concatenate([both[:HEAD_DIM, :BLOCK], both[HEAD_DIM:, BLOCK:]], axis=0)
                dq_ref[:, cols] = dq_t.T.astype(dq_ref.dtype)
                dk = _dot(ds, qs, NN)
                dv = _dot(pr_ref[pair], dos, NN)
                if rep == 1:
                    dk_ref[:, cols] = (ck_ref[:, cols] + dk[:BLOCK]).astype(dk_ref.dtype)
                    dv_ref[:, cols] = (cv_ref[:, cols] + dv[:BLOCK]).astype(dv_ref.dtype)
                    ck_ref[:, cols] = dk[BLOCK:]
                    cv_ref[:, cols] = dv[BLOCK:]
                else:
                    g = 2 * pair // rep
                    grouped[g] = (dk, dv) if g not in grouped else (grouped[g][0] + dk, grouped[g][1] + dv)
            if rep > 1:
                fold = lambda t: t + pltpu.roll(t, HEAD_DIM, 1)
                first_half = _first_head_lanes((2 * BLOCK, LANES))
                dk = jnp.where(first_half, fold(grouped[0][0]), fold(grouped[1][0]))
                dv = jnp.where(first_half, fold(grouped[0][1]), fold(grouped[1][1]))
                dk_ref[...] = (ck_ref[...] + dk[:BLOCK]).astype(dk_ref.dtype)
                dv_ref[...] = (cv_ref[...] + dv[:BLOCK]).astype(dv_ref.dtype)
                ck_ref[...] = dk[BLOCK:]
                cv_ref[...] = dv[BLOCK:]

        @pl.when(i == nb)
        def _():
            dk_ref[...] = ck_ref[...].astype(dk_ref.dtype)
            dv_ref[...] = cv_ref[...].astype(dv_ref.dtype)
            if use_sinks:
                @pl.when(r == lay.dil - 1)
                def _():
                    ds_ref[...] = jnp.broadcast_to(jnp.sum(ds_ref[...], axis=1, keepdims=True), ds_ref.shape)

    cur = lambda i: jnp.minimum(i, nb - 1)
    prev = lambda i: jnp.maximum(jnp.minimum(i, nb - 1) - 1, 0)
    done = lambda i: jnp.maximum(i - 1, 0)
    qspec = lambda col: pl.BlockSpec((BLOCK, QW), lambda r, i: (cur(i), col(r)))
    per_head = pl.BlockSpec((None, N_HEADS, BLOCK), lambda r, i: (r, 0, cur(i)))
    in_specs = [
        qspec(lay.q_col),
        pl.BlockSpec((BLOCK, kw), lambda r, i: (cur(i), lay.k_col(r))),
        pl.BlockSpec((BLOCK, kw), lambda r, i: (prev(i), lay.k_col(r))),
        pl.BlockSpec((BLOCK, kw), lambda r, i: (cur(i), lay.v_col(r))),
        pl.BlockSpec((BLOCK, kw), lambda r, i: (prev(i), lay.v_col(r))),
        qspec(lambda r: r), per_head, per_head,
        pl.BlockSpec((PAIRS, 2 * BLOCK, 2 * BLOCK), lambda r, i: (0, 0, 0)),
    ]
    args = [proj, proj, proj, proj, proj, do, lse, dd, bias]
    out_specs = [
        qspec(lambda r: r),
        pl.BlockSpec((BLOCK, kw), lambda r, i: (done(i), r)),
        pl.BlockSpec((BLOCK, kw), lambda r, i: (done(i), r)),
    ]
    dkv_shape = jax.ShapeDtypeStruct((L, lay.dil * kw), BF16)
    out_shape = [jax.ShapeDtypeStruct((L, lay.dil * QW), BF16), dkv_shape, dkv_shape]
    if use_sinks:
        in_specs.append(pl.BlockSpec((1, N_HEADS), lambda r, i: (0, 0)))
        args.append(sinks)
        out_specs.append(pl.BlockSpec((N_HEADS, LANES), lambda r, i: (0, 0)))
        out_shape.append(jax.ShapeDtypeStruct((N_HEADS, LANES), F32))
    return pl.pallas_call(
        kern, name=name, grid=(lay.dil, nb + 1),
        in_specs=in_specs + ride.in_specs, out_specs=out_specs + ride.out_specs,
        out_shape=out_shape + ride.out_shapes,
        scratch_shapes=[pltpu.VMEM((BLOCK, kw), F32), pltpu.VMEM((BLOCK, kw), F32)]
        + [pltpu.VMEM((PAIRS, 2 * BLOCK, 2 * BLOCK), dt) for dt in (F32, F32, BF16, BF16)] + ride.scratch,
        input_output_aliases=ride.aliases,
        compiler_params=_params("arbitrary", "arbitrary"),
    )(*args, *ride.args)


def _assemble(groups, name, dils=(1,)):
    T = groups[0][0].shape[0] * dils[0]
    widths = [g[0].shape[1] // dils[0] for g in groups]
    total = sum(widths)
    flat = [a for g in groups for a in g]
    member_dils = [d for g in groups for d in dils[:len(g)]]

    def kern(*refs):
        ins = refs[:len(flat)]
        out_ref, cs_ref = refs[len(flat):len(flat) + 2]
        scr = refs[len(flat) + 2:]

        @pl.when(pl.program_id(0) == 0)
        def _():
            cs_ref[...] = jnp.zeros_like(cs_ref)

        pos = off = 0
        for g, w in zip(groups, widths):
            acc = _to_token_order(ins[pos], None, dils[0])
            for j in range(1, len(g)):
                acc = acc + _to_token_order(ins[pos + j], scr[j - 1], dils[j])
            pos += len(g)
            out_ref[:, off:off + w] = acc.astype(BF16)
            cs_ref[:, off:off + w] += jnp.sum(acc, axis=0, keepdims=True)
            off += w

    return pl.pallas_call(
        kern, name=name, grid=(T // ROWS,),
        in_specs=[_view_spec(ROWS, a.shape[1] // d, d) for a, d in zip(flat, member_dils)],
        out_specs=[pl.BlockSpec((ROWS, total), lambda i: (i, 0)), pl.BlockSpec((1, total), lambda i: (0, 0))],
        out_shape=[jax.ShapeDtypeStruct((T, total), BF16), jax.ShapeDtypeStruct((1, total), F32)],
        scratch_shapes=[_regroup_scratch(ROWS, max(widths))] * (len(dils) - 1),
        compiler_params=_params("arbitrary"),
    )(*flat)


def _adamw(w, g, m, v, name):
    _, R, C = w.shape
    rows = min(R, ROWS)
    assert R % rows == 0

    def kern(w_ref, g_ref, m_ref, v_ref, d_ref, nm_ref, nv_ref):
        gv = g_ref[...]
        mn = ADAM_B1 * m_ref[...] + (1.0 - ADAM_B1) * gv
        vn = ADAM_B2 * v_ref[...] + (1.0 - ADAM_B2) * jnp.square(gv)
        m_hat = mn / (1.0 - ADAM_B1 ** ADAM_STEP)
        v_hat = vn / (1.0 - ADAM_B2 ** ADAM_STEP)
        d_ref[...] = -ADAM_LR * (m_hat / (jnp.sqrt(v_hat) + ADAM_EPS) + ADAM_WD * w_ref[...])
        nm_ref[...] = mn
        nv_ref[...] = vn

    blk = pl.BlockSpec((None, rows, C), lambda i: (0, i, 0))
    shp = jax.ShapeDtypeStruct((1, R, C), F32)
    return pl.pallas_call(
        kern, name=name, grid=(R // rows,),
        in_specs=[blk, pl.BlockSpec((rows, C), lambda i: (i, 0)), blk, blk], out_specs=[blk] * 3, out_shape=[shp] * 3,
        compiler_params=_params("parallel"),
    )(w, g, m, v)


def _sum_slots(slots, name):
    n, R, C = slots.shape
    SUM_ROWS = next(rows for rows in (128, 64, 32, 16) if R % rows == 0)

    def kern(s_ref, o_ref):
        acc = s_ref[0].astype(F32)
        for k in range(1, n):
            acc = acc + s_ref[k].astype(F32)
        o_ref[...] = acc

    return pl.pallas_call(
        kern, name=name, grid=(R // SUM_ROWS,),
        in_specs=[pl.BlockSpec((n, SUM_ROWS, C), lambda i: (0, i, 0))],
        out_specs=pl.BlockSpec((SUM_ROWS, C), lambda i: (i, 0)),
        out_shape=jax.ShapeDtypeStruct((R, C), F32),
        compiler_params=_params("parallel"),
    )(slots)


def _place():
    return lax.axis_index("x"), lax.axis_index("y"), lax.axis_index("c")


def _index(p):
    return 4 * p[0] + 2 * p[1] + p[2]


FLIPS = [(fx, fy, fc) for fx in (0, 1) for fy in (0, 1) for fc in (0, 1)][1:]


def _peer(me, flip):
    return tuple(1 - a if f else a for a, f in zip(me, flip))


def _gather_rows(shards, part=(0, 1), into=None, relay_at=RELAY_AT):
    nw = len(shards)

    def plan(ins, outs, send_sems, recv_sems):
        x, y, c = me = _place()
        sibling = (x, y, 1 - c)
        chips = [(1 - x, y), (x, 1 - y), (1 - x, 1 - y)]

        def span(w):
            cnt = ins[w].shape[0] // part[1]
            return part[0] * cnt, cnt

        def rows(w, p):
            lo, cnt = span(w)
            return outs[w].at[pl.ds(_index(p) * ins[w].shape[0] + lo, cnt), :]

        def own(w):
            lo, cnt = span(w)
            return ins[w].at[pl.ds(lo, cnt), :]

        def copy(w, k, block, to):
            return pltpu.make_async_remote_copy(
                src_ref=own(w) if block is me else rows(w, block), dst_ref=rows(w, block),
                send_sem=send_sems.at[7 * w + k], recv_sem=recv_sems.at[7 * w + k],
                device_id=to, device_id_type=MESH)

        return me, sibling, chips, c, rows, own, copy

    def copies(ins, outs, send_sems, recv_sems, local_sems):
        me, sibling, chips, c, rows, own, copy = plan(ins, outs, send_sems, recv_sems)
        local = [pltpu.make_async_copy(own(w), rows(w, me), local_sems.at[w]) for w in range(nw)]
        sends, recvs = [], []
        for w in range(nw):
            sends.append(copy(w, 0, me, sibling))
            sends += [copy(w, 1 + j, me, (*chip, c)) for j, chip in enumerate(chips)]
            recvs.append(copy(w, 0, sibling, me))
            recvs += [copy(w, 4 + j, (*chip, 1 - c), me) for j, chip in enumerate(chips)]
        return local, sends, recvs

    def relay(ins, outs, send_sems, recv_sems, local_sems):
        me, sibling, chips, c, rows, own, copy = plan(ins, outs, send_sems, recv_sems)
        arrived = [copy(w, 1 + j, (*chip, c), me) for w in range(nw) for j, chip in enumerate(chips)]
        onward = [copy(w, 4 + j, (*chip, c), sibling) for w in range(nw) for j, chip in enumerate(chips)]
        return arrived, onward

    shapes = [jax.ShapeDtypeStruct((N_DEV * s.shape[0], s.shape[1]), s.dtype) for s in shards]
    aliases = {nw + w: w for w in range(nw)} if into else None
    return _Exchange(shards + (into or []), shapes, 7 * nw, nw, copies, aliases=aliases, relay=relay,
                     relay_at=relay_at)


def _scatter_rows(parts, part=(0, 1)):
    nw = len(parts)

    def copies(ins, outs, send_sems, recv_sems, local_sems):
        me = _place()

        def src(w, owner):
            n = ins[w].shape[0] // N_DEV
            cnt = n // part[1]
            return ins[w].at[pl.ds(_index(owner) * n + part[0] * cnt, cnt), :]

        def copy(k, w, owner, sender, to):
            return pltpu.make_async_remote_copy(
                src_ref=src(w, owner), dst_ref=outs[w].at[_index(sender)],
                send_sem=send_sems.at[nw * k + w], recv_sem=recv_sems.at[nw * k + w],
                device_id=to, device_id_type=MESH)

        local = [pltpu.make_async_copy(src(w, me), outs[w].at[_index(me)], local_sems.at[w]) for w in range(nw)]
        peers = [_peer(me, flip) for flip in FLIPS]
        sends = [copy(k, w, peer, me, peer) for k, peer in enumerate(peers) for w in range(nw)]
        recvs = [copy(k, w, me, peer, me) for k, peer in enumerate(peers) for w in range(nw)]
        return local, sends, recvs

    shapes = [jax.ShapeDtypeStruct((N_DEV, p.shape[0] // N_DEV // part[1], p.shape[1]), p.dtype) for p in parts]
    return _Exchange(parts, shapes, 7 * nw, nw, copies)


def _sum_over_devices(v):
    shape = v.shape

    def body(v_ref, sum_ref, all_ref, send_sems, recv_sems):
        me = _place()
        all_ref[_index(me)] = v_ref[...]
        sends = []
        for k, flip in enumerate(FLIPS):
            peer = _peer(me, flip)
            sends.append(pltpu.make_async_remote_copy(
                src_ref=v_ref, dst_ref=all_ref.at[_index(me)],
                send_sem=send_sems.at[k], recv_sem=recv_sems.at[k], device_id=peer, device_id_type=MESH))
            sends[-1].start()
        for k, flip in enumerate(FLIPS):
            peer = _peer(me, flip)
            pltpu.make_async_remote_copy(
                src_ref=v_ref, dst_ref=all_ref.at[_index(peer)],
                send_sem=send_sems.at[k], recv_sem=recv_sems.at[k], device_id=peer, device_id_type=MESH).wait_recv()
        for cp in sends:
            cp.wait_send()
        acc = all_ref[0]
        for s in range(1, N_DEV):
            acc = acc + all_ref[s]
        sum_ref[...] = acc

    vmem = pl.BlockSpec(memory_space=pltpu.VMEM)
    return pl.pallas_call(
        body, name="sum_small_grads",
        in_specs=[vmem], out_specs=[vmem, vmem],
        out_shape=[jax.ShapeDtypeStruct(shape, F32), jax.ShapeDtypeStruct((N_DEV,) + shape, F32)],
        scratch_shapes=[pltpu.SemaphoreType.DMA((7,)), pltpu.SemaphoreType.DMA((7,))],
    )(v)[0]


SMALL_ROWS = 8


def _pack_small(vectors):
    padded = []
    for vec in vectors:
        vec = vec.reshape(-1)
        padded.append(jnp.pad(vec, (0, -vec.shape[0] % 128)))
    flat = jnp.concatenate(padded)
    flat = jnp.pad(flat, (0, -flat.shape[0] % (SMALL_ROWS * 128)))
    return flat.reshape(SMALL_ROWS, -1)


def _unpack_small(packed, shapes):
    flat = packed.reshape(-1)
    out, off = [], 0
    for shp in shapes:
        n = int(np.prod(shp))
        out.append(flat[off:off + n].reshape(shp))
        off += n + (-n % 128)
    return out


def kernel(x, g_attn, w_in, b_in, sinks_a, g_out_a, g_out_b, w_out, g_mlp, w_1, w_2, g_final, loss_target, m_g_attn, m_w_in, m_b_in, m_sinks_a, m_g_out_a, m_g_out_b, m_w_out, m_g_mlp, m_w_1, m_w_2, m_g_final, v_g_attn, v_w_in, v_b_in, v_sinks_a, v_g_out_a, v_g_out_b, v_w_out, v_g_mlp, v_w_1, v_w_2, v_g_final):
    xs, tgt = x[0], loss_target[0]
    T, D = xs.shape
    n_a = QW + 2 * KV_HEADS_A * HEAD_DIM
    g_fin = g_final.reshape(1, D)

    shards = [w_in[0].T.astype(BF16), w_out[0].astype(BF16), w_1[0].T.astype(BF16), w_2[0].astype(BF16)]
    ident = lambda acc: (acc,)
    add = lambda acc, other: (acc + other,)
    tiles = dict(tm=512, tn=1024)

    h1, w_in_t = _norm_fwd(xs, g_attn, "norm_attn", exchange=_gather_rows(shards[:1], relay_at=1.0))
    n_in = w_in_t.shape[0]
    proj_a, = _proj_views(h1, w_in_t, b_in, (0, n_a), [1], "proj_a")
    dils = [dil for _, dil in DILATED_BRANCHES]
    *proj_b, w_o = _proj_views(h1, w_in_t, b_in, (n_a, n_in - n_a), dils, "proj_b", exchange=_gather_rows(shards[1:2]))

    lay_a = _AttnLayout(1, KV_HEADS_A, 0, 0, 0, QW // (KV_HEADS_A * HEAD_DIM), QW // (KV_HEADS_A * HEAD_DIM) + 1)
    bias_a = _band_bias(WINDOW_A - 1, 1)
    o_a, l_a, w_1_t = _attn_fwd(proj_a, bias_a, sinks_a, lay_a, "attn_a_fwd",
                                exchange=_gather_rows(shards[2:3], part=(0, 4)))
    branches = []
    for n, (window, dil) in enumerate(DILATED_BRANCHES):
        lay = _AttnLayout(dil, N_HEADS, 3, 0, 3, 1, 2)
        bias = _band_bias(window // dil, dil)
        ride = _gather_rows(shards[2:3], part=(n + 1, 4), into=[w_1_t])
        o, lse, w_1_t = _attn_fwd(proj_b[n], bias, None, lay, f"attn_b{dil}_fwd", exchange=ride)
        branches.append((lay, bias, proj_b[n], o, lse))
    o_b = [br[3] for br in branches]
    l_b = [br[4].transpose(2, 0, 1).reshape(T, N_HEADS) for br in branches]

    mix = _mix_fwd(o_a, o_b, l_b, g_out_a, g_out_b, dils)
    wide = dict(tm=512, tn=2048)

    def residual_and_norm(acc, res, g):
        x_new = acc + res
        return x_new, (x_new * _rstd(x_new)) * g

    assert wide["tn"] == D
    x2, h2 = _matmul(mix, w_o, "nn", [F32, BF16], residual_and_norm, tk=D, tile_ins=[xs], row_ins=[g_mlp],
                     name="out_proj", **wide)

    def relu_sq(acc):
        u = jnp.maximum(acc, 0.0)
        return u, u * u

    u, u_sq, w_2_f = _matmul(h2, w_1_t, "nt", [BF16, BF16], relu_sq, tk=D, name="mlp_up",
                             exchange=_gather_rows(shards[3:]), **wide)
    x3, = _matmul(u_sq, w_2_f, "nn", [F32], add, tk=4096, tile_ins=[x2], name="mlp_down", **tiles)

    dx3, dx3_b, dg_final, loss_dev = _loss_head(x3, tgt, g_fin)

    d_pre, = _matmul(dx3_b, w_2_f, "nt", [BF16], lambda acc, uu: (acc * (2.0 * uu.astype(F32)),),
                     tk=D, tile_ins=[u], name="mlp_down_bwd", **wide)
    wtiles = dict(tm=1024, tn=1024, tk=4096)
    dw_2, = _matmul(u_sq, dx3_b, "tn", [BF16], ident, name="mlp_down_wgrad", **wtiles)
    dh2, slots_2a = _matmul(d_pre, w_1_t, "nn", [BF16], ident, tk=4096, name="mlp_up_bwd",
                            exchange=_scatter_rows([dw_2], part=(0, 2)), **tiles)
    dw_1_t, slots_2b = _matmul(d_pre, h2, "tn", [BF16], ident, name="mlp_up_wgrad",
                               exchange=_scatter_rows([dw_2], part=(1, 2)), **wtiles)
    dx2, dg_mlp, dx2_b, dmix = _norm_bwd(dh2, x2, g_mlp, dx3, "norm_mlp_bwd", then_w_t=w_o)
    dw_o, = _matmul(mix, dx2_b, "tn", [BF16], ident, name="out_proj_wgrad", **wtiles)
    do_a, dd_a, do1, do2, do3, dd1, dd2, dd3, dg_out_a, dg_out_b, slots_o = _mix_bwd(
        dmix, o_a, o_b, l_b, g_out_a, g_out_b, dils, exchange=_scatter_rows([dw_o]))

    by_class = lambda d, dil: d.reshape(T // dil, dil, N_HEADS).transpose(1, 2, 0)
    slots_1 = [None] * 4
    dq_a, dk_a, dv_a, dsinks, slots_1[0] = _attn_bwd(proj_a, do_a, l_a, by_class(dd_a, 1), bias_a, sinks_a, lay_a,
                                                     "attn_a_bwd", exchange=_scatter_rows([dw_1_t], part=(0, 4)))
    dsinks = dsinks[:, 0].reshape(1, N_HEADS)
    dqs, dks, dvs = [], [], []
    for n, ((lay, bias, view, _, lse), do_n, dd_n) in enumerate(zip(branches, (do1, do2, do3), (dd1, dd2, dd3))):
        dq, dk, dv, slots_1[n + 1] = _attn_bwd(view, do_n, lse, by_class(dd_n, lay.dil), bias, None, lay,
                                               f"attn_b{lay.dil}_bwd",
                                               exchange=_scatter_rows([dw_1_t], part=(n + 1, 4)))
        dqs.append(dq)
        dks.append(dk)
        dvs.append(dv)
    dproj, db_in = _assemble([[dq_a], [dk_a], [dv_a], dqs, dks, dvs], "dproj", dils)

    dw_in_t, = _matmul(dproj, h1, "tn", [BF16], ident, tm=n_in // 2, tn=1024, tk=1024, name="in_proj_wgrad")
    dh1, slots_in = _matmul(dproj, w_in_t, "nn", [BF16], ident, tk=n_in, name="in_proj_bwd",
                            exchange=_scatter_rows([dw_in_t]), **tiles)
    dx, dg_attn = _norm_bwd(dh1, xs, g_attn, dx2, "norm_attn_bwd")

    g_w_in = _sum_slots(slots_in, "sum_w_in_grads").T
    g_w_out = _sum_slots(slots_o, "sum_w_out_grads")
    g_w_1 = jnp.concatenate([_sum_slots(s, f"sum_w_1_grads_{n}") for n, s in enumerate(slots_1)]).T
    g_w_2 = jnp.concatenate([_sum_slots(slots_2a, "sum_w_2_grads_0"), _sum_slots(slots_2b, "sum_w_2_grads_1")])

    small_w = [g_attn, b_in, sinks_a, g_out_a, g_out_b, g_mlp, g_final]
    small_m = [m_g_attn, m_b_in, m_sinks_a, m_g_out_a, m_g_out_b, m_g_mlp, m_g_final]
    small_v = [v_g_attn, v_b_in, v_sinks_a, v_g_out_a, v_g_out_b, v_g_mlp, v_g_final]
    small_g = [dg_attn, db_in, dsinks, dg_out_a, dg_out_b, dg_mlp, dg_final]
    summed = _sum_over_devices(_pack_small(small_g + [loss_dev[:, :1]]))
    shapes = [w.shape for w in small_w]
    *g_small, loss = _unpack_small(summed, shapes + [()])

    big = [
        _adamw(w_in, g_w_in, m_w_in, v_w_in, "adamw_w_in"),
        _adamw(w_out, g_w_out, m_w_out, v_w_out, "adamw_w_out"),
        _adamw(w_1, g_w_1, m_w_1, v_w_1, "adamw_w_1"),
        _adamw(w_2, g_w_2, m_w_2, v_w_2, "adamw_w_2"),
    ]
    g_packed = _pack_small(g_small)
    small = _adamw(_pack_small(small_w)[None], g_packed, _pack_small(small_m)[None], _pack_small(small_v)[None],
                   "adamw_small")
    small = [_unpack_small(s, shapes) for s in small]

    def ordered(small_list, big_list):
        s = list(small_list)
        return [s[0], big_list[0], s[1], s[2], s[3], s[4], big_list[1], s[5], big_list[2], big_list[3], s[6]]

    grads = ordered(g_small, [g[None] for g in (g_w_in, g_w_out, g_w_1, g_w_2)])
    deltas = ordered(small[0], [b[0] for b in big])
    new_m = ordered(small[1], [b[1] for b in big])
    new_v = ordered(small[2], [b[2] for b in big])
    return (loss, dx[None], *grads, *deltas, *new_m, *new_v)
```

```python
import numpy as np
import jax
import jax.numpy as jnp
from jax import lax
from jax.experimental import pallas as pl
from jax.experimental.pallas import tpu as pltpu

F32 = jnp.float32
BF16 = jnp.bfloat16

HEAD_DIM = 64
N_HEADS = 16
KV_HEADS_A = 2
BLOCK = 128
WINDOW_A = 128
DILATED_BRANCHES = ((128, 1), (512, 4), (2048, 16))
EPS = 1e-5
NEG_INF = -1e30
N_DEV = 8

ADAM_LR = 0.001
ADAM_B1 = 0.9
ADAM_B2 = 0.999
ADAM_EPS = 1e-08
ADAM_WD = 0.01
ADAM_STEP = 10

VMEM_LIMIT_BYTES = 56 * 1024 * 1024
MESH = pl.DeviceIdType.MESH
ANY = pl.BlockSpec(memory_space=pl.ANY)

NN = (((1,), (0,)), ((), ()))
NT = (((1,), (1,)), ((), ()))
TN = (((0,), (0,)), ((), ()))


def _dot(a, b, dims):
    return lax.dot_general(a, b, dims, preferred_element_type=F32)


def _params(*sem):
    return pltpu.CompilerParams(dimension_semantics=sem, vmem_limit_bytes=VMEM_LIMIT_BYTES)


RELAY_AT = 0.6


class _Exchange:
    def __init__(self, ins, out_shapes, n_remote, n_local, copies, aliases=None, relay=None, relay_at=RELAY_AT):
        self.ins, self.out_shapes = list(ins), list(out_shapes)
        self.n_remote, self.n_local = n_remote, n_local
        self.copies = copies
        self.relay = relay
        self.relay_at = relay_at
        self.aliases = aliases or {}

    def start(self, refs):
        local, sends, _ = self.copies(*refs)
        for cp in local + sends:
            cp.start()

    def middle(self, refs):
        arrived, onward = self.relay(*refs)
        for got, cp in zip(arrived, onward):
            got.wait_recv()
            cp.start()

    def finish(self, refs):
        local, sends, recvs = self.copies(*refs)
        for cp in recvs:
            cp.wait_recv()
        for cp in sends:
            cp.wait_send()
        for cp in local:
            cp.wait()
        if self.relay:
            for cp in self.relay(*refs)[1]:
                cp.wait_send()


class _Ride:
    def __init__(self, ex, n_in, n_out, n_scratch):
        self.ex = ex
        self.n = (n_in, n_out, n_scratch)
        self.args = ex.ins if ex else []
        self.in_specs = [ANY] * len(self.args)
        self.out_shapes = ex.out_shapes if ex else []
        self.out_specs = [ANY] * len(self.out_shapes)
        self.scratch = [pltpu.SemaphoreType.DMA((ex.n_remote,)), pltpu.SemaphoreType.DMA((ex.n_remote,)),
                        pltpu.SemaphoreType.DMA((max(ex.n_local, 1),))] if ex else []
        self.aliases = {n_in + i: n_out + o for i, o in ex.aliases.items()} if ex else {}

    def split(self, refs):
        n_in, n_out, n_scratch = self.n
        a = n_in
        b = a + len(self.args)
        c = b + n_out
        d = c + len(self.out_shapes)
        e = d + n_scratch
        return refs[:a], refs[b:c], refs[d:e], (refs[a:b], refs[c:d], *refs[e:])

    def around(self, step, n_steps, exrefs, compute):
        if self.ex is None:
            compute()
            return

        @pl.when(step == 0)
        def _():
            self.ex.start(exrefs)

        compute()

        if self.ex.relay:
            @pl.when(step == int(self.ex.relay_at * (n_steps - 1)))
            def _():
                self.ex.middle(exrefs)

        @pl.when(step == n_steps - 1)
        def _():
            self.ex.finish(exrefs)


def _matmul(a, b, dims, out_dtypes, epilogue, *, tm, tn, tk, name, tile_ins=(), row_ins=(), exchange=None):
    if dims == "tn":
        K, M = a.shape
    else:
        M, K = a.shape
    N = b.shape[0] if dims == "nt" else b.shape[1]
    tm, tn, tk = min(tm, M), min(tn, N), min(tk, K)
    assert M % tm == 0 and N % tn == 0 and K % tk == 0, (name, M, N, K, tm, tn, tk)
    grid = (M // tm, N // tn, K // tk)
    nk = grid[2]
    n_tile, n_row, n_out = len(tile_ins), len(row_ins), len(out_dtypes)
    dn = {"nn": NN, "nt": NT, "tn": TN}[dims]
    ride = _Ride(exchange, 2 + n_tile + n_row, n_out, 1 if nk > 1 else 0)

    def kern(*refs):
        ins, out_refs, scratch, exrefs = ride.split(refs)
        a_ref, b_ref = ins[:2]
        tile_refs = ins[2:2 + n_tile]
        row_refs = ins[2 + n_tile:]
        ids = [pl.program_id(d) for d in range(3)]

        def finish(acc):
            outs = epilogue(acc, *[r[...] for r in tile_refs], *[r[...] for r in row_refs])
            for o_ref, o in zip(out_refs, outs):
                o_ref[...] = o.astype(o_ref.dtype)

        def compute():
            if nk == 1:
                finish(_dot(a_ref[...], b_ref[...], dn))
                return
            acc_ref = scratch[0]

            @pl.when(ids[2] == 0)
            def _():
                acc_ref[...] = jnp.zeros_like(acc_ref)

            acc_ref[...] += _dot(a_ref[...], b_ref[...], dn)

            @pl.when(ids[2] == nk - 1)
            def _():
                finish(acc_ref[...])

        ride.around((ids[0] * grid[1] + ids[1]) * grid[2] + ids[2], grid[0] * grid[1] * grid[2], exrefs, compute)

    if dims == "tn":
        a_spec = pl.BlockSpec((tk, tm), lambda i, j, k: (k, i))
    else:
        a_spec = pl.BlockSpec((tm, tk), lambda i, j, k: (i, k))
    if dims == "nt":
        b_spec = pl.BlockSpec((tn, tk), lambda i, j, k: (j, k))
    else:
        b_spec = pl.BlockSpec((tk, tn), lambda i, j, k: (k, j))
    tile_spec = pl.BlockSpec((tm, tn), lambda i, j, k: (i, j))
    row_spec = pl.BlockSpec((1, tn), lambda i, j, k: (0, j))
    sem = ("arbitrary",) * 3 if exchange else ("parallel", "parallel", "arbitrary")
    return pl.pallas_call(
        kern,
        name=name,
        grid=grid,
        in_specs=[a_spec, b_spec] + [tile_spec] * n_tile + [row_spec] * n_row + ride.in_specs,
        out_specs=[tile_spec] * n_out + ride.out_specs,
        out_shape=[jax.ShapeDtypeStruct((M, N), dt) for dt in out_dtypes] + ride.out_shapes,
        scratch_shapes=([pltpu.VMEM((tm, tn), F32)] if nk > 1 else []) + ride.scratch,
        input_output_aliases=ride.aliases,
        compiler_params=_params(*sem),
    )(a, b, *tile_ins, *row_ins, *ride.args)


PROJ_ROWS = 256


def _proj_views(a, w_t, bias, cols, dils, name, exchange=None):
    T, K = a.shape
    first, N = cols
    ride = _Ride(exchange, 3, len(dils), 0)

    def kern(*refs):
        (a_ref, w_ref, b_ref), outs, _, exrefs = ride.split(refs)

        def compute():
            acc = _dot(a_ref[...], w_ref[...], NT) + b_ref[...]
            for out_ref, dil in zip(outs, dils):
                _to_class_order(acc, out_ref, dil)

        ride.around(pl.program_id(0), T // PROJ_ROWS, exrefs, compute)

    return pl.pallas_call(
        kern, name=name, grid=(T // PROJ_ROWS,),
        in_specs=[pl.BlockSpec((PROJ_ROWS, K), lambda i: (i, 0)),
                  pl.BlockSpec((pl.Element(N), pl.Element(K)), lambda i: (first, 0)),
                  pl.BlockSpec((pl.Element(1), pl.Element(N)), lambda i: (0, first))] + ride.in_specs,
        out_specs=[_view_spec(PROJ_ROWS, N, d) for d in dils] + ride.out_specs,
        out_shape=[jax.ShapeDtypeStruct((T // d, d * N), BF16) for d in dils] + ride.out_shapes,
        scratch_shapes=ride.scratch,
        input_output_aliases=ride.aliases,
        compiler_params=_params("arbitrary"),
    )(a, w_t, bias, *ride.args)


ROWS = 256
MIX_ROWS = 128


def _rstd(xv):
    return lax.rsqrt(jnp.mean(xv * xv, axis=-1, keepdims=True) + EPS)


def _norm_fwd(x, g, name, exchange=None):
    T, D = x.shape
    ride = _Ride(exchange, 2, 1, 0)

    def kern(*refs):
        (x_ref, g_ref), (h_ref,), _, exrefs = ride.split(refs)

        def compute():
            xv = x_ref[...]
            h_ref[...] = ((xv * _rstd(xv)) * g_ref[...]).astype(h_ref.dtype)

        ride.around(pl.program_id(0), T // ROWS, exrefs, compute)

    row = pl.BlockSpec((ROWS, D), lambda i: (i, 0))
    return pl.pallas_call(
        kern, name=name, grid=(T // ROWS,),
        in_specs=[row, pl.BlockSpec((1, D), lambda i: (0, 0))] + ride.in_specs,
        out_specs=[row] + ride.out_specs,
        out_shape=[jax.ShapeDtypeStruct((T, D), BF16)] + ride.out_shapes,
        scratch_shapes=ride.scratch, input_output_aliases=ride.aliases,
        compiler_params=_params("arbitrary"),
    )(x, g, *ride.args)


def _norm_bwd(dh, x, g, res, name, then_w_t=None):
    T, D = x.shape
    rows = PROJ_ROWS if then_w_t is not None else ROWS

    def kern(dh_ref, x_ref, g_ref, res_ref, *rest):
        if then_w_t is None:
            dx_ref, dg_ref = rest
        else:
            w_ref, dx_ref, dg_ref, dxb_ref, y_ref = rest

        @pl.when(pl.program_id(0) == 0)
        def _():
            dg_ref[...] = jnp.zeros_like(dg_ref)

        xv = x_ref[...]
        r = _rstd(xv)
        xn = xv * r
        dhv = dh_ref[...].astype(F32)
        dg_ref[...] += jnp.sum(dhv * xn, axis=0, keepdims=True)
        t = dhv * g_ref[...]
        dx = res_ref[...] + r * (t - xn * jnp.mean(t * xn, axis=-1, keepdims=True))
        dx_ref[...] = dx
        if then_w_t is not None:
            dxb = dx.astype(BF16)
            dxb_ref[...] = dxb
            y_ref[...] = _dot(dxb, w_ref[...], NT)

    row = pl.BlockSpec((rows, D), lambda i: (i, 0))
    vec = pl.BlockSpec((1, D), lambda i: (0, 0))
    in_specs, args = [row, row, vec, row], [dh, x, g, res]
    out_specs = [row, vec]
    out_shape = [jax.ShapeDtypeStruct((T, D), F32), jax.ShapeDtypeStruct((1, D), F32)]
    if then_w_t is not None:
        N = then_w_t.shape[0]
        in_specs.append(pl.BlockSpec((N, D), lambda i: (0, 0)))
        args.append(then_w_t)
        out_specs += [row, pl.BlockSpec((rows, N), lambda i: (i, 0))]
        out_shape += [jax.ShapeDtypeStruct((T, D), BF16), jax.ShapeDtypeStruct((T, N), F32)]
    return pl.pallas_call(
        kern, name=name, grid=(T // rows,), in_specs=in_specs, out_specs=out_specs, out_shape=out_shape,
        compiler_params=_params("arbitrary"),
    )(*args)


def _loss_head(x3, tgt, g):
    T, D = x3.shape

    def kern(x_ref, t_ref, g_ref, dx_ref, dxb_ref, dg_ref, loss_ref):
        @pl.when(pl.program_id(0) == 0)
        def _():
            dg_ref[...] = jnp.zeros_like(dg_ref)
            loss_ref[...] = jnp.zeros_like(loss_ref)

        xv = x_ref[...]
        gv = g_ref[...]
        r = _rstd(xv)
        xn = xv * r
        err = xn * gv - t_ref[...]
        per_tok = jnp.mean(err * err, axis=-1, keepdims=True)
        loss_ref[...] += 0.5 * jnp.sum(per_tok, axis=0, keepdims=True)
        dy = err * (1.0 / D)
        dg_ref[...] += jnp.sum(dy * xn, axis=0, keepdims=True)
        t = dy * gv
        dx = r * (t - xn * jnp.mean(t * xn, axis=-1, keepdims=True))
        dx_ref[...] = dx
        dxb_ref[...] = dx.astype(BF16)

    row = pl.BlockSpec((ROWS, D), lambda i: (i, 0))
    vec = pl.BlockSpec((1, D), lambda i: (0, 0))
    return pl.pallas_call(
        kern, name="loss_head", grid=(T // ROWS,),
        in_specs=[row, row, vec],
        out_specs=[row, row, vec, pl.BlockSpec((1, 128), lambda i: (0, 0))],
        out_shape=[jax.ShapeDtypeStruct((T, D), F32), jax.ShapeDtypeStruct((T, D), BF16),
                   jax.ShapeDtypeStruct((1, D), F32), jax.ShapeDtypeStruct((1, 128), F32)],
        compiler_params=_params("arbitrary"),
    )(x3, tgt, g)


def _spread_matrix():
    head_of_lane = np.arange(N_HEADS * HEAD_DIM) // HEAD_DIM
    return jnp.asarray(np.arange(N_HEADS)[:, None] == head_of_lane[None, :], dtype=BF16)


def _pieces(v, n):
    out = []
    for _ in range(n):
        piece = v.astype(BF16)
        out.append(piece)
        v = v - piece.astype(F32)
    return out


def _spread(v, spread):
    return sum(_dot(p, spread, NN) for p in _pieces(v, 2))


def _spread_weights(w1, w2, spread):
    s1, s2 = _spread(w1, spread), _spread(w2, spread)
    return s1, s2, 1.0 - s1 - s2


def _head_sums(v, spread):
    return sum(_dot(p, spread, NT) for p in _pieces(v, 2))


def _branch_weights(l1, l2, l3):
    lm = jnp.maximum(jnp.maximum(l1, l2), l3)
    e1, e2, e3 = jnp.exp(l1 - lm), jnp.exp(l2 - lm), jnp.exp(l3 - lm)
    inv = 1.0 / (e1 + e2 + e3)
    return e1 * inv, e2 * inv, e3 * inv


def _to_token_order(view_ref, dil):
    if dil == 1:
        return view_ref[...].astype(F32)
    n_l, w = view_ref.shape[0], view_ref.shape[1] // dil
    cols = []
    for cb in range(w // LANES):
        by_class = jnp.stack([view_ref[:, r * w + cb * LANES:r * w + (cb + 1) * LANES].astype(F32) for r in range(dil)])
        cols.append(jnp.swapaxes(by_class, 0, 1).reshape(n_l * dil, LANES))
    return jnp.concatenate(cols, axis=1)


def _to_class_order(val, view_ref, dil):
    if dil == 1:
        view_ref[...] = val.astype(view_ref.dtype)
        return
    n, w = val.shape
    for cb in range(w // LANES):
        by_class = jnp.swapaxes(val[:, cb * LANES:(cb + 1) * LANES].reshape(n // dil, dil, LANES), 0, 1)
        for r in range(dil):
            view_ref[:, r * w + cb * LANES:r * w + (cb + 1) * LANES] = by_class[r].astype(view_ref.dtype)


def _view_spec(rows, width, dil):
    return pl.BlockSpec((rows // dil, dil * width), lambda i: (i, 0))


def _mix_fwd(oa, obs, lbs, ga, gb, dils):
    T, W = oa.shape

    def kern(oa_ref, o1, o2, o3, l1, l2, l3, ga_ref, gb_ref, sp_ref, mix_ref):
        sp = sp_ref[...]
        w1, w2, w3 = _branch_weights(l1[...], l2[...], l3[...])
        on = [_to_token_order(o, d) for o, d in zip((o1, o2, o3), dils)]
        s1, s2, s3 = _spread_weights(w1, w2, sp)
        ob = s1 * on[0] + s2 * on[1] + s3 * on[2]
        oav = oa_ref[...]
        mix_ref[:, :W] = ((oav * _rstd(oav)) * ga_ref[...]).astype(BF16)
        mix_ref[:, W:] = ((ob * _rstd(ob)) * gb_ref[...]).astype(BF16)

    row = pl.BlockSpec((MIX_ROWS, W), lambda i: (i, 0))
    per_head = pl.BlockSpec((MIX_ROWS, N_HEADS), lambda i: (i, 0))
    vec = pl.BlockSpec((1, W), lambda i: (0, 0))
    return pl.pallas_call(
        kern, name="mix_fwd", grid=(T // MIX_ROWS,),
        in_specs=[row] + [_view_spec(MIX_ROWS, W, d) for d in dils] + [per_head] * 3
        + [vec, vec, pl.BlockSpec((N_HEADS, W), lambda i: (0, 0))],
        out_specs=pl.BlockSpec((MIX_ROWS, 2 * W), lambda i: (i, 0)),
        out_shape=jax.ShapeDtypeStruct((T, 2 * W), BF16),
        compiler_params=_params("parallel"),
    )(oa, *obs, *lbs, ga, gb, _spread_matrix())


def _mix_bwd(dmix, oa, obs, lbs, ga, gb, dils, exchange=None):
    T, W = oa.shape
    ride = _Ride(exchange, 11, 10, 0)

    def kern(*refs):
        ins, outs, _, exrefs = ride.split(refs)
        ride.around(pl.program_id(0), T // MIX_ROWS, exrefs, lambda: compute(*ins, *outs))

    def compute(dm_ref, oa_ref, o1, o2, o3, l1, l2, l3, ga_ref, gb_ref, sp_ref,
                doa_ref, da_ref, do1, do2, do3, d1, d2, d3, dga_ref, dgb_ref):
        @pl.when(pl.program_id(0) == 0)
        def _():
            dga_ref[...] = jnp.zeros_like(dga_ref)
            dgb_ref[...] = jnp.zeros_like(dgb_ref)

        sp = sp_ref[...]
        oav = oa_ref[...]
        r = _rstd(oav)
        on = oav * r
        dy = dm_ref[:, :W]
        dga_ref[...] += jnp.sum(dy * on, axis=0, keepdims=True)
        t = dy * ga_ref[...]
        doa = r * (t - on * jnp.mean(t * on, axis=-1, keepdims=True))
        doa_ref[...] = doa.astype(BF16)
        da_ref[...] = _head_sums(doa * oav, sp)
        w1, w2, w3 = _branch_weights(l1[...], l2[...], l3[...])
        s1, s2, s3 = _spread_weights(w1, w2, sp)
        on = [_to_token_order(o, d) for o, d in zip((o1, o2, o3), dils)]
        ob = s1 * on[0] + s2 * on[1] + s3 * on[2]
        r = _rstd(ob)
        on = ob * r
        dy = dm_ref[:, W:]
        dgb_ref[...] += jnp.sum(dy * on, axis=0, keepdims=True)
        t = dy * gb_ref[...]
        dob = r * (t - on * jnp.mean(t * on, axis=-1, keepdims=True))
        c = _head_sums(dob * ob, sp)
        for do_ref, sn, d in zip((do1, do2, do3), (s1, s2, s3), dils):
            _to_class_order(sn * dob, do_ref, d)
        d1[...] = w1 * c
        d2[...] = w2 * c
        d3[...] = w3 * c

    row = pl.BlockSpec((MIX_ROWS, W), lambda i: (i, 0))
    per_head = pl.BlockSpec((MIX_ROWS, N_HEADS), lambda i: (i, 0))
    vec = pl.BlockSpec((1, W), lambda i: (0, 0))
    bf = jax.ShapeDtypeStruct((T, W), BF16)
    ph = jax.ShapeDtypeStruct((T, N_HEADS), F32)
    vv = jax.ShapeDtypeStruct((1, W), F32)
    views = [_view_spec(MIX_ROWS, W, d) for d in dils]
    return pl.pallas_call(
        kern, name="mix_bwd", grid=(T // MIX_ROWS,),
        in_specs=[pl.BlockSpec((MIX_ROWS, 2 * W), lambda i: (i, 0)), row] + views + [per_head] * 3 + [vec, vec,
                  pl.BlockSpec((N_HEADS, W), lambda i: (0, 0))] + ride.in_specs,
        out_specs=[row, per_head] + views + [per_head, per_head, per_head, vec, vec] + ride.out_specs,
        out_shape=[bf, ph] + [jax.ShapeDtypeStruct(o.shape, F32) for o in obs] + [ph, ph, ph, vv, vv]
        + ride.out_shapes,
        scratch_shapes=ride.scratch,
        input_output_aliases=ride.aliases,
        compiler_params=_params("arbitrary"),
    )(dmix, oa, *obs, *lbs, ga, gb, _spread_matrix(), *ride.args)


def _alibi_slopes(n):
    return np.asarray(2.0 ** (-8.0 * (np.arange(n) + 1) / n)).astype(np.float32)


def _band_bias(max_steps, step_dist):
    qi = np.arange(BLOCK)[None, :]
    kj = np.arange(BLOCK)[:, None]
    slopes = _alibi_slopes(N_HEADS)
    halves = []
    for steps in (qi + BLOCK - kj, qi - kj):
        valid = (steps >= 0) & (steps <= max_steps)
        alibi = slopes[:, None, None] * (step_dist * steps).astype(np.float32)[None]
        halves.append(np.where(valid[None], -alibi, np.float32(NEG_INF)).astype(np.float32))
    per_head = np.concatenate(halves, axis=1)
    return jnp.asarray(np.concatenate([per_head[0::2], per_head[1::2]], axis=2))


class _AttnLayout:
    def __init__(self, dil, kv_heads, q_stride, q_off, k_stride, k_off, v_off):
        self.dil = dil
        self.kv_heads = kv_heads
        self.kw = kv_heads * HEAD_DIM
        self.rep = N_HEADS // kv_heads
        self.q_col = lambda r: r * q_stride + q_off
        self.k_col = lambda r: r * k_stride + k_off
        self.v_col = lambda r: r * k_stride + v_off


QW = N_HEADS * HEAD_DIM
LANES = 128


PAIRS = N_HEADS // 2


def _pair_cols(pair):
    return slice(pair * LANES, (pair + 1) * LANES)


def _first_head_lanes(shape):
    return lax.broadcasted_iota(jnp.int32, shape, 1) < HEAD_DIM


def _split_heads(pair):
    first = _first_head_lanes(pair.shape)
    zero = jnp.zeros_like(pair)
    return jnp.concatenate([jnp.where(first, pair, zero), jnp.where(first, zero, pair)], axis=0)


def _kv_pair(ref, pair, rep):
    if rep == 1:
        return ref[:, _pair_cols(pair)]
    blk = ref[...].astype(F32)
    other = pltpu.roll(blk, HEAD_DIM, 1)
    first = _first_head_lanes(blk.shape)
    both = jnp.where(first, blk, other) if (2 * pair // rep) % 2 == 0 else jnp.where(first, other, blk)
    return both.astype(ref.dtype)


def _paired_kv(prev_ref, cur_ref, rep, transposed=False):
    memo = {}

    def get(pair):
        key = pair if rep == 1 else 2 * pair // rep
        if key not in memo:
            blocks = [_kv_pair(ref, pair, rep) for ref in (prev_ref, cur_ref)]
            memo[key] = jnp.concatenate([b.T for b in blocks], axis=1) if transposed else jnp.concatenate(blocks, axis=0)
        return memo[key]

    return get


def _attn_fwd(proj, bias, sinks, lay, name, exchange=None):
    L = proj.shape[0]
    nb = L // BLOCK
    kw, rep = lay.kw, lay.rep
    use_sinks = sinks is not None
    scale = HEAD_DIM ** -0.5
    ride = _Ride(exchange, 7 if use_sinks else 6, 2, 2)

    def kern(*refs):
        ins, (o_ref, l_ref), (sc_ref, pr_ref), exrefs = ride.split(refs)
        q_ref, kc_ref, kp_ref, vc_ref, vp_ref, b_ref = ins[:6]
        s_ref = ins[6] if use_sinks else None
        r, i = pl.program_id(0), pl.program_id(1)
        first = i == 0
        ride.around(r * nb + i, lay.dil * nb, exrefs,
                    lambda: compute(q_ref, kc_ref, kp_ref, vc_ref, vp_ref, b_ref, s_ref, o_ref, l_ref, first,
                                    sc_ref, pr_ref))

    def compute(q_ref, kc_ref, kp_ref, vc_ref, vp_ref, b_ref, s_ref, o_ref, l_ref, first, sc_ref, pr_ref):
        keys, values_t = _paired_kv(kp_ref, kc_ref, rep), _paired_kv(vp_ref, vc_ref, rep, transposed=True)
        for pair in range(PAIRS):
            qs = _split_heads(q_ref[:, _pair_cols(pair)])
            s_prev = _dot(keys(pair)[:BLOCK], qs, NT) * scale + b_ref[pair, :BLOCK]
            sc_ref[pair, :BLOCK] = jnp.where(first, NEG_INF, s_prev)
            sc_ref[pair, BLOCK:] = _dot(keys(pair)[BLOCK:], qs, NT) * scale + b_ref[pair, BLOCK:]
        inv = []
        for h in range(N_HEADS):
            cols = slice(h % 2 * BLOCK, (h % 2 + 1) * BLOCK)
            s = sc_ref[h // 2, :, cols]
            m = jnp.max(s, axis=0, keepdims=True)
            if use_sinks:
                sink = s_ref[:, h:h + 1]
                m = jnp.maximum(m, sink)
            p = jnp.exp(s - m)
            denom = jnp.sum(p, axis=0, keepdims=True)
            if use_sinks:
                denom = denom + jnp.exp(sink - m)
            pr_ref[h // 2, :, cols] = p.astype(BF16)
            l_ref[h:h + 1, :] = m + jnp.log(denom)
            inv.append(1.0 / denom)
        for pair in range(PAIRS):
            both = _dot(values_t(pair), pr_ref[pair], NN)
            o_t = jnp.concatenate([both[:HEAD_DIM, :BLOCK] * inv[2 * pair], both[HEAD_DIM:, BLOCK:] * inv[2 * pair + 1]],
                                  axis=0)
            o_ref[:, _pair_cols(pair)] = o_t.T

    prev = lambda i: jnp.maximum(i - 1, 0)
    in_specs = [
        pl.BlockSpec((BLOCK, QW), lambda r, i: (i, lay.q_col(r))),
        pl.BlockSpec((BLOCK, kw), lambda r, i: (i, lay.k_col(r))),
        pl.BlockSpec((BLOCK, kw), lambda r, i: (prev(i), lay.k_col(r))),
        pl.BlockSpec((BLOCK, kw), lambda r, i: (i, lay.v_col(r))),
        pl.BlockSpec((BLOCK, kw), lambda r, i: (prev(i), lay.v_col(r))),
        pl.BlockSpec((PAIRS, 2 * BLOCK, 2 * BLOCK), lambda r, i: (0, 0, 0)),
    ]
    args = [proj, proj, proj, proj, proj, bias]
    if use_sinks:
        in_specs.append(pl.BlockSpec((1, N_HEADS), lambda r, i: (0, 0)))
        args.append(sinks)
    out_specs = [pl.BlockSpec((BLOCK, QW), lambda r, i: (i, r)),
                 pl.BlockSpec((None, N_HEADS, BLOCK), lambda r, i: (r, 0, i))]
    out_shape = [jax.ShapeDtypeStruct((L, lay.dil * QW), F32), jax.ShapeDtypeStruct((lay.dil, N_HEADS, L), F32)]
    return pl.pallas_call(
        kern, name=name, grid=(lay.dil, nb),
        in_specs=in_specs + ride.in_specs, out_specs=out_specs + ride.out_specs,
        out_shape=out_shape + ride.out_shapes,
        scratch_shapes=[pltpu.VMEM((PAIRS, 2 * BLOCK, 2 * BLOCK), dt) for dt in (F32, BF16)] + ride.scratch,
        input_output_aliases=ride.aliases,
        compiler_params=_params("arbitrary", "arbitrary"),
    )(*args, *ride.args)


def _attn_bwd(proj, do, lse, dd, bias, sinks, lay, name, exchange=None):
    L = proj.shape[0]
    nb = L // BLOCK
    kw, rep = lay.kw, lay.rep
    assert rep == 1 or lay.kv_heads == 2, "grouped queries: the two kv heads fill one 128-lane block"
    use_sinks = sinks is not None
    scale = HEAD_DIM ** -0.5
    ride = _Ride(exchange, 10 if use_sinks else 9, 4 if use_sinks else 3, 6)

    def kern(*refs):
        ins, outs, (ck_ref, cv_ref, *staged), exrefs = ride.split(refs)
        q_ref, kc_ref, kp_ref, vc_ref, vp_ref, do_ref, l_ref, d_ref, b_ref = ins[:9]
        s_ref = ins[9] if use_sinks else None
        dq_ref, dk_ref, dv_ref = outs[:3]
        ds_ref = outs[3] if use_sinks else None
        r = pl.program_id(0)
        i = pl.program_id(1)
        ride.around(r * (nb + 1) + i, lay.dil * (nb + 1), exrefs,
                    lambda: compute(q_ref, kc_ref, kp_ref, vc_ref, vp_ref, do_ref, l_ref, d_ref, b_ref, s_ref,
                                    dq_ref, dk_ref, dv_ref, ds_ref, ck_ref, cv_ref, r, i, *staged))

    def compute(q_ref, kc_ref, kp_ref, vc_ref, vp_ref, do_ref, l_ref, d_ref, b_ref, s_ref,
                dq_ref, dk_ref, dv_ref, ds_ref, ck_ref, cv_ref, r, i, sc_ref, dp_ref, pr_ref, dsc_ref):
        first = i == 0

        @pl.when(first)
        def _():
            ck_ref[...] = jnp.zeros_like(ck_ref)
            cv_ref[...] = jnp.zeros_like(cv_ref)

        if use_sinks:
            @pl.when(first & (r == 0))
            def _():
                ds_ref[...] = jnp.zeros_like(ds_ref)

        @pl.when(i < nb)
        def _():
            keys, values = _paired_kv(kp_ref, kc_ref, rep), _paired_kv(vp_ref, vc_ref, rep)
            keys_t = _paired_kv(kp_ref, kc_ref, rep, transposed=True)
            for pair in range(PAIRS):
                qs = _split_heads(q_ref[:, _pair_cols(pair)])
                dos = _split_heads(do_ref[:, _pair_cols(pair)].astype(BF16))
                s = _dot(keys(pair), qs, NT) * scale + b_ref[pair]
                sc_ref[pair, :BLOCK] = jnp.where(first, NEG_INF, s[:BLOCK])
                sc_ref[pair, BLOCK:] = s[BLOCK:]
                dp_ref[pair] = _dot(values(pair), dos, NT)
            for h in range(N_HEADS):
                cols = slice(h % 2 * BLOCK, (h % 2 + 1) * BLOCK)
                lrow = l_ref[h:h + 1, :]
                drow = d_ref[h:h + 1, :]
                p = jnp.exp(sc_ref[h // 2, :, cols] - lrow)
                pr_ref[h // 2, :, cols] = p.astype(BF16)
                dsc_ref[h // 2, :, cols] = (p * (dp_ref[h // 2, :, cols] - drow) * scale).astype(BF16)
                if use_sinks:
                    ds_ref[h:h + 1, :] += -(jnp.exp(s_ref[:, h:h + 1] - lrow) * drow)
            grouped = {}
            for pair in range(PAIRS):
                cols = _pair_cols(pair)
                qs = _split_heads(q_ref[:, cols])
                dos = _split_heads(do_ref[:, cols].astype(BF16))
                ds = dsc_ref[pair]
                both = _dot(keys_t(pair), ds, NN)
                dq_t = jnp.concatenate([both[:HEAD_DIM, :BLOCK], both[HEAD_DIM:, BLOCK:]], axis=0)
                dq_ref[:, cols] = dq_t.T.astype(dq_ref.dtype)
                dk = _dot(ds, qs, NN)
                dv = _dot(pr_ref[pair], dos, NN)
                if rep == 1:
                    dk_ref[:, cols] = (ck_ref[:, cols] + dk[:BLOCK]).astype(dk_ref.dtype)
                    dv_ref[:, cols] = (cv_ref[:, cols] + dv[:BLOCK]).astype(dv_ref.dtype)
                    ck_ref[:, cols] = dk[BLOCK:]
                    cv_ref[:, cols] = dv[BLOCK:]
                else:
                    g = 2 * pair // rep
                    grouped[g] = (dk, dv) if g not in grouped else (grouped[g][0] + dk, grouped[g][1] + dv)
            if rep > 1:
                fold = lambda t: t + pltpu.roll(t, HEAD_DIM, 1)
                first_half = _first_head_lanes((2 * BLOCK, LANES))
                dk = jnp.where(first_half, fold(grouped[0][0]), fold(grouped[1][0]))
                dv = jnp.where(first_half, fold(grouped[0][1]), fold(grouped[1][1]))
                dk_ref[...] = (ck_ref[...] + dk[:BLOCK]).astype(dk_ref.dtype)
                dv_ref[...] = (cv_ref[...] + dv[:BLOCK]).astype(dv_ref.dtype)
                ck_ref[...] = dk[BLOCK:]
                cv_ref[...] = dv[BLOCK:]

        @pl.when(i == nb)
        def _():
            dk_ref[...] = ck_ref[...].astype(dk_ref.dtype)
            dv_ref[...] = cv_ref[...].astype(dv_ref.dtype)
            if use_sinks:
                @pl.when(r == lay.dil - 1)
                def _():
                    ds_ref[...] = jnp.broadcast_to(jnp.sum(ds_ref[...], axis=1, keepdims=True), ds_ref.shape)

    cur = lambda i: jnp.minimum(i, nb - 1)
    prev = lambda i: jnp.maximum(jnp.minimum(i, nb - 1) - 1, 0)
    done = lambda i: jnp.maximum(i - 1, 0)
    qspec = lambda col: pl.BlockSpec((BLOCK, QW), lambda r, i: (cur(i), col(r)))
    per_head = pl.BlockSpec((None, N_HEADS, BLOCK), lambda r, i: (r, 0, cur(i)))
    in_specs = [
        qspec(lay.q_col),
        pl.BlockSpec((BLOCK, kw), lambda r, i: (cur(i), lay.k_col(r))),
        pl.BlockSpec((BLOCK, kw), lambda r, i: (prev(i), lay.k_col(r))),
        pl.BlockSpec((BLOCK, kw), lambda r, i: (cur(i), lay.v_col(r))),
        pl.BlockSpec((BLOCK, kw), lambda r, i: (prev(i), lay.v_col(r))),
        qspec(lambda r: r), per_head, per_head,
        pl.BlockSpec((PAIRS, 2 * BLOCK, 2 * BLOCK), lambda r, i: (0, 0, 0)),
    ]
    args = [proj, proj, proj, proj, proj, do, lse, dd, bias]
    out_specs = [
        qspec(lambda r: r),
        pl.BlockSpec((BLOCK, kw), lambda r, i: (done(i), r)),
        pl.BlockSpec((BLOCK, kw), lambda r, i: (done(i), r)),
    ]
    dkv_shape = jax.ShapeDtypeStruct((L, lay.dil * kw), BF16)
    out_shape = [jax.ShapeDtypeStruct((L, lay.dil * QW), BF16), dkv_shape, dkv_shape]
    if use_sinks:
        in_specs.append(pl.BlockSpec((1, N_HEADS), lambda r, i: (0, 0)))
        args.append(sinks)
        out_specs.append(pl.BlockSpec((N_HEADS, LANES), lambda r, i: (0, 0)))
        out_shape.append(jax.ShapeDtypeStruct((N_HEADS, LANES), F32))
    return pl.pallas_call(
        kern, name=name, grid=(lay.dil, nb + 1),
        in_specs=in_specs + ride.in_specs, out_specs=out_specs + ride.out_specs,
        out_shape=out_shape + ride.out_shapes,
        scratch_shapes=[pltpu.VMEM((BLOCK, kw), F32), pltpu.VMEM((BLOCK, kw), F32)]
        + [pltpu.VMEM((PAIRS, 2 * BLOCK, 2 * BLOCK), dt) for dt in (F32, F32, BF16, BF16)] + ride.scratch,
        input_output_aliases=ride.aliases,
        compiler_params=_params("arbitrary", "arbitrary"),
    )(*args, *ride.args)


def _assemble(groups, name, dils=(1,)):
    T = groups[0][0].shape[0] * dils[0]
    widths = [g[0].shape[1] // dils[0] for g in groups]
    total = sum(widths)
    flat = [a for g in groups for a in g]
    member_dils = [d for g in groups for d in dils[:len(g)]]

    def kern(*refs):
        ins = refs[:len(flat)]
        out_ref, cs_ref = refs[len(flat):]

        @pl.when(pl.program_id(0) == 0)
        def _():
            cs_ref[...] = jnp.zeros_like(cs_ref)

        pos = off = 0
        for g, w in zip(groups, widths):
            acc = _to_token_order(ins[pos], dils[0])
            for j in range(1, len(g)):
                acc = acc + _to_token_order(ins[pos + j], dils[j])
            pos += len(g)
            out_ref[:, off:off + w] = acc.astype(BF16)
            cs_ref[:, off:off + w] += jnp.sum(acc, axis=0, keepdims=True)
            off += w

    return pl.pallas_call(
        kern, name=name, grid=(T // ROWS,),
        in_specs=[_view_spec(ROWS, a.shape[1] // d, d) for a, d in zip(flat, member_dils)],
        out_specs=[pl.BlockSpec((ROWS, total), lambda i: (i, 0)), pl.BlockSpec((1, total), lambda i: (0, 0))],
        out_shape=[jax.ShapeDtypeStruct((T, total), BF16), jax.ShapeDtypeStruct((1, total), F32)],
        compiler_params=_params("arbitrary"),
    )(*flat)


def _adamw(w, g, m, v, name):
    _, R, C = w.shape
    rows = min(R, ROWS)
    assert R % rows == 0

    def kern(w_ref, g_ref, m_ref, v_ref, d_ref, nm_ref, nv_ref):
        gv = g_ref[...]
        mn = ADAM_B1 * m_ref[...] + (1.0 - ADAM_B1) * gv
        vn = ADAM_B2 * v_ref[...] + (1.0 - ADAM_B2) * jnp.square(gv)
        m_hat = mn / (1.0 - ADAM_B1 ** ADAM_STEP)
        v_hat = vn / (1.0 - ADAM_B2 ** ADAM_STEP)
        d_ref[...] = -ADAM_LR * (m_hat / (jnp.sqrt(v_hat) + ADAM_EPS) + ADAM_WD * w_ref[...])
        nm_ref[...] = mn
        nv_ref[...] = vn

    blk = pl.BlockSpec((None, rows, C), lambda i: (0, i, 0))
    shp = jax.ShapeDtypeStruct((1, R, C), F32)
    return pl.pallas_call(
        kern, name=name, grid=(R // rows,),
        in_specs=[blk, pl.BlockSpec((rows, C), lambda i: (i, 0)), blk, blk], out_specs=[blk] * 3, out_shape=[shp] * 3,
        compiler_params=_params("parallel"),
    )(w, g, m, v)


def _sum_slots(slots, name):
    n, R, C = slots.shape
    SUM_ROWS = next(rows for rows in (128, 64, 32, 16) if R % rows == 0)

    def kern(s_ref, o_ref):
        acc = s_ref[0].astype(F32)
        for k in range(1, n):
            acc = acc + s_ref[k].astype(F32)
        o_ref[...] = acc

    return pl.pallas_call(
        kern, name=name, grid=(R // SUM_ROWS,),
        in_specs=[pl.BlockSpec((n, SUM_ROWS, C), lambda i: (0, i, 0))],
        out_specs=pl.BlockSpec((SUM_ROWS, C), lambda i: (i, 0)),
        out_shape=jax.ShapeDtypeStruct((R, C), F32),
        compiler_params=_params("parallel"),
    )(slots)


def _place():
    return lax.axis_index("x"), lax.axis_index("y"), lax.axis_index("c")


def _index(p):
    return 4 * p[0] + 2 * p[1] + p[2]


FLIPS = [(fx, fy, fc) for fx in (0, 1) for fy in (0, 1) for fc in (0, 1)][1:]


def _peer(me, flip):
    return tuple(1 - a if f else a for a, f in zip(me, flip))


def _gather_rows(shards, part=(0, 1), into=None, relay_at=RELAY_AT):
    nw = len(shards)

    def plan(ins, outs, send_sems, recv_sems):
        x, y, c = me = _place()
        sibling = (x, y, 1 - c)
        chips = [(1 - x, y), (x, 1 - y), (1 - x, 1 - y)]

        def span(w):
            cnt = ins[w].shape[0] // part[1]
            return part[0] * cnt, cnt

        def rows(w, p):
            lo, cnt = span(w)
            return outs[w].at[pl.ds(_index(p) * ins[w].shape[0] + lo, cnt), :]

        def own(w):
            lo, cnt = span(w)
            return ins[w].at[pl.ds(lo, cnt), :]

        def copy(w, k, block, to):
            return pltpu.make_async_remote_copy(
                src_ref=own(w) if block is me else rows(w, block), dst_ref=rows(w, block),
                send_sem=send_sems.at[7 * w + k], recv_sem=recv_sems.at[7 * w + k],
                device_id=to, device_id_type=MESH)

        return me, sibling, chips, c, rows, own, copy

    def copies(ins, outs, send_sems, recv_sems, local_sems):
        me, sibling, chips, c, rows, own, copy = plan(ins, outs, send_sems, recv_sems)
        local = [pltpu.make_async_copy(own(w), rows(w, me), local_sems.at[w]) for w in range(nw)]
        sends, recvs = [], []
        for w in range(nw):
            sends.append(copy(w, 0, me, sibling))
            sends += [copy(w, 1 + j, me, (*chip, c)) for j, chip in enumerate(chips)]
            recvs.append(copy(w, 0, sibling, me))
            recvs += [copy(w, 4 + j, (*chip, 1 - c), me) for j, chip in enumerate(chips)]
        return local, sends, recvs

    def relay(ins, outs, send_sems, recv_sems, local_sems):
        me, sibling, chips, c, rows, own, copy = plan(ins, outs, send_sems, recv_sems)
        arrived = [copy(w, 1 + j, (*chip, c), me) for w in range(nw) for j, chip in enumerate(chips)]
        onward = [copy(w, 4 + j, (*chip, c), sibling) for w in range(nw) for j, chip in enumerate(chips)]
        return arrived, onward

    shapes = [jax.ShapeDtypeStruct((N_DEV * s.shape[0], s.shape[1]), s.dtype) for s in shards]
    aliases = {nw + w: w for w in range(nw)} if into else None
    return _Exchange(shards + (into or []), shapes, 7 * nw, nw, copies, aliases=aliases, relay=relay,
                     relay_at=relay_at)


def _scatter_rows(parts, part=(0, 1)):
    nw = len(parts)

    def copies(ins, outs, send_sems, recv_sems, local_sems):
        me = _place()

        def src(w, owner):
            n = ins[w].shape[0] // N_DEV
            cnt = n // part[1]
            return ins[w].at[pl.ds(_index(owner) * n + part[0] * cnt, cnt), :]

        def copy(k, w, owner, sender, to):
            return pltpu.make_async_remote_copy(
                src_ref=src(w, owner), dst_ref=outs[w].at[_index(sender)],
                send_sem=send_sems.at[nw * k + w], recv_sem=recv_sems.at[nw * k + w],
                device_id=to, device_id_type=MESH)

        local = [pltpu.make_async_copy(src(w, me), outs[w].at[_index(me)], local_sems.at[w]) for w in range(nw)]
        peers = [_peer(me, flip) for flip in FLIPS]
        sends = [copy(k, w, peer, me, peer) for k, peer in enumerate(peers) for w in range(nw)]
        recvs = [copy(k, w, me, peer, me) for k, peer in enumerate(peers) for w in range(nw)]
        return local, sends, recvs

    shapes = [jax.ShapeDtypeStruct((N_DEV, p.shape[0] // N_DEV // part[1], p.shape[1]), p.dtype) for p in parts]
    return _Exchange(parts, shapes, 7 * nw, nw, copies)


def _sum_over_devices(v):
    shape = v.shape

    def body(v_ref, sum_ref, all_ref, send_sems, recv_sems):
        me = _place()
        all_ref[_index(me)] = v_ref[...]
        sends = []
        for k, flip in enumerate(FLIPS):
            peer = _peer(me, flip)
            sends.append(pltpu.make_async_remote_copy(
                src_ref=v_ref, dst_ref=all_ref.at[_index(me)],
                send_sem=send_sems.at[k], recv_sem=recv_sems.at[k], device_id=peer, device_id_type=MESH))
            sends[-1].start()
        for k, flip in enumerate(FLIPS):
            peer = _peer(me, flip)
            pltpu.make_async_remote_copy(
                src_ref=v_ref, dst_ref=all_ref.at[_index(peer)],
                send_sem=send_sems.at[k], recv_sem=recv_sems.at[k], device_id=peer, device_id_type=MESH).wait_recv()
        for cp in sends:
            cp.wait_send()
        acc = all_ref[0]
        for s in range(1, N_DEV):
            acc = acc + all_ref[s]
        sum_ref[...] = acc

    vmem = pl.BlockSpec(memory_space=pltpu.VMEM)
    return pl.pallas_call(
        body, name="sum_small_grads",
        in_specs=[vmem], out_specs=[vmem, vmem],
        out_shape=[jax.ShapeDtypeStruct(shape, F32), jax.ShapeDtypeStruct((N_DEV,) + shape, F32)],
        scratch_shapes=[pltpu.SemaphoreType.DMA((7,)), pltpu.SemaphoreType.DMA((7,))],
    )(v)[0]


SMALL_ROWS = 8


def _pack_small(vectors):
    padded = []
    for vec in vectors:
        vec = vec.reshape(-1)
        padded.append(jnp.pad(vec, (0, -vec.shape[0] % 128)))
    flat = jnp.concatenate(padded)
    flat = jnp.pad(flat, (0, -flat.shape[0] % (SMALL_ROWS * 128)))
    return flat.reshape(SMALL_ROWS, -1)


def _unpack_small(packed, shapes):
    flat = packed.reshape(-1)
    out, off = [], 0
    for shp in shapes:
        n = int(np.prod(shp))
        out.append(flat[off:off + n].reshape(shp))
        off += n + (-n % 128)
    return out


def kernel(x, g_attn, w_in, b_in, sinks_a, g_out_a, g_out_b, w_out, g_mlp, w_1, w_2, g_final, loss_target, m_g_attn, m_w_in, m_b_in, m_sinks_a, m_g_out_a, m_g_out_b, m_w_out, m_g_mlp, m_w_1, m_w_2, m_g_final, v_g_attn, v_w_in, v_b_in, v_sinks_a, v_g_out_a, v_g_out_b, v_w_out, v_g_mlp, v_w_1, v_w_2, v_g_final):
    xs, tgt = x[0], loss_target[0]
    T, D = xs.shape
    n_a = QW + 2 * KV_HEADS_A * HEAD_DIM
    g_fin = g_final.reshape(1, D)

    shards = [w_in[0].T.astype(BF16), w_out[0].astype(BF16), w_1[0].T.astype(BF16), w_2[0].astype(BF16)]
    ident = lambda acc: (acc,)
    add = lambda acc, other: (acc + other,)
    tiles = dict(tm=512, tn=1024)

    h1, w_in_t = _norm_fwd(xs, g_attn, "norm_attn", exchange=_gather_rows(shards[:1], relay_at=1.0))
    n_in = w_in_t.shape[0]
    proj_a, = _proj_views(h1, w_in_t, b_in, (0, n_a), [1], "proj_a")
    dils = [dil for _, dil in DILATED_BRANCHES]
    *proj_b, w_o = _proj_views(h1, w_in_t, b_in, (n_a, n_in - n_a), dils, "proj_b", exchange=_gather_rows(shards[1:2]))

    lay_a = _AttnLayout(1, KV_HEADS_A, 0, 0, 0, QW // (KV_HEADS_A * HEAD_DIM), QW // (KV_HEADS_A * HEAD_DIM) + 1)
    bias_a = _band_bias(WINDOW_A - 1, 1)
    o_a, l_a, w_1_t = _attn_fwd(proj_a, bias_a, sinks_a, lay_a, "attn_a_fwd",
                                exchange=_gather_rows(shards[2:3], part=(0, 4)))
    branches = []
    for n, (window, dil) in enumerate(DILATED_BRANCHES):
        lay = _AttnLayout(dil, N_HEADS, 3, 0, 3, 1, 2)
        bias = _band_bias(window // dil, dil)
        ride = _gather_rows(shards[2:3], part=(n + 1, 4), into=[w_1_t])
        o, lse, w_1_t = _attn_fwd(proj_b[n], bias, None, lay, f"attn_b{dil}_fwd", exchange=ride)
        branches.append((lay, bias, proj_b[n], o, lse))
    o_b = [br[3] for br in branches]
    l_b = [br[4].transpose(2, 0, 1).reshape(T, N_HEADS) for br in branches]

    mix = _mix_fwd(o_a, o_b, l_b, g_out_a, g_out_b, dils)
    wide = dict(tm=512, tn=2048)

    def residual_and_norm(acc, res, g):
        x_new = acc + res
        return x_new, (x_new * _rstd(x_new)) * g

    assert wide["tn"] == D
    x2, h2 = _matmul(mix, w_o, "nn", [F32, BF16], residual_and_norm, tk=D, tile_ins=[xs], row_ins=[g_mlp],
                     name="out_proj", **wide)

    def relu_sq(acc):
        u = jnp.maximum(acc, 0.0)
        return u, u * u

    u, u_sq, w_2_f = _matmul(h2, w_1_t, "nt", [BF16, BF16], relu_sq, tk=D, name="mlp_up",
                             exchange=_gather_rows(shards[3:]), **wide)
    x3, = _matmul(u_sq, w_2_f, "nn", [F32], add, tk=4096, tile_ins=[x2], name="mlp_down", **tiles)

    dx3, dx3_b, dg_final, loss_dev = _loss_head(x3, tgt, g_fin)

    d_pre, = _matmul(dx3_b, w_2_f, "nt", [BF16], lambda acc, uu: (acc * (2.0 * uu.astype(F32)),),
                     tk=D, tile_ins=[u], name="mlp_down_bwd", **wide)
    wtiles = dict(tm=1024, tn=1024, tk=4096)
    dw_2, = _matmul(u_sq, dx3_b, "tn", [BF16], ident, name="mlp_down_wgrad", **wtiles)
    dh2, slots_2a = _matmul(d_pre, w_1_t, "nn", [BF16], ident, tk=4096, name="mlp_up_bwd",
                            exchange=_scatter_rows([dw_2], part=(0, 2)), **tiles)
    dw_1_t, slots_2b = _matmul(d_pre, h2, "tn", [BF16], ident, name="mlp_up_wgrad",
                               exchange=_scatter_rows([dw_2], part=(1, 2)), **wtiles)
    dx2, dg_mlp, dx2_b, dmix = _norm_bwd(dh2, x2, g_mlp, dx3, "norm_mlp_bwd", then_w_t=w_o)
    dw_o, = _matmul(mix, dx2_b, "tn", [BF16], ident, name="out_proj_wgrad", **wtiles)
    do_a, dd_a, do1, do2, do3, dd1, dd2, dd3, dg_out_a, dg_out_b, slots_o = _mix_bwd(
        dmix, o_a, o_b, l_b, g_out_a, g_out_b, dils, exchange=_scatter_rows([dw_o]))

    by_class = lambda d, dil: d.reshape(T // dil, dil, N_HEADS).transpose(1, 2, 0)
    slots_1 = [None] * 4
    dq_a, dk_a, dv_a, dsinks, slots_1[0] = _attn_bwd(proj_a, do_a, l_a, by_class(dd_a, 1), bias_a, sinks_a, lay_a,
                                                     "attn_a_bwd", exchange=_scatter_rows([dw_1_t], part=(0, 4)))
    dsinks = dsinks[:, 0].reshape(1, N_HEADS)
    dqs, dks, dvs = [], [], []
    for n, ((lay, bias, view, _, lse), do_n, dd_n) in enumerate(zip(branches, (do1, do2, do3), (dd1, dd2, dd3))):
        dq, dk, dv, slots_1[n + 1] = _attn_bwd(view, do_n, lse, by_class(dd_n, lay.dil), bias, None, lay,
                                               f"attn_b{lay.dil}_bwd",
                                               exchange=_scatter_rows([dw_1_t], part=(n + 1, 4)))
        dqs.append(dq)
        dks.append(dk)
        dvs.append(dv)
    dproj, db_in = _assemble([[dq_a], [dk_a], [dv_a], dqs, dks, dvs], "dproj", dils)

    dw_in_t, = _matmul(dproj, h1, "tn", [BF16], ident, tm=n_in // 2, tn=1024, tk=1024, name="in_proj_wgrad")
    dh1, slots_in = _matmul(dproj, w_in_t, "nn", [BF16], ident, tk=n_in, name="in_proj_bwd",
                            exchange=_scatter_rows([dw_in_t]), **tiles)
    dx, dg_attn = _norm_bwd(dh1, xs, g_attn, dx2, "norm_attn_bwd")

    g_w_in = _sum_slots(slots_in, "sum_w_in_grads").T
    g_w_out = _sum_slots(slots_o, "sum_w_out_grads")
    g_w_1 = jnp.concatenate([_sum_slots(s, f"sum_w_1_grads_{n}") for n, s in enumerate(slots_1)]).T
    g_w_2 = jnp.concatenate([_sum_slots(slots_2a, "sum_w_2_grads_0"), _sum_slots(slots_2b, "sum_w_2_grads_1")])

    small_w = [g_attn, b_in, sinks_a, g_out_a, g_out_b, g_mlp, g_final]
    small_m = [m_g_attn, m_b_in, m_sinks_a, m_g_out_a, m_g_out_b, m_g_mlp, m_g_final]
    small_v = [v_g_attn, v_b_in, v_sinks_a, v_g_out_a, v_g_out_b, v_g_mlp, v_g_final]
    small_g = [dg_attn, db_in, dsinks, dg_out_a, dg_out_b, dg_mlp, dg_final]
    summed = _sum_over_devices(_pack_small(small_g + [loss_dev[:, :1]]))
    shapes = [w.shape for w in small_w]
    *g_small, loss = _unpack_small(summed, shapes + [()])

    big = [
        _adamw(w_in, g_w_in, m_w_in, v_w_in, "adamw_w_in"),
        _adamw(w_out, g_w_out, m_w_out, v_w_out, "adamw_w_out"),
        _adamw(w_1, g_w_1, m_w_1, v_w_1, "adamw_w_1"),
        _adamw(w_2, g_w_2, m_w_2, v_w_2, "adamw_w_2"),
    ]
    g_packed = _pack_small(g_small)
    small = _adamw(_pack_small(small_w)[None], g_packed, _pack_small(small_m)[None], _pack_small(small_v)[None],
                   "adamw_small")
    small = [_unpack_small(s, shapes) for s in small]

    def ordered(small_list, big_list):
        s = list(small_list)
        return [s[0], big_list[0], s[1], s[2], s[3], s[4], big_list[1], s[5], big_list[2], big_list[3], s[6]]

    grads = ordered(g_small, [g[None] for g in (g_w_in, g_w_out, g_w_1, g_w_2)])
    deltas = ordered(small[0], [b[0] for b in big])
    new_m = ordered(small[1], [b[1] for b in big])
    new_v = ordered(small[2], [b[2] for b in big])
    return (loss, dx[None], *grads, *deltas, *new_m, *new_v)
```

```python
import numpy as np
import jax
import jax.numpy as jnp
from jax import lax
from jax.experimental import pallas as pl
from jax.experimental.pallas import tpu as pltpu

F32 = jnp.float32
BF16 = jnp.bfloat16

HEAD_DIM = 64
N_HEADS = 16
KV_HEADS_A = 2
BLOCK = 128
WINDOW_A = 128
DILATED_BRANCHES = ((128, 1), (512, 4), (2048, 16))
EPS = 1e-5
NEG_INF = -1e30
N_DEV = 8

ADAM_LR = 0.001
ADAM_B1 = 0.9
ADAM_B2 = 0.999
ADAM_EPS = 1e-08
ADAM_WD = 0.01
ADAM_STEP = 10

VMEM_LIMIT_BYTES = 56 * 1024 * 1024
MESH = pl.DeviceIdType.MESH
ANY = pl.BlockSpec(memory_space=pl.ANY)

NN = (((1,), (0,)), ((), ()))
NT = (((1,), (1,)), ((), ()))
TN = (((0,), (0,)), ((), ()))


def _dot(a, b, dims):
    return lax.dot_general(a, b, dims, preferred_element_type=F32)


def _params(*sem):
    return pltpu.CompilerParams(dimension_semantics=sem, vmem_limit_bytes=VMEM_LIMIT_BYTES)


RELAY_AT = 0.6


class _Exchange:
    def __init__(self, ins, out_shapes, n_remote, n_local, copies, aliases=None, relay=None, relay_at=RELAY_AT):
        self.ins, self.out_shapes = list(ins), list(out_shapes)
        self.n_remote, self.n_local = n_remote, n_local
        self.copies = copies
        self.relay = relay
        self.relay_at = relay_at
        self.aliases = aliases or {}

    def start(self, refs):
        local, sends, _ = self.copies(*refs)
        for cp in local + sends:
            cp.start()

    def middle(self, refs):
        arrived, onward = self.relay(*refs)
        for got, cp in zip(arrived, onward):
            got.wait_recv()
            cp.start()

    def finish(self, refs):
        local, sends, recvs = self.copies(*refs)
        for cp in recvs:
            cp.wait_recv()
        for cp in sends:
            cp.wait_send()
        for cp in local:
            cp.wait()
        if self.relay:
            for cp in self.relay(*refs)[1]:
                cp.wait_send()


class _Ride:
    def __init__(self, ex, n_in, n_out, n_scratch):
        self.ex = ex
        self.n = (n_in, n_out, n_scratch)
        self.args = ex.ins if ex else []
        self.in_specs = [ANY] * len(self.args)
        self.out_shapes = ex.out_shapes if ex else []
        self.out_specs = [ANY] * len(self.out_shapes)
        self.scratch = [pltpu.SemaphoreType.DMA((ex.n_remote,)), pltpu.SemaphoreType.DMA((ex.n_remote,)),
                        pltpu.SemaphoreType.DMA((max(ex.n_local, 1),))] if ex else []
        self.aliases = {n_in + i: n_out + o for i, o in ex.aliases.items()} if ex else {}

    def split(self, refs):
        n_in, n_out, n_scratch = self.n
        a = n_in
        b = a + len(self.args)
        c = b + n_out
        d = c + len(self.out_shapes)
        e = d + n_scratch
        return refs[:a], refs[b:c], refs[d:e], (refs[a:b], refs[c:d], *refs[e:])

    def around(self, step, n_steps, exrefs, compute):
        if self.ex is None:
            compute()
            return

        @pl.when(step == 0)
        def _():
            self.ex.start(exrefs)

        compute()

        if self.ex.relay:
            @pl.when(step == int(self.ex.relay_at * (n_steps - 1)))
            def _():
                self.ex.middle(exrefs)

        @pl.when(step == n_steps - 1)
        def _():
            self.ex.finish(exrefs)


TILE = dict(tm=512, tn=1024)
TILE_WHOLE_ROWS = dict(tm=512, tn=2048)
TILE_WGRAD = dict(tm=1024, tn=1024, tk=4096)

def _matmul(a, b, dims, out_dtypes, epilogue, *, tm, tn, tk, name, tile_ins=(), row_ins=(), exchange=None):
    if dims == "tn":
        K, M = a.shape
    else:
        M, K = a.shape
    N = b.shape[0] if dims == "nt" else b.shape[1]
    tm, tn, tk = min(tm, M), min(tn, N), min(tk, K)
    assert M % tm == 0 and N % tn == 0 and K % tk == 0, (name, M, N, K, tm, tn, tk)
    grid = (M // tm, N // tn, K // tk)
    nk = grid[2]
    n_tile, n_row, n_out = len(tile_ins), len(row_ins), len(out_dtypes)
    dn = {"nn": NN, "nt": NT, "tn": TN}[dims]
    ride = _Ride(exchange, 2 + n_tile + n_row, n_out, 1 if nk > 1 else 0)

    def kern(*refs):
        ins, out_refs, scratch, exrefs = ride.split(refs)
        a_ref, b_ref = ins[:2]
        tile_refs = ins[2:2 + n_tile]
        row_refs = ins[2 + n_tile:]
        ids = [pl.program_id(d) for d in range(3)]

        def finish(acc):
            outs = epilogue(acc, *[r[...] for r in tile_refs], *[r[...] for r in row_refs])
            for o_ref, o in zip(out_refs, outs):
                o_ref[...] = o.astype(o_ref.dtype)

        def compute():
            if nk == 1:
                finish(_dot(a_ref[...], b_ref[...], dn))
                return
            acc_ref = scratch[0]

            @pl.when(ids[2] == 0)
            def _():
                acc_ref[...] = jnp.zeros_like(acc_ref)

            acc_ref[...] += _dot(a_ref[...], b_ref[...], dn)

            @pl.when(ids[2] == nk - 1)
            def _():
                finish(acc_ref[...])

        ride.around((ids[0] * grid[1] + ids[1]) * grid[2] + ids[2], grid[0] * grid[1] * grid[2], exrefs, compute)

    if dims == "tn":
        a_spec = pl.BlockSpec((tk, tm), lambda i, j, k: (k, i))
    else:
        a_spec = pl.BlockSpec((tm, tk), lambda i, j, k: (i, k))
    if dims == "nt":
        b_spec = pl.BlockSpec((tn, tk), lambda i, j, k: (j, k))
    else:
        b_spec = pl.BlockSpec((tk, tn), lambda i, j, k: (k, j))
    tile_spec = pl.BlockSpec((tm, tn), lambda i, j, k: (i, j))
    row_spec = pl.BlockSpec((1, tn), lambda i, j, k: (0, j))
    sem = ("arbitrary",) * 3 if exchange else ("parallel", "parallel", "arbitrary")
    return pl.pallas_call(
        kern,
        name=name,
        grid=grid,
        in_specs=[a_spec, b_spec] + [tile_spec] * n_tile + [row_spec] * n_row + ride.in_specs,
        out_specs=[tile_spec] * n_out + ride.out_specs,
        out_shape=[jax.ShapeDtypeStruct((M, N), dt) for dt in out_dtypes] + ride.out_shapes,
        scratch_shapes=([pltpu.VMEM((tm, tn), F32)] if nk > 1 else []) + ride.scratch,
        input_output_aliases=ride.aliases,
        compiler_params=_params(*sem),
    )(a, b, *tile_ins, *row_ins, *ride.args)


PROJ_ROWS = 256


def _proj_views(a, w_t, bias, cols, dils, name, exchange=None):
    T, K = a.shape
    first, N = cols
    ride = _Ride(exchange, 3, len(dils), 0)

    def kern(*refs):
        (a_ref, w_ref, b_ref), outs, _, exrefs = ride.split(refs)

        def compute():
            acc = _dot(a_ref[...], w_ref[...], NT) + b_ref[...]
            for out_ref, dil in zip(outs, dils):
                _to_class_order(acc, out_ref, dil)

        ride.around(pl.program_id(0), T // PROJ_ROWS, exrefs, compute)

    return pl.pallas_call(
        kern, name=name, grid=(T // PROJ_ROWS,),
        in_specs=[pl.BlockSpec((PROJ_ROWS, K), lambda i: (i, 0)),
                  pl.BlockSpec((pl.Element(N), pl.Element(K)), lambda i: (first, 0)),
                  pl.BlockSpec((pl.Element(1), pl.Element(N)), lambda i: (0, first))] + ride.in_specs,
        out_specs=[_view_spec(PROJ_ROWS, N, d) for d in dils] + ride.out_specs,
        out_shape=[jax.ShapeDtypeStruct((T // d, d * N), BF16) for d in dils] + ride.out_shapes,
        scratch_shapes=ride.scratch,
        input_output_aliases=ride.aliases,
        compiler_params=_params("arbitrary"),
    )(a, w_t, bias, *ride.args)


ROWS = 256
LEAN_ROWS = 512
MIX_ROWS = 128


def _rstd(xv):
    return lax.rsqrt(jnp.mean(xv * xv, axis=-1, keepdims=True) + EPS)


def _norm_fwd(x, g, name, exchange=None):
    T, D = x.shape
    ride = _Ride(exchange, 2, 1, 0)

    def kern(*refs):
        (x_ref, g_ref), (h_ref,), _, exrefs = ride.split(refs)

        def compute():
            xv = x_ref[...]
            h_ref[...] = ((xv * _rstd(xv)) * g_ref[...]).astype(h_ref.dtype)

        ride.around(pl.program_id(0), T // ROWS, exrefs, compute)

    row = pl.BlockSpec((ROWS, D), lambda i: (i, 0))
    return pl.pallas_call(
        kern, name=name, grid=(T // ROWS,),
        in_specs=[row, pl.BlockSpec((1, D), lambda i: (0, 0))] + ride.in_specs,
        out_specs=[row] + ride.out_specs,
        out_shape=[jax.ShapeDtypeStruct((T, D), BF16)] + ride.out_shapes,
        scratch_shapes=ride.scratch, input_output_aliases=ride.aliases,
        compiler_params=_params("arbitrary"),
    )(x, g, *ride.args)


def _norm_bwd(dh, x, g, res, name, then_w_t=None):
    T, D = x.shape
    rows = PROJ_ROWS if then_w_t is not None else min(LEAN_ROWS, T)

    def kern(dh_ref, x_ref, g_ref, res_ref, *rest):
        if then_w_t is None:
            dx_ref, dg_ref = rest
        else:
            w_ref, dx_ref, dg_ref, dxb_ref, y_ref = rest

        @pl.when(pl.program_id(0) == 0)
        def _():
            dg_ref[...] = jnp.zeros_like(dg_ref)

        xv = x_ref[...]
        r = _rstd(xv)
        xn = xv * r
        dhv = dh_ref[...].astype(F32)
        dg_ref[...] += jnp.sum(dhv * xn, axis=0, keepdims=True)
        t = dhv * g_ref[...]
        dx = res_ref[...] + r * (t - xn * jnp.mean(t * xn, axis=-1, keepdims=True))
        dx_ref[...] = dx
        if then_w_t is not None:
            dxb = dx.astype(BF16)
            dxb_ref[...] = dxb
            y_ref[...] = _dot(dxb, w_ref[...], NT)

    row = pl.BlockSpec((rows, D), lambda i: (i, 0))
    vec = pl.BlockSpec((1, D), lambda i: (0, 0))
    in_specs, args = [row, row, vec, row], [dh, x, g, res]
    out_specs = [row, vec]
    out_shape = [jax.ShapeDtypeStruct((T, D), F32), jax.ShapeDtypeStruct((1, D), F32)]
    if then_w_t is not None:
        N = then_w_t.shape[0]
        in_specs.append(pl.BlockSpec((N, D), lambda i: (0, 0)))
        args.append(then_w_t)
        out_specs += [row, pl.BlockSpec((rows, N), lambda i: (i, 0))]
        out_shape += [jax.ShapeDtypeStruct((T, D), BF16), jax.ShapeDtypeStruct((T, N), F32)]
    return pl.pallas_call(
        kern, name=name, grid=(T // rows,), in_specs=in_specs, out_specs=out_specs, out_shape=out_shape,
        compiler_params=_params("arbitrary"),
    )(*args)


def _loss_head(x3, tgt, g):
    T, D = x3.shape

    def kern(x_ref, t_ref, g_ref, dx_ref, dxb_ref, dg_ref, loss_ref):
        @pl.when(pl.program_id(0) == 0)
        def _():
            dg_ref[...] = jnp.zeros_like(dg_ref)
            loss_ref[...] = jnp.zeros_like(loss_ref)

        xv = x_ref[...]
        gv = g_ref[...]
        r = _rstd(xv)
        xn = xv * r
        err = xn * gv - t_ref[...]
        per_tok = jnp.mean(err * err, axis=-1, keepdims=True)
        loss_ref[...] += 0.5 * jnp.sum(per_tok, axis=0, keepdims=True)
        dy = err * (1.0 / D)
        dg_ref[...] += jnp.sum(dy * xn, axis=0, keepdims=True)
        t = dy * gv
        dx = r * (t - xn * jnp.mean(t * xn, axis=-1, keepdims=True))
        dx_ref[...] = dx
        dxb_ref[...] = dx.astype(BF16)

    row = pl.BlockSpec((LEAN_ROWS, D), lambda i: (i, 0))
    vec = pl.BlockSpec((1, D), lambda i: (0, 0))
    return pl.pallas_call(
        kern, name="loss_head", grid=(T // LEAN_ROWS,),
        in_specs=[row, row, vec],
        out_specs=[row, row, vec, pl.BlockSpec((1, 128), lambda i: (0, 0))],
        out_shape=[jax.ShapeDtypeStruct((T, D), F32), jax.ShapeDtypeStruct((T, D), BF16),
                   jax.ShapeDtypeStruct((1, D), F32), jax.ShapeDtypeStruct((1, 128), F32)],
        compiler_params=_params("arbitrary"),
    )(x3, tgt, g)


def _spread_matrix():
    head_of_lane = np.arange(N_HEADS * HEAD_DIM) // HEAD_DIM
    return jnp.asarray(np.arange(N_HEADS)[:, None] == head_of_lane[None, :], dtype=BF16)


def _pieces(v, n):
    out = []
    for _ in range(n):
        piece = v.astype(BF16)
        out.append(piece)
        v = v - piece.astype(F32)
    return out


def _spread(v, spread):
    return sum(_dot(p, spread, NN) for p in _pieces(v, 2))


def _spread_weights(w1, w2, spread):
    s1, s2 = _spread(w1, spread), _spread(w2, spread)
    return s1, s2, 1.0 - s1 - s2


def _head_sums(v, spread):
    return sum(_dot(p, spread, NT) for p in _pieces(v, 2))


def _branch_weights(l1, l2, l3):
    lm = jnp.maximum(jnp.maximum(l1, l2), l3)
    e1, e2, e3 = jnp.exp(l1 - lm), jnp.exp(l2 - lm), jnp.exp(l3 - lm)
    inv = 1.0 / (e1 + e2 + e3)
    return e1 * inv, e2 * inv, e3 * inv


def _to_token_order(view_ref, dil):
    if dil == 1:
        return view_ref[...].astype(F32)
    n_l, w = view_ref.shape[0], view_ref.shape[1] // dil
    cols = []
    for cb in range(w // LANES):
        by_class = jnp.stack([view_ref[:, r * w + cb * LANES:r * w + (cb + 1) * LANES].astype(F32) for r in range(dil)])
        cols.append(jnp.swapaxes(by_class, 0, 1).reshape(n_l * dil, LANES))
    return jnp.concatenate(cols, axis=1)


def _to_class_order(val, view_ref, dil):
    if dil == 1:
        view_ref[...] = val.astype(view_ref.dtype)
        return
    n, w = val.shape
    for cb in range(w // LANES):
        by_class = jnp.swapaxes(val[:, cb * LANES:(cb + 1) * LANES].reshape(n // dil, dil, LANES), 0, 1)
        for r in range(dil):
            view_ref[:, r * w + cb * LANES:r * w + (cb + 1) * LANES] = by_class[r].astype(view_ref.dtype)


def _view_spec(rows, width, dil):
    return pl.BlockSpec((rows // dil, dil * width), lambda i: (i, 0))


def _mix_fwd(oa, obs, lbs, ga, gb, dils):
    T, W = oa.shape

    def kern(oa_ref, o1, o2, o3, l1, l2, l3, ga_ref, gb_ref, sp_ref, mix_ref):
        sp = sp_ref[...]
        w1, w2, w3 = _branch_weights(l1[...], l2[...], l3[...])
        on = [_to_token_order(o, d) for o, d in zip((o1, o2, o3), dils)]
        s1, s2, s3 = _spread_weights(w1, w2, sp)
        ob = s1 * on[0] + s2 * on[1] + s3 * on[2]
        oav = oa_ref[...]
        mix_ref[:, :W] = ((oav * _rstd(oav)) * ga_ref[...]).astype(BF16)
        mix_ref[:, W:] = ((ob * _rstd(ob)) * gb_ref[...]).astype(BF16)

    row = pl.BlockSpec((MIX_ROWS, W), lambda i: (i, 0))
    per_head = pl.BlockSpec((MIX_ROWS, N_HEADS), lambda i: (i, 0))
    vec = pl.BlockSpec((1, W), lambda i: (0, 0))
    return pl.pallas_call(
        kern, name="mix_fwd", grid=(T // MIX_ROWS,),
        in_specs=[row] + [_view_spec(MIX_ROWS, W, d) for d in dils] + [per_head] * 3
        + [vec, vec, pl.BlockSpec((N_HEADS, W), lambda i: (0, 0))],
        out_specs=pl.BlockSpec((MIX_ROWS, 2 * W), lambda i: (i, 0)),
        out_shape=jax.ShapeDtypeStruct((T, 2 * W), BF16),
        compiler_params=_params("parallel"),
    )(oa, *obs, *lbs, ga, gb, _spread_matrix())


def _mix_bwd(dmix, oa, obs, lbs, ga, gb, dils, exchange=None):
    T, W = oa.shape
    ride = _Ride(exchange, 11, 10, 0)

    def kern(*refs):
        ins, outs, _, exrefs = ride.split(refs)
        ride.around(pl.program_id(0), T // MIX_ROWS, exrefs, lambda: compute(*ins, *outs))

    def compute(dm_ref, oa_ref, o1, o2, o3, l1, l2, l3, ga_ref, gb_ref, sp_ref,
                doa_ref, da_ref, do1, do2, do3, d1, d2, d3, dga_ref, dgb_ref):
        @pl.when(pl.program_id(0) == 0)
        def _():
            dga_ref[...] = jnp.zeros_like(dga_ref)
            dgb_ref[...] = jnp.zeros_like(dgb_ref)

        sp = sp_ref[...]
        oav = oa_ref[...]
        r = _rstd(oav)
        on = oav * r
        dy = dm_ref[:, :W]
        dga_ref[...] += jnp.sum(dy * on, axis=0, keepdims=True)
        t = dy * ga_ref[...]
        doa = r * (t - on * jnp.mean(t * on, axis=-1, keepdims=True))
        doa_ref[...] = doa.astype(BF16)
        da_ref[...] = _head_sums(doa * oav, sp)
        w1, w2, w3 = _branch_weights(l1[...], l2[...], l3[...])
        s1, s2, s3 = _spread_weights(w1, w2, sp)
        on = [_to_token_order(o, d) for o, d in zip((o1, o2, o3), dils)]
        ob = s1 * on[0] + s2 * on[1] + s3 * on[2]
        r = _rstd(ob)
        on = ob * r
        dy = dm_ref[:, W:]
        dgb_ref[...] += jnp.sum(dy * on, axis=0, keepdims=True)
        t = dy * gb_ref[...]
        dob = r * (t - on * jnp.mean(t * on, axis=-1, keepdims=True))
        c = _head_sums(dob * ob, sp)
        for do_ref, sn, d in zip((do1, do2, do3), (s1, s2, s3), dils):
            _to_class_order(sn * dob, do_ref, d)
        d1[...] = w1 * c
        d2[...] = w2 * c
        d3[...] = w3 * c

    row = pl.BlockSpec((MIX_ROWS, W), lambda i: (i, 0))
    per_head = pl.BlockSpec((MIX_ROWS, N_HEADS), lambda i: (i, 0))
    vec = pl.BlockSpec((1, W), lambda i: (0, 0))
    bf = jax.ShapeDtypeStruct((T, W), BF16)
    ph = jax.ShapeDtypeStruct((T, N_HEADS), F32)
    vv = jax.ShapeDtypeStruct((1, W), F32)
    views = [_view_spec(MIX_ROWS, W, d) for d in dils]
    return pl.pallas_call(
        kern, name="mix_bwd", grid=(T // MIX_ROWS,),
        in_specs=[pl.BlockSpec((MIX_ROWS, 2 * W), lambda i: (i, 0)), row] + views + [per_head] * 3 + [vec, vec,
                  pl.BlockSpec((N_HEADS, W), lambda i: (0, 0))] + ride.in_specs,
        out_specs=[row, per_head] + views + [per_head, per_head, per_head, vec, vec] + ride.out_specs,
        out_shape=[bf, ph] + [jax.ShapeDtypeStruct(o.shape, F32) for o in obs] + [ph, ph, ph, vv, vv]
        + ride.out_shapes,
        scratch_shapes=ride.scratch,
        input_output_aliases=ride.aliases,
        compiler_params=_params("arbitrary"),
    )(dmix, oa, *obs, *lbs, ga, gb, _spread_matrix(), *ride.args)


def _alibi_slopes(n):
    return np.asarray(2.0 ** (-8.0 * (np.arange(n) + 1) / n)).astype(np.float32)


def _band_bias(max_steps, step_dist):
    qi = np.arange(BLOCK)[None, :]
    kj = np.arange(BLOCK)[:, None]
    slopes = _alibi_slopes(N_HEADS)
    halves = []
    for steps in (qi + BLOCK - kj, qi - kj):
        valid = (steps >= 0) & (steps <= max_steps)
        alibi = slopes[:, None, None] * (step_dist * steps).astype(np.float32)[None]
        halves.append(np.where(valid[None], -alibi, np.float32(NEG_INF)).astype(np.float32))
    per_head = np.concatenate(halves, axis=1)
    return jnp.asarray(np.concatenate([per_head[0::2], per_head[1::2]], axis=2))


class _AttnLayout:
    def __init__(self, dil, kv_heads, q_stride, q_off, k_stride, k_off, v_off):
        self.dil = dil
        self.kv_heads = kv_heads
        self.kw = kv_heads * HEAD_DIM
        self.rep = N_HEADS // kv_heads
        self.q_col = lambda r: r * q_stride + q_off
        self.k_col = lambda r: r * k_stride + k_off
        self.v_col = lambda r: r * k_stride + v_off


QW = N_HEADS * HEAD_DIM
LANES = 128


PAIRS = N_HEADS // 2


def _pair_cols(pair):
    return slice(pair * LANES, (pair + 1) * LANES)


def _first_head_lanes(shape):
    return lax.broadcasted_iota(jnp.int32, shape, 1) < HEAD_DIM


def _split_heads(pair):
    first = _first_head_lanes(pair.shape)
    zero = jnp.zeros_like(pair)
    return jnp.concatenate([jnp.where(first, pair, zero), jnp.where(first, zero, pair)], axis=0)


def _kv_pair(ref, pair, rep):
    if rep == 1:
        return ref[:, _pair_cols(pair)]
    blk = ref[...].astype(F32)
    other = pltpu.roll(blk, HEAD_DIM, 1)
    first = _first_head_lanes(blk.shape)
    both = jnp.where(first, blk, other) if (2 * pair // rep) % 2 == 0 else jnp.where(first, other, blk)
    return both.astype(ref.dtype)


def _paired_kv(prev_ref, cur_ref, rep, transposed=False):
    memo = {}

    def get(pair):
        key = pair if rep == 1 else 2 * pair // rep
        if key not in memo:
            blocks = [_kv_pair(ref, pair, rep) for ref in (prev_ref, cur_ref)]
            memo[key] = jnp.concatenate([b.T for b in blocks], axis=1) if transposed else jnp.concatenate(blocks, axis=0)
        return memo[key]

    return get


def _attn_fwd(proj, bias, sinks, lay, name, exchange=None):
    L = proj.shape[0]
    nb = L // BLOCK
    kw, rep = lay.kw, lay.rep
    use_sinks = sinks is not None
    scale = HEAD_DIM ** -0.5
    ride = _Ride(exchange, 7 if use_sinks else 6, 2, 2)

    def kern(*refs):
        ins, (o_ref, l_ref), (sc_ref, pr_ref), exrefs = ride.split(refs)
        q_ref, kc_ref, kp_ref, vc_ref, vp_ref, b_ref = ins[:6]
        s_ref = ins[6] if use_sinks else None
        r, i = pl.program_id(0), pl.program_id(1)
        first = i == 0
        ride.around(r * nb + i, lay.dil * nb, exrefs,
                    lambda: compute(q_ref, kc_ref, kp_ref, vc_ref, vp_ref, b_ref, s_ref, o_ref, l_ref, first,
                                    sc_ref, pr_ref))

    def compute(q_ref, kc_ref, kp_ref, vc_ref, vp_ref, b_ref, s_ref, o_ref, l_ref, first, sc_ref, pr_ref):
        keys, values_t = _paired_kv(kp_ref, kc_ref, rep), _paired_kv(vp_ref, vc_ref, rep, transposed=True)
        for pair in range(PAIRS):
            qs = _split_heads(q_ref[:, _pair_cols(pair)])
            s_prev = _dot(keys(pair)[:BLOCK], qs, NT) * scale + b_ref[pair, :BLOCK]
            sc_ref[pair, :BLOCK] = jnp.where(first, NEG_INF, s_prev)
            sc_ref[pair, BLOCK:] = _dot(keys(pair)[BLOCK:], qs, NT) * scale + b_ref[pair, BLOCK:]
        inv = []
        for h in range(N_HEADS):
            cols = slice(h % 2 * BLOCK, (h % 2 + 1) * BLOCK)
            s = sc_ref[h // 2, :, cols]
            m = jnp.max(s, axis=0, keepdims=True)
            if use_sinks:
                sink = s_ref[:, h:h + 1]
                m = jnp.maximum(m, sink)
            p = jnp.exp(s - m)
            denom = jnp.sum(p, axis=0, keepdims=True)
            if use_sinks:
                denom = denom + jnp.exp(sink - m)
            pr_ref[h // 2, :, cols] = p.astype(BF16)
            l_ref[h:h + 1, :] = m + jnp.log(denom)
            inv.append(1.0 / denom)
        for pair in range(PAIRS):
            both = _dot(values_t(pair), pr_ref[pair], NN)
            o_t = jnp.concatenate([both[:HEAD_DIM, :BLOCK] * inv[2 * pair], both[HEAD_DIM:, BLOCK:] * inv[2 * pair + 1]],
                                  axis=0)
            o_ref[:, _pair_cols(pair)] = o_t.T

    prev = lambda i: jnp.maximum(i - 1, 0)
    in_specs = [
        pl.BlockSpec((BLOCK, QW), lambda r, i: (i, lay.q_col(r))),
        pl.BlockSpec((BLOCK, kw), lambda r, i: (i, lay.k_col(r))),
        pl.BlockSpec((BLOCK, kw), lambda r, i: (prev(i), lay.k_col(r))),
        pl.BlockSpec((BLOCK, kw), lambda r, i: (i, lay.v_col(r))),
        pl.BlockSpec((BLOCK, kw), lambda r, i: (prev(i), lay.v_col(r))),
        pl.BlockSpec((PAIRS, 2 * BLOCK, 2 * BLOCK), lambda r, i: (0, 0, 0)),
    ]
    args = [proj, proj, proj, proj, proj, bias]
    if use_sinks:
        in_specs.append(pl.BlockSpec((1, N_HEADS), lambda r, i: (0, 0)))
        args.append(sinks)
    out_specs = [pl.BlockSpec((BLOCK, QW), lambda r, i: (i, r)),
                 pl.BlockSpec((None, N_HEADS, BLOCK), lambda r, i: (r, 0, i))]
    out_shape = [jax.ShapeDtypeStruct((L, lay.dil * QW), F32), jax.ShapeDtypeStruct((lay.dil, N_HEADS, L), F32)]
    return pl.pallas_call(
        kern, name=name, grid=(lay.dil, nb),
        in_specs=in_specs + ride.in_specs, out_specs=out_specs + ride.out_specs,
        out_shape=out_shape + ride.out_shapes,
        scratch_shapes=[pltpu.VMEM((PAIRS, 2 * BLOCK, 2 * BLOCK), dt) for dt in (F32, BF16)] + ride.scratch,
        input_output_aliases=ride.aliases,
        compiler_params=_params("arbitrary", "arbitrary"),
    )(*args, *ride.args)


def _attn_bwd(proj, do, lse, dd, bias, sinks, lay, name, exchange=None):
    L = proj.shape[0]
    nb = L // BLOCK
    kw, rep = lay.kw, lay.rep
    assert rep == 1 or lay.kv_heads == 2, "grouped queries: the two kv heads fill one 128-lane block"
    use_sinks = sinks is not None
    scale = HEAD_DIM ** -0.5
    ride = _Ride(exchange, 10 if use_sinks else 9, 4 if use_sinks else 3, 6)

    def kern(*refs):
        ins, outs, (ck_ref, cv_ref, *staged), exrefs = ride.split(refs)
        q_ref, kc_ref, kp_ref, vc_ref, vp_ref, do_ref, l_ref, d_ref, b_ref = ins[:9]
        s_ref = ins[9] if use_sinks else None
        dq_ref, dk_ref, dv_ref = outs[:3]
        ds_ref = outs[3] if use_sinks else None
        r = pl.program_id(0)
        i = pl.program_id(1)
        ride.around(r * (nb + 1) + i, lay.dil * (nb + 1), exrefs,
                    lambda: compute(q_ref, kc_ref, kp_ref, vc_ref, vp_ref, do_ref, l_ref, d_ref, b_ref, s_ref,
                                    dq_ref, dk_ref, dv_ref, ds_ref, ck_ref, cv_ref, r, i, *staged))

    def compute(q_ref, kc_ref, kp_ref, vc_ref, vp_ref, do_ref, l_ref, d_ref, b_ref, s_ref,
                dq_ref, dk_ref, dv_ref, ds_ref, ck_ref, cv_ref, r, i, sc_ref, dp_ref, pr_ref, dsc_ref):
        first = i == 0

        @pl.when(first)
        def _():
            ck_ref[...] = jnp.zeros_like(ck_ref)
            cv_ref[...] = jnp.zeros_like(cv_ref)

        if use_sinks:
            @pl.when(first & (r == 0))
            def _():
                ds_ref[...] = jnp.zeros_like(ds_ref)

        @pl.when(i < nb)
        def _():
            keys, values = _paired_kv(kp_ref, kc_ref, rep), _paired_kv(vp_ref, vc_ref, rep)
            keys_t = _paired_kv(kp_ref, kc_ref, rep, transposed=True)
            for pair in range(PAIRS):
                qs = _split_heads(q_ref[:, _pair_cols(pair)])
                dos = _split_heads(do_ref[:, _pair_cols(pair)].astype(BF16))
                s = _dot(keys(pair), qs, NT) * scale + b_ref[pair]
                sc_ref[pair, :BLOCK] = jnp.where(first, NEG_INF, s[:BLOCK])
                sc_ref[pair, BLOCK:] = s[BLOCK:]
                dp_ref[pair] = _dot(values(pair), dos, NT)
            for h in range(N_HEADS):
                cols = slice(h % 2 * BLOCK, (h % 2 + 1) * BLOCK)
                lrow = l_ref[h:h + 1, :]
                drow = d_ref[h:h + 1, :]
                p = jnp.exp(sc_ref[h // 2, :, cols] - lrow)
                pr_ref[h // 2, :, cols] = p.astype(BF16)
                dsc_ref[h // 2, :, cols] = (p * (dp_ref[h // 2, :, cols] - drow) * scale).astype(BF16)
                if use_sinks:
                    ds_ref[h:h + 1, :] += -(jnp.exp(s_ref[:, h:h + 1] - lrow) * drow)
            grouped = {}
            for pair in range(PAIRS):
                cols = _pair_cols(pair)
                qs = _split_heads(q_ref[:, cols])
                dos = _split_heads(do_ref[:, cols].astype(BF16))
                ds = dsc_ref[pair]
                both = _dot(keys_t(pair), ds, NN)
                dq_t = jnp.concatenate([both[:HEAD_DIM, :BLOCK], both[HEAD_DIM:, BLOCK:]], axis=0)
                dq_ref[:, cols] = dq_t.T.astype(dq_ref.dtype)
                dk = _dot(ds, qs, NN)
                dv = _dot(pr_ref[pair], dos, NN)
                if rep == 1:
                    dk_ref[:, cols] = (ck_ref[:, cols] + dk[:BLOCK]).astype(dk_ref.dtype)
                    dv_ref[:, cols] = (cv_ref[:, cols] + dv[:BLOCK]).astype(dv_ref.dtype)
                    ck_ref[:, cols] = dk[BLOCK:]
                    cv_ref[:, cols] = dv[BLOCK:]
                else:
                    g = 2 * pair // rep
                    grouped[g] = (dk, dv) if g not in grouped else (grouped[g][0] + dk, grouped[g][1] + dv)
            if rep > 1:
                fold = lambda t: t + pltpu.roll(t, HEAD_DIM, 1)
                first_half = _first_head_lanes((2 * BLOCK, LANES))
                dk = jnp.where(first_half, fold(grouped[0][0]), fold(grouped[1][0]))
                dv = jnp.where(first_half, fold(grouped[0][1]), fold(grouped[1][1]))
                dk_ref[...] = (ck_ref[...] + dk[:BLOCK]).astype(dk_ref.dtype)
                dv_ref[...] = (cv_ref[...] + dv[:BLOCK]).astype(dv_ref.dtype)
                ck_ref[...] = dk[BLOCK:]
                cv_ref[...] = dv[BLOCK:]

        @pl.when(i == nb)
        def _():
            dk_ref[...] = ck_ref[...].astype(dk_ref.dtype)
            dv_ref[...] = cv_ref[...].astype(dv_ref.dtype)
            if use_sinks:
                @pl.when(r == lay.dil - 1)
                def _():
                    ds_ref[...] = jnp.broadcast_to(jnp.sum(ds_ref[...], axis=1, keepdims=True), ds_ref.shape)

    cur = lambda i: jnp.minimum(i, nb - 1)
    prev = lambda i: jnp.maximum(jnp.minimum(i, nb - 1) - 1, 0)
    done = lambda i: jnp.maximum(i - 1, 0)
    qspec = lambda col: pl.BlockSpec((BLOCK, QW), lambda r, i: (cur(i), col(r)))
    per_head = pl.BlockSpec((None, N_HEADS, BLOCK), lambda r, i: (r, 0, cur(i)))
    in_specs = [
        qspec(lay.q_col),
        pl.BlockSpec((BLOCK, kw), lambda r, i: (cur(i), lay.k_col(r))),
        pl.BlockSpec((BLOCK, kw), lambda r, i: (prev(i), lay.k_col(r))),
        pl.BlockSpec((BLOCK, kw), lambda r, i: (cur(i), lay.v_col(r))),
        pl.BlockSpec((BLOCK, kw), lambda r, i: (prev(i), lay.v_col(r))),
        qspec(lambda r: r), per_head, per_head,
        pl.BlockSpec((PAIRS, 2 * BLOCK, 2 * BLOCK), lambda r, i: (0, 0, 0)),
    ]
    args = [proj, proj, proj, proj, proj, do, lse, dd, bias]
    out_specs = [
        qspec(lambda r: r),
        pl.BlockSpec((BLOCK, kw), lambda r, i: (done(i), r)),
        pl.BlockSpec((BLOCK, kw), lambda r, i: (done(i), r)),
    ]
    dkv_shape = jax.ShapeDtypeStruct((L, lay.dil * kw), BF16)
    out_shape = [jax.ShapeDtypeStruct((L, lay.dil * QW), BF16), dkv_shape, dkv_shape]
    if use_sinks:
        in_specs.append(pl.BlockSpec((1, N_HEADS), lambda r, i: (0, 0)))
        args.append(sinks)
        out_specs.append(pl.BlockSpec((N_HEADS, LANES), lambda r, i: (0, 0)))
        out_shape.append(jax.ShapeDtypeStruct((N_HEADS, LANES), F32))
    return pl.pallas_call(
        kern, name=name, grid=(lay.dil, nb + 1),
        in_specs=in_specs + ride.in_specs, out_specs=out_specs + ride.out_specs,
        out_shape=out_shape + ride.out_shapes,
        scratch_shapes=[pltpu.VMEM((BLOCK, kw), F32), pltpu.VMEM((BLOCK, kw), F32)]
        + [pltpu.VMEM((PAIRS, 2 * BLOCK, 2 * BLOCK), dt) for dt in (F32, F32, BF16, BF16)] + ride.scratch,
        input_output_aliases=ride.aliases,
        compiler_params=_params("arbitrary", "arbitrary"),
    )(*args, *ride.args)


def _assemble(groups, name, dils=(1,)):
    T = groups[0][0].shape[0] * dils[0]
    widths = [g[0].shape[1] // dils[0] for g in groups]
    total = sum(widths)
    flat = [a for g in groups for a in g]
    member_dils = [d for g in groups for d in dils[:len(g)]]

    def kern(*refs):
        ins = refs[:len(flat)]
        out_ref, cs_ref = refs[len(flat):]

        @pl.when(pl.program_id(0) == 0)
        def _():
            cs_ref[...] = jnp.zeros_like(cs_ref)

        pos = off = 0
        for g, w in zip(groups, widths):
            acc = _to_token_order(ins[pos], dils[0])
            for j in range(1, len(g)):
                acc = acc + _to_token_order(ins[pos + j], dils[j])
            pos += len(g)
            out_ref[:, off:off + w] = acc.astype(BF16)
            cs_ref[:, off:off + w] += jnp.sum(acc, axis=0, keepdims=True)
            off += w

    return pl.pallas_call(
        kern, name=name, grid=(T // ROWS,),
        in_specs=[_view_spec(ROWS, a.shape[1] // d, d) for a, d in zip(flat, member_dils)],
        out_specs=[pl.BlockSpec((ROWS, total), lambda i: (i, 0)), pl.BlockSpec((1, total), lambda i: (0, 0))],
        out_shape=[jax.ShapeDtypeStruct((T, total), BF16), jax.ShapeDtypeStruct((1, total), F32)],
        compiler_params=_params("arbitrary"),
    )(*flat)


def _adamw(w, g, m, v, name):
    _, R, C = w.shape
    rows = min(R, ROWS)
    assert R % rows == 0

    def kern(w_ref, g_ref, m_ref, v_ref, d_ref, nm_ref, nv_ref):
        gv = g_ref[...]
        mn = ADAM_B1 * m_ref[...] + (1.0 - ADAM_B1) * gv
        vn = ADAM_B2 * v_ref[...] + (1.0 - ADAM_B2) * jnp.square(gv)
        m_hat = mn / (1.0 - ADAM_B1 ** ADAM_STEP)
        v_hat = vn / (1.0 - ADAM_B2 ** ADAM_STEP)
        d_ref[...] = -ADAM_LR * (m_hat / (jnp.sqrt(v_hat) + ADAM_EPS) + ADAM_WD * w_ref[...])
        nm_ref[...] = mn
        nv_ref[...] = vn

    blk = pl.BlockSpec((None, rows, C), lambda i: (0, i, 0))
    shp = jax.ShapeDtypeStruct((1, R, C), F32)
    return pl.pallas_call(
        kern, name=name, grid=(R // rows,),
        in_specs=[blk, pl.BlockSpec((rows, C), lambda i: (i, 0)), blk, blk], out_specs=[blk] * 3, out_shape=[shp] * 3,
        compiler_params=_params("parallel"),
    )(w, g, m, v)


def _sum_slots(slots, name):
    n, R, C = slots.shape
    SUM_ROWS = next(rows for rows in (128, 64, 32, 16) if R % rows == 0)

    def kern(s_ref, o_ref):
        acc = s_ref[0].astype(F32)
        for k in range(1, n):
            acc = acc + s_ref[k].astype(F32)
        o_ref[...] = acc

    return pl.pallas_call(
        kern, name=name, grid=(R // SUM_ROWS,),
        in_specs=[pl.BlockSpec((n, SUM_ROWS, C), lambda i: (0, i, 0))],
        out_specs=pl.BlockSpec((SUM_ROWS, C), lambda i: (i, 0)),
        out_shape=jax.ShapeDtypeStruct((R, C), F32),
        compiler_params=_params("parallel"),
    )(slots)


def _place():
    return lax.axis_index("x"), lax.axis_index("y"), lax.axis_index("c")


def _index(p):
    return 4 * p[0] + 2 * p[1] + p[2]


FLIPS = [(fx, fy, fc) for fx in (0, 1) for fy in (0, 1) for fc in (0, 1)][1:]


def _peer(me, flip):
    return tuple(1 - a if f else a for a, f in zip(me, flip))


def _gather_rows(shards, part=(0, 1), into=None, relay_at=RELAY_AT):
    nw = len(shards)

    def plan(ins, outs, send_sems, recv_sems):
        x, y, c = me = _place()
        sibling = (x, y, 1 - c)
        chips = [(1 - x, y), (x, 1 - y), (1 - x, 1 - y)]

        def span(w):
            cnt = ins[w].shape[0] // part[1]
            return part[0] * cnt, cnt

        def rows(w, p):
            lo, cnt = span(w)
            return outs[w].at[pl.ds(_index(p) * ins[w].shape[0] + lo, cnt), :]

        def own(w):
            lo, cnt = span(w)
            return ins[w].at[pl.ds(lo, cnt), :]

        def copy(w, k, block, to):
            return pltpu.make_async_remote_copy(
                src_ref=own(w) if block is me else rows(w, block), dst_ref=rows(w, block),
                send_sem=send_sems.at[7 * w + k], recv_sem=recv_sems.at[7 * w + k],
                device_id=to, device_id_type=MESH)

        return me, sibling, chips, c, rows, own, copy

    def copies(ins, outs, send_sems, recv_sems, local_sems):
        me, sibling, chips, c, rows, own, copy = plan(ins, outs, send_sems, recv_sems)
        local = [pltpu.make_async_copy(own(w), rows(w, me), local_sems.at[w]) for w in range(nw)]
        sends, recvs = [], []
        for w in range(nw):
            sends.append(copy(w, 0, me, sibling))
            sends += [copy(w, 1 + j, me, (*chip, c)) for j, chip in enumerate(chips)]
            recvs.append(copy(w, 0, sibling, me))
            recvs += [copy(w, 4 + j, (*chip, 1 - c), me) for j, chip in enumerate(chips)]
        return local, sends, recvs

    def relay(ins, outs, send_sems, recv_sems, local_sems):
        me, sibling, chips, c, rows, own, copy = plan(ins, outs, send_sems, recv_sems)
        arrived = [copy(w, 1 + j, (*chip, c), me) for w in range(nw) for j, chip in enumerate(chips)]
        onward = [copy(w, 4 + j, (*chip, c), sibling) for w in range(nw) for j, chip in enumerate(chips)]
        return arrived, onward

    shapes = [jax.ShapeDtypeStruct((N_DEV * s.shape[0], s.shape[1]), s.dtype) for s in shards]
    aliases = {nw + w: w for w in range(nw)} if into else None
    return _Exchange(shards + (into or []), shapes, 7 * nw, nw, copies, aliases=aliases, relay=relay,
                     relay_at=relay_at)


def _scatter_rows(parts, part=(0, 1)):
    nw = len(parts)

    def copies(ins, outs, send_sems, recv_sems, local_sems):
        me = _place()

        def src(w, owner):
            n = ins[w].shape[0] // N_DEV
            cnt = n // part[1]
            return ins[w].at[pl.ds(_index(owner) * n + part[0] * cnt, cnt), :]

        def copy(k, w, owner, sender, to):
            return pltpu.make_async_remote_copy(
                src_ref=src(w, owner), dst_ref=outs[w].at[_index(sender)],
                send_sem=send_sems.at[nw * k + w], recv_sem=recv_sems.at[nw * k + w],
                device_id=to, device_id_type=MESH)

        local = [pltpu.make_async_copy(src(w, me), outs[w].at[_index(me)], local_sems.at[w]) for w in range(nw)]
        peers = [_peer(me, flip) for flip in FLIPS]
        sends = [copy(k, w, peer, me, peer) for k, peer in enumerate(peers) for w in range(nw)]
        recvs = [copy(k, w, me, peer, me) for k, peer in enumerate(peers) for w in range(nw)]
        return local, sends, recvs

    shapes = [jax.ShapeDtypeStruct((N_DEV, p.shape[0] // N_DEV // part[1], p.shape[1]), p.dtype) for p in parts]
    return _Exchange(parts, shapes, 7 * nw, nw, copies)


def _sum_over_devices(v):
    shape = v.shape

    def body(v_ref, sum_ref, all_ref, send_sems, recv_sems):
        me = _place()
        all_ref[_index(me)] = v_ref[...]
        sends = []
        for k, flip in enumerate(FLIPS):
            peer = _peer(me, flip)
            sends.append(pltpu.make_async_remote_copy(
                src_ref=v_ref, dst_ref=all_ref.at[_index(me)],
                send_sem=send_sems.at[k], recv_sem=recv_sems.at[k], device_id=peer, device_id_type=MESH))
            sends[-1].start()
        for k, flip in enumerate(FLIPS):
            peer = _peer(me, flip)
            pltpu.make_async_remote_copy(
                src_ref=v_ref, dst_ref=all_ref.at[_index(peer)],
                send_sem=send_sems.at[k], recv_sem=recv_sems.at[k], device_id=peer, device_id_type=MESH).wait_recv()
        for cp in sends:
            cp.wait_send()
        acc = all_ref[0]
        for s in range(1, N_DEV):
            acc = acc + all_ref[s]
        sum_ref[...] = acc

    vmem = pl.BlockSpec(memory_space=pltpu.VMEM)
    return pl.pallas_call(
        body, name="sum_small_grads",
        in_specs=[vmem], out_specs=[vmem, vmem],
        out_shape=[jax.ShapeDtypeStruct(shape, F32), jax.ShapeDtypeStruct((N_DEV,) + shape, F32)],
        scratch_shapes=[pltpu.SemaphoreType.DMA((7,)), pltpu.SemaphoreType.DMA((7,))],
    )(v)[0]


SMALL_ROWS = 8


def _pack_small(vectors):
    padded = []
    for vec in vectors:
        vec = vec.reshape(-1)
        padded.append(jnp.pad(vec, (0, -vec.shape[0] % 128)))
    flat = jnp.concatenate(padded)
    flat = jnp.pad(flat, (0, -flat.shape[0] % (SMALL_ROWS * 128)))
    return flat.reshape(SMALL_ROWS, -1)


def _unpack_small(packed, shapes):
    flat = packed.reshape(-1)
    out, off = [], 0
    for shp in shapes:
        n = int(np.prod(shp))
        out.append(flat[off:off + n].reshape(shp))
        off += n + (-n % 128)
    return out


def kernel(x, g_attn, w_in, b_in, sinks_a, g_out_a, g_out_b, w_out, g_mlp, w_1, w_2, g_final, loss_target, m_g_attn, m_w_in, m_b_in, m_sinks_a, m_g_out_a, m_g_out_b, m_w_out, m_g_mlp, m_w_1, m_w_2, m_g_final, v_g_attn, v_w_in, v_b_in, v_sinks_a, v_g_out_a, v_g_out_b, v_w_out, v_g_mlp, v_w_1, v_w_2, v_g_final):
    xs, tgt = x[0], loss_target[0]
    T, D = xs.shape
    n_a = QW + 2 * KV_HEADS_A * HEAD_DIM
    g_fin = g_final.reshape(1, D)

    shards = [w_in[0].T.astype(BF16), w_out[0].astype(BF16), w_1[0].T.astype(BF16), w_2[0].astype(BF16)]
    ident = lambda acc: (acc,)
    add = lambda acc, other: (acc + other,)

    h1, w_in_t = _norm_fwd(xs, g_attn, "norm_attn", exchange=_gather_rows(shards[:1], relay_at=1.0))
    n_in = w_in_t.shape[0]
    proj_a, = _proj_views(h1, w_in_t, b_in, (0, n_a), [1], "proj_a")
    dils = [dil for _, dil in DILATED_BRANCHES]
    *proj_b, w_o = _proj_views(h1, w_in_t, b_in, (n_a, n_in - n_a), dils, "proj_b", exchange=_gather_rows(shards[1:2]))

    lay_a = _AttnLayout(1, KV_HEADS_A, 0, 0, 0, QW // (KV_HEADS_A * HEAD_DIM), QW // (KV_HEADS_A * HEAD_DIM) + 1)
    bias_a = _band_bias(WINDOW_A - 1, 1)
    o_a, l_a, w_1_t = _attn_fwd(proj_a, bias_a, sinks_a, lay_a, "attn_a_fwd",
                                exchange=_gather_rows(shards[2:3], part=(0, 4)))
    branches = []
    for n, (window, dil) in enumerate(DILATED_BRANCHES):
        lay = _AttnLayout(dil, N_HEADS, 3, 0, 3, 1, 2)
        bias = _band_bias(window // dil, dil)
        ride = _gather_rows(shards[2:3], part=(n + 1, 4), into=[w_1_t])
        o, lse, w_1_t = _attn_fwd(proj_b[n], bias, None, lay, f"attn_b{dil}_fwd", exchange=ride)
        branches.append((lay, bias, proj_b[n], o, lse))
    o_b = [br[3] for br in branches]
    l_b = [br[4].transpose(2, 0, 1).reshape(T, N_HEADS) for br in branches]

    mix = _mix_fwd(o_a, o_b, l_b, g_out_a, g_out_b, dils)
    def residual_and_norm(acc, res, g):
        x_new = acc + res
        return x_new, (x_new * _rstd(x_new)) * g

    assert TILE_WHOLE_ROWS["tn"] == D
    x2, h2 = _matmul(mix, w_o, "nn", [F32, BF16], residual_and_norm, tk=D, tile_ins=[xs], row_ins=[g_mlp],
                     name="out_proj", **TILE_WHOLE_ROWS)

    def relu_sq(acc):
        u = jnp.maximum(acc, 0.0)
        return u, u * u

    u, u_sq, w_2_f = _matmul(h2, w_1_t, "nt", [BF16, BF16], relu_sq, tk=D, name="mlp_up",
                             exchange=_gather_rows(shards[3:]), **TILE_WHOLE_ROWS)
    x3, = _matmul(u_sq, w_2_f, "nn", [F32], add, tk=4096, tile_ins=[x2], name="mlp_down", **TILE)

    dx3, dx3_b, dg_final, loss_dev = _loss_head(x3, tgt, g_fin)

    d_pre, = _matmul(dx3_b, w_2_f, "nt", [BF16], lambda acc, uu: (acc * (2.0 * uu.astype(F32)),),
                     tk=D, tile_ins=[u], name="mlp_down_bwd", **TILE_WHOLE_ROWS)
    dw_2, = _matmul(u_sq, dx3_b, "tn", [BF16], ident, name="mlp_down_wgrad", **TILE_WGRAD)
    dh2, slots_2a = _matmul(d_pre, w_1_t, "nn", [BF16], ident, tk=4096, name="mlp_up_bwd",
                            exchange=_scatter_rows([dw_2], part=(0, 2)), **TILE)
    dw_1_t, slots_2b = _matmul(d_pre, h2, "tn", [BF16], ident, name="mlp_up_wgrad",
                               exchange=_scatter_rows([dw_2], part=(1, 2)), **TILE_WGRAD)
    dx2, dg_mlp, dx2_b, dmix = _norm_bwd(dh2, x2, g_mlp, dx3, "norm_mlp_bwd", then_w_t=w_o)
    dw_o, = _matmul(mix, dx2_b, "tn", [BF16], ident, name="out_proj_wgrad", **TILE_WGRAD)
    do_a, dd_a, do1, do2, do3, dd1, dd2, dd3, dg_out_a, dg_out_b, slots_o = _mix_bwd(
        dmix, o_a, o_b, l_b, g_out_a, g_out_b, dils, exchange=_scatter_rows([dw_o]))

    by_class = lambda d, dil: d.reshape(T // dil, dil, N_HEADS).transpose(1, 2, 0)
    slots_1 = [None] * 4
    dq_a, dk_a, dv_a, dsinks, slots_1[0] = _attn_bwd(proj_a, do_a, l_a, by_class(dd_a, 1), bias_a, sinks_a, lay_a,
                                                     "attn_a_bwd", exchange=_scatter_rows([dw_1_t], part=(0, 4)))
    dsinks = dsinks[:, 0].reshape(1, N_HEADS)
    dqs, dks, dvs = [], [], []
    for n, ((lay, bias, view, _, lse), do_n, dd_n) in enumerate(zip(branches, (do1, do2, do3), (dd1, dd2, dd3))):
        dq, dk, dv, slots_1[n + 1] = _attn_bwd(view, do_n, lse, by_class(dd_n, lay.dil), bias, None, lay,
                                               f"attn_b{lay.dil}_bwd",
                                               exchange=_scatter_rows([dw_1_t], part=(n + 1, 4)))
        dqs.append(dq)
        dks.append(dk)
        dvs.append(dv)
    dproj, db_in = _assemble([[dq_a], [dk_a], [dv_a], dqs, dks, dvs], "dproj", dils)

    dw_in_t, = _matmul(dproj, h1, "tn", [BF16], ident, tm=n_in // 2, tn=1024, tk=1024, name="in_proj_wgrad")
    dh1, slots_in = _matmul(dproj, w_in_t, "nn", [BF16], ident, tk=n_in, name="in_proj_bwd",
                            exchange=_scatter_rows([dw_in_t]), **TILE)
    dx, dg_attn = _norm_bwd(dh1, xs, g_attn, dx2, "norm_attn_bwd")

    g_w_in = _sum_slots(slots_in, "sum_w_in_grads").T
    g_w_out = _sum_slots(slots_o, "sum_w_out_grads")
    g_w_1 = jnp.concatenate([_sum_slots(s, f"sum_w_1_grads_{n}") for n, s in enumerate(slots_1)]).T
    g_w_2 = jnp.concatenate([_sum_slots(slots_2a, "sum_w_2_grads_0"), _sum_slots(slots_2b, "sum_w_2_grads_1")])

    small_w = [g_attn, b_in, sinks_a, g_out_a, g_out_b, g_mlp, g_final]
    small_m = [m_g_attn, m_b_in, m_sinks_a, m_g_out_a, m_g_out_b, m_g_mlp, m_g_final]
    small_v = [v_g_attn, v_b_in, v_sinks_a, v_g_out_a, v_g_out_b, v_g_mlp, v_g_final]
    small_g = [dg_attn, db_in, dsinks, dg_out_a, dg_out_b, dg_mlp, dg_final]
    summed = _sum_over_devices(_pack_small(small_g + [loss_dev[:, :1]]))
    shapes = [w.shape for w in small_w]
    *g_small, loss = _unpack_small(summed, shapes + [()])

    big = [
        _adamw(w_in, g_w_in, m_w_in, v_w_in, "adamw_w_in"),
        _adamw(w_out, g_w_out, m_w_out, v_w_out, "adamw_w_out"),
        _adamw(w_1, g_w_1, m_w_1, v_w_1, "adamw_w_1"),
        _adamw(w_2, g_w_2, m_w_2, v_w_2, "adamw_w_2"),
    ]
    g_packed = _pack_small(g_small)
    small = _adamw(_pack_small(small_w)[None], g_packed, _pack_small(small_m)[None], _pack_small(small_v)[None],
                   "adamw_small")
    small = [_unpack_small(s, shapes) for s in small]

    def ordered(small_list, big_list):
        s = list(small_list)
        return [s[0], big_list[0], s[1], s[2], s[3], s[4], big_list[1], s[5], big_list[2], big_list[3], s[6]]

    grads = ordered(g_small, [g[None] for g in (g_w_in, g_w_out, g_w_1, g_w_2)])
    deltas = ordered(small[0], [b[0] for b in big])
    new_m = ordered(small[1], [b[1] for b in big])
    new_v = ordered(small[2], [b[2] for b in big])
    return (loss, dx[None], *grads, *deltas, *new_m, *new_v)
```

```python
import numpy as np
import jax
import jax.numpy as jnp
from jax import lax
from jax.experimental import pallas as pl
from jax.experimental.pallas import tpu as pltpu

F32 = jnp.float32
BF16 = jnp.bfloat16

HEAD_DIM = 64
N_HEADS = 16
KV_HEADS_A = 2
BLOCK = 128
WINDOW_A = 128
DILATED_BRANCHES = ((128, 1), (512, 4), (2048, 16))
EPS = 1e-5
NEG_INF = -1e30
N_DEV = 8

ADAM_LR = 0.001
ADAM_B1 = 0.9
ADAM_B2 = 0.999
ADAM_EPS = 1e-08
ADAM_WD = 0.01
ADAM_STEP = 10

VMEM_LIMIT_BYTES = 56 * 1024 * 1024
MESH = pl.DeviceIdType.MESH
ANY = pl.BlockSpec(memory_space=pl.ANY)

NN = (((1,), (0,)), ((), ()))
NT = (((1,), (1,)), ((), ()))
TN = (((0,), (0,)), ((), ()))


def _dot(a, b, dims):
    return lax.dot_general(a, b, dims, preferred_element_type=F32)


def _params(*sem):
    return pltpu.CompilerParams(dimension_semantics=sem, vmem_limit_bytes=VMEM_LIMIT_BYTES)


RELAY_AT = 0.6


class _Exchange:
    def __init__(self, ins, out_shapes, n_remote, n_local, copies, aliases=None, relay=None, relay_at=RELAY_AT):
        self.ins, self.out_shapes = list(ins), list(out_shapes)
        self.n_remote, self.n_local = n_remote, n_local
        self.copies = copies
        self.relay = relay
        self.relay_at = relay_at
        self.aliases = aliases or {}

    def start(self, refs):
        local, sends, _ = self.copies(*refs)
        for cp in local + sends:
            cp.start()

    def middle(self, refs):
        arrived, onward = self.relay(*refs)
        for got, cp in zip(arrived, onward):
            got.wait_recv()
            cp.start()

    def finish(self, refs):
        local, sends, recvs = self.copies(*refs)
        for cp in recvs:
            cp.wait_recv()
        for cp in sends:
            cp.wait_send()
        for cp in local:
            cp.wait()
        if self.relay:
            for cp in self.relay(*refs)[1]:
                cp.wait_send()


class _Ride:
    def __init__(self, ex, n_in, n_out, n_scratch):
        self.ex = ex
        self.n = (n_in, n_out, n_scratch)
        self.args = ex.ins if ex else []
        self.in_specs = [ANY] * len(self.args)
        self.out_shapes = ex.out_shapes if ex else []
        self.out_specs = [ANY] * len(self.out_shapes)
        self.scratch = [pltpu.SemaphoreType.DMA((ex.n_remote,)), pltpu.SemaphoreType.DMA((ex.n_remote,)),
                        pltpu.SemaphoreType.DMA((max(ex.n_local, 1),))] if ex else []
        self.aliases = {n_in + i: n_out + o for i, o in ex.aliases.items()} if ex else {}

    def split(self, refs):
        n_in, n_out, n_scratch = self.n
        a = n_in
        b = a + len(self.args)
        c = b + n_out
        d = c + len(self.out_shapes)
        e = d + n_scratch
        return refs[:a], refs[b:c], refs[d:e], (refs[a:b], refs[c:d], *refs[e:])

    def around(self, step, n_steps, exrefs, compute):
        if self.ex is None:
            compute()
            return

        @pl.when(step == 0)
        def _():
            self.ex.start(exrefs)

        compute()

        if self.ex.relay:
            @pl.when(step == int(self.ex.relay_at * (n_steps - 1)))
            def _():
                self.ex.middle(exrefs)

        @pl.when(step == n_steps - 1)
        def _():
            self.ex.finish(exrefs)


TILE = dict(tm=512, tn=1024)
TILE_WHOLE_ROWS = dict(tm=512, tn=2048)
TILE_WGRAD = dict(tm=1024, tn=1024, tk=4096)

def _matmul(a, b, dims, out_dtypes, epilogue, *, tm, tn, tk, name, tile_ins=(), row_ins=(), exchange=None):
    if dims == "tn":
        K, M = a.shape
    else:
        M, K = a.shape
    N = b.shape[0] if dims == "nt" else b.shape[1]
    tm, tn, tk = min(tm, M), min(tn, N), min(tk, K)
    assert M % tm == 0 and N % tn == 0 and K % tk == 0, (name, M, N, K, tm, tn, tk)
    grid = (M // tm, N // tn, K // tk)
    nk = grid[2]
    n_tile, n_row, n_out = len(tile_ins), len(row_ins), len(out_dtypes)
    dn = {"nn": NN, "nt": NT, "tn": TN}[dims]
    ride = _Ride(exchange, 2 + n_tile + n_row, n_out, 1 if nk > 1 else 0)

    def kern(*refs):
        ins, out_refs, scratch, exrefs = ride.split(refs)
        a_ref, b_ref = ins[:2]
        tile_refs = ins[2:2 + n_tile]
        row_refs = ins[2 + n_tile:]
        ids = [pl.program_id(d) for d in range(3)]

        def finish(acc):
            outs = epilogue(acc, *[r[...] for r in tile_refs], *[r[...] for r in row_refs])
            for o_ref, o in zip(out_refs, outs):
                o_ref[...] = o.astype(o_ref.dtype)

        def compute():
            if nk == 1:
                finish(_dot(a_ref[...], b_ref[...], dn))
                return
            acc_ref = scratch[0]

            @pl.when(ids[2] == 0)
            def _():
                acc_ref[...] = jnp.zeros_like(acc_ref)

            acc_ref[...] += _dot(a_ref[...], b_ref[...], dn)

            @pl.when(ids[2] == nk - 1)
            def _():
                finish(acc_ref[...])

        ride.around((ids[0] * grid[1] + ids[1]) * grid[2] + ids[2], grid[0] * grid[1] * grid[2], exrefs, compute)

    if dims == "tn":
        a_spec = pl.BlockSpec((tk, tm), lambda i, j, k: (k, i))
    else:
        a_spec = pl.BlockSpec((tm, tk), lambda i, j, k: (i, k))
    if dims == "nt":
        b_spec = pl.BlockSpec((tn, tk), lambda i, j, k: (j, k))
    else:
        b_spec = pl.BlockSpec((tk, tn), lambda i, j, k: (k, j))
    tile_spec = pl.BlockSpec((tm, tn), lambda i, j, k: (i, j))
    row_spec = pl.BlockSpec((1, tn), lambda i, j, k: (0, j))
    sem = ("arbitrary",) * 3 if exchange else ("parallel", "parallel", "arbitrary")
    return pl.pallas_call(
        kern,
        name=name,
        grid=grid,
        in_specs=[a_spec, b_spec] + [tile_spec] * n_tile + [row_spec] * n_row + ride.in_specs,
        out_specs=[tile_spec] * n_out + ride.out_specs,
        out_shape=[jax.ShapeDtypeStruct((M, N), dt) for dt in out_dtypes] + ride.out_shapes,
        scratch_shapes=([pltpu.VMEM((tm, tn), F32)] if nk > 1 else []) + ride.scratch,
        input_output_aliases=ride.aliases,
        compiler_params=_params(*sem),
    )(a, b, *tile_ins, *row_ins, *ride.args)


PROJ_ROWS = 256


def _proj_views(a, w_t, bias, cols, dils, name, exchange=None):
    T, K = a.shape
    first, N = cols
    ride = _Ride(exchange, 3, len(dils), 0)

    def kern(*refs):
        (a_ref, w_ref, b_ref), outs, _, exrefs = ride.split(refs)

        def compute():
            acc = _dot(a_ref[...], w_ref[...], NT) + b_ref[...]
            for out_ref, dil in zip(outs, dils):
                _to_class_order(acc, out_ref, dil)

        ride.around(pl.program_id(0), T // PROJ_ROWS, exrefs, compute)

    return pl.pallas_call(
        kern, name=name, grid=(T // PROJ_ROWS,),
        in_specs=[pl.BlockSpec((PROJ_ROWS, K), lambda i: (i, 0)),
                  pl.BlockSpec((pl.Element(N), pl.Element(K)), lambda i: (first, 0)),
                  pl.BlockSpec((pl.Element(1), pl.Element(N)), lambda i: (0, first))] + ride.in_specs,
        out_specs=[_view_spec(PROJ_ROWS, N, d) for d in dils] + ride.out_specs,
        out_shape=[jax.ShapeDtypeStruct((T // d, d * N), BF16) for d in dils] + ride.out_shapes,
        scratch_shapes=ride.scratch,
        input_output_aliases=ride.aliases,
        compiler_params=_params("arbitrary"),
    )(a, w_t, bias, *ride.args)


ROWS = 256
LEAN_ROWS = 512
MIX_ROWS = 128


def _rstd(xv):
    return lax.rsqrt(jnp.mean(xv * xv, axis=-1, keepdims=True) + EPS)


def _norm_fwd(x, g, name, exchange=None):
    T, D = x.shape
    ride = _Ride(exchange, 2, 1, 0)

    def kern(*refs):
        (x_ref, g_ref), (h_ref,), _, exrefs = ride.split(refs)

        def compute():
            xv = x_ref[...]
            h_ref[...] = ((xv * _rstd(xv)) * g_ref[...]).astype(h_ref.dtype)

        ride.around(pl.program_id(0), T // ROWS, exrefs, compute)

    row = pl.BlockSpec((ROWS, D), lambda i: (i, 0))
    return pl.pallas_call(
        kern, name=name, grid=(T // ROWS,),
        in_specs=[row, pl.BlockSpec((1, D), lambda i: (0, 0))] + ride.in_specs,
        out_specs=[row] + ride.out_specs,
        out_shape=[jax.ShapeDtypeStruct((T, D), BF16)] + ride.out_shapes,
        scratch_shapes=ride.scratch, input_output_aliases=ride.aliases,
        compiler_params=_params("arbitrary"),
    )(x, g, *ride.args)


def _norm_bwd(dh, x, g, res, name, then_w_t=None):
    T, D = x.shape
    rows = PROJ_ROWS if then_w_t is not None else min(LEAN_ROWS, T)

    def kern(dh_ref, x_ref, g_ref, res_ref, *rest):
        if then_w_t is None:
            dx_ref, dg_ref = rest
        else:
            w_ref, dx_ref, dg_ref, dxb_ref, y_ref = rest

        @pl.when(pl.program_id(0) == 0)
        def _():
            dg_ref[...] = jnp.zeros_like(dg_ref)

        xv = x_ref[...]
        r = _rstd(xv)
        xn = xv * r
        dhv = dh_ref[...].astype(F32)
        dg_ref[...] += jnp.sum(dhv * xn, axis=0, keepdims=True)
        t = dhv * g_ref[...]
        dx = res_ref[...] + r * (t - xn * jnp.mean(t * xn, axis=-1, keepdims=True))
        dx_ref[...] = dx
        if then_w_t is not None:
            dxb = dx.astype(BF16)
            dxb_ref[...] = dxb
            y_ref[...] = _dot(dxb, w_ref[...], NT)

    row = pl.BlockSpec((rows, D), lambda i: (i, 0))
    vec = pl.BlockSpec((1, D), lambda i: (0, 0))
    in_specs, args = [row, row, vec, row], [dh, x, g, res]
    out_specs = [row, vec]
    out_shape = [jax.ShapeDtypeStruct((T, D), F32), jax.ShapeDtypeStruct((1, D), F32)]
    if then_w_t is not None:
        N = then_w_t.shape[0]
        in_specs.append(pl.BlockSpec((N, D), lambda i: (0, 0)))
        args.append(then_w_t)
        out_specs += [row, pl.BlockSpec((rows, N), lambda i: (i, 0))]
        out_shape += [jax.ShapeDtypeStruct((T, D), BF16), jax.ShapeDtypeStruct((T, N), F32)]
    return pl.pallas_call(
        kern, name=name, grid=(T // rows,), in_specs=in_specs, out_specs=out_specs, out_shape=out_shape,
        compiler_params=_params("arbitrary"),
    )(*args)


def _loss_head(x3, tgt, g):
    T, D = x3.shape

    def kern(x_ref, t_ref, g_ref, dx_ref, dxb_ref, dg_ref, loss_ref):
        @pl.when(pl.program_id(0) == 0)
        def _():
            dg_ref[...] = jnp.zeros_like(dg_ref)
            loss_ref[...] = jnp.zeros_like(loss_ref)

        xv = x_ref[...]
        gv = g_ref[...]
        r = _rstd(xv)
        xn = xv * r
        err = xn * gv - t_ref[...]
        per_tok = jnp.mean(err * err, axis=-1, keepdims=True)
        loss_ref[...] += 0.5 * jnp.sum(per_tok, axis=0, keepdims=True)
        dy = err * (1.0 / D)
        dg_ref[...] += jnp.sum(dy * xn, axis=0, keepdims=True)
        t = dy * gv
        dx = r * (t - xn * jnp.mean(t * xn, axis=-1, keepdims=True))
        dx_ref[...] = dx
        dxb_ref[...] = dx.astype(BF16)

    row = pl.BlockSpec((LEAN_ROWS, D), lambda i: (i, 0))
    vec = pl.BlockSpec((1, D), lambda i: (0, 0))
    return pl.pallas_call(
        kern, name="loss_head", grid=(T // LEAN_ROWS,),
        in_specs=[row, row, vec],
        out_specs=[row, row, vec, pl.BlockSpec((1, 128), lambda i: (0, 0))],
        out_shape=[jax.ShapeDtypeStruct((T, D), F32), jax.ShapeDtypeStruct((T, D), BF16),
                   jax.ShapeDtypeStruct((1, D), F32), jax.ShapeDtypeStruct((1, 128), F32)],
        compiler_params=_params("arbitrary"),
    )(x3, tgt, g)


def _spread_matrix():
    head_of_lane = np.arange(N_HEADS * HEAD_DIM) // HEAD_DIM
    return jnp.asarray(np.arange(N_HEADS)[:, None] == head_of_lane[None, :], dtype=BF16)


def _pieces(v, n):
    out = []
    for _ in range(n):
        piece = v.astype(BF16)
        out.append(piece)
        v = v - piece.astype(F32)
    return out


def _spread(v, spread):
    return sum(_dot(p, spread, NN) for p in _pieces(v, 2))


def _spread_weights(w1, w2, spread):
    s1, s2 = _spread(w1, spread), _spread(w2, spread)
    return s1, s2, 1.0 - s1 - s2


def _head_sums(v, spread):
    return sum(_dot(p, spread, NT) for p in _pieces(v, 2))


def _branch_weights(l1, l2, l3):
    lm = jnp.maximum(jnp.maximum(l1, l2), l3)
    e1, e2, e3 = jnp.exp(l1 - lm), jnp.exp(l2 - lm), jnp.exp(l3 - lm)
    inv = 1.0 / (e1 + e2 + e3)
    return e1 * inv, e2 * inv, e3 * inv


def _to_token_order(view_ref, dil):
    if dil == 1:
        return view_ref[...].astype(F32)
    n_l, w = view_ref.shape[0], view_ref.shape[1] // dil
    cols = []
    for cb in range(w // LANES):
        by_class = jnp.stack([view_ref[:, r * w + cb * LANES:r * w + (cb + 1) * LANES].astype(F32) for r in range(dil)])
        cols.append(jnp.swapaxes(by_class, 0, 1).reshape(n_l * dil, LANES))
    return jnp.concatenate(cols, axis=1)


def _to_class_order(val, view_ref, dil):
    if dil == 1:
        view_ref[...] = val.astype(view_ref.dtype)
        return
    n, w = val.shape
    for cb in range(w // LANES):
        by_class = jnp.swapaxes(val[:, cb * LANES:(cb + 1) * LANES].reshape(n // dil, dil, LANES), 0, 1)
        for r in range(dil):
            view_ref[:, r * w + cb * LANES:r * w + (cb + 1) * LANES] = by_class[r].astype(view_ref.dtype)


def _view_spec(rows, width, dil):
    return pl.BlockSpec((rows // dil, dil * width), lambda i: (i, 0))


def _mix_fwd(oa, obs, lbs, ga, gb, dils):
    T, W = oa.shape

    def kern(oa_ref, o1, o2, o3, l1, l2, l3, ga_ref, gb_ref, sp_ref, mix_ref):
        sp = sp_ref[...]
        w1, w2, w3 = _branch_weights(l1[...], l2[...], l3[...])
        on = [_to_token_order(o, d) for o, d in zip((o1, o2, o3), dils)]
        s1, s2, s3 = _spread_weights(w1, w2, sp)
        ob = s1 * on[0] + s2 * on[1] + s3 * on[2]
        oav = oa_ref[...]
        mix_ref[:, :W] = ((oav * _rstd(oav)) * ga_ref[...]).astype(BF16)
        mix_ref[:, W:] = ((ob * _rstd(ob)) * gb_ref[...]).astype(BF16)

    row = pl.BlockSpec((ROWS, W), lambda i: (i, 0))
    per_head = pl.BlockSpec((ROWS, N_HEADS), lambda i: (i, 0))
    vec = pl.BlockSpec((1, W), lambda i: (0, 0))
    return pl.pallas_call(
        kern, name="mix_fwd", grid=(T // ROWS,),
        in_specs=[row] + [_view_spec(ROWS, W, d) for d in dils] + [per_head] * 3
        + [vec, vec, pl.BlockSpec((N_HEADS, W), lambda i: (0, 0))],
        out_specs=pl.BlockSpec((ROWS, 2 * W), lambda i: (i, 0)),
        out_shape=jax.ShapeDtypeStruct((T, 2 * W), BF16),
        compiler_params=_params("parallel"),
    )(oa, *obs, *lbs, ga, gb, _spread_matrix())


def _mix_bwd(dmix, oa, obs, lbs, ga, gb, dils, exchange=None):
    T, W = oa.shape
    ride = _Ride(exchange, 11, 10, 0)

    def kern(*refs):
        ins, outs, _, exrefs = ride.split(refs)
        ride.around(pl.program_id(0), T // MIX_ROWS, exrefs, lambda: compute(*ins, *outs))

    def compute(dm_ref, oa_ref, o1, o2, o3, l1, l2, l3, ga_ref, gb_ref, sp_ref,
                doa_ref, da_ref, do1, do2, do3, d1, d2, d3, dga_ref, dgb_ref):
        @pl.when(pl.program_id(0) == 0)
        def _():
            dga_ref[...] = jnp.zeros_like(dga_ref)
            dgb_ref[...] = jnp.zeros_like(dgb_ref)

        sp = sp_ref[...]
        oav = oa_ref[...]
        r = _rstd(oav)
        on = oav * r
        dy = dm_ref[:, :W]
        dga_ref[...] += jnp.sum(dy * on, axis=0, keepdims=True)
        t = dy * ga_ref[...]
        doa = r * (t - on * jnp.mean(t * on, axis=-1, keepdims=True))
        doa_ref[...] = doa.astype(BF16)
        da_ref[...] = _head_sums(doa * oav, sp)
        w1, w2, w3 = _branch_weights(l1[...], l2[...], l3[...])
        s1, s2, s3 = _spread_weights(w1, w2, sp)
        on = [_to_token_order(o, d) for o, d in zip((o1, o2, o3), dils)]
        ob = s1 * on[0] + s2 * on[1] + s3 * on[2]
        r = _rstd(ob)
        on = ob * r
        dy = dm_ref[:, W:]
        dgb_ref[...] += jnp.sum(dy * on, axis=0, keepdims=True)
        t = dy * gb_ref[...]
        dob = r * (t - on * jnp.mean(t * on, axis=-1, keepdims=True))
        c = _head_sums(dob * ob, sp)
        for do_ref, sn, d in zip((do1, do2, do3), (s1, s2, s3), dils):
            _to_class_order(sn * dob, do_ref, d)
        d1[...] = w1 * c
        d2[...] = w2 * c
        d3[...] = w3 * c

    row = pl.BlockSpec((MIX_ROWS, W), lambda i: (i, 0))
    per_head = pl.BlockSpec((MIX_ROWS, N_HEADS), lambda i: (i, 0))
    vec = pl.BlockSpec((1, W), lambda i: (0, 0))
    bf = jax.ShapeDtypeStruct((T, W), BF16)
    ph = jax.ShapeDtypeStruct((T, N_HEADS), F32)
    vv = jax.ShapeDtypeStruct((1, W), F32)
    views = [_view_spec(MIX_ROWS, W, d) for d in dils]
    return pl.pallas_call(
        kern, name="mix_bwd", grid=(T // MIX_ROWS,),
        in_specs=[pl.BlockSpec((MIX_ROWS, 2 * W), lambda i: (i, 0)), row] + views + [per_head] * 3 + [vec, vec,
                  pl.BlockSpec((N_HEADS, W), lambda i: (0, 0))] + ride.in_specs,
        out_specs=[row, per_head] + views + [per_head, per_head, per_head, vec, vec] + ride.out_specs,
        out_shape=[bf, ph] + [jax.ShapeDtypeStruct(o.shape, F32) for o in obs] + [ph, ph, ph, vv, vv]
        + ride.out_shapes,
        scratch_shapes=ride.scratch,
        input_output_aliases=ride.aliases,
        compiler_params=_params("arbitrary"),
    )(dmix, oa, *obs, *lbs, ga, gb, _spread_matrix(), *ride.args)


def _alibi_slopes(n):
    return np.asarray(2.0 ** (-8.0 * (np.arange(n) + 1) / n)).astype(np.float32)


def _band_bias(max_steps, step_dist):
    qi = np.arange(BLOCK)[None, :]
    kj = np.arange(BLOCK)[:, None]
    slopes = _alibi_slopes(N_HEADS)
    halves = []
    for steps in (qi + BLOCK - kj, qi - kj):
        valid = (steps >= 0) & (steps <= max_steps)
        alibi = slopes[:, None, None] * (step_dist * steps).astype(np.float32)[None]
        halves.append(np.where(valid[None], -alibi, np.float32(NEG_INF)).astype(np.float32))
    per_head = np.concatenate(halves, axis=1)
    return jnp.asarray(np.concatenate([per_head[0::2], per_head[1::2]], axis=2))


class _AttnLayout:
    def __init__(self, dil, kv_heads, q_stride, q_off, k_stride, k_off, v_off):
        self.dil = dil
        self.kv_heads = kv_heads
        self.kw = kv_heads * HEAD_DIM
        self.rep = N_HEADS // kv_heads
        self.q_col = lambda r: r * q_stride + q_off
        self.k_col = lambda r: r * k_stride + k_off
        self.v_col = lambda r: r * k_stride + v_off


QW = N_HEADS * HEAD_DIM
LANES = 128


PAIRS = N_HEADS // 2


def _pair_cols(pair):
    return slice(pair * LANES, (pair + 1) * LANES)


def _first_head_lanes(shape):
    return lax.broadcasted_iota(jnp.int32, shape, 1) < HEAD_DIM


def _split_heads(pair):
    first = _first_head_lanes(pair.shape)
    zero = jnp.zeros_like(pair)
    return jnp.concatenate([jnp.where(first, pair, zero), jnp.where(first, zero, pair)], axis=0)


def _kv_pair(ref, pair, rep):
    if rep == 1:
        return ref[:, _pair_cols(pair)]
    blk = ref[...].astype(F32)
    other = pltpu.roll(blk, HEAD_DIM, 1)
    first = _first_head_lanes(blk.shape)
    both = jnp.where(first, blk, other) if (2 * pair // rep) % 2 == 0 else jnp.where(first, other, blk)
    return both.astype(ref.dtype)


def _paired_kv(prev_ref, cur_ref, rep, transposed=False):
    memo = {}

    def get(pair):
        key = pair if rep == 1 else 2 * pair // rep
        if key not in memo:
            blocks = [_kv_pair(ref, pair, rep) for ref in (prev_ref, cur_ref)]
            memo[key] = jnp.concatenate([b.T for b in blocks], axis=1) if transposed else jnp.concatenate(blocks, axis=0)
        return memo[key]

    return get


def _attn_fwd(proj, bias, sinks, lay, name, exchange=None):
    L = proj.shape[0]
    nb = L // BLOCK
    kw, rep = lay.kw, lay.rep
    use_sinks = sinks is not None
    scale = HEAD_DIM ** -0.5
    ride = _Ride(exchange, 7 if use_sinks else 6, 2, 2)

    def kern(*refs):
        ins, (o_ref, l_ref), (sc_ref, pr_ref), exrefs = ride.split(refs)
        q_ref, kc_ref, kp_ref, vc_ref, vp_ref, b_ref = ins[:6]
        s_ref = ins[6] if use_sinks else None
        r, i = pl.program_id(0), pl.program_id(1)
        first = i == 0
        ride.around(r * nb + i, lay.dil * nb, exrefs,
                    lambda: compute(q_ref, kc_ref, kp_ref, vc_ref, vp_ref, b_ref, s_ref, o_ref, l_ref, first,
                                    sc_ref, pr_ref))

    def compute(q_ref, kc_ref, kp_ref, vc_ref, vp_ref, b_ref, s_ref, o_ref, l_ref, first, sc_ref, pr_ref):
        keys, values_t = _paired_kv(kp_ref, kc_ref, rep), _paired_kv(vp_ref, vc_ref, rep, transposed=True)
        for pair in range(PAIRS):
            qs = _split_heads(q_ref[:, _pair_cols(pair)])
            s_prev = _dot(keys(pair)[:BLOCK], qs, NT) * scale + b_ref[pair, :BLOCK]
            sc_ref[pair, :BLOCK] = jnp.where(first, NEG_INF, s_prev)
            sc_ref[pair, BLOCK:] = _dot(keys(pair)[BLOCK:], qs, NT) * scale + b_ref[pair, BLOCK:]
        inv = []
        for h in range(N_HEADS):
            cols = slice(h % 2 * BLOCK, (h % 2 + 1) * BLOCK)
            s = sc_ref[h // 2, :, cols]
            m = jnp.max(s, axis=0, keepdims=True)
            if use_sinks:
                sink = s_ref[:, h:h + 1]
                m = jnp.maximum(m, sink)
            p = jnp.exp(s - m)
            denom = jnp.sum(p, axis=0, keepdims=True)
            if use_sinks:
                denom = denom + jnp.exp(sink - m)
            pr_ref[h // 2, :, cols] = p.astype(BF16)
            l_ref[h:h + 1, :] = m + jnp.log(denom)
            inv.append(1.0 / denom)
        for pair in range(PAIRS):
            both = _dot(values_t(pair), pr_ref[pair], NN)
            o_t = jnp.concatenate([both[:HEAD_DIM, :BLOCK] * inv[2 * pair], both[HEAD_DIM:, BLOCK:] * inv[2 * pair + 1]],
                                  axis=0)
            o_ref[:, _pair_cols(pair)] = o_t.T

    prev = lambda i: jnp.maximum(i - 1, 0)
    in_specs = [
        pl.BlockSpec((BLOCK, QW), lambda r, i: (i, lay.q_col(r))),
        pl.BlockSpec((BLOCK, kw), lambda r, i: (i, lay.k_col(r))),
        pl.BlockSpec((BLOCK, kw), lambda r, i: (prev(i), lay.k_col(r))),
        pl.BlockSpec((BLOCK, kw), lambda r, i: (i, lay.v_col(r))),
        pl.BlockSpec((BLOCK, kw), lambda r, i: (prev(i), lay.v_col(r))),
        pl.BlockSpec((PAIRS, 2 * BLOCK, 2 * BLOCK), lambda r, i: (0, 0, 0)),
    ]
    args = [proj, proj, proj, proj, proj, bias]
    if use_sinks:
        in_specs.append(pl.BlockSpec((1, N_HEADS), lambda r, i: (0, 0)))
        args.append(sinks)
    out_specs = [pl.BlockSpec((BLOCK, QW), lambda r, i: (i, r)),
                 pl.BlockSpec((None, N_HEADS, BLOCK), lambda r, i: (r, 0, i))]
    out_shape = [jax.ShapeDtypeStruct((L, lay.dil * QW), F32), jax.ShapeDtypeStruct((lay.dil, N_HEADS, L), F32)]
    return pl.pallas_call(
        kern, name=name, grid=(lay.dil, nb),
        in_specs=in_specs + ride.in_specs, out_specs=out_specs + ride.out_specs,
        out_shape=out_shape + ride.out_shapes,
        scratch_shapes=[pltpu.VMEM((PAIRS, 2 * BLOCK, 2 * BLOCK), dt) for dt in (F32, BF16)] + ride.scratch,
        input_output_aliases=ride.aliases,
        compiler_params=_params("arbitrary", "arbitrary"),
    )(*args, *ride.args)


def _attn_bwd(proj, do, lse, dd, bias, sinks, lay, name, exchange=None):
    L = proj.shape[0]
    nb = L // BLOCK
    kw, rep = lay.kw, lay.rep
    assert rep == 1 or lay.kv_heads == 2, "grouped queries: the two kv heads fill one 128-lane block"
    use_sinks = sinks is not None
    scale = HEAD_DIM ** -0.5
    ride = _Ride(exchange, 10 if use_sinks else 9, 4 if use_sinks else 3, 6)

    def kern(*refs):
        ins, outs, (ck_ref, cv_ref, *staged), exrefs = ride.split(refs)
        q_ref, kc_ref, kp_ref, vc_ref, vp_ref, do_ref, l_ref, d_ref, b_ref = ins[:9]
        s_ref = ins[9] if use_sinks else None
        dq_ref, dk_ref, dv_ref = outs[:3]
        ds_ref = outs[3] if use_sinks else None
        r = pl.program_id(0)
        i = pl.program_id(1)
        ride.around(r * (nb + 1) + i, lay.dil * (nb + 1), exrefs,
                    lambda: compute(q_ref, kc_ref, kp_ref, vc_ref, vp_ref, do_ref, l_ref, d_ref, b_ref, s_ref,
                                    dq_ref, dk_ref, dv_ref, ds_ref, ck_ref, cv_ref, r, i, *staged))

    def compute(q_ref, kc_ref, kp_ref, vc_ref, vp_ref, do_ref, l_ref, d_ref, b_ref, s_ref,
                dq_ref, dk_ref, dv_ref, ds_ref, ck_ref, cv_ref, r, i, sc_ref, dp_ref, pr_ref, dsc_ref):
        first = i == 0

        @pl.when(first)
        def _():
            ck_ref[...] = jnp.zeros_like(ck_ref)
            cv_ref[...] = jnp.zeros_like(cv_ref)

        if use_sinks:
            @pl.when(first & (r == 0))
            def _():
                ds_ref[...] = jnp.zeros_like(ds_ref)

        @pl.when(i < nb)
        def _():
            keys, values = _paired_kv(kp_ref, kc_ref, rep), _paired_kv(vp_ref, vc_ref, rep)
            keys_t = _paired_kv(kp_ref, kc_ref, rep, transposed=True)
            for pair in range(PAIRS):
                qs = _split_heads(q_ref[:, _pair_cols(pair)])
                dos = _split_heads(do_ref[:, _pair_cols(pair)].astype(BF16))
                s = _dot(keys(pair), qs, NT) * scale + b_ref[pair]
                sc_ref[pair, :BLOCK] = jnp.where(first, NEG_INF, s[:BLOCK])
                sc_ref[pair, BLOCK:] = s[BLOCK:]
                dp_ref[pair] = _dot(values(pair), dos, NT)
            for h in range(N_HEADS):
                cols = slice(h % 2 * BLOCK, (h % 2 + 1) * BLOCK)
                lrow = l_ref[h:h + 1, :]
                drow = d_ref[h:h + 1, :]
                p = jnp.exp(sc_ref[h // 2, :, cols] - lrow)
                pr_ref[h // 2, :, cols] = p.astype(BF16)
                dsc_ref[h // 2, :, cols] = (p * (dp_ref[h // 2, :, cols] - drow) * scale).astype(BF16)
                if use_sinks:
                    ds_ref[h:h + 1, :] += -(jnp.exp(s_ref[:, h:h + 1] - lrow) * drow)
            grouped = {}
            for pair in range(PAIRS):
                cols = _pair_cols(pair)
                qs = _split_heads(q_ref[:, cols])
                dos = _split_heads(do_ref[:, cols].astype(BF16))
                ds = dsc_ref[pair]
                both = _dot(keys_t(pair), ds, NN)
                dq_t = jnp.concatenate([both[:HEAD_DIM, :BLOCK], both[HEAD_DIM:, BLOCK:]], axis=0)
                dq_ref[:, cols] = dq_t.T.astype(dq_ref.dtype)
                dk = _dot(ds, qs, NN)
                dv = _dot(pr_ref[pair], dos, NN)
                if rep == 1:
                    dk_ref[:, cols] = (ck_ref[:, cols] + dk[:BLOCK]).astype(dk_ref.dtype)
                    dv_ref[:, cols] = (cv_ref[:, cols] + dv[:BLOCK]).astype(dv_ref.dtype)
                    ck_ref[:, cols] = dk[BLOCK:]
                    cv_ref[:, cols] = dv[BLOCK:]
                else:
                    g = 2 * pair // rep
                    grouped[g] = (dk, dv) if g not in grouped else (grouped[g][0] + dk, grouped[g][1] + dv)
            if rep > 1:
                fold = lambda t: t + pltpu.roll(t, HEAD_DIM, 1)
                first_half = _first_head_lanes((2 * BLOCK, LANES))
                dk = jnp.where(first_half, fold(grouped[0][0]), fold(grouped[1][0]))
                dv = jnp.where(first_half, fold(grouped[0][1]), fold(grouped[1][1]))
                dk_ref[...] = (ck_ref[...] + dk[:BLOCK]).astype(dk_ref.dtype)
                dv_ref[...] = (cv_ref[...] + dv[:BLOCK]).astype(dv_ref.dtype)
                ck_ref[...] = dk[BLOCK:]
                cv_ref[...] = dv[BLOCK:]

        @pl.when(i == nb)
        def _():
            dk_ref[...] = ck_ref[...].astype(dk_ref.dtype)
            dv_ref[...] = cv_ref[...].astype(dv_ref.dtype)
            if use_sinks:
                @pl.when(r == lay.dil - 1)
                def _():
                    ds_ref[...] = jnp.broadcast_to(jnp.sum(ds_ref[...], axis=1, keepdims=True), ds_ref.shape)

    cur = lambda i: jnp.minimum(i, nb - 1)
    prev = lambda i: jnp.maximum(jnp.minimum(i, nb - 1) - 1, 0)
    done = lambda i: jnp.maximum(i - 1, 0)
    qspec = lambda col: pl.BlockSpec((BLOCK, QW), lambda r, i: (cur(i), col(r)))
    per_head = pl.BlockSpec((None, N_HEADS, BLOCK), lambda r, i: (r, 0, cur(i)))
    in_specs = [
        qspec(lay.q_col),
        pl.BlockSpec((BLOCK, kw), lambda r, i: (cur(i), lay.k_col(r))),
        pl.BlockSpec((BLOCK, kw), lambda r, i: (prev(i), lay.k_col(r))),
        pl.BlockSpec((BLOCK, kw), lambda r, i: (cur(i), lay.v_col(r))),
        pl.BlockSpec((BLOCK, kw), lambda r, i: (prev(i), lay.v_col(r))),
        qspec(lambda r: r), per_head, per_head,
        pl.BlockSpec((PAIRS, 2 * BLOCK, 2 * BLOCK), lambda r, i: (0, 0, 0)),
    ]
    args = [proj, proj, proj, proj, proj, do, lse, dd, bias]
    out_specs = [
        qspec(lambda r: r),
        pl.BlockSpec((BLOCK, kw), lambda r, i: (done(i), r)),
        pl.BlockSpec((BLOCK, kw), lambda r, i: (done(i), r)),
    ]
    dkv_shape = jax.ShapeDtypeStruct((L, lay.dil * kw), BF16)
    out_shape = [jax.ShapeDtypeStruct((L, lay.dil * QW), BF16), dkv_shape, dkv_shape]
    if use_sinks:
        in_specs.append(pl.BlockSpec((1, N_HEADS), lambda r, i: (0, 0)))
        args.append(sinks)
        out_specs.append(pl.BlockSpec((N_HEADS, LANES), lambda r, i: (0, 0)))
        out_shape.append(jax.ShapeDtypeStruct((N_HEADS, LANES), F32))
    return pl.pallas_call(
        kern, name=name, grid=(lay.dil, nb + 1),
        in_specs=in_specs + ride.in_specs, out_specs=out_specs + ride.out_specs,
        out_shape=out_shape + ride.out_shapes,
        scratch_shapes=[pltpu.VMEM((BLOCK, kw), F32), pltpu.VMEM((BLOCK, kw), F32)]
        + [pltpu.VMEM((PAIRS, 2 * BLOCK, 2 * BLOCK), dt) for dt in (F32, F32, BF16, BF16)] + ride.scratch,
        input_output_aliases=ride.aliases,
        compiler_params=_params("arbitrary", "arbitrary"),
    )(*args, *ride.args)


def _assemble(groups, name, dils=(1,)):
    T = groups[0][0].shape[0] * dils[0]
    widths = [g[0].shape[1] // dils[0] for g in groups]
    total = sum(widths)
    flat = [a for g in groups for a in g]
    member_dils = [d for g in groups for d in dils[:len(g)]]

    def kern(*refs):
        ins = refs[:len(flat)]
        out_ref, cs_ref = refs[len(flat):]

        @pl.when(pl.program_id(0) == 0)
        def _():
            cs_ref[...] = jnp.zeros_like(cs_ref)

        pos = off = 0
        for g, w in zip(groups, widths):
            acc = _to_token_order(ins[pos], dils[0])
            for j in range(1, len(g)):
                acc = acc + _to_token_order(ins[pos + j], dils[j])
            pos += len(g)
            out_ref[:, off:off + w] = acc.astype(BF16)
            cs_ref[:, off:off + w] += jnp.sum(acc, axis=0, keepdims=True)
            off += w

    return pl.pallas_call(
        kern, name=name, grid=(T // ROWS,),
        in_specs=[_view_spec(ROWS, a.shape[1] // d, d) for a, d in zip(flat, member_dils)],
        out_specs=[pl.BlockSpec((ROWS, total), lambda i: (i, 0)), pl.BlockSpec((1, total), lambda i: (0, 0))],
        out_shape=[jax.ShapeDtypeStruct((T, total), BF16), jax.ShapeDtypeStruct((1, total), F32)],
        compiler_params=_params("arbitrary"),
    )(*flat)


def _adamw(w, g, m, v, name):
    _, R, C = w.shape
    rows = min(R, ROWS)
    assert R % rows == 0

    def kern(w_ref, g_ref, m_ref, v_ref, d_ref, nm_ref, nv_ref):
        gv = g_ref[...]
        mn = ADAM_B1 * m_ref[...] + (1.0 - ADAM_B1) * gv
        vn = ADAM_B2 * v_ref[...] + (1.0 - ADAM_B2) * jnp.square(gv)
        m_hat = mn / (1.0 - ADAM_B1 ** ADAM_STEP)
        v_hat = vn / (1.0 - ADAM_B2 ** ADAM_STEP)
        d_ref[...] = -ADAM_LR * (m_hat / (jnp.sqrt(v_hat) + ADAM_EPS) + ADAM_WD * w_ref[...])
        nm_ref[...] = mn
        nv_ref[...] = vn

    blk = pl.BlockSpec((None, rows, C), lambda i: (0, i, 0))
    shp = jax.ShapeDtypeStruct((1, R, C), F32)
    return pl.pallas_call(
        kern, name=name, grid=(R // rows,),
        in_specs=[blk, pl.BlockSpec((rows, C), lambda i: (i, 0)), blk, blk], out_specs=[blk] * 3, out_shape=[shp] * 3,
        compiler_params=_params("parallel"),
    )(w, g, m, v)


def _sum_slots(slots, name):
    n, R, C = slots.shape
    SUM_ROWS = next(rows for rows in (128, 64, 32, 16) if R % rows == 0)

    def kern(s_ref, o_ref):
        acc = s_ref[0].astype(F32)
        for k in range(1, n):
            acc = acc + s_ref[k].astype(F32)
        o_ref[...] = acc

    return pl.pallas_call(
        kern, name=name, grid=(R // SUM_ROWS,),
        in_specs=[pl.BlockSpec((n, SUM_ROWS, C), lambda i: (0, i, 0))],
        out_specs=pl.BlockSpec((SUM_ROWS, C), lambda i: (i, 0)),
        out_shape=jax.ShapeDtypeStruct((R, C), F32),
        compiler_params=_params("parallel"),
    )(slots)


def _place():
    return lax.axis_index("x"), lax.axis_index("y"), lax.axis_index("c")


def _index(p):
    return 4 * p[0] + 2 * p[1] + p[2]


FLIPS = [(fx, fy, fc) for fx in (0, 1) for fy in (0, 1) for fc in (0, 1)][1:]


def _peer(me, flip):
    return tuple(1 - a if f else a for a, f in zip(me, flip))


def _gather_rows(shards, part=(0, 1), into=None, relay_at=RELAY_AT):
    nw = len(shards)

    def plan(ins, outs, send_sems, recv_sems):
        x, y, c = me = _place()
        sibling = (x, y, 1 - c)
        chips = [(1 - x, y), (x, 1 - y), (1 - x, 1 - y)]

        def span(w):
            cnt = ins[w].shape[0] // part[1]
            return part[0] * cnt, cnt

        def rows(w, p):
            lo, cnt = span(w)
            return outs[w].at[pl.ds(_index(p) * ins[w].shape[0] + lo, cnt), :]

        def own(w):
            lo, cnt = span(w)
            return ins[w].at[pl.ds(lo, cnt), :]

        def copy(w, k, block, to):
            return pltpu.make_async_remote_copy(
                src_ref=own(w) if block is me else rows(w, block), dst_ref=rows(w, block),
                send_sem=send_sems.at[7 * w + k], recv_sem=recv_sems.at[7 * w + k],
                device_id=to, device_id_type=MESH)

        return me, sibling, chips, c, rows, own, copy

    def copies(ins, outs, send_sems, recv_sems, local_sems):
        me, sibling, chips, c, rows, own, copy = plan(ins, outs, send_sems, recv_sems)
        local = [pltpu.make_async_copy(own(w), rows(w, me), local_sems.at[w]) for w in range(nw)]
        sends, recvs = [], []
        for w in range(nw):
            sends.append(copy(w, 0, me, sibling))
            sends += [copy(w, 1 + j, me, (*chip, c)) for j, chip in enumerate(chips)]
            recvs.append(copy(w, 0, sibling, me))
            recvs += [copy(w, 4 + j, (*chip, 1 - c), me) for j, chip in enumerate(chips)]
        return local, sends, recvs

    def relay(ins, outs, send_sems, recv_sems, local_sems):
        me, sibling, chips, c, rows, own, copy = plan(ins, outs, send_sems, recv_sems)
        arrived = [copy(w, 1 + j, (*chip, c), me) for w in range(nw) for j, chip in enumerate(chips)]
        onward = [copy(w, 4 + j, (*chip, c), sibling) for w in range(nw) for j, chip in enumerate(chips)]
        return arrived, onward

    shapes = [jax.ShapeDtypeStruct((N_DEV * s.shape[0], s.shape[1]), s.dtype) for s in shards]
    aliases = {nw + w: w for w in range(nw)} if into else None
    return _Exchange(shards + (into or []), shapes, 7 * nw, nw, copies, aliases=aliases, relay=relay,
                     relay_at=relay_at)


def _scatter_rows(parts, part=(0, 1)):
    nw = len(parts)

    def copies(ins, outs, send_sems, recv_sems, local_sems):
        me = _place()

        def src(w, owner):
            n = ins[w].shape[0] // N_DEV
            cnt = n // part[1]
            return ins[w].at[pl.ds(_index(owner) * n + part[0] * cnt, cnt), :]

        def copy(k, w, owner, sender, to):
            return pltpu.make_async_remote_copy(
                src_ref=src(w, owner), dst_ref=outs[w].at[_index(sender)],
                send_sem=send_sems.at[nw * k + w], recv_sem=recv_sems.at[nw * k + w],
                device_id=to, device_id_type=MESH)

        local = [pltpu.make_async_copy(src(w, me), outs[w].at[_index(me)], local_sems.at[w]) for w in range(nw)]
        peers = [_peer(me, flip) for flip in FLIPS]
        sends = [copy(k, w, peer, me, peer) for k, peer in enumerate(peers) for w in range(nw)]
        recvs = [copy(k, w, me, peer, me) for k, peer in enumerate(peers) for w in range(nw)]
        return local, sends, recvs

    shapes = [jax.ShapeDtypeStruct((N_DEV, p.shape[0] // N_DEV // part[1], p.shape[1]), p.dtype) for p in parts]
    return _Exchange(parts, shapes, 7 * nw, nw, copies)


def _sum_over_devices(v):
    shape = v.shape

    def body(v_ref, sum_ref, all_ref, send_sems, recv_sems):
        me = _place()
        all_ref[_index(me)] = v_ref[...]
        sends = []
        for k, flip in enumerate(FLIPS):
            peer = _peer(me, flip)
            sends.append(pltpu.make_async_remote_copy(
                src_ref=v_ref, dst_ref=all_ref.at[_index(me)],
                send_sem=send_sems.at[k], recv_sem=recv_sems.at[k], device_id=peer, device_id_type=MESH))
            sends[-1].start()
        for k, flip in enumerate(FLIPS):
            peer = _peer(me, flip)
            pltpu.make_async_remote_copy(
                src_ref=v_ref, dst_ref=all_ref.at[_index(peer)],
                send_sem=send_sems.at[k], recv_sem=recv_sems.at[k], device_id=peer, device_id_type=MESH).wait_recv()
        for cp in sends:
            cp.wait_send()
        acc = all_ref[0]
        for s in range(1, N_DEV):
            acc = acc + all_ref[s]
        sum_ref[...] = acc

    vmem = pl.BlockSpec(memory_space=pltpu.VMEM)
    return pl.pallas_call(
        body, name="sum_small_grads",
        in_specs=[vmem], out_specs=[vmem, vmem],
        out_shape=[jax.ShapeDtypeStruct(shape, F32), jax.ShapeDtypeStruct((N_DEV,) + shape, F32)],
        scratch_shapes=[pltpu.SemaphoreType.DMA((7,)), pltpu.SemaphoreType.DMA((7,))],
    )(v)[0]


SMALL_ROWS = 8


def _pack_small(vectors):
    padded = []
    for vec in vectors:
        vec = vec.reshape(-1)
        padded.append(jnp.pad(vec, (0, -vec.shape[0] % 128)))
    flat = jnp.concatenate(padded)
    flat = jnp.pad(flat, (0, -flat.shape[0] % (SMALL_ROWS * 128)))
    return flat.reshape(SMALL_ROWS, -1)


def _unpack_small(packed, shapes):
    flat = packed.reshape(-1)
    out, off = [], 0
    for shp in shapes:
        n = int(np.prod(shp))
        out.append(flat[off:off + n].reshape(shp))
        off += n + (-n % 128)
    return out


def kernel(x, g_attn, w_in, b_in, sinks_a, g_out_a, g_out_b, w_out, g_mlp, w_1, w_2, g_final, loss_target, m_g_attn, m_w_in, m_b_in, m_sinks_a, m_g_out_a, m_g_out_b, m_w_out, m_g_mlp, m_w_1, m_w_2, m_g_final, v_g_attn, v_w_in, v_b_in, v_sinks_a, v_g_out_a, v_g_out_b, v_w_out, v_g_mlp, v_w_1, v_w_2, v_g_final):
    xs, tgt = x[0], loss_target[0]
    T, D = xs.shape
    n_a = QW + 2 * KV_HEADS_A * HEAD_DIM
    g_fin = g_final.reshape(1, D)

    shards = [w_in[0].T.astype(BF16), w_out[0].astype(BF16), w_1[0].T.astype(BF16), w_2[0].astype(BF16)]
    ident = lambda acc: (acc,)
    add = lambda acc, other: (acc + other,)

    h1, w_in_t = _norm_fwd(xs, g_attn, "norm_attn", exchange=_gather_rows(shards[:1], relay_at=1.0))
    n_in = w_in_t.shape[0]
    proj_a, = _proj_views(h1, w_in_t, b_in, (0, n_a), [1], "proj_a")
    dils = [dil for _, dil in DILATED_BRANCHES]
    *proj_b, w_o = _proj_views(h1, w_in_t, b_in, (n_a, n_in - n_a), dils, "proj_b", exchange=_gather_rows(shards[1:2]))

    lay_a = _AttnLayout(1, KV_HEADS_A, 0, 0, 0, QW // (KV_HEADS_A * HEAD_DIM), QW // (KV_HEADS_A * HEAD_DIM) + 1)
    bias_a = _band_bias(WINDOW_A - 1, 1)
    o_a, l_a, w_1_t = _attn_fwd(proj_a, bias_a, sinks_a, lay_a, "attn_a_fwd",
                                exchange=_gather_rows(shards[2:3], part=(0, 4)))
    branches = []
    for n, (window, dil) in enumerate(DILATED_BRANCHES):
        lay = _AttnLayout(dil, N_HEADS, 3, 0, 3, 1, 2)
        bias = _band_bias(window // dil, dil)
        ride = _gather_rows(shards[2:3], part=(n + 1, 4), into=[w_1_t])
        o, lse, w_1_t = _attn_fwd(proj_b[n], bias, None, lay, f"attn_b{dil}_fwd", exchange=ride)
        branches.append((lay, bias, proj_b[n], o, lse))
    o_b = [br[3] for br in branches]
    l_b = [br[4].transpose(2, 0, 1).reshape(T, N_HEADS) for br in branches]

    mix = _mix_fwd(o_a, o_b, l_b, g_out_a, g_out_b, dils)
    def residual_and_norm(acc, res, g):
        x_new = acc + res
        return x_new, (x_new * _rstd(x_new)) * g

    assert TILE_WHOLE_ROWS["tn"] == D
    x2, h2 = _matmul(mix, w_o, "nn", [F32, BF16], residual_and_norm, tk=D, tile_ins=[xs], row_ins=[g_mlp],
                     name="out_proj", **TILE_WHOLE_ROWS)

    def relu_sq(acc):
        u = jnp.maximum(acc, 0.0)
        return u, u * u

    u, u_sq, w_2_f = _matmul(h2, w_1_t, "nt", [BF16, BF16], relu_sq, tk=D, name="mlp_up",
                             exchange=_gather_rows(shards[3:]), **TILE_WHOLE_ROWS)
    x3, = _matmul(u_sq, w_2_f, "nn", [F32], add, tk=4096, tile_ins=[x2], name="mlp_down", **TILE)

    dx3, dx3_b, dg_final, loss_dev = _loss_head(x3, tgt, g_fin)

    d_pre, = _matmul(dx3_b, w_2_f, "nt", [BF16], lambda acc, uu: (acc * (2.0 * uu.astype(F32)),),
                     tk=D, tile_ins=[u], name="mlp_down_bwd", **TILE_WHOLE_ROWS)
    dw_2, = _matmul(u_sq, dx3_b, "tn", [BF16], ident, name="mlp_down_wgrad", **TILE_WGRAD)
    dh2, slots_2a = _matmul(d_pre, w_1_t, "nn", [BF16], ident, tk=4096, name="mlp_up_bwd",
                            exchange=_scatter_rows([dw_2], part=(0, 2)), **TILE)
    dw_1_t, slots_2b = _matmul(d_pre, h2, "tn", [BF16], ident, name="mlp_up_wgrad",
                               exchange=_scatter_rows([dw_2], part=(1, 2)), **TILE_WGRAD)
    dx2, dg_mlp, dx2_b, dmix = _norm_bwd(dh2, x2, g_mlp, dx3, "norm_mlp_bwd", then_w_t=w_o)
    dw_o, = _matmul(mix, dx2_b, "tn", [BF16], ident, name="out_proj_wgrad", **TILE_WGRAD)
    do_a, dd_a, do1, do2, do3, dd1, dd2, dd3, dg_out_a, dg_out_b, slots_o = _mix_bwd(
        dmix, o_a, o_b, l_b, g_out_a, g_out_b, dils, exchange=_scatter_rows([dw_o]))

    by_class = lambda d, dil: d.reshape(T // dil, dil, N_HEADS).transpose(1, 2, 0)
    slots_1 = [None] * 4
    dq_a, dk_a, dv_a, dsinks, slots_1[0] = _attn_bwd(proj_a, do_a, l_a, by_class(dd_a, 1), bias_a, sinks_a, lay_a,
                                                     "attn_a_bwd", exchange=_scatter_rows([dw_1_t], part=(0, 4)))
    dsinks = dsinks[:, 0].reshape(1, N_HEADS)
    dqs, dks, dvs = [], [], []
    for n, ((lay, bias, view, _, lse), do_n, dd_n) in enumerate(zip(branches, (do1, do2, do3), (dd1, dd2, dd3))):
        dq, dk, dv, slots_1[n + 1] = _attn_bwd(view, do_n, lse, by_class(dd_n, lay.dil), bias, None, lay,
                                               f"attn_b{lay.dil}_bwd",
                                               exchange=_scatter_rows([dw_1_t], part=(n + 1, 4)))
        dqs.append(dq)
        dks.append(dk)
        dvs.append(dv)
    dproj, db_in = _assemble([[dq_a], [dk_a], [dv_a], dqs, dks, dvs], "dproj", dils)

    dw_in_t, = _matmul(dproj, h1, "tn", [BF16], ident, tm=n_in // 2, tn=1024, tk=1024, name="in_proj_wgrad")
    dh1, slots_in = _matmul(dproj, w_in_t, "nn", [BF16], ident, tk=n_in, name="in_proj_bwd",
                            exchange=_scatter_rows([dw_in_t]), **TILE)
    dx, dg_attn = _norm_bwd(dh1, xs, g_attn, dx2, "norm_attn_bwd")

    g_w_in = _sum_slots(slots_in, "sum_w_in_grads").T
    g_w_out = _sum_slots(slots_o, "sum_w_out_grads")
    g_w_1 = jnp.concatenate([_sum_slots(s, f"sum_w_1_grads_{n}") for n, s in enumerate(slots_1)]).T
    g_w_2 = jnp.concatenate([_sum_slots(slots_2a, "sum_w_2_grads_0"), _sum_slots(slots_2b, "sum_w_2_grads_1")])

    small_w = [g_attn, b_in, sinks_a, g_out_a, g_out_b, g_mlp, g_final]
    small_m = [m_g_attn, m_b_in, m_sinks_a, m_g_out_a, m_g_out_b, m_g_mlp, m_g_final]
    small_v = [v_g_attn, v_b_in, v_sinks_a, v_g_out_a, v_g_out_b, v_g_mlp, v_g_final]
    small_g = [dg_attn, db_in, dsinks, dg_out_a, dg_out_b, dg_mlp, dg_final]
    summed = _sum_over_devices(_pack_small(small_g + [loss_dev[:, :1]]))
    shapes = [w.shape for w in small_w]
    *g_small, loss = _unpack_small(summed, shapes + [()])

    big = [
        _adamw(w_in, g_w_in, m_w_in, v_w_in, "adamw_w_in"),
        _adamw(w_out, g_w_out, m_w_out, v_w_out, "adamw_w_out"),
        _adamw(w_1, g_w_1, m_w_1, v_w_1, "adamw_w_1"),
        _adamw(w_2, g_w_2, m_w_2, v_w_2, "adamw_w_2"),
    ]
    g_packed = _pack_small(g_small)
    small = _adamw(_pack_small(small_w)[None], g_packed, _pack_small(small_m)[None], _pack_small(small_v)[None],
                   "adamw_small")
    small = [_unpack_small(s, shapes) for s in small]

    def ordered(small_list, big_list):
        s = list(small_list)
        return [s[0], big_list[0], s[1], s[2], s[3], s[4], big_list[1], s[5], big_list[2], big_list[3], s[6]]

    grads = ordered(g_small, [g[None] for g in (g_w_in, g_w_out, g_w_1, g_w_2)])
    deltas = ordered(small[0], [b[0] for b in big])
    new_m = ordered(small[1], [b[1] for b in big])
    new_v = ordered(small[2], [b[2] for b in big])
    return (loss, dx[None], *grads, *deltas, *new_m, *new_v)
```

```python
import numpy as np
import jax
import jax.numpy as jnp
from jax import lax
from jax.experimental import pallas as pl
from jax.experimental.pallas import tpu as pltpu

F32 = jnp.float32
BF16 = jnp.bfloat16

HEAD_DIM = 64
N_HEADS = 16
KV_HEADS_A = 2
BLOCK = 128
WINDOW_A = 128
DILATED_BRANCHES = ((128, 1), (512, 4), (2048, 16))
EPS = 1e-5
NEG_INF = -1e30
N_DEV = 8

ADAM_LR = 0.001
ADAM_B1 = 0.9
ADAM_B2 = 0.999
ADAM_EPS = 1e-08
ADAM_WD = 0.01
ADAM_STEP = 10

VMEM_LIMIT_BYTES = 56 * 1024 * 1024
MESH = pl.DeviceIdType.MESH
ANY = pl.BlockSpec(memory_space=pl.ANY)

NN = (((1,), (0,)), ((), ()))
NT = (((1,), (1,)), ((), ()))
TN = (((0,), (0,)), ((), ()))


def _dot(a, b, dims):
    return lax.dot_general(a, b, dims, preferred_element_type=F32)


def _params(*sem):
    return pltpu.CompilerParams(dimension_semantics=sem, vmem_limit_bytes=VMEM_LIMIT_BYTES)


RELAY_AT = 0.6


class _Exchange:
    def __init__(self, ins, out_shapes, n_remote, n_local, copies, aliases=None, relay=None, relay_at=RELAY_AT):
        self.ins, self.out_shapes = list(ins), list(out_shapes)
        self.n_remote, self.n_local = n_remote, n_local
        self.copies = copies
        self.relay = relay
        self.relay_at = relay_at
        self.aliases = aliases or {}

    def start(self, refs):
        local, sends, _ = self.copies(*refs)
        for cp in local + sends:
            cp.start()

    def middle(self, refs):
        arrived, onward = self.relay(*refs)
        for got, cp in zip(arrived, onward):
            got.wait_recv()
            cp.start()

    def finish(self, refs):
        local, sends, recvs = self.copies(*refs)
        for cp in recvs:
            cp.wait_recv()
        for cp in sends:
            cp.wait_send()
        for cp in local:
            cp.wait()
        if self.relay:
            for cp in self.relay(*refs)[1]:
                cp.wait_send()


class _Ride:
    def __init__(self, ex, n_in, n_out, n_scratch):
        self.ex = ex
        self.n = (n_in, n_out, n_scratch)
        self.args = ex.ins if ex else []
        self.in_specs = [ANY] * len(self.args)
        self.out_shapes = ex.out_shapes if ex else []
        self.out_specs = [ANY] * len(self.out_shapes)
        self.scratch = [pltpu.SemaphoreType.DMA((ex.n_remote,)), pltpu.SemaphoreType.DMA((ex.n_remote,)),
                        pltpu.SemaphoreType.DMA((max(ex.n_local, 1),))] if ex else []
        self.aliases = {n_in + i: n_out + o for i, o in ex.aliases.items()} if ex else {}

    def split(self, refs):
        n_in, n_out, n_scratch = self.n
        a = n_in
        b = a + len(self.args)
        c = b + n_out
        d = c + len(self.out_shapes)
        e = d + n_scratch
        return refs[:a], refs[b:c], refs[d:e], (refs[a:b], refs[c:d], *refs[e:])

    def around(self, step, n_steps, exrefs, compute):
        if self.ex is None:
            compute()
            return

        @pl.when(step == 0)
        def _():
            self.ex.start(exrefs)

        compute()

        if self.ex.relay:
            @pl.when(step == int(self.ex.relay_at * (n_steps - 1)))
            def _():
                self.ex.middle(exrefs)

        @pl.when(step == n_steps - 1)
        def _():
            self.ex.finish(exrefs)


TILE = dict(tm=512, tn=1024)
TILE_WHOLE_ROWS = dict(tm=512, tn=2048)
TILE_WGRAD = dict(tm=1024, tn=1024, tk=4096)

def _matmul(a, b, dims, out_dtypes, epilogue, *, tm, tn, tk, name, tile_ins=(), row_ins=(), exchange=None):
    if dims == "tn":
        K, M = a.shape
    else:
        M, K = a.shape
    N = b.shape[0] if dims == "nt" else b.shape[1]
    tm, tn, tk = min(tm, M), min(tn, N), min(tk, K)
    assert M % tm == 0 and N % tn == 0 and K % tk == 0, (name, M, N, K, tm, tn, tk)
    grid = (M // tm, N // tn, K // tk)
    nk = grid[2]
    n_tile, n_row, n_out = len(tile_ins), len(row_ins), len(out_dtypes)
    dn = {"nn": NN, "nt": NT, "tn": TN}[dims]
    ride = _Ride(exchange, 2 + n_tile + n_row, n_out, 1 if nk > 1 else 0)

    def kern(*refs):
        ins, out_refs, scratch, exrefs = ride.split(refs)
        a_ref, b_ref = ins[:2]
        tile_refs = ins[2:2 + n_tile]
        row_refs = ins[2 + n_tile:]
        ids = [pl.program_id(d) for d in range(3)]

        def finish(acc):
            outs = epilogue(acc, *[r[...] for r in tile_refs], *[r[...] for r in row_refs])
            for o_ref, o in zip(out_refs, outs):
                o_ref[...] = o.astype(o_ref.dtype)

        def compute():
            if nk == 1:
                finish(_dot(a_ref[...], b_ref[...], dn))
                return
            acc_ref = scratch[0]

            @pl.when(ids[2] == 0)
            def _():
                acc_ref[...] = jnp.zeros_like(acc_ref)

            acc_ref[...] += _dot(a_ref[...], b_ref[...], dn)

            @pl.when(ids[2] == nk - 1)
            def _():
                finish(acc_ref[...])

        ride.around((ids[0] * grid[1] + ids[1]) * grid[2] + ids[2], grid[0] * grid[1] * grid[2], exrefs, compute)

    if dims == "tn":
        a_spec = pl.BlockSpec((tk, tm), lambda i, j, k: (k, i))
    else:
        a_spec = pl.BlockSpec((tm, tk), lambda i, j, k: (i, k))
    if dims == "nt":
        b_spec = pl.BlockSpec((tn, tk), lambda i, j, k: (j, k))
    else:
        b_spec = pl.BlockSpec((tk, tn), lambda i, j, k: (k, j))
    tile_spec = pl.BlockSpec((tm, tn), lambda i, j, k: (i, j))
    row_spec = pl.BlockSpec((1, tn), lambda i, j, k: (0, j))
    sem = ("arbitrary",) * 3 if exchange else ("parallel", "parallel", "arbitrary")
    return pl.pallas_call(
        kern,
        name=name,
        grid=grid,
        in_specs=[a_spec, b_spec] + [tile_spec] * n_tile + [row_spec] * n_row + ride.in_specs,
        out_specs=[tile_spec] * n_out + ride.out_specs,
        out_shape=[jax.ShapeDtypeStruct((M, N), dt) for dt in out_dtypes] + ride.out_shapes,
        scratch_shapes=([pltpu.VMEM((tm, tn), F32)] if nk > 1 else []) + ride.scratch,
        input_output_aliases=ride.aliases,
        compiler_params=_params(*sem),
    )(a, b, *tile_ins, *row_ins, *ride.args)


PROJ_ROWS = 256


def _proj_views(a, w_t, bias, cols, dils, name, exchange=None):
    T, K = a.shape
    first, N = cols
    ride = _Ride(exchange, 3, len(dils), 0)

    def kern(*refs):
        (a_ref, w_ref, b_ref), outs, _, exrefs = ride.split(refs)

        def compute():
            acc = _dot(a_ref[...], w_ref[...], NT) + b_ref[...]
            for out_ref, dil in zip(outs, dils):
                _to_class_order(acc, out_ref, dil)

        ride.around(pl.program_id(0), T // PROJ_ROWS, exrefs, compute)

    return pl.pallas_call(
        kern, name=name, grid=(T // PROJ_ROWS,),
        in_specs=[pl.BlockSpec((PROJ_ROWS, K), lambda i: (i, 0)),
                  pl.BlockSpec((pl.Element(N), pl.Element(K)), lambda i: (first, 0)),
                  pl.BlockSpec((pl.Element(1), pl.Element(N)), lambda i: (0, first))] + ride.in_specs,
        out_specs=[_view_spec(PROJ_ROWS, N, d) for d in dils] + ride.out_specs,
        out_shape=[jax.ShapeDtypeStruct((T // d, d * N), BF16) for d in dils] + ride.out_shapes,
        scratch_shapes=ride.scratch,
        input_output_aliases=ride.aliases,
        compiler_params=_params("arbitrary"),
    )(a, w_t, bias, *ride.args)


ROWS = 256
LEAN_ROWS = 512
MIX_ROWS = 256


def _rstd(xv):
    return lax.rsqrt(jnp.mean(xv * xv, axis=-1, keepdims=True) + EPS)


def _norm_fwd(x, g, name, exchange=None):
    T, D = x.shape
    ride = _Ride(exchange, 2, 1, 0)

    def kern(*refs):
        (x_ref, g_ref), (h_ref,), _, exrefs = ride.split(refs)

        def compute():
            xv = x_ref[...]
            h_ref[...] = ((xv * _rstd(xv)) * g_ref[...]).astype(h_ref.dtype)

        ride.around(pl.program_id(0), T // ROWS, exrefs, compute)

    row = pl.BlockSpec((ROWS, D), lambda i: (i, 0))
    return pl.pallas_call(
        kern, name=name, grid=(T // ROWS,),
        in_specs=[row, pl.BlockSpec((1, D), lambda i: (0, 0))] + ride.in_specs,
        out_specs=[row] + ride.out_specs,
        out_shape=[jax.ShapeDtypeStruct((T, D), BF16)] + ride.out_shapes,
        scratch_shapes=ride.scratch, input_output_aliases=ride.aliases,
        compiler_params=_params("arbitrary"),
    )(x, g, *ride.args)


def _norm_bwd(dh, x, g, res, name, then_w_t=None):
    T, D = x.shape
    rows = PROJ_ROWS if then_w_t is not None else min(LEAN_ROWS, T)

    def kern(dh_ref, x_ref, g_ref, res_ref, *rest):
        if then_w_t is None:
            dx_ref, dg_ref = rest
        else:
            w_ref, dx_ref, dg_ref, dxb_ref, y_ref = rest

        @pl.when(pl.program_id(0) == 0)
        def _():
            dg_ref[...] = jnp.zeros_like(dg_ref)

        xv = x_ref[...]
        r = _rstd(xv)
        xn = xv * r
        dhv = dh_ref[...].astype(F32)
        dg_ref[...] += jnp.sum(dhv * xn, axis=0, keepdims=True)
        t = dhv * g_ref[...]
        dx = res_ref[...] + r * (t - xn * jnp.mean(t * xn, axis=-1, keepdims=True))
        dx_ref[...] = dx
        if then_w_t is not None:
            dxb = dx.astype(BF16)
            dxb_ref[...] = dxb
            y_ref[...] = _dot(dxb, w_ref[...], NT)

    row = pl.BlockSpec((rows, D), lambda i: (i, 0))
    vec = pl.BlockSpec((1, D), lambda i: (0, 0))
    in_specs, args = [row, row, vec, row], [dh, x, g, res]
    out_specs = [row, vec]
    out_shape = [jax.ShapeDtypeStruct((T, D), F32), jax.ShapeDtypeStruct((1, D), F32)]
    if then_w_t is not None:
        N = then_w_t.shape[0]
        in_specs.append(pl.BlockSpec((N, D), lambda i: (0, 0)))
        args.append(then_w_t)
        out_specs += [row, pl.BlockSpec((rows, N), lambda i: (i, 0))]
        out_shape += [jax.ShapeDtypeStruct((T, D), BF16), jax.ShapeDtypeStruct((T, N), F32)]
    return pl.pallas_call(
        kern, name=name, grid=(T // rows,), in_specs=in_specs, out_specs=out_specs, out_shape=out_shape,
        compiler_params=_params("arbitrary"),
    )(*args)


def _loss_head(x3, tgt, g):
    T, D = x3.shape

    def kern(x_ref, t_ref, g_ref, dx_ref, dxb_ref, dg_ref, loss_ref):
        @pl.when(pl.program_id(0) == 0)
        def _():
            dg_ref[...] = jnp.zeros_like(dg_ref)
            loss_ref[...] = jnp.zeros_like(loss_ref)

        xv = x_ref[...]
        gv = g_ref[...]
        r = _rstd(xv)
        xn = xv * r
        err = xn * gv - t_ref[...]
        per_tok = jnp.mean(err * err, axis=-1, keepdims=True)
        loss_ref[...] += 0.5 * jnp.sum(per_tok, axis=0, keepdims=True)
        dy = err * (1.0 / D)
        dg_ref[...] += jnp.sum(dy * xn, axis=0, keepdims=True)
        t = dy * gv
        dx = r * (t - xn * jnp.mean(t * xn, axis=-1, keepdims=True))
        dx_ref[...] = dx
        dxb_ref[...] = dx.astype(BF16)

    row = pl.BlockSpec((LEAN_ROWS, D), lambda i: (i, 0))
    vec = pl.BlockSpec((1, D), lambda i: (0, 0))
    return pl.pallas_call(
        kern, name="loss_head", grid=(T // LEAN_ROWS,),
        in_specs=[row, row, vec],
        out_specs=[row, row, vec, pl.BlockSpec((1, 128), lambda i: (0, 0))],
        out_shape=[jax.ShapeDtypeStruct((T, D), F32), jax.ShapeDtypeStruct((T, D), BF16),
                   jax.ShapeDtypeStruct((1, D), F32), jax.ShapeDtypeStruct((1, 128), F32)],
        compiler_params=_params("arbitrary"),
    )(x3, tgt, g)


def _spread_matrix():
    head_of_lane = np.arange(N_HEADS * HEAD_DIM) // HEAD_DIM
    return jnp.asarray(np.arange(N_HEADS)[:, None] == head_of_lane[None, :], dtype=BF16)


def _pieces(v, n):
    out = []
    for _ in range(n):
        piece = v.astype(BF16)
        out.append(piece)
        v = v - piece.astype(F32)
    return out


def _spread(v, spread):
    return sum(_dot(p, spread, NN) for p in _pieces(v, 2))


def _spread_weights(w1, w2, spread):
    s1, s2 = _spread(w1, spread), _spread(w2, spread)
    return s1, s2, 1.0 - s1 - s2


def _head_sums(v, spread):
    return sum(_dot(p, spread, NT) for p in _pieces(v, 2))


def _branch_weights(l1, l2, l3):
    lm = jnp.maximum(jnp.maximum(l1, l2), l3)
    e1, e2, e3 = jnp.exp(l1 - lm), jnp.exp(l2 - lm), jnp.exp(l3 - lm)
    inv = 1.0 / (e1 + e2 + e3)
    return e1 * inv, e2 * inv, e3 * inv


def _to_token_order(view_ref, dil):
    if dil == 1:
        return view_ref[...].astype(F32)
    n_l, w = view_ref.shape[0], view_ref.shape[1] // dil
    cols = []
    for cb in range(w // LANES):
        by_class = jnp.stack([view_ref[:, r * w + cb * LANES:r * w + (cb + 1) * LANES].astype(F32) for r in range(dil)])
        cols.append(jnp.swapaxes(by_class, 0, 1).reshape(n_l * dil, LANES))
    return jnp.concatenate(cols, axis=1)


def _to_class_order(val, view_ref, dil):
    if dil == 1:
        view_ref[...] = val.astype(view_ref.dtype)
        return
    n, w = val.shape
    for cb in range(w // LANES):
        by_class = jnp.swapaxes(val[:, cb * LANES:(cb + 1) * LANES].reshape(n // dil, dil, LANES), 0, 1)
        for r in range(dil):
            view_ref[:, r * w + cb * LANES:r * w + (cb + 1) * LANES] = by_class[r].astype(view_ref.dtype)


def _view_spec(rows, width, dil):
    return pl.BlockSpec((rows // dil, dil * width), lambda i: (i, 0))


def _mix_fwd(oa, obs, lbs, ga, gb, dils):
    T, W = oa.shape

    def kern(oa_ref, o1, o2, o3, l1, l2, l3, ga_ref, gb_ref, sp_ref, mix_ref):
        sp = sp_ref[...]
        w1, w2, w3 = _branch_weights(l1[...], l2[...], l3[...])
        on = [_to_token_order(o, d) for o, d in zip((o1, o2, o3), dils)]
        s1, s2, s3 = _spread_weights(w1, w2, sp)
        ob = s1 * on[0] + s2 * on[1] + s3 * on[2]
        oav = oa_ref[...]
        mix_ref[:, :W] = ((oav * _rstd(oav)) * ga_ref[...]).astype(BF16)
        mix_ref[:, W:] = ((ob * _rstd(ob)) * gb_ref[...]).astype(BF16)

    row = pl.BlockSpec((ROWS, W), lambda i: (i, 0))
    per_head = pl.BlockSpec((ROWS, N_HEADS), lambda i: (i, 0))
    vec = pl.BlockSpec((1, W), lambda i: (0, 0))
    return pl.pallas_call(
        kern, name="mix_fwd", grid=(T // ROWS,),
        in_specs=[row] + [_view_spec(ROWS, W, d) for d in dils] + [per_head] * 3
        + [vec, vec, pl.BlockSpec((N_HEADS, W), lambda i: (0, 0))],
        out_specs=pl.BlockSpec((ROWS, 2 * W), lambda i: (i, 0)),
        out_shape=jax.ShapeDtypeStruct((T, 2 * W), BF16),
        compiler_params=_params("parallel"),
    )(oa, *obs, *lbs, ga, gb, _spread_matrix())


def _mix_bwd(dmix, oa, obs, lbs, ga, gb, dils, exchange=None):
    T, W = oa.shape
    ride = _Ride(exchange, 11, 10, 0)

    def kern(*refs):
        ins, outs, _, exrefs = ride.split(refs)
        ride.around(pl.program_id(0), T // MIX_ROWS, exrefs, lambda: compute(*ins, *outs))

    def compute(dm_ref, oa_ref, o1, o2, o3, l1, l2, l3, ga_ref, gb_ref, sp_ref,
                doa_ref, da_ref, do1, do2, do3, d1, d2, d3, dga_ref, dgb_ref):
        @pl.when(pl.program_id(0) == 0)
        def _():
            dga_ref[...] = jnp.zeros_like(dga_ref)
            dgb_ref[...] = jnp.zeros_like(dgb_ref)

        sp = sp_ref[...]
        oav = oa_ref[...]
        r = _rstd(oav)
        on = oav * r
        dy = dm_ref[:, :W]
        dga_ref[...] += jnp.sum(dy * on, axis=0, keepdims=True)
        t = dy * ga_ref[...]
        doa = r * (t - on * jnp.mean(t * on, axis=-1, keepdims=True))
        doa_ref[...] = doa.astype(BF16)
        da_ref[...] = _head_sums(doa * oav, sp)
        w1, w2, w3 = _branch_weights(l1[...], l2[...], l3[...])
        s1, s2, s3 = _spread_weights(w1, w2, sp)
        on = [_to_token_order(o, d) for o, d in zip((o1, o2, o3), dils)]
        ob = s1 * on[0] + s2 * on[1] + s3 * on[2]
        r = _rstd(ob)
        on = ob * r
        dy = dm_ref[:, W:]
        dgb_ref[...] += jnp.sum(dy * on, axis=0, keepdims=True)
        t = dy * gb_ref[...]
        dob = r * (t - on * jnp.mean(t * on, axis=-1, keepdims=True))
        c = _head_sums(dob * ob, sp)
        for do_ref, sn, d in zip((do1, do2, do3), (s1, s2, s3), dils):
            _to_class_order(sn * dob, do_ref, d)
        d1[...] = w1 * c
        d2[...] = w2 * c
        d3[...] = w3 * c

    row = pl.BlockSpec((MIX_ROWS, W), lambda i: (i, 0))
    per_head = pl.BlockSpec((MIX_ROWS, N_HEADS), lambda i: (i, 0))
    vec = pl.BlockSpec((1, W), lambda i: (0, 0))
    bf = jax.ShapeDtypeStruct((T, W), BF16)
    ph = jax.ShapeDtypeStruct((T, N_HEADS), F32)
    vv = jax.ShapeDtypeStruct((1, W), F32)
    views = [_view_spec(MIX_ROWS, W, d) for d in dils]
    return pl.pallas_call(
        kern, name="mix_bwd", grid=(T // MIX_ROWS,),
        in_specs=[pl.BlockSpec((MIX_ROWS, 2 * W), lambda i: (i, 0)), row] + views + [per_head] * 3 + [vec, vec,
                  pl.BlockSpec((N_HEADS, W), lambda i: (0, 0))] + ride.in_specs,
        out_specs=[row, per_head] + views + [per_head, per_head, per_head, vec, vec] + ride.out_specs,
        out_shape=[bf, ph] + [jax.ShapeDtypeStruct(o.shape, F32) for o in obs] + [ph, ph, ph, vv, vv]
        + ride.out_shapes,
        scratch_shapes=ride.scratch,
        input_output_aliases=ride.aliases,
        compiler_params=_params("arbitrary"),
    )(dmix, oa, *obs, *lbs, ga, gb, _spread_matrix(), *ride.args)


def _alibi_slopes(n):
    return np.asarray(2.0 ** (-8.0 * (np.arange(n) + 1) / n)).astype(np.float32)


def _band_bias(max_steps, step_dist):
    qi = np.arange(BLOCK)[None, :]
    kj = np.arange(BLOCK)[:, None]
    slopes = _alibi_slopes(N_HEADS)
    halves = []
    for steps in (qi + BLOCK - kj, qi - kj):
        valid = (steps >= 0) & (steps <= max_steps)
        alibi = slopes[:, None, None] * (step_dist * steps).astype(np.float32)[None]
        halves.append(np.where(valid[None], -alibi, np.float32(NEG_INF)).astype(np.float32))
    per_head = np.concatenate(halves, axis=1)
    return jnp.asarray(np.concatenate([per_head[0::2], per_head[1::2]], axis=2))


class _AttnLayout:
    def __init__(self, dil, kv_heads, q_stride, q_off, k_stride, k_off, v_off):
        self.dil = dil
        self.kv_heads = kv_heads
        self.kw = kv_heads * HEAD_DIM
        self.rep = N_HEADS // kv_heads
        self.q_col = lambda r: r * q_stride + q_off
        self.k_col = lambda r: r * k_stride + k_off
        self.v_col = lambda r: r * k_stride + v_off


QW = N_HEADS * HEAD_DIM
LANES = 128


PAIRS = N_HEADS // 2


def _pair_cols(pair):
    return slice(pair * LANES, (pair + 1) * LANES)


def _first_head_lanes(shape):
    return lax.broadcasted_iota(jnp.int32, shape, 1) < HEAD_DIM


def _split_heads(pair):
    first = _first_head_lanes(pair.shape)
    zero = jnp.zeros_like(pair)
    return jnp.concatenate([jnp.where(first, pair, zero), jnp.where(first, zero, pair)], axis=0)


def _kv_pair(ref, pair, rep):
    if rep == 1:
        return ref[:, _pair_cols(pair)]
    blk = ref[...].astype(F32)
    other = pltpu.roll(blk, HEAD_DIM, 1)
    first = _first_head_lanes(blk.shape)
    both = jnp.where(first, blk, other) if (2 * pair // rep) % 2 == 0 else jnp.where(first, other, blk)
    return both.astype(ref.dtype)


def _paired_kv(prev_ref, cur_ref, rep, transposed=False):
    memo = {}

    def get(pair):
        key = pair if rep == 1 else 2 * pair // rep
        if key not in memo:
            blocks = [_kv_pair(ref, pair, rep) for ref in (prev_ref, cur_ref)]
            memo[key] = jnp.concatenate([b.T for b in blocks], axis=1) if transposed else jnp.concatenate(blocks, axis=0)
        return memo[key]

    return get


def _attn_fwd(proj, bias, sinks, lay, name, exchange=None):
    L = proj.shape[0]
    nb = L // BLOCK
    kw, rep = lay.kw, lay.rep
    use_sinks = sinks is not None
    scale = HEAD_DIM ** -0.5
    ride = _Ride(exchange, 7 if use_sinks else 6, 2, 2)

    def kern(*refs):
        ins, (o_ref, l_ref), (sc_ref, pr_ref), exrefs = ride.split(refs)
        q_ref, kc_ref, kp_ref, vc_ref, vp_ref, b_ref = ins[:6]
        s_ref = ins[6] if use_sinks else None
        r, i = pl.program_id(0), pl.program_id(1)
        first = i == 0
        ride.around(r * nb + i, lay.dil * nb, exrefs,
                    lambda: compute(q_ref, kc_ref, kp_ref, vc_ref, vp_ref, b_ref, s_ref, o_ref, l_ref, first,
                                    sc_ref, pr_ref))

    def compute(q_ref, kc_ref, kp_ref, vc_ref, vp_ref, b_ref, s_ref, o_ref, l_ref, first, sc_ref, pr_ref):
        keys, values_t = _paired_kv(kp_ref, kc_ref, rep), _paired_kv(vp_ref, vc_ref, rep, transposed=True)
        for pair in range(PAIRS):
            qs = _split_heads(q_ref[:, _pair_cols(pair)])
            s_prev = _dot(keys(pair)[:BLOCK], qs, NT) * scale + b_ref[pair, :BLOCK]
            sc_ref[pair, :BLOCK] = jnp.where(first, NEG_INF, s_prev)
            sc_ref[pair, BLOCK:] = _dot(keys(pair)[BLOCK:], qs, NT) * scale + b_ref[pair, BLOCK:]
        inv = []
        for h in range(N_HEADS):
            cols = slice(h % 2 * BLOCK, (h % 2 + 1) * BLOCK)
            s = sc_ref[h // 2, :, cols]
            m = jnp.max(s, axis=0, keepdims=True)
            if use_sinks:
                sink = s_ref[:, h:h + 1]
                m = jnp.maximum(m, sink)
            p = jnp.exp(s - m)
            denom = jnp.sum(p, axis=0, keepdims=True)
            if use_sinks:
                denom = denom + jnp.exp(sink - m)
            pr_ref[h // 2, :, cols] = p.astype(BF16)
            l_ref[h:h + 1, :] = m + jnp.log(denom)
            inv.append(1.0 / denom)
        for pair in range(PAIRS):
            both = _dot(values_t(pair), pr_ref[pair], NN)
            o_t = jnp.concatenate([both[:HEAD_DIM, :BLOCK] * inv[2 * pair], both[HEAD_DIM:, BLOCK:] * inv[2 * pair + 1]],
                                  axis=0)
            o_ref[:, _pair_cols(pair)] = o_t.T

    prev = lambda i: jnp.maximum(i - 1, 0)
    in_specs = [
        pl.BlockSpec((BLOCK, QW), lambda r, i: (i, lay.q_col(r))),
        pl.BlockSpec((BLOCK, kw), lambda r, i: (i, lay.k_col(r))),
        pl.BlockSpec((BLOCK, kw), lambda r, i: (prev(i), lay.k_col(r))),
        pl.BlockSpec((BLOCK, kw), lambda r, i: (i, lay.v_col(r))),
        pl.BlockSpec((BLOCK, kw), lambda r, i: (prev(i), lay.v_col(r))),
        pl.BlockSpec((PAIRS, 2 * BLOCK, 2 * BLOCK), lambda r, i: (0, 0, 0)),
    ]
    args = [proj, proj, proj, proj, proj, bias]
    if use_sinks:
        in_specs.append(pl.BlockSpec((1, N_HEADS), lambda r, i: (0, 0)))
        args.append(sinks)
    out_specs = [pl.BlockSpec((BLOCK, QW), lambda r, i: (i, r)),
                 pl.BlockSpec((None, N_HEADS, BLOCK), lambda r, i: (r, 0, i))]
    out_shape = [jax.ShapeDtypeStruct((L, lay.dil * QW), F32), jax.ShapeDtypeStruct((lay.dil, N_HEADS, L), F32)]
    return pl.pallas_call(
        kern, name=name, grid=(lay.dil, nb),
        in_specs=in_specs + ride.in_specs, out_specs=out_specs + ride.out_specs,
        out_shape=out_shape + ride.out_shapes,
        scratch_shapes=[pltpu.VMEM((PAIRS, 2 * BLOCK, 2 * BLOCK), dt) for dt in (F32, BF16)] + ride.scratch,
        input_output_aliases=ride.aliases,
        compiler_params=_params("arbitrary", "arbitrary"),
    )(*args, *ride.args)


def _attn_bwd(proj, do, lse, dd, bias, sinks, lay, name, exchange=None):
    L = proj.shape[0]
    nb = L // BLOCK
    kw, rep = lay.kw, lay.rep
    assert rep == 1 or lay.kv_heads == 2, "grouped queries: the two kv heads fill one 128-lane block"
    use_sinks = sinks is not None
    scale = HEAD_DIM ** -0.5
    ride = _Ride(exchange, 10 if use_sinks else 9, 4 if use_sinks else 3, 6)

    def kern(*refs):
        ins, outs, (ck_ref, cv_ref, *staged), exrefs = ride.split(refs)
        q_ref, kc_ref, kp_ref, vc_ref, vp_ref, do_ref, l_ref, d_ref, b_ref = ins[:9]
        s_ref = ins[9] if use_sinks else None
        dq_ref, dk_ref, dv_ref = outs[:3]
        ds_ref = outs[3] if use_sinks else None
        r = pl.program_id(0)
        i = pl.program_id(1)
        ride.around(r * (nb + 1) + i, lay.dil * (nb + 1), exrefs,
                    lambda: compute(q_ref, kc_ref, kp_ref, vc_ref, vp_ref, do_ref, l_ref, d_ref, b_ref, s_ref,
                                    dq_ref, dk_ref, dv_ref, ds_ref, ck_ref, cv_ref, r, i, *staged))

    def compute(q_ref, kc_ref, kp_ref, vc_ref, vp_ref, do_ref, l_ref, d_ref, b_ref, s_ref,
                dq_ref, dk_ref, dv_ref, ds_ref, ck_ref, cv_ref, r, i, sc_ref, dp_ref, pr_ref, dsc_ref):
        first = i == 0

        @pl.when(first)
        def _():
            ck_ref[...] = jnp.zeros_like(ck_ref)
            cv_ref[...] = jnp.zeros_like(cv_ref)

        if use_sinks:
            @pl.when(first & (r == 0))
            def _():
                ds_ref[...] = jnp.zeros_like(ds_ref)

        @pl.when(i < nb)
        def _():
            keys, values = _paired_kv(kp_ref, kc_ref, rep), _paired_kv(vp_ref, vc_ref, rep)
            keys_t = _paired_kv(kp_ref, kc_ref, rep, transposed=True)
            for pair in range(PAIRS):
                qs = _split_heads(q_ref[:, _pair_cols(pair)])
                dos = _split_heads(do_ref[:, _pair_cols(pair)].astype(BF16))
                s = _dot(keys(pair), qs, NT) * scale + b_ref[pair]
                sc_ref[pair, :BLOCK] = jnp.where(first, NEG_INF, s[:BLOCK])
                sc_ref[pair, BLOCK:] = s[BLOCK:]
                dp_ref[pair] = _dot(values(pair), dos, NT)
            for h in range(N_HEADS):
                cols = slice(h % 2 * BLOCK, (h % 2 + 1) * BLOCK)
                lrow = l_ref[h:h + 1, :]
                drow = d_ref[h:h + 1, :]
                p = jnp.exp(sc_ref[h // 2, :, cols] - lrow)
                pr_ref[h // 2, :, cols] = p.astype(BF16)
                dsc_ref[h // 2, :, cols] = (p * (dp_ref[h // 2, :, cols] - drow) * scale).astype(BF16)
                if use_sinks:
                    ds_ref[h:h + 1, :] += -(jnp.exp(s_ref[:, h:h + 1] - lrow) * drow)
            grouped = {}
            for pair in range(PAIRS):
                cols = _pair_cols(pair)
                qs = _split_heads(q_ref[:, cols])
                dos = _split_heads(do_ref[:, cols].astype(BF16))
                ds = dsc_ref[pair]
                both = _dot(keys_t(pair), ds, NN)
                dq_t = jnp.concatenate([both[:HEAD_DIM, :BLOCK], both[HEAD_DIM:, BLOCK:]], axis=0)
                dq_ref[:, cols] = dq_t.T.astype(dq_ref.dtype)
                dk = _dot(ds, qs, NN)
                dv = _dot(pr_ref[pair], dos, NN)
                if rep == 1:
                    dk_ref[:, cols] = (ck_ref[:, cols] + dk[:BLOCK]).astype(dk_ref.dtype)
                    dv_ref[:, cols] = (cv_ref[:, cols] + dv[:BLOCK]).astype(dv_ref.dtype)
                    ck_ref[:, cols] = dk[BLOCK:]
                    cv_ref[:, cols] = dv[BLOCK:]
                else:
                    g = 2 * pair // rep
                    grouped[g] = (dk, dv) if g not in grouped else (grouped[g][0] + dk, grouped[g][1] + dv)
            if rep > 1:
                fold = lambda t: t + pltpu.roll(t, HEAD_DIM, 1)
                first_half = _first_head_lanes((2 * BLOCK, LANES))
                dk = jnp.where(first_half, fold(grouped[0][0]), fold(grouped[1][0]))
                dv = jnp.where(first_half, fold(grouped[0][1]), fold(grouped[1][1]))
                dk_ref[...] = (ck_ref[...] + dk[:BLOCK]).astype(dk_ref.dtype)
                dv_ref[...] = (cv_ref[...] + dv[:BLOCK]).astype(dv_ref.dtype)
                ck_ref[...] = dk[BLOCK:]
                cv_ref[...] = dv[BLOCK:]

        @pl.when(i == nb)
        def _():
            dk_ref[...] = ck_ref[...].astype(dk_ref.dtype)
            dv_ref[...] = cv_ref[...].astype(dv_ref.dtype)
            if use_sinks:
                @pl.when(r == lay.dil - 1)
                def _():
                    ds_ref[...] = jnp.broadcast_to(jnp.sum(ds_ref[...], axis=1, keepdims=True), ds_ref.shape)

    cur = lambda i: jnp.minimum(i, nb - 1)
    prev = lambda i: jnp.maximum(jnp.minimum(i, nb - 1) - 1, 0)
    done = lambda i: jnp.maximum(i - 1, 0)
    qspec = lambda col: pl.BlockSpec((BLOCK, QW), lambda r, i: (cur(i), col(r)))
    per_head = pl.BlockSpec((None, N_HEADS, BLOCK), lambda r, i: (r, 0, cur(i)))
    in_specs = [
        qspec(lay.q_col),
        pl.BlockSpec((BLOCK, kw), lambda r, i: (cur(i), lay.k_col(r))),
        pl.BlockSpec((BLOCK, kw), lambda r, i: (prev(i), lay.k_col(r))),
        pl.BlockSpec((BLOCK, kw), lambda r, i: (cur(i), lay.v_col(r))),
        pl.BlockSpec((BLOCK, kw), lambda r, i: (prev(i), lay.v_col(r))),
        qspec(lambda r: r), per_head, per_head,
        pl.BlockSpec((PAIRS, 2 * BLOCK, 2 * BLOCK), lambda r, i: (0, 0, 0)),
    ]
    args = [proj, proj, proj, proj, proj, do, lse, dd, bias]
    out_specs = [
        qspec(lambda r: r),
        pl.BlockSpec((BLOCK, kw), lambda r, i: (done(i), r)),
        pl.BlockSpec((BLOCK, kw), lambda r, i: (done(i), r)),
    ]
    dkv_shape = jax.ShapeDtypeStruct((L, lay.dil * kw), BF16)
    out_shape = [jax.ShapeDtypeStruct((L, lay.dil * QW), BF16), dkv_shape, dkv_shape]
    if use_sinks:
        in_specs.append(pl.BlockSpec((1, N_HEADS), lambda r, i: (0, 0)))
        args.append(sinks)
        out_specs.append(pl.BlockSpec((N_HEADS, LANES), lambda r, i: (0, 0)))
        out_shape.append(jax.ShapeDtypeStruct((N_HEADS, LANES), F32))
    return pl.pallas_call(
        kern, name=name, grid=(lay.dil, nb + 1),
        in_specs=in_specs + ride.in_specs, out_specs=out_specs + ride.out_specs,
        out_shape=out_shape + ride.out_shapes,
        scratch_shapes=[pltpu.VMEM((BLOCK, kw), F32), pltpu.VMEM((BLOCK, kw), F32)]
        + [pltpu.VMEM((PAIRS, 2 * BLOCK, 2 * BLOCK), dt) for dt in (F32, F32, BF16, BF16)] + ride.scratch,
        input_output_aliases=ride.aliases,
        compiler_params=_params("arbitrary", "arbitrary"),
    )(*args, *ride.args)


def _assemble(groups, name, dils=(1,)):
    T = groups[0][0].shape[0] * dils[0]
    widths = [g[0].shape[1] // dils[0] for g in groups]
    total = sum(widths)
    flat = [a for g in groups for a in g]
    member_dils = [d for g in groups for d in dils[:len(g)]]

    def kern(*refs):
        ins = refs[:len(flat)]
        out_ref, cs_ref = refs[len(flat):]

        @pl.when(pl.program_id(0) == 0)
        def _():
            cs_ref[...] = jnp.zeros_like(cs_ref)

        pos = off = 0
        for g, w in zip(groups, widths):
            acc = _to_token_order(ins[pos], dils[0])
            for j in range(1, len(g)):
                acc = acc + _to_token_order(ins[pos + j], dils[j])
            pos += len(g)
            out_ref[:, off:off + w] = acc.astype(BF16)
            cs_ref[:, off:off + w] += jnp.sum(acc, axis=0, keepdims=True)
            off += w

    return pl.pallas_call(
        kern, name=name, grid=(T // LEAN_ROWS,),
        in_specs=[_view_spec(LEAN_ROWS, a.shape[1] // d, d) for a, d in zip(flat, member_dils)],
        out_specs=[pl.BlockSpec((LEAN_ROWS, total), lambda i: (i, 0)), pl.BlockSpec((1, total), lambda i: (0, 0))],
        out_shape=[jax.ShapeDtypeStruct((T, total), BF16), jax.ShapeDtypeStruct((1, total), F32)],
        compiler_params=_params("arbitrary"),
    )(*flat)


def _adamw(w, g, m, v, name):
    _, R, C = w.shape
    rows = min(R, ROWS)
    assert R % rows == 0

    def kern(w_ref, g_ref, m_ref, v_ref, d_ref, nm_ref, nv_ref):
        gv = g_ref[...]
        mn = ADAM_B1 * m_ref[...] + (1.0 - ADAM_B1) * gv
        vn = ADAM_B2 * v_ref[...] + (1.0 - ADAM_B2) * jnp.square(gv)
        m_hat = mn / (1.0 - ADAM_B1 ** ADAM_STEP)
        v_hat = vn / (1.0 - ADAM_B2 ** ADAM_STEP)
        d_ref[...] = -ADAM_LR * (m_hat / (jnp.sqrt(v_hat) + ADAM_EPS) + ADAM_WD * w_ref[...])
        nm_ref[...] = mn
        nv_ref[...] = vn

    blk = pl.BlockSpec((None, rows, C), lambda i: (0, i, 0))
    shp = jax.ShapeDtypeStruct((1, R, C), F32)
    return pl.pallas_call(
        kern, name=name, grid=(R // rows,),
        in_specs=[blk, pl.BlockSpec((rows, C), lambda i: (i, 0)), blk, blk], out_specs=[blk] * 3, out_shape=[shp] * 3,
        compiler_params=_params("parallel"),
    )(w, g, m, v)


def _sum_slots(slots, name):
    n, R, C = slots.shape
    SUM_ROWS = next(rows for rows in (256, 128, 64, 32, 16) if R % rows == 0)

    def kern(s_ref, o_ref):
        acc = s_ref[0].astype(F32)
        for k in range(1, n):
            acc = acc + s_ref[k].astype(F32)
        o_ref[...] = acc

    return pl.pallas_call(
        kern, name=name, grid=(R // SUM_ROWS,),
        in_specs=[pl.BlockSpec((n, SUM_ROWS, C), lambda i: (0, i, 0))],
        out_specs=pl.BlockSpec((SUM_ROWS, C), lambda i: (i, 0)),
        out_shape=jax.ShapeDtypeStruct((R, C), F32),
        compiler_params=_params("parallel"),
    )(slots)


def _place():
    return lax.axis_index("x"), lax.axis_index("y"), lax.axis_index("c")


def _index(p):
    return 4 * p[0] + 2 * p[1] + p[2]


FLIPS = [(fx, fy, fc) for fx in (0, 1) for fy in (0, 1) for fc in (0, 1)][1:]


def _peer(me, flip):
    return tuple(1 - a if f else a for a, f in zip(me, flip))


def _gather_rows(shards, part=(0, 1), into=None, relay_at=RELAY_AT):
    nw = len(shards)

    def plan(ins, outs, send_sems, recv_sems):
        x, y, c = me = _place()
        sibling = (x, y, 1 - c)
        chips = [(1 - x, y), (x, 1 - y), (1 - x, 1 - y)]

        def span(w):
            cnt = ins[w].shape[0] // part[1]
            return part[0] * cnt, cnt

        def rows(w, p):
            lo, cnt = span(w)
            return outs[w].at[pl.ds(_index(p) * ins[w].shape[0] + lo, cnt), :]

        def own(w):
            lo, cnt = span(w)
            return ins[w].at[pl.ds(lo, cnt), :]

        def copy(w, k, block, to):
            return pltpu.make_async_remote_copy(
                src_ref=own(w) if block is me else rows(w, block), dst_ref=rows(w, block),
                send_sem=send_sems.at[7 * w + k], recv_sem=recv_sems.at[7 * w + k],
                device_id=to, device_id_type=MESH)

        return me, sibling, chips, c, rows, own, copy

    def copies(ins, outs, send_sems, recv_sems, local_sems):
        me, sibling, chips, c, rows, own, copy = plan(ins, outs, send_sems, recv_sems)
        local = [pltpu.make_async_copy(own(w), rows(w, me), local_sems.at[w]) for w in range(nw)]
        sends, recvs = [], []
        for w in range(nw):
            sends.append(copy(w, 0, me, sibling))
            sends += [copy(w, 1 + j, me, (*chip, c)) for j, chip in enumerate(chips)]
            recvs.append(copy(w, 0, sibling, me))
            recvs += [copy(w, 4 + j, (*chip, 1 - c), me) for j, chip in enumerate(chips)]
        return local, sends, recvs

    def relay(ins, outs, send_sems, recv_sems, local_sems):
        me, sibling, chips, c, rows, own, copy = plan(ins, outs, send_sems, recv_sems)
        arrived = [copy(w, 1 + j, (*chip, c), me) for w in range(nw) for j, chip in enumerate(chips)]
        onward = [copy(w, 4 + j, (*chip, c), sibling) for w in range(nw) for j, chip in enumerate(chips)]
        return arrived, onward

    shapes = [jax.ShapeDtypeStruct((N_DEV * s.shape[0], s.shape[1]), s.dtype) for s in shards]
    aliases = {nw + w: w for w in range(nw)} if into else None
    return _Exchange(shards + (into or []), shapes, 7 * nw, nw, copies, aliases=aliases, relay=relay,
                     relay_at=relay_at)


def _scatter_rows(parts, part=(0, 1)):
    nw = len(parts)

    def copies(ins, outs, send_sems, recv_sems, local_sems):
        me = _place()

        def src(w, owner):
            n = ins[w].shape[0] // N_DEV
            cnt = n // part[1]
            return ins[w].at[pl.ds(_index(owner) * n + part[0] * cnt, cnt), :]

        def copy(k, w, owner, sender, to):
            return pltpu.make_async_remote_copy(
                src_ref=src(w, owner), dst_ref=outs[w].at[_index(sender)],
                send_sem=send_sems.at[nw * k + w], recv_sem=recv_sems.at[nw * k + w],
                device_id=to, device_id_type=MESH)

        local = [pltpu.make_async_copy(src(w, me), outs[w].at[_index(me)], local_sems.at[w]) for w in range(nw)]
        peers = [_peer(me, flip) for flip in FLIPS]
        sends = [copy(k, w, peer, me, peer) for k, peer in enumerate(peers) for w in range(nw)]
        recvs = [copy(k, w, me, peer, me) for k, peer in enumerate(peers) for w in range(nw)]
        return local, sends, recvs

    shapes = [jax.ShapeDtypeStruct((N_DEV, p.shape[0] // N_DEV // part[1], p.shape[1]), p.dtype) for p in parts]
    return _Exchange(parts, shapes, 7 * nw, nw, copies)


def _sum_over_devices(v):
    shape = v.shape

    def body(v_ref, sum_ref, all_ref, send_sems, recv_sems):
        me = _place()
        all_ref[_index(me)] = v_ref[...]
        sends = []
        for k, flip in enumerate(FLIPS):
            peer = _peer(me, flip)
            sends.append(pltpu.make_async_remote_copy(
                src_ref=v_ref, dst_ref=all_ref.at[_index(me)],
                send_sem=send_sems.at[k], recv_sem=recv_sems.at[k], device_id=peer, device_id_type=MESH))
            sends[-1].start()
        for k, flip in enumerate(FLIPS):
            peer = _peer(me, flip)
            pltpu.make_async_remote_copy(
                src_ref=v_ref, dst_ref=all_ref.at[_index(peer)],
                send_sem=send_sems.at[k], recv_sem=recv_sems.at[k], device_id=peer, device_id_type=MESH).wait_recv()
        for cp in sends:
            cp.wait_send()
        acc = all_ref[0]
        for s in range(1, N_DEV):
            acc = acc + all_ref[s]
        sum_ref[...] = acc

    vmem = pl.BlockSpec(memory_space=pltpu.VMEM)
    return pl.pallas_call(
        body, name="sum_small_grads",
        in_specs=[vmem], out_specs=[vmem, vmem],
        out_shape=[jax.ShapeDtypeStruct(shape, F32), jax.ShapeDtypeStruct((N_DEV,) + shape, F32)],
        scratch_shapes=[pltpu.SemaphoreType.DMA((7,)), pltpu.SemaphoreType.DMA((7,))],
    )(v)[0]


SMALL_ROWS = 8


def _pack_small(vectors):
    padded = []
    for vec in vectors:
        vec = vec.reshape(-1)
        padded.append(jnp.pad(vec, (0, -vec.shape[0] % 128)))
    flat = jnp.concatenate(padded)
    flat = jnp.pad(flat, (0, -flat.shape[0] % (SMALL_ROWS * 128)))
    return flat.reshape(SMALL_ROWS, -1)


def _unpack_small(packed, shapes):
    flat = packed.reshape(-1)
    out, off = [], 0
    for shp in shapes:
        n = int(np.prod(shp))
        out.append(flat[off:off + n].reshape(shp))
        off += n + (-n % 128)
    return out


def kernel(x, g_attn, w_in, b_in, sinks_a, g_out_a, g_out_b, w_out, g_mlp, w_1, w_2, g_final, loss_target, m_g_attn, m_w_in, m_b_in, m_sinks_a, m_g_out_a, m_g_out_b, m_w_out, m_g_mlp, m_w_1, m_w_2, m_g_final, v_g_attn, v_w_in, v_b_in, v_sinks_a, v_g_out_a, v_g_out_b, v_w_out, v_g_mlp, v_w_1, v_w_2, v_g_final):
    xs, tgt = x[0], loss_target[0]
    T, D = xs.shape
    n_a = QW + 2 * KV_HEADS_A * HEAD_DIM
    g_fin = g_final.reshape(1, D)

    shards = [w_in[0].T.astype(BF16), w_out[0].astype(BF16), w_1[0].T.astype(BF16), w_2[0].astype(BF16)]
    ident = lambda acc: (acc,)
    add = lambda acc, other: (acc + other,)

    h1, w_in_t = _norm_fwd(xs, g_attn, "norm_attn", exchange=_gather_rows(shards[:1], relay_at=1.0))
    n_in = w_in_t.shape[0]
    proj_a, = _proj_views(h1, w_in_t, b_in, (0, n_a), [1], "proj_a")
    dils = [dil for _, dil in DILATED_BRANCHES]
    *proj_b, w_o = _proj_views(h1, w_in_t, b_in, (n_a, n_in - n_a), dils, "proj_b", exchange=_gather_rows(shards[1:2]))

    lay_a = _AttnLayout(1, KV_HEADS_A, 0, 0, 0, QW // (KV_HEADS_A * HEAD_DIM), QW // (KV_HEADS_A * HEAD_DIM) + 1)
    bias_a = _band_bias(WINDOW_A - 1, 1)
    o_a, l_a, w_1_t = _attn_fwd(proj_a, bias_a, sinks_a, lay_a, "attn_a_fwd",
                                exchange=_gather_rows(shards[2:3], part=(0, 4)))
    branches = []
    for n, (window, dil) in enumerate(DILATED_BRANCHES):
        lay = _AttnLayout(dil, N_HEADS, 3, 0, 3, 1, 2)
        bias = _band_bias(window // dil, dil)
        ride = _gather_rows(shards[2:3], part=(n + 1, 4), into=[w_1_t])
        o, lse, w_1_t = _attn_fwd(proj_b[n], bias, None, lay, f"attn_b{dil}_fwd", exchange=ride)
        branches.append((lay, bias, proj_b[n], o, lse))
    o_b = [br[3] for br in branches]
    l_b = [br[4].transpose(2, 0, 1).reshape(T, N_HEADS) for br in branches]

    mix = _mix_fwd(o_a, o_b, l_b, g_out_a, g_out_b, dils)
    def residual_and_norm(acc, res, g):
        x_new = acc + res
        return x_new, (x_new * _rstd(x_new)) * g

    assert TILE_WHOLE_ROWS["tn"] == D
    x2, h2 = _matmul(mix, w_o, "nn", [F32, BF16], residual_and_norm, tk=D, tile_ins=[xs], row_ins=[g_mlp],
                     name="out_proj", **TILE_WHOLE_ROWS)

    def relu_sq(acc):
        u = jnp.maximum(acc, 0.0)
        return u, u * u

    u, u_sq, w_2_f = _matmul(h2, w_1_t, "nt", [BF16, BF16], relu_sq, tk=D, name="mlp_up",
                             exchange=_gather_rows(shards[3:]), **TILE_WHOLE_ROWS)
    x3, = _matmul(u_sq, w_2_f, "nn", [F32], add, tk=4096, tile_ins=[x2], name="mlp_down", **TILE)

    dx3, dx3_b, dg_final, loss_dev = _loss_head(x3, tgt, g_fin)

    d_pre, = _matmul(dx3_b, w_2_f, "nt", [BF16], lambda acc, uu: (acc * (2.0 * uu.astype(F32)),),
                     tk=D, tile_ins=[u], name="mlp_down_bwd", **TILE_WHOLE_ROWS)
    dw_2, = _matmul(u_sq, dx3_b, "tn", [BF16], ident, name="mlp_down_wgrad", **TILE_WGRAD)
    dh2, slots_2a = _matmul(d_pre, w_1_t, "nn", [BF16], ident, tk=4096, name="mlp_up_bwd",
                            exchange=_scatter_rows([dw_2], part=(0, 2)), **TILE)
    dw_1_t, slots_2b = _matmul(d_pre, h2, "tn", [BF16], ident, name="mlp_up_wgrad",
                               exchange=_scatter_rows([dw_2], part=(1, 2)), **TILE_WGRAD)
    dx2, dg_mlp, dx2_b, dmix = _norm_bwd(dh2, x2, g_mlp, dx3, "norm_mlp_bwd", then_w_t=w_o)
    dw_o, = _matmul(mix, dx2_b, "tn", [BF16], ident, name="out_proj_wgrad", **TILE_WGRAD)
    do_a, dd_a, do1, do2, do3, dd1, dd2, dd3, dg_out_a, dg_out_b, slots_o = _mix_bwd(
        dmix, o_a, o_b, l_b, g_out_a, g_out_b, dils, exchange=_scatter_rows([dw_o]))

    by_class = lambda d, dil: d.reshape(T // dil, dil, N_HEADS).transpose(1, 2, 0)
    slots_1 = [None] * 4
    dq_a, dk_a, dv_a, dsinks, slots_1[0] = _attn_bwd(proj_a, do_a, l_a, by_class(dd_a, 1), bias_a, sinks_a, lay_a,
                                                     "attn_a_bwd", exchange=_scatter_rows([dw_1_t], part=(0, 4)))
    dsinks = dsinks[:, 0].reshape(1, N_HEADS)
    dqs, dks, dvs = [], [], []
    for n, ((lay, bias, view, _, lse), do_n, dd_n) in enumerate(zip(branches, (do1, do2, do3), (dd1, dd2, dd3))):
        dq, dk, dv, slots_1[n + 1] = _attn_bwd(view, do_n, lse, by_class(dd_n, lay.dil), bias, None, lay,
                                               f"attn_b{lay.dil}_bwd",
                                               exchange=_scatter_rows([dw_1_t], part=(n + 1, 4)))
        dqs.append(dq)
        dks.append(dk)
        dvs.append(dv)
    dproj, db_in = _assemble([[dq_a], [dk_a], [dv_a], dqs, dks, dvs], "dproj", dils)

    dw_in_t, = _matmul(dproj, h1, "tn", [BF16], ident, tm=n_in // 2, tn=1024, tk=1024, name="in_proj_wgrad")
    dh1, slots_in = _matmul(dproj, w_in_t, "nn", [BF16], ident, tk=n_in, name="in_proj_bwd",
                            exchange=_scatter_rows([dw_in_t]), **TILE)
    dx, dg_attn = _norm_bwd(dh1, xs, g_attn, dx2, "norm_attn_bwd")

    g_w_in = _sum_slots(slots_in, "sum_w_in_grads").T
    g_w_out = _sum_slots(slots_o, "sum_w_out_grads")
    g_w_1 = jnp.concatenate([_sum_slots(s, f"sum_w_1_grads_{n}") for n, s in enumerate(slots_1)]).T
    g_w_2 = jnp.concatenate([_sum_slots(slots_2a, "sum_w_2_grads_0"), _sum_slots(slots_2b, "sum_w_2_grads_1")])

    small_w = [g_attn, b_in, sinks_a, g_out_a, g_out_b, g_mlp, g_final]
    small_m = [m_g_attn, m_b_in, m_sinks_a, m_g_out_a, m_g_out_b, m_g_mlp, m_g_final]
    small_v = [v_g_attn, v_b_in, v_sinks_a, v_g_out_a, v_g_out_b, v_g_mlp, v_g_final]
    small_g = [dg_attn, db_in, dsinks, dg_out_a, dg_out_b, dg_mlp, dg_final]
    summed = _sum_over_devices(_pack_small(small_g + [loss_dev[:, :1]]))
    shapes = [w.shape for w in small_w]
    *g_small, loss = _unpack_small(summed, shapes + [()])

    big = [
        _adamw(w_in, g_w_in, m_w_in, v_w_in, "adamw_w_in"),
        _adamw(w_out, g_w_out, m_w_out, v_w_out, "adamw_w_out"),
        _adamw(w_1, g_w_1, m_w_1, v_w_1, "adamw_w_1"),
        _adamw(w_2, g_w_2, m_w_2, v_w_2, "adamw_w_2"),
    ]
    g_packed = _pack_small(g_small)
    small = _adamw(_pack_small(small_w)[None], g_packed, _pack_small(small_m)[None], _pack_small(small_v)[None],
                   "adamw_small")
    small = [_unpack_small(s, shapes) for s in small]

    def ordered(small_list, big_list):
        s = list(small_list)
        return [s[0], big_list[0], s[1], s[2], s[3], s[4], big_list[1], s[5], big_list[2], big_list[3], s[6]]

    grads = ordered(g_small, [g[None] for g in (g_w_in, g_w_out, g_w_1, g_w_2)])
    deltas = ordered(small[0], [b[0] for b in big])
    new_m = ordered(small[1], [b[1] for b in big])
    new_v = ordered(small[2], [b[2] for b in big])
    return (loss, dx[None], *grads, *deltas, *new_m, *new_v)
```

```python
import numpy as np
import jax
import jax.numpy as jnp
from jax import lax
from jax.experimental import pallas as pl
from jax.experimental.pallas import tpu as pltpu

F32 = jnp.float32
BF16 = jnp.bfloat16

HEAD_DIM = 64
N_HEADS = 16
KV_HEADS_A = 2
BLOCK = 128
WINDOW_A = 128
DILATED_BRANCHES = ((128, 1), (512, 4), (2048, 16))
EPS = 1e-5
NEG_INF = -1e30
N_DEV = 8

ADAM_LR = 0.001
ADAM_B1 = 0.9
ADAM_B2 = 0.999
ADAM_EPS = 1e-08
ADAM_WD = 0.01
ADAM_STEP = 10

VMEM_LIMIT_BYTES = 56 * 1024 * 1024
MESH = pl.DeviceIdType.MESH
ANY = pl.BlockSpec(memory_space=pl.ANY)

NN = (((1,), (0,)), ((), ()))
NT = (((1,), (1,)), ((), ()))
TN = (((0,), (0,)), ((), ()))


def _dot(a, b, dims):
    return lax.dot_general(a, b, dims, preferred_element_type=F32)


def _params(*sem):
    return pltpu.CompilerParams(dimension_semantics=sem, vmem_limit_bytes=VMEM_LIMIT_BYTES)


RELAY_AT = 0.6


class _Exchange:
    def __init__(self, ins, out_shapes, n_remote, n_local, copies, aliases=None, relay=None, relay_at=RELAY_AT):
        self.ins, self.out_shapes = list(ins), list(out_shapes)
        self.n_remote, self.n_local = n_remote, n_local
        self.copies = copies
        self.relay = relay
        self.relay_at = relay_at
        self.aliases = aliases or {}

    def start(self, refs):
        local, sends, _ = self.copies(*refs)
        for cp in local + sends:
            cp.start()

    def middle(self, refs):
        arrived, onward = self.relay(*refs)
        for got, cp in zip(arrived, onward):
            got.wait_recv()
            cp.start()

    def finish(self, refs):
        local, sends, recvs = self.copies(*refs)
        for cp in recvs:
            cp.wait_recv()
        for cp in sends:
            cp.wait_send()
        for cp in local:
            cp.wait()
        if self.relay:
            for cp in self.relay(*refs)[1]:
                cp.wait_send()


class _Ride:
    def __init__(self, ex, n_in, n_out, n_scratch):
        self.ex = ex
        self.n = (n_in, n_out, n_scratch)
        self.args = ex.ins if ex else []
        self.in_specs = [ANY] * len(self.args)
        self.out_shapes = ex.out_shapes if ex else []
        self.out_specs = [ANY] * len(self.out_shapes)
        self.scratch = [pltpu.SemaphoreType.DMA((ex.n_remote,)), pltpu.SemaphoreType.DMA((ex.n_remote,)),
                        pltpu.SemaphoreType.DMA((max(ex.n_local, 1),))] if ex else []
        self.aliases = {n_in + i: n_out + o for i, o in ex.aliases.items()} if ex else {}

    def split(self, refs):
        n_in, n_out, n_scratch = self.n
        a = n_in
        b = a + len(self.args)
        c = b + n_out
        d = c + len(self.out_shapes)
        e = d + n_scratch
        return refs[:a], refs[b:c], refs[d:e], (refs[a:b], refs[c:d], *refs[e:])

    def around(self, step, n_steps, exrefs, compute):
        if self.ex is None:
            compute()
            return

        @pl.when(step == 0)
        def _():
            self.ex.start(exrefs)

        compute()

        if self.ex.relay:
            @pl.when(step == int(self.ex.relay_at * (n_steps - 1)))
            def _():
                self.ex.middle(exrefs)

        @pl.when(step == n_steps - 1)
        def _():
            self.ex.finish(exrefs)


TILE = dict(tm=512, tn=1024)
TILE_WHOLE_ROWS = dict(tm=512, tn=2048)
TILE_WGRAD = dict(tm=1024, tn=1024, tk=4096)

def _matmul(a, b, dims, out_dtypes, epilogue, *, tm, tn, tk, name, tile_ins=(), row_ins=(), exchange=None):
    if dims == "tn":
        K, M = a.shape
    else:
        M, K = a.shape
    N = b.shape[0] if dims == "nt" else b.shape[1]
    tm, tn, tk = min(tm, M), min(tn, N), min(tk, K)
    assert M % tm == 0 and N % tn == 0 and K % tk == 0, (name, M, N, K, tm, tn, tk)
    grid = (M // tm, N // tn, K // tk)
    nk = grid[2]
    n_tile, n_row, n_out = len(tile_ins), len(row_ins), len(out_dtypes)
    dn = {"nn": NN, "nt": NT, "tn": TN}[dims]
    ride = _Ride(exchange, 2 + n_tile + n_row, n_out, 1 if nk > 1 else 0)

    def kern(*refs):
        ins, out_refs, scratch, exrefs = ride.split(refs)
        a_ref, b_ref = ins[:2]
        tile_refs = ins[2:2 + n_tile]
        row_refs = ins[2 + n_tile:]
        ids = [pl.program_id(d) for d in range(3)]

        def finish(acc):
            outs = epilogue(acc, *[r[...] for r in tile_refs], *[r[...] for r in row_refs])
            for o_ref, o in zip(out_refs, outs):
                o_ref[...] = o.astype(o_ref.dtype)

        def compute():
            if nk == 1:
                finish(_dot(a_ref[...], b_ref[...], dn))
                return
            acc_ref = scratch[0]

            @pl.when(ids[2] == 0)
            def _():
                acc_ref[...] = jnp.zeros_like(acc_ref)

            acc_ref[...] += _dot(a_ref[...], b_ref[...], dn)

            @pl.when(ids[2] == nk - 1)
            def _():
                finish(acc_ref[...])

        ride.around((ids[0] * grid[1] + ids[1]) * grid[2] + ids[2], grid[0] * grid[1] * grid[2], exrefs, compute)

    if dims == "tn":
        a_spec = pl.BlockSpec((tk, tm), lambda i, j, k: (k, i))
    else:
        a_spec = pl.BlockSpec((tm, tk), lambda i, j, k: (i, k))
    if dims == "nt":
        b_spec = pl.BlockSpec((tn, tk), lambda i, j, k: (j, k))
    else:
        b_spec = pl.BlockSpec((tk, tn), lambda i, j, k: (k, j))
    tile_spec = pl.BlockSpec((tm, tn), lambda i, j, k: (i, j))
    row_spec = pl.BlockSpec((1, tn), lambda i, j, k: (0, j))
    sem = ("arbitrary",) * 3 if exchange else ("parallel", "parallel", "arbitrary")
    return pl.pallas_call(
        kern,
        name=name,
        grid=grid,
        in_specs=[a_spec, b_spec] + [tile_spec] * n_tile + [row_spec] * n_row + ride.in_specs,
        out_specs=[tile_spec] * n_out + ride.out_specs,
        out_shape=[jax.ShapeDtypeStruct((M, N), dt) for dt in out_dtypes] + ride.out_shapes,
        scratch_shapes=([pltpu.VMEM((tm, tn), F32)] if nk > 1 else []) + ride.scratch,
        input_output_aliases=ride.aliases,
        compiler_params=_params(*sem),
    )(a, b, *tile_ins, *row_ins, *ride.args)


PROJ_ROWS = 256


def _proj_views(a, w_t, bias, cols, dils, name, exchange=None):
    T, K = a.shape
    first, N = cols
    ride = _Ride(exchange, 3, len(dils), 0)

    def kern(*refs):
        (a_ref, w_ref, b_ref), outs, _, exrefs = ride.split(refs)

        def compute():
            acc = _dot(a_ref[...], w_ref[...], NT) + b_ref[...]
            for out_ref, dil in zip(outs, dils):
                _to_class_order(acc, out_ref, dil)

        ride.around(pl.program_id(0), T // LEAN_ROWS, exrefs, compute)

    once = pl.Buffered(1)
    return pl.pallas_call(
        kern, name=name, grid=(T // LEAN_ROWS,),
        in_specs=[pl.BlockSpec((LEAN_ROWS, K), lambda i: (i, 0)),
                  pl.BlockSpec((pl.Element(N), pl.Element(K)), lambda i: (first, 0), pipeline_mode=once),
                  pl.BlockSpec((pl.Element(1), pl.Element(N)), lambda i: (0, first), pipeline_mode=once)] + ride.in_specs,
        out_specs=[_view_spec(LEAN_ROWS, N, d) for d in dils] + ride.out_specs,
        out_shape=[jax.ShapeDtypeStruct((T // d, d * N), BF16) for d in dils] + ride.out_shapes,
        scratch_shapes=ride.scratch,
        input_output_aliases=ride.aliases,
        compiler_params=_params("arbitrary"),
    )(a, w_t, bias, *ride.args)


ROWS = 256
LEAN_ROWS = 512
MIX_ROWS = 256


def _rstd(xv):
    return lax.rsqrt(jnp.mean(xv * xv, axis=-1, keepdims=True) + EPS)


def _norm_fwd(x, g, name, exchange=None):
    T, D = x.shape
    ride = _Ride(exchange, 2, 1, 0)

    def kern(*refs):
        (x_ref, g_ref), (h_ref,), _, exrefs = ride.split(refs)

        def compute():
            xv = x_ref[...]
            h_ref[...] = ((xv * _rstd(xv)) * g_ref[...]).astype(h_ref.dtype)

        ride.around(pl.program_id(0), T // ROWS, exrefs, compute)

    row = pl.BlockSpec((ROWS, D), lambda i: (i, 0))
    return pl.pallas_call(
        kern, name=name, grid=(T // ROWS,),
        in_specs=[row, pl.BlockSpec((1, D), lambda i: (0, 0))] + ride.in_specs,
        out_specs=[row] + ride.out_specs,
        out_shape=[jax.ShapeDtypeStruct((T, D), BF16)] + ride.out_shapes,
        scratch_shapes=ride.scratch, input_output_aliases=ride.aliases,
        compiler_params=_params("arbitrary"),
    )(x, g, *ride.args)


def _norm_bwd(dh, x, g, res, name, then_w_t=None):
    T, D = x.shape
    rows = PROJ_ROWS if then_w_t is not None else min(LEAN_ROWS, T)

    def kern(dh_ref, x_ref, g_ref, res_ref, *rest):
        if then_w_t is None:
            dx_ref, dg_ref = rest
        else:
            w_ref, dx_ref, dg_ref, dxb_ref, y_ref = rest

        @pl.when(pl.program_id(0) == 0)
        def _():
            dg_ref[...] = jnp.zeros_like(dg_ref)

        xv = x_ref[...]
        r = _rstd(xv)
        xn = xv * r
        dhv = dh_ref[...].astype(F32)
        dg_ref[...] += jnp.sum(dhv * xn, axis=0, keepdims=True)
        t = dhv * g_ref[...]
        dx = res_ref[...] + r * (t - xn * jnp.mean(t * xn, axis=-1, keepdims=True))
        dx_ref[...] = dx
        if then_w_t is not None:
            dxb = dx.astype(BF16)
            dxb_ref[...] = dxb
            y_ref[...] = _dot(dxb, w_ref[...], NT)

    row = pl.BlockSpec((rows, D), lambda i: (i, 0))
    vec = pl.BlockSpec((1, D), lambda i: (0, 0))
    in_specs, args = [row, row, vec, row], [dh, x, g, res]
    out_specs = [row, vec]
    out_shape = [jax.ShapeDtypeStruct((T, D), F32), jax.ShapeDtypeStruct((1, D), F32)]
    if then_w_t is not None:
        N = then_w_t.shape[0]
        in_specs.append(pl.BlockSpec((N, D), lambda i: (0, 0)))
        args.append(then_w_t)
        out_specs += [row, pl.BlockSpec((rows, N), lambda i: (i, 0))]
        out_shape += [jax.ShapeDtypeStruct((T, D), BF16), jax.ShapeDtypeStruct((T, N), F32)]
    return pl.pallas_call(
        kern, name=name, grid=(T // rows,), in_specs=in_specs, out_specs=out_specs, out_shape=out_shape,
        compiler_params=_params("arbitrary"),
    )(*args)


def _loss_head(x3, tgt, g):
    T, D = x3.shape

    def kern(x_ref, t_ref, g_ref, dx_ref, dxb_ref, dg_ref, loss_ref):
        @pl.when(pl.program_id(0) == 0)
        def _():
            dg_ref[...] = jnp.zeros_like(dg_ref)
            loss_ref[...] = jnp.zeros_like(loss_ref)

        xv = x_ref[...]
        gv = g_ref[...]
        r = _rstd(xv)
        xn = xv * r
        err = xn * gv - t_ref[...]
        per_tok = jnp.mean(err * err, axis=-1, keepdims=True)
        loss_ref[...] += 0.5 * jnp.sum(per_tok, axis=0, keepdims=True)
        dy = err * (1.0 / D)
        dg_ref[...] += jnp.sum(dy * xn, axis=0, keepdims=True)
        t = dy * gv
        dx = r * (t - xn * jnp.mean(t * xn, axis=-1, keepdims=True))
        dx_ref[...] = dx
        dxb_ref[...] = dx.astype(BF16)

    row = pl.BlockSpec((LEAN_ROWS, D), lambda i: (i, 0))
    vec = pl.BlockSpec((1, D), lambda i: (0, 0))
    return pl.pallas_call(
        kern, name="loss_head", grid=(T // LEAN_ROWS,),
        in_specs=[row, row, vec],
        out_specs=[row, row, vec, pl.BlockSpec((1, 128), lambda i: (0, 0))],
        out_shape=[jax.ShapeDtypeStruct((T, D), F32), jax.ShapeDtypeStruct((T, D), BF16),
                   jax.ShapeDtypeStruct((1, D), F32), jax.ShapeDtypeStruct((1, 128), F32)],
        compiler_params=_params("arbitrary"),
    )(x3, tgt, g)


def _spread_matrix():
    head_of_lane = np.arange(N_HEADS * HEAD_DIM) // HEAD_DIM
    return jnp.asarray(np.arange(N_HEADS)[:, None] == head_of_lane[None, :], dtype=BF16)


def _pieces(v, n):
    out = []
    for _ in range(n):
        piece = v.astype(BF16)
        out.append(piece)
        v = v - piece.astype(F32)
    return out


def _spread(v, spread):
    return sum(_dot(p, spread, NN) for p in _pieces(v, 2))


def _spread_weights(w1, w2, spread):
    s1, s2 = _spread(w1, spread), _spread(w2, spread)
    return s1, s2, 1.0 - s1 - s2


def _head_sums(v, spread):
    return sum(_dot(p, spread, NT) for p in _pieces(v, 2))


def _branch_weights(l1, l2, l3):
    lm = jnp.maximum(jnp.maximum(l1, l2), l3)
    e1, e2, e3 = jnp.exp(l1 - lm), jnp.exp(l2 - lm), jnp.exp(l3 - lm)
    inv = 1.0 / (e1 + e2 + e3)
    return e1 * inv, e2 * inv, e3 * inv


def _to_token_order(view_ref, dil):
    if dil == 1:
        return view_ref[...].astype(F32)
    n_l, w = view_ref.shape[0], view_ref.shape[1] // dil
    cols = []
    for cb in range(w // LANES):
        by_class = jnp.stack([view_ref[:, r * w + cb * LANES:r * w + (cb + 1) * LANES].astype(F32) for r in range(dil)])
        cols.append(jnp.swapaxes(by_class, 0, 1).reshape(n_l * dil, LANES))
    return jnp.concatenate(cols, axis=1)


def _to_class_order(val, view_ref, dil):
    if dil == 1:
        view_ref[...] = val.astype(view_ref.dtype)
        return
    n, w = val.shape
    for cb in range(w // LANES):
        by_class = jnp.swapaxes(val[:, cb * LANES:(cb + 1) * LANES].reshape(n // dil, dil, LANES), 0, 1)
        for r in range(dil):
            view_ref[:, r * w + cb * LANES:r * w + (cb + 1) * LANES] = by_class[r].astype(view_ref.dtype)


def _view_spec(rows, width, dil):
    return pl.BlockSpec((rows // dil, dil * width), lambda i: (i, 0))


def _mix_fwd(oa, obs, lbs, ga, gb, dils):
    T, W = oa.shape

    def kern(oa_ref, o1, o2, o3, l1, l2, l3, ga_ref, gb_ref, sp_ref, mix_ref):
        sp = sp_ref[...]
        w1, w2, w3 = _branch_weights(l1[...], l2[...], l3[...])
        on = [_to_token_order(o, d) for o, d in zip((o1, o2, o3), dils)]
        s1, s2, s3 = _spread_weights(w1, w2, sp)
        ob = s1 * on[0] + s2 * on[1] + s3 * on[2]
        oav = oa_ref[...]
        mix_ref[:, :W] = ((oav * _rstd(oav)) * ga_ref[...]).astype(BF16)
        mix_ref[:, W:] = ((ob * _rstd(ob)) * gb_ref[...]).astype(BF16)

    row = pl.BlockSpec((ROWS, W), lambda i: (i, 0))
    per_head = pl.BlockSpec((ROWS, N_HEADS), lambda i: (i, 0))
    vec = pl.BlockSpec((1, W), lambda i: (0, 0))
    return pl.pallas_call(
        kern, name="mix_fwd", grid=(T // ROWS,),
        in_specs=[row] + [_view_spec(ROWS, W, d) for d in dils] + [per_head] * 3
        + [vec, vec, pl.BlockSpec((N_HEADS, W), lambda i: (0, 0))],
        out_specs=pl.BlockSpec((ROWS, 2 * W), lambda i: (i, 0)),
        out_shape=jax.ShapeDtypeStruct((T, 2 * W), BF16),
        compiler_params=_params("parallel"),
    )(oa, *obs, *lbs, ga, gb, _spread_matrix())


def _mix_bwd(dmix, oa, obs, lbs, ga, gb, dils, exchange=None):
    T, W = oa.shape
    ride = _Ride(exchange, 11, 10, 0)

    def kern(*refs):
        ins, outs, _, exrefs = ride.split(refs)
        ride.around(pl.program_id(0), T // MIX_ROWS, exrefs, lambda: compute(*ins, *outs))

    def compute(dm_ref, oa_ref, o1, o2, o3, l1, l2, l3, ga_ref, gb_ref, sp_ref,
                doa_ref, da_ref, do1, do2, do3, d1, d2, d3, dga_ref, dgb_ref):
        @pl.when(pl.program_id(0) == 0)
        def _():
            dga_ref[...] = jnp.zeros_like(dga_ref)
            dgb_ref[...] = jnp.zeros_like(dgb_ref)

        sp = sp_ref[...]
        oav = oa_ref[...]
        r = _rstd(oav)
        on = oav * r
        dy = dm_ref[:, :W]
        dga_ref[...] += jnp.sum(dy * on, axis=0, keepdims=True)
        t = dy * ga_ref[...]
        doa = r * (t - on * jnp.mean(t * on, axis=-1, keepdims=True))
        doa_ref[...] = doa.astype(BF16)
        da_ref[...] = _head_sums(doa * oav, sp)
        w1, w2, w3 = _branch_weights(l1[...], l2[...], l3[...])
        s1, s2, s3 = _spread_weights(w1, w2, sp)
        on = [_to_token_order(o, d) for o, d in zip((o1, o2, o3), dils)]
        ob = s1 * on[0] + s2 * on[1] + s3 * on[2]
        r = _rstd(ob)
        on = ob * r
        dy = dm_ref[:, W:]
        dgb_ref[...] += jnp.sum(dy * on, axis=0, keepdims=True)
        t = dy * gb_ref[...]
        dob = r * (t - on * jnp.mean(t * on, axis=-1, keepdims=True))
        c = _head_sums(dob * ob, sp)
        for do_ref, sn, d in zip((do1, do2, do3), (s1, s2, s3), dils):
            _to_class_order(sn * dob, do_ref, d)
        d1[...] = w1 * c
        d2[...] = w2 * c
        d3[...] = w3 * c

    row = pl.BlockSpec((MIX_ROWS, W), lambda i: (i, 0))
    per_head = pl.BlockSpec((MIX_ROWS, N_HEADS), lambda i: (i, 0))
    vec = pl.BlockSpec((1, W), lambda i: (0, 0))
    bf = jax.ShapeDtypeStruct((T, W), BF16)
    ph = jax.ShapeDtypeStruct((T, N_HEADS), F32)
    vv = jax.ShapeDtypeStruct((1, W), F32)
    views = [_view_spec(MIX_ROWS, W, d) for d in dils]
    return pl.pallas_call(
        kern, name="mix_bwd", grid=(T // MIX_ROWS,),
        in_specs=[pl.BlockSpec((MIX_ROWS, 2 * W), lambda i: (i, 0)), row] + views + [per_head] * 3 + [vec, vec,
                  pl.BlockSpec((N_HEADS, W), lambda i: (0, 0))] + ride.in_specs,
        out_specs=[row, per_head] + views + [per_head, per_head, per_head, vec, vec] + ride.out_specs,
        out_shape=[bf, ph] + [jax.ShapeDtypeStruct(o.shape, F32) for o in obs] + [ph, ph, ph, vv, vv]
        + ride.out_shapes,
        scratch_shapes=ride.scratch,
        input_output_aliases=ride.aliases,
        compiler_params=_params("arbitrary"),
    )(dmix, oa, *obs, *lbs, ga, gb, _spread_matrix(), *ride.args)


def _alibi_slopes(n):
    return np.asarray(2.0 ** (-8.0 * (np.arange(n) + 1) / n)).astype(np.float32)


def _band_bias(max_steps, step_dist):
    qi = np.arange(BLOCK)[None, :]
    kj = np.arange(BLOCK)[:, None]
    slopes = _alibi_slopes(N_HEADS)
    halves = []
    for steps in (qi + BLOCK - kj, qi - kj):
        valid = (steps >= 0) & (steps <= max_steps)
        alibi = slopes[:, None, None] * (step_dist * steps).astype(np.float32)[None]
        halves.append(np.where(valid[None], -alibi, np.float32(NEG_INF)).astype(np.float32))
    per_head = np.concatenate(halves, axis=1)
    return jnp.asarray(np.concatenate([per_head[0::2], per_head[1::2]], axis=2))


class _AttnLayout:
    def __init__(self, dil, kv_heads, q_stride, q_off, k_stride, k_off, v_off):
        self.dil = dil
        self.kv_heads = kv_heads
        self.kw = kv_heads * HEAD_DIM
        self.rep = N_HEADS // kv_heads
        self.q_col = lambda r: r * q_stride + q_off
        self.k_col = lambda r: r * k_stride + k_off
        self.v_col = lambda r: r * k_stride + v_off


QW = N_HEADS * HEAD_DIM
LANES = 128


PAIRS = N_HEADS // 2


def _pair_cols(pair):
    return slice(pair * LANES, (pair + 1) * LANES)


def _first_head_lanes(shape):
    return lax.broadcasted_iota(jnp.int32, shape, 1) < HEAD_DIM


def _split_heads(pair):
    first = _first_head_lanes(pair.shape)
    zero = jnp.zeros_like(pair)
    return jnp.concatenate([jnp.where(first, pair, zero), jnp.where(first, zero, pair)], axis=0)


def _kv_pair(ref, pair, rep):
    if rep == 1:
        return ref[:, _pair_cols(pair)]
    blk = ref[...].astype(F32)
    other = pltpu.roll(blk, HEAD_DIM, 1)
    first = _first_head_lanes(blk.shape)
    both = jnp.where(first, blk, other) if (2 * pair // rep) % 2 == 0 else jnp.where(first, other, blk)
    return both.astype(ref.dtype)


def _paired_kv(prev_ref, cur_ref, rep, transposed=False):
    memo = {}

    def get(pair):
        key = pair if rep == 1 else 2 * pair // rep
        if key not in memo:
            blocks = [_kv_pair(ref, pair, rep) for ref in (prev_ref, cur_ref)]
            memo[key] = jnp.concatenate([b.T for b in blocks], axis=1) if transposed else jnp.concatenate(blocks, axis=0)
        return memo[key]

    return get


def _attn_fwd(proj, bias, sinks, lay, name, exchange=None):
    L = proj.shape[0]
    nb = L // BLOCK
    kw, rep = lay.kw, lay.rep
    use_sinks = sinks is not None
    scale = HEAD_DIM ** -0.5
    ride = _Ride(exchange, 7 if use_sinks else 6, 2, 2)

    def kern(*refs):
        ins, (o_ref, l_ref), (sc_ref, pr_ref), exrefs = ride.split(refs)
        q_ref, kc_ref, kp_ref, vc_ref, vp_ref, b_ref = ins[:6]
        s_ref = ins[6] if use_sinks else None
        r, i = pl.program_id(0), pl.program_id(1)
        first = i == 0
        ride.around(r * nb + i, lay.dil * nb, exrefs,
                    lambda: compute(q_ref, kc_ref, kp_ref, vc_ref, vp_ref, b_ref, s_ref, o_ref, l_ref, first,
                                    sc_ref, pr_ref))

    def compute(q_ref, kc_ref, kp_ref, vc_ref, vp_ref, b_ref, s_ref, o_ref, l_ref, first, sc_ref, pr_ref):
        keys, values_t = _paired_kv(kp_ref, kc_ref, rep), _paired_kv(vp_ref, vc_ref, rep, transposed=True)
        for pair in range(PAIRS):
            qs = _split_heads(q_ref[:, _pair_cols(pair)])
            s_prev = _dot(keys(pair)[:BLOCK], qs, NT) * scale + b_ref[pair, :BLOCK]
            sc_ref[pair, :BLOCK] = jnp.where(first, NEG_INF, s_prev)
            sc_ref[pair, BLOCK:] = _dot(keys(pair)[BLOCK:], qs, NT) * scale + b_ref[pair, BLOCK:]
        inv = []
        for h in range(N_HEADS):
            cols = slice(h % 2 * BLOCK, (h % 2 + 1) * BLOCK)
            s = sc_ref[h // 2, :, cols]
            m = jnp.max(s, axis=0, keepdims=True)
            if use_sinks:
                sink = s_ref[:, h:h + 1]
                m = jnp.maximum(m, sink)
            p = jnp.exp(s - m)
            denom = jnp.sum(p, axis=0, keepdims=True)
            if use_sinks:
                denom = denom + jnp.exp(sink - m)
            pr_ref[h // 2, :, cols] = p.astype(BF16)
            l_ref[h:h + 1, :] = m + jnp.log(denom)
            inv.append(1.0 / denom)
        for pair in range(PAIRS):
            both = _dot(values_t(pair), pr_ref[pair], NN)
            o_t = jnp.concatenate([both[:HEAD_DIM, :BLOCK] * inv[2 * pair], both[HEAD_DIM:, BLOCK:] * inv[2 * pair + 1]],
                                  axis=0)
            o_ref[:, _pair_cols(pair)] = o_t.T

    prev = lambda i: jnp.maximum(i - 1, 0)
    in_specs = [
        pl.BlockSpec((BLOCK, QW), lambda r, i: (i, lay.q_col(r))),
        pl.BlockSpec((BLOCK, kw), lambda r, i: (i, lay.k_col(r))),
        pl.BlockSpec((BLOCK, kw), lambda r, i: (prev(i), lay.k_col(r))),
        pl.BlockSpec((BLOCK, kw), lambda r, i: (i, lay.v_col(r))),
        pl.BlockSpec((BLOCK, kw), lambda r, i: (prev(i), lay.v_col(r))),
        pl.BlockSpec((PAIRS, 2 * BLOCK, 2 * BLOCK), lambda r, i: (0, 0, 0)),
    ]
    args = [proj, proj, proj, proj, proj, bias]
    if use_sinks:
        in_specs.append(pl.BlockSpec((1, N_HEADS), lambda r, i: (0, 0)))
        args.append(sinks)
    out_specs = [pl.BlockSpec((BLOCK, QW), lambda r, i: (i, r)),
                 pl.BlockSpec((None, N_HEADS, BLOCK), lambda r, i: (r, 0, i))]
    out_shape = [jax.ShapeDtypeStruct((L, lay.dil * QW), F32), jax.ShapeDtypeStruct((lay.dil, N_HEADS, L), F32)]
    return pl.pallas_call(
        kern, name=name, grid=(lay.dil, nb),
        in_specs=in_specs + ride.in_specs, out_specs=out_specs + ride.out_specs,
        out_shape=out_shape + ride.out_shapes,
        scratch_shapes=[pltpu.VMEM((PAIRS, 2 * BLOCK, 2 * BLOCK), dt) for dt in (F32, BF16)] + ride.scratch,
        input_output_aliases=ride.aliases,
        compiler_params=_params("arbitrary", "arbitrary"),
    )(*args, *ride.args)


def _attn_bwd(proj, do, lse, dd, bias, sinks, lay, name, exchange=None):
    L = proj.shape[0]
    nb = L // BLOCK
    kw, rep = lay.kw, lay.rep
    assert rep == 1 or lay.kv_heads == 2, "grouped queries: the two kv heads fill one 128-lane block"
    use_sinks = sinks is not None
    scale = HEAD_DIM ** -0.5
    ride = _Ride(exchange, 10 if use_sinks else 9, 4 if use_sinks else 3, 6)

    def kern(*refs):
        ins, outs, (ck_ref, cv_ref, *staged), exrefs = ride.split(refs)
        q_ref, kc_ref, kp_ref, vc_ref, vp_ref, do_ref, l_ref, d_ref, b_ref = ins[:9]
        s_ref = ins[9] if use_sinks else None
        dq_ref, dk_ref, dv_ref = outs[:3]
        ds_ref = outs[3] if use_sinks else None
        r = pl.program_id(0)
        i = pl.program_id(1)
        ride.around(r * (nb + 1) + i, lay.dil * (nb + 1), exrefs,
                    lambda: compute(q_ref, kc_ref, kp_ref, vc_ref, vp_ref, do_ref, l_ref, d_ref, b_ref, s_ref,
                                    dq_ref, dk_ref, dv_ref, ds_ref, ck_ref, cv_ref, r, i, *staged))

    def compute(q_ref, kc_ref, kp_ref, vc_ref, vp_ref, do_ref, l_ref, d_ref, b_ref, s_ref,
                dq_ref, dk_ref, dv_ref, ds_ref, ck_ref, cv_ref, r, i, sc_ref, dp_ref, pr_ref, dsc_ref):
        first = i == 0

        @pl.when(first)
        def _():
            ck_ref[...] = jnp.zeros_like(ck_ref)
            cv_ref[...] = jnp.zeros_like(cv_ref)

        if use_sinks:
            @pl.when(first & (r == 0))
            def _():
                ds_ref[...] = jnp.zeros_like(ds_ref)

        @pl.when(i < nb)
        def _():
            keys, values = _paired_kv(kp_ref, kc_ref, rep), _paired_kv(vp_ref, vc_ref, rep)
            keys_t = _paired_kv(kp_ref, kc_ref, rep, transposed=True)
            for pair in range(PAIRS):
                qs = _split_heads(q_ref[:, _pair_cols(pair)])
                dos = _split_heads(do_ref[:, _pair_cols(pair)].astype(BF16))
                s = _dot(keys(pair), qs, NT) * scale + b_ref[pair]
                sc_ref[pair, :BLOCK] = jnp.where(first, NEG_INF, s[:BLOCK])
                sc_ref[pair, BLOCK:] = s[BLOCK:]
                dp_ref[pair] = _dot(values(pair), dos, NT)
            for h in range(N_HEADS):
                cols = slice(h % 2 * BLOCK, (h % 2 + 1) * BLOCK)
                lrow = l_ref[h:h + 1, :]
                drow = d_ref[h:h + 1, :]
                p = jnp.exp(sc_ref[h // 2, :, cols] - lrow)
                pr_ref[h // 2, :, cols] = p.astype(BF16)
                dsc_ref[h // 2, :, cols] = (p * (dp_ref[h // 2, :, cols] - drow) * scale).astype(BF16)
                if use_sinks:
                    ds_ref[h:h + 1, :] += -(jnp.exp(s_ref[:, h:h + 1] - lrow) * drow)
            grouped = {}
            for pair in range(PAIRS):
                cols = _pair_cols(pair)
                qs = _split_heads(q_ref[:, cols])
                dos = _split_heads(do_ref[:, cols].astype(BF16))
                ds = dsc_ref[pair]
                both = _dot(keys_t(pair), ds, NN)
                dq_t = jnp.concatenate([both[:HEAD_DIM, :BLOCK], both[HEAD_DIM:, BLOCK:]], axis=0)
                dq_ref[:, cols] = dq_t.T.astype(dq_ref.dtype)
                dk = _dot(ds, qs, NN)
                dv = _dot(pr_ref[pair], dos, NN)
                if rep == 1:
                    dk_ref[:, cols] = (ck_ref[:, cols] + dk[:BLOCK]).astype(dk_ref.dtype)
                    dv_ref[:, cols] = (cv_ref[:, cols] + dv[:BLOCK]).astype(dv_ref.dtype)
                    ck_ref[:, cols] = dk[BLOCK:]
                    cv_ref[:, cols] = dv[BLOCK:]
                else:
                    g = 2 * pair // rep
                    grouped[g] = (dk, dv) if g not in grouped else (grouped[g][0] + dk, grouped[g][1] + dv)
            if rep > 1:
                fold = lambda t: t + pltpu.roll(t, HEAD_DIM, 1)
                first_half = _first_head_lanes((2 * BLOCK, LANES))
                dk = jnp.where(first_half, fold(grouped[0][0]), fold(grouped[1][0]))
                dv = jnp.where(first_half, fold(grouped[0][1]), fold(grouped[1][1]))
                dk_ref[...] = (ck_ref[...] + dk[:BLOCK]).astype(dk_ref.dtype)
                dv_ref[...] = (cv_ref[...] + dv[:BLOCK]).astype(dv_ref.dtype)
                ck_ref[...] = dk[BLOCK:]
                cv_ref[...] = dv[BLOCK:]

        @pl.when(i == nb)
        def _():
            dk_ref[...] = ck_ref[...].astype(dk_ref.dtype)
            dv_ref[...] = cv_ref[...].astype(dv_ref.dtype)
            if use_sinks:
                @pl.when(r == lay.dil - 1)
                def _():
                    ds_ref[...] = jnp.broadcast_to(jnp.sum(ds_ref[...], axis=1, keepdims=True), ds_ref.shape)

    cur = lambda i: jnp.minimum(i, nb - 1)
    prev = lambda i: jnp.maximum(jnp.minimum(i, nb - 1) - 1, 0)
    done = lambda i: jnp.maximum(i - 1, 0)
    qspec = lambda col: pl.BlockSpec((BLOCK, QW), lambda r, i: (cur(i), col(r)))
    per_head = pl.BlockSpec((None, N_HEADS, BLOCK), lambda r, i: (r, 0, cur(i)))
    in_specs = [
        qspec(lay.q_col),
        pl.BlockSpec((BLOCK, kw), lambda r, i: (cur(i), lay.k_col(r))),
        pl.BlockSpec((BLOCK, kw), lambda r, i: (prev(i), lay.k_col(r))),
        pl.BlockSpec((BLOCK, kw), lambda r, i: (cur(i), lay.v_col(r))),
        pl.BlockSpec((BLOCK, kw), lambda r, i: (prev(i), lay.v_col(r))),
        qspec(lambda r: r), per_head, per_head,
        pl.BlockSpec((PAIRS, 2 * BLOCK, 2 * BLOCK), lambda r, i: (0, 0, 0)),
    ]
    args = [proj, proj, proj, proj, proj, do, lse, dd, bias]
    out_specs = [
        qspec(lambda r: r),
        pl.BlockSpec((BLOCK, kw), lambda r, i: (done(i), r)),
        pl.BlockSpec((BLOCK, kw), lambda r, i: (done(i), r)),
    ]
    dkv_shape = jax.ShapeDtypeStruct((L, lay.dil * kw), BF16)
    out_shape = [jax.ShapeDtypeStruct((L, lay.dil * QW), BF16), dkv_shape, dkv_shape]
    if use_sinks:
        in_specs.append(pl.BlockSpec((1, N_HEADS), lambda r, i: (0, 0)))
        args.append(sinks)
        out_specs.append(pl.BlockSpec((N_HEADS, LANES), lambda r, i: (0, 0)))
        out_shape.append(jax.ShapeDtypeStruct((N_HEADS, LANES), F32))
    return pl.pallas_call(
        kern, name=name, grid=(lay.dil, nb + 1),
        in_specs=in_specs + ride.in_specs, out_specs=out_specs + ride.out_specs,
        out_shape=out_shape + ride.out_shapes,
        scratch_shapes=[pltpu.VMEM((BLOCK, kw), F32), pltpu.VMEM((BLOCK, kw), F32)]
        + [pltpu.VMEM((PAIRS, 2 * BLOCK, 2 * BLOCK), dt) for dt in (F32, F32, BF16, BF16)] + ride.scratch,
        input_output_aliases=ride.aliases,
        compiler_params=_params("arbitrary", "arbitrary"),
    )(*args, *ride.args)


def _assemble(groups, name, dils=(1,)):
    T = groups[0][0].shape[0] * dils[0]
    widths = [g[0].shape[1] // dils[0] for g in groups]
    total = sum(widths)
    flat = [a for g in groups for a in g]
    member_dils = [d for g in groups for d in dils[:len(g)]]

    def kern(*refs):
        ins = refs[:len(flat)]
        out_ref, cs_ref = refs[len(flat):]

        @pl.when(pl.program_id(0) == 0)
        def _():
            cs_ref[...] = jnp.zeros_like(cs_ref)

        pos = off = 0
        for g, w in zip(groups, widths):
            acc = _to_token_order(ins[pos], dils[0])
            for j in range(1, len(g)):
                acc = acc + _to_token_order(ins[pos + j], dils[j])
            pos += len(g)
            out_ref[:, off:off + w] = acc.astype(BF16)
            cs_ref[:, off:off + w] += jnp.sum(acc, axis=0, keepdims=True)
            off += w

    return pl.pallas_call(
        kern, name=name, grid=(T // LEAN_ROWS,),
        in_specs=[_view_spec(LEAN_ROWS, a.shape[1] // d, d) for a, d in zip(flat, member_dils)],
        out_specs=[pl.BlockSpec((LEAN_ROWS, total), lambda i: (i, 0)), pl.BlockSpec((1, total), lambda i: (0, 0))],
        out_shape=[jax.ShapeDtypeStruct((T, total), BF16), jax.ShapeDtypeStruct((1, total), F32)],
        compiler_params=_params("arbitrary"),
    )(*flat)


def _adamw(w, g, m, v, name):
    _, R, C = w.shape
    rows = min(R, ROWS)
    assert R % rows == 0

    def kern(w_ref, g_ref, m_ref, v_ref, d_ref, nm_ref, nv_ref):
        gv = g_ref[...]
        mn = ADAM_B1 * m_ref[...] + (1.0 - ADAM_B1) * gv
        vn = ADAM_B2 * v_ref[...] + (1.0 - ADAM_B2) * jnp.square(gv)
        m_hat = mn / (1.0 - ADAM_B1 ** ADAM_STEP)
        v_hat = vn / (1.0 - ADAM_B2 ** ADAM_STEP)
        d_ref[...] = -ADAM_LR * (m_hat / (jnp.sqrt(v_hat) + ADAM_EPS) + ADAM_WD * w_ref[...])
        nm_ref[...] = mn
        nv_ref[...] = vn

    blk = pl.BlockSpec((None, rows, C), lambda i: (0, i, 0))
    shp = jax.ShapeDtypeStruct((1, R, C), F32)
    return pl.pallas_call(
        kern, name=name, grid=(R // rows,),
        in_specs=[blk, pl.BlockSpec((rows, C), lambda i: (i, 0)), blk, blk], out_specs=[blk] * 3, out_shape=[shp] * 3,
        compiler_params=_params("parallel"),
    )(w, g, m, v)


def _sum_slots(slots, name):
    n, R, C = slots.shape
    SUM_ROWS = next(rows for rows in (256, 128, 64, 32, 16) if R % rows == 0)

    def kern(s_ref, o_ref):
        acc = s_ref[0].astype(F32)
        for k in range(1, n):
            acc = acc + s_ref[k].astype(F32)
        o_ref[...] = acc

    return pl.pallas_call(
        kern, name=name, grid=(R // SUM_ROWS,),
        in_specs=[pl.BlockSpec((n, SUM_ROWS, C), lambda i: (0, i, 0))],
        out_specs=pl.BlockSpec((SUM_ROWS, C), lambda i: (i, 0)),
        out_shape=jax.ShapeDtypeStruct((R, C), F32),
        compiler_params=_params("parallel"),
    )(slots)


def _place():
    return lax.axis_index("x"), lax.axis_index("y"), lax.axis_index("c")


def _index(p):
    return 4 * p[0] + 2 * p[1] + p[2]


FLIPS = [(fx, fy, fc) for fx in (0, 1) for fy in (0, 1) for fc in (0, 1)][1:]


def _peer(me, flip):
    return tuple(1 - a if f else a for a, f in zip(me, flip))


def _gather_rows(shards, part=(0, 1), into=None, relay_at=RELAY_AT):
    nw = len(shards)

    def plan(ins, outs, send_sems, recv_sems):
        x, y, c = me = _place()
        sibling = (x, y, 1 - c)
        chips = [(1 - x, y), (x, 1 - y), (1 - x, 1 - y)]

        def span(w):
            cnt = ins[w].shape[0] // part[1]
            return part[0] * cnt, cnt

        def rows(w, p):
            lo, cnt = span(w)
            return outs[w].at[pl.ds(_index(p) * ins[w].shape[0] + lo, cnt), :]

        def own(w):
            lo, cnt = span(w)
            return ins[w].at[pl.ds(lo, cnt), :]

        def copy(w, k, block, to):
            return pltpu.make_async_remote_copy(
                src_ref=own(w) if block is me else rows(w, block), dst_ref=rows(w, block),
                send_sem=send_sems.at[7 * w + k], recv_sem=recv_sems.at[7 * w + k],
                device_id=to, device_id_type=MESH)

        return me, sibling, chips, c, rows, own, copy

    def copies(ins, outs, send_sems, recv_sems, local_sems):
        me, sibling, chips, c, rows, own, copy = plan(ins, outs, send_sems, recv_sems)
        local = [pltpu.make_async_copy(own(w), rows(w, me), local_sems.at[w]) for w in range(nw)]
        sends, recvs = [], []
        for w in range(nw):
            sends.append(copy(w, 0, me, sibling))
            sends += [copy(w, 1 + j, me, (*chip, c)) for j, chip in enumerate(chips)]
            recvs.append(copy(w, 0, sibling, me))
            recvs += [copy(w, 4 + j, (*chip, 1 - c), me) for j, chip in enumerate(chips)]
        return local, sends, recvs

    def relay(ins, outs, send_sems, recv_sems, local_sems):
        me, sibling, chips, c, rows, own, copy = plan(ins, outs, send_sems, recv_sems)
        arrived = [copy(w, 1 + j, (*chip, c), me) for w in range(nw) for j, chip in enumerate(chips)]
        onward = [copy(w, 4 + j, (*chip, c), sibling) for w in range(nw) for j, chip in enumerate(chips)]
        return arrived, onward

    shapes = [jax.ShapeDtypeStruct((N_DEV * s.shape[0], s.shape[1]), s.dtype) for s in shards]
    aliases = {nw + w: w for w in range(nw)} if into else None
    return _Exchange(shards + (into or []), shapes, 7 * nw, nw, copies, aliases=aliases, relay=relay,
                     relay_at=relay_at)


def _scatter_rows(parts, part=(0, 1)):
    nw = len(parts)

    def copies(ins, outs, send_sems, recv_sems, local_sems):
        me = _place()

        def src(w, owner):
            n = ins[w].shape[0] // N_DEV
            cnt = n // part[1]
            return ins[w].at[pl.ds(_index(owner) * n + part[0] * cnt, cnt), :]

        def copy(k, w, owner, sender, to):
            return pltpu.make_async_remote_copy(
                src_ref=src(w, owner), dst_ref=outs[w].at[_index(sender)],
                send_sem=send_sems.at[nw * k + w], recv_sem=recv_sems.at[nw * k + w],
                device_id=to, device_id_type=MESH)

        local = [pltpu.make_async_copy(src(w, me), outs[w].at[_index(me)], local_sems.at[w]) for w in range(nw)]
        peers = [_peer(me, flip) for flip in FLIPS]
        sends = [copy(k, w, peer, me, peer) for k, peer in enumerate(peers) for w in range(nw)]
        recvs = [copy(k, w, me, peer, me) for k, peer in enumerate(peers) for w in range(nw)]
        return local, sends, recvs

    shapes = [jax.ShapeDtypeStruct((N_DEV, p.shape[0] // N_DEV // part[1], p.shape[1]), p.dtype) for p in parts]
    return _Exchange(parts, shapes, 7 * nw, nw, copies)


def _sum_over_devices(v):
    shape = v.shape

    def body(v_ref, sum_ref, all_ref, send_sems, recv_sems):
        me = _place()
        all_ref[_index(me)] = v_ref[...]
        sends = []
        for k, flip in enumerate(FLIPS):
            peer = _peer(me, flip)
            sends.append(pltpu.make_async_remote_copy(
                src_ref=v_ref, dst_ref=all_ref.at[_index(me)],
                send_sem=send_sems.at[k], recv_sem=recv_sems.at[k], device_id=peer, device_id_type=MESH))
            sends[-1].start()
        for k, flip in enumerate(FLIPS):
            peer = _peer(me, flip)
            pltpu.make_async_remote_copy(
                src_ref=v_ref, dst_ref=all_ref.at[_index(peer)],
                send_sem=send_sems.at[k], recv_sem=recv_sems.at[k], device_id=peer, device_id_type=MESH).wait_recv()
        for cp in sends:
            cp.wait_send()
        acc = all_ref[0]
        for s in range(1, N_DEV):
            acc = acc + all_ref[s]
        sum_ref[...] = acc

    vmem = pl.BlockSpec(memory_space=pltpu.VMEM)
    return pl.pallas_call(
        body, name="sum_small_grads",
        in_specs=[vmem], out_specs=[vmem, vmem],
        out_shape=[jax.ShapeDtypeStruct(shape, F32), jax.ShapeDtypeStruct((N_DEV,) + shape, F32)],
        scratch_shapes=[pltpu.SemaphoreType.DMA((7,)), pltpu.SemaphoreType.DMA((7,))],
    )(v)[0]


SMALL_ROWS = 8


def _pack_small(vectors):
    padded = []
    for vec in vectors:
        vec = vec.reshape(-1)
        padded.append(jnp.pad(vec, (0, -vec.shape[0] % 128)))
    flat = jnp.concatenate(padded)
    flat = jnp.pad(flat, (0, -flat.shape[0] % (SMALL_ROWS * 128)))
    return flat.reshape(SMALL_ROWS, -1)


def _unpack_small(packed, shapes):
    flat = packed.reshape(-1)
    out, off = [], 0
    for shp in shapes:
        n = int(np.prod(shp))
        out.append(flat[off:off + n].reshape(shp))
        off += n + (-n % 128)
    return out


def kernel(x, g_attn, w_in, b_in, sinks_a, g_out_a, g_out_b, w_out, g_mlp, w_1, w_2, g_final, loss_target, m_g_attn, m_w_in, m_b_in, m_sinks_a, m_g_out_a, m_g_out_b, m_w_out, m_g_mlp, m_w_1, m_w_2, m_g_final, v_g_attn, v_w_in, v_b_in, v_sinks_a, v_g_out_a, v_g_out_b, v_w_out, v_g_mlp, v_w_1, v_w_2, v_g_final):
    xs, tgt = x[0], loss_target[0]
    T, D = xs.shape
    n_a = QW + 2 * KV_HEADS_A * HEAD_DIM
    g_fin = g_final.reshape(1, D)

    shards = [w_in[0].T.astype(BF16), w_out[0].astype(BF16), w_1[0].T.astype(BF16), w_2[0].astype(BF16)]
    ident = lambda acc: (acc,)
    add = lambda acc, other: (acc + other,)

    h1, w_in_t = _norm_fwd(xs, g_attn, "norm_attn", exchange=_gather_rows(shards[:1], relay_at=1.0))
    n_in = w_in_t.shape[0]
    proj_a, = _proj_views(h1, w_in_t, b_in, (0, n_a), [1], "proj_a")
    dils = [dil for _, dil in DILATED_BRANCHES]
    *proj_b, w_o = _proj_views(h1, w_in_t, b_in, (n_a, n_in - n_a), dils, "proj_b", exchange=_gather_rows(shards[1:2]))

    lay_a = _AttnLayout(1, KV_HEADS_A, 0, 0, 0, QW // (KV_HEADS_A * HEAD_DIM), QW // (KV_HEADS_A * HEAD_DIM) + 1)
    bias_a = _band_bias(WINDOW_A - 1, 1)
    o_a, l_a, w_1_t = _attn_fwd(proj_a, bias_a, sinks_a, lay_a, "attn_a_fwd",
                                exchange=_gather_rows(shards[2:3], part=(0, 4)))
    branches = []
    for n, (window, dil) in enumerate(DILATED_BRANCHES):
        lay = _AttnLayout(dil, N_HEADS, 3, 0, 3, 1, 2)
        bias = _band_bias(window // dil, dil)
        ride = _gather_rows(shards[2:3], part=(n + 1, 4), into=[w_1_t])
        o, lse, w_1_t = _attn_fwd(proj_b[n], bias, None, lay, f"attn_b{dil}_fwd", exchange=ride)
        branches.append((lay, bias, proj_b[n], o, lse))
    o_b = [br[3] for br in branches]
    l_b = [br[4].transpose(2, 0, 1).reshape(T, N_HEADS) for br in branches]

    mix = _mix_fwd(o_a, o_b, l_b, g_out_a, g_out_b, dils)
    def residual_and_norm(acc, res, g):
        x_new = acc + res
        return x_new, (x_new * _rstd(x_new)) * g

    assert TILE_WHOLE_ROWS["tn"] == D
    x2, h2 = _matmul(mix, w_o, "nn", [F32, BF16], residual_and_norm, tk=D, tile_ins=[xs], row_ins=[g_mlp],
                     name="out_proj", **TILE_WHOLE_ROWS)

    def relu_sq(acc):
        u = jnp.maximum(acc, 0.0)
        return u, u * u

    u, u_sq, w_2_f = _matmul(h2, w_1_t, "nt", [BF16, BF16], relu_sq, tk=D, name="mlp_up",
                             exchange=_gather_rows(shards[3:]), **TILE_WHOLE_ROWS)
    x3, = _matmul(u_sq, w_2_f, "nn", [F32], add, tk=4096, tile_ins=[x2], name="mlp_down", **TILE)

    dx3, dx3_b, dg_final, loss_dev = _loss_head(x3, tgt, g_fin)

    d_pre, = _matmul(dx3_b, w_2_f, "nt", [BF16], lambda acc, uu: (acc * (2.0 * uu.astype(F32)),),
                     tk=D, tile_ins=[u], name="mlp_down_bwd", **TILE_WHOLE_ROWS)
    dw_2, = _matmul(u_sq, dx3_b, "tn", [BF16], ident, name="mlp_down_wgrad", **TILE_WGRAD)
    dh2, slots_2a = _matmul(d_pre, w_1_t, "nn", [BF16], ident, tk=4096, name="mlp_up_bwd",
                            exchange=_scatter_rows([dw_2], part=(0, 2)), **TILE)
    dw_1_t, slots_2b = _matmul(d_pre, h2, "tn", [BF16], ident, name="mlp_up_wgrad",
                               exchange=_scatter_rows([dw_2], part=(1, 2)), **TILE_WGRAD)
    dx2, dg_mlp, dx2_b, dmix = _norm_bwd(dh2, x2, g_mlp, dx3, "norm_mlp_bwd", then_w_t=w_o)
    dw_o, = _matmul(mix, dx2_b, "tn", [BF16], ident, name="out_proj_wgrad", **TILE_WGRAD)
    do_a, dd_a, do1, do2, do3, dd1, dd2, dd3, dg_out_a, dg_out_b, slots_o = _mix_bwd(
        dmix, o_a, o_b, l_b, g_out_a, g_out_b, dils, exchange=_scatter_rows([dw_o]))

    by_class = lambda d, dil: d.reshape(T // dil, dil, N_HEADS).transpose(1, 2, 0)
    slots_1 = [None] * 4
    dq_a, dk_a, dv_a, dsinks, slots_1[0] = _attn_bwd(proj_a, do_a, l_a, by_class(dd_a, 1), bias_a, sinks_a, lay_a,
                                                     "attn_a_bwd", exchange=_scatter_rows([dw_1_t], part=(0, 4)))
    dsinks = dsinks[:, 0].reshape(1, N_HEADS)
    dqs, dks, dvs = [], [], []
    for n, ((lay, bias, view, _, lse), do_n, dd_n) in enumerate(zip(branches, (do1, do2, do3), (dd1, dd2, dd3))):
        dq, dk, dv, slots_1[n + 1] = _attn_bwd(view, do_n, lse, by_class(dd_n, lay.dil), bias, None, lay,
                                               f"attn_b{lay.dil}_bwd",
                                               exchange=_scatter_rows([dw_1_t], part=(n + 1, 4)))
        dqs.append(dq)
        dks.append(dk)
        dvs.append(dv)
    dproj, db_in = _assemble([[dq_a], [dk_a], [dv_a], dqs, dks, dvs], "dproj", dils)

    dw_in_t, = _matmul(dproj, h1, "tn", [BF16], ident, tm=n_in // 2, tn=1024, tk=1024, name="in_proj_wgrad")
    dh1, slots_in = _matmul(dproj, w_in_t, "nn", [BF16], ident, tk=n_in, name="in_proj_bwd",
                            exchange=_scatter_rows([dw_in_t]), **TILE)
    dx, dg_attn = _norm_bwd(dh1, xs, g_attn, dx2, "norm_attn_bwd")

    g_w_in = _sum_slots(slots_in, "sum_w_in_grads").T
    g_w_out = _sum_slots(slots_o, "sum_w_out_grads")
    g_w_1 = jnp.concatenate([_sum_slots(s, f"sum_w_1_grads_{n}") for n, s in enumerate(slots_1)]).T
    g_w_2 = jnp.concatenate([_sum_slots(slots_2a, "sum_w_2_grads_0"), _sum_slots(slots_2b, "sum_w_2_grads_1")])

    small_w = [g_attn, b_in, sinks_a, g_out_a, g_out_b, g_mlp, g_final]
    small_m = [m_g_attn, m_b_in, m_sinks_a, m_g_out_a, m_g_out_b, m_g_mlp, m_g_final]
    small_v = [v_g_attn, v_b_in, v_sinks_a, v_g_out_a, v_g_out_b, v_g_mlp, v_g_final]
    small_g = [dg_attn, db_in, dsinks, dg_out_a, dg_out_b, dg_mlp, dg_final]
    summed = _sum_over_devices(_pack_small(small_g + [loss_dev[:, :1]]))
    shapes = [w.shape for w in small_w]
    *g_small, loss = _unpack_small(summed, shapes + [()])

    big = [
        _adamw(w_in, g_w_in, m_w_in, v_w_in, "adamw_w_in"),
        _adamw(w_out, g_w_out, m_w_out, v_w_out, "adamw_w_out"),
        _adamw(w_1, g_w_1, m_w_1, v_w_1, "adamw_w_1"),
        _adamw(w_2, g_w_2, m_w_2, v_w_2, "adamw_w_2"),
    ]
    g_packed = _pack_small(g_small)
    small = _adamw(_pack_small(small_w)[None], g_packed, _pack_small(small_m)[None], _pack_small(small_v)[None],
                   "adamw_small")
    small = [_unpack_small(s, shapes) for s in small]

    def ordered(small_list, big_list):
        s = list(small_list)
        return [s[0], big_list[0], s[1], s[2], s[3], s[4], big_list[1], s[5], big_list[2], big_list[3], s[6]]

    grads = ordered(g_small, [g[None] for g in (g_w_in, g_w_out, g_w_1, g_w_2)])
    deltas = ordered(small[0], [b[0] for b in big])
    new_m = ordered(small[1], [b[1] for b in big])
    new_v = ordered(small[2], [b[2] for b in big])
    return (loss, dx[None], *grads, *deltas, *new_m, *new_v)
```

```python
import numpy as np
import jax
import jax.numpy as jnp
from jax import lax
from jax.experimental import pallas as pl
from jax.experimental.pallas import tpu as pltpu

F32 = jnp.float32
BF16 = jnp.bfloat16

HEAD_DIM = 64
N_HEADS = 16
KV_HEADS_A = 2
BLOCK = 128
WINDOW_A = 128
DILATED_BRANCHES = ((128, 1), (512, 4), (2048, 16))
EPS = 1e-5
NEG_INF = -1e30
N_DEV = 8

ADAM_LR = 0.001
ADAM_B1 = 0.9
ADAM_B2 = 0.999
ADAM_EPS = 1e-08
ADAM_WD = 0.01
ADAM_STEP = 10

VMEM_LIMIT_BYTES = 56 * 1024 * 1024
MESH = pl.DeviceIdType.MESH
ANY = pl.BlockSpec(memory_space=pl.ANY)

NN = (((1,), (0,)), ((), ()))
NT = (((1,), (1,)), ((), ()))
TN = (((0,), (0,)), ((), ()))


def _dot(a, b, dims):
    return lax.dot_general(a, b, dims, preferred_element_type=F32)


def _params(*sem):
    return pltpu.CompilerParams(dimension_semantics=sem, vmem_limit_bytes=VMEM_LIMIT_BYTES)


RELAY_AT = 0.6


class _Exchange:
    def __init__(self, ins, out_shapes, n_remote, n_local, copies, aliases=None, relay=None, relay_at=RELAY_AT):
        self.ins, self.out_shapes = list(ins), list(out_shapes)
        self.n_remote, self.n_local = n_remote, n_local
        self.copies = copies
        self.relay = relay
        self.relay_at = relay_at
        self.aliases = aliases or {}

    def start(self, refs):
        local, sends, _ = self.copies(*refs)
        for cp in local + sends:
            cp.start()

    def middle(self, refs):
        arrived, onward = self.relay(*refs)
        for got, cp in zip(arrived, onward):
            got.wait_recv()
            cp.start()

    def finish(self, refs):
        local, sends, recvs = self.copies(*refs)
        for cp in recvs:
            cp.wait_recv()
        for cp in sends:
            cp.wait_send()
        for cp in local:
            cp.wait()
        if self.relay:
            for cp in self.relay(*refs)[1]:
                cp.wait_send()


class _Ride:
    def __init__(self, ex, n_in, n_out, n_scratch):
        self.ex = ex
        self.n = (n_in, n_out, n_scratch)
        self.args = ex.ins if ex else []
        self.in_specs = [ANY] * len(self.args)
        self.out_shapes = ex.out_shapes if ex else []
        self.out_specs = [ANY] * len(self.out_shapes)
        self.scratch = [pltpu.SemaphoreType.DMA((ex.n_remote,)), pltpu.SemaphoreType.DMA((ex.n_remote,)),
                        pltpu.SemaphoreType.DMA((max(ex.n_local, 1),))] if ex else []
        self.aliases = {n_in + i: n_out + o for i, o in ex.aliases.items()} if ex else {}

    def split(self, refs):
        n_in, n_out, n_scratch = self.n
        a = n_in
        b = a + len(self.args)
        c = b + n_out
        d = c + len(self.out_shapes)
        e = d + n_scratch
        return refs[:a], refs[b:c], refs[d:e], (refs[a:b], refs[c:d], *refs[e:])

    def around(self, step, n_steps, exrefs, compute):
        if self.ex is None:
            compute()
            return

        @pl.when(step == 0)
        def _():
            self.ex.start(exrefs)

        compute()

        if self.ex.relay:
            @pl.when(step == int(self.ex.relay_at * (n_steps - 1)))
            def _():
                self.ex.middle(exrefs)

        @pl.when(step == n_steps - 1)
        def _():
            self.ex.finish(exrefs)


TILE = dict(tm=512, tn=1024)
TILE_WHOLE_ROWS = dict(tm=512, tn=2048)
TILE_WGRAD = dict(tm=1024, tn=1024, tk=4096)

def _matmul(a, b, dims, out_dtypes, epilogue, *, tm, tn, tk, name, tile_ins=(), row_ins=(), exchange=None):
    if dims == "tn":
        K, M = a.shape
    else:
        M, K = a.shape
    N = b.shape[0] if dims == "nt" else b.shape[1]
    tm, tn, tk = min(tm, M), min(tn, N), min(tk, K)
    assert M % tm == 0 and N % tn == 0 and K % tk == 0, (name, M, N, K, tm, tn, tk)
    grid = (M // tm, N // tn, K // tk)
    nk = grid[2]
    n_tile, n_row, n_out = len(tile_ins), len(row_ins), len(out_dtypes)
    dn = {"nn": NN, "nt": NT, "tn": TN}[dims]
    ride = _Ride(exchange, 2 + n_tile + n_row, n_out, 1 if nk > 1 else 0)

    def kern(*refs):
        ins, out_refs, scratch, exrefs = ride.split(refs)
        a_ref, b_ref = ins[:2]
        tile_refs = ins[2:2 + n_tile]
        row_refs = ins[2 + n_tile:]
        ids = [pl.program_id(d) for d in range(3)]

        def finish(acc):
            outs = epilogue(acc, *[r[...] for r in tile_refs], *[r[...] for r in row_refs])
            for o_ref, o in zip(out_refs, outs):
                o_ref[...] = o.astype(o_ref.dtype)

        def compute():
            if nk == 1:
                finish(_dot(a_ref[...], b_ref[...], dn))
                return
            acc_ref = scratch[0]

            @pl.when(ids[2] == 0)
            def _():
                acc_ref[...] = jnp.zeros_like(acc_ref)

            acc_ref[...] += _dot(a_ref[...], b_ref[...], dn)

            @pl.when(ids[2] == nk - 1)
            def _():
                finish(acc_ref[...])

        ride.around((ids[0] * grid[1] + ids[1]) * grid[2] + ids[2], grid[0] * grid[1] * grid[2], exrefs, compute)

    if dims == "tn":
        a_spec = pl.BlockSpec((tk, tm), lambda i, j, k: (k, i))
    else:
        a_spec = pl.BlockSpec((tm, tk), lambda i, j, k: (i, k))
    if dims == "nt":
        b_spec = pl.BlockSpec((tn, tk), lambda i, j, k: (j, k))
    else:
        b_spec = pl.BlockSpec((tk, tn), lambda i, j, k: (k, j))
    tile_spec = pl.BlockSpec((tm, tn), lambda i, j, k: (i, j))
    row_spec = pl.BlockSpec((1, tn), lambda i, j, k: (0, j))
    sem = ("arbitrary",) * 3 if exchange else ("parallel", "parallel", "arbitrary")
    return pl.pallas_call(
        kern,
        name=name,
        grid=grid,
        in_specs=[a_spec, b_spec] + [tile_spec] * n_tile + [row_spec] * n_row + ride.in_specs,
        out_specs=[tile_spec] * n_out + ride.out_specs,
        out_shape=[jax.ShapeDtypeStruct((M, N), dt) for dt in out_dtypes] + ride.out_shapes,
        scratch_shapes=([pltpu.VMEM((tm, tn), F32)] if nk > 1 else []) + ride.scratch,
        input_output_aliases=ride.aliases,
        compiler_params=_params(*sem),
    )(a, b, *tile_ins, *row_ins, *ride.args)


PROJ_ROWS = 256


def _proj_views(a, w_t, bias, cols, dils, name, exchange=None):
    T, K = a.shape
    first, N = cols
    ride = _Ride(exchange, 3, len(dils), 0)

    def kern(*refs):
        (a_ref, w_ref, b_ref), outs, _, exrefs = ride.split(refs)

        def compute():
            acc = _dot(a_ref[...], w_ref[...], NT) + b_ref[...]
            for out_ref, dil in zip(outs, dils):
                _to_class_order(acc, out_ref, dil)

        ride.around(pl.program_id(0), T // LEAN_ROWS, exrefs, compute)

    once = pl.Buffered(1)
    return pl.pallas_call(
        kern, name=name, grid=(T // LEAN_ROWS,),
        in_specs=[pl.BlockSpec((LEAN_ROWS, K), lambda i: (i, 0)),
                  pl.BlockSpec((pl.Element(N), pl.Element(K)), lambda i: (first, 0), pipeline_mode=once),
                  pl.BlockSpec((pl.Element(1), pl.Element(N)), lambda i: (0, first), pipeline_mode=once)] + ride.in_specs,
        out_specs=[_view_spec(LEAN_ROWS, N, d) for d in dils] + ride.out_specs,
        out_shape=[jax.ShapeDtypeStruct((T // d, d * N), BF16) for d in dils] + ride.out_shapes,
        scratch_shapes=ride.scratch,
        input_output_aliases=ride.aliases,
        compiler_params=_params("arbitrary"),
    )(a, w_t, bias, *ride.args)


ROWS = 256
LEAN_ROWS = 512
MIX_ROWS = 256


def _rstd(xv):
    return lax.rsqrt(jnp.mean(xv * xv, axis=-1, keepdims=True) + EPS)


def _norm_fwd(x, g, name, exchange=None):
    T, D = x.shape
    ride = _Ride(exchange, 2, 1, 0)

    def kern(*refs):
        (x_ref, g_ref), (h_ref,), _, exrefs = ride.split(refs)

        def compute():
            xv = x_ref[...]
            h_ref[...] = ((xv * _rstd(xv)) * g_ref[...]).astype(h_ref.dtype)

        ride.around(pl.program_id(0), T // ROWS, exrefs, compute)

    row = pl.BlockSpec((ROWS, D), lambda i: (i, 0))
    return pl.pallas_call(
        kern, name=name, grid=(T // ROWS,),
        in_specs=[row, pl.BlockSpec((1, D), lambda i: (0, 0))] + ride.in_specs,
        out_specs=[row] + ride.out_specs,
        out_shape=[jax.ShapeDtypeStruct((T, D), BF16)] + ride.out_shapes,
        scratch_shapes=ride.scratch, input_output_aliases=ride.aliases,
        compiler_params=_params("arbitrary"),
    )(x, g, *ride.args)


def _norm_bwd(dh, x, g, res, name, then_w_t=None):
    T, D = x.shape
    rows = PROJ_ROWS if then_w_t is not None else min(LEAN_ROWS, T)

    def kern(dh_ref, x_ref, g_ref, res_ref, *rest):
        if then_w_t is None:
            dx_ref, dg_ref = rest
        else:
            w_ref, dx_ref, dg_ref, dxb_ref, y_ref = rest

        @pl.when(pl.program_id(0) == 0)
        def _():
            dg_ref[...] = jnp.zeros_like(dg_ref)

        xv = x_ref[...]
        r = _rstd(xv)
        xn = xv * r
        dhv = dh_ref[...].astype(F32)
        dg_ref[...] += jnp.sum(dhv * xn, axis=0, keepdims=True)
        t = dhv * g_ref[...]
        dx = res_ref[...] + r * (t - xn * jnp.mean(t * xn, axis=-1, keepdims=True))
        dx_ref[...] = dx
        if then_w_t is not None:
            dxb = dx.astype(BF16)
            dxb_ref[...] = dxb
            y_ref[...] = _dot(dxb, w_ref[...], NT)

    row = pl.BlockSpec((rows, D), lambda i: (i, 0))
    vec = pl.BlockSpec((1, D), lambda i: (0, 0))
    in_specs, args = [row, row, vec, row], [dh, x, g, res]
    out_specs = [row, vec]
    out_shape = [jax.ShapeDtypeStruct((T, D), F32), jax.ShapeDtypeStruct((1, D), F32)]
    if then_w_t is not None:
        N = then_w_t.shape[0]
        in_specs.append(pl.BlockSpec((N, D), lambda i: (0, 0)))
        args.append(then_w_t)
        out_specs += [row, pl.BlockSpec((rows, N), lambda i: (i, 0))]
        out_shape += [jax.ShapeDtypeStruct((T, D), BF16), jax.ShapeDtypeStruct((T, N), F32)]
    return pl.pallas_call(
        kern, name=name, grid=(T // rows,), in_specs=in_specs, out_specs=out_specs, out_shape=out_shape,
        compiler_params=_params("arbitrary"),
    )(*args)


def _loss_head(x3, tgt, g):
    T, D = x3.shape

    def kern(x_ref, t_ref, g_ref, dx_ref, dxb_ref, dg_ref, loss_ref):
        @pl.when(pl.program_id(0) == 0)
        def _():
            dg_ref[...] = jnp.zeros_like(dg_ref)
            loss_ref[...] = jnp.zeros_like(loss_ref)

        xv = x_ref[...]
        gv = g_ref[...]
        r = _rstd(xv)
        xn = xv * r
        err = xn * gv - t_ref[...]
        per_tok = jnp.mean(err * err, axis=-1, keepdims=True)
        loss_ref[...] += 0.5 * jnp.sum(per_tok, axis=0, keepdims=True)
        dy = err * (1.0 / D)
        dg_ref[...] += jnp.sum(dy * xn, axis=0, keepdims=True)
        t = dy * gv
        dx = r * (t - xn * jnp.mean(t * xn, axis=-1, keepdims=True))
        dx_ref[...] = dx
        dxb_ref[...] = dx.astype(BF16)

    row = pl.BlockSpec((LEAN_ROWS, D), lambda i: (i, 0))
    vec = pl.BlockSpec((1, D), lambda i: (0, 0))
    return pl.pallas_call(
        kern, name="loss_head", grid=(T // LEAN_ROWS,),
        in_specs=[row, row, vec],
        out_specs=[row, row, vec, pl.BlockSpec((1, 128), lambda i: (0, 0))],
        out_shape=[jax.ShapeDtypeStruct((T, D), F32), jax.ShapeDtypeStruct((T, D), BF16),
                   jax.ShapeDtypeStruct((1, D), F32), jax.ShapeDtypeStruct((1, 128), F32)],
        compiler_params=_params("arbitrary"),
    )(x3, tgt, g)


def _spread_matrix():
    head_of_lane = np.arange(N_HEADS * HEAD_DIM) // HEAD_DIM
    return jnp.asarray(np.arange(N_HEADS)[:, None] == head_of_lane[None, :], dtype=BF16)


def _pieces(v, n):
    out = []
    for _ in range(n):
        piece = v.astype(BF16)
        out.append(piece)
        v = v - piece.astype(F32)
    return out


def _spread(v, spread):
    return sum(_dot(p, spread, NN) for p in _pieces(v, 2))


def _spread_weights(w1, w2, spread):
    s1, s2 = _spread(w1, spread), _spread(w2, spread)
    return s1, s2, 1.0 - s1 - s2


def _head_sums(v, spread):
    return sum(_dot(p, spread, NT) for p in _pieces(v, 2))


def _branch_weights(l1, l2, l3):
    lm = jnp.maximum(jnp.maximum(l1, l2), l3)
    e1, e2, e3 = jnp.exp(l1 - lm), jnp.exp(l2 - lm), jnp.exp(l3 - lm)
    inv = 1.0 / (e1 + e2 + e3)
    return e1 * inv, e2 * inv, e3 * inv


def _to_token_order(view_ref, dil):
    if dil == 1:
        return view_ref[...].astype(F32)
    n_l, w = view_ref.shape[0], view_ref.shape[1] // dil
    cols = []
    for cb in range(w // LANES):
        by_class = jnp.stack([view_ref[:, r * w + cb * LANES:r * w + (cb + 1) * LANES].astype(F32) for r in range(dil)])
        cols.append(jnp.swapaxes(by_class, 0, 1).reshape(n_l * dil, LANES))
    return jnp.concatenate(cols, axis=1)


def _to_class_order(val, view_ref, dil):
    if dil == 1:
        view_ref[...] = val.astype(view_ref.dtype)
        return
    n, w = val.shape
    for cb in range(w // LANES):
        by_class = jnp.swapaxes(val[:, cb * LANES:(cb + 1) * LANES].reshape(n // dil, dil, LANES), 0, 1)
        for r in range(dil):
            view_ref[:, r * w + cb * LANES:r * w + (cb + 1) * LANES] = by_class[r].astype(view_ref.dtype)


def _view_spec(rows, width, dil):
    return pl.BlockSpec((rows // dil, dil * width), lambda i: (i, 0))


def _mix_fwd(oa, obs, lbs, ga, gb, dils):
    T, W = oa.shape

    def kern(oa_ref, o1, o2, o3, l1, l2, l3, ga_ref, gb_ref, sp_ref, mix_ref):
        sp = sp_ref[...]
        w1, w2, w3 = _branch_weights(l1[...], l2[...], l3[...])
        on = [_to_token_order(o, d) for o, d in zip((o1, o2, o3), dils)]
        s1, s2, s3 = _spread_weights(w1, w2, sp)
        ob = s1 * on[0] + s2 * on[1] + s3 * on[2]
        oav = oa_ref[...]
        mix_ref[:, :W] = ((oav * _rstd(oav)) * ga_ref[...]).astype(BF16)
        mix_ref[:, W:] = ((ob * _rstd(ob)) * gb_ref[...]).astype(BF16)

    row = pl.BlockSpec((LEAN_ROWS, W), lambda i: (i, 0))
    per_head = pl.BlockSpec((LEAN_ROWS, N_HEADS), lambda i: (i, 0))
    vec = pl.BlockSpec((1, W), lambda i: (0, 0))
    return pl.pallas_call(
        kern, name="mix_fwd", grid=(T // LEAN_ROWS,),
        in_specs=[row] + [_view_spec(LEAN_ROWS, W, d) for d in dils] + [per_head] * 3
        + [vec, vec, pl.BlockSpec((N_HEADS, W), lambda i: (0, 0))],
        out_specs=pl.BlockSpec((LEAN_ROWS, 2 * W), lambda i: (i, 0)),
        out_shape=jax.ShapeDtypeStruct((T, 2 * W), BF16),
        compiler_params=_params("parallel"),
    )(oa, *obs, *lbs, ga, gb, _spread_matrix())


def _mix_bwd(dmix, oa, obs, lbs, ga, gb, dils, exchange=None):
    T, W = oa.shape
    ride = _Ride(exchange, 11, 10, 0)

    def kern(*refs):
        ins, outs, _, exrefs = ride.split(refs)
        ride.around(pl.program_id(0), T // MIX_ROWS, exrefs, lambda: compute(*ins, *outs))

    def compute(dm_ref, oa_ref, o1, o2, o3, l1, l2, l3, ga_ref, gb_ref, sp_ref,
                doa_ref, da_ref, do1, do2, do3, d1, d2, d3, dga_ref, dgb_ref):
        @pl.when(pl.program_id(0) == 0)
        def _():
            dga_ref[...] = jnp.zeros_like(dga_ref)
            dgb_ref[...] = jnp.zeros_like(dgb_ref)

        sp = sp_ref[...]
        oav = oa_ref[...]
        r = _rstd(oav)
        on = oav * r
        dy = dm_ref[:, :W]
        dga_ref[...] += jnp.sum(dy * on, axis=0, keepdims=True)
        t = dy * ga_ref[...]
        doa = r * (t - on * jnp.mean(t * on, axis=-1, keepdims=True))
        doa_ref[...] = doa.astype(BF16)
        da_ref[...] = _head_sums(doa * oav, sp)
        w1, w2, w3 = _branch_weights(l1[...], l2[...], l3[...])
        s1, s2, s3 = _spread_weights(w1, w2, sp)
        on = [_to_token_order(o, d) for o, d in zip((o1, o2, o3), dils)]
        ob = s1 * on[0] + s2 * on[1] + s3 * on[2]
        r = _rstd(ob)
        on = ob * r
        dy = dm_ref[:, W:]
        dgb_ref[...] += jnp.sum(dy * on, axis=0, keepdims=True)
        t = dy * gb_ref[...]
        dob = r * (t - on * jnp.mean(t * on, axis=-1, keepdims=True))
        c = _head_sums(dob * ob, sp)
        for do_ref, sn, d in zip((do1, do2, do3), (s1, s2, s3), dils):
            _to_class_order(sn * dob, do_ref, d)
        d1[...] = w1 * c
        d2[...] = w2 * c
        d3[...] = w3 * c

    row = pl.BlockSpec((MIX_ROWS, W), lambda i: (i, 0))
    per_head = pl.BlockSpec((MIX_ROWS, N_HEADS), lambda i: (i, 0))
    vec = pl.BlockSpec((1, W), lambda i: (0, 0))
    bf = jax.ShapeDtypeStruct((T, W), BF16)
    ph = jax.ShapeDtypeStruct((T, N_HEADS), F32)
    vv = jax.ShapeDtypeStruct((1, W), F32)
    views = [_view_spec(MIX_ROWS, W, d) for d in dils]
    return pl.pallas_call(
        kern, name="mix_bwd", grid=(T // MIX_ROWS,),
        in_specs=[pl.BlockSpec((MIX_ROWS, 2 * W), lambda i: (i, 0)), row] + views + [per_head] * 3 + [vec, vec,
                  pl.BlockSpec((N_HEADS, W), lambda i: (0, 0))] + ride.in_specs,
        out_specs=[row, per_head] + views + [per_head, per_head, per_head, vec, vec] + ride.out_specs,
        out_shape=[bf, ph] + [jax.ShapeDtypeStruct(o.shape, F32) for o in obs] + [ph, ph, ph, vv, vv]
        + ride.out_shapes,
        scratch_shapes=ride.scratch,
        input_output_aliases=ride.aliases,
        compiler_params=_params("arbitrary"),
    )(dmix, oa, *obs, *lbs, ga, gb, _spread_matrix(), *ride.args)


def _alibi_slopes(n):
    return np.asarray(2.0 ** (-8.0 * (np.arange(n) + 1) / n)).astype(np.float32)


def _band_bias(max_steps, step_dist):
    qi = np.arange(BLOCK)[None, :]
    kj = np.arange(BLOCK)[:, None]
    slopes = _alibi_slopes(N_HEADS)
    halves = []
    for steps in (qi + BLOCK - kj, qi - kj):
        valid = (steps >= 0) & (steps <= max_steps)
        alibi = slopes[:, None, None] * (step_dist * steps).astype(np.float32)[None]
        halves.append(np.where(valid[None], -alibi, np.float32(NEG_INF)).astype(np.float32))
    per_head = np.concatenate(halves, axis=1)
    return jnp.asarray(np.concatenate([per_head[0::2], per_head[1::2]], axis=2))


class _AttnLayout:
    def __init__(self, dil, kv_heads, q_stride, q_off, k_stride, k_off, v_off):
        self.dil = dil
        self.kv_heads = kv_heads
        self.kw = kv_heads * HEAD_DIM
        self.rep = N_HEADS // kv_heads
        self.q_col = lambda r: r * q_stride + q_off
        self.k_col = lambda r: r * k_stride + k_off
        self.v_col = lambda r: r * k_stride + v_off


QW = N_HEADS * HEAD_DIM
LANES = 128


PAIRS = N_HEADS // 2


def _pair_cols(pair):
    return slice(pair * LANES, (pair + 1) * LANES)


def _first_head_lanes(shape):
    return lax.broadcasted_iota(jnp.int32, shape, 1) < HEAD_DIM


def _split_heads(pair):
    first = _first_head_lanes(pair.shape)
    zero = jnp.zeros_like(pair)
    return jnp.concatenate([jnp.where(first, pair, zero), jnp.where(first, zero, pair)], axis=0)


def _kv_pair(ref, pair, rep):
    if rep == 1:
        return ref[:, _pair_cols(pair)]
    blk = ref[...].astype(F32)
    other = pltpu.roll(blk, HEAD_DIM, 1)
    first = _first_head_lanes(blk.shape)
    both = jnp.where(first, blk, other) if (2 * pair // rep) % 2 == 0 else jnp.where(first, other, blk)
    return both.astype(ref.dtype)


def _paired_kv(prev_ref, cur_ref, rep, transposed=False):
    memo = {}

    def get(pair):
        key = pair if rep == 1 else 2 * pair // rep
        if key not in memo:
            blocks = [_kv_pair(ref, pair, rep) for ref in (prev_ref, cur_ref)]
            memo[key] = jnp.concatenate([b.T for b in blocks], axis=1) if transposed else jnp.concatenate(blocks, axis=0)
        return memo[key]

    return get


def _attn_fwd(proj, bias, sinks, lay, name, exchange=None):
    L = proj.shape[0]
    nb = L // BLOCK
    kw, rep = lay.kw, lay.rep
    use_sinks = sinks is not None
    scale = HEAD_DIM ** -0.5
    ride = _Ride(exchange, 7 if use_sinks else 6, 2, 2)

    def kern(*refs):
        ins, (o_ref, l_ref), (sc_ref, pr_ref), exrefs = ride.split(refs)
        q_ref, kc_ref, kp_ref, vc_ref, vp_ref, b_ref = ins[:6]
        s_ref = ins[6] if use_sinks else None
        r, i = pl.program_id(0), pl.program_id(1)
        first = i == 0
        ride.around(r * nb + i, lay.dil * nb, exrefs,
                    lambda: compute(q_ref, kc_ref, kp_ref, vc_ref, vp_ref, b_ref, s_ref, o_ref, l_ref, first,
                                    sc_ref, pr_ref))

    def compute(q_ref, kc_ref, kp_ref, vc_ref, vp_ref, b_ref, s_ref, o_ref, l_ref, first, sc_ref, pr_ref):
        keys, values_t = _paired_kv(kp_ref, kc_ref, rep), _paired_kv(vp_ref, vc_ref, rep, transposed=True)
        for pair in range(PAIRS):
            qs = _split_heads(q_ref[:, _pair_cols(pair)])
            s_prev = _dot(keys(pair)[:BLOCK], qs, NT) * scale + b_ref[pair, :BLOCK]
            sc_ref[pair, :BLOCK] = jnp.where(first, NEG_INF, s_prev)
            sc_ref[pair, BLOCK:] = _dot(keys(pair)[BLOCK:], qs, NT) * scale + b_ref[pair, BLOCK:]
        inv = []
        for h in range(N_HEADS):
            cols = slice(h % 2 * BLOCK, (h % 2 + 1) * BLOCK)
            s = sc_ref[h // 2, :, cols]
            m = jnp.max(s, axis=0, keepdims=True)
            if use_sinks:
                sink = s_ref[:, h:h + 1]
                m = jnp.maximum(m, sink)
            p = jnp.exp(s - m)
            denom = jnp.sum(p, axis=0, keepdims=True)
            if use_sinks:
                denom = denom + jnp.exp(sink - m)
            pr_ref[h // 2, :, cols] = p.astype(BF16)
            l_ref[h:h + 1, :] = m + jnp.log(denom)
            inv.append(1.0 / denom)
        for pair in range(PAIRS):
            both = _dot(values_t(pair), pr_ref[pair], NN)
            o_t = jnp.concatenate([both[:HEAD_DIM, :BLOCK] * inv[2 * pair], both[HEAD_DIM:, BLOCK:] * inv[2 * pair + 1]],
                                  axis=0)
            o_ref[:, _pair_cols(pair)] = o_t.T

    prev = lambda i: jnp.maximum(i - 1, 0)
    in_specs = [
        pl.BlockSpec((BLOCK, QW), lambda r, i: (i, lay.q_col(r))),
        pl.BlockSpec((BLOCK, kw), lambda r, i: (i, lay.k_col(r))),
        pl.BlockSpec((BLOCK, kw), lambda r, i: (prev(i), lay.k_col(r))),
        pl.BlockSpec((BLOCK, kw), lambda r, i: (i, lay.v_col(r))),
        pl.BlockSpec((BLOCK, kw), lambda r, i: (prev(i), lay.v_col(r))),
        pl.BlockSpec((PAIRS, 2 * BLOCK, 2 * BLOCK), lambda r, i: (0, 0, 0)),
    ]
    args = [proj, proj, proj, proj, proj, bias]
    if use_sinks:
        in_specs.append(pl.BlockSpec((1, N_HEADS), lambda r, i: (0, 0)))
        args.append(sinks)
    out_specs = [pl.BlockSpec((BLOCK, QW), lambda r, i: (i, r)),
                 pl.BlockSpec((None, N_HEADS, BLOCK), lambda r, i: (r, 0, i))]
    out_shape = [jax.ShapeDtypeStruct((L, lay.dil * QW), F32), jax.ShapeDtypeStruct((lay.dil, N_HEADS, L), F32)]
    return pl.pallas_call(
        kern, name=name, grid=(lay.dil, nb),
        in_specs=in_specs + ride.in_specs, out_specs=out_specs + ride.out_specs,
        out_shape=out_shape + ride.out_shapes,
        scratch_shapes=[pltpu.VMEM((PAIRS, 2 * BLOCK, 2 * BLOCK), dt) for dt in (F32, BF16)] + ride.scratch,
        input_output_aliases=ride.aliases,
        compiler_params=_params("arbitrary", "arbitrary"),
    )(*args, *ride.args)


def _attn_bwd(proj, do, lse, dd, bias, sinks, lay, name, exchange=None):
    L = proj.shape[0]
    nb = L // BLOCK
    kw, rep = lay.kw, lay.rep
    assert rep == 1 or lay.kv_heads == 2, "grouped queries: the two kv heads fill one 128-lane block"
    use_sinks = sinks is not None
    scale = HEAD_DIM ** -0.5
    ride = _Ride(exchange, 10 if use_sinks else 9, 4 if use_sinks else 3, 6)

    def kern(*refs):
        ins, outs, (ck_ref, cv_ref, *staged), exrefs = ride.split(refs)
        q_ref, kc_ref, kp_ref, vc_ref, vp_ref, do_ref, l_ref, d_ref, b_ref = ins[:9]
        s_ref = ins[9] if use_sinks else None
        dq_ref, dk_ref, dv_ref = outs[:3]
        ds_ref = outs[3] if use_sinks else None
        r = pl.program_id(0)
        i = pl.program_id(1)
        ride.around(r * (nb + 1) + i, lay.dil * (nb + 1), exrefs,
                    lambda: compute(q_ref, kc_ref, kp_ref, vc_ref, vp_ref, do_ref, l_ref, d_ref, b_ref, s_ref,
                                    dq_ref, dk_ref, dv_ref, ds_ref, ck_ref, cv_ref, r, i, *staged))

    def compute(q_ref, kc_ref, kp_ref, vc_ref, vp_ref, do_ref, l_ref, d_ref, b_ref, s_ref,
                dq_ref, dk_ref, dv_ref, ds_ref, ck_ref, cv_ref, r, i, sc_ref, dp_ref, pr_ref, dsc_ref):
        first = i == 0

        @pl.when(first)
        def _():
            ck_ref[...] = jnp.zeros_like(ck_ref)
            cv_ref[...] = jnp.zeros_like(cv_ref)

        if use_sinks:
            @pl.when(first & (r == 0))
            def _():
                ds_ref[...] = jnp.zeros_like(ds_ref)

        @pl.when(i < nb)
        def _():
            keys, values = _paired_kv(kp_ref, kc_ref, rep), _paired_kv(vp_ref, vc_ref, rep)
            keys_t = _paired_kv(kp_ref, kc_ref, rep, transposed=True)
            for pair in range(PAIRS):
                qs = _split_heads(q_ref[:, _pair_cols(pair)])
                dos = _split_heads(do_ref[:, _pair_cols(pair)].astype(BF16))
                s = _dot(keys(pair), qs, NT) * scale + b_ref[pair]
                sc_ref[pair, :BLOCK] = jnp.where(first, NEG_INF, s[:BLOCK])
                sc_ref[pair, BLOCK:] = s[BLOCK:]
                dp_ref[pair] = _dot(values(pair), dos, NT)
            for h in range(N_HEADS):
                cols = slice(h % 2 * BLOCK, (h % 2 + 1) * BLOCK)
                lrow = l_ref[h:h + 1, :]
                drow = d_ref[h:h + 1, :]
                p = jnp.exp(sc_ref[h // 2, :, cols] - lrow)
                pr_ref[h // 2, :, cols] = p.astype(BF16)
                dsc_ref[h // 2, :, cols] = (p * (dp_ref[h // 2, :, cols] - drow) * scale).astype(BF16)
                if use_sinks:
                    ds_ref[h:h + 1, :] += -(jnp.exp(s_ref[:, h:h + 1] - lrow) * drow)
            grouped = {}
            for pair in range(PAIRS):
                cols = _pair_cols(pair)
                qs = _split_heads(q_ref[:, cols])
                dos = _split_heads(do_ref[:, cols].astype(BF16))
                ds = dsc_ref[pair]
                both = _dot(keys_t(pair), ds, NN)
                dq_t = jnp.concatenate([both[:HEAD_DIM, :BLOCK], both[HEAD_DIM:, BLOCK:]], axis=0)
                dq_ref[:, cols] = dq_t.T.astype(dq_ref.dtype)
                dk = _dot(ds, qs, NN)
                dv = _dot(pr_ref[pair], dos, NN)
                if rep == 1:
                    dk_ref[:, cols] = (ck_ref[:, cols] + dk[:BLOCK]).astype(dk_ref.dtype)
                    dv_ref[:, cols] = (cv_ref[:, cols] + dv[:BLOCK]).astype(dv_ref.dtype)
                    ck_ref[:, cols] = dk[BLOCK:]
                    cv_ref[:, cols] = dv[BLOCK:]
                else:
                    g = 2 * pair // rep
                    grouped[g] = (dk, dv) if g not in grouped else (grouped[g][0] + dk, grouped[g][1] + dv)
            if rep > 1:
                fold = lambda t: t + pltpu.roll(t, HEAD_DIM, 1)
                first_half = _first_head_lanes((2 * BLOCK, LANES))
                dk = jnp.where(first_half, fold(grouped[0][0]), fold(grouped[1][0]))
                dv = jnp.where(first_half, fold(grouped[0][1]), fold(grouped[1][1]))
                dk_ref[...] = (ck_ref[...] + dk[:BLOCK]).astype(dk_ref.dtype)
                dv_ref[...] = (cv_ref[...] + dv[:BLOCK]).astype(dv_ref.dtype)
                ck_ref[...] = dk[BLOCK:]
                cv_ref[...] = dv[BLOCK:]

        @pl.when(i == nb)
        def _():
            dk_ref[...] = ck_ref[...].astype(dk_ref.dtype)
            dv_ref[...] = cv_ref[...].astype(dv_ref.dtype)
            if use_sinks:
                @pl.when(r == lay.dil - 1)
                def _():
                    ds_ref[...] = jnp.broadcast_to(jnp.sum(ds_ref[...], axis=1, keepdims=True), ds_ref.shape)

    cur = lambda i: jnp.minimum(i, nb - 1)
    prev = lambda i: jnp.maximum(jnp.minimum(i, nb - 1) - 1, 0)
    done = lambda i: jnp.maximum(i - 1, 0)
    qspec = lambda col: pl.BlockSpec((BLOCK, QW), lambda r, i: (cur(i), col(r)))
    per_head = pl.BlockSpec((None, N_HEADS, BLOCK), lambda r, i: (r, 0, cur(i)))
    in_specs = [
        qspec(lay.q_col),
        pl.BlockSpec((BLOCK, kw), lambda r, i: (cur(i), lay.k_col(r))),
        pl.BlockSpec((BLOCK, kw), lambda r, i: (prev(i), lay.k_col(r))),
        pl.BlockSpec((BLOCK, kw), lambda r, i: (cur(i), lay.v_col(r))),
        pl.BlockSpec((BLOCK, kw), lambda r, i: (prev(i), lay.v_col(r))),
        qspec(lambda r: r), per_head, per_head,
        pl.BlockSpec((PAIRS, 2 * BLOCK, 2 * BLOCK), lambda r, i: (0, 0, 0)),
    ]
    args = [proj, proj, proj, proj, proj, do, lse, dd, bias]
    out_specs = [
        qspec(lambda r: r),
        pl.BlockSpec((BLOCK, kw), lambda r, i: (done(i), r)),
        pl.BlockSpec((BLOCK, kw), lambda r, i: (done(i), r)),
    ]
    dkv_shape = jax.ShapeDtypeStruct((L, lay.dil * kw), BF16)
    out_shape = [jax.ShapeDtypeStruct((L, lay.dil * QW), BF16), dkv_shape, dkv_shape]
    if use_sinks:
        in_specs.append(pl.BlockSpec((1, N_HEADS), lambda r, i: (0, 0)))
        args.append(sinks)
        out_specs.append(pl.BlockSpec((N_HEADS, LANES), lambda r, i: (0, 0)))
        out_shape.append(jax.ShapeDtypeStruct((N_HEADS, LANES), F32))
    return pl.pallas_call(
        kern, name=name, grid=(lay.dil, nb + 1),
        in_specs=in_specs + ride.in_specs, out_specs=out_specs + ride.out_specs,
        out_shape=out_shape + ride.out_shapes,
        scratch_shapes=[pltpu.VMEM((BLOCK, kw), F32), pltpu.VMEM((BLOCK, kw), F32)]
        + [pltpu.VMEM((PAIRS, 2 * BLOCK, 2 * BLOCK), dt) for dt in (F32, F32, BF16, BF16)] + ride.scratch,
        input_output_aliases=ride.aliases,
        compiler_params=_params("arbitrary", "arbitrary"),
    )(*args, *ride.args)


def _assemble(groups, name, dils=(1,)):
    T = groups[0][0].shape[0] * dils[0]
    widths = [g[0].shape[1] // dils[0] for g in groups]
    total = sum(widths)
    flat = [a for g in groups for a in g]
    member_dils = [d for g in groups for d in dils[:len(g)]]

    def kern(*refs):
        ins = refs[:len(flat)]
        out_ref, cs_ref = refs[len(flat):]

        @pl.when(pl.program_id(0) == 0)
        def _():
            cs_ref[...] = jnp.zeros_like(cs_ref)

        pos = off = 0
        for g, w in zip(groups, widths):
            acc = _to_token_order(ins[pos], dils[0])
            for j in range(1, len(g)):
                acc = acc + _to_token_order(ins[pos + j], dils[j])
            pos += len(g)
            out_ref[:, off:off + w] = acc.astype(BF16)
            cs_ref[:, off:off + w] += jnp.sum(acc, axis=0, keepdims=True)
            off += w

    return pl.pallas_call(
        kern, name=name, grid=(T // LEAN_ROWS,),
        in_specs=[_view_spec(LEAN_ROWS, a.shape[1] // d, d) for a, d in zip(flat, member_dils)],
        out_specs=[pl.BlockSpec((LEAN_ROWS, total), lambda i: (i, 0)), pl.BlockSpec((1, total), lambda i: (0, 0))],
        out_shape=[jax.ShapeDtypeStruct((T, total), BF16), jax.ShapeDtypeStruct((1, total), F32)],
        compiler_params=_params("arbitrary"),
    )(*flat)


def _adamw(w, g, m, v, name):
    _, R, C = w.shape
    rows = min(R, ROWS)
    assert R % rows == 0

    def kern(w_ref, g_ref, m_ref, v_ref, d_ref, nm_ref, nv_ref):
        gv = g_ref[...]
        mn = ADAM_B1 * m_ref[...] + (1.0 - ADAM_B1) * gv
        vn = ADAM_B2 * v_ref[...] + (1.0 - ADAM_B2) * jnp.square(gv)
        m_hat = mn / (1.0 - ADAM_B1 ** ADAM_STEP)
        v_hat = vn / (1.0 - ADAM_B2 ** ADAM_STEP)
        d_ref[...] = -ADAM_LR * (m_hat / (jnp.sqrt(v_hat) + ADAM_EPS) + ADAM_WD * w_ref[...])
        nm_ref[...] = mn
        nv_ref[...] = vn

    blk = pl.BlockSpec((None, rows, C), lambda i: (0, i, 0))
    shp = jax.ShapeDtypeStruct((1, R, C), F32)
    return pl.pallas_call(
        kern, name=name, grid=(R // rows,),
        in_specs=[blk, pl.BlockSpec((rows, C), lambda i: (i, 0)), blk, blk], out_specs=[blk] * 3, out_shape=[shp] * 3,
        compiler_params=_params("parallel"),
    )(w, g, m, v)


def _sum_slots(slots, name):
    n, R, C = slots.shape
    SUM_ROWS = next(rows for rows in (256, 128, 64, 32, 16) if R % rows == 0)

    def kern(s_ref, o_ref):
        acc = s_ref[0].astype(F32)
        for k in range(1, n):
            acc = acc + s_ref[k].astype(F32)
        o_ref[...] = acc

    return pl.pallas_call(
        kern, name=name, grid=(R // SUM_ROWS,),
        in_specs=[pl.BlockSpec((n, SUM_ROWS, C), lambda i: (0, i, 0))],
        out_specs=pl.BlockSpec((SUM_ROWS, C), lambda i: (i, 0)),
        out_shape=jax.ShapeDtypeStruct((R, C), F32),
        compiler_params=_params("parallel"),
    )(slots)


def _place():
    return lax.axis_index("x"), lax.axis_index("y"), lax.axis_index("c")


def _index(p):
    return 4 * p[0] + 2 * p[1] + p[2]


FLIPS = [(fx, fy, fc) for fx in (0, 1) for fy in (0, 1) for fc in (0, 1)][1:]


def _peer(me, flip):
    return tuple(1 - a if f else a for a, f in zip(me, flip))


def _gather_rows(shards, part=(0, 1), into=None, relay_at=RELAY_AT):
    nw = len(shards)

    def plan(ins, outs, send_sems, recv_sems):
        x, y, c = me = _place()
        sibling = (x, y, 1 - c)
        chips = [(1 - x, y), (x, 1 - y), (1 - x, 1 - y)]

        def span(w):
            cnt = ins[w].shape[0] // part[1]
            return part[0] * cnt, cnt

        def rows(w, p):
            lo, cnt = span(w)
            return outs[w].at[pl.ds(_index(p) * ins[w].shape[0] + lo, cnt), :]

        def own(w):
            lo, cnt = span(w)
            return ins[w].at[pl.ds(lo, cnt), :]

        def copy(w, k, block, to):
            return pltpu.make_async_remote_copy(
                src_ref=own(w) if block is me else rows(w, block), dst_ref=rows(w, block),
                send_sem=send_sems.at[7 * w + k], recv_sem=recv_sems.at[7 * w + k],
                device_id=to, device_id_type=MESH)

        return me, sibling, chips, c, rows, own, copy

    def copies(ins, outs, send_sems, recv_sems, local_sems):
        me, sibling, chips, c, rows, own, copy = plan(ins, outs, send_sems, recv_sems)
        local = [pltpu.make_async_copy(own(w), rows(w, me), local_sems.at[w]) for w in range(nw)]
        sends, recvs = [], []
        for w in range(nw):
            sends.append(copy(w, 0, me, sibling))
            sends += [copy(w, 1 + j, me, (*chip, c)) for j, chip in enumerate(chips)]
            recvs.append(copy(w, 0, sibling, me))
            recvs += [copy(w, 4 + j, (*chip, 1 - c), me) for j, chip in enumerate(chips)]
        return local, sends, recvs

    def relay(ins, outs, send_sems, recv_sems, local_sems):
        me, sibling, chips, c, rows, own, copy = plan(ins, outs, send_sems, recv_sems)
        arrived = [copy(w, 1 + j, (*chip, c), me) for w in range(nw) for j, chip in enumerate(chips)]
        onward = [copy(w, 4 + j, (*chip, c), sibling) for w in range(nw) for j, chip in enumerate(chips)]
        return arrived, onward

    shapes = [jax.ShapeDtypeStruct((N_DEV * s.shape[0], s.shape[1]), s.dtype) for s in shards]
    aliases = {nw + w: w for w in range(nw)} if into else None
    return _Exchange(shards + (into or []), shapes, 7 * nw, nw, copies, aliases=aliases, relay=relay,
                     relay_at=relay_at)


def _scatter_rows(parts, part=(0, 1)):
    nw = len(parts)

    def copies(ins, outs, send_sems, recv_sems, local_sems):
        me = _place()

        def src(w, owner):
            n = ins[w].shape[0] // N_DEV
            cnt = n // part[1]
            return ins[w].at[pl.ds(_index(owner) * n + part[0] * cnt, cnt), :]

        def copy(k, w, owner, sender, to):
            return pltpu.make_async_remote_copy(
                src_ref=src(w, owner), dst_ref=outs[w].at[_index(sender)],
                send_sem=send_sems.at[nw * k + w], recv_sem=recv_sems.at[nw * k + w],
                device_id=to, device_id_type=MESH)

        local = [pltpu.make_async_copy(src(w, me), outs[w].at[_index(me)], local_sems.at[w]) for w in range(nw)]
        peers = [_peer(me, flip) for flip in FLIPS]
        sends = [copy(k, w, peer, me, peer) for k, peer in enumerate(peers) for w in range(nw)]
        recvs = [copy(k, w, me, peer, me) for k, peer in enumerate(peers) for w in range(nw)]
        return local, sends, recvs

    shapes = [jax.ShapeDtypeStruct((N_DEV, p.shape[0] // N_DEV // part[1], p.shape[1]), p.dtype) for p in parts]
    return _Exchange(parts, shapes, 7 * nw, nw, copies)


def _sum_over_devices(v):
    shape = v.shape

    def body(v_ref, sum_ref, all_ref, send_sems, recv_sems):
        me = _place()
        all_ref[_index(me)] = v_ref[...]
        sends = []
        for k, flip in enumerate(FLIPS):
            peer = _peer(me, flip)
            sends.append(pltpu.make_async_remote_copy(
                src_ref=v_ref, dst_ref=all_ref.at[_index(me)],
                send_sem=send_sems.at[k], recv_sem=recv_sems.at[k], device_id=peer, device_id_type=MESH))
            sends[-1].start()
        for k, flip in enumerate(FLIPS):
            peer = _peer(me, flip)
            pltpu.make_async_remote_copy(
                src_ref=v_ref, dst_ref=all_ref.at[_index(peer)],
                send_sem=send_sems.at[k], recv_sem=recv_sems.at[k], device_id=peer, device_id_type=MESH).wait_recv()
        for cp in sends:
            cp.wait_send()
        acc = all_ref[0]
        for s in range(1, N_DEV):
            acc = acc + all_ref[s]
        sum_ref[...] = acc

    vmem = pl.BlockSpec(memory_space=pltpu.VMEM)
    return pl.pallas_call(
        body, name="sum_small_grads",
        in_specs=[vmem], out_specs=[vmem, vmem],
        out_shape=[jax.ShapeDtypeStruct(shape, F32), jax.ShapeDtypeStruct((N_DEV,) + shape, F32)],
        scratch_shapes=[pltpu.SemaphoreType.DMA((7,)), pltpu.SemaphoreType.DMA((7,))],
    )(v)[0]


SMALL_ROWS = 8


def _pack_small(vectors):
    padded = []
    for vec in vectors:
        vec = vec.reshape(-1)
        padded.append(jnp.pad(vec, (0, -vec.shape[0] % 128)))
    flat = jnp.concatenate(padded)
    flat = jnp.pad(flat, (0, -flat.shape[0] % (SMALL_ROWS * 128)))
    return flat.reshape(SMALL_ROWS, -1)


def _unpack_small(packed, shapes):
    flat = packed.reshape(-1)
    out, off = [], 0
    for shp in shapes:
        n = int(np.prod(shp))
        out.append(flat[off:off + n].reshape(shp))
        off += n + (-n % 128)
    return out


def kernel(x, g_attn, w_in, b_in, sinks_a, g_out_a, g_out_b, w_out, g_mlp, w_1, w_2, g_final, loss_target, m_g_attn, m_w_in, m_b_in, m_sinks_a, m_g_out_a, m_g_out_b, m_w_out, m_g_mlp, m_w_1, m_w_2, m_g_final, v_g_attn, v_w_in, v_b_in, v_sinks_a, v_g_out_a, v_g_out_b, v_w_out, v_g_mlp, v_w_1, v_w_2, v_g_final):
    xs, tgt = x[0], loss_target[0]
    T, D = xs.shape
    n_a = QW + 2 * KV_HEADS_A * HEAD_DIM
    g_fin = g_final.reshape(1, D)

    shards = [w_in[0].T.astype(BF16), w_out[0].astype(BF16), w_1[0].T.astype(BF16), w_2[0].astype(BF16)]
    ident = lambda acc: (acc,)
    add = lambda acc, other: (acc + other,)

    h1, w_in_t = _norm_fwd(xs, g_attn, "norm_attn", exchange=_gather_rows(shards[:1], relay_at=1.0))
    n_in = w_in_t.shape[0]
    proj_a, = _proj_views(h1, w_in_t, b_in, (0, n_a), [1], "proj_a")
    dils = [dil for _, dil in DILATED_BRANCHES]
    *proj_b, w_o = _proj_views(h1, w_in_t, b_in, (n_a, n_in - n_a), dils, "proj_b", exchange=_gather_rows(shards[1:2]))

    lay_a = _AttnLayout(1, KV_HEADS_A, 0, 0, 0, QW // (KV_HEADS_A * HEAD_DIM), QW // (KV_HEADS_A * HEAD_DIM) + 1)
    bias_a = _band_bias(WINDOW_A - 1, 1)
    o_a, l_a, w_1_t = _attn_fwd(proj_a, bias_a, sinks_a, lay_a, "attn_a_fwd",
                                exchange=_gather_rows(shards[2:3], part=(0, 4)))
    branches = []
    for n, (window, dil) in enumerate(DILATED_BRANCHES):
        lay = _AttnLayout(dil, N_HEADS, 3, 0, 3, 1, 2)
        bias = _band_bias(window // dil, dil)
        ride = _gather_rows(shards[2:3], part=(n + 1, 4), into=[w_1_t])
        o, lse, w_1_t = _attn_fwd(proj_b[n], bias, None, lay, f"attn_b{dil}_fwd", exchange=ride)
        branches.append((lay, bias, proj_b[n], o, lse))
    o_b = [br[3] for br in branches]
    l_b = [br[4].transpose(2, 0, 1).reshape(T, N_HEADS) for br in branches]

    mix = _mix_fwd(o_a, o_b, l_b, g_out_a, g_out_b, dils)
    def residual_and_norm(acc, res, g):
        x_new = acc + res
        return x_new, (x_new * _rstd(x_new)) * g

    assert TILE_WHOLE_ROWS["tn"] == D
    x2, h2 = _matmul(mix, w_o, "nn", [F32, BF16], residual_and_norm, tk=D, tile_ins=[xs], row_ins=[g_mlp],
                     name="out_proj", **TILE_WHOLE_ROWS)

    def relu_sq(acc):
        u = jnp.maximum(acc, 0.0)
        return u, u * u

    u, u_sq, w_2_f = _matmul(h2, w_1_t, "nt", [BF16, BF16], relu_sq, tk=D, name="mlp_up",
                             exchange=_gather_rows(shards[3:]), **TILE_WHOLE_ROWS)
    x3, = _matmul(u_sq, w_2_f, "nn", [F32], add, tk=4096, tile_ins=[x2], name="mlp_down", **TILE)

    dx3, dx3_b, dg_final, loss_dev = _loss_head(x3, tgt, g_fin)

    d_pre, = _matmul(dx3_b, w_2_f, "nt", [BF16], lambda acc, uu: (acc * (2.0 * uu.astype(F32)),),
                     tk=D, tile_ins=[u], name="mlp_down_bwd", **TILE_WHOLE_ROWS)
    dw_2, = _matmul(u_sq, dx3_b, "tn", [BF16], ident, name="mlp_down_wgrad", **TILE_WGRAD)
    dh2, slots_2a = _matmul(d_pre, w_1_t, "nn", [BF16], ident, tk=4096, name="mlp_up_bwd",
                            exchange=_scatter_rows([dw_2], part=(0, 2)), **TILE)
    dw_1_t, slots_2b = _matmul(d_pre, h2, "tn", [BF16], ident, name="mlp_up_wgrad",
                               exchange=_scatter_rows([dw_2], part=(1, 2)), **TILE_WGRAD)
    dx2, dg_mlp, dx2_b, dmix = _norm_bwd(dh2, x2, g_mlp, dx3, "norm_mlp_bwd", then_w_t=w_o)
    dw_o, = _matmul(mix, dx2_b, "tn", [BF16], ident, name="out_proj_wgrad", **TILE_WGRAD)
    do_a, dd_a, do1, do2, do3, dd1, dd2, dd3, dg_out_a, dg_out_b, slots_o = _mix_bwd(
        dmix, o_a, o_b, l_b, g_out_a, g_out_b, dils, exchange=_scatter_rows([dw_o]))

    by_class = lambda d, dil: d.reshape(T // dil, dil, N_HEADS).transpose(1, 2, 0)
    slots_1 = [None] * 4
    dq_a, dk_a, dv_a, dsinks, slots_1[0] = _attn_bwd(proj_a, do_a, l_a, by_class(dd_a, 1), bias_a, sinks_a, lay_a,
                                                     "attn_a_bwd", exchange=_scatter_rows([dw_1_t], part=(0, 4)))
    dsinks = dsinks[:, 0].reshape(1, N_HEADS)
    dqs, dks, dvs = [], [], []
    for n, ((lay, bias, view, _, lse), do_n, dd_n) in enumerate(zip(branches, (do1, do2, do3), (dd1, dd2, dd3))):
        dq, dk, dv, slots_1[n + 1] = _attn_bwd(view, do_n, lse, by_class(dd_n, lay.dil), bias, None, lay,
                                               f"attn_b{lay.dil}_bwd",
                                               exchange=_scatter_rows([dw_1_t], part=(n + 1, 4)))
        dqs.append(dq)
        dks.append(dk)
        dvs.append(dv)
    dproj, db_in = _assemble([[dq_a], [dk_a], [dv_a], dqs, dks, dvs], "dproj", dils)

    dw_in_t, = _matmul(dproj, h1, "tn", [BF16], ident, tm=n_in // 2, tn=1024, tk=1024, name="in_proj_wgrad")
    dh1, slots_in = _matmul(dproj, w_in_t, "nn", [BF16], ident, tk=n_in, name="in_proj_bwd",
                            exchange=_scatter_rows([dw_in_t]), **TILE)
    dx, dg_attn = _norm_bwd(dh1, xs, g_attn, dx2, "norm_attn_bwd")

    g_w_in = _sum_slots(slots_in, "sum_w_in_grads").T
    g_w_out = _sum_slots(slots_o, "sum_w_out_grads")
    g_w_1 = jnp.concatenate([_sum_slots(s, f"sum_w_1_grads_{n}") for n, s in enumerate(slots_1)]).T
    g_w_2 = jnp.concatenate([_sum_slots(slots_2a, "sum_w_2_grads_0"), _sum_slots(slots_2b, "sum_w_2_grads_1")])

    small_w = [g_attn, b_in, sinks_a, g_out_a, g_out_b, g_mlp, g_final]
    small_m = [m_g_attn, m_b_in, m_sinks_a, m_g_out_a, m_g_out_b, m_g_mlp, m_g_final]
    small_v = [v_g_attn, v_b_in, v_sinks_a, v_g_out_a, v_g_out_b, v_g_mlp, v_g_final]
    small_g = [dg_attn, db_in, dsinks, dg_out_a, dg_out_b, dg_mlp, dg_final]
    summed = _sum_over_devices(_pack_small(small_g + [loss_dev[:, :1]]))
    shapes = [w.shape for w in small_w]
    *g_small, loss = _unpack_small(summed, shapes + [()])

    big = [
        _adamw(w_in, g_w_in, m_w_in, v_w_in, "adamw_w_in"),
        _adamw(w_out, g_w_out, m_w_out, v_w_out, "adamw_w_out"),
        _adamw(w_1, g_w_1, m_w_1, v_w_1, "adamw_w_1"),
        _adamw(w_2, g_w_2, m_w_2, v_w_2, "adamw_w_2"),
    ]
    g_packed = _pack_small(g_small)
    small = _adamw(_pack_small(small_w)[None], g_packed, _pack_small(small_m)[None], _pack_small(small_v)[None],
                   "adamw_small")
    small = [_unpack_small(s, shapes) for s in small]

    def ordered(small_list, big_list):
        s = list(small_list)
        return [s[0], big_list[0], s[1], s[2], s[3], s[4], big_list[1], s[5], big_list[2], big_list[3], s[6]]

    grads = ordered(g_small, [g[None] for g in (g_w_in, g_w_out, g_w_1, g_w_2)])
    deltas = ordered(small[0], [b[0] for b in big])
    new_m = ordered(small[1], [b[1] for b in big])
    new_v = ordered(small[2], [b[2] for b in big])
    return (loss, dx[None], *grads, *deltas, *new_m, *new_v)
```

```python
import numpy as np
import jax
import jax.numpy as jnp
from jax import lax
from jax.experimental import pallas as pl
from jax.experimental.pallas import tpu as pltpu

F32 = jnp.float32
BF16 = jnp.bfloat16

HEAD_DIM = 64
N_HEADS = 16
KV_HEADS_A = 2
BLOCK = 128
WINDOW_A = 128
DILATED_BRANCHES = ((128, 1), (512, 4), (2048, 16))
EPS = 1e-5
NEG_INF = -1e30
N_DEV = 8

ADAM_LR = 0.001
ADAM_B1 = 0.9
ADAM_B2 = 0.999
ADAM_EPS = 1e-08
ADAM_WD = 0.01
ADAM_STEP = 10

VMEM_LIMIT_BYTES = 56 * 1024 * 1024
MESH = pl.DeviceIdType.MESH
ANY = pl.BlockSpec(memory_space=pl.ANY)

NN = (((1,), (0,)), ((), ()))
NT = (((1,), (1,)), ((), ()))
TN = (((0,), (0,)), ((), ()))


def _dot(a, b, dims):
    return lax.dot_general(a, b, dims, preferred_element_type=F32)


def _params(*sem):
    return pltpu.CompilerParams(dimension_semantics=sem, vmem_limit_bytes=VMEM_LIMIT_BYTES)


RELAY_AT = 0.6


class _Exchange:
    def __init__(self, ins, out_shapes, n_remote, n_local, copies, aliases=None, relay=None, relay_at=RELAY_AT):
        self.ins, self.out_shapes = list(ins), list(out_shapes)
        self.n_remote, self.n_local = n_remote, n_local
        self.copies = copies
        self.relay = relay
        self.relay_at = relay_at
        self.aliases = aliases or {}

    def start(self, refs):
        local, sends, _ = self.copies(*refs)
        for cp in local + sends:
            cp.start()

    def middle(self, refs):
        arrived, onward = self.relay(*refs)
        for got, cp in zip(arrived, onward):
            got.wait_recv()
            cp.start()

    def finish(self, refs):
        local, sends, recvs = self.copies(*refs)
        for cp in recvs:
            cp.wait_recv()
        for cp in sends:
            cp.wait_send()
        for cp in local:
            cp.wait()
        if self.relay:
            for cp in self.relay(*refs)[1]:
                cp.wait_send()


class _Ride:
    def __init__(self, ex, n_in, n_out, n_scratch):
        self.ex = ex
        self.n = (n_in, n_out, n_scratch)
        self.args = ex.ins if ex else []
        self.in_specs = [ANY] * len(self.args)
        self.out_shapes = ex.out_shapes if ex else []
        self.out_specs = [ANY] * len(self.out_shapes)
        self.scratch = [pltpu.SemaphoreType.DMA((ex.n_remote,)), pltpu.SemaphoreType.DMA((ex.n_remote,)),
                        pltpu.SemaphoreType.DMA((max(ex.n_local, 1),))] if ex else []
        self.aliases = {n_in + i: n_out + o for i, o in ex.aliases.items()} if ex else {}

    def split(self, refs):
        n_in, n_out, n_scratch = self.n
        a = n_in
        b = a + len(self.args)
        c = b + n_out
        d = c + len(self.out_shapes)
        e = d + n_scratch
        return refs[:a], refs[b:c], refs[d:e], (refs[a:b], refs[c:d], *refs[e:])

    def around(self, step, n_steps, exrefs, compute):
        if self.ex is None:
            compute()
            return

        @pl.when(step == 0)
        def _():
            self.ex.start(exrefs)

        compute()

        if self.ex.relay:
            @pl.when(step == int(self.ex.relay_at * (n_steps - 1)))
            def _():
                self.ex.middle(exrefs)

        @pl.when(step == n_steps - 1)
        def _():
            self.ex.finish(exrefs)


TILE = dict(tm=512, tn=1024)
TILE_WHOLE_ROWS = dict(tm=512, tn=2048)
TILE_WGRAD = dict(tm=1024, tn=1024, tk=4096)

def _matmul(a, b, dims, out_dtypes, epilogue, *, tm, tn, tk, name, tile_ins=(), row_ins=(), exchange=None):
    if dims == "tn":
        K, M = a.shape
    else:
        M, K = a.shape
    N = b.shape[0] if dims == "nt" else b.shape[1]
    tm, tn, tk = min(tm, M), min(tn, N), min(tk, K)
    assert M % tm == 0 and N % tn == 0 and K % tk == 0, (name, M, N, K, tm, tn, tk)
    grid = (M // tm, N // tn, K // tk)
    nk = grid[2]
    n_tile, n_row, n_out = len(tile_ins), len(row_ins), len(out_dtypes)
    dn = {"nn": NN, "nt": NT, "tn": TN}[dims]
    ride = _Ride(exchange, 2 + n_tile + n_row, n_out, 1 if nk > 1 else 0)

    def kern(*refs):
        ins, out_refs, scratch, exrefs = ride.split(refs)
        a_ref, b_ref = ins[:2]
        tile_refs = ins[2:2 + n_tile]
        row_refs = ins[2 + n_tile:]
        ids = [pl.program_id(d) for d in range(3)]

        def finish(acc):
            outs = epilogue(acc, *[r[...] for r in tile_refs], *[r[...] for r in row_refs])
            for o_ref, o in zip(out_refs, outs):
                o_ref[...] = o.astype(o_ref.dtype)

        def compute():
            if nk == 1:
                finish(_dot(a_ref[...], b_ref[...], dn))
                return
            acc_ref = scratch[0]

            @pl.when(ids[2] == 0)
            def _():
                acc_ref[...] = jnp.zeros_like(acc_ref)

            acc_ref[...] += _dot(a_ref[...], b_ref[...], dn)

            @pl.when(ids[2] == nk - 1)
            def _():
                finish(acc_ref[...])

        ride.around((ids[0] * grid[1] + ids[1]) * grid[2] + ids[2], grid[0] * grid[1] * grid[2], exrefs, compute)

    if dims == "tn":
        a_spec = pl.BlockSpec((tk, tm), lambda i, j, k: (k, i))
    else:
        a_spec = pl.BlockSpec((tm, tk), lambda i, j, k: (i, k))
    if dims == "nt":
        b_spec = pl.BlockSpec((tn, tk), lambda i, j, k: (j, k))
    else:
        b_spec = pl.BlockSpec((tk, tn), lambda i, j, k: (k, j))
    tile_spec = pl.BlockSpec((tm, tn), lambda i, j, k: (i, j))
    row_spec = pl.BlockSpec((1, tn), lambda i, j, k: (0, j))
    sem = ("arbitrary",) * 3 if exchange else ("parallel", "parallel", "arbitrary")
    return pl.pallas_call(
        kern,
        name=name,
        grid=grid,
        in_specs=[a_spec, b_spec] + [tile_spec] * n_tile + [row_spec] * n_row + ride.in_specs,
        out_specs=[tile_spec] * n_out + ride.out_specs,
        out_shape=[jax.ShapeDtypeStruct((M, N), dt) for dt in out_dtypes] + ride.out_shapes,
        scratch_shapes=([pltpu.VMEM((tm, tn), F32)] if nk > 1 else []) + ride.scratch,
        input_output_aliases=ride.aliases,
        compiler_params=_params(*sem),
    )(a, b, *tile_ins, *row_ins, *ride.args)


PROJ_ROWS = 256


def _proj_views(a, w_t, bias, cols, dils, name, exchange=None):
    T, K = a.shape
    first, N = cols
    ride = _Ride(exchange, 3, len(dils), 0)

    def kern(*refs):
        (a_ref, w_ref, b_ref), outs, _, exrefs = ride.split(refs)

        def compute():
            acc = _dot(a_ref[...], w_ref[...], NT) + b_ref[...]
            for out_ref, dil in zip(outs, dils):
                _to_class_order(acc, out_ref, dil)

        ride.around(pl.program_id(0), T // LEAN_ROWS, exrefs, compute)

    once = pl.Buffered(1)
    return pl.pallas_call(
        kern, name=name, grid=(T // LEAN_ROWS,),
        in_specs=[pl.BlockSpec((LEAN_ROWS, K), lambda i: (i, 0)),
                  pl.BlockSpec((pl.Element(N), pl.Element(K)), lambda i: (first, 0), pipeline_mode=once),
                  pl.BlockSpec((pl.Element(1), pl.Element(N)), lambda i: (0, first), pipeline_mode=once)] + ride.in_specs,
        out_specs=[_view_spec(LEAN_ROWS, N, d) for d in dils] + ride.out_specs,
        out_shape=[jax.ShapeDtypeStruct((T // d, d * N), BF16) for d in dils] + ride.out_shapes,
        scratch_shapes=ride.scratch,
        input_output_aliases=ride.aliases,
        compiler_params=_params("arbitrary"),
    )(a, w_t, bias, *ride.args)


ROWS = 256
LEAN_ROWS = 512
MIX_ROWS = 256


def _rstd(xv):
    return lax.rsqrt(jnp.mean(xv * xv, axis=-1, keepdims=True) + EPS)


def _norm_fwd(x, g, name, exchange=None):
    T, D = x.shape
    ride = _Ride(exchange, 2, 1, 0)

    def kern(*refs):
        (x_ref, g_ref), (h_ref,), _, exrefs = ride.split(refs)

        def compute():
            xv = x_ref[...]
            h_ref[...] = ((xv * _rstd(xv)) * g_ref[...]).astype(h_ref.dtype)

        ride.around(pl.program_id(0), T // ROWS, exrefs, compute)

    row = pl.BlockSpec((ROWS, D), lambda i: (i, 0))
    return pl.pallas_call(
        kern, name=name, grid=(T // ROWS,),
        in_specs=[row, pl.BlockSpec((1, D), lambda i: (0, 0))] + ride.in_specs,
        out_specs=[row] + ride.out_specs,
        out_shape=[jax.ShapeDtypeStruct((T, D), BF16)] + ride.out_shapes,
        scratch_shapes=ride.scratch, input_output_aliases=ride.aliases,
        compiler_params=_params("arbitrary"),
    )(x, g, *ride.args)


def _norm_bwd(dh, x, g, res, name, then_w_t=None):
    T, D = x.shape
    rows = PROJ_ROWS if then_w_t is not None else min(LEAN_ROWS, T)

    def kern(dh_ref, x_ref, g_ref, res_ref, *rest):
        if then_w_t is None:
            dx_ref, dg_ref = rest
        else:
            w_ref, dx_ref, dg_ref, dxb_ref, y_ref = rest

        @pl.when(pl.program_id(0) == 0)
        def _():
            dg_ref[...] = jnp.zeros_like(dg_ref)

        xv = x_ref[...]
        r = _rstd(xv)
        xn = xv * r
        dhv = dh_ref[...].astype(F32)
        dg_ref[...] += jnp.sum(dhv * xn, axis=0, keepdims=True)
        t = dhv * g_ref[...]
        dx = res_ref[...] + r * (t - xn * jnp.mean(t * xn, axis=-1, keepdims=True))
        dx_ref[...] = dx
        if then_w_t is not None:
            dxb = dx.astype(BF16)
            dxb_ref[...] = dxb
            y_ref[...] = _dot(dxb, w_ref[...], NT)

    row = pl.BlockSpec((rows, D), lambda i: (i, 0))
    vec = pl.BlockSpec((1, D), lambda i: (0, 0))
    in_specs, args = [row, row, vec, row], [dh, x, g, res]
    out_specs = [row, vec]
    out_shape = [jax.ShapeDtypeStruct((T, D), F32), jax.ShapeDtypeStruct((1, D), F32)]
    if then_w_t is not None:
        N = then_w_t.shape[0]
        in_specs.append(pl.BlockSpec((N, D), lambda i: (0, 0)))
        args.append(then_w_t)
        out_specs += [row, pl.BlockSpec((rows, N), lambda i: (i, 0))]
        out_shape += [jax.ShapeDtypeStruct((T, D), BF16), jax.ShapeDtypeStruct((T, N), F32)]
    return pl.pallas_call(
        kern, name=name, grid=(T // rows,), in_specs=in_specs, out_specs=out_specs, out_shape=out_shape,
        compiler_params=_params("arbitrary"),
    )(*args)


def _loss_head(x3, tgt, g):
    T, D = x3.shape

    def kern(x_ref, t_ref, g_ref, dx_ref, dxb_ref, dg_ref, loss_ref):
        @pl.when(pl.program_id(0) == 0)
        def _():
            dg_ref[...] = jnp.zeros_like(dg_ref)
            loss_ref[...] = jnp.zeros_like(loss_ref)

        xv = x_ref[...]
        gv = g_ref[...]
        r = _rstd(xv)
        xn = xv * r
        err = xn * gv - t_ref[...]
        per_tok = jnp.mean(err * err, axis=-1, keepdims=True)
        loss_ref[...] += 0.5 * jnp.sum(per_tok, axis=0, keepdims=True)
        dy = err * (1.0 / D)
        dg_ref[...] += jnp.sum(dy * xn, axis=0, keepdims=True)
        t = dy * gv
        dx = r * (t - xn * jnp.mean(t * xn, axis=-1, keepdims=True))
        dx_ref[...] = dx
        dxb_ref[...] = dx.astype(BF16)

    row = pl.BlockSpec((LEAN_ROWS, D), lambda i: (i, 0))
    vec = pl.BlockSpec((1, D), lambda i: (0, 0))
    return pl.pallas_call(
        kern, name="loss_head", grid=(T // LEAN_ROWS,),
        in_specs=[row, row, vec],
        out_specs=[row, row, vec, pl.BlockSpec((1, 128), lambda i: (0, 0))],
        out_shape=[jax.ShapeDtypeStruct((T, D), F32), jax.ShapeDtypeStruct((T, D), BF16),
                   jax.ShapeDtypeStruct((1, D), F32), jax.ShapeDtypeStruct((1, 128), F32)],
        compiler_params=_params("arbitrary"),
    )(x3, tgt, g)


def _spread_matrix():
    head_of_lane = np.arange(N_HEADS * HEAD_DIM) // HEAD_DIM
    return jnp.asarray(np.arange(N_HEADS)[:, None] == head_of_lane[None, :], dtype=BF16)


def _pieces(v, n):
    out = []
    for _ in range(n):
        piece = v.astype(BF16)
        out.append(piece)
        v = v - piece.astype(F32)
    return out


def _spread(v, spread):
    return sum(_dot(p, spread, NN) for p in _pieces(v, 2))


def _spread_weights(w1, w2, spread):
    s1, s2 = _spread(w1, spread), _spread(w2, spread)
    return s1, s2, 1.0 - s1 - s2


def _head_sums(v, spread):
    return sum(_dot(p, spread, NT) for p in _pieces(v, 2))


def _branch_weights(l1, l2, l3):
    lm = jnp.maximum(jnp.maximum(l1, l2), l3)
    e1, e2, e3 = jnp.exp(l1 - lm), jnp.exp(l2 - lm), jnp.exp(l3 - lm)
    inv = 1.0 / (e1 + e2 + e3)
    return e1 * inv, e2 * inv, e3 * inv


def _to_token_order(view_ref, dil):
    if dil == 1:
        return view_ref[...].astype(F32)
    n_l, w = view_ref.shape[0], view_ref.shape[1] // dil
    cols = []
    for cb in range(w // LANES):
        by_class = jnp.stack([view_ref[:, r * w + cb * LANES:r * w + (cb + 1) * LANES].astype(F32) for r in range(dil)])
        cols.append(jnp.swapaxes(by_class, 0, 1).reshape(n_l * dil, LANES))
    return jnp.concatenate(cols, axis=1)


def _to_class_order(val, view_ref, dil):
    if dil == 1:
        view_ref[...] = val.astype(view_ref.dtype)
        return
    n, w = val.shape
    for cb in range(w // LANES):
        by_class = jnp.swapaxes(val[:, cb * LANES:(cb + 1) * LANES].reshape(n // dil, dil, LANES), 0, 1)
        for r in range(dil):
            view_ref[:, r * w + cb * LANES:r * w + (cb + 1) * LANES] = by_class[r].astype(view_ref.dtype)


def _view_spec(rows, width, dil):
    return pl.BlockSpec((rows // dil, dil * width), lambda i: (i, 0))


def _mix_fwd(oa, obs, lbs, ga, gb, dils):
    T, W = oa.shape

    def kern(oa_ref, o1, o2, o3, l1, l2, l3, ga_ref, gb_ref, sp_ref, mix_ref):
        sp = sp_ref[...]
        w1, w2, w3 = _branch_weights(l1[...], l2[...], l3[...])
        on = [_to_token_order(o, d) for o, d in zip((o1, o2, o3), dils)]
        s1, s2, s3 = _spread_weights(w1, w2, sp)
        ob = s1 * on[0] + s2 * on[1] + s3 * on[2]
        oav = oa_ref[...]
        mix_ref[:, :W] = ((oav * _rstd(oav)) * ga_ref[...]).astype(BF16)
        mix_ref[:, W:] = ((ob * _rstd(ob)) * gb_ref[...]).astype(BF16)

    row = pl.BlockSpec((LEAN_ROWS, W), lambda i: (i, 0))
    per_head = pl.BlockSpec((LEAN_ROWS, N_HEADS), lambda i: (i, 0))
    vec = pl.BlockSpec((1, W), lambda i: (0, 0))
    return pl.pallas_call(
        kern, name="mix_fwd", grid=(T // LEAN_ROWS,),
        in_specs=[row] + [_view_spec(LEAN_ROWS, W, d) for d in dils] + [per_head] * 3
        + [vec, vec, pl.BlockSpec((N_HEADS, W), lambda i: (0, 0))],
        out_specs=pl.BlockSpec((LEAN_ROWS, 2 * W), lambda i: (i, 0)),
        out_shape=jax.ShapeDtypeStruct((T, 2 * W), BF16),
        compiler_params=_params("parallel"),
    )(oa, *obs, *lbs, ga, gb, _spread_matrix())


def _mix_bwd(dmix, oa, obs, lbs, ga, gb, dils, exchange=None):
    T, W = oa.shape
    ride = _Ride(exchange, 11, 10, 0)

    def kern(*refs):
        ins, outs, _, exrefs = ride.split(refs)
        ride.around(pl.program_id(0), T // MIX_ROWS, exrefs, lambda: compute(*ins, *outs))

    def compute(dm_ref, oa_ref, o1, o2, o3, l1, l2, l3, ga_ref, gb_ref, sp_ref,
                doa_ref, da_ref, do1, do2, do3, d1, d2, d3, dga_ref, dgb_ref):
        @pl.when(pl.program_id(0) == 0)
        def _():
            dga_ref[...] = jnp.zeros_like(dga_ref)
            dgb_ref[...] = jnp.zeros_like(dgb_ref)

        sp = sp_ref[...]
        oav = oa_ref[...]
        r = _rstd(oav)
        on = oav * r
        dy = dm_ref[:, :W]
        dga_ref[...] += jnp.sum(dy * on, axis=0, keepdims=True)
        t = dy * ga_ref[...]
        doa = r * (t - on * jnp.mean(t * on, axis=-1, keepdims=True))
        doa_ref[...] = doa.astype(BF16)
        da_ref[...] = _head_sums(doa * oav, sp)
        w1, w2, w3 = _branch_weights(l1[...], l2[...], l3[...])
        s1, s2, s3 = _spread_weights(w1, w2, sp)
        on = [_to_token_order(o, d) for o, d in zip((o1, o2, o3), dils)]
        ob = s1 * on[0] + s2 * on[1] + s3 * on[2]
        r = _rstd(ob)
        on = ob * r
        dy = dm_ref[:, W:]
        dgb_ref[...] += jnp.sum(dy * on, axis=0, keepdims=True)
        t = dy * gb_ref[...]
        dob = r * (t - on * jnp.mean(t * on, axis=-1, keepdims=True))
        c = _head_sums(dob * ob, sp)
        for do_ref, sn, d in zip((do1, do2, do3), (s1, s2, s3), dils):
            _to_class_order(sn * dob, do_ref, d)
        d1[...] = w1 * c
        d2[...] = w2 * c
        d3[...] = w3 * c

    row = pl.BlockSpec((MIX_ROWS, W), lambda i: (i, 0))
    per_head = pl.BlockSpec((MIX_ROWS, N_HEADS), lambda i: (i, 0))
    vec = pl.BlockSpec((1, W), lambda i: (0, 0))
    bf = jax.ShapeDtypeStruct((T, W), BF16)
    ph = jax.ShapeDtypeStruct((T, N_HEADS), F32)
    vv = jax.ShapeDtypeStruct((1, W), F32)
    views = [_view_spec(MIX_ROWS, W, d) for d in dils]
    return pl.pallas_call(
        kern, name="mix_bwd", grid=(T // MIX_ROWS,),
        in_specs=[pl.BlockSpec((MIX_ROWS, 2 * W), lambda i: (i, 0)), row] + views + [per_head] * 3 + [vec, vec,
                  pl.BlockSpec((N_HEADS, W), lambda i: (0, 0))] + ride.in_specs,
        out_specs=[row, per_head] + views + [per_head, per_head, per_head, vec, vec] + ride.out_specs,
        out_shape=[bf, ph] + [jax.ShapeDtypeStruct(o.shape, F32) for o in obs] + [ph, ph, ph, vv, vv]
        + ride.out_shapes,
        scratch_shapes=ride.scratch,
        input_output_aliases=ride.aliases,
        compiler_params=_params("arbitrary"),
    )(dmix, oa, *obs, *lbs, ga, gb, _spread_matrix(), *ride.args)


def _alibi_slopes(n):
    return np.asarray(2.0 ** (-8.0 * (np.arange(n) + 1) / n)).astype(np.float32)


def _band_bias(max_steps, step_dist):
    qi = np.arange(BLOCK)[None, :]
    kj = np.arange(BLOCK)[:, None]
    slopes = _alibi_slopes(N_HEADS)
    halves = []
    for steps in (qi + BLOCK - kj, qi - kj):
        valid = (steps >= 0) & (steps <= max_steps)
        alibi = slopes[:, None, None] * (step_dist * steps).astype(np.float32)[None]
        halves.append(np.where(valid[None], -alibi, np.float32(NEG_INF)).astype(np.float32))
    per_head = np.concatenate(halves, axis=1)
    return jnp.asarray(np.concatenate([per_head[0::2], per_head[1::2]], axis=2))


class _AttnLayout:
    def __init__(self, dil, kv_heads, q_stride, q_off, k_stride, k_off, v_off):
        self.dil = dil
        self.kv_heads = kv_heads
        self.kw = kv_heads * HEAD_DIM
        self.rep = N_HEADS // kv_heads
        self.q_col = lambda r: r * q_stride + q_off
        self.k_col = lambda r: r * k_stride + k_off
        self.v_col = lambda r: r * k_stride + v_off


QW = N_HEADS * HEAD_DIM
LANES = 128


PAIRS = N_HEADS // 2


def _pair_cols(pair):
    return slice(pair * LANES, (pair + 1) * LANES)


def _first_head_lanes(shape):
    return lax.broadcasted_iota(jnp.int32, shape, 1) < HEAD_DIM


def _split_heads(pair):
    first = _first_head_lanes(pair.shape)
    zero = jnp.zeros_like(pair)
    return jnp.concatenate([jnp.where(first, pair, zero), jnp.where(first, zero, pair)], axis=0)


def _kv_pair(ref, pair, rep):
    if rep == 1:
        return ref[:, _pair_cols(pair)]
    blk = ref[...].astype(F32)
    other = pltpu.roll(blk, HEAD_DIM, 1)
    first = _first_head_lanes(blk.shape)
    both = jnp.where(first, blk, other) if (2 * pair // rep) % 2 == 0 else jnp.where(first, other, blk)
    return both.astype(ref.dtype)


def _paired_kv(prev_ref, cur_ref, rep, transposed=False):
    memo = {}

    def get(pair):
        key = pair if rep == 1 else 2 * pair // rep
        if key not in memo:
            blocks = [_kv_pair(ref, pair, rep) for ref in (prev_ref, cur_ref)]
            memo[key] = jnp.concatenate([b.T for b in blocks], axis=1) if transposed else jnp.concatenate(blocks, axis=0)
        return memo[key]

    return get


def _attn_fwd(proj, bias, sinks, lay, name, exchange=None):
    L = proj.shape[0]
    nb = L // BLOCK
    kw, rep = lay.kw, lay.rep
    use_sinks = sinks is not None
    scale = HEAD_DIM ** -0.5
    ride = _Ride(exchange, 7 if use_sinks else 6, 2, 2)

    def kern(*refs):
        ins, (o_ref, l_ref), (sc_ref, pr_ref), exrefs = ride.split(refs)
        q_ref, kc_ref, kp_ref, vc_ref, vp_ref, b_ref = ins[:6]
        s_ref = ins[6] if use_sinks else None
        r, i = pl.program_id(0), pl.program_id(1)
        first = i == 0
        ride.around(r * nb + i, lay.dil * nb, exrefs,
                    lambda: compute(q_ref, kc_ref, kp_ref, vc_ref, vp_ref, b_ref, s_ref, o_ref, l_ref, first,
                                    sc_ref, pr_ref))

    def compute(q_ref, kc_ref, kp_ref, vc_ref, vp_ref, b_ref, s_ref, o_ref, l_ref, first, sc_ref, pr_ref):
        keys, values_t = _paired_kv(kp_ref, kc_ref, rep), _paired_kv(vp_ref, vc_ref, rep, transposed=True)
        for pair in range(PAIRS):
            qs = _split_heads(q_ref[:, _pair_cols(pair)])
            s_prev = _dot(keys(pair)[:BLOCK], qs, NT) * scale + b_ref[pair, :BLOCK]
            sc_ref[pair, :BLOCK] = jnp.where(first, NEG_INF, s_prev)
            sc_ref[pair, BLOCK:] = _dot(keys(pair)[BLOCK:], qs, NT) * scale + b_ref[pair, BLOCK:]
        inv = []
        for h in range(N_HEADS):
            cols = slice(h % 2 * BLOCK, (h % 2 + 1) * BLOCK)
            s = sc_ref[h // 2, :, cols]
            m = jnp.max(s, axis=0, keepdims=True)
            if use_sinks:
                sink = s_ref[:, h:h + 1]
                m = jnp.maximum(m, sink)
            p = jnp.exp(s - m)
            denom = jnp.sum(p, axis=0, keepdims=True)
            if use_sinks:
                denom = denom + jnp.exp(sink - m)
            pr_ref[h // 2, :, cols] = p.astype(BF16)
            l_ref[h:h + 1, :] = m + jnp.log(denom)
            inv.append(1.0 / denom)
        for pair in range(PAIRS):
            both = _dot(values_t(pair), pr_ref[pair], NN)
            o_t = jnp.concatenate([both[:HEAD_DIM, :BLOCK] * inv[2 * pair], both[HEAD_DIM:, BLOCK:] * inv[2 * pair + 1]],
                                  axis=0)
            o_ref[:, _pair_cols(pair)] = o_t.T

    prev = lambda i: jnp.maximum(i - 1, 0)
    in_specs = [
        pl.BlockSpec((BLOCK, QW), lambda r, i: (i, lay.q_col(r))),
        pl.BlockSpec((BLOCK, kw), lambda r, i: (i, lay.k_col(r))),
        pl.BlockSpec((BLOCK, kw), lambda r, i: (prev(i), lay.k_col(r))),
        pl.BlockSpec((BLOCK, kw), lambda r, i: (i, lay.v_col(r))),
        pl.BlockSpec((BLOCK, kw), lambda r, i: (prev(i), lay.v_col(r))),
        pl.BlockSpec((PAIRS, 2 * BLOCK, 2 * BLOCK), lambda r, i: (0, 0, 0)),
    ]
    args = [proj, proj, proj, proj, proj, bias]
    if use_sinks:
        in_specs.append(pl.BlockSpec((1, N_HEADS), lambda r, i: (0, 0)))
        args.append(sinks)
    out_specs = [pl.BlockSpec((BLOCK, QW), lambda r, i: (i, r)),
                 pl.BlockSpec((None, N_HEADS, BLOCK), lambda r, i: (r, 0, i))]
    out_shape = [jax.ShapeDtypeStruct((L, lay.dil * QW), F32), jax.ShapeDtypeStruct((lay.dil, N_HEADS, L), F32)]
    return pl.pallas_call(
        kern, name=name, grid=(lay.dil, nb),
        in_specs=in_specs + ride.in_specs, out_specs=out_specs + ride.out_specs,
        out_shape=out_shape + ride.out_shapes,
        scratch_shapes=[pltpu.VMEM((PAIRS, 2 * BLOCK, 2 * BLOCK), dt) for dt in (F32, BF16)] + ride.scratch,
        input_output_aliases=ride.aliases,
        compiler_params=_params("arbitrary", "arbitrary"),
    )(*args, *ride.args)


def _attn_bwd(proj, do, lse, dd, bias, sinks, lay, name, exchange=None):
    L = proj.shape[0]
    nb = L // BLOCK
    kw, rep = lay.kw, lay.rep
    assert rep == 1 or lay.kv_heads == 2, "grouped queries: the two kv heads fill one 128-lane block"
    use_sinks = sinks is not None
    scale = HEAD_DIM ** -0.5
    ride = _Ride(exchange, 10 if use_sinks else 9, 4 if use_sinks else 3, 6)

    def kern(*refs):
        ins, outs, (ck_ref, cv_ref, *staged), exrefs = ride.split(refs)
        q_ref, kc_ref, kp_ref, vc_ref, vp_ref, do_ref, l_ref, d_ref, b_ref = ins[:9]
        s_ref = ins[9] if use_sinks else None
        dq_ref, dk_ref, dv_ref = outs[:3]
        ds_ref = outs[3] if use_sinks else None
        r = pl.program_id(0)
        i = pl.program_id(1)
        ride.around(r * (nb + 1) + i, lay.dil * (nb + 1), exrefs,
                    lambda: compute(q_ref, kc_ref, kp_ref, vc_ref, vp_ref, do_ref, l_ref, d_ref, b_ref, s_ref,
                                    dq_ref, dk_ref, dv_ref, ds_ref, ck_ref, cv_ref, r, i, *staged))

    def compute(q_ref, kc_ref, kp_ref, vc_ref, vp_ref, do_ref, l_ref, d_ref, b_ref, s_ref,
                dq_ref, dk_ref, dv_ref, ds_ref, ck_ref, cv_ref, r, i, sc_ref, dp_ref, pr_ref, dsc_ref):
        first = i == 0

        @pl.when(first)
        def _():
            ck_ref[...] = jnp.zeros_like(ck_ref)
            cv_ref[...] = jnp.zeros_like(cv_ref)

        if use_sinks:
            @pl.when(first & (r == 0))
            def _():
                ds_ref[...] = jnp.zeros_like(ds_ref)

        @pl.when(i < nb)
        def _():
            keys, values = _paired_kv(kp_ref, kc_ref, rep), _paired_kv(vp_ref, vc_ref, rep)
            keys_t = _paired_kv(kp_ref, kc_ref, rep, transposed=True)
            for pair in range(PAIRS):
                qs = _split_heads(q_ref[:, _pair_cols(pair)])
                dos = _split_heads(do_ref[:, _pair_cols(pair)].astype(BF16))
                s = _dot(keys(pair), qs, NT) * scale + b_ref[pair]
                sc_ref[pair, :BLOCK] = jnp.where(first, NEG_INF, s[:BLOCK])
                sc_ref[pair, BLOCK:] = s[BLOCK:]
                dp_ref[pair] = _dot(values(pair), dos, NT)
            for h in range(N_HEADS):
                cols = slice(h % 2 * BLOCK, (h % 2 + 1) * BLOCK)
                lrow = l_ref[h:h + 1, :]
                drow = d_ref[h:h + 1, :]
                p = jnp.exp(sc_ref[h // 2, :, cols] - lrow)
                pr_ref[h // 2, :, cols] = p.astype(BF16)
                dsc_ref[h // 2, :, cols] = (p * (dp_ref[h // 2, :, cols] - drow) * scale).astype(BF16)
                if use_sinks:
                    ds_ref[h:h + 1, :] += -(jnp.exp(s_ref[:, h:h + 1] - lrow) * drow)
            grouped = {}
            for pair in range(PAIRS):
                cols = _pair_cols(pair)
                qs = _split_heads(q_ref[:, cols])
                dos = _split_heads(do_ref[:, cols].astype(BF16))
                ds = dsc_ref[pair]
                both = _dot(keys_t(pair), ds, NN)
                dq_t = jnp.concatenate([both[:HEAD_DIM, :BLOCK], both[HEAD_DIM:, BLOCK:]], axis=0)
                dq_ref[:, cols] = dq_t.T.astype(dq_ref.dtype)
                dk = _dot(ds, qs, NN)
                dv = _dot(pr_ref[pair], dos, NN)
                if rep == 1:
                    dk_ref[:, cols] = (ck_ref[:, cols] + dk[:BLOCK]).astype(dk_ref.dtype)
                    dv_ref[:, cols] = (cv_ref[:, cols] + dv[:BLOCK]).astype(dv_ref.dtype)
                    ck_ref[:, cols] = dk[BLOCK:]
                    cv_ref[:, cols] = dv[BLOCK:]
                else:
                    g = 2 * pair // rep
                    grouped[g] = (dk, dv) if g not in grouped else (grouped[g][0] + dk, grouped[g][1] + dv)
            if rep > 1:
                fold = lambda t: t + pltpu.roll(t, HEAD_DIM, 1)
                first_half = _first_head_lanes((2 * BLOCK, LANES))
                dk = jnp.where(first_half, fold(grouped[0][0]), fold(grouped[1][0]))
                dv = jnp.where(first_half, fold(grouped[0][1]), fold(grouped[1][1]))
                dk_ref[...] = (ck_ref[...] + dk[:BLOCK]).astype(dk_ref.dtype)
                dv_ref[...] = (cv_ref[...] + dv[:BLOCK]).astype(dv_ref.dtype)
                ck_ref[...] = dk[BLOCK:]
                cv_ref[...] = dv[BLOCK:]

        @pl.when(i == nb)
        def _():
            dk_ref[...] = ck_ref[...].astype(dk_ref.dtype)
            dv_ref[...] = cv_ref[...].astype(dv_ref.dtype)
            if use_sinks:
                @pl.when(r == lay.dil - 1)
                def _():
                    ds_ref[...] = jnp.broadcast_to(jnp.sum(ds_ref[...], axis=1, keepdims=True), ds_ref.shape)

    cur = lambda i: jnp.minimum(i, nb - 1)
    prev = lambda i: jnp.maximum(jnp.minimum(i, nb - 1) - 1, 0)
    done = lambda i: jnp.maximum(i - 1, 0)
    qspec = lambda col: pl.BlockSpec((BLOCK, QW), lambda r, i: (cur(i), col(r)))
    per_head = pl.BlockSpec((None, N_HEADS, BLOCK), lambda r, i: (r, 0, cur(i)))
    in_specs = [
        qspec(lay.q_col),
        pl.BlockSpec((BLOCK, kw), lambda r, i: (cur(i), lay.k_col(r))),
        pl.BlockSpec((BLOCK, kw), lambda r, i: (prev(i), lay.k_col(r))),
        pl.BlockSpec((BLOCK, kw), lambda r, i: (cur(i), lay.v_col(r))),
        pl.BlockSpec((BLOCK, kw), lambda r, i: (prev(i), lay.v_col(r))),
        qspec(lambda r: r), per_head, per_head,
        pl.BlockSpec((PAIRS, 2 * BLOCK, 2 * BLOCK), lambda r, i: (0, 0, 0)),
    ]
    args = [proj, proj, proj, proj, proj, do, lse, dd, bias]
    out_specs = [
        qspec(lambda r: r),
        pl.BlockSpec((BLOCK, kw), lambda r, i: (done(i), r)),
        pl.BlockSpec((BLOCK, kw), lambda r, i: (done(i), r)),
    ]
    dkv_shape = jax.ShapeDtypeStruct((L, lay.dil * kw), BF16)
    out_shape = [jax.ShapeDtypeStruct((L, lay.dil * QW), BF16), dkv_shape, dkv_shape]
    if use_sinks:
        in_specs.append(pl.BlockSpec((1, N_HEADS), lambda r, i: (0, 0)))
        args.append(sinks)
        out_specs.append(pl.BlockSpec((N_HEADS, LANES), lambda r, i: (0, 0)))
        out_shape.append(jax.ShapeDtypeStruct((N_HEADS, LANES), F32))
    return pl.pallas_call(
        kern, name=name, grid=(lay.dil, nb + 1),
        in_specs=in_specs + ride.in_specs, out_specs=out_specs + ride.out_specs,
        out_shape=out_shape + ride.out_shapes,
        scratch_shapes=[pltpu.VMEM((BLOCK, kw), F32), pltpu.VMEM((BLOCK, kw), F32)]
        + [pltpu.VMEM((PAIRS, 2 * BLOCK, 2 * BLOCK), dt) for dt in (F32, F32, BF16, BF16)] + ride.scratch,
        input_output_aliases=ride.aliases,
        compiler_params=_params("arbitrary", "arbitrary"),
    )(*args, *ride.args)


def _assemble(groups, name, dils=(1,)):
    T = groups[0][0].shape[0] * dils[0]
    widths = [g[0].shape[1] // dils[0] for g in groups]
    total = sum(widths)
    flat = [a for g in groups for a in g]
    member_dils = [d for g in groups for d in dils[:len(g)]]

    def kern(*refs):
        ins = refs[:len(flat)]
        out_ref, cs_ref = refs[len(flat):]

        @pl.when(pl.program_id(0) == 0)
        def _():
            cs_ref[...] = jnp.zeros_like(cs_ref)

        pos = off = 0
        for g, w in zip(groups, widths):
            acc = _to_token_order(ins[pos], dils[0])
            for j in range(1, len(g)):
                acc = acc + _to_token_order(ins[pos + j], dils[j])
            pos += len(g)
            out_ref[:, off:off + w] = acc.astype(BF16)
            cs_ref[:, off:off + w] += jnp.sum(acc, axis=0, keepdims=True)
            off += w

    return pl.pallas_call(
        kern, name=name, grid=(T // LEAN_ROWS,),
        in_specs=[_view_spec(LEAN_ROWS, a.shape[1] // d, d) for a, d in zip(flat, member_dils)],
        out_specs=[pl.BlockSpec((LEAN_ROWS, total), lambda i: (i, 0)), pl.BlockSpec((1, total), lambda i: (0, 0))],
        out_shape=[jax.ShapeDtypeStruct((T, total), BF16), jax.ShapeDtypeStruct((1, total), F32)],
        compiler_params=_params("arbitrary"),
    )(*flat)


def _adamw(w, g, m, v, name):
    _, R, C = w.shape
    rows = min(R, ROWS)
    assert R % rows == 0

    def kern(w_ref, g_ref, m_ref, v_ref, d_ref, nm_ref, nv_ref):
        gv = g_ref[...]
        mn = ADAM_B1 * m_ref[...] + (1.0 - ADAM_B1) * gv
        vn = ADAM_B2 * v_ref[...] + (1.0 - ADAM_B2) * jnp.square(gv)
        m_hat = mn / (1.0 - ADAM_B1 ** ADAM_STEP)
        v_hat = vn / (1.0 - ADAM_B2 ** ADAM_STEP)
        d_ref[...] = -ADAM_LR * (m_hat / (jnp.sqrt(v_hat) + ADAM_EPS) + ADAM_WD * w_ref[...])
        nm_ref[...] = mn
        nv_ref[...] = vn

    blk = pl.BlockSpec((None, rows, C), lambda i: (0, i, 0))
    shp = jax.ShapeDtypeStruct((1, R, C), F32)
    return pl.pallas_call(
        kern, name=name, grid=(R // rows,),
        in_specs=[blk, pl.BlockSpec((rows, C), lambda i: (i, 0)), blk, blk], out_specs=[blk] * 3, out_shape=[shp] * 3,
        compiler_params=_params("parallel"),
    )(w, g, m, v)


FUSED_ROWS = 128


def _adamw_from_slots(w, slot_parts, m, v, name):
    _, R, C = w.shape
    n_parts = len(slot_parts)
    steps = R // n_parts // FUSED_ROWS
    assert steps * n_parts * FUSED_ROWS == R

    def kern(w_ref, m_ref, v_ref, *rest):
        part_refs, (g_ref, d_ref, nm_ref, nv_ref) = rest[:n_parts], rest[n_parts:]
        i = pl.program_id(0)
        gv = jnp.zeros((FUSED_ROWS, C), F32)
        for q, s_ref in enumerate(part_refs):
            acc = s_ref[0].astype(F32)
            for k in range(1, N_DEV):
                acc = acc + s_ref[k].astype(F32)
            gv = acc if n_parts == 1 else jnp.where(i // steps == q, acc, gv)
        g_ref[...] = gv
        mn = ADAM_B1 * m_ref[...] + (1.0 - ADAM_B1) * gv
        vn = ADAM_B2 * v_ref[...] + (1.0 - ADAM_B2) * jnp.square(gv)
        m_hat = mn / (1.0 - ADAM_B1 ** ADAM_STEP)
        v_hat = vn / (1.0 - ADAM_B2 ** ADAM_STEP)
        d_ref[...] = -ADAM_LR * (m_hat / (jnp.sqrt(v_hat) + ADAM_EPS) + ADAM_WD * w_ref[...])
        nm_ref[...] = mn
        nv_ref[...] = vn

    blk = pl.BlockSpec((None, FUSED_ROWS, C), lambda i: (0, i, 0))
    parts = [pl.BlockSpec((N_DEV, FUSED_ROWS, C), lambda i, q=q: (0, jnp.clip(i - q * steps, 0, steps - 1), 0))
             for q in range(n_parts)]
    shp = jax.ShapeDtypeStruct((1, R, C), F32)
    return pl.pallas_call(
        kern, name=name, grid=(R // FUSED_ROWS,),
        in_specs=[blk, blk, blk] + parts, out_specs=[blk] * 4, out_shape=[shp] * 4,
        compiler_params=_params("parallel"),
    )(w, m, v, *slot_parts)


def _sum_slots(slots, name):
    n, R, C = slots.shape
    SUM_ROWS = next(rows for rows in (256, 128, 64, 32, 16) if R % rows == 0)

    def kern(s_ref, o_ref):
        acc = s_ref[0].astype(F32)
        for k in range(1, n):
            acc = acc + s_ref[k].astype(F32)
        o_ref[...] = acc

    return pl.pallas_call(
        kern, name=name, grid=(R // SUM_ROWS,),
        in_specs=[pl.BlockSpec((n, SUM_ROWS, C), lambda i: (0, i, 0))],
        out_specs=pl.BlockSpec((SUM_ROWS, C), lambda i: (i, 0)),
        out_shape=jax.ShapeDtypeStruct((R, C), F32),
        compiler_params=_params("parallel"),
    )(slots)


def _place():
    return lax.axis_index("x"), lax.axis_index("y"), lax.axis_index("c")


def _index(p):
    return 4 * p[0] + 2 * p[1] + p[2]


FLIPS = [(fx, fy, fc) for fx in (0, 1) for fy in (0, 1) for fc in (0, 1)][1:]


def _peer(me, flip):
    return tuple(1 - a if f else a for a, f in zip(me, flip))


def _gather_rows(shards, part=(0, 1), into=None, relay_at=RELAY_AT):
    nw = len(shards)

    def plan(ins, outs, send_sems, recv_sems):
        x, y, c = me = _place()
        sibling = (x, y, 1 - c)
        chips = [(1 - x, y), (x, 1 - y), (1 - x, 1 - y)]

        def span(w):
            cnt = ins[w].shape[0] // part[1]
            return part[0] * cnt, cnt

        def rows(w, p):
            lo, cnt = span(w)
            return outs[w].at[pl.ds(_index(p) * ins[w].shape[0] + lo, cnt), :]

        def own(w):
            lo, cnt = span(w)
            return ins[w].at[pl.ds(lo, cnt), :]

        def copy(w, k, block, to):
            return pltpu.make_async_remote_copy(
                src_ref=own(w) if block is me else rows(w, block), dst_ref=rows(w, block),
                send_sem=send_sems.at[7 * w + k], recv_sem=recv_sems.at[7 * w + k],
                device_id=to, device_id_type=MESH)

        return me, sibling, chips, c, rows, own, copy

    def copies(ins, outs, send_sems, recv_sems, local_sems):
        me, sibling, chips, c, rows, own, copy = plan(ins, outs, send_sems, recv_sems)
        local = [pltpu.make_async_copy(own(w), rows(w, me), local_sems.at[w]) for w in range(nw)]
        sends, recvs = [], []
        for w in range(nw):
            sends.append(copy(w, 0, me, sibling))
            sends += [copy(w, 1 + j, me, (*chip, c)) for j, chip in enumerate(chips)]
            recvs.append(copy(w, 0, sibling, me))
            recvs += [copy(w, 4 + j, (*chip, 1 - c), me) for j, chip in enumerate(chips)]
        return local, sends, recvs

    def relay(ins, outs, send_sems, recv_sems, local_sems):
        me, sibling, chips, c, rows, own, copy = plan(ins, outs, send_sems, recv_sems)
        arrived = [copy(w, 1 + j, (*chip, c), me) for w in range(nw) for j, chip in enumerate(chips)]
        onward = [copy(w, 4 + j, (*chip, c), sibling) for w in range(nw) for j, chip in enumerate(chips)]
        return arrived, onward

    shapes = [jax.ShapeDtypeStruct((N_DEV * s.shape[0], s.shape[1]), s.dtype) for s in shards]
    aliases = {nw + w: w for w in range(nw)} if into else None
    return _Exchange(shards + (into or []), shapes, 7 * nw, nw, copies, aliases=aliases, relay=relay,
                     relay_at=relay_at)


def _scatter_rows(parts, part=(0, 1)):
    nw = len(parts)

    def copies(ins, outs, send_sems, recv_sems, local_sems):
        me = _place()

        def src(w, owner):
            n = ins[w].shape[0] // N_DEV
            cnt = n // part[1]
            return ins[w].at[pl.ds(_index(owner) * n + part[0] * cnt, cnt), :]

        def copy(k, w, owner, sender, to):
            return pltpu.make_async_remote_copy(
                src_ref=src(w, owner), dst_ref=outs[w].at[_index(sender)],
                send_sem=send_sems.at[nw * k + w], recv_sem=recv_sems.at[nw * k + w],
                device_id=to, device_id_type=MESH)

        local = [pltpu.make_async_copy(src(w, me), outs[w].at[_index(me)], local_sems.at[w]) for w in range(nw)]
        peers = [_peer(me, flip) for flip in FLIPS]
        sends = [copy(k, w, peer, me, peer) for k, peer in enumerate(peers) for w in range(nw)]
        recvs = [copy(k, w, me, peer, me) for k, peer in enumerate(peers) for w in range(nw)]
        return local, sends, recvs

    shapes = [jax.ShapeDtypeStruct((N_DEV, p.shape[0] // N_DEV // part[1], p.shape[1]), p.dtype) for p in parts]
    return _Exchange(parts, shapes, 7 * nw, nw, copies)


def _sum_over_devices(v):
    shape = v.shape

    def body(v_ref, sum_ref, all_ref, send_sems, recv_sems):
        me = _place()
        all_ref[_index(me)] = v_ref[...]
        sends = []
        for k, flip in enumerate(FLIPS):
            peer = _peer(me, flip)
            sends.append(pltpu.make_async_remote_copy(
                src_ref=v_ref, dst_ref=all_ref.at[_index(me)],
                send_sem=send_sems.at[k], recv_sem=recv_sems.at[k], device_id=peer, device_id_type=MESH))
            sends[-1].start()
        for k, flip in enumerate(FLIPS):
            peer = _peer(me, flip)
            pltpu.make_async_remote_copy(
                src_ref=v_ref, dst_ref=all_ref.at[_index(peer)],
                send_sem=send_sems.at[k], recv_sem=recv_sems.at[k], device_id=peer, device_id_type=MESH).wait_recv()
        for cp in sends:
            cp.wait_send()
        acc = all_ref[0]
        for s in range(1, N_DEV):
            acc = acc + all_ref[s]
        sum_ref[...] = acc

    vmem = pl.BlockSpec(memory_space=pltpu.VMEM)
    return pl.pallas_call(
        body, name="sum_small_grads",
        in_specs=[vmem], out_specs=[vmem, vmem],
        out_shape=[jax.ShapeDtypeStruct(shape, F32), jax.ShapeDtypeStruct((N_DEV,) + shape, F32)],
        scratch_shapes=[pltpu.SemaphoreType.DMA((7,)), pltpu.SemaphoreType.DMA((7,))],
    )(v)[0]


SMALL_ROWS = 8


def _pack_small(vectors):
    padded = []
    for vec in vectors:
        vec = vec.reshape(-1)
        padded.append(jnp.pad(vec, (0, -vec.shape[0] % 128)))
    flat = jnp.concatenate(padded)
    flat = jnp.pad(flat, (0, -flat.shape[0] % (SMALL_ROWS * 128)))
    return flat.reshape(SMALL_ROWS, -1)


def _unpack_small(packed, shapes):
    flat = packed.reshape(-1)
    out, off = [], 0
    for shp in shapes:
        n = int(np.prod(shp))
        out.append(flat[off:off + n].reshape(shp))
        off += n + (-n % 128)
    return out


def kernel(x, g_attn, w_in, b_in, sinks_a, g_out_a, g_out_b, w_out, g_mlp, w_1, w_2, g_final, loss_target, m_g_attn, m_w_in, m_b_in, m_sinks_a, m_g_out_a, m_g_out_b, m_w_out, m_g_mlp, m_w_1, m_w_2, m_g_final, v_g_attn, v_w_in, v_b_in, v_sinks_a, v_g_out_a, v_g_out_b, v_w_out, v_g_mlp, v_w_1, v_w_2, v_g_final):
    xs, tgt = x[0], loss_target[0]
    T, D = xs.shape
    n_a = QW + 2 * KV_HEADS_A * HEAD_DIM
    g_fin = g_final.reshape(1, D)

    shards = [w_in[0].T.astype(BF16), w_out[0].astype(BF16), w_1[0].T.astype(BF16), w_2[0].astype(BF16)]
    ident = lambda acc: (acc,)
    add = lambda acc, other: (acc + other,)

    h1, w_in_t = _norm_fwd(xs, g_attn, "norm_attn", exchange=_gather_rows(shards[:1], relay_at=1.0))
    n_in = w_in_t.shape[0]
    proj_a, = _proj_views(h1, w_in_t, b_in, (0, n_a), [1], "proj_a")
    dils = [dil for _, dil in DILATED_BRANCHES]
    *proj_b, w_o = _proj_views(h1, w_in_t, b_in, (n_a, n_in - n_a), dils, "proj_b", exchange=_gather_rows(shards[1:2]))

    lay_a = _AttnLayout(1, KV_HEADS_A, 0, 0, 0, QW // (KV_HEADS_A * HEAD_DIM), QW // (KV_HEADS_A * HEAD_DIM) + 1)
    bias_a = _band_bias(WINDOW_A - 1, 1)
    o_a, l_a, w_1_t = _attn_fwd(proj_a, bias_a, sinks_a, lay_a, "attn_a_fwd",
                                exchange=_gather_rows(shards[2:3], part=(0, 4)))
    branches = []
    for n, (window, dil) in enumerate(DILATED_BRANCHES):
        lay = _AttnLayout(dil, N_HEADS, 3, 0, 3, 1, 2)
        bias = _band_bias(window // dil, dil)
        ride = _gather_rows(shards[2:3], part=(n + 1, 4), into=[w_1_t])
        o, lse, w_1_t = _attn_fwd(proj_b[n], bias, None, lay, f"attn_b{dil}_fwd", exchange=ride)
        branches.append((lay, bias, proj_b[n], o, lse))
    o_b = [br[3] for br in branches]
    l_b = [br[4].transpose(2, 0, 1).reshape(T, N_HEADS) for br in branches]

    mix = _mix_fwd(o_a, o_b, l_b, g_out_a, g_out_b, dils)
    def residual_and_norm(acc, res, g):
        x_new = acc + res
        return x_new, (x_new * _rstd(x_new)) * g

    assert TILE_WHOLE_ROWS["tn"] == D
    x2, h2 = _matmul(mix, w_o, "nn", [F32, BF16], residual_and_norm, tk=D, tile_ins=[xs], row_ins=[g_mlp],
                     name="out_proj", **TILE_WHOLE_ROWS)

    def relu_sq(acc):
        u = jnp.maximum(acc, 0.0)
        return u, u * u

    u, u_sq, w_2_f = _matmul(h2, w_1_t, "nt", [BF16, BF16], relu_sq, tk=D, name="mlp_up",
                             exchange=_gather_rows(shards[3:]), **TILE_WHOLE_ROWS)
    x3, = _matmul(u_sq, w_2_f, "nn", [F32], add, tk=4096, tile_ins=[x2], name="mlp_down", **TILE)

    dx3, dx3_b, dg_final, loss_dev = _loss_head(x3, tgt, g_fin)

    d_pre, = _matmul(dx3_b, w_2_f, "nt", [BF16], lambda acc, uu: (acc * (2.0 * uu.astype(F32)),),
                     tk=D, tile_ins=[u], name="mlp_down_bwd", **TILE_WHOLE_ROWS)
    dw_2, = _matmul(u_sq, dx3_b, "tn", [BF16], ident, name="mlp_down_wgrad", **TILE_WGRAD)
    dh2, slots_2a = _matmul(d_pre, w_1_t, "nn", [BF16], ident, tk=4096, name="mlp_up_bwd",
                            exchange=_scatter_rows([dw_2], part=(0, 2)), **TILE)
    dw_1_t, slots_2b = _matmul(d_pre, h2, "tn", [BF16], ident, name="mlp_up_wgrad",
                               exchange=_scatter_rows([dw_2], part=(1, 2)), **TILE_WGRAD)
    dx2, dg_mlp, dx2_b, dmix = _norm_bwd(dh2, x2, g_mlp, dx3, "norm_mlp_bwd", then_w_t=w_o)
    dw_o, = _matmul(mix, dx2_b, "tn", [BF16], ident, name="out_proj_wgrad", **TILE_WGRAD)
    do_a, dd_a, do1, do2, do3, dd1, dd2, dd3, dg_out_a, dg_out_b, slots_o = _mix_bwd(
        dmix, o_a, o_b, l_b, g_out_a, g_out_b, dils, exchange=_scatter_rows([dw_o]))

    by_class = lambda d, dil: d.reshape(T // dil, dil, N_HEADS).transpose(1, 2, 0)
    slots_1 = [None] * 4
    dq_a, dk_a, dv_a, dsinks, slots_1[0] = _attn_bwd(proj_a, do_a, l_a, by_class(dd_a, 1), bias_a, sinks_a, lay_a,
                                                     "attn_a_bwd", exchange=_scatter_rows([dw_1_t], part=(0, 4)))
    dsinks = dsinks[:, 0].reshape(1, N_HEADS)
    dqs, dks, dvs = [], [], []
    for n, ((lay, bias, view, _, lse), do_n, dd_n) in enumerate(zip(branches, (do1, do2, do3), (dd1, dd2, dd3))):
        dq, dk, dv, slots_1[n + 1] = _attn_bwd(view, do_n, lse, by_class(dd_n, lay.dil), bias, None, lay,
                                               f"attn_b{lay.dil}_bwd",
                                               exchange=_scatter_rows([dw_1_t], part=(n + 1, 4)))
        dqs.append(dq)
        dks.append(dk)
        dvs.append(dv)
    dproj, db_in = _assemble([[dq_a], [dk_a], [dv_a], dqs, dks, dvs], "dproj", dils)

    dw_in_t, = _matmul(dproj, h1, "tn", [BF16], ident, tm=n_in // 2, tn=1024, tk=1024, name="in_proj_wgrad")
    dh1, slots_in = _matmul(dproj, w_in_t, "nn", [BF16], ident, tk=n_in, name="in_proj_bwd",
                            exchange=_scatter_rows([dw_in_t]), **TILE)
    dx, dg_attn = _norm_bwd(dh1, xs, g_attn, dx2, "norm_attn_bwd")

    g_w_in = _sum_slots(slots_in, "sum_w_in_grads").T
    g_w_1 = jnp.concatenate([_sum_slots(s, f"sum_w_1_grads_{n}") for n, s in enumerate(slots_1)]).T
    g_w_out, *step_w_out = _adamw_from_slots(w_out, [slots_o], m_w_out, v_w_out, "adamw_w_out")
    g_w_2, *step_w_2 = _adamw_from_slots(w_2, [slots_2a, slots_2b], m_w_2, v_w_2, "adamw_w_2")

    small_w = [g_attn, b_in, sinks_a, g_out_a, g_out_b, g_mlp, g_final]
    small_m = [m_g_attn, m_b_in, m_sinks_a, m_g_out_a, m_g_out_b, m_g_mlp, m_g_final]
    small_v = [v_g_attn, v_b_in, v_sinks_a, v_g_out_a, v_g_out_b, v_g_mlp, v_g_final]
    small_g = [dg_attn, db_in, dsinks, dg_out_a, dg_out_b, dg_mlp, dg_final]
    summed = _sum_over_devices(_pack_small(small_g + [loss_dev[:, :1]]))
    shapes = [w.shape for w in small_w]
    *g_small, loss = _unpack_small(summed, shapes + [()])

    big = [
        _adamw(w_in, g_w_in, m_w_in, v_w_in, "adamw_w_in"),
        step_w_out,
        _adamw(w_1, g_w_1, m_w_1, v_w_1, "adamw_w_1"),
        step_w_2,
    ]
    g_packed = _pack_small(g_small)
    small = _adamw(_pack_small(small_w)[None], g_packed, _pack_small(small_m)[None], _pack_small(small_v)[None],
                   "adamw_small")
    small = [_unpack_small(s, shapes) for s in small]

    def ordered(small_list, big_list):
        s = list(small_list)
        return [s[0], big_list[0], s[1], s[2], s[3], s[4], big_list[1], s[5], big_list[2], big_list[3], s[6]]

    grads = ordered(g_small, [g_w_in[None], g_w_out, g_w_1[None], g_w_2])
    deltas = ordered(small[0], [b[0] for b in big])
    new_m = ordered(small[1], [b[1] for b in big])
    new_v = ordered(small[2], [b[2] for b in big])
    return (loss, dx[None], *grads, *deltas, *new_m, *new_v)
```
